```python
import jax, jax.numpy as jnp
from jax import lax
import numpy as np

D_MODEL = 1024
BATCH = 8
SEQ = 16384
DEPTH = 2

GRID_W = 64
CTX_LEN = 256
NA_HEADS = 8
HEAD_DIM = 64
NA_WIDTH = NA_HEADS * HEAD_DIM
WIN_H = 8
WIN_W = 16
SG_GROUPS = 4
SG_CHUNK = 128
SG_WIDTH = 512
D_FF = 2816
ROPE_THETA = 10000.0
EPS = 1e-6
N_MOD = 9
Q0, K0, V0 = 0, NA_WIDTH, 2 * NA_WIDTH
U0 = 3 * NA_WIDTH
VS0 = U0 + SG_WIDTH
G0 = VS0 + SG_WIDTH
IN_COLS = G0 + 2 * D_MODEL

kernel_name = "hybrid_na_gmlp_macaron_dit"


def rmsnorm(t, g):
    tf = t.astype(jnp.float32)
    y = tf * lax.rsqrt(jnp.mean(tf * tf, axis=-1, keepdims=True) + EPS)
    return (y * g.astype(jnp.float32)).astype(t.dtype)


def layernorm(t, g, b):
    tf = t.astype(jnp.float32)
    mu = jnp.mean(tf, axis=-1, keepdims=True)
    var = jnp.mean(jnp.square(tf - mu), axis=-1, keepdims=True)
    y = (tf - mu) * lax.rsqrt(var + EPS)
    return (y * g.astype(jnp.float32) + b.astype(jnp.float32)).astype(t.dtype)


def modulate(t, g, shift, scale):
    return rmsnorm(t, g) * (1.0 + scale) + shift


def ada_mod(cond, w, b):
    m = jax.nn.silu(cond) @ w + b
    return jnp.moveaxis(m.reshape(cond.shape[0], N_MOD, D_MODEL), 1, 0)[:, :, None, :]


def swiglu(h, w_up, w_down):
    a, b = jnp.split(h @ w_up, 2, axis=-1)
    return (jax.nn.silu(a) * b) @ w_down


def ffn_sublayer(t, mod, i, g, w_up, w_down):
    h = modulate(t, g, mod[3 * i], mod[3 * i + 1])
    return t + 0.5 * mod[3 * i + 2] * swiglu(h, w_up, w_down)


def heads(t):
    return t.reshape(*t.shape[:-1], NA_HEADS, HEAD_DIM)


def axial_rope(t, rows, cols):
    n_freq = HEAD_DIM // 4
    freqs = ROPE_THETA ** (-jnp.arange(n_freq, dtype=jnp.float32) / n_freq)
    ang = jnp.concatenate([rows[:, None] * freqs, cols[:, None] * freqs], axis=-1)
    cos = jnp.cos(ang)[None, :, None, :]
    sin = jnp.sin(ang)[None, :, None, :]
    tf = t.astype(jnp.float32).reshape(*t.shape[:-1], HEAD_DIM // 2, 2)
    e, o = tf[..., 0], tf[..., 1]
    out = jnp.stack([e * cos - o * sin, e * sin + o * cos], axis=-1).reshape(t.shape)
    return out.astype(t.dtype)


def neighbourhood_attention(q, k, v, k_ctx, v_ctx, rpb):
    B, N, H, hd = q.shape
    rows = N // GRID_W
    win_h = min(WIN_H, rows)
    n_nb = win_h * WIN_W
    scale = hd ** -0.5
    qg = q.reshape(B, rows, GRID_W, H, hd)
    kg = k.reshape(B, rows, GRID_W, H, hd)
    vg = v.reshape(B, rows, GRID_W, H, hd)
    cols = jnp.arange(GRID_W)
    col_start = jnp.clip(cols - WIN_W // 2, 0, GRID_W - WIN_W)
    col_idx = col_start[:, None] + jnp.arange(WIN_W)[None, :]
    dc = col_idx - cols[:, None] + (WIN_W - 1)
    rpb_c = rpb[:, :, dc]

    def row_block(r):
        rs = jnp.clip(r - WIN_H // 2, 0, rows - win_h)
        q_r = lax.dynamic_index_in_dim(qg, r, axis=1, keepdims=False)
        k_rows = lax.dynamic_slice_in_dim(kg, rs, win_h, axis=1)
        v_rows = lax.dynamic_slice_in_dim(vg, rs, win_h, axis=1)
        k_nb = jnp.moveaxis(k_rows[:, :, col_idx], 2, 1).reshape(B, GRID_W, n_nb, H, hd)
        v_nb = jnp.moveaxis(v_rows[:, :, col_idx], 2, 1).reshape(B, GRID_W, n_nb, H, hd)
        dr = rs + jnp.arange(win_h) - r + (WIN_H - 1)
        bias = jnp.moveaxis(jnp.take(rpb_c, dr, axis=1), 2, 1).reshape(H, GRID_W, n_nb)
        s_nb = jnp.einsum('bqhd,bqkhd->bhqk', q_r, k_nb).astype(jnp.float32) * scale \
            + bias.astype(jnp.float32)
        s_ctx = jnp.einsum('bqhd,bkhd->bhqk', q_r, k_ctx).astype(jnp.float32) * scale
        p = jax.nn.softmax(jnp.concatenate([s_nb, s_ctx], axis=-1), axis=-1).astype(v.dtype)
        return (jnp.einsum('bhqk,bqkhd->bqhd', p[..., :n_nb], v_nb)
                + jnp.einsum('bhqk,bkhd->bqhd', p[..., n_nb:], v_ctx))

    out = lax.map(row_block, jnp.arange(rows))
    return jnp.moveaxis(out, 0, 1).reshape(B, N, H * hd)


def context_attention(q, k, v):
    B, L, H, hd = q.shape
    s = jnp.einsum('bqhd,bkhd->bhqk', q, k).astype(jnp.float32) * (hd ** -0.5)
    p = jax.nn.softmax(s, axis=-1).astype(v.dtype)
    return jnp.einsum('bhqk,bkhd->bqhd', p, v).reshape(B, L, H * hd)


def chunk_spatial_gating(u, v, ln_g, ln_b, w_s, b_s):
    B, N, _ = v.shape
    vn = layernorm(v, ln_g, ln_b).reshape(B, N // SG_CHUNK, SG_CHUNK, SG_GROUPS, SG_WIDTH // SG_GROUPS)
    s = jnp.einsum('gpq,bnqgc->bnpgc', w_s, vn) + b_s.T[None, None, :, :, None]
    return u * s.reshape(B, N, SG_WIDTH)


def merge_branches(o_a, o_b, g_logits, b_gate, w_pa, w_pb, w_o):
    g_a, g_b = jnp.split(jax.nn.sigmoid(g_logits + b_gate), 2, axis=-1)
    return (g_a * (o_a @ w_pa) + g_b * (o_b @ w_pb)) @ w_o


def _fwd_setup_inputs(seed: int = 0) -> dict:
    key = jax.random.key(seed)
    ks = jax.random.split(key, 24)
    f32 = jnp.float32
    nrm = lambda k, shape, s: jax.random.normal(k, shape, f32) * s
    D, L = D_MODEL, DEPTH
    return {
        "x": nrm(ks[0], (BATCH, SEQ, D), 1.0),
        "c": nrm(ks[1], (BATCH, D), 1.0),
        "ctx": nrm(ks[2], (BATCH, CTX_LEN, D), 1.0),
        "c_ctx": nrm(ks[3], (D,), 1.0),
        "w_ada": nrm(ks[4], (L, D, N_MOD * D), 0.5 * D ** -0.5),
        "b_ada": nrm(ks[5], (L, N_MOD * D), 0.02),
        "norm_g": 1.0 + nrm(ks[6], (L, 3, D), 0.02),
        "w_ff1_up": nrm(ks[7], (L, D, 2 * D_FF), D ** -0.5),
        "w_ff1_down": nrm(ks[8], (L, D_FF, D), D_FF ** -0.5),
        "w_in": nrm(ks[9], (L, D, IN_COLS), D ** -0.5),
        "b_gate": nrm(ks[10], (L, 2 * D), 0.02),
        "rpb": nrm(ks[11], (L, NA_HEADS, 2 * WIN_H - 1, 2 * WIN_W - 1), 0.1),
        "ln_v_g": 1.0 + nrm(ks[12], (L, SG_WIDTH), 0.02),
        "ln_v_b": nrm(ks[13], (L, SG_WIDTH), 0.02),
        "w_s": nrm(ks[14], (L, SG_GROUPS, SG_CHUNK, SG_CHUNK), 0.5 * SG_CHUNK ** -0.5),
        "b_s": 1.0 + nrm(ks[15], (L, SG_GROUPS, SG_CHUNK), 0.02),
        "w_pa": nrm(ks[16], (L, NA_WIDTH, D), NA_WIDTH ** -0.5),
        "w_pb": nrm(ks[17], (L, SG_WIDTH, D), SG_WIDTH ** -0.5),
        "w_o": nrm(ks[18], (L, D, D), D ** -0.5),
        "w_ff2_up": nrm(ks[19], (L, D, 2 * D_FF), D ** -0.5),
        "w_ff2_down": nrm(ks[20], (L, D_FF, D), D_FF ** -0.5),
        "final_g": 1.0 + nrm(ks[21], (D,), 0.02),
    }


def _fwd_reference(x, c, ctx, c_ctx, w_ada, b_ada, norm_g, w_ff1_up, w_ff1_down, w_in, b_gate,
              rpb, ln_v_g, ln_v_b, w_s, b_s, w_pa, w_pb, w_o, w_ff2_up, w_ff2_down, final_g):
    N = x.shape[1]
    t = jnp.arange(N)
    pos_r = (t // GRID_W).astype(jnp.float32)
    pos_c = (t % GRID_W).astype(jnp.float32)
    gelu = jax.nn.gelu

    for l in range(DEPTH):
        last = l == DEPTH - 1
        mx = ada_mod(c, w_ada[l], b_ada[l])
        mc = ada_mod(c_ctx[None, :], w_ada[l], b_ada[l])

        x = ffn_sublayer(x, mx, 0, norm_g[l, 0], w_ff1_up[l], w_ff1_down[l])
        ctx = ffn_sublayer(ctx, mc, 0, norm_g[l, 0], w_ff1_up[l], w_ff1_down[l])

        hx = modulate(x, norm_g[l, 1], mx[3], mx[4])
        hc = modulate(ctx, norm_g[l, 1], mc[3], mc[4])
        px = hx @ w_in[l]
        if last:
            pc = hc @ w_in[l][:, K0:U0]
            kc, vc = heads(pc[..., :NA_WIDTH]), heads(pc[..., NA_WIDTH:])
        else:
            pc = hc @ w_in[l]
            kc, vc = heads(pc[..., K0:V0]), heads(pc[..., V0:U0])

        qx = axial_rope(heads(px[..., Q0:K0]), pos_r, pos_c)
        kx = axial_rope(heads(px[..., K0:V0]), pos_r, pos_c)
        o_a = neighbourhood_attention(qx, kx, heads(px[..., V0:U0]), kc, vc, rpb[l])
        o_b = chunk_spatial_gating(gelu(px[..., U0:VS0]), gelu(px[..., VS0:G0]),
                                   ln_v_g[l], ln_v_b[l], w_s[l], b_s[l])
        x = x + mx[5] * merge_branches(o_a, o_b, px[..., G0:], b_gate[l], w_pa[l], w_pb[l], w_o[l])

        if not last:
            o_ac = context_attention(heads(pc[..., Q0:K0]), kc, vc)
            o_bc = chunk_spatial_gating(gelu(pc[..., U0:VS0]), gelu(pc[..., VS0:G0]),
                                        ln_v_g[l], ln_v_b[l], w_s[l], b_s[l])
            ctx = ctx + mc[5] * merge_branches(o_ac, o_bc, pc[..., G0:], b_gate[l],
                                               w_pa[l], w_pb[l], w_o[l])
            ctx = ffn_sublayer(ctx, mc, 2, norm_g[l, 2], w_ff2_up[l], w_ff2_down[l])

        x = ffn_sublayer(x, mx, 2, norm_g[l, 2], w_ff2_up[l], w_ff2_down[l])

    return rmsnorm(x, final_g)


import jax as _jax
import jax.numpy as _jnp

TWIN_FORMAT = 'train_step'
FWD_PARAMS = ['x', 'c', 'ctx', 'c_ctx', 'w_ada', 'b_ada', 'norm_g', 'w_ff1_up', 'w_ff1_down', 'w_in', 'b_gate', 'rpb', 'ln_v_g', 'ln_v_b', 'w_s', 'b_s', 'w_pa', 'w_pb', 'w_o', 'w_ff2_up', 'w_ff2_down', 'final_g']
TWIN_WEIGHTS = ['c_ctx', 'w_ada', 'b_ada', 'norm_g', 'w_ff1_up', 'w_ff1_down', 'w_in', 'b_gate', 'rpb', 'ln_v_g', 'ln_v_b', 'w_s', 'b_s', 'w_pa', 'w_pb', 'w_o', 'w_ff2_up', 'w_ff2_down', 'final_g']
TWIN_DIFF_INPUT = 'x'
TWIN_INPUTS = ['x', 'c', 'ctx', 'c_ctx', 'w_ada', 'b_ada', 'norm_g', 'w_ff1_up', 'w_ff1_down', 'w_in', 'b_gate', 'rpb', 'ln_v_g', 'ln_v_b', 'w_s', 'b_s', 'w_pa', 'w_pb', 'w_o', 'w_ff2_up', 'w_ff2_down', 'final_g', 'loss_target', 'm_c_ctx', 'm_w_ada', 'm_b_ada', 'm_norm_g', 'm_w_ff1_up', 'm_w_ff1_down', 'm_w_in', 'm_b_gate', 'm_rpb', 'm_ln_v_g', 'm_ln_v_b', 'm_w_s', 'm_b_s', 'm_w_pa', 'm_w_pb', 'm_w_o', 'm_w_ff2_up', 'm_w_ff2_down', 'm_final_g', 'v_c_ctx', 'v_w_ada', 'v_b_ada', 'v_norm_g', 'v_w_ff1_up', 'v_w_ff1_down', 'v_w_in', 'v_b_gate', 'v_rpb', 'v_ln_v_g', 'v_ln_v_b', 'v_w_s', 'v_b_s', 'v_w_pa', 'v_w_pb', 'v_w_o', 'v_w_ff2_up', 'v_w_ff2_down', 'v_final_g']
TWIN_OUTPUTS = ['loss', 'grad_x', 'grad_c_ctx', 'grad_w_ada', 'grad_b_ada', 'grad_norm_g', 'grad_w_ff1_up', 'grad_w_ff1_down', 'grad_w_in', 'grad_b_gate', 'grad_rpb', 'grad_ln_v_g', 'grad_ln_v_b', 'grad_w_s', 'grad_b_s', 'grad_w_pa', 'grad_w_pb', 'grad_w_o', 'grad_w_ff2_up', 'grad_w_ff2_down', 'grad_final_g', 'delta_c_ctx', 'delta_w_ada', 'delta_b_ada', 'delta_norm_g', 'delta_w_ff1_up', 'delta_w_ff1_down', 'delta_w_in', 'delta_b_gate', 'delta_rpb', 'delta_ln_v_g', 'delta_ln_v_b', 'delta_w_s', 'delta_b_s', 'delta_w_pa', 'delta_w_pb', 'delta_w_o', 'delta_w_ff2_up', 'delta_w_ff2_down', 'delta_final_g', 'new_m_c_ctx', 'new_m_w_ada', 'new_m_b_ada', 'new_m_norm_g', 'new_m_w_ff1_up', 'new_m_w_ff1_down', 'new_m_w_in', 'new_m_b_gate', 'new_m_rpb', 'new_m_ln_v_g', 'new_m_ln_v_b', 'new_m_w_s', 'new_m_b_s', 'new_m_w_pa', 'new_m_w_pb', 'new_m_w_o', 'new_m_w_ff2_up', 'new_m_w_ff2_down', 'new_m_final_g', 'new_v_c_ctx', 'new_v_w_ada', 'new_v_b_ada', 'new_v_norm_g', 'new_v_w_ff1_up', 'new_v_w_ff1_down', 'new_v_w_in', 'new_v_b_gate', 'new_v_rpb', 'new_v_ln_v_g', 'new_v_ln_v_b', 'new_v_w_s', 'new_v_b_s', 'new_v_w_pa', 'new_v_w_pb', 'new_v_w_o', 'new_v_w_ff2_up', 'new_v_w_ff2_down', 'new_v_final_g']
TWIN_LEAF_KINDS = {'loss': 'loss', 'grad_x': 'grad_x', 'grad_c_ctx': 'grad_w', 'grad_w_ada': 'grad_w', 'grad_b_ada': 'grad_w', 'grad_norm_g': 'grad_w', 'grad_w_ff1_up': 'grad_w', 'grad_w_ff1_down': 'grad_w', 'grad_w_in': 'grad_w', 'grad_b_gate': 'grad_w', 'grad_rpb': 'grad_w', 'grad_ln_v_g': 'grad_w', 'grad_ln_v_b': 'grad_w', 'grad_w_s': 'grad_w', 'grad_b_s': 'grad_w', 'grad_w_pa': 'grad_w', 'grad_w_pb': 'grad_w', 'grad_w_o': 'grad_w', 'grad_w_ff2_up': 'grad_w', 'grad_w_ff2_down': 'grad_w', 'grad_final_g': 'grad_w', 'delta_c_ctx': 'delta_w', 'delta_w_ada': 'delta_w', 'delta_b_ada': 'delta_w', 'delta_norm_g': 'delta_w', 'delta_w_ff1_up': 'delta_w', 'delta_w_ff1_down': 'delta_w', 'delta_w_in': 'delta_w', 'delta_b_gate': 'delta_w', 'delta_rpb': 'delta_w', 'delta_ln_v_g': 'delta_w', 'delta_ln_v_b': 'delta_w', 'delta_w_s': 'delta_w', 'delta_b_s': 'delta_w', 'delta_w_pa': 'delta_w', 'delta_w_pb': 'delta_w', 'delta_w_o': 'delta_w', 'delta_w_ff2_up': 'delta_w', 'delta_w_ff2_down': 'delta_w', 'delta_final_g': 'delta_w', 'new_m_c_ctx': 'new_m', 'new_m_w_ada': 'new_m', 'new_m_b_ada': 'new_m', 'new_m_norm_g': 'new_m', 'new_m_w_ff1_up': 'new_m', 'new_m_w_ff1_down': 'new_m', 'new_m_w_in': 'new_m', 'new_m_b_gate': 'new_m', 'new_m_rpb': 'new_m', 'new_m_ln_v_g': 'new_m', 'new_m_ln_v_b': 'new_m', 'new_m_w_s': 'new_m', 'new_m_b_s': 'new_m', 'new_m_w_pa': 'new_m', 'new_m_w_pb': 'new_m', 'new_m_w_o': 'new_m', 'new_m_w_ff2_up': 'new_m', 'new_m_w_ff2_down': 'new_m', 'new_m_final_g': 'new_m', 'new_v_c_ctx': 'new_v', 'new_v_w_ada': 'new_v', 'new_v_b_ada': 'new_v', 'new_v_norm_g': 'new_v', 'new_v_w_ff1_up': 'new_v', 'new_v_w_ff1_down': 'new_v', 'new_v_w_in': 'new_v', 'new_v_b_gate': 'new_v', 'new_v_rpb': 'new_v', 'new_v_ln_v_g': 'new_v', 'new_v_ln_v_b': 'new_v', 'new_v_w_s': 'new_v', 'new_v_b_s': 'new_v', 'new_v_w_pa': 'new_v', 'new_v_w_pb': 'new_v', 'new_v_w_o': 'new_v', 'new_v_w_ff2_up': 'new_v', 'new_v_w_ff2_down': 'new_v', 'new_v_final_g': 'new_v'}


def _forward(args):
    return _fwd_reference(*[args[k] for k in FWD_PARAMS])


def _output_shape():
    def fwd():
        inp = _fwd_setup_inputs(0)
        return _fwd_reference(*[inp[k] for k in FWD_PARAMS])
    out = _jax.eval_shape(fwd)
    return out.shape, out.dtype

N_MICROBATCH = 1
ADAM_LR = 0.001
ADAM_B1 = 0.9
ADAM_B2 = 0.999
ADAM_EPS = 1e-08
ADAM_WD = 0.01
ADAM_STEP = 10
PER_EXAMPLE_BATCH_AXIS = {'x': 0, 'c': 0, 'ctx': 0, 'loss_target': 0}
SHARED_INPUTS = []
_WEIGHT_DTYPES = {'c_ctx': _jnp.float32, 'w_ada': _jnp.float32, 'b_ada': _jnp.float32, 'norm_g': _jnp.float32, 'w_ff1_up': _jnp.float32, 'w_ff1_down': _jnp.float32, 'w_in': _jnp.float32, 'b_gate': _jnp.float32, 'rpb': _jnp.float32, 'ln_v_g': _jnp.float32, 'ln_v_b': _jnp.float32, 'w_s': _jnp.float32, 'b_s': _jnp.float32, 'w_pa': _jnp.float32, 'w_pb': _jnp.float32, 'w_o': _jnp.float32, 'w_ff2_up': _jnp.float32, 'w_ff2_down': _jnp.float32, 'final_g': _jnp.float32}
MOMENT_SCALE = {'c_ctx': 2.072724e-02, 'w_ada': 5.603736e-02, 'b_ada': 9.479949e-02, 'norm_g': 5.376163e-02, 'w_ff1_up': 2.367676e-02, 'w_ff1_down': 3.860073e-02, 'w_in': 2.825376e-02, 'b_gate': 1.400499e-02, 'rpb': 2.344676e-03, 'ln_v_g': 2.876001e-02, 'ln_v_b': 3.271956e-02, 'w_s': 5.835509e-02, 'b_s': 5.819802e-02, 'w_pa': 1.627541e-02, 'w_pb': 4.636732e-02, 'w_o': 4.943751e-02, 'w_ff2_up': 2.288381e-02, 'w_ff2_down': 3.733056e-02, 'final_g': 1.280731e+02}


def _to_microbatches(a, axis):
    t = _jnp.moveaxis(a, axis, 0)
    t = t.reshape((N_MICROBATCH, t.shape[0] // N_MICROBATCH) + t.shape[1:])
    return _jnp.moveaxis(t, 1, axis + 1)


def setup_inputs(seed: int = 0) -> dict:
    inp = _fwd_setup_inputs(seed)
    key = _jax.random.fold_in(_jax.random.key(seed), 7919)
    shape, _ = _output_shape()
    out = dict(inp)
    out["loss_target"] = _jax.random.normal(_jax.random.fold_in(key, 0), shape, _jnp.float32)
    for i, name in enumerate(TWIN_WEIGHTS):
        w = inp[name].astype(_jnp.float32)
        if MOMENT_SCALE is None:
            s = _jnp.sqrt(_jnp.mean(_jnp.square(w)) + 1e-30)
        else:
            s = MOMENT_SCALE[name]
        km, kv = _jax.random.split(_jax.random.fold_in(key, i + 1))
        out[name] = w
        out["m_" + name] = s * _jax.random.normal(km, w.shape, _jnp.float32)
        out["v_" + name] = (s * s) * _jax.random.uniform(kv, w.shape, _jnp.float32, 0.5, 1.5)
    if N_MICROBATCH > 1:
        for name, axis in PER_EXAMPLE_BATCH_AXIS.items():
            out[name] = _to_microbatches(out[name], axis)
    return {'x': out['x'], 'c': out['c'], 'ctx': out['ctx'], 'c_ctx': out['c_ctx'], 'w_ada': out['w_ada'], 'b_ada': out['b_ada'], 'norm_g': out['norm_g'], 'w_ff1_up': out['w_ff1_up'], 'w_ff1_down': out['w_ff1_down'], 'w_in': out['w_in'], 'b_gate': out['b_gate'], 'rpb': out['rpb'], 'ln_v_g': out['ln_v_g'], 'ln_v_b': out['ln_v_b'], 'w_s': out['w_s'], 'b_s': out['b_s'], 'w_pa': out['w_pa'], 'w_pb': out['w_pb'], 'w_o': out['w_o'], 'w_ff2_up': out['w_ff2_up'], 'w_ff2_down': out['w_ff2_down'], 'final_g': out['final_g'], 'loss_target': out['loss_target'], 'm_c_ctx': out['m_c_ctx'], 'm_w_ada': out['m_w_ada'], 'm_b_ada': out['m_b_ada'], 'm_norm_g': out['m_norm_g'], 'm_w_ff1_up': out['m_w_ff1_up'], 'm_w_ff1_down': out['m_w_ff1_down'], 'm_w_in': out['m_w_in'], 'm_b_gate': out['m_b_gate'], 'm_rpb': out['m_rpb'], 'm_ln_v_g': out['m_ln_v_g'], 'm_ln_v_b': out['m_ln_v_b'], 'm_w_s': out['m_w_s'], 'm_b_s': out['m_b_s'], 'm_w_pa': out['m_w_pa'], 'm_w_pb': out['m_w_pb'], 'm_w_o': out['m_w_o'], 'm_w_ff2_up': out['m_w_ff2_up'], 'm_w_ff2_down': out['m_w_ff2_down'], 'm_final_g': out['m_final_g'], 'v_c_ctx': out['v_c_ctx'], 'v_w_ada': out['v_w_ada'], 'v_b_ada': out['v_b_ada'], 'v_norm_g': out['v_norm_g'], 'v_w_ff1_up': out['v_w_ff1_up'], 'v_w_ff1_down': out['v_w_ff1_down'], 'v_w_in': out['v_w_in'], 'v_b_gate': out['v_b_gate'], 'v_rpb': out['v_rpb'], 'v_ln_v_g': out['v_ln_v_g'], 'v_ln_v_b': out['v_ln_v_b'], 'v_w_s': out['v_w_s'], 'v_b_s': out['v_b_s'], 'v_w_pa': out['v_w_pa'], 'v_w_pb': out['v_w_pb'], 'v_w_o': out['v_w_o'], 'v_w_ff2_up': out['v_w_ff2_up'], 'v_w_ff2_down': out['v_w_ff2_down'], 'v_final_g': out['v_final_g']}


def _loss(weights, diff, rest, loss_target):
    with _jax.named_scope("forward"):
        args = {**rest, TWIN_DIFF_INPUT: diff, **{k: w.astype(_WEIGHT_DTYPES[k]) for k, w in weights.items()}}
        y = _forward(args)
    with _jax.named_scope("loss_head"):
        err = _jnp.square(y.astype(_jnp.float32) - loss_target)
        return 0.5 * _jnp.sum(_jnp.mean(err, axis=-1)) if err.ndim else 0.5 * err


def _adamw(w, g, m, v):
    m = ADAM_B1 * m + (1.0 - ADAM_B1) * g
    v = ADAM_B2 * v + (1.0 - ADAM_B2) * _jnp.square(g)
    m_hat = m / (1.0 - ADAM_B1 ** ADAM_STEP)
    v_hat = v / (1.0 - ADAM_B2 ** ADAM_STEP)
    delta = -ADAM_LR * (m_hat / (_jnp.sqrt(v_hat) + ADAM_EPS) + ADAM_WD * w)
    return delta, m, v


def reference(x, c, ctx, c_ctx, w_ada, b_ada, norm_g, w_ff1_up, w_ff1_down, w_in, b_gate, rpb, ln_v_g, ln_v_b, w_s, b_s, w_pa, w_pb, w_o, w_ff2_up, w_ff2_down, final_g, loss_target, m_c_ctx, m_w_ada, m_b_ada, m_norm_g, m_w_ff1_up, m_w_ff1_down, m_w_in, m_b_gate, m_rpb, m_ln_v_g, m_ln_v_b, m_w_s, m_b_s, m_w_pa, m_w_pb, m_w_o, m_w_ff2_up, m_w_ff2_down, m_final_g, v_c_ctx, v_w_ada, v_b_ada, v_norm_g, v_w_ff1_up, v_w_ff1_down, v_w_in, v_b_gate, v_rpb, v_ln_v_g, v_ln_v_b, v_w_s, v_b_s, v_w_pa, v_w_pb, v_w_o, v_w_ff2_up, v_w_ff2_down, v_final_g):
    given = dict(x=x, c=c, ctx=ctx, c_ctx=c_ctx, w_ada=w_ada, b_ada=b_ada, norm_g=norm_g, w_ff1_up=w_ff1_up, w_ff1_down=w_ff1_down, w_in=w_in, b_gate=b_gate, rpb=rpb, ln_v_g=ln_v_g, ln_v_b=ln_v_b, w_s=w_s, b_s=b_s, w_pa=w_pa, w_pb=w_pb, w_o=w_o, w_ff2_up=w_ff2_up, w_ff2_down=w_ff2_down, final_g=final_g, loss_target=loss_target, m_c_ctx=m_c_ctx, m_w_ada=m_w_ada, m_b_ada=m_b_ada, m_norm_g=m_norm_g, m_w_ff1_up=m_w_ff1_up, m_w_ff1_down=m_w_ff1_down, m_w_in=m_w_in, m_b_gate=m_b_gate, m_rpb=m_rpb, m_ln_v_g=m_ln_v_g, m_ln_v_b=m_ln_v_b, m_w_s=m_w_s, m_b_s=m_b_s, m_w_pa=m_w_pa, m_w_pb=m_w_pb, m_w_o=m_w_o, m_w_ff2_up=m_w_ff2_up, m_w_ff2_down=m_w_ff2_down, m_final_g=m_final_g, v_c_ctx=v_c_ctx, v_w_ada=v_w_ada, v_b_ada=v_b_ada, v_norm_g=v_norm_g, v_w_ff1_up=v_w_ff1_up, v_w_ff1_down=v_w_ff1_down, v_w_in=v_w_in, v_b_gate=v_b_gate, v_rpb=v_rpb, v_ln_v_g=v_ln_v_g, v_ln_v_b=v_ln_v_b, v_w_s=v_w_s, v_b_s=v_b_s, v_w_pa=v_w_pa, v_w_pb=v_w_pb, v_w_o=v_w_o, v_w_ff2_up=v_w_ff2_up, v_w_ff2_down=v_w_ff2_down, v_final_g=v_final_g)
    weights = {n: given[n] for n in TWIN_WEIGHTS}
    shared = {n: given[n] for n in SHARED_INPUTS}
    per_example = {n: given[n] for n in ['x', 'c', 'ctx']}
    grad_fn = _jax.value_and_grad(_loss, argnums=(0, 1))

    def one_microbatch(ex, loss_target):
        ex = dict(ex)
        diff = ex.pop(TWIN_DIFF_INPUT)
        return grad_fn(weights, diff, {**shared, **ex}, loss_target)

    if N_MICROBATCH == 1:
        loss, (grad_w, grad_x) = one_microbatch(per_example, given["loss_target"])
    else:
        def body(carry, xs):
            loss_sum, grad_sum = carry
            l_k, (gw_k, gx_k) = one_microbatch(xs[0], xs[1])
            with _jax.named_scope("update"):
                return (loss_sum + l_k, _jax.tree.map(_jnp.add, grad_sum, gw_k)), gx_k

        init = (_jnp.zeros((), _jnp.float32), _jax.tree.map(_jnp.zeros_like, weights))
        (loss, grad_w), grad_x = _jax.lax.scan(body, init, (per_example, given["loss_target"]))
    with _jax.named_scope("update"):
        delta_w, new_m, new_v = {}, {}, {}
        for n in TWIN_WEIGHTS:
            delta_w[n], new_m[n], new_v[n] = _adamw(weights[n], grad_w[n], given["m_" + n], given["v_" + n])
    return (loss, grad_x, *[grad_w[n] for n in TWIN_WEIGHTS], *[delta_w[n] for n in TWIN_WEIGHTS],
            *[new_m[n] for n in TWIN_WEIGHTS], *[new_v[n] for n in TWIN_WEIGHTS])
```

```python
import numpy as np
import jax
import jax.numpy as jnp
from jax import lax
from jax.experimental import pallas as pl
from jax.experimental.pallas import tpu as pltpu

F32 = jnp.float32
MXU = jnp.bfloat16
EPS = 1e-6
GRID_W, HEADS, HEAD_DIM = 64, 8, 64
NA_WIDTH = SG_WIDTH = 512
WIN_H, WIN_W = 8, 16
SG_CHUNK, SG_GROUPS = 128, 4
N_MOD = 9
ROPE_THETA = 10000.0
Q_ROWS, K_ROWS = 4, 12
TQ, TK = Q_ROWS * GRID_W, K_ROWS * GRID_W
TM = 256
LANES = 128
NEG = -1e30
VMEM_LIMIT = 56 * 2 ** 20
ADAM_LR, ADAM_B1, ADAM_B2, ADAM_EPS, ADAM_WD, ADAM_STEP = 0.001, 0.9, 0.999, 1e-08, 0.01, 10
MESH = pl.DeviceIdType.MESH
BIG = ("w_ff1_up", "w_ff1_down", "w_in", "w_pa", "w_pb", "w_o", "w_ff2_up", "w_ff2_down")
SHARD_AXIS = {"w_ff1_up": 2, "w_ff1_down": 1, "w_in": 2, "w_pa": 2, "w_pb": 2, "w_o": 1, "w_ff2_up": 2, "w_ff2_down": 1}


def _call(body, *, name, grid, in_specs, out_specs, out_shape, scratch=()):
    return pl.pallas_call(
        body, name=name, grid=grid, in_specs=in_specs, out_specs=out_specs, out_shape=out_shape,
        scratch_shapes=list(scratch),
        compiler_params=pltpu.CompilerParams(dimension_semantics=("arbitrary",) * len(grid), vmem_limit_bytes=VMEM_LIMIT))


def _pick(n, prefs):
    for p in prefs:
        if n % p == 0:
            return p
    return n


def _rows(tm, n, col=0):
    return pl.BlockSpec((tm, n), lambda i: (i, col))


def _fixed(shape):
    return pl.BlockSpec(shape, lambda *_: (0,) * len(shape))


def _stream(d, nx):
    return pl.BlockSpec((1, 1, d), lambda i: (i // nx, 0, 0))


def _sds(shape, dtype):
    return jax.ShapeDtypeStruct(shape, dtype)


def _mm(a, b, mode, out_dtype, name):
    if mode == "tn":
        r, m = a.shape
        n = b.shape[1]
        tm = _pick(m, (1024, 1408, 704, 512, 256, 128))
        tn = _pick(n, (512, 256, 128))
        tr = _pick(r, (640, 512, 256, 128))

        def body(a_ref, b_ref, o_ref):
            @pl.when(pl.program_id(2) == 0)
            def _():
                o_ref[...] = jnp.zeros_like(o_ref)

            o_ref[...] += lax.dot_general(a_ref[...].astype(MXU), b_ref[...].astype(MXU), (((0,), (0,)), ((), ())),
                                          preferred_element_type=F32)

        return _call(body, name=name, grid=(m // tm, n // tn, r // tr),
                     in_specs=[pl.BlockSpec((tr, tm), lambda i, j, k: (k, i)), pl.BlockSpec((tr, tn), lambda i, j, k: (k, j))],
                     out_specs=pl.BlockSpec((tm, tn), lambda i, j, k: (i, j)), out_shape=_sds((m, n), F32))(a, b)
    m, k = a.shape
    n = b.shape[1] if mode == "nn" else b.shape[0]
    tm = _pick(m, (640, 512, 256, 128))
    tn = _pick(n, (512, 256, 128))
    dims = (((1,), (0,)), ((), ())) if mode == "nn" else (((1,), (1,)), ((), ()))

    def body(a_ref, b_ref, o_ref):
        o_ref[...] = lax.dot_general(a_ref[...].astype(MXU), b_ref[...].astype(MXU), dims,
                                     preferred_element_type=F32).astype(o_ref.dtype)

    b_spec = pl.BlockSpec((k, tn), lambda i, j: (0, j)) if mode == "nn" else pl.BlockSpec((tn, k), lambda i, j: (j, 0))
    return _call(body, name=name, grid=(m // tm, n // tn), in_specs=[pl.BlockSpec((tm, k), lambda i, j: (i, 0)), b_spec],
                 out_specs=pl.BlockSpec((tm, tn), lambda i, j: (i, j)), out_shape=_sds((m, n), out_dtype))(a, b)


def _normmod(xs, g, mods, k_shift, k_scale, nx, name):
    t, d = xs.shape

    def body(x_ref, g_ref, m_ref, h_ref):
        x = x_ref[...]
        rstd = lax.rsqrt(jnp.mean(x * x, axis=-1, keepdims=True) + EPS)
        scale = m_ref[0, k_scale:k_scale + 1, :]
        shift = m_ref[0, k_shift:k_shift + 1, :]
        h_ref[...] = (x * rstd * g_ref[...] * (1.0 + scale) + shift).astype(h_ref.dtype)

    return _call(body, name=name, grid=(t // TM,),
                 in_specs=[_rows(TM, d), _fixed((1, d)), pl.BlockSpec((1, N_MOD, d), lambda i: (i // nx, 0, 0))],
                 out_specs=_rows(TM, d), out_shape=_sds((t, d), MXU))(xs, g, mods)


def _normmod_bwd(xs, dh, dres, g, mods, k_scale, nx, name):
    t, d = xs.shape

    def body(x_ref, dh_ref, dr_ref, g_ref, m_ref, dx_ref, dsh_ref, dsc_ref, dg_ref):
        i = pl.program_id(0)
        x = x_ref[...]
        dh = dh_ref[...].astype(F32)
        rstd = lax.rsqrt(jnp.mean(x * x, axis=-1, keepdims=True) + EPS)
        xhat = x * rstd
        gg = g_ref[...]

        @pl.when((i == 0) | (i == nx))
        def _():
            dsh_ref[...] = jnp.zeros_like(dsh_ref)
            dsc_ref[...] = jnp.zeros_like(dsc_ref)

        @pl.when(i == 0)
        def _():
            dg_ref[...] = jnp.zeros_like(dg_ref)

        dsh_ref[0] += jnp.sum(dh, axis=0, keepdims=True)
        dsc_ref[0] += jnp.sum(dh * (xhat * gg), axis=0, keepdims=True)
        dy = dh * (1.0 + m_ref[0, k_scale:k_scale + 1, :])
        dg_ref[...] += jnp.sum(dy * xhat, axis=0, keepdims=True)
        dxh = dy * gg
        dx_ref[...] = dr_ref[...] + rstd * (dxh - xhat * jnp.mean(dxh * xhat, axis=-1, keepdims=True))

    return _call(body, name=name, grid=(t // TM,),
                 in_specs=[_rows(TM, d), _rows(TM, d), _rows(TM, d), _fixed((1, d)),
                           pl.BlockSpec((1, N_MOD, d), lambda i: (i // nx, 0, 0))],
                 out_specs=[_rows(TM, d), _stream(d, nx), _stream(d, nx), _fixed((1, d))],
                 out_shape=[_sds((t, d), F32), _sds((2, 1, d), F32), _sds((2, 1, d), F32), _sds((1, d), F32)])(
                     xs, dh, dres, g, mods)


def _resid(xs, y, mods, k_gate, coef, nx, name):
    t, d = xs.shape

    def body(x_ref, y_ref, m_ref, o_ref):
        o_ref[...] = x_ref[...] + (coef * m_ref[0, k_gate:k_gate + 1, :]) * y_ref[...]

    return _call(body, name=name, grid=(t // TM,),
                 in_specs=[_rows(TM, d), _rows(TM, d), pl.BlockSpec((1, N_MOD, d), lambda i: (i // nx, 0, 0))],
                 out_specs=_rows(TM, d), out_shape=_sds((t, d), F32))(xs, y, mods)


def _resid_bwd(dxn, y, mods, k_gate, coef, nx, name):
    t, d = dxn.shape

    def body(dx_ref, y_ref, m_ref, dy_ref, dgt_ref):
        i = pl.program_id(0)

        @pl.when((i == 0) | (i == nx))
        def _():
            dgt_ref[...] = jnp.zeros_like(dgt_ref)

        dx = dx_ref[...]
        dy_ref[...] = ((coef * m_ref[0, k_gate:k_gate + 1, :]) * dx).astype(dy_ref.dtype)
        dgt_ref[0] += jnp.sum(coef * y_ref[...] * dx, axis=0, keepdims=True)

    return _call(body, name=name, grid=(t // TM,),
                 in_specs=[_rows(TM, d), _rows(TM, d), pl.BlockSpec((1, N_MOD, d), lambda i: (i // nx, 0, 0))],
                 out_specs=[_rows(TM, d), _stream(d, nx)],
                 out_shape=[_sds((t, d), MXU), _sds((2, 1, d), F32)])(dxn, y, mods)


def _swiglu(up, name):
    t, f2 = up.shape
    f = f2 // 2

    def body(u_ref, o_ref):
        a = u_ref[:, :f]
        o_ref[...] = (a * jax.nn.sigmoid(a) * u_ref[:, f:]).astype(o_ref.dtype)

    return _call(body, name=name, grid=(t // TM,), in_specs=[_rows(TM, f2)], out_specs=_rows(TM, f),
                 out_shape=_sds((t, f), MXU))(up)


def _swiglu_bwd(up, dact, name):
    t, f2 = up.shape
    f = f2 // 2

    def body(u_ref, d_ref, o_ref):
        a = u_ref[:, :f]
        b = u_ref[:, f:]
        dact = d_ref[...]
        sg = jax.nn.sigmoid(a)
        o_ref[:, :f] = (dact * b * sg * (1.0 + a * (1.0 - sg))).astype(o_ref.dtype)
        o_ref[:, f:] = (dact * a * sg).astype(o_ref.dtype)

    return _call(body, name=name, grid=(t // TM,), in_specs=[_rows(TM, f2), _rows(TM, f)], out_specs=_rows(TM, f2),
                 out_shape=_sds((t, f2), MXU))(up, dact)


def _rope_tables(s, ctx_len):
    n_freq = HEAD_DIM // 4
    tok = jnp.arange(s)
    freqs = ROPE_THETA ** (-jnp.arange(n_freq, dtype=F32) / n_freq)
    ang = jnp.concatenate([(tok // GRID_W).astype(F32)[:, None] * freqs, (tok % GRID_W).astype(F32)[:, None] * freqs], axis=-1)
    cos = jnp.repeat(jnp.cos(ang), 2, axis=-1)
    sin = jnp.repeat(jnp.sin(ang), 2, axis=-1) * jnp.tile(jnp.array([-1.0, 1.0], F32), HEAD_DIM // 2)
    cos = jnp.concatenate([jnp.tile(cos, (1, HEADS)), jnp.ones((ctx_len, NA_WIDTH), F32)], axis=0)
    sin = jnp.concatenate([jnp.tile(sin, (1, HEADS)), jnp.zeros((ctx_len, NA_WIDTH), F32)], axis=0)
    return cos, sin


def _swap_pairs(x):
    n = x.shape[-1]
    lane = lax.broadcasted_iota(jnp.int32, x.shape, 1)
    return jnp.where(lane % 2 == 0, pltpu.roll(x, n - 1, 1), pltpu.roll(x, 1, 1))


def _rope(p, cos, sin, name):
    t = p.shape[0]
    w = NA_WIDTH

    def body(q_ref, k_ref, v_ref, c_ref, s_ref, qo_ref, ko_ref, vo_ref):
        c, s = c_ref[...], s_ref[...]
        q, k = q_ref[...], k_ref[...]
        qo_ref[...] = (q * c + _swap_pairs(q) * s).astype(qo_ref.dtype)
        ko_ref[...] = (k * c + _swap_pairs(k) * s).astype(ko_ref.dtype)
        vo_ref[...] = v_ref[...].astype(vo_ref.dtype)

    return _call(body, name=name, grid=(t // TM,),
                 in_specs=[_rows(TM, w, 0), _rows(TM, w, 1), _rows(TM, w, 2), _rows(TM, w), _rows(TM, w)],
                 out_specs=[_rows(TM, w)] * 3, out_shape=[_sds((t, w), MXU)] * 3)(p, p, p, cos, sin)


def _rope_bwd(dq, dk, dv, cos, sin, name):
    t = dq.shape[0]

    def body(dq_ref, dk_ref, dv_ref, c_ref, s_ref, qo_ref, ko_ref, vo_ref):
        c, s = c_ref[...], s_ref[...]
        a, b = dq_ref[...], dk_ref[0]
        qo_ref[...] = (a * c + _swap_pairs(a * s)).astype(qo_ref.dtype)
        ko_ref[...] = (b * c + _swap_pairs(b * s)).astype(ko_ref.dtype)
        vo_ref[...] = dv_ref[0].astype(vo_ref.dtype)

    tile = pl.BlockSpec((TM, LANES), lambda i, j: (i, j))
    pair = pl.BlockSpec((1, TM, LANES), lambda i, j: (j, i, 0))
    return _call(body, name=name, grid=(t // TM, NA_WIDTH // LANES), in_specs=[tile, pair, pair, tile, tile],
                 out_specs=[tile] * 3, out_shape=[_sds((t, NA_WIDTH), MXU)] * 3)(dq, dk, dv, cos, sin)


def _na_geometry(r_grid):
    rows = []
    for r0, ks in ((0, 0), (Q_ROWS, 0), (r_grid - Q_ROWS, r_grid - K_ROWS)):
        dr = np.zeros((Q_ROWS, K_ROWS), np.int32)
        vr = np.zeros((Q_ROWS, K_ROWS), bool)
        for a in range(Q_ROWS):
            r = r0 + a
            rs = min(max(r - WIN_H // 2, 0), r_grid - WIN_H)
            for i in range(K_ROWS):
                kr = ks + i
                vr[a, i] = rs <= kr <= rs + WIN_H - 1
                dr[a, i] = kr - r + WIN_H - 1
        rows.append((dr, vr))
    c = np.arange(GRID_W)
    cs = np.clip(c - WIN_W // 2, 0, GRID_W - WIN_W)
    kc = np.arange(GRID_W)
    vc = (kc[None, :] >= cs[:, None]) & (kc[None, :] <= cs[:, None] + WIN_W - 1)
    dc = kc[None, :] - c[:, None] + WIN_W - 1
    return rows, dc, vc


def _bias_table(rpb, r_grid):
    rows, dc, vc = _na_geometry(r_grid)
    tabs = []
    for dr, vr in rows:
        t = rpb[:, np.clip(dr, 0, 2 * WIN_H - 2)[:, :, None, None], np.clip(dc, 0, 2 * WIN_W - 2)[None, None, :, :]]
        valid = vr[:, :, None, None] & vc[None, None, :, :]
        t = jnp.where(valid[None], t, NEG)
        tabs.append(jnp.transpose(t, (0, 1, 3, 2, 4)).reshape(HEADS, TQ, TK))
    tabs.append(jnp.full((HEADS, TQ, TK), NEG, F32))
    return jnp.stack(tabs)


def _variant(g, ngx):
    return jnp.where(g == 0, 0, jnp.where(g >= ngx, 3, jnp.where(g == ngx - 1, 2, 1)))


def _key_start(g, r_grid):
    return pl.multiple_of(jnp.clip(g * Q_ROWS - WIN_H // 2, 0, r_grid - K_ROWS) * GRID_W, TQ)


def _nt(a, b):
    return lax.dot_general(a, b, (((1,), (1,)), ((), ())), preferred_element_type=F32)


def _tn(a, b):
    return lax.dot_general(a, b, (((0,), (0,)), ((), ())), preferred_element_type=F32)


def _nn(a, b):
    return jnp.dot(a, b, preferred_element_type=F32)


def _head_mask(h):
    lane = lax.broadcasted_iota(jnp.int32, (1, LANES), 1)
    return ((lane >= HEAD_DIM * h) & (lane < HEAD_DIM * (h + 1))).astype(F32)


def _softmax_parts(qm, knb, kcx, bias):
    s_nb = _nt(qm, knb) + bias
    s_cx = _nt(qm, kcx)
    m = jnp.maximum(jnp.max(s_nb, axis=-1, keepdims=True), jnp.max(s_cx, axis=-1, keepdims=True))
    e_nb = jnp.exp(s_nb - m)
    e_cx = jnp.exp(s_cx - m)
    inv = 1.0 / (jnp.sum(e_nb, axis=-1, keepdims=True) + jnp.sum(e_cx, axis=-1, keepdims=True))
    return e_nb * inv, e_cx * inv


def _na_specs(t, ngx):
    q_spec = pl.BlockSpec((TQ, LANES), lambda hp, g: (g, hp))
    kv_spec = pl.BlockSpec((t, LANES), lambda hp, g: (0, hp))
    b_spec = pl.BlockSpec((1, 2, TQ, TK), lambda hp, g: (_variant(g, ngx), hp, 0, 0))
    return q_spec, kv_spec, b_spec


def _na_fwd(q, k, v, bias, s, name):
    t = q.shape[0]
    ctx_len = t - s
    r_grid = s // GRID_W
    q_spec, kv_spec, b_spec = _na_specs(t, s // TQ)

    def body(q_ref, k_ref, v_ref, b_ref, o_ref):
        start = _key_start(pl.program_id(1), r_grid)
        qf = q_ref[...].astype(F32) * (HEAD_DIM ** -0.5)
        knb, vnb = k_ref[pl.ds(start, TK), :], v_ref[pl.ds(start, TK), :]
        kcx, vcx = k_ref[pl.ds(s, ctx_len), :], v_ref[pl.ds(s, ctx_len), :]
        acc = jnp.zeros((TQ, LANES), F32)
        for h in range(2):
            mask = _head_mask(h)
            p_nb, p_cx = _softmax_parts((qf * mask).astype(MXU), knb, kcx, b_ref[0, h])
            acc += (_nn(p_nb.astype(MXU), vnb) + _nn(p_cx.astype(MXU), vcx)) * mask
        o_ref[...] = acc.astype(o_ref.dtype)

    return _call(body, name=name, grid=(NA_WIDTH // LANES, t // TQ), in_specs=[q_spec, kv_spec, kv_spec, b_spec],
                 out_specs=q_spec, out_shape=_sds((t, NA_WIDTH), MXU))(q, k, v, bias)


def _na_bwd(q, k, v, do, bias, s, name):
    t = q.shape[0]
    ctx_len = t - s
    r_grid = s // GRID_W
    ng, ngx = t // TQ, s // TQ
    q_spec, kv_spec, b_spec = _na_specs(t, ngx)

    def body(q_ref, k_ref, v_ref, do_ref, b_ref, dq_ref, dk_hbm, dv_hbm, db_ref, dk_acc, dv_acc):
        hp, g = pl.program_id(0), pl.program_id(1)
        start = _key_start(g, r_grid)

        @pl.when(g == 0)
        def _():
            dk_acc[...] = jnp.zeros_like(dk_acc)
            dv_acc[...] = jnp.zeros_like(dv_acc)

        @pl.when((g == 0) | (g == 1) | (g == ngx - 1) | (g == ngx))
        def _():
            db_ref[...] = jnp.zeros_like(db_ref)

        qf = q_ref[...].astype(F32) * (HEAD_DIM ** -0.5)
        do = do_ref[...].astype(F32)
        knb, vnb = k_ref[pl.ds(start, TK), :], v_ref[pl.ds(start, TK), :]
        kcx, vcx = k_ref[pl.ds(s, ctx_len), :], v_ref[pl.ds(s, ctx_len), :]
        dq = jnp.zeros((TQ, LANES), F32)
        dk_nb = jnp.zeros((TK, LANES), F32)
        dv_nb = jnp.zeros((TK, LANES), F32)
        dk_cx = jnp.zeros((ctx_len, LANES), F32)
        dv_cx = jnp.zeros((ctx_len, LANES), F32)
        for h in range(2):
            mask = _head_mask(h)
            qm = (qf * mask).astype(MXU)
            dom = (do * mask).astype(MXU)
            p_nb, p_cx = _softmax_parts(qm, knb, kcx, b_ref[0, h])
            dp_nb = _nt(dom, vnb)
            dp_cx = _nt(dom, vcx)
            delta = jnp.sum(p_nb * dp_nb, axis=-1, keepdims=True) + jnp.sum(p_cx * dp_cx, axis=-1, keepdims=True)
            ds_nb = p_nb * (dp_nb - delta)
            ds_cx = p_cx * (dp_cx - delta)
            db_ref[0, h] += ds_nb
            ds_nb, ds_cx = ds_nb.astype(MXU), ds_cx.astype(MXU)
            dq += (_nn(ds_nb, knb) + _nn(ds_cx, kcx)) * (mask * (HEAD_DIM ** -0.5))
            dk_nb += _tn(ds_nb, qm)
            dk_cx += _tn(ds_cx, qm)
            dv_nb += _tn(p_nb.astype(MXU), dom)
            dv_cx += _tn(p_cx.astype(MXU), dom)
        dq_ref[...] = dq
        dk_acc[pl.ds(start, TK), :] += dk_nb
        dv_acc[pl.ds(start, TK), :] += dv_nb
        dk_acc[pl.ds(s, ctx_len), :] += dk_cx
        dv_acc[pl.ds(s, ctx_len), :] += dv_cx

        @pl.when(g == ng - 1)
        def _():
            pltpu.sync_copy(dk_acc, dk_hbm.at[hp])
            pltpu.sync_copy(dv_acc, dv_hbm.at[hp])

    n_pairs = NA_WIDTH // LANES
    hbm = pl.BlockSpec(memory_space=pl.ANY)
    return _call(body, name=name, grid=(n_pairs, ng), in_specs=[q_spec, kv_spec, kv_spec, q_spec, b_spec],
                 out_specs=[q_spec, hbm, hbm, b_spec],
                 out_shape=[_sds((t, NA_WIDTH), F32), _sds((n_pairs, t, LANES), F32), _sds((n_pairs, t, LANES), F32),
                            _sds((4, HEADS, TQ, TK), F32)],
                 scratch=[pltpu.VMEM((t, LANES), F32), pltpu.VMEM((t, LANES), F32)])(q, k, v, do, bias)


def _rpb_grad(dbias, r_grid, name):
    rows, _, _ = _na_geometry(r_grid)
    n_blk = 3 * Q_ROWS * K_ROWS
    z = dbias[:3].reshape(3, HEADS, Q_ROWS, GRID_W, K_ROWS, GRID_W)
    z = jnp.transpose(z, (1, 0, 2, 4, 3, 5)).reshape(HEADS, n_blk, GRID_W, GRID_W)
    z = jnp.pad(z[:, :, ::-1, :], ((0, 0), (0, 0), (0, 0), (0, GRID_W))).reshape(HEADS, n_blk, 2 * GRID_W * GRID_W)
    skew = 2 * GRID_W - 1
    z = jnp.pad(z, ((0, 0), (0, 0), (0, (GRID_W + 1) * skew - 2 * GRID_W * GRID_W))).reshape(HEADS, n_blk, GRID_W + 1, skew)
    z = jnp.pad(z, ((0, 0), (0, 0), (0, 72 - (GRID_W + 1)), (0, 1)))
    members = [[] for _ in range(2 * WIN_H - 1)]
    for vi, (dr, vr) in enumerate(rows):
        for a in range(Q_ROWS):
            for i in range(K_ROWS):
                if vr[a, i]:
                    members[dr[a, i]].append((vi * Q_ROWS + a) * K_ROWS + i)

    def body(z_ref, o_ref):
        zs = jnp.sum(z_ref[0], axis=1)
        out = []
        for mem in members:
            acc = jnp.zeros((1, LANES), F32)
            for j in mem:
                acc = acc + zs[j:j + 1, :]
            out.append(acc)
        out.append(jnp.zeros((1, LANES), F32))
        o_ref[0] = jnp.concatenate(out, axis=0)

    o = _call(body, name=name, grid=(HEADS,), in_specs=[pl.BlockSpec((1, n_blk, 72, LANES), lambda h: (h, 0, 0, 0))],
              out_specs=pl.BlockSpec((1, 16, LANES), lambda h: (h, 0, 0)), out_shape=_sds((HEADS, 16, LANES), F32))(z)
    off = GRID_W - 1 - (WIN_W - 1)
    return o[:, :2 * WIN_H - 1, off:off + 2 * WIN_W - 1]


_GELU_K, _GELU_C = 0.7978845608028654, 0.044715


def _gelu(x):
    return 0.5 * x * (1.0 + jnp.tanh(_GELU_K * (x + _GELU_C * x * x * x)))


def _gelu_grad(x):
    th = jnp.tanh(_GELU_K * (x + _GELU_C * x * x * x))
    return 0.5 * (1.0 + th) + 0.5 * x * (1.0 - th * th) * (_GELU_K * (1.0 + 3.0 * _GELU_C * x * x))


def _ln_stats(v):
    mu = jnp.mean(v, axis=-1, keepdims=True)
    vc = v - mu
    rstd = lax.rsqrt(jnp.mean(vc * vc, axis=-1, keepdims=True) + EPS)
    return vc * rstd, rstd


def _gmlp(p, ln_g, ln_b, w_s, b_s, name):
    t = p.shape[0]
    w = SG_WIDTH
    cw = w // SG_GROUPS

    def body(u_ref, v_ref, g_ref, b_ref, ws_ref, bs_ref, o_ref):
        xhat, _ = _ln_stats(_gelu(v_ref[...]))
        vn = (xhat * g_ref[...] + b_ref[...]).astype(MXU)
        ug = _gelu(u_ref[...])
        for ci in range(TM // SG_CHUNK):
            rs = slice(ci * SG_CHUNK, (ci + 1) * SG_CHUNK)
            for gi in range(SG_GROUPS):
                cs = slice(gi * cw, (gi + 1) * cw)
                sg = _nn(ws_ref[gi].astype(MXU), vn[rs, cs]) + bs_ref[gi]
                o_ref[rs, cs] = (ug[rs, cs] * sg).astype(o_ref.dtype)

    return _call(body, name=name, grid=(t // TM,),
                 in_specs=[_rows(TM, w, 3), _rows(TM, w, 4), _fixed((1, w)), _fixed((1, w)),
                           _fixed((SG_GROUPS, SG_CHUNK, SG_CHUNK)), _fixed((SG_GROUPS, SG_CHUNK, 1))],
                 out_specs=_rows(TM, w), out_shape=_sds((t, w), MXU))(p, p, ln_g, ln_b, w_s, b_s)


def _gmlp_bwd(p, dob, ln_g, ln_b, w_s, b_s, name):
    t = p.shape[0]
    w = SG_WIDTH
    cw = w // SG_GROUPS

    def body(u_ref, v_ref, do_ref, g_ref, b_ref, ws_ref, bs_ref, du_ref, dv_ref, dws_ref, dbs_ref, dg_ref, db_ref, dvn_ref):
        @pl.when(pl.program_id(0) == 0)
        def _():
            dws_ref[...] = jnp.zeros_like(dws_ref)
            dbs_ref[...] = jnp.zeros_like(dbs_ref)
            dg_ref[...] = jnp.zeros_like(dg_ref)
            db_ref[...] = jnp.zeros_like(db_ref)

        u, v = u_ref[...], v_ref[...]
        xhat, rstd = _ln_stats(_gelu(v))
        vn = (xhat * g_ref[...] + b_ref[...]).astype(MXU)
        ug = _gelu(u)
        dob = do_ref[...]
        for ci in range(TM // SG_CHUNK):
            rs = slice(ci * SG_CHUNK, (ci + 1) * SG_CHUNK)
            for gi in range(SG_GROUPS):
                cs = slice(gi * cw, (gi + 1) * cw)
                wsg = ws_ref[gi].astype(MXU)
                sg = _nn(wsg, vn[rs, cs]) + bs_ref[gi]
                du_ref[rs, cs] = (dob[rs, cs] * sg * _gelu_grad(u[rs, cs])).astype(du_ref.dtype)
                ds = dob[rs, cs] * ug[rs, cs]
                dbs_ref[gi] += jnp.sum(ds, axis=-1, keepdims=True)
                ds = ds.astype(MXU)
                dws_ref[gi] += _nt(ds, vn[rs, cs])
                dvn_ref[rs, cs] = _tn(wsg, ds)
        dvn = dvn_ref[...]
        dg_ref[...] += jnp.sum(dvn * xhat, axis=0, keepdims=True)
        db_ref[...] += jnp.sum(dvn, axis=0, keepdims=True)
        dxh = dvn * g_ref[...]
        dvg = rstd * (dxh - jnp.mean(dxh, axis=-1, keepdims=True) - xhat * jnp.mean(dxh * xhat, axis=-1, keepdims=True))
        dv_ref[...] = (dvg * _gelu_grad(v)).astype(dv_ref.dtype)

    return _call(body, name=name, grid=(t // TM,),
                 in_specs=[_rows(TM, w, 3), _rows(TM, w, 4), _rows(TM, w), _fixed((1, w)), _fixed((1, w)),
                           _fixed((SG_GROUPS, SG_CHUNK, SG_CHUNK)), _fixed((SG_GROUPS, SG_CHUNK, 1))],
                 out_specs=[_rows(TM, w), _rows(TM, w), _fixed((SG_GROUPS, SG_CHUNK, SG_CHUNK)),
                            _fixed((SG_GROUPS, SG_CHUNK, 1)), _fixed((1, w)), _fixed((1, w))],
                 out_shape=[_sds((t, w), MXU), _sds((t, w), MXU), _sds((SG_GROUPS, SG_CHUNK, SG_CHUNK), F32),
                            _sds((SG_GROUPS, SG_CHUNK, 1), F32), _sds((1, w), F32), _sds((1, w), F32)],
                 scratch=[pltpu.VMEM((TM, w), F32)])(p, p, dob, ln_g, ln_b, w_s, b_s)


def _merge(pa, pb, p, b_gate, name):
    t, d = pa.shape
    hw = NA_WIDTH
    nh = d // hw
    c0 = (NA_WIDTH * 3 + SG_WIDTH * 2) // hw

    def body(pa_ref, pb_ref, la_ref, lb_ref, ba_ref, bb_ref, o_ref):
        ga = jax.nn.sigmoid(la_ref[...] + ba_ref[...])
        gb = jax.nn.sigmoid(lb_ref[...] + bb_ref[...])
        o_ref[...] = (ga * pa_ref[...] + gb * pb_ref[...]).astype(o_ref.dtype)

    tile = pl.BlockSpec((TM, hw), lambda i, j: (i, j))
    return _call(body, name=name, grid=(t // TM, nh),
                 in_specs=[tile, tile, pl.BlockSpec((TM, hw), lambda i, j: (i, c0 + j)),
                           pl.BlockSpec((TM, hw), lambda i, j: (i, c0 + nh + j)),
                           pl.BlockSpec((1, hw), lambda i, j: (0, j)), pl.BlockSpec((1, hw), lambda i, j: (0, nh + j))],
                 out_specs=tile, out_shape=_sds((t, d), MXU))(pa, pb, p, p, b_gate, b_gate)


def _merge_bwd(dmg, pa, pb, p, b_gate, name):
    t, d = pa.shape
    hw = NA_WIDTH
    nh = d // hw
    c0 = (NA_WIDTH * 3 + SG_WIDTH * 2) // hw

    def body(dm_ref, pa_ref, pb_ref, la_ref, lb_ref, ba_ref, bb_ref, dpa_ref, dpb_ref, dla_ref, dlb_ref, dba_ref, dbb_ref):
        @pl.when(pl.program_id(1) == 0)
        def _():
            dba_ref[...] = jnp.zeros_like(dba_ref)
            dbb_ref[...] = jnp.zeros_like(dbb_ref)

        dm = dm_ref[...]
        ga = jax.nn.sigmoid(la_ref[...] + ba_ref[...])
        gb = jax.nn.sigmoid(lb_ref[...] + bb_ref[...])
        dpa_ref[...] = (dm * ga).astype(dpa_ref.dtype)
        dpb_ref[...] = (dm * gb).astype(dpb_ref.dtype)
        dla = dm * pa_ref[...] * ga * (1.0 - ga)
        dlb = dm * pb_ref[...] * gb * (1.0 - gb)
        dla_ref[...] = dla.astype(dla_ref.dtype)
        dlb_ref[...] = dlb.astype(dlb_ref.dtype)
        dba_ref[...] += jnp.sum(dla, axis=0, keepdims=True)
        dbb_ref[...] += jnp.sum(dlb, axis=0, keepdims=True)

    tile = pl.BlockSpec((TM, hw), lambda j, i: (i, j))
    bias_a = pl.BlockSpec((1, hw), lambda j, i: (0, j))
    bias_b = pl.BlockSpec((1, hw), lambda j, i: (0, nh + j))
    return _call(body, name=name, grid=(nh, t // TM),
                 in_specs=[tile, tile, tile, pl.BlockSpec((TM, hw), lambda j, i: (i, c0 + j)),
                           pl.BlockSpec((TM, hw), lambda j, i: (i, c0 + nh + j)), bias_a, bias_b],
                 out_specs=[tile, tile, tile, tile, bias_a, bias_a],
                 out_shape=[_sds((t, d), MXU)] * 4 + [_sds((1, d), F32)] * 2)(dmg, pa, pb, p, p, b_gate, b_gate)


def _final(xs, tgt, g, name):
    t, d = xs.shape
    nx = tgt.shape[0] // TM

    def body(x_ref, t_ref, g_ref, l_ref, dx_ref, dg_ref):
        i = pl.program_id(0)

        @pl.when(i == 0)
        def _():
            l_ref[...] = jnp.zeros_like(l_ref)
            dg_ref[...] = jnp.zeros_like(dg_ref)

        @pl.when(i < nx)
        def _():
            x = x_ref[...]
            rstd = lax.rsqrt(jnp.mean(x * x, axis=-1, keepdims=True) + EPS)
            xhat = x * rstd
            err = xhat * g_ref[...] - t_ref[...]
            l_ref[...] += 0.5 * jnp.sum(jnp.mean(err * err, axis=-1, keepdims=True))
            dy = err * (1.0 / d)
            dg_ref[...] += jnp.sum(dy * xhat, axis=0, keepdims=True)
            dxh = dy * g_ref[...]
            dx_ref[...] = rstd * (dxh - xhat * jnp.mean(dxh * xhat, axis=-1, keepdims=True))

        @pl.when(i >= nx)
        def _():
            dx_ref[...] = jnp.zeros_like(dx_ref)

    return _call(body, name=name, grid=(t // TM,),
                 in_specs=[_rows(TM, d), pl.BlockSpec((TM, d), lambda i: (jnp.minimum(i, nx - 1), 0)), _fixed((1, d))],
                 out_specs=[_fixed((1, LANES)), _rows(TM, d), _fixed((1, d))],
                 out_shape=[_sds((1, LANES), F32), _sds((t, d), F32), _sds((1, d), F32)])(xs, tgt, g)


def _view2d(a):
    return a.reshape(1, -1) if a.ndim == 1 else a.reshape(-1, a.shape[-1])


def _ew(fn, arrays, n_out, name):
    shape = arrays[0].shape
    views = [_view2d(a) for a in arrays]
    r, c = views[0].shape
    tr = r
    for cand in (1024, 512, 256, 128, 64, 32, 16, 8):
        if r % cand == 0 and cand * c * 4 <= 2 ** 20:
            tr = cand
            break

    def body(*refs):
        outs = fn(*[ref[...] for ref in refs[:len(views)]])
        for ref, o in zip(refs[len(views):], outs):
            ref[...] = o

    res = _call(body, name=name, grid=(r // tr,), in_specs=[_rows(tr, c)] * len(views), out_specs=[_rows(tr, c)] * n_out,
                out_shape=[_sds((r, c), F32)] * n_out)(*views)
    return [o.reshape(shape) for o in res]


def _sum_pieces(pieces, name):
    def fn(*vals):
        acc = vals[0]
        for v in vals[1:]:
            acc = acc + v
        return (acc,)

    return _ew(fn, pieces, 1, name)[0]


def _adamw(w, g_pieces, m, v, name):
    n_g = len(g_pieces)

    def fn(w_, *rest):
        g = rest[0]
        for piece in rest[1:n_g]:
            g = g + piece
        m_, v_ = rest[n_g], rest[n_g + 1]
        m2 = ADAM_B1 * m_ + (1.0 - ADAM_B1) * g
        v2 = ADAM_B2 * v_ + (1.0 - ADAM_B2) * (g * g)
        m_hat = m2 / (1.0 - ADAM_B1 ** ADAM_STEP)
        v_hat = v2 / (1.0 - ADAM_B2 ** ADAM_STEP)
        delta = -ADAM_LR * (m_hat / (jnp.sqrt(v_hat) + ADAM_EPS) + ADAM_WD * w_)
        return g, delta, m2, v2

    return _ew(fn, [w, *g_pieces, m, v], 4, name)


def _ada_fwd(cond, w, b, name):
    r, d = cond.shape
    n = w.shape[1]
    tn = _pick(n, (1152, 768, 512, 384, 256, 128))

    def body(c_ref, w_ref, b_ref, o_ref, s_ref):
        c = c_ref[...]
        sc = c * jax.nn.sigmoid(c)
        s_ref[...] = sc
        o_ref[...] = _nn(sc.astype(MXU), w_ref[...].astype(MXU)) + b_ref[...]

    return _call(body, name=name, grid=(n // tn,),
                 in_specs=[_fixed((r, d)), pl.BlockSpec((d, tn), lambda j: (0, j)), pl.BlockSpec((1, tn), lambda j: (0, j))],
                 out_specs=[pl.BlockSpec((r, tn), lambda j: (0, j)), _fixed((r, d))],
                 out_shape=[_sds((r, n), F32), _sds((r, d), F32)])(cond, w, b)


def _cctx_grad(parts, c_ctx, name):
    n, d = parts.shape

    def body(p_ref, c_ref, o_ref):
        c = c_ref[...]
        sg = jax.nn.sigmoid(c)
        acc = p_ref[0:1, :]
        for j in range(1, n):
            acc = acc + p_ref[j:j + 1, :]
        o_ref[...] = acc * (sg * (1.0 + c * (1.0 - sg)))

    return _call(body, name=name, grid=(1,), in_specs=[_fixed((n, d)), _fixed((1, d))], out_specs=_fixed((1, d)),
                 out_shape=_sds((1, d), F32))(parts, c_ctx)


def _here():
    return lax.axis_index("x"), lax.axis_index("y"), lax.axis_index("c")


def _flip(v, bit):
    return 1 - v if bit else v


def _allgather8(xb, name):
    r, n = xb.shape

    def body(x_ref, out_ref, send_sems, recv_sems, local_sem):
        x, y, c = _here()
        me = 4 * x + 2 * y + c
        local = pltpu.make_async_copy(x_ref, out_ref.at[me], local_sem)
        local.start()
        sends = []
        for k in range(1, 8):
            peer = (_flip(x, k & 4), _flip(y, k & 2), _flip(c, k & 1))
            cp = pltpu.make_async_remote_copy(src_ref=x_ref, dst_ref=out_ref.at[me], send_sem=send_sems.at[k - 1],
                                              recv_sem=recv_sems.at[k - 1], device_id=peer, device_id_type=MESH)
            cp.start()
            sends.append(cp)
        for k in range(1, 8):
            peer = (_flip(x, k & 4), _flip(y, k & 2), _flip(c, k & 1))
            src = 4 * peer[0] + 2 * peer[1] + peer[2]
            pltpu.make_async_remote_copy(src_ref=x_ref, dst_ref=out_ref.at[src], send_sem=send_sems.at[k - 1],
                                         recv_sem=recv_sems.at[k - 1], device_id=peer, device_id_type=MESH).wait_recv()
        for cp in sends:
            cp.wait_send()
        local.wait()

    vmem = pl.BlockSpec(memory_space=pltpu.VMEM)
    return pl.pallas_call(
        body, name=name, out_shape=_sds((8, r, n), xb.dtype), in_specs=[vmem], out_specs=vmem,
        scratch_shapes=[pltpu.SemaphoreType.DMA((7,)), pltpu.SemaphoreType.DMA((7,)), pltpu.SemaphoreType.DMA(())],
        compiler_params=pltpu.CompilerParams(vmem_limit_bytes=VMEM_LIMIT))(xb)


def _shard_of(ref, axis, j, size):
    sl = pl.ds(j * size, size)
    return ref.at[:, sl, :] if axis == 1 else ref.at[:, :, sl]


def _gather_chips(shards, axes, name):
    n = len(shards)
    fulls = []
    for a, ax in zip(shards, axes):
        shp = list(a.shape)
        shp[ax] *= 4
        fulls.append(_sds(tuple(shp), a.dtype))

    def body(*refs):
        ins, outs = refs[:n], refs[n:2 * n]
        send_sems, recv_sems, local_sems = refs[2 * n:]
        x, y, c = _here()
        local, sends = [], []
        for a in range(n):
            size = ins[a].shape[axes[a]]
            cp = pltpu.make_async_copy(ins[a], _shard_of(outs[a], axes[a], 2 * x + y, size), local_sems.at[a])
            cp.start()
            local.append(cp)
            for k in range(1, 4):
                peer = (_flip(x, k & 2), _flip(y, k & 1), c)
                cp = pltpu.make_async_remote_copy(src_ref=ins[a], dst_ref=_shard_of(outs[a], axes[a], 2 * x + y, size),
                                                  send_sem=send_sems.at[3 * a + k - 1], recv_sem=recv_sems.at[3 * a + k - 1],
                                                  device_id=peer, device_id_type=MESH)
                cp.start()
                sends.append(cp)
        for a in range(n):
            size = ins[a].shape[axes[a]]
            for k in range(1, 4):
                peer = (_flip(x, k & 2), _flip(y, k & 1), c)
                pltpu.make_async_remote_copy(src_ref=ins[a], dst_ref=_shard_of(outs[a], axes[a], 2 * peer[0] + peer[1], size),
                                             send_sem=send_sems.at[3 * a + k - 1], recv_sem=recv_sems.at[3 * a + k - 1],
                                             device_id=peer, device_id_type=MESH).wait_recv()
        for cp in sends:
            cp.wait_send()
        for cp in local:
            cp.wait()

    hbm = pl.BlockSpec(memory_space=pl.ANY)
    return pl.pallas_call(
        body, name=name, out_shape=fulls, in_specs=[hbm] * n, out_specs=[hbm] * n,
        scratch_shapes=[pltpu.SemaphoreType.DMA((3 * n,)), pltpu.SemaphoreType.DMA((3 * n,)), pltpu.SemaphoreType.DMA((n,))])(*shards)


def _scatter_chips(fulls, axes, name):
    n = len(fulls)
    recvs = []
    for a, ax in zip(fulls, axes):
        shp = list(a.shape)
        shp[ax] //= 4
        recvs.append(_sds((3, *shp), a.dtype))

    def body(*refs):
        ins, outs = refs[:n], refs[n:2 * n]
        send_sems, recv_sems = refs[2 * n:]
        x, y, c = _here()
        sends = []
        for a in range(n):
            size = ins[a].shape[axes[a]] // 4
            for k in range(1, 4):
                peer = (_flip(x, k & 2), _flip(y, k & 1), c)
                cp = pltpu.make_async_remote_copy(src_ref=_shard_of(ins[a], axes[a], 2 * peer[0] + peer[1], size),
                                                  dst_ref=outs[a].at[k - 1],
                                                  send_sem=send_sems.at[3 * a + k - 1], recv_sem=recv_sems.at[3 * a + k - 1],
                                                  device_id=peer, device_id_type=MESH)
                cp.start()
                sends.append(cp)
        for cp in sends:
            cp.wait_recv()
        for cp in sends:
            cp.wait_send()

    hbm = pl.BlockSpec(memory_space=pl.ANY)
    return pl.pallas_call(
        body, name=name, out_shape=recvs, in_specs=[hbm] * n, out_specs=[hbm] * n,
        scratch_shapes=[pltpu.SemaphoreType.DMA((3 * n,)), pltpu.SemaphoreType.DMA((3 * n,))])(*fulls)


def _sibling_swap(arrays, name):
    n = len(arrays)

    def body(*refs):
        ins, outs = refs[:n], refs[n:2 * n]
        send_sems, recv_sems = refs[2 * n:]
        x, y, c = _here()
        copies = []
        for a in range(n):
            cp = pltpu.make_async_remote_copy(src_ref=ins[a], dst_ref=outs[a], send_sem=send_sems.at[a], recv_sem=recv_sems.at[a],
                                              device_id=(x, y, 1 - c), device_id_type=MESH)
            cp.start()
            copies.append(cp)
        for cp in copies:
            cp.wait()

    hbm = pl.BlockSpec(memory_space=pl.ANY)
    return pl.pallas_call(
        body, name=name, out_shape=[_sds(a.shape, a.dtype) for a in arrays], in_specs=[hbm] * n, out_specs=[hbm] * n,
        scratch_shapes=[pltpu.SemaphoreType.DMA((n,)), pltpu.SemaphoreType.DMA((n,))])(*arrays)


def _ffn_fwd(xs, g, mods, k0, w_up, w_down, nx, tag):
    h = _normmod(xs, g, mods, k0, k0 + 1, nx, tag + "_norm")
    up = _mm(h, w_up, "nn", F32, tag + "_up")
    act = _swiglu(up, tag + "_act")
    y = _mm(act, w_down, "nn", F32, tag + "_down")
    return _resid(xs, y, mods, k0 + 2, 0.5, nx, tag + "_res"), (xs, h, up, act, y)


def _ffn_bwd(dxn, saved, g, mods, k0, w_up, w_down, nx, tag):
    xs, h, up, act, y = saved
    dy, dgate = _resid_bwd(dxn, y, mods, k0 + 2, 0.5, nx, tag + "_res_b")
    d_down = _mm(act, dy, "tn", F32, tag + "_down_dw")
    dact = _mm(dy, w_down, "nt", F32, tag + "_down_dx")
    dup = _swiglu_bwd(up, dact, tag + "_act_b")
    d_up = _mm(h, dup, "tn", F32, tag + "_up_dw")
    dh = _mm(dup, w_up, "nt", F32, tag + "_up_dx")
    dx, dsh, dsc, dg = _normmod_bwd(xs, dh, dxn, g, mods, k0 + 1, nx, tag + "_norm_b")
    return dx, d_up, d_down, dg, [dsh, dsc, dgate]


def _mix_fwd(xs, g, mods, wl, pl_, tabs, s, nx, tag):
    cos, sin = tabs
    h = _normmod(xs, g, mods, 3, 4, nx, tag + "_norm")
    p = _mm(h, wl["w_in"], "nn", F32, tag + "_in")
    q, k, v = _rope(p, cos, sin, tag + "_rope")
    bias = _bias_table(pl_["rpb"], s // GRID_W)
    oa = _na_fwd(q, k, v, bias, s, tag + "_na")
    ob = _gmlp(p, pl_["ln_v_g"], pl_["ln_v_b"], pl_["w_s"], pl_["b_s"], tag + "_sg")
    pa = _mm(oa, wl["w_pa"], "nn", F32, tag + "_pa")
    pb = _mm(ob, wl["w_pb"], "nn", F32, tag + "_pb")
    mg = _merge(pa, pb, p, pl_["b_gate"], tag + "_merge")
    y = _mm(mg, wl["w_o"], "nn", F32, tag + "_o")
    return _resid(xs, y, mods, 5, 1.0, nx, tag + "_res"), (xs, h, p, q, k, v, bias, oa, ob, pa, pb, mg, y)


def _mix_bwd(dxn, saved, g, mods, wl, pl_, tabs, s, nx, tag):
    xs, h, p, q, k, v, bias, oa, ob, pa, pb, mg, y = saved
    cos, sin = tabs
    gw, gp = {}, {}
    dy, dgate = _resid_bwd(dxn, y, mods, 5, 1.0, nx, tag + "_res_b")
    gw["w_o"] = _mm(mg, dy, "tn", F32, tag + "_o_dw")
    dmg = _mm(dy, wl["w_o"], "nt", F32, tag + "_o_dx")
    dpa, dpb, dla, dlb, dba, dbb = _merge_bwd(dmg, pa, pb, p, pl_["b_gate"], tag + "_merge_b")
    gp["b_gate"] = jnp.concatenate([dba, dbb], axis=1)
    gw["w_pa"] = _mm(oa, dpa, "tn", F32, tag + "_pa_dw")
    doa = _mm(dpa, wl["w_pa"], "nt", F32, tag + "_pa_dx")
    gw["w_pb"] = _mm(ob, dpb, "tn", F32, tag + "_pb_dw")
    dob = _mm(dpb, wl["w_pb"], "nt", F32, tag + "_pb_dx")
    du, dvs, gp["w_s"], gp["b_s"], gp["ln_v_g"], gp["ln_v_b"] = _gmlp_bwd(
        p, dob, pl_["ln_v_g"], pl_["ln_v_b"], pl_["w_s"], pl_["b_s"], tag + "_sg_b")
    dqr, dkr, dv, dbias = _na_bwd(q, k, v, doa, bias, s, tag + "_na_b")
    gp["rpb"] = _rpb_grad(dbias, s // GRID_W, tag + "_rpb")
    dq, dk, dvv = _rope_bwd(dqr, dkr, dv, cos, sin, tag + "_rope_b")
    dp = jnp.concatenate([dq, dk, dvv, du, dvs, dla, dlb], axis=1)
    gw["w_in"] = _mm(h, dp, "tn", F32, tag + "_in_dw")
    dh = _mm(dp, wl["w_in"], "nt", F32, tag + "_in_dx")
    dx, dsh, dsc, dg = _normmod_bwd(xs, dh, dxn, g, mods, 4, nx, tag + "_norm_b")
    return dx, gw, gp, dg, [dsh, dsc, dgate]


def _local_step(x, ctx, tgt, mods, wts, prm):
    s, d = x.shape
    depth = mods.shape[0]
    nx = s // TM
    tabs = _rope_tables(s, ctx.shape[0])
    xs = jnp.concatenate([x, ctx], axis=0)
    saved = []
    for l in range(depth):
        wl = {k: v[l] for k, v in wts.items()}
        pl_ = _layer_params(prm, l)
        xs, s1 = _ffn_fwd(xs, pl_["g"][0], mods[l], 0, wl["w_ff1_up"], wl["w_ff1_down"], nx, f"l{l}_ff1")
        xs, s2 = _mix_fwd(xs, pl_["g"][1], mods[l], wl, pl_, tabs, s, nx, f"l{l}_mix")
        xs, s3 = _ffn_fwd(xs, pl_["g"][2], mods[l], 6, wl["w_ff2_up"], wl["w_ff2_down"], nx, f"l{l}_ff2")
        saved.append((s1, s2, s3))
    loss, dxs, d_final_g = _final(xs, tgt, prm["final_g"].reshape(1, d), "final")
    gw = {k: [None] * depth for k in wts}
    gp = {k: [None] * depth for k in ("norm_g", "b_gate", "rpb", "ln_v_g", "ln_v_b", "w_s", "b_s")}
    dmods = [None] * depth
    for l in reversed(range(depth)):
        wl = {k: v[l] for k, v in wts.items()}
        pl_ = _layer_params(prm, l)
        s1, s2, s3 = saved[l]
        dxs, gw["w_ff2_up"][l], gw["w_ff2_down"][l], dg2, dm2 = _ffn_bwd(
            dxs, s3, pl_["g"][2], mods[l], 6, wl["w_ff2_up"], wl["w_ff2_down"], nx, f"l{l}_ff2")
        dxs, gwm, gpm, dg1, dm1 = _mix_bwd(dxs, s2, pl_["g"][1], mods[l], wl, pl_, tabs, s, nx, f"l{l}_mix")
        dxs, gw["w_ff1_up"][l], gw["w_ff1_down"][l], dg0, dm0 = _ffn_bwd(
            dxs, s1, pl_["g"][0], mods[l], 0, wl["w_ff1_up"], wl["w_ff1_down"], nx, f"l{l}_ff1")
        for k, v in gwm.items():
            gw[k][l] = v
        gp["b_gate"][l] = gpm["b_gate"][0]
        gp["rpb"][l] = gpm["rpb"]
        gp["ln_v_g"][l] = gpm["ln_v_g"][0]
        gp["ln_v_b"][l] = gpm["ln_v_b"][0]
        gp["w_s"][l] = gpm["w_s"]
        gp["b_s"][l] = gpm["b_s"][..., 0]
        gp["norm_g"][l] = jnp.concatenate([dg0, dg1, dg2], axis=0)
        dmods[l] = jnp.concatenate(dm0 + dm1 + dm2, axis=1)
    gw = {k: jnp.stack(v) for k, v in gw.items()}
    gp = {k: jnp.stack(v) for k, v in gp.items()}
    gp["final_g"] = d_final_g[0]
    return loss[0, 0], dxs[:s], jnp.stack(dmods), gw, gp


def _layer_params(prm, l):
    d = prm["norm_g"].shape[-1]
    return {
        "g": [prm["norm_g"][l, i].reshape(1, d) for i in range(3)],
        "b_gate": prm["b_gate"][l].reshape(1, -1),
        "rpb": prm["rpb"][l],
        "ln_v_g": prm["ln_v_g"][l].reshape(1, -1),
        "ln_v_b": prm["ln_v_b"][l].reshape(1, -1),
        "w_s": prm["w_s"][l],
        "b_s": prm["b_s"][l][..., None],
    }


SMALL = ("norm_g", "b_gate", "rpb", "ln_v_g", "ln_v_b", "w_s", "b_s", "final_g")
PACK_LANES = 1024


def _pack(parts):
    flat = jnp.concatenate([p.reshape(-1) for p in parts])
    rows = -(-flat.shape[0] // PACK_LANES)
    rows = -(-rows // 8) * 8
    return jnp.pad(flat, (0, rows * PACK_LANES - flat.shape[0])).reshape(rows, PACK_LANES)


def _unpack(flat, shapes):
    out, off = [], 0
    for shp in shapes:
        n = int(np.prod(shp))
        out.append(flat[..., off:off + n].reshape(*flat.shape[:-1], *shp))
        off += n
    return out


def kernel(x, c, ctx, c_ctx, w_ada, b_ada, norm_g, w_ff1_up, w_ff1_down, w_in, b_gate, rpb, ln_v_g, ln_v_b, w_s, b_s, w_pa, w_pb, w_o, w_ff2_up, w_ff2_down, final_g, loss_target, m_c_ctx, m_w_ada, m_b_ada, m_norm_g, m_w_ff1_up, m_w_ff1_down, m_w_in, m_b_gate, m_rpb, m_ln_v_g, m_ln_v_b, m_w_s, m_b_s, m_w_pa, m_w_pb, m_w_o, m_w_ff2_up, m_w_ff2_down, m_final_g, v_c_ctx, v_w_ada, v_b_ada, v_norm_g, v_w_ff1_up, v_w_ff1_down, v_w_in, v_b_gate, v_rpb, v_ln_v_g, v_ln_v_b, v_w_s, v_b_s, v_w_pa, v_w_pb, v_w_o, v_w_ff2_up, v_w_ff2_down, v_final_g):
    weights = dict(c_ctx=c_ctx, w_ada=w_ada, b_ada=b_ada, norm_g=norm_g, w_ff1_up=w_ff1_up, w_ff1_down=w_ff1_down, w_in=w_in,
                   b_gate=b_gate, rpb=rpb, ln_v_g=ln_v_g, ln_v_b=ln_v_b, w_s=w_s, b_s=b_s, w_pa=w_pa, w_pb=w_pb, w_o=w_o,
                   w_ff2_up=w_ff2_up, w_ff2_down=w_ff2_down, final_g=final_g)
    mom_m = dict(c_ctx=m_c_ctx, w_ada=m_w_ada, b_ada=m_b_ada, norm_g=m_norm_g, w_ff1_up=m_w_ff1_up, w_ff1_down=m_w_ff1_down,
                 w_in=m_w_in, b_gate=m_b_gate, rpb=m_rpb, ln_v_g=m_ln_v_g, ln_v_b=m_ln_v_b, w_s=m_w_s, b_s=m_b_s, w_pa=m_w_pa,
                 w_pb=m_w_pb, w_o=m_w_o, w_ff2_up=m_w_ff2_up, w_ff2_down=m_w_ff2_down, final_g=m_final_g)
    mom_v = dict(c_ctx=v_c_ctx, w_ada=v_w_ada, b_ada=v_b_ada, norm_g=v_norm_g, w_ff1_up=v_w_ff1_up, w_ff1_down=v_w_ff1_down,
                 w_in=v_w_in, b_gate=v_b_gate, rpb=v_rpb, ln_v_g=v_ln_v_g, ln_v_b=v_ln_v_b, w_s=v_w_s, b_s=v_b_s, w_pa=v_w_pa,
                 w_pb=v_w_pb, w_o=v_w_o, w_ff2_up=v_w_ff2_up, w_ff2_down=v_w_ff2_down, final_g=v_final_g)
    order = list(weights)
    mx, my, mc = _here()
    dev = 4 * mx + 2 * my + mc
    chip = 2 * mx + my
    depth, d, n_ada = w_ada.shape
    dq = d // 4

    c_all = _allgather8(jnp.pad(c, ((0, 7), (0, 0))), "gather_c")[:, 0, :]
    cond = jnp.concatenate([c_all, c_ctx[None, :], jnp.zeros((7, d), F32)], axis=0)
    b_shard = lax.dynamic_slice(b_ada, (0, chip * n_ada), (depth, n_ada))
    proj = [_ada_fwd(cond, w_ada[l], b_shard[l:l + 1], f"ada{l}") for l in range(depth)]
    silu_c = proj[0][1]
    mods_sh = _allgather8(jnp.concatenate([p[0] for p in proj], axis=0), "gather_mods")
    mods_all = jnp.transpose(mods_sh[0::2].reshape(4, depth, 16, n_ada), (1, 2, 0, 3)).reshape(depth, 16, N_MOD, d)
    mods = jnp.stack([lax.dynamic_index_in_dim(mods_all, dev, axis=1, keepdims=False), mods_all[:, 8]], axis=1)

    full = _gather_chips([weights[k].astype(MXU) for k in BIG], [SHARD_AXIS[k] for k in BIG], "gather_w")
    wts = dict(zip(BIG, full))
    prm = {k: weights[k] for k in SMALL if k != "norm_g"}
    norm_full = _allgather8(jnp.pad(norm_g.reshape(depth * 3, dq), ((0, 8 - depth * 3), (0, 0))), "gather_norm_g")
    prm["norm_g"] = jnp.transpose(norm_full[0::2, :depth * 3].reshape(4, depth, 3, dq), (1, 2, 0, 3)).reshape(depth, 3, d)

    loss, grad_x, dmods, gw, gp = _local_step(x[0], ctx[0], loss_target[0], mods, wts, prm)
    loss = lax.psum(loss, ("x", "y", "c"))

    small_shapes = [(depth, 2, N_MOD * d)] + [weights[k].shape if k != "norm_g" else (depth, 3, d) for k in SMALL]
    packed = _allgather8(_pack([dmods.reshape(depth, 2, N_MOD * d)] + [gp[k] for k in SMALL]), "gather_small")
    rows = packed.shape[1]
    total = _sum_pieces([packed[i] for i in range(8)], "sum_small").reshape(-1)
    sums = dict(zip(("dmods",) + SMALL, _unpack(total, small_shapes)))
    dmods_dev = _unpack(packed.reshape(8, rows * PACK_LANES), small_shapes[:1])[0]

    g_ada, cc_parts = [], []
    for l in range(depth):
        dm = jnp.concatenate([dmods_dev[:, l, 0], sums["dmods"][l, 1][None], jnp.zeros((7, N_MOD * d), F32)], axis=0)
        dm_sh = lax.dynamic_slice(dm, (0, chip * n_ada), (16, n_ada))
        g_ada.append(_mm(silu_c, dm_sh, "tn", F32, f"ada{l}_dw"))
        cc_parts.append(_mm(dm_sh, w_ada[l], "nt", F32, f"ada{l}_dc")[8:9])
    cc_all = _allgather8(jnp.pad(jnp.concatenate(cc_parts, axis=0), ((0, 8 - depth), (0, 0))), "gather_cctx")
    g_cctx = _cctx_grad(cc_all[0::2, :depth].reshape(4 * depth, d), c_ctx.reshape(1, d), "cctx_grad")

    axes = [SHARD_AXIS[k] for k in BIG]
    recv = _scatter_chips([gw[k] for k in BIG], axes, "scatter_gw")
    part = []
    for k, ax, r in zip(BIG, axes, recv):
        size = gw[k].shape[ax] // 4
        own = lax.dynamic_slice_in_dim(gw[k], chip * size, size, axis=ax)
        part.append(_sum_pieces([own, r[0], r[1], r[2]], "sum_" + k))
    other = _sibling_swap(part, "swap_gw")

    pieces = {k: [a, b] for k, a, b in zip(BIG, part, other)}
    pieces["w_ada"] = [jnp.stack(g_ada)]
    pieces["b_ada"] = [sums["dmods"][:, 0], sums["dmods"][:, 1]]
    pieces["c_ctx"] = [g_cctx[0]]
    for k in SMALL:
        pieces[k] = [sums[k]]
    pieces["norm_g"] = [lax.dynamic_slice_in_dim(sums["norm_g"], chip * dq, dq, axis=2)]
    res = {k: _adamw(weights[k], pieces[k], mom_m[k], mom_v[k], "adamw_" + k) for k in order}
    return (loss, grad_x[None], *[res[k][0] for k in order], *[res[k][1] for k in order],
            *[res[k][2] for k in order], *[res[k][3] for k in order])
```

```python
import numpy as np
import jax
import jax.numpy as jnp
from jax import lax
from jax.experimental import pallas as pl
from jax.experimental.pallas import tpu as pltpu

F32 = jnp.float32
MXU = jnp.bfloat16
EPS = 1e-6
GRID_W, HEADS, HEAD_DIM = 64, 8, 64
NA_WIDTH = SG_WIDTH = 512
WIN_H, WIN_W = 8, 16
SG_CHUNK, SG_GROUPS = 128, 4
N_MOD = 9
ROPE_THETA = 10000.0
Q_ROWS, K_ROWS = 4, 12
TQ, TK = Q_ROWS * GRID_W, K_ROWS * GRID_W
TM = 256
LANES = 128
NEG = -1e30
VMEM_LIMIT = 56 * 2 ** 20
ADAM_LR, ADAM_B1, ADAM_B2, ADAM_EPS, ADAM_WD, ADAM_STEP = 0.001, 0.9, 0.999, 1e-08, 0.01, 10
MESH = pl.DeviceIdType.MESH
BIG = ("w_ff1_up", "w_ff1_down", "w_in", "w_pa", "w_pb", "w_o", "w_ff2_up", "w_ff2_down")
SHARD_AXIS = {"w_ff1_up": 2, "w_ff1_down": 1, "w_in": 2, "w_pa": 2, "w_pb": 2, "w_o": 1, "w_ff2_up": 2, "w_ff2_down": 1}


def _call(body, *, name, grid, in_specs, out_specs, out_shape, scratch=()):
    return pl.pallas_call(
        body, name=name, grid=grid, in_specs=in_specs, out_specs=out_specs, out_shape=out_shape,
        scratch_shapes=list(scratch),
        compiler_params=pltpu.CompilerParams(dimension_semantics=("arbitrary",) * len(grid), vmem_limit_bytes=VMEM_LIMIT))


def _pick(n, prefs):
    for p in prefs:
        if n % p == 0:
            return p
    return n


def _rows(tm, n, col=0):
    return pl.BlockSpec((tm, n), lambda i: (i, col))


def _fixed(shape):
    return pl.BlockSpec(shape, lambda *_: (0,) * len(shape))


def _stream(d, nx):
    return pl.BlockSpec((1, 1, d), lambda i: (i // nx, 0, 0))


def _sds(shape, dtype):
    return jax.ShapeDtypeStruct(shape, dtype)


def _mm(a, b, mode, out_dtype, name):
    if mode == "tn":
        r, m = a.shape
        n = b.shape[1]
        tm = _pick(m, (1024, 1408, 704, 512, 256, 128))
        tn = _pick(n, (512, 256, 128))
        tr = _pick(r, (1280, 640, 512, 256, 128))

        def body(a_ref, b_ref, o_ref):
            @pl.when(pl.program_id(2) == 0)
            def _():
                o_ref[...] = jnp.zeros_like(o_ref)

            o_ref[...] += lax.dot_general(a_ref[...].astype(MXU), b_ref[...].astype(MXU), (((0,), (0,)), ((), ())),
                                          preferred_element_type=F32)

        return _call(body, name=name, grid=(m // tm, n // tn, r // tr),
                     in_specs=[pl.BlockSpec((tr, tm), lambda i, j, k: (k, i)), pl.BlockSpec((tr, tn), lambda i, j, k: (k, j))],
                     out_specs=pl.BlockSpec((tm, tn), lambda i, j, k: (i, j)), out_shape=_sds((m, n), F32))(a, b)
    m, k = a.shape
    n = b.shape[1] if mode == "nn" else b.shape[0]
    tm = _pick(m, (1280, 640, 512, 256, 128) if k <= 2816 else (640, 512, 256, 128))
    tn = _pick(n, (512, 1408, 256, 128))
    dims = (((1,), (0,)), ((), ())) if mode == "nn" else (((1,), (1,)), ((), ()))

    def body(a_ref, b_ref, o_ref):
        o_ref[...] = lax.dot_general(a_ref[...].astype(MXU), b_ref[...].astype(MXU), dims,
                                     preferred_element_type=F32).astype(o_ref.dtype)

    b_spec = pl.BlockSpec((k, tn), lambda i, j: (0, j)) if mode == "nn" else pl.BlockSpec((tn, k), lambda i, j: (j, 0))
    return _call(body, name=name, grid=(m // tm, n // tn), in_specs=[pl.BlockSpec((tm, k), lambda i, j: (i, 0)), b_spec],
                 out_specs=pl.BlockSpec((tm, tn), lambda i, j: (i, j)), out_shape=_sds((m, n), out_dtype))(a, b)


def _normmod(xs, g, mods, k_shift, k_scale, nx, name):
    t, d = xs.shape

    def body(x_ref, g_ref, m_ref, h_ref):
        x = x_ref[...]
        rstd = lax.rsqrt(jnp.mean(x * x, axis=-1, keepdims=True) + EPS)
        scale = m_ref[0, k_scale:k_scale + 1, :]
        shift = m_ref[0, k_shift:k_shift + 1, :]
        h_ref[...] = (x * rstd * g_ref[...] * (1.0 + scale) + shift).astype(h_ref.dtype)

    return _call(body, name=name, grid=(t // TM,),
                 in_specs=[_rows(TM, d), _fixed((1, d)), pl.BlockSpec((1, N_MOD, d), lambda i: (i // nx, 0, 0))],
                 out_specs=_rows(TM, d), out_shape=_sds((t, d), MXU))(xs, g, mods)


def _normmod_bwd(xs, dh, dres, g, mods, k_scale, nx, name):
    t, d = xs.shape

    def body(x_ref, dh_ref, dr_ref, g_ref, m_ref, dx_ref, dsh_ref, dsc_ref, dg_ref):
        i = pl.program_id(0)
        x = x_ref[...]
        dh = dh_ref[...].astype(F32)
        rstd = lax.rsqrt(jnp.mean(x * x, axis=-1, keepdims=True) + EPS)
        xhat = x * rstd
        gg = g_ref[...]

        @pl.when((i == 0) | (i == nx))
        def _():
            dsh_ref[...] = jnp.zeros_like(dsh_ref)
            dsc_ref[...] = jnp.zeros_like(dsc_ref)

        @pl.when(i == 0)
        def _():
            dg_ref[...] = jnp.zeros_like(dg_ref)

        dsh_ref[0] += jnp.sum(dh, axis=0, keepdims=True)
        dsc_ref[0] += jnp.sum(dh * (xhat * gg), axis=0, keepdims=True)
        dy = dh * (1.0 + m_ref[0, k_scale:k_scale + 1, :])
        dg_ref[...] += jnp.sum(dy * xhat, axis=0, keepdims=True)
        dxh = dy * gg
        dx_ref[...] = dr_ref[...] + rstd * (dxh - xhat * jnp.mean(dxh * xhat, axis=-1, keepdims=True))

    return _call(body, name=name, grid=(t // TM,),
                 in_specs=[_rows(TM, d), _rows(TM, d), _rows(TM, d), _fixed((1, d)),
                           pl.BlockSpec((1, N_MOD, d), lambda i: (i // nx, 0, 0))],
                 out_specs=[_rows(TM, d), _stream(d, nx), _stream(d, nx), _fixed((1, d))],
                 out_shape=[_sds((t, d), F32), _sds((2, 1, d), F32), _sds((2, 1, d), F32), _sds((1, d), F32)])(
                     xs, dh, dres, g, mods)


def _resid(xs, y, mods, k_gate, coef, nx, name):
    t, d = xs.shape

    def body(x_ref, y_ref, m_ref, o_ref):
        o_ref[...] = x_ref[...] + (coef * m_ref[0, k_gate:k_gate + 1, :]) * y_ref[...]

    return _call(body, name=name, grid=(t // TM,),
                 in_specs=[_rows(TM, d), _rows(TM, d), pl.BlockSpec((1, N_MOD, d), lambda i: (i // nx, 0, 0))],
                 out_specs=_rows(TM, d), out_shape=_sds((t, d), F32))(xs, y, mods)


def _resid_bwd(dxn, y, mods, k_gate, coef, nx, name):
    t, d = dxn.shape

    def body(dx_ref, y_ref, m_ref, dy_ref, dgt_ref):
        i = pl.program_id(0)

        @pl.when((i == 0) | (i == nx))
        def _():
            dgt_ref[...] = jnp.zeros_like(dgt_ref)

        dx = dx_ref[...]
        dy_ref[...] = ((coef * m_ref[0, k_gate:k_gate + 1, :]) * dx).astype(dy_ref.dtype)
        dgt_ref[0] += jnp.sum(coef * y_ref[...] * dx, axis=0, keepdims=True)

    return _call(body, name=name, grid=(t // TM,),
                 in_specs=[_rows(TM, d), _rows(TM, d), pl.BlockSpec((1, N_MOD, d), lambda i: (i // nx, 0, 0))],
                 out_specs=[_rows(TM, d), _stream(d, nx)],
                 out_shape=[_sds((t, d), MXU), _sds((2, 1, d), F32)])(dxn, y, mods)


def _swiglu(up, name):
    t, f2 = up.shape
    f = f2 // 2

    def body(u_ref, o_ref):
        a = u_ref[:, :f]
        o_ref[...] = (a * jax.nn.sigmoid(a) * u_ref[:, f:]).astype(o_ref.dtype)

    return _call(body, name=name, grid=(t // TM,), in_specs=[_rows(TM, f2)], out_specs=_rows(TM, f),
                 out_shape=_sds((t, f), MXU))(up)


def _swiglu_bwd(up, dact, name):
    t, f2 = up.shape
    f = f2 // 2

    def body(u_ref, d_ref, o_ref):
        a = u_ref[:, :f]
        b = u_ref[:, f:]
        dact = d_ref[...]
        sg = jax.nn.sigmoid(a)
        o_ref[:, :f] = (dact * b * sg * (1.0 + a * (1.0 - sg))).astype(o_ref.dtype)
        o_ref[:, f:] = (dact * a * sg).astype(o_ref.dtype)

    return _call(body, name=name, grid=(t // TM,), in_specs=[_rows(TM, f2), _rows(TM, f)], out_specs=_rows(TM, f2),
                 out_shape=_sds((t, f2), MXU))(up, dact)


def _rope_tables(s, ctx_len):
    n_freq = HEAD_DIM // 4
    tok = jnp.arange(s)
    freqs = ROPE_THETA ** (-jnp.arange(n_freq, dtype=F32) / n_freq)
    ang = jnp.concatenate([(tok // GRID_W).astype(F32)[:, None] * freqs, (tok % GRID_W).astype(F32)[:, None] * freqs], axis=-1)
    cos = jnp.repeat(jnp.cos(ang), 2, axis=-1)
    sin = jnp.repeat(jnp.sin(ang), 2, axis=-1) * jnp.tile(jnp.array([-1.0, 1.0], F32), HEAD_DIM // 2)
    cos = jnp.concatenate([jnp.tile(cos, (1, HEADS)), jnp.ones((ctx_len, NA_WIDTH), F32)], axis=0)
    sin = jnp.concatenate([jnp.tile(sin, (1, HEADS)), jnp.zeros((ctx_len, NA_WIDTH), F32)], axis=0)
    return cos, sin


def _swap_pairs(x):
    n = x.shape[-1]
    lane = lax.broadcasted_iota(jnp.int32, x.shape, 1)
    return jnp.where(lane % 2 == 0, pltpu.roll(x, n - 1, 1), pltpu.roll(x, 1, 1))


def _rope(p, cos, sin, name):
    t = p.shape[0]
    w = NA_WIDTH

    def body(q_ref, k_ref, v_ref, c_ref, s_ref, qo_ref, ko_ref, vo_ref):
        c, s = c_ref[...], s_ref[...]
        q, k = q_ref[...], k_ref[...]
        qo_ref[...] = (q * c + _swap_pairs(q) * s).astype(qo_ref.dtype)
        ko_ref[...] = (k * c + _swap_pairs(k) * s).astype(ko_ref.dtype)
        vo_ref[...] = v_ref[...].astype(vo_ref.dtype)

    return _call(body, name=name, grid=(t // TM,),
                 in_specs=[_rows(TM, w, 0), _rows(TM, w, 1), _rows(TM, w, 2), _rows(TM, w), _rows(TM, w)],
                 out_specs=[_rows(TM, w)] * 3, out_shape=[_sds((t, w), MXU)] * 3)(p, p, p, cos, sin)


def _rope_bwd(dq, dk, dv, cos, sin, name):
    t = dq.shape[0]

    def body(dq_ref, dk_ref, dv_ref, c_ref, s_ref, qo_ref, ko_ref, vo_ref):
        c, s = c_ref[...], s_ref[...]
        a, b = dq_ref[...], dk_ref[0]
        qo_ref[...] = (a * c + _swap_pairs(a * s)).astype(qo_ref.dtype)
        ko_ref[...] = (b * c + _swap_pairs(b * s)).astype(ko_ref.dtype)
        vo_ref[...] = dv_ref[0].astype(vo_ref.dtype)

    tile = pl.BlockSpec((TM, LANES), lambda i, j: (i, j))
    pair = pl.BlockSpec((1, TM, LANES), lambda i, j: (j, i, 0))
    return _call(body, name=name, grid=(t // TM, NA_WIDTH // LANES), in_specs=[tile, pair, pair, tile, tile],
                 out_specs=[tile] * 3, out_shape=[_sds((t, NA_WIDTH), MXU)] * 3)(dq, dk, dv, cos, sin)


def _na_geometry(r_grid):
    rows = []
    for r0, ks in ((0, 0), (Q_ROWS, 0), (r_grid - Q_ROWS, r_grid - K_ROWS)):
        dr = np.zeros((Q_ROWS, K_ROWS), np.int32)
        vr = np.zeros((Q_ROWS, K_ROWS), bool)
        for a in range(Q_ROWS):
            r = r0 + a
            rs = min(max(r - WIN_H // 2, 0), r_grid - WIN_H)
            for i in range(K_ROWS):
                kr = ks + i
                vr[a, i] = rs <= kr <= rs + WIN_H - 1
                dr[a, i] = kr - r + WIN_H - 1
        rows.append((dr, vr))
    c = np.arange(GRID_W)
    cs = np.clip(c - WIN_W // 2, 0, GRID_W - WIN_W)
    kc = np.arange(GRID_W)
    vc = (kc[None, :] >= cs[:, None]) & (kc[None, :] <= cs[:, None] + WIN_W - 1)
    dc = kc[None, :] - c[:, None] + WIN_W - 1
    return rows, dc, vc


def _bias_table(rpb, r_grid):
    rows, _, vc = _na_geometry(r_grid)
    n_dr, n_dc, skew, off = 2 * WIN_H - 1, 2 * WIN_W - 1, 2 * GRID_W - 1, GRID_W - WIN_W
    u = jnp.pad(rpb, ((0, 0), (0, 0), (off, skew - off - n_dc)))
    toep = jnp.tile(u, (1, 1, GRID_W + 1))[:, :, :2 * GRID_W * GRID_W].reshape(HEADS, n_dr, GRID_W, 2 * GRID_W)
    toep = jnp.pad(toep[:, :, ::-1, :GRID_W], ((0, 0), (Q_ROWS, Q_ROWS), (0, 0), (0, 0)))
    tabs = []
    for dr, vr in rows:
        per_row = []
        for a in range(Q_ROWS):
            lo = int(dr[a, 0]) + Q_ROWS
            valid = vr[a][:, None, None] & vc[None, :, :]
            per_row.append(jnp.where(valid[None], toep[:, lo:lo + K_ROWS], NEG))
        t = jnp.stack(per_row, axis=1)
        tabs.append(jnp.transpose(t, (0, 1, 3, 2, 4)).reshape(HEADS, TQ, TK))
    tabs.append(jnp.full((HEADS, TQ, TK), NEG, F32))
    return jnp.stack(tabs)


def _variant(g, ngx):
    return jnp.where(g == 0, 0, jnp.where(g >= ngx, 3, jnp.where(g == ngx - 1, 2, 1)))


def _key_start(g, r_grid):
    return pl.multiple_of(jnp.clip(g * Q_ROWS - WIN_H // 2, 0, r_grid - K_ROWS) * GRID_W, TQ)


def _nt(a, b):
    return lax.dot_general(a, b, (((1,), (1,)), ((), ())), preferred_element_type=F32)


def _tn(a, b):
    return lax.dot_general(a, b, (((0,), (0,)), ((), ())), preferred_element_type=F32)


def _nn(a, b):
    return jnp.dot(a, b, preferred_element_type=F32)


def _head_mask(h):
    lane = lax.broadcasted_iota(jnp.int32, (1, LANES), 1)
    return ((lane >= HEAD_DIM * h) & (lane < HEAD_DIM * (h + 1))).astype(F32)


def _softmax_parts(qm, knb, kcx, bias):
    s_nb = _nt(qm, knb) + bias
    s_cx = _nt(qm, kcx)
    m = jnp.maximum(jnp.max(s_nb, axis=-1, keepdims=True), jnp.max(s_cx, axis=-1, keepdims=True))
    e_nb = jnp.exp(s_nb - m)
    e_cx = jnp.exp(s_cx - m)
    inv = 1.0 / (jnp.sum(e_nb, axis=-1, keepdims=True) + jnp.sum(e_cx, axis=-1, keepdims=True))
    return e_nb * inv, e_cx * inv


def _na_specs(t, ngx):
    q_spec = pl.BlockSpec((TQ, LANES), lambda hp, g: (g, hp))
    kv_spec = pl.BlockSpec((t, LANES), lambda hp, g: (0, hp))
    b_spec = pl.BlockSpec((1, 2, TQ, TK), lambda hp, g: (_variant(g, ngx), hp, 0, 0))
    return q_spec, kv_spec, b_spec


def _na_fwd(q, k, v, bias, s, name):
    t = q.shape[0]
    ctx_len = t - s
    r_grid = s // GRID_W
    q_spec, kv_spec, b_spec = _na_specs(t, s // TQ)

    def body(q_ref, k_ref, v_ref, b_ref, o_ref):
        start = _key_start(pl.program_id(1), r_grid)
        qf = q_ref[...].astype(F32) * (HEAD_DIM ** -0.5)
        knb, vnb = k_ref[pl.ds(start, TK), :], v_ref[pl.ds(start, TK), :]
        kcx, vcx = k_ref[pl.ds(s, ctx_len), :], v_ref[pl.ds(s, ctx_len), :]
        acc = jnp.zeros((TQ, LANES), F32)
        for h in range(2):
            mask = _head_mask(h)
            p_nb, p_cx = _softmax_parts((qf * mask).astype(MXU), knb, kcx, b_ref[0, h])
            acc += (_nn(p_nb.astype(MXU), vnb) + _nn(p_cx.astype(MXU), vcx)) * mask
        o_ref[...] = acc.astype(o_ref.dtype)

    return _call(body, name=name, grid=(NA_WIDTH // LANES, t // TQ), in_specs=[q_spec, kv_spec, kv_spec, b_spec],
                 out_specs=q_spec, out_shape=_sds((t, NA_WIDTH), MXU))(q, k, v, bias)


def _na_bwd(q, k, v, do, bias, s, name):
    t = q.shape[0]
    ctx_len = t - s
    r_grid = s // GRID_W
    ng, ngx = t // TQ, s // TQ
    q_spec, kv_spec, b_spec = _na_specs(t, ngx)

    def body(q_ref, k_ref, v_ref, do_ref, b_ref, dq_ref, dk_hbm, dv_hbm, db_ref, dk_acc, dv_acc):
        hp, g = pl.program_id(0), pl.program_id(1)
        start = _key_start(g, r_grid)

        @pl.when(g == 0)
        def _():
            dk_acc[...] = jnp.zeros_like(dk_acc)
            dv_acc[...] = jnp.zeros_like(dv_acc)

        @pl.when((g == 0) | (g == 1) | (g == ngx - 1) | (g == ngx))
        def _():
            db_ref[...] = jnp.zeros_like(db_ref)

        qf = q_ref[...].astype(F32) * (HEAD_DIM ** -0.5)
        do = do_ref[...].astype(F32)
        knb, vnb = k_ref[pl.ds(start, TK), :], v_ref[pl.ds(start, TK), :]
        kcx, vcx = k_ref[pl.ds(s, ctx_len), :], v_ref[pl.ds(s, ctx_len), :]
        dq = jnp.zeros((TQ, LANES), F32)
        dk_nb = jnp.zeros((TK, LANES), F32)
        dv_nb = jnp.zeros((TK, LANES), F32)
        dk_cx = jnp.zeros((ctx_len, LANES), F32)
        dv_cx = jnp.zeros((ctx_len, LANES), F32)
        for h in range(2):
            mask = _head_mask(h)
            qm = (qf * mask).astype(MXU)
            dom = (do * mask).astype(MXU)
            p_nb, p_cx = _softmax_parts(qm, knb, kcx, b_ref[0, h])
            dp_nb = _nt(dom, vnb)
            dp_cx = _nt(dom, vcx)
            delta = jnp.sum(p_nb * dp_nb, axis=-1, keepdims=True) + jnp.sum(p_cx * dp_cx, axis=-1, keepdims=True)
            ds_nb = p_nb * (dp_nb - delta)
            ds_cx = p_cx * (dp_cx - delta)
            db_ref[0, h] += ds_nb
            ds_nb, ds_cx = ds_nb.astype(MXU), ds_cx.astype(MXU)
            dq += (_nn(ds_nb, knb) + _nn(ds_cx, kcx)) * (mask * (HEAD_DIM ** -0.5))
            dk_nb += _tn(ds_nb, qm)
            dk_cx += _tn(ds_cx, qm)
            dv_nb += _tn(p_nb.astype(MXU), dom)
            dv_cx += _tn(p_cx.astype(MXU), dom)
        dq_ref[...] = dq
        dk_acc[pl.ds(start, TK), :] += dk_nb
        dv_acc[pl.ds(start, TK), :] += dv_nb
        dk_acc[pl.ds(s, ctx_len), :] += dk_cx
        dv_acc[pl.ds(s, ctx_len), :] += dv_cx

        @pl.when(g == ng - 1)
        def _():
            pltpu.sync_copy(dk_acc, dk_hbm.at[hp])
            pltpu.sync_copy(dv_acc, dv_hbm.at[hp])

    n_pairs = NA_WIDTH // LANES
    hbm = pl.BlockSpec(memory_space=pl.ANY)
    return _call(body, name=name, grid=(n_pairs, ng), in_specs=[q_spec, kv_spec, kv_spec, q_spec, b_spec],
                 out_specs=[q_spec, hbm, hbm, b_spec],
                 out_shape=[_sds((t, NA_WIDTH), F32), _sds((n_pairs, t, LANES), F32), _sds((n_pairs, t, LANES), F32),
                            _sds((4, HEADS, TQ, TK), F32)],
                 scratch=[pltpu.VMEM((t, LANES), F32), pltpu.VMEM((t, LANES), F32)])(q, k, v, do, bias)


def _rpb_grad(dbias, r_grid, name):
    rows, _, _ = _na_geometry(r_grid)
    n_blk = 3 * Q_ROWS * K_ROWS
    z = dbias[:3].reshape(3, HEADS, Q_ROWS, GRID_W, K_ROWS, GRID_W)
    z = jnp.transpose(z, (1, 0, 2, 4, 3, 5)).reshape(HEADS, n_blk, GRID_W, GRID_W)
    z = jnp.pad(z[:, :, ::-1, :], ((0, 0), (0, 0), (0, 0), (0, GRID_W))).reshape(HEADS, n_blk, 2 * GRID_W * GRID_W)
    skew = 2 * GRID_W - 1
    z = jnp.pad(z, ((0, 0), (0, 0), (0, (GRID_W + 1) * skew - 2 * GRID_W * GRID_W))).reshape(HEADS, n_blk, GRID_W + 1, skew)
    z = jnp.pad(z, ((0, 0), (0, 0), (0, 72 - (GRID_W + 1)), (0, 1)))
    members = [[] for _ in range(2 * WIN_H - 1)]
    for vi, (dr, vr) in enumerate(rows):
        for a in range(Q_ROWS):
            for i in range(K_ROWS):
                if vr[a, i]:
                    members[dr[a, i]].append((vi * Q_ROWS + a) * K_ROWS + i)

    def body(z_ref, o_ref):
        zs = jnp.sum(z_ref[0], axis=1)
        out = []
        for mem in members:
            acc = jnp.zeros((1, LANES), F32)
            for j in mem:
                acc = acc + zs[j:j + 1, :]
            out.append(acc)
        out.append(jnp.zeros((1, LANES), F32))
        o_ref[0] = jnp.concatenate(out, axis=0)

    o = _call(body, name=name, grid=(HEADS,), in_specs=[pl.BlockSpec((1, n_blk, 72, LANES), lambda h: (h, 0, 0, 0))],
              out_specs=pl.BlockSpec((1, 16, LANES), lambda h: (h, 0, 0)), out_shape=_sds((HEADS, 16, LANES), F32))(z)
    off = GRID_W - 1 - (WIN_W - 1)
    return o[:, :2 * WIN_H - 1, off:off + 2 * WIN_W - 1]


_GELU_K, _GELU_C = 0.7978845608028654, 0.044715


def _gelu(x):
    return 0.5 * x * (1.0 + jnp.tanh(_GELU_K * (x + _GELU_C * x * x * x)))


def _gelu_grad(x):
    th = jnp.tanh(_GELU_K * (x + _GELU_C * x * x * x))
    return 0.5 * (1.0 + th) + 0.5 * x * (1.0 - th * th) * (_GELU_K * (1.0 + 3.0 * _GELU_C * x * x))


def _ln_stats(v):
    mu = jnp.mean(v, axis=-1, keepdims=True)
    vc = v - mu
    rstd = lax.rsqrt(jnp.mean(vc * vc, axis=-1, keepdims=True) + EPS)
    return vc * rstd, rstd


def _gmlp(p, ln_g, ln_b, w_s, b_s, name):
    t = p.shape[0]
    w = SG_WIDTH
    cw = w // SG_GROUPS

    def body(u_ref, v_ref, g_ref, b_ref, ws_ref, bs_ref, o_ref):
        xhat, _ = _ln_stats(_gelu(v_ref[...]))
        vn = (xhat * g_ref[...] + b_ref[...]).astype(MXU)
        ug = _gelu(u_ref[...])
        for ci in range(TM // SG_CHUNK):
            rs = slice(ci * SG_CHUNK, (ci + 1) * SG_CHUNK)
            for gi in range(SG_GROUPS):
                cs = slice(gi * cw, (gi + 1) * cw)
                sg = _nn(ws_ref[gi].astype(MXU), vn[rs, cs]) + bs_ref[gi]
                o_ref[rs, cs] = (ug[rs, cs] * sg).astype(o_ref.dtype)

    return _call(body, name=name, grid=(t // TM,),
                 in_specs=[_rows(TM, w, 3), _rows(TM, w, 4), _fixed((1, w)), _fixed((1, w)),
                           _fixed((SG_GROUPS, SG_CHUNK, SG_CHUNK)), _fixed((SG_GROUPS, SG_CHUNK, 1))],
                 out_specs=_rows(TM, w), out_shape=_sds((t, w), MXU))(p, p, ln_g, ln_b, w_s, b_s)


def _gmlp_bwd(p, dob, ln_g, ln_b, w_s, b_s, name):
    t = p.shape[0]
    w = SG_WIDTH
    cw = w // SG_GROUPS

    def body(u_ref, v_ref, do_ref, g_ref, b_ref, ws_ref, bs_ref, du_ref, dv_ref, dws_ref, dbs_ref, dg_ref, db_ref, dvn_ref):
        @pl.when(pl.program_id(0) == 0)
        def _():
            dws_ref[...] = jnp.zeros_like(dws_ref)
            dbs_ref[...] = jnp.zeros_like(dbs_ref)
            dg_ref[...] = jnp.zeros_like(dg_ref)
            db_ref[...] = jnp.zeros_like(db_ref)

        u, v = u_ref[...], v_ref[...]
        xhat, rstd = _ln_stats(_gelu(v))
        vn = (xhat * g_ref[...] + b_ref[...]).astype(MXU)
        ug = _gelu(u)
        dob = do_ref[...]
        for ci in range(TM // SG_CHUNK):
            rs = slice(ci * SG_CHUNK, (ci + 1) * SG_CHUNK)
            for gi in range(SG_GROUPS):
                cs = slice(gi * cw, (gi + 1) * cw)
                wsg = ws_ref[gi].astype(MXU)
                sg = _nn(wsg, vn[rs, cs]) + bs_ref[gi]
                du_ref[rs, cs] = (dob[rs, cs] * sg * _gelu_grad(u[rs, cs])).astype(du_ref.dtype)
                ds = dob[rs, cs] * ug[rs, cs]
                dbs_ref[gi] += jnp.sum(ds, axis=-1, keepdims=True)
                ds = ds.astype(MXU)
                dws_ref[gi] += _nt(ds, vn[rs, cs])
                dvn_ref[rs, cs] = _tn(wsg, ds)
        dvn = dvn_ref[...]
        dg_ref[...] += jnp.sum(dvn * xhat, axis=0, keepdims=True)
        db_ref[...] += jnp.sum(dvn, axis=0, keepdims=True)
        dxh = dvn * g_ref[...]
        dvg = rstd * (dxh - jnp.mean(dxh, axis=-1, keepdims=True) - xhat * jnp.mean(dxh * xhat, axis=-1, keepdims=True))
        dv_ref[...] = (dvg * _gelu_grad(v)).astype(dv_ref.dtype)

    return _call(body, name=name, grid=(t // TM,),
                 in_specs=[_rows(TM, w, 3), _rows(TM, w, 4), _rows(TM, w), _fixed((1, w)), _fixed((1, w)),
                           _fixed((SG_GROUPS, SG_CHUNK, SG_CHUNK)), _fixed((SG_GROUPS, SG_CHUNK, 1))],
                 out_specs=[_rows(TM, w), _rows(TM, w), _fixed((SG_GROUPS, SG_CHUNK, SG_CHUNK)),
                            _fixed((SG_GROUPS, SG_CHUNK, 1)), _fixed((1, w)), _fixed((1, w))],
                 out_shape=[_sds((t, w), MXU), _sds((t, w), MXU), _sds((SG_GROUPS, SG_CHUNK, SG_CHUNK), F32),
                            _sds((SG_GROUPS, SG_CHUNK, 1), F32), _sds((1, w), F32), _sds((1, w), F32)],
                 scratch=[pltpu.VMEM((TM, w), F32)])(p, p, dob, ln_g, ln_b, w_s, b_s)


def _merge(pa, pb, p, b_gate, name):
    t, d = pa.shape
    hw = NA_WIDTH
    nh = d // hw
    c0 = (NA_WIDTH * 3 + SG_WIDTH * 2) // hw

    def body(pa_ref, pb_ref, la_ref, lb_ref, ba_ref, bb_ref, o_ref):
        ga = jax.nn.sigmoid(la_ref[...] + ba_ref[...])
        gb = jax.nn.sigmoid(lb_ref[...] + bb_ref[...])
        o_ref[...] = (ga * pa_ref[...] + gb * pb_ref[...]).astype(o_ref.dtype)

    tile = pl.BlockSpec((TM, hw), lambda i, j: (i, j))
    return _call(body, name=name, grid=(t // TM, nh),
                 in_specs=[tile, tile, pl.BlockSpec((TM, hw), lambda i, j: (i, c0 + j)),
                           pl.BlockSpec((TM, hw), lambda i, j: (i, c0 + nh + j)),
                           pl.BlockSpec((1, hw), lambda i, j: (0, j)), pl.BlockSpec((1, hw), lambda i, j: (0, nh + j))],
                 out_specs=tile, out_shape=_sds((t, d), MXU))(pa, pb, p, p, b_gate, b_gate)


def _merge_bwd(dmg, pa, pb, p, b_gate, name):
    t, d = pa.shape
    hw = NA_WIDTH
    nh = d // hw
    c0 = (NA_WIDTH * 3 + SG_WIDTH * 2) // hw

    def body(dm_ref, pa_ref, pb_ref, la_ref, lb_ref, ba_ref, bb_ref, dpa_ref, dpb_ref, dla_ref, dlb_ref, dba_ref, dbb_ref):
        @pl.when(pl.program_id(1) == 0)
        def _():
            dba_ref[...] = jnp.zeros_like(dba_ref)
            dbb_ref[...] = jnp.zeros_like(dbb_ref)

        dm = dm_ref[...]
        ga = jax.nn.sigmoid(la_ref[...] + ba_ref[...])
        gb = jax.nn.sigmoid(lb_ref[...] + bb_ref[...])
        dpa_ref[...] = (dm * ga).astype(dpa_ref.dtype)
        dpb_ref[...] = (dm * gb).astype(dpb_ref.dtype)
        dla = dm * pa_ref[...] * ga * (1.0 - ga)
        dlb = dm * pb_ref[...] * gb * (1.0 - gb)
        dla_ref[...] = dla.astype(dla_ref.dtype)
        dlb_ref[...] = dlb.astype(dlb_ref.dtype)
        dba_ref[...] += jnp.sum(dla, axis=0, keepdims=True)
        dbb_ref[...] += jnp.sum(dlb, axis=0, keepdims=True)

    tile = pl.BlockSpec((TM, hw), lambda j, i: (i, j))
    bias_a = pl.BlockSpec((1, hw), lambda j, i: (0, j))
    bias_b = pl.BlockSpec((1, hw), lambda j, i: (0, nh + j))
    return _call(body, name=name, grid=(nh, t // TM),
                 in_specs=[tile, tile, tile, pl.BlockSpec((TM, hw), lambda j, i: (i, c0 + j)),
                           pl.BlockSpec((TM, hw), lambda j, i: (i, c0 + nh + j)), bias_a, bias_b],
                 out_specs=[tile, tile, tile, tile, bias_a, bias_a],
                 out_shape=[_sds((t, d), MXU)] * 4 + [_sds((1, d), F32)] * 2)(dmg, pa, pb, p, p, b_gate, b_gate)


def _final(xs, tgt, g, name):
    t, d = xs.shape
    nx = tgt.shape[0] // TM

    def body(x_ref, t_ref, g_ref, l_ref, dx_ref, dg_ref):
        i = pl.program_id(0)

        @pl.when(i == 0)
        def _():
            l_ref[...] = jnp.zeros_like(l_ref)
            dg_ref[...] = jnp.zeros_like(dg_ref)

        @pl.when(i < nx)
        def _():
            x = x_ref[...]
            rstd = lax.rsqrt(jnp.mean(x * x, axis=-1, keepdims=True) + EPS)
            xhat = x * rstd
            err = xhat * g_ref[...] - t_ref[...]
            l_ref[...] += 0.5 * jnp.sum(jnp.mean(err * err, axis=-1, keepdims=True))
            dy = err * (1.0 / d)
            dg_ref[...] += jnp.sum(dy * xhat, axis=0, keepdims=True)
            dxh = dy * g_ref[...]
            dx_ref[...] = rstd * (dxh - xhat * jnp.mean(dxh * xhat, axis=-1, keepdims=True))

        @pl.when(i >= nx)
        def _():
            dx_ref[...] = jnp.zeros_like(dx_ref)

    return _call(body, name=name, grid=(t // TM,),
                 in_specs=[_rows(TM, d), pl.BlockSpec((TM, d), lambda i: (jnp.minimum(i, nx - 1), 0)), _fixed((1, d))],
                 out_specs=[_fixed((1, LANES)), _rows(TM, d), _fixed((1, d))],
                 out_shape=[_sds((1, LANES), F32), _sds((t, d), F32), _sds((1, d), F32)])(xs, tgt, g)


def _view2d(a):
    return a.reshape(1, -1) if a.ndim == 1 else a.reshape(-1, a.shape[-1])


def _ew(fn, arrays, n_out, name):
    shape = arrays[0].shape
    views = [_view2d(a) for a in arrays]
    r, c = views[0].shape
    tr = r
    for cand in (1024, 512, 256, 128, 64, 32, 16, 8):
        if r % cand == 0 and cand * c * 4 <= 2 ** 20:
            tr = cand
            break

    def body(*refs):
        outs = fn(*[ref[...] for ref in refs[:len(views)]])
        for ref, o in zip(refs[len(views):], outs):
            ref[...] = o

    res = _call(body, name=name, grid=(r // tr,), in_specs=[_rows(tr, c)] * len(views), out_specs=[_rows(tr, c)] * n_out,
                out_shape=[_sds((r, c), F32)] * n_out)(*views)
    return [o.reshape(shape) for o in res]


def _sum_pieces(pieces, name):
    def fn(*vals):
        acc = vals[0]
        for v in vals[1:]:
            acc = acc + v
        return (acc,)

    return _ew(fn, pieces, 1, name)[0]


def _adamw(w, g_pieces, m, v, name):
    n_g = len(g_pieces)

    def fn(w_, *rest):
        g = rest[0]
        for piece in rest[1:n_g]:
            g = g + piece
        m_, v_ = rest[n_g], rest[n_g + 1]
        m2 = ADAM_B1 * m_ + (1.0 - ADAM_B1) * g
        v2 = ADAM_B2 * v_ + (1.0 - ADAM_B2) * (g * g)
        m_hat = m2 / (1.0 - ADAM_B1 ** ADAM_STEP)
        v_hat = v2 / (1.0 - ADAM_B2 ** ADAM_STEP)
        delta = -ADAM_LR * (m_hat / (jnp.sqrt(v_hat) + ADAM_EPS) + ADAM_WD * w_)
        return g, delta, m2, v2

    return _ew(fn, [w, *g_pieces, m, v], 4, name)


def _ada_fwd(cond, w, b, name):
    r, d = cond.shape
    n = w.shape[1]
    tn = _pick(n, (1152, 768, 512, 384, 256, 128))

    def body(c_ref, w_ref, b_ref, o_ref, s_ref):
        c = c_ref[...]
        sc = c * jax.nn.sigmoid(c)
        s_ref[...] = sc
        o_ref[...] = _nn(sc.astype(MXU), w_ref[...].astype(MXU)) + b_ref[...]

    return _call(body, name=name, grid=(n // tn,),
                 in_specs=[_fixed((r, d)), pl.BlockSpec((d, tn), lambda j: (0, j)), pl.BlockSpec((1, tn), lambda j: (0, j))],
                 out_specs=[pl.BlockSpec((r, tn), lambda j: (0, j)), _fixed((r, d))],
                 out_shape=[_sds((r, n), F32), _sds((r, d), F32)])(cond, w, b)


def _cctx_grad(parts, c_ctx, name):
    n, d = parts.shape

    def body(p_ref, c_ref, o_ref):
        c = c_ref[...]
        sg = jax.nn.sigmoid(c)
        acc = p_ref[0:1, :]
        for j in range(1, n):
            acc = acc + p_ref[j:j + 1, :]
        o_ref[...] = acc * (sg * (1.0 + c * (1.0 - sg)))

    return _call(body, name=name, grid=(1,), in_specs=[_fixed((n, d)), _fixed((1, d))], out_specs=_fixed((1, d)),
                 out_shape=_sds((1, d), F32))(parts, c_ctx)


def _here():
    return lax.axis_index("x"), lax.axis_index("y"), lax.axis_index("c")


def _flip(v, bit):
    return 1 - v if bit else v


def _allgather8(xb, name):
    r, n = xb.shape

    def body(x_ref, out_ref, send_sems, recv_sems, local_sem):
        x, y, c = _here()
        me = 4 * x + 2 * y + c
        local = pltpu.make_async_copy(x_ref, out_ref.at[me], local_sem)
        local.start()
        sends = []
        for k in range(1, 8):
            peer = (_flip(x, k & 4), _flip(y, k & 2), _flip(c, k & 1))
            cp = pltpu.make_async_remote_copy(src_ref=x_ref, dst_ref=out_ref.at[me], send_sem=send_sems.at[k - 1],
                                              recv_sem=recv_sems.at[k - 1], device_id=peer, device_id_type=MESH)
            cp.start()
            sends.append(cp)
        for k in range(1, 8):
            peer = (_flip(x, k & 4), _flip(y, k & 2), _flip(c, k & 1))
            src = 4 * peer[0] + 2 * peer[1] + peer[2]
            pltpu.make_async_remote_copy(src_ref=x_ref, dst_ref=out_ref.at[src], send_sem=send_sems.at[k - 1],
                                         recv_sem=recv_sems.at[k - 1], device_id=peer, device_id_type=MESH).wait_recv()
        for cp in sends:
            cp.wait_send()
        local.wait()

    vmem = pl.BlockSpec(memory_space=pltpu.VMEM)
    return pl.pallas_call(
        body, name=name, out_shape=_sds((8, r, n), xb.dtype), in_specs=[vmem], out_specs=vmem,
        scratch_shapes=[pltpu.SemaphoreType.DMA((7,)), pltpu.SemaphoreType.DMA((7,)), pltpu.SemaphoreType.DMA(())],
        compiler_params=pltpu.CompilerParams(vmem_limit_bytes=VMEM_LIMIT))(xb)


def _shard_of(ref, axis, j, size):
    sl = pl.ds(j * size, size)
    return ref.at[:, sl, :] if axis == 1 else ref.at[:, :, sl]


def _gather_chips(shards, axes, name):
    n = len(shards)
    fulls = []
    for a, ax in zip(shards, axes):
        shp = list(a.shape)
        shp[ax] *= 4
        fulls.append(_sds(tuple(shp), a.dtype))

    def body(*refs):
        ins, outs = refs[:n], refs[n:2 * n]
        send_sems, recv_sems, local_sems = refs[2 * n:]
        x, y, c = _here()
        local, sends = [], []
        for a in range(n):
            size = ins[a].shape[axes[a]]
            cp = pltpu.make_async_copy(ins[a], _shard_of(outs[a], axes[a], 2 * x + y, size), local_sems.at[a])
            cp.start()
            local.append(cp)
            for k in range(1, 4):
                peer = (_flip(x, k & 2), _flip(y, k & 1), c)
                cp = pltpu.make_async_remote_copy(src_ref=ins[a], dst_ref=_shard_of(outs[a], axes[a], 2 * x + y, size),
                                                  send_sem=send_sems.at[3 * a + k - 1], recv_sem=recv_sems.at[3 * a + k - 1],
                                                  device_id=peer, device_id_type=MESH)
                cp.start()
                sends.append(cp)
        for a in range(n):
            size = ins[a].shape[axes[a]]
            for k in range(1, 4):
                peer = (_flip(x, k & 2), _flip(y, k & 1), c)
                pltpu.make_async_remote_copy(src_ref=ins[a], dst_ref=_shard_of(outs[a], axes[a], 2 * peer[0] + peer[1], size),
                                             send_sem=send_sems.at[3 * a + k - 1], recv_sem=recv_sems.at[3 * a + k - 1],
                                             device_id=peer, device_id_type=MESH).wait_recv()
        for cp in sends:
            cp.wait_send()
        for cp in local:
            cp.wait()

    hbm = pl.BlockSpec(memory_space=pl.ANY)
    return pl.pallas_call(
        body, name=name, out_shape=fulls, in_specs=[hbm] * n, out_specs=[hbm] * n,
        scratch_shapes=[pltpu.SemaphoreType.DMA((3 * n,)), pltpu.SemaphoreType.DMA((3 * n,)), pltpu.SemaphoreType.DMA((n,))])(*shards)


def _scatter_chips(fulls, axes, name):
    n = len(fulls)
    recvs = []
    for a, ax in zip(fulls, axes):
        shp = list(a.shape)
        shp[ax] //= 4
        recvs.append(_sds((3, *shp), a.dtype))

    def body(*refs):
        ins, outs = refs[:n], refs[n:2 * n]
        send_sems, recv_sems = refs[2 * n:]
        x, y, c = _here()
        sends = []
        for a in range(n):
            size = ins[a].shape[axes[a]] // 4
            for k in range(1, 4):
                peer = (_flip(x, k & 2), _flip(y, k & 1), c)
                cp = pltpu.make_async_remote_copy(src_ref=_shard_of(ins[a], axes[a], 2 * peer[0] + peer[1], size),
                                                  dst_ref=outs[a].at[k - 1],
                                                  send_sem=send_sems.at[3 * a + k - 1], recv_sem=recv_sems.at[3 * a + k - 1],
                                                  device_id=peer, device_id_type=MESH)
                cp.start()
                sends.append(cp)
        for cp in sends:
            cp.wait_recv()
        for cp in sends:
            cp.wait_send()

    hbm = pl.BlockSpec(memory_space=pl.ANY)
    return pl.pallas_call(
        body, name=name, out_shape=recvs, in_specs=[hbm] * n, out_specs=[hbm] * n,
        scratch_shapes=[pltpu.SemaphoreType.DMA((3 * n,)), pltpu.SemaphoreType.DMA((3 * n,))])(*fulls)


def _sibling_swap(arrays, name):
    n = len(arrays)

    def body(*refs):
        ins, outs = refs[:n], refs[n:2 * n]
        send_sems, recv_sems = refs[2 * n:]
        x, y, c = _here()
        copies = []
        for a in range(n):
            cp = pltpu.make_async_remote_copy(src_ref=ins[a], dst_ref=outs[a], send_sem=send_sems.at[a], recv_sem=recv_sems.at[a],
                                              device_id=(x, y, 1 - c), device_id_type=MESH)
            cp.start()
            copies.append(cp)
        for cp in copies:
            cp.wait()

    hbm = pl.BlockSpec(memory_space=pl.ANY)
    return pl.pallas_call(
        body, name=name, out_shape=[_sds(a.shape, a.dtype) for a in arrays], in_specs=[hbm] * n, out_specs=[hbm] * n,
        scratch_shapes=[pltpu.SemaphoreType.DMA((n,)), pltpu.SemaphoreType.DMA((n,))])(*arrays)


def _ffn_fwd(xs, g, mods, k0, w_up, w_down, nx, tag):
    h = _normmod(xs, g, mods, k0, k0 + 1, nx, tag + "_norm")
    up = _mm(h, w_up, "nn", F32, tag + "_up")
    act = _swiglu(up, tag + "_act")
    y = _mm(act, w_down, "nn", F32, tag + "_down")
    return _resid(xs, y, mods, k0 + 2, 0.5, nx, tag + "_res"), (xs, h, up, act, y)


def _ffn_bwd(dxn, saved, g, mods, k0, w_up, w_down, nx, tag):
    xs, h, up, act, y = saved
    dy, dgate = _resid_bwd(dxn, y, mods, k0 + 2, 0.5, nx, tag + "_res_b")
    d_down = _mm(act, dy, "tn", F32, tag + "_down_dw")
    dact = _mm(dy, w_down, "nt", F32, tag + "_down_dx")
    dup = _swiglu_bwd(up, dact, tag + "_act_b")
    d_up = _mm(h, dup, "tn", F32, tag + "_up_dw")
    dh = _mm(dup, w_up, "nt", F32, tag + "_up_dx")
    dx, dsh, dsc, dg = _normmod_bwd(xs, dh, dxn, g, mods, k0 + 1, nx, tag + "_norm_b")
    return dx, d_up, d_down, dg, [dsh, dsc, dgate]


def _mix_fwd(xs, g, mods, wl, pl_, tabs, s, nx, tag):
    cos, sin = tabs
    h = _normmod(xs, g, mods, 3, 4, nx, tag + "_norm")
    p = _mm(h, wl["w_in"], "nn", F32, tag + "_in")
    q, k, v = _rope(p, cos, sin, tag + "_rope")
    bias = _bias_table(pl_["rpb"], s // GRID_W)
    oa = _na_fwd(q, k, v, bias, s, tag + "_na")
    ob = _gmlp(p, pl_["ln_v_g"], pl_["ln_v_b"], pl_["w_s"], pl_["b_s"], tag + "_sg")
    pa = _mm(oa, wl["w_pa"], "nn", F32, tag + "_pa")
    pb = _mm(ob, wl["w_pb"], "nn", F32, tag + "_pb")
    mg = _merge(pa, pb, p, pl_["b_gate"], tag + "_merge")
    y = _mm(mg, wl["w_o"], "nn", F32, tag + "_o")
    return _resid(xs, y, mods, 5, 1.0, nx, tag + "_res"), (xs, h, p, q, k, v, bias, oa, ob, pa, pb, mg, y)


def _mix_bwd(dxn, saved, g, mods, wl, pl_, tabs, s, nx, tag):
    xs, h, p, q, k, v, bias, oa, ob, pa, pb, mg, y = saved
    cos, sin = tabs
    gw, gp = {}, {}
    dy, dgate = _resid_bwd(dxn, y, mods, 5, 1.0, nx, tag + "_res_b")
    gw["w_o"] = _mm(mg, dy, "tn", F32, tag + "_o_dw")
    dmg = _mm(dy, wl["w_o"], "nt", F32, tag + "_o_dx")
    dpa, dpb, dla, dlb, dba, dbb = _merge_bwd(dmg, pa, pb, p, pl_["b_gate"], tag + "_merge_b")
    gp["b_gate"] = jnp.concatenate([dba, dbb], axis=1)
    gw["w_pa"] = _mm(oa, dpa, "tn", F32, tag + "_pa_dw")
    doa = _mm(dpa, wl["w_pa"], "nt", F32, tag + "_pa_dx")
    gw["w_pb"] = _mm(ob, dpb, "tn", F32, tag + "_pb_dw")
    dob = _mm(dpb, wl["w_pb"], "nt", F32, tag + "_pb_dx")
    du, dvs, gp["w_s"], gp["b_s"], gp["ln_v_g"], gp["ln_v_b"] = _gmlp_bwd(
        p, dob, pl_["ln_v_g"], pl_["ln_v_b"], pl_["w_s"], pl_["b_s"], tag + "_sg_b")
    dqr, dkr, dv, dbias = _na_bwd(q, k, v, doa, bias, s, tag + "_na_b")
    gp["rpb"] = _rpb_grad(dbias, s // GRID_W, tag + "_rpb")
    dq, dk, dvv = _rope_bwd(dqr, dkr, dv, cos, sin, tag + "_rope_b")
    dp = jnp.concatenate([dq, dk, dvv, du, dvs, dla, dlb], axis=1)
    gw["w_in"] = _mm(h, dp, "tn", F32, tag + "_in_dw")
    dh = _mm(dp, wl["w_in"], "nt", F32, tag + "_in_dx")
    dx, dsh, dsc, dg = _normmod_bwd(xs, dh, dxn, g, mods, 4, nx, tag + "_norm_b")
    return dx, gw, gp, dg, [dsh, dsc, dgate]


def _local_step(x, ctx, tgt, mods, wts, prm):
    s, d = x.shape
    depth = mods.shape[0]
    nx = s // TM
    tabs = _rope_tables(s, ctx.shape[0])
    xs = jnp.concatenate([x, ctx], axis=0)
    saved = []
    for l in range(depth):
        wl = {k: v[l] for k, v in wts.items()}
        pl_ = _layer_params(prm, l)
        xs, s1 = _ffn_fwd(xs, pl_["g"][0], mods[l], 0, wl["w_ff1_up"], wl["w_ff1_down"], nx, f"l{l}_ff1")
        xs, s2 = _mix_fwd(xs, pl_["g"][1], mods[l], wl, pl_, tabs, s, nx, f"l{l}_mix")
        xs, s3 = _ffn_fwd(xs, pl_["g"][2], mods[l], 6, wl["w_ff2_up"], wl["w_ff2_down"], nx, f"l{l}_ff2")
        saved.append((s1, s2, s3))
    loss, dxs, d_final_g = _final(xs, tgt, prm["final_g"].reshape(1, d), "final")
    gw = {k: [None] * depth for k in wts}
    gp = {k: [None] * depth for k in ("norm_g", "b_gate", "rpb", "ln_v_g", "ln_v_b", "w_s", "b_s")}
    dmods = [None] * depth
    for l in reversed(range(depth)):
        wl = {k: v[l] for k, v in wts.items()}
        pl_ = _layer_params(prm, l)
        s1, s2, s3 = saved[l]
        dxs, gw["w_ff2_up"][l], gw["w_ff2_down"][l], dg2, dm2 = _ffn_bwd(
            dxs, s3, pl_["g"][2], mods[l], 6, wl["w_ff2_up"], wl["w_ff2_down"], nx, f"l{l}_ff2")
        dxs, gwm, gpm, dg1, dm1 = _mix_bwd(dxs, s2, pl_["g"][1], mods[l], wl, pl_, tabs, s, nx, f"l{l}_mix")
        dxs, gw["w_ff1_up"][l], gw["w_ff1_down"][l], dg0, dm0 = _ffn_bwd(
            dxs, s1, pl_["g"][0], mods[l], 0, wl["w_ff1_up"], wl["w_ff1_down"], nx, f"l{l}_ff1")
        for k, v in gwm.items():
            gw[k][l] = v
        gp["b_gate"][l] = gpm["b_gate"][0]
        gp["rpb"][l] = gpm["rpb"]
        gp["ln_v_g"][l] = gpm["ln_v_g"][0]
        gp["ln_v_b"][l] = gpm["ln_v_b"][0]
        gp["w_s"][l] = gpm["w_s"]
        gp["b_s"][l] = gpm["b_s"][..., 0]
        gp["norm_g"][l] = jnp.concatenate([dg0, dg1, dg2], axis=0)
        dmods[l] = jnp.concatenate(dm0 + dm1 + dm2, axis=1)
    gw = {k: jnp.stack(v) for k, v in gw.items()}
    gp = {k: jnp.stack(v) for k, v in gp.items()}
    gp["final_g"] = d_final_g[0]
    return loss[0, 0], dxs[:s], jnp.stack(dmods), gw, gp


def _layer_params(prm, l):
    d = prm["norm_g"].shape[-1]
    return {
        "g": [prm["norm_g"][l, i].reshape(1, d) for i in range(3)],
        "b_gate": prm["b_gate"][l].reshape(1, -1),
        "rpb": prm["rpb"][l],
        "ln_v_g": prm["ln_v_g"][l].reshape(1, -1),
        "ln_v_b": prm["ln_v_b"][l].reshape(1, -1),
        "w_s": prm["w_s"][l],
        "b_s": prm["b_s"][l][..., None],
    }


SMALL = ("norm_g", "b_gate", "rpb", "ln_v_g", "ln_v_b", "w_s", "b_s", "final_g")
PACK_LANES = 1024


def _pack(parts):
    flat = jnp.concatenate([p.reshape(-1) for p in parts])
    rows = -(-flat.shape[0] // PACK_LANES)
    rows = -(-rows // 8) * 8
    return jnp.pad(flat, (0, rows * PACK_LANES - flat.shape[0])).reshape(rows, PACK_LANES)


def _unpack(flat, shapes):
    out, off = [], 0
    for shp in shapes:
        n = int(np.prod(shp))
        out.append(flat[..., off:off + n].reshape(*flat.shape[:-1], *shp))
        off += n
    return out


def kernel(x, c, ctx, c_ctx, w_ada, b_ada, norm_g, w_ff1_up, w_ff1_down, w_in, b_gate, rpb, ln_v_g, ln_v_b, w_s, b_s, w_pa, w_pb, w_o, w_ff2_up, w_ff2_down, final_g, loss_target, m_c_ctx, m_w_ada, m_b_ada, m_norm_g, m_w_ff1_up, m_w_ff1_down, m_w_in, m_b_gate, m_rpb, m_ln_v_g, m_ln_v_b, m_w_s, m_b_s, m_w_pa, m_w_pb, m_w_o, m_w_ff2_up, m_w_ff2_down, m_final_g, v_c_ctx, v_w_ada, v_b_ada, v_norm_g, v_w_ff1_up, v_w_ff1_down, v_w_in, v_b_gate, v_rpb, v_ln_v_g, v_ln_v_b, v_w_s, v_b_s, v_w_pa, v_w_pb, v_w_o, v_w_ff2_up, v_w_ff2_down, v_final_g):
    weights = dict(c_ctx=c_ctx, w_ada=w_ada, b_ada=b_ada, norm_g=norm_g, w_ff1_up=w_ff1_up, w_ff1_down=w_ff1_down, w_in=w_in,
                   b_gate=b_gate, rpb=rpb, ln_v_g=ln_v_g, ln_v_b=ln_v_b, w_s=w_s, b_s=b_s, w_pa=w_pa, w_pb=w_pb, w_o=w_o,
                   w_ff2_up=w_ff2_up, w_ff2_down=w_ff2_down, final_g=final_g)
    mom_m = dict(c_ctx=m_c_ctx, w_ada=m_w_ada, b_ada=m_b_ada, norm_g=m_norm_g, w_ff1_up=m_w_ff1_up, w_ff1_down=m_w_ff1_down,
                 w_in=m_w_in, b_gate=m_b_gate, rpb=m_rpb, ln_v_g=m_ln_v_g, ln_v_b=m_ln_v_b, w_s=m_w_s, b_s=m_b_s, w_pa=m_w_pa,
                 w_pb=m_w_pb, w_o=m_w_o, w_ff2_up=m_w_ff2_up, w_ff2_down=m_w_ff2_down, final_g=m_final_g)
    mom_v = dict(c_ctx=v_c_ctx, w_ada=v_w_ada, b_ada=v_b_ada, norm_g=v_norm_g, w_ff1_up=v_w_ff1_up, w_ff1_down=v_w_ff1_down,
                 w_in=v_w_in, b_gate=v_b_gate, rpb=v_rpb, ln_v_g=v_ln_v_g, ln_v_b=v_ln_v_b, w_s=v_w_s, b_s=v_b_s, w_pa=v_w_pa,
                 w_pb=v_w_pb, w_o=v_w_o, w_ff2_up=v_w_ff2_up, w_ff2_down=v_w_ff2_down, final_g=v_final_g)
    order = list(weights)
    mx, my, mc = _here()
    dev = 4 * mx + 2 * my + mc
    chip = 2 * mx + my
    depth, d, n_ada = w_ada.shape
    dq = d // 4

    c_all = _allgather8(jnp.pad(c, ((0, 7), (0, 0))), "gather_c")[:, 0, :]
    cond = jnp.concatenate([c_all, c_ctx[None, :], jnp.zeros((7, d), F32)], axis=0)
    b_shard = lax.dynamic_slice(b_ada, (0, chip * n_ada), (depth, n_ada))
    proj = [_ada_fwd(cond, w_ada[l], b_shard[l:l + 1], f"ada{l}") for l in range(depth)]
    silu_c = proj[0][1]
    mods_sh = _allgather8(jnp.concatenate([p[0] for p in proj], axis=0), "gather_mods")
    mods_all = jnp.transpose(mods_sh[0::2].reshape(4, depth, 16, n_ada), (1, 2, 0, 3)).reshape(depth, 16, N_MOD, d)
    mods = jnp.stack([lax.dynamic_index_in_dim(mods_all, dev, axis=1, keepdims=False), mods_all[:, 8]], axis=1)

    full = _gather_chips([weights[k].astype(MXU) for k in BIG], [SHARD_AXIS[k] for k in BIG], "gather_w")
    wts = dict(zip(BIG, full))
    prm = {k: weights[k] for k in SMALL if k != "norm_g"}
    norm_full = _allgather8(jnp.pad(norm_g.reshape(depth * 3, dq), ((0, 8 - depth * 3), (0, 0))), "gather_norm_g")
    prm["norm_g"] = jnp.transpose(norm_full[0::2, :depth * 3].reshape(4, depth, 3, dq), (1, 2, 0, 3)).reshape(depth, 3, d)

    loss, grad_x, dmods, gw, gp = _local_step(x[0], ctx[0], loss_target[0], mods, wts, prm)
    loss = lax.psum(loss, ("x", "y", "c"))

    small_shapes = [(depth, 2, N_MOD * d)] + [weights[k].shape if k != "norm_g" else (depth, 3, d) for k in SMALL]
    packed = _allgather8(_pack([dmods.reshape(depth, 2, N_MOD * d)] + [gp[k] for k in SMALL]), "gather_small")
    rows = packed.shape[1]
    total = _sum_pieces([packed[i] for i in range(8)], "sum_small").reshape(-1)
    sums = dict(zip(("dmods",) + SMALL, _unpack(total, small_shapes)))
    dmods_dev = _unpack(packed.reshape(8, rows * PACK_LANES), small_shapes[:1])[0]

    g_ada, cc_parts = [], []
    for l in range(depth):
        dm = jnp.concatenate([dmods_dev[:, l, 0], sums["dmods"][l, 1][None], jnp.zeros((7, N_MOD * d), F32)], axis=0)
        dm_sh = lax.dynamic_slice(dm, (0, chip * n_ada), (16, n_ada))
        g_ada.append(_mm(silu_c, dm_sh, "tn", F32, f"ada{l}_dw"))
        cc_parts.append(_mm(dm_sh, w_ada[l], "nt", F32, f"ada{l}_dc")[8:9])
    cc_all = _allgather8(jnp.pad(jnp.concatenate(cc_parts, axis=0), ((0, 8 - depth), (0, 0))), "gather_cctx")
    g_cctx = _cctx_grad(cc_all[0::2, :depth].reshape(4 * depth, d), c_ctx.reshape(1, d), "cctx_grad")

    axes = [SHARD_AXIS[k] for k in BIG]
    recv = _scatter_chips([gw[k] for k in BIG], axes, "scatter_gw")
    part = []
    for k, ax, r in zip(BIG, axes, recv):
        size = gw[k].shape[ax] // 4
        own = lax.dynamic_slice_in_dim(gw[k], chip * size, size, axis=ax)
        part.append(_sum_pieces([own, r[0], r[1], r[2]], "sum_" + k))
    other = _sibling_swap(part, "swap_gw")

    pieces = {k: [a, b] for k, a, b in zip(BIG, part, other)}
    pieces["w_ada"] = [jnp.stack(g_ada)]
    pieces["b_ada"] = [sums["dmods"][:, 0], sums["dmods"][:, 1]]
    pieces["c_ctx"] = [g_cctx[0]]
    for k in SMALL:
        pieces[k] = [sums[k]]
    pieces["norm_g"] = [lax.dynamic_slice_in_dim(sums["norm_g"], chip * dq, dq, axis=2)]
    res = {k: _adamw(weights[k], pieces[k], mom_m[k], mom_v[k], "adamw_" + k) for k in order}
    return (loss, grad_x[None], *[res[k][0] for k in order], *[res[k][1] for k in order],
            *[res[k][2] for k in order], *[res[k][3] for k in order])
```

```python
import numpy as np
import jax
import jax.numpy as jnp
from jax import lax
from jax.experimental import pallas as pl
from jax.experimental.pallas import tpu as pltpu

F32 = jnp.float32
MXU = jnp.bfloat16
EPS = 1e-6
GRID_W, HEADS, HEAD_DIM = 64, 8, 64
NA_WIDTH = SG_WIDTH = 512
WIN_H, WIN_W = 8, 16
SG_CHUNK, SG_GROUPS = 128, 4
N_MOD = 9
ROPE_THETA = 10000.0
Q_ROWS, K_ROWS = 4, 12
TQ, TK = Q_ROWS * GRID_W, K_ROWS * GRID_W
TM = 256
LANES = 128
NEG = -1e30
VMEM_LIMIT = 56 * 2 ** 20
ADAM_LR, ADAM_B1, ADAM_B2, ADAM_EPS, ADAM_WD, ADAM_STEP = 0.001, 0.9, 0.999, 1e-08, 0.01, 10
MESH = pl.DeviceIdType.MESH
BIG = ("w_ff1_up", "w_ff1_down", "w_in", "w_pa", "w_pb", "w_o", "w_ff2_up", "w_ff2_down")
SHARD_AXIS = {"w_ff1_up": 2, "w_ff1_down": 1, "w_in": 2, "w_pa": 2, "w_pb": 2, "w_o": 1, "w_ff2_up": 2, "w_ff2_down": 1}


def _call(body, *, name, grid, in_specs, out_specs, out_shape, scratch=()):
    return pl.pallas_call(
        body, name=name, grid=grid, in_specs=in_specs, out_specs=out_specs, out_shape=out_shape,
        scratch_shapes=list(scratch),
        compiler_params=pltpu.CompilerParams(dimension_semantics=("arbitrary",) * len(grid), vmem_limit_bytes=VMEM_LIMIT))


def _pick(n, prefs):
    for p in prefs:
        if n % p == 0:
            return p
    return n


def _rows(tm, n, col=0):
    return pl.BlockSpec((tm, n), lambda i: (i, col))


def _fixed(shape):
    return pl.BlockSpec(shape, lambda *_: (0,) * len(shape))


def _stream(d, nx):
    return pl.BlockSpec((1, 1, d), lambda i: (i // nx, 0, 0))


def _sds(shape, dtype):
    return jax.ShapeDtypeStruct(shape, dtype)


def _mm(a, b, mode, out_dtype, name):
    if mode == "tn":
        r, m = a.shape
        n = b.shape[1]
        tm = _pick(m, (1024, 1408, 704, 512, 256, 128))
        tn = _pick(n, (512, 256, 128))
        tr = _pick(r, (1280, 640, 512, 256, 128))

        def body(a_ref, b_ref, o_ref):
            @pl.when(pl.program_id(2) == 0)
            def _():
                o_ref[...] = jnp.zeros_like(o_ref)

            o_ref[...] += lax.dot_general(a_ref[...].astype(MXU), b_ref[...].astype(MXU), (((0,), (0,)), ((), ())),
                                          preferred_element_type=F32)

        return _call(body, name=name, grid=(m // tm, n // tn, r // tr),
                     in_specs=[pl.BlockSpec((tr, tm), lambda i, j, k: (k, i)), pl.BlockSpec((tr, tn), lambda i, j, k: (k, j))],
                     out_specs=pl.BlockSpec((tm, tn), lambda i, j, k: (i, j)), out_shape=_sds((m, n), F32))(a, b)
    m, k = a.shape
    n = b.shape[1] if mode == "nn" else b.shape[0]
    tm = _pick(m, (1280, 640, 512, 256, 128) if k <= 2816 else (640, 512, 256, 128))
    tn = _pick(n, (512, 1408, 256, 128))
    dims = (((1,), (0,)), ((), ())) if mode == "nn" else (((1,), (1,)), ((), ()))

    def body(a_ref, b_ref, o_ref):
        o_ref[...] = lax.dot_general(a_ref[...].astype(MXU), b_ref[...].astype(MXU), dims,
                                     preferred_element_type=F32).astype(o_ref.dtype)

    b_spec = pl.BlockSpec((k, tn), lambda i, j: (0, j)) if mode == "nn" else pl.BlockSpec((tn, k), lambda i, j: (j, 0))
    return _call(body, name=name, grid=(m // tm, n // tn), in_specs=[pl.BlockSpec((tm, k), lambda i, j: (i, 0)), b_spec],
                 out_specs=pl.BlockSpec((tm, tn), lambda i, j: (i, j)), out_shape=_sds((m, n), out_dtype))(a, b)


def _normmod(xs, g, mods, k_shift, k_scale, nx, name):
    t, d = xs.shape

    def body(x_ref, g_ref, m_ref, h_ref):
        x = x_ref[...]
        rstd = lax.rsqrt(jnp.mean(x * x, axis=-1, keepdims=True) + EPS)
        scale = m_ref[0, k_scale:k_scale + 1, :]
        shift = m_ref[0, k_shift:k_shift + 1, :]
        h_ref[...] = (x * rstd * g_ref[...] * (1.0 + scale) + shift).astype(h_ref.dtype)

    return _call(body, name=name, grid=(t // TM,),
                 in_specs=[_rows(TM, d), _fixed((1, d)), pl.BlockSpec((1, N_MOD, d), lambda i: (i // nx, 0, 0))],
                 out_specs=_rows(TM, d), out_shape=_sds((t, d), MXU))(xs, g, mods)


def _normmod_bwd(xs, dh, dres, g, mods, k_scale, nx, name):
    t, d = xs.shape

    def body(x_ref, dh_ref, dr_ref, g_ref, m_ref, dx_ref, dsh_ref, dsc_ref, dg_ref):
        i = pl.program_id(0)
        x = x_ref[...]
        dh = dh_ref[...].astype(F32)
        rstd = lax.rsqrt(jnp.mean(x * x, axis=-1, keepdims=True) + EPS)
        xhat = x * rstd
        gg = g_ref[...]

        @pl.when((i == 0) | (i == nx))
        def _():
            dsh_ref[...] = jnp.zeros_like(dsh_ref)
            dsc_ref[...] = jnp.zeros_like(dsc_ref)

        @pl.when(i == 0)
        def _():
            dg_ref[...] = jnp.zeros_like(dg_ref)

        dsh_ref[0] += jnp.sum(dh, axis=0, keepdims=True)
        dsc_ref[0] += jnp.sum(dh * (xhat * gg), axis=0, keepdims=True)
        dy = dh * (1.0 + m_ref[0, k_scale:k_scale + 1, :])
        dg_ref[...] += jnp.sum(dy * xhat, axis=0, keepdims=True)
        dxh = dy * gg
        dx_ref[...] = dr_ref[...] + rstd * (dxh - xhat * jnp.mean(dxh * xhat, axis=-1, keepdims=True))

    return _call(body, name=name, grid=(t // TM,),
                 in_specs=[_rows(TM, d), _rows(TM, d), _rows(TM, d), _fixed((1, d)),
                           pl.BlockSpec((1, N_MOD, d), lambda i: (i // nx, 0, 0))],
                 out_specs=[_rows(TM, d), _stream(d, nx), _stream(d, nx), _fixed((1, d))],
                 out_shape=[_sds((t, d), F32), _sds((2, 1, d), F32), _sds((2, 1, d), F32), _sds((1, d), F32)])(
                     xs, dh, dres, g, mods)


def _resid(xs, y, mods, k_gate, coef, nx, name):
    t, d = xs.shape

    def body(x_ref, y_ref, m_ref, o_ref):
        o_ref[...] = x_ref[...] + (coef * m_ref[0, k_gate:k_gate + 1, :]) * y_ref[...].astype(F32)

    return _call(body, name=name, grid=(t // TM,),
                 in_specs=[_rows(TM, d), _rows(TM, d), pl.BlockSpec((1, N_MOD, d), lambda i: (i // nx, 0, 0))],
                 out_specs=_rows(TM, d), out_shape=_sds((t, d), F32))(xs, y, mods)


def _resid_bwd(dxn, y, mods, k_gate, coef, nx, name):
    t, d = dxn.shape

    def body(dx_ref, y_ref, m_ref, dy_ref, dgt_ref):
        i = pl.program_id(0)

        @pl.when((i == 0) | (i == nx))
        def _():
            dgt_ref[...] = jnp.zeros_like(dgt_ref)

        dx = dx_ref[...]
        dy_ref[...] = ((coef * m_ref[0, k_gate:k_gate + 1, :]) * dx).astype(dy_ref.dtype)
        dgt_ref[0] += jnp.sum(coef * y_ref[...].astype(F32) * dx, axis=0, keepdims=True)

    return _call(body, name=name, grid=(t // TM,),
                 in_specs=[_rows(TM, d), _rows(TM, d), pl.BlockSpec((1, N_MOD, d), lambda i: (i // nx, 0, 0))],
                 out_specs=[_rows(TM, d), _stream(d, nx)],
                 out_shape=[_sds((t, d), MXU), _sds((2, 1, d), F32)])(dxn, y, mods)


def _swiglu(up, name):
    t, f2 = up.shape
    f = f2 // 2

    def body(u_ref, o_ref):
        a = u_ref[:, :f].astype(F32)
        o_ref[...] = (a * jax.nn.sigmoid(a) * u_ref[:, f:].astype(F32)).astype(o_ref.dtype)

    return _call(body, name=name, grid=(t // TM,), in_specs=[_rows(TM, f2)], out_specs=_rows(TM, f),
                 out_shape=_sds((t, f), MXU))(up)


def _swiglu_bwd(up, dact, name):
    t, f2 = up.shape
    f = f2 // 2

    def body(u_ref, d_ref, o_ref):
        a = u_ref[:, :f].astype(F32)
        b = u_ref[:, f:].astype(F32)
        dact = d_ref[...].astype(F32)
        sg = jax.nn.sigmoid(a)
        o_ref[:, :f] = (dact * b * sg * (1.0 + a * (1.0 - sg))).astype(o_ref.dtype)
        o_ref[:, f:] = (dact * a * sg).astype(o_ref.dtype)

    return _call(body, name=name, grid=(t // TM,), in_specs=[_rows(TM, f2), _rows(TM, f)], out_specs=_rows(TM, f2),
                 out_shape=_sds((t, f2), MXU))(up, dact)


def _rope_tables(s, ctx_len):
    n_freq = HEAD_DIM // 4
    tok = jnp.arange(s)
    freqs = ROPE_THETA ** (-jnp.arange(n_freq, dtype=F32) / n_freq)
    ang = jnp.concatenate([(tok // GRID_W).astype(F32)[:, None] * freqs, (tok % GRID_W).astype(F32)[:, None] * freqs], axis=-1)
    cos = jnp.repeat(jnp.cos(ang), 2, axis=-1)
    sin = jnp.repeat(jnp.sin(ang), 2, axis=-1) * jnp.tile(jnp.array([-1.0, 1.0], F32), HEAD_DIM // 2)
    cos = jnp.concatenate([jnp.tile(cos, (1, HEADS)), jnp.ones((ctx_len, NA_WIDTH), F32)], axis=0)
    sin = jnp.concatenate([jnp.tile(sin, (1, HEADS)), jnp.zeros((ctx_len, NA_WIDTH), F32)], axis=0)
    return cos, sin


def _swap_pairs(x):
    n = x.shape[-1]
    lane = lax.broadcasted_iota(jnp.int32, x.shape, 1)
    return jnp.where(lane % 2 == 0, pltpu.roll(x, n - 1, 1), pltpu.roll(x, 1, 1))


def _rope(p, cos, sin, name):
    t = p.shape[0]
    w = NA_WIDTH

    def body(q_ref, k_ref, v_ref, c_ref, s_ref, qo_ref, ko_ref, vo_ref):
        c, s = c_ref[...], s_ref[...]
        q, k = q_ref[...].astype(F32), k_ref[...].astype(F32)
        qo_ref[...] = (q * c + _swap_pairs(q) * s).astype(qo_ref.dtype)
        ko_ref[...] = (k * c + _swap_pairs(k) * s).astype(ko_ref.dtype)
        vo_ref[...] = v_ref[...].astype(vo_ref.dtype)

    return _call(body, name=name, grid=(t // TM,),
                 in_specs=[_rows(TM, w, 0), _rows(TM, w, 1), _rows(TM, w, 2), _rows(TM, w), _rows(TM, w)],
                 out_specs=[_rows(TM, w)] * 3, out_shape=[_sds((t, w), MXU)] * 3)(p, p, p, cos, sin)


def _rope_bwd(dq, dk, dv, cos, sin, name):
    t = dq.shape[0]

    def body(dq_ref, dk_ref, dv_ref, c_ref, s_ref, qo_ref, ko_ref, vo_ref):
        c, s = c_ref[...], s_ref[...]
        a, b = dq_ref[...], dk_ref[0]
        qo_ref[...] = (a * c + _swap_pairs(a * s)).astype(qo_ref.dtype)
        ko_ref[...] = (b * c + _swap_pairs(b * s)).astype(ko_ref.dtype)
        vo_ref[...] = dv_ref[0].astype(vo_ref.dtype)

    tile = pl.BlockSpec((TM, LANES), lambda i, j: (i, j))
    pair = pl.BlockSpec((1, TM, LANES), lambda i, j: (j, i, 0))
    return _call(body, name=name, grid=(t // TM, NA_WIDTH // LANES), in_specs=[tile, pair, pair, tile, tile],
                 out_specs=[tile] * 3, out_shape=[_sds((t, NA_WIDTH), MXU)] * 3)(dq, dk, dv, cos, sin)


def _na_geometry(r_grid):
    rows = []
    for r0, ks in ((0, 0), (Q_ROWS, 0), (r_grid - Q_ROWS, r_grid - K_ROWS)):
        dr = np.zeros((Q_ROWS, K_ROWS), np.int32)
        vr = np.zeros((Q_ROWS, K_ROWS), bool)
        for a in range(Q_ROWS):
            r = r0 + a
            rs = min(max(r - WIN_H // 2, 0), r_grid - WIN_H)
            for i in range(K_ROWS):
                kr = ks + i
                vr[a, i] = rs <= kr <= rs + WIN_H - 1
                dr[a, i] = kr - r + WIN_H - 1
        rows.append((dr, vr))
    c = np.arange(GRID_W)
    cs = np.clip(c - WIN_W // 2, 0, GRID_W - WIN_W)
    kc = np.arange(GRID_W)
    vc = (kc[None, :] >= cs[:, None]) & (kc[None, :] <= cs[:, None] + WIN_W - 1)
    dc = kc[None, :] - c[:, None] + WIN_W - 1
    return rows, dc, vc


def _bias_table(rpb, r_grid):
    rows, _, vc = _na_geometry(r_grid)
    n_dr, n_dc, skew, off = 2 * WIN_H - 1, 2 * WIN_W - 1, 2 * GRID_W - 1, GRID_W - WIN_W
    u = jnp.pad(rpb, ((0, 0), (0, 0), (off, skew - off - n_dc)))
    toep = jnp.tile(u, (1, 1, GRID_W + 1))[:, :, :2 * GRID_W * GRID_W].reshape(HEADS, n_dr, GRID_W, 2 * GRID_W)
    toep = jnp.pad(toep[:, :, ::-1, :GRID_W], ((0, 0), (Q_ROWS, Q_ROWS), (0, 0), (0, 0)))
    tabs = []
    for dr, vr in rows:
        per_row = []
        for a in range(Q_ROWS):
            lo = int(dr[a, 0]) + Q_ROWS
            valid = vr[a][:, None, None] & vc[None, :, :]
            per_row.append(jnp.where(valid[None], toep[:, lo:lo + K_ROWS], NEG))
        t = jnp.stack(per_row, axis=1)
        tabs.append(jnp.transpose(t, (0, 1, 3, 2, 4)).reshape(HEADS, TQ, TK))
    tabs.append(jnp.full((HEADS, TQ, TK), NEG, F32))
    return jnp.stack(tabs)


def _variant(g, ngx):
    return jnp.where(g == 0, 0, jnp.where(g >= ngx, 3, jnp.where(g == ngx - 1, 2, 1)))


def _key_start(g, r_grid):
    return pl.multiple_of(jnp.clip(g * Q_ROWS - WIN_H // 2, 0, r_grid - K_ROWS) * GRID_W, TQ)


def _nt(a, b):
    return lax.dot_general(a, b, (((1,), (1,)), ((), ())), preferred_element_type=F32)


def _tn(a, b):
    return lax.dot_general(a, b, (((0,), (0,)), ((), ())), preferred_element_type=F32)


def _nn(a, b):
    return jnp.dot(a, b, preferred_element_type=F32)


def _head_mask(h):
    lane = lax.broadcasted_iota(jnp.int32, (1, LANES), 1)
    return ((lane >= HEAD_DIM * h) & (lane < HEAD_DIM * (h + 1))).astype(F32)


def _softmax_parts(qm, knb, kcx, bias):
    s_nb = _nt(qm, knb) + bias
    s_cx = _nt(qm, kcx)
    m = jnp.maximum(jnp.max(s_nb, axis=-1, keepdims=True), jnp.max(s_cx, axis=-1, keepdims=True))
    e_nb = jnp.exp(s_nb - m)
    e_cx = jnp.exp(s_cx - m)
    inv = 1.0 / (jnp.sum(e_nb, axis=-1, keepdims=True) + jnp.sum(e_cx, axis=-1, keepdims=True))
    return e_nb * inv, e_cx * inv


def _na_specs(t, ngx):
    q_spec = pl.BlockSpec((TQ, LANES), lambda hp, g: (g, hp))
    kv_spec = pl.BlockSpec((t, LANES), lambda hp, g: (0, hp))
    b_spec = pl.BlockSpec((1, 2, TQ, TK), lambda hp, g: (_variant(g, ngx), hp, 0, 0))
    return q_spec, kv_spec, b_spec


def _na_fwd(q, k, v, bias, s, name):
    t = q.shape[0]
    ctx_len = t - s
    r_grid = s // GRID_W
    q_spec, kv_spec, b_spec = _na_specs(t, s // TQ)

    def body(q_ref, k_ref, v_ref, b_ref, o_ref):
        start = _key_start(pl.program_id(1), r_grid)
        qf = q_ref[...].astype(F32) * (HEAD_DIM ** -0.5)
        knb, vnb = k_ref[pl.ds(start, TK), :], v_ref[pl.ds(start, TK), :]
        kcx, vcx = k_ref[pl.ds(s, ctx_len), :], v_ref[pl.ds(s, ctx_len), :]
        acc = jnp.zeros((TQ, LANES), F32)
        for h in range(2):
            mask = _head_mask(h)
            p_nb, p_cx = _softmax_parts((qf * mask).astype(MXU), knb, kcx, b_ref[0, h])
            acc += (_nn(p_nb.astype(MXU), vnb) + _nn(p_cx.astype(MXU), vcx)) * mask
        o_ref[...] = acc.astype(o_ref.dtype)

    return _call(body, name=name, grid=(NA_WIDTH // LANES, t // TQ), in_specs=[q_spec, kv_spec, kv_spec, b_spec],
                 out_specs=q_spec, out_shape=_sds((t, NA_WIDTH), MXU))(q, k, v, bias)


def _na_bwd(q, k, v, do, bias, s, name):
    t = q.shape[0]
    ctx_len = t - s
    r_grid = s // GRID_W
    ng, ngx = t // TQ, s // TQ
    q_spec, kv_spec, b_spec = _na_specs(t, ngx)

    def body(q_ref, k_ref, v_ref, do_ref, b_ref, dq_ref, dk_hbm, dv_hbm, db_ref, dk_acc, dv_acc):
        hp, g = pl.program_id(0), pl.program_id(1)
        start = _key_start(g, r_grid)

        @pl.when(g == 0)
        def _():
            dk_acc[...] = jnp.zeros_like(dk_acc)
            dv_acc[...] = jnp.zeros_like(dv_acc)

        @pl.when((g == 0) | (g == 1) | (g == ngx - 1) | (g == ngx))
        def _():
            db_ref[...] = jnp.zeros_like(db_ref)

        qf = q_ref[...].astype(F32) * (HEAD_DIM ** -0.5)
        do = do_ref[...].astype(F32)
        knb, vnb = k_ref[pl.ds(start, TK), :], v_ref[pl.ds(start, TK), :]
        kcx, vcx = k_ref[pl.ds(s, ctx_len), :], v_ref[pl.ds(s, ctx_len), :]
        dq = jnp.zeros((TQ, LANES), F32)
        dk_nb = jnp.zeros((TK, LANES), F32)
        dv_nb = jnp.zeros((TK, LANES), F32)
        dk_cx = jnp.zeros((ctx_len, LANES), F32)
        dv_cx = jnp.zeros((ctx_len, LANES), F32)
        for h in range(2):
            mask = _head_mask(h)
            qm = (qf * mask).astype(MXU)
            dom = (do * mask).astype(MXU)
            p_nb, p_cx = _softmax_parts(qm, knb, kcx, b_ref[0, h])
            dp_nb = _nt(dom, vnb)
            dp_cx = _nt(dom, vcx)
            delta = jnp.sum(p_nb * dp_nb, axis=-1, keepdims=True) + jnp.sum(p_cx * dp_cx, axis=-1, keepdims=True)
            ds_nb = p_nb * (dp_nb - delta)
            ds_cx = p_cx * (dp_cx - delta)
            db_ref[0, h] += ds_nb
            ds_nb, ds_cx = ds_nb.astype(MXU), ds_cx.astype(MXU)
            dq += (_nn(ds_nb, knb) + _nn(ds_cx, kcx)) * (mask * (HEAD_DIM ** -0.5))
            dk_nb += _tn(ds_nb, qm)
            dk_cx += _tn(ds_cx, qm)
            dv_nb += _tn(p_nb.astype(MXU), dom)
            dv_cx += _tn(p_cx.astype(MXU), dom)
        dq_ref[...] = dq
        dk_acc[pl.ds(start, TK), :] += dk_nb
        dv_acc[pl.ds(start, TK), :] += dv_nb
        dk_acc[pl.ds(s, ctx_len), :] += dk_cx
        dv_acc[pl.ds(s, ctx_len), :] += dv_cx

        @pl.when(g == ng - 1)
        def _():
            pltpu.sync_copy(dk_acc, dk_hbm.at[hp])
            pltpu.sync_copy(dv_acc, dv_hbm.at[hp])

    n_pairs = NA_WIDTH // LANES
    hbm = pl.BlockSpec(memory_space=pl.ANY)
    return _call(body, name=name, grid=(n_pairs, ng), in_specs=[q_spec, kv_spec, kv_spec, q_spec, b_spec],
                 out_specs=[q_spec, hbm, hbm, b_spec],
                 out_shape=[_sds((t, NA_WIDTH), F32), _sds((n_pairs, t, LANES), F32), _sds((n_pairs, t, LANES), F32),
                            _sds((4, HEADS, TQ, TK), F32)],
                 scratch=[pltpu.VMEM((t, LANES), F32), pltpu.VMEM((t, LANES), F32)])(q, k, v, do, bias)


def _rpb_grad(dbias, r_grid, name):
    rows, _, _ = _na_geometry(r_grid)
    n_blk = 3 * Q_ROWS * K_ROWS
    z = dbias[:3].reshape(3, HEADS, Q_ROWS, GRID_W, K_ROWS, GRID_W)
    z = jnp.transpose(z, (1, 0, 2, 4, 3, 5)).reshape(HEADS, n_blk, GRID_W, GRID_W)
    z = jnp.pad(z[:, :, ::-1, :], ((0, 0), (0, 0), (0, 0), (0, GRID_W))).reshape(HEADS, n_blk, 2 * GRID_W * GRID_W)
    skew = 2 * GRID_W - 1
    z = jnp.pad(z, ((0, 0), (0, 0), (0, (GRID_W + 1) * skew - 2 * GRID_W * GRID_W))).reshape(HEADS, n_blk, GRID_W + 1, skew)
    z = jnp.pad(z, ((0, 0), (0, 0), (0, 72 - (GRID_W + 1)), (0, 1)))
    members = [[] for _ in range(2 * WIN_H - 1)]
    for vi, (dr, vr) in enumerate(rows):
        for a in range(Q_ROWS):
            for i in range(K_ROWS):
                if vr[a, i]:
                    members[dr[a, i]].append((vi * Q_ROWS + a) * K_ROWS + i)

    def body(z_ref, o_ref):
        zs = jnp.sum(z_ref[0], axis=1)
        out = []
        for mem in members:
            acc = jnp.zeros((1, LANES), F32)
            for j in mem:
                acc = acc + zs[j:j + 1, :]
            out.append(acc)
        out.append(jnp.zeros((1, LANES), F32))
        o_ref[0] = jnp.concatenate(out, axis=0)

    o = _call(body, name=name, grid=(HEADS,), in_specs=[pl.BlockSpec((1, n_blk, 72, LANES), lambda h: (h, 0, 0, 0))],
              out_specs=pl.BlockSpec((1, 16, LANES), lambda h: (h, 0, 0)), out_shape=_sds((HEADS, 16, LANES), F32))(z)
    off = GRID_W - 1 - (WIN_W - 1)
    return o[:, :2 * WIN_H - 1, off:off + 2 * WIN_W - 1]


_GELU_K, _GELU_C = 0.7978845608028654, 0.044715


def _gelu(x):
    return 0.5 * x * (1.0 + jnp.tanh(_GELU_K * (x + _GELU_C * x * x * x)))


def _gelu_grad(x):
    th = jnp.tanh(_GELU_K * (x + _GELU_C * x * x * x))
    return 0.5 * (1.0 + th) + 0.5 * x * (1.0 - th * th) * (_GELU_K * (1.0 + 3.0 * _GELU_C * x * x))


def _ln_stats(v):
    mu = jnp.mean(v, axis=-1, keepdims=True)
    vc = v - mu
    rstd = lax.rsqrt(jnp.mean(vc * vc, axis=-1, keepdims=True) + EPS)
    return vc * rstd, rstd


def _gmlp(p, ln_g, ln_b, w_s, b_s, name):
    t = p.shape[0]
    w = SG_WIDTH
    cw = w // SG_GROUPS

    def body(u_ref, v_ref, g_ref, b_ref, ws_ref, bs_ref, o_ref):
        xhat, _ = _ln_stats(_gelu(v_ref[...].astype(F32)))
        vn = (xhat * g_ref[...] + b_ref[...]).astype(MXU)
        ug = _gelu(u_ref[...].astype(F32))
        for ci in range(TM // SG_CHUNK):
            rs = slice(ci * SG_CHUNK, (ci + 1) * SG_CHUNK)
            for gi in range(SG_GROUPS):
                cs = slice(gi * cw, (gi + 1) * cw)
                sg = _nn(ws_ref[gi].astype(MXU), vn[rs, cs]) + bs_ref[gi]
                o_ref[rs, cs] = (ug[rs, cs] * sg).astype(o_ref.dtype)

    return _call(body, name=name, grid=(t // TM,),
                 in_specs=[_rows(TM, w, 3), _rows(TM, w, 4), _fixed((1, w)), _fixed((1, w)),
                           _fixed((SG_GROUPS, SG_CHUNK, SG_CHUNK)), _fixed((SG_GROUPS, SG_CHUNK, 1))],
                 out_specs=_rows(TM, w), out_shape=_sds((t, w), MXU))(p, p, ln_g, ln_b, w_s, b_s)


def _gmlp_bwd(p, dob, ln_g, ln_b, w_s, b_s, name):
    t = p.shape[0]
    w = SG_WIDTH
    cw = w // SG_GROUPS

    def body(u_ref, v_ref, do_ref, g_ref, b_ref, ws_ref, bs_ref, du_ref, dv_ref, dws_ref, dbs_ref, dg_ref, db_ref, dvn_ref):
        @pl.when(pl.program_id(0) == 0)
        def _():
            dws_ref[...] = jnp.zeros_like(dws_ref)
            dbs_ref[...] = jnp.zeros_like(dbs_ref)
            dg_ref[...] = jnp.zeros_like(dg_ref)
            db_ref[...] = jnp.zeros_like(db_ref)

        u, v = u_ref[...].astype(F32), v_ref[...].astype(F32)
        xhat, rstd = _ln_stats(_gelu(v))
        vn = (xhat * g_ref[...] + b_ref[...]).astype(MXU)
        ug = _gelu(u)
        dob = do_ref[...].astype(F32)
        for ci in range(TM // SG_CHUNK):
            rs = slice(ci * SG_CHUNK, (ci + 1) * SG_CHUNK)
            for gi in range(SG_GROUPS):
                cs = slice(gi * cw, (gi + 1) * cw)
                wsg = ws_ref[gi].astype(MXU)
                sg = _nn(wsg, vn[rs, cs]) + bs_ref[gi]
                du_ref[rs, cs] = (dob[rs, cs] * sg * _gelu_grad(u[rs, cs])).astype(du_ref.dtype)
                ds = dob[rs, cs] * ug[rs, cs]
                dbs_ref[gi] += jnp.sum(ds, axis=-1, keepdims=True)
                ds = ds.astype(MXU)
                dws_ref[gi] += _nt(ds, vn[rs, cs])
                dvn_ref[rs, cs] = _tn(wsg, ds)
        dvn = dvn_ref[...]
        dg_ref[...] += jnp.sum(dvn * xhat, axis=0, keepdims=True)
        db_ref[...] += jnp.sum(dvn, axis=0, keepdims=True)
        dxh = dvn * g_ref[...]
        dvg = rstd * (dxh - jnp.mean(dxh, axis=-1, keepdims=True) - xhat * jnp.mean(dxh * xhat, axis=-1, keepdims=True))
        dv_ref[...] = (dvg * _gelu_grad(v)).astype(dv_ref.dtype)

    return _call(body, name=name, grid=(t // TM,),
                 in_specs=[_rows(TM, w, 3), _rows(TM, w, 4), _rows(TM, w), _fixed((1, w)), _fixed((1, w)),
                           _fixed((SG_GROUPS, SG_CHUNK, SG_CHUNK)), _fixed((SG_GROUPS, SG_CHUNK, 1))],
                 out_specs=[_rows(TM, w), _rows(TM, w), _fixed((SG_GROUPS, SG_CHUNK, SG_CHUNK)),
                            _fixed((SG_GROUPS, SG_CHUNK, 1)), _fixed((1, w)), _fixed((1, w))],
                 out_shape=[_sds((t, w), MXU), _sds((t, w), MXU), _sds((SG_GROUPS, SG_CHUNK, SG_CHUNK), F32),
                            _sds((SG_GROUPS, SG_CHUNK, 1), F32), _sds((1, w), F32), _sds((1, w), F32)],
                 scratch=[pltpu.VMEM((TM, w), F32)])(p, p, dob, ln_g, ln_b, w_s, b_s)


def _merge(pa, pb, p, b_gate, name):
    t, d = pa.shape
    hw = NA_WIDTH
    nh = d // hw
    c0 = (NA_WIDTH * 3 + SG_WIDTH * 2) // hw

    def body(pa_ref, pb_ref, la_ref, lb_ref, ba_ref, bb_ref, o_ref):
        ga = jax.nn.sigmoid(la_ref[...].astype(F32) + ba_ref[...])
        gb = jax.nn.sigmoid(lb_ref[...].astype(F32) + bb_ref[...])
        o_ref[...] = (ga * pa_ref[...].astype(F32) + gb * pb_ref[...].astype(F32)).astype(o_ref.dtype)

    tile = pl.BlockSpec((TM, hw), lambda i, j: (i, j))
    return _call(body, name=name, grid=(t // TM, nh),
                 in_specs=[tile, tile, pl.BlockSpec((TM, hw), lambda i, j: (i, c0 + j)),
                           pl.BlockSpec((TM, hw), lambda i, j: (i, c0 + nh + j)),
                           pl.BlockSpec((1, hw), lambda i, j: (0, j)), pl.BlockSpec((1, hw), lambda i, j: (0, nh + j))],
                 out_specs=tile, out_shape=_sds((t, d), MXU))(pa, pb, p, p, b_gate, b_gate)


def _merge_bwd(dmg, pa, pb, p, b_gate, name):
    t, d = pa.shape
    hw = NA_WIDTH
    nh = d // hw
    c0 = (NA_WIDTH * 3 + SG_WIDTH * 2) // hw

    def body(dm_ref, pa_ref, pb_ref, la_ref, lb_ref, ba_ref, bb_ref, dpa_ref, dpb_ref, dla_ref, dlb_ref, dba_ref, dbb_ref):
        @pl.when(pl.program_id(1) == 0)
        def _():
            dba_ref[...] = jnp.zeros_like(dba_ref)
            dbb_ref[...] = jnp.zeros_like(dbb_ref)

        dm = dm_ref[...].astype(F32)
        ga = jax.nn.sigmoid(la_ref[...].astype(F32) + ba_ref[...])
        gb = jax.nn.sigmoid(lb_ref[...].astype(F32) + bb_ref[...])
        dpa_ref[...] = (dm * ga).astype(dpa_ref.dtype)
        dpb_ref[...] = (dm * gb).astype(dpb_ref.dtype)
        dla = dm * pa_ref[...].astype(F32) * ga * (1.0 - ga)
        dlb = dm * pb_ref[...].astype(F32) * gb * (1.0 - gb)
        dla_ref[...] = dla.astype(dla_ref.dtype)
        dlb_ref[...] = dlb.astype(dlb_ref.dtype)
        dba_ref[...] += jnp.sum(dla, axis=0, keepdims=True)
        dbb_ref[...] += jnp.sum(dlb, axis=0, keepdims=True)

    tile = pl.BlockSpec((TM, hw), lambda j, i: (i, j))
    bias_a = pl.BlockSpec((1, hw), lambda j, i: (0, j))
    bias_b = pl.BlockSpec((1, hw), lambda j, i: (0, nh + j))
    return _call(body, name=name, grid=(nh, t // TM),
                 in_specs=[tile, tile, tile, pl.BlockSpec((TM, hw), lambda j, i: (i, c0 + j)),
                           pl.BlockSpec((TM, hw), lambda j, i: (i, c0 + nh + j)), bias_a, bias_b],
                 out_specs=[tile, tile, tile, tile, bias_a, bias_a],
                 out_shape=[_sds((t, d), MXU)] * 4 + [_sds((1, d), F32)] * 2)(dmg, pa, pb, p, p, b_gate, b_gate)


def _final(xs, tgt, g, name):
    t, d = xs.shape
    nx = tgt.shape[0] // TM

    def body(x_ref, t_ref, g_ref, l_ref, dx_ref, dg_ref):
        i = pl.program_id(0)

        @pl.when(i == 0)
        def _():
            l_ref[...] = jnp.zeros_like(l_ref)
            dg_ref[...] = jnp.zeros_like(dg_ref)

        @pl.when(i < nx)
        def _():
            x = x_ref[...]
            rstd = lax.rsqrt(jnp.mean(x * x, axis=-1, keepdims=True) + EPS)
            xhat = x * rstd
            err = xhat * g_ref[...] - t_ref[...]
            l_ref[...] += 0.5 * jnp.sum(jnp.mean(err * err, axis=-1, keepdims=True))
            dy = err * (1.0 / d)
            dg_ref[...] += jnp.sum(dy * xhat, axis=0, keepdims=True)
            dxh = dy * g_ref[...]
            dx_ref[...] = rstd * (dxh - xhat * jnp.mean(dxh * xhat, axis=-1, keepdims=True))

        @pl.when(i >= nx)
        def _():
            dx_ref[...] = jnp.zeros_like(dx_ref)

    return _call(body, name=name, grid=(t // TM,),
                 in_specs=[_rows(TM, d), pl.BlockSpec((TM, d), lambda i: (jnp.minimum(i, nx - 1), 0)), _fixed((1, d))],
                 out_specs=[_fixed((1, LANES)), _rows(TM, d), _fixed((1, d))],
                 out_shape=[_sds((1, LANES), F32), _sds((t, d), F32), _sds((1, d), F32)])(xs, tgt, g)


def _view2d(a):
    return a.reshape(1, -1) if a.ndim == 1 else a.reshape(-1, a.shape[-1])


def _ew(fn, arrays, out_dtypes, name):
    shape = arrays[0].shape
    views = [_view2d(a) for a in arrays]
    r, c = views[0].shape
    tr = r
    for cand in (1024, 512, 256, 128, 64, 32, 16):
        if r % cand == 0 and cand * c * 4 <= 2 ** 20:
            tr = cand
            break

    def body(*refs):
        outs = fn(*[ref[...] for ref in refs[:len(views)]])
        for ref, o in zip(refs[len(views):], outs):
            ref[...] = o.astype(ref.dtype)

    res = _call(body, name=name, grid=(r // tr,), in_specs=[_rows(tr, c)] * len(views), out_specs=[_rows(tr, c)] * len(out_dtypes),
                out_shape=[_sds((r, c), dt) for dt in out_dtypes])(*views)
    return [o.reshape(shape) for o in res]


def _sum_pieces(pieces, name, out_dtypes=(F32,)):
    def fn(*vals):
        acc = vals[0].astype(F32)
        for v in vals[1:]:
            acc = acc + v.astype(F32)
        return (acc,) * len(out_dtypes)

    return _ew(fn, pieces, list(out_dtypes), name)


def _adamw(w, g_pieces, m, v, name):
    n_g = len(g_pieces)

    def fn(w_, *rest):
        g = rest[0]
        for piece in rest[1:n_g]:
            g = g + piece
        m_, v_ = rest[n_g], rest[n_g + 1]
        m2 = ADAM_B1 * m_ + (1.0 - ADAM_B1) * g
        v2 = ADAM_B2 * v_ + (1.0 - ADAM_B2) * (g * g)
        m_hat = m2 / (1.0 - ADAM_B1 ** ADAM_STEP)
        v_hat = v2 / (1.0 - ADAM_B2 ** ADAM_STEP)
        delta = -ADAM_LR * (m_hat / (jnp.sqrt(v_hat) + ADAM_EPS) + ADAM_WD * w_)
        return g, delta, m2, v2

    return _ew(fn, [w, *g_pieces, m, v], [F32] * 4, name)


def _ada_fwd(cond, w, b, name):
    r, d = cond.shape
    n = w.shape[1]
    tn = _pick(n, (1152, 768, 512, 384, 256, 128))

    def body(c_ref, w_ref, b_ref, o_ref, s_ref):
        c = c_ref[...]
        sc = c * jax.nn.sigmoid(c)
        s_ref[...] = sc
        o_ref[...] = _nn(sc.astype(MXU), w_ref[...].astype(MXU)) + b_ref[...]

    return _call(body, name=name, grid=(n // tn,),
                 in_specs=[_fixed((r, d)), pl.BlockSpec((d, tn), lambda j: (0, j)), pl.BlockSpec((1, tn), lambda j: (0, j))],
                 out_specs=[pl.BlockSpec((r, tn), lambda j: (0, j)), _fixed((r, d))],
                 out_shape=[_sds((r, n), F32), _sds((r, d), F32)])(cond, w, b)


def _cctx_grad(parts, c_ctx, name):
    n, d = parts.shape

    def body(p_ref, c_ref, o_ref):
        c = c_ref[...]
        sg = jax.nn.sigmoid(c)
        acc = p_ref[0:1, :]
        for j in range(1, n):
            acc = acc + p_ref[j:j + 1, :]
        o_ref[...] = acc * (sg * (1.0 + c * (1.0 - sg)))

    return _call(body, name=name, grid=(1,), in_specs=[_fixed((n, d)), _fixed((1, d))], out_specs=_fixed((1, d)),
                 out_shape=_sds((1, d), F32))(parts, c_ctx)


def _here():
    return lax.axis_index("x"), lax.axis_index("y"), lax.axis_index("c")


def _flip(v, bit):
    return 1 - v if bit else v


def _allgather8(xb, name):
    r, n = xb.shape

    def body(x_ref, out_ref, send_sems, recv_sems, local_sem):
        x, y, c = _here()
        me = 4 * x + 2 * y + c
        local = pltpu.make_async_copy(x_ref, out_ref.at[me], local_sem)
        local.start()
        sends = []
        for k in range(1, 8):
            peer = (_flip(x, k & 4), _flip(y, k & 2), _flip(c, k & 1))
            cp = pltpu.make_async_remote_copy(src_ref=x_ref, dst_ref=out_ref.at[me], send_sem=send_sems.at[k - 1],
                                              recv_sem=recv_sems.at[k - 1], device_id=peer, device_id_type=MESH)
            cp.start()
            sends.append(cp)
        for k in range(1, 8):
            peer = (_flip(x, k & 4), _flip(y, k & 2), _flip(c, k & 1))
            src = 4 * peer[0] + 2 * peer[1] + peer[2]
            pltpu.make_async_remote_copy(src_ref=x_ref, dst_ref=out_ref.at[src], send_sem=send_sems.at[k - 1],
                                         recv_sem=recv_sems.at[k - 1], device_id=peer, device_id_type=MESH).wait_recv()
        for cp in sends:
            cp.wait_send()
        local.wait()

    vmem = pl.BlockSpec(memory_space=pltpu.VMEM)
    return pl.pallas_call(
        body, name=name, out_shape=_sds((8, r, n), xb.dtype), in_specs=[vmem], out_specs=vmem,
        scratch_shapes=[pltpu.SemaphoreType.DMA((7,)), pltpu.SemaphoreType.DMA((7,)), pltpu.SemaphoreType.DMA(())],
        compiler_params=pltpu.CompilerParams(vmem_limit_bytes=VMEM_LIMIT))(xb)


def _shard_of(ref, axis, j, size):
    sl = pl.ds(j * size, size)
    return ref.at[:, sl, :] if axis == 1 else ref.at[:, :, sl]


def _piece(ref, axis, j, size, layer):
    lay, sl = pl.ds(layer, 1), pl.ds(j * size, size)
    return ref.at[lay, sl, :] if axis == 1 else ref.at[lay, :, sl]


def _gather_chips(shards, axes, name):
    n = len(shards)
    fulls = []
    for a, ax in zip(shards, axes):
        assert a.shape[0] == 2
        shp = list(a.shape)
        shp[ax] *= 4
        fulls.append(_sds(tuple(shp), a.dtype))

    def body(*refs):
        ins, outs = refs[:n], refs[n:2 * n]
        ici_send, ici_recv, d2d_send, d2d_recv, local_sems = refs[2 * n:]
        x, y, c = _here()
        chips = [(_flip(x, k & 2), _flip(y, k & 1)) for k in range(1, 4)]
        local, sends = [], []
        for a in range(n):
            size = ins[a].shape[axes[a]]
            cp = pltpu.make_async_copy(ins[a], _shard_of(outs[a], axes[a], 2 * x + y, size), local_sems.at[a])
            cp.start()
            local.append(cp)
            for j, (px, py) in enumerate(chips):
                cp = pltpu.make_async_remote_copy(src_ref=ins[a].at[pl.ds(c, 1)], dst_ref=_piece(outs[a], axes[a], 2 * x + y, size, c),
                                                  send_sem=ici_send.at[3 * a + j], recv_sem=ici_recv.at[3 * a + j],
                                                  device_id=(px, py, c), device_id_type=MESH)
                cp.start()
                sends.append(cp)
        for a in range(n):
            size = ins[a].shape[axes[a]]
            for j, (px, py) in enumerate(chips):
                landed = _piece(outs[a], axes[a], 2 * px + py, size, c)
                pltpu.make_async_remote_copy(src_ref=ins[a].at[pl.ds(c, 1)], dst_ref=landed, send_sem=ici_send.at[3 * a + j],
                                             recv_sem=ici_recv.at[3 * a + j], device_id=(px, py, c), device_id_type=MESH).wait_recv()
                cp = pltpu.make_async_remote_copy(src_ref=landed, dst_ref=landed, send_sem=d2d_send.at[3 * a + j],
                                                  recv_sem=d2d_recv.at[3 * a + j], device_id=(x, y, 1 - c), device_id_type=MESH)
                cp.start()
                sends.append(cp)
        for a in range(n):
            size = ins[a].shape[axes[a]]
            for j, (px, py) in enumerate(chips):
                passed = _piece(outs[a], axes[a], 2 * px + py, size, 1 - c)
                pltpu.make_async_remote_copy(src_ref=passed, dst_ref=passed, send_sem=d2d_send.at[3 * a + j],
                                             recv_sem=d2d_recv.at[3 * a + j], device_id=(x, y, 1 - c), device_id_type=MESH).wait_recv()
        for cp in sends:
            cp.wait_send()
        for cp in local:
            cp.wait()

    hbm = pl.BlockSpec(memory_space=pl.ANY)
    return pl.pallas_call(
        body, name=name, out_shape=fulls, in_specs=[hbm] * n, out_specs=[hbm] * n,
        scratch_shapes=[pltpu.SemaphoreType.DMA((3 * n,))] * 4 + [pltpu.SemaphoreType.DMA((n,))])(*shards)


def _swap_layers(arrays, name):
    n = len(arrays)

    def body(*refs):
        ins, outs = refs[:n], refs[n:2 * n]
        send_sems, recv_sems = refs[2 * n:]
        x, y, c = _here()
        copies = []
        for a in range(n):
            cp = pltpu.make_async_remote_copy(src_ref=ins[a].at[pl.ds(1 - c, 1)], dst_ref=outs[a], send_sem=send_sems.at[a],
                                              recv_sem=recv_sems.at[a], device_id=(x, y, 1 - c), device_id_type=MESH)
            cp.start()
            copies.append(cp)
        for cp in copies:
            cp.wait()

    hbm = pl.BlockSpec(memory_space=pl.ANY)
    return pl.pallas_call(
        body, name=name, out_shape=[_sds((1, *a.shape[1:]), a.dtype) for a in arrays], in_specs=[hbm] * n, out_specs=[hbm] * n,
        scratch_shapes=[pltpu.SemaphoreType.DMA((n,)), pltpu.SemaphoreType.DMA((n,))])(*arrays)


def _scatter_chips(fulls, axes, name):
    n = len(fulls)
    recvs = []
    for a, ax in zip(fulls, axes):
        shp = list(a.shape)
        shp[ax] //= 4
        recvs.append(_sds((3, *shp), a.dtype))

    def body(*refs):
        ins, outs = refs[:n], refs[n:2 * n]
        send_sems, recv_sems = refs[2 * n:]
        x, y, c = _here()
        sends = []
        for a in range(n):
            size = ins[a].shape[axes[a]] // 4
            for k in range(1, 4):
                peer = (_flip(x, k & 2), _flip(y, k & 1), c)
                cp = pltpu.make_async_remote_copy(src_ref=_shard_of(ins[a], axes[a], 2 * peer[0] + peer[1], size),
                                                  dst_ref=outs[a].at[k - 1],
                                                  send_sem=send_sems.at[3 * a + k - 1], recv_sem=recv_sems.at[3 * a + k - 1],
                                                  device_id=peer, device_id_type=MESH)
                cp.start()
                sends.append(cp)
        for cp in sends:
            cp.wait_recv()
        for cp in sends:
            cp.wait_send()

    hbm = pl.BlockSpec(memory_space=pl.ANY)
    return pl.pallas_call(
        body, name=name, out_shape=recvs, in_specs=[hbm] * n, out_specs=[hbm] * n,
        scratch_shapes=[pltpu.SemaphoreType.DMA((3 * n,)), pltpu.SemaphoreType.DMA((3 * n,))])(*fulls)


def _sibling_swap(arrays, name):
    n = len(arrays)

    def body(*refs):
        ins, outs = refs[:n], refs[n:2 * n]
        send_sems, recv_sems = refs[2 * n:]
        x, y, c = _here()
        copies = []
        for a in range(n):
            cp = pltpu.make_async_remote_copy(src_ref=ins[a], dst_ref=outs[a], send_sem=send_sems.at[a], recv_sem=recv_sems.at[a],
                                              device_id=(x, y, 1 - c), device_id_type=MESH)
            cp.start()
            copies.append(cp)
        for cp in copies:
            cp.wait()

    hbm = pl.BlockSpec(memory_space=pl.ANY)
    return pl.pallas_call(
        body, name=name, out_shape=[_sds(a.shape, a.dtype) for a in arrays], in_specs=[hbm] * n, out_specs=[hbm] * n,
        scratch_shapes=[pltpu.SemaphoreType.DMA((n,)), pltpu.SemaphoreType.DMA((n,))])(*arrays)


def _ffn_fwd(xs, g, mods, k0, w_up, w_down, nx, tag):
    h = _normmod(xs, g, mods, k0, k0 + 1, nx, tag + "_norm")
    up = _mm(h, w_up, "nn", MXU, tag + "_up")
    act = _swiglu(up, tag + "_act")
    y = _mm(act, w_down, "nn", MXU, tag + "_down")
    return _resid(xs, y, mods, k0 + 2, 0.5, nx, tag + "_res"), (xs, h, up, act, y)


def _ffn_bwd(dxn, saved, g, mods, k0, w_up, w_down, nx, tag):
    xs, h, up, act, y = saved
    dy, dgate = _resid_bwd(dxn, y, mods, k0 + 2, 0.5, nx, tag + "_res_b")
    d_down = _mm(act, dy, "tn", F32, tag + "_down_dw")
    dact = _mm(dy, w_down, "nt", MXU, tag + "_down_dx")
    dup = _swiglu_bwd(up, dact, tag + "_act_b")
    d_up = _mm(h, dup, "tn", F32, tag + "_up_dw")
    dh = _mm(dup, w_up, "nt", MXU, tag + "_up_dx")
    dx, dsh, dsc, dg = _normmod_bwd(xs, dh, dxn, g, mods, k0 + 1, nx, tag + "_norm_b")
    return dx, d_up, d_down, dg, [dsh, dsc, dgate]


def _mix_fwd(xs, g, mods, wl, pl_, tabs, s, nx, tag):
    cos, sin = tabs
    h = _normmod(xs, g, mods, 3, 4, nx, tag + "_norm")
    p = _mm(h, wl["w_in"], "nn", MXU, tag + "_in")
    q, k, v = _rope(p, cos, sin, tag + "_rope")
    bias = _bias_table(pl_["rpb"], s // GRID_W)
    oa = _na_fwd(q, k, v, bias, s, tag + "_na")
    ob = _gmlp(p, pl_["ln_v_g"], pl_["ln_v_b"], pl_["w_s"], pl_["b_s"], tag + "_sg")
    pa = _mm(oa, wl["w_pa"], "nn", MXU, tag + "_pa")
    pb = _mm(ob, wl["w_pb"], "nn", MXU, tag + "_pb")
    mg = _merge(pa, pb, p, pl_["b_gate"], tag + "_merge")
    y = _mm(mg, wl["w_o"], "nn", MXU, tag + "_o")
    return _resid(xs, y, mods, 5, 1.0, nx, tag + "_res"), (xs, h, p, q, k, v, bias, oa, ob, pa, pb, mg, y)


def _mix_bwd(dxn, saved, g, mods, wl, pl_, tabs, s, nx, tag):
    xs, h, p, q, k, v, bias, oa, ob, pa, pb, mg, y = saved
    cos, sin = tabs
    gw, gp = {}, {}
    dy, dgate = _resid_bwd(dxn, y, mods, 5, 1.0, nx, tag + "_res_b")
    gw["w_o"] = _mm(mg, dy, "tn", F32, tag + "_o_dw")
    dmg = _mm(dy, wl["w_o"], "nt", MXU, tag + "_o_dx")
    dpa, dpb, dla, dlb, dba, dbb = _merge_bwd(dmg, pa, pb, p, pl_["b_gate"], tag + "_merge_b")
    gp["b_gate"] = jnp.concatenate([dba, dbb], axis=1)
    gw["w_pa"] = _mm(oa, dpa, "tn", F32, tag + "_pa_dw")
    doa = _mm(dpa, wl["w_pa"], "nt", MXU, tag + "_pa_dx")
    gw["w_pb"] = _mm(ob, dpb, "tn", F32, tag + "_pb_dw")
    dob = _mm(dpb, wl["w_pb"], "nt", MXU, tag + "_pb_dx")
    du, dvs, gp["w_s"], gp["b_s"], gp["ln_v_g"], gp["ln_v_b"] = _gmlp_bwd(
        p, dob, pl_["ln_v_g"], pl_["ln_v_b"], pl_["w_s"], pl_["b_s"], tag + "_sg_b")
    dqr, dkr, dv, dbias = _na_bwd(q, k, v, doa, bias, s, tag + "_na_b")
    gp["rpb"] = _rpb_grad(dbias, s // GRID_W, tag + "_rpb")
    dq, dk, dvv = _rope_bwd(dqr, dkr, dv, cos, sin, tag + "_rope_b")
    dp = jnp.concatenate([dq, dk, dvv, du, dvs, dla, dlb], axis=1)
    gw["w_in"] = _mm(h, dp, "tn", F32, tag + "_in_dw")
    dh = _mm(dp, wl["w_in"], "nt", MXU, tag + "_in_dx")
    dx, dsh, dsc, dg = _normmod_bwd(xs, dh, dxn, g, mods, 4, nx, tag + "_norm_b")
    return dx, gw, gp, dg, [dsh, dsc, dgate]


def _local_step(x, ctx, tgt, mods, wts, prm):
    s, d = x.shape
    depth = mods.shape[0]
    nx = s // TM
    tabs = _rope_tables(s, ctx.shape[0])
    xs = jnp.concatenate([x, ctx], axis=0)
    saved = []
    for l in range(depth):
        wl = {k: v[l] for k, v in wts.items()}
        pl_ = _layer_params(prm, l)
        xs, s1 = _ffn_fwd(xs, pl_["g"][0], mods[l], 0, wl["w_ff1_up"], wl["w_ff1_down"], nx, f"l{l}_ff1")
        xs, s2 = _mix_fwd(xs, pl_["g"][1], mods[l], wl, pl_, tabs, s, nx, f"l{l}_mix")
        xs, s3 = _ffn_fwd(xs, pl_["g"][2], mods[l], 6, wl["w_ff2_up"], wl["w_ff2_down"], nx, f"l{l}_ff2")
        saved.append((s1, s2, s3))
    loss, dxs, d_final_g = _final(xs, tgt, prm["final_g"].reshape(1, d), "final")
    gw = {k: [None] * depth for k in wts}
    gp = {k: [None] * depth for k in ("norm_g", "b_gate", "rpb", "ln_v_g", "ln_v_b", "w_s", "b_s")}
    dmods = [None] * depth
    for l in reversed(range(depth)):
        wl = {k: v[l] for k, v in wts.items()}
        pl_ = _layer_params(prm, l)
        s1, s2, s3 = saved[l]
        dxs, gw["w_ff2_up"][l], gw["w_ff2_down"][l], dg2, dm2 = _ffn_bwd(
            dxs, s3, pl_["g"][2], mods[l], 6, wl["w_ff2_up"], wl["w_ff2_down"], nx, f"l{l}_ff2")
        dxs, gwm, gpm, dg1, dm1 = _mix_bwd(dxs, s2, pl_["g"][1], mods[l], wl, pl_, tabs, s, nx, f"l{l}_mix")
        dxs, gw["w_ff1_up"][l], gw["w_ff1_down"][l], dg0, dm0 = _ffn_bwd(
            dxs, s1, pl_["g"][0], mods[l], 0, wl["w_ff1_up"], wl["w_ff1_down"], nx, f"l{l}_ff1")
        for k, v in gwm.items():
            gw[k][l] = v
        gp["b_gate"][l] = gpm["b_gate"][0]
        gp["rpb"][l] = gpm["rpb"]
        gp["ln_v_g"][l] = gpm["ln_v_g"][0]
        gp["ln_v_b"][l] = gpm["ln_v_b"][0]
        gp["w_s"][l] = gpm["w_s"]
        gp["b_s"][l] = gpm["b_s"][..., 0]
        gp["norm_g"][l] = jnp.concatenate([dg0, dg1, dg2], axis=0)
        dmods[l] = jnp.concatenate(dm0 + dm1 + dm2, axis=1)
    gw = {k: jnp.stack(v) for k, v in gw.items()}
    gp = {k: jnp.stack(v) for k, v in gp.items()}
    gp["final_g"] = d_final_g[0]
    return loss[0, 0], dxs[:s], jnp.stack(dmods), gw, gp


def _layer_params(prm, l):
    d = prm["norm_g"].shape[-1]
    return {
        "g": [prm["norm_g"][l, i].reshape(1, d) for i in range(3)],
        "b_gate": prm["b_gate"][l].reshape(1, -1),
        "rpb": prm["rpb"][l],
        "ln_v_g": prm["ln_v_g"][l].reshape(1, -1),
        "ln_v_b": prm["ln_v_b"][l].reshape(1, -1),
        "w_s": prm["w_s"][l],
        "b_s": prm["b_s"][l][..., None],
    }


SMALL = ("norm_g", "b_gate", "rpb", "ln_v_g", "ln_v_b", "w_s", "b_s", "final_g")
PACK_LANES = 1024


def _pack(parts):
    flat = jnp.concatenate([p.reshape(-1) for p in parts])
    rows = -(-flat.shape[0] // PACK_LANES)
    rows = -(-rows // 8) * 8
    return jnp.pad(flat, (0, rows * PACK_LANES - flat.shape[0])).reshape(rows, PACK_LANES)


def _unpack(flat, shapes):
    out, off = [], 0
    for shp in shapes:
        n = int(np.prod(shp))
        out.append(flat[..., off:off + n].reshape(*flat.shape[:-1], *shp))
        off += n
    return out


def kernel(x, c, ctx, c_ctx, w_ada, b_ada, norm_g, w_ff1_up, w_ff1_down, w_in, b_gate, rpb, ln_v_g, ln_v_b, w_s, b_s, w_pa, w_pb, w_o, w_ff2_up, w_ff2_down, final_g, loss_target, m_c_ctx, m_w_ada, m_b_ada, m_norm_g, m_w_ff1_up, m_w_ff1_down, m_w_in, m_b_gate, m_rpb, m_ln_v_g, m_ln_v_b, m_w_s, m_b_s, m_w_pa, m_w_pb, m_w_o, m_w_ff2_up, m_w_ff2_down, m_final_g, v_c_ctx, v_w_ada, v_b_ada, v_norm_g, v_w_ff1_up, v_w_ff1_down, v_w_in, v_b_gate, v_rpb, v_ln_v_g, v_ln_v_b, v_w_s, v_b_s, v_w_pa, v_w_pb, v_w_o, v_w_ff2_up, v_w_ff2_down, v_final_g):
    weights = dict(c_ctx=c_ctx, w_ada=w_ada, b_ada=b_ada, norm_g=norm_g, w_ff1_up=w_ff1_up, w_ff1_down=w_ff1_down, w_in=w_in,
                   b_gate=b_gate, rpb=rpb, ln_v_g=ln_v_g, ln_v_b=ln_v_b, w_s=w_s, b_s=b_s, w_pa=w_pa, w_pb=w_pb, w_o=w_o,
                   w_ff2_up=w_ff2_up, w_ff2_down=w_ff2_down, final_g=final_g)
    mom_m = dict(c_ctx=m_c_ctx, w_ada=m_w_ada, b_ada=m_b_ada, norm_g=m_norm_g, w_ff1_up=m_w_ff1_up, w_ff1_down=m_w_ff1_down,
                 w_in=m_w_in, b_gate=m_b_gate, rpb=m_rpb, ln_v_g=m_ln_v_g, ln_v_b=m_ln_v_b, w_s=m_w_s, b_s=m_b_s, w_pa=m_w_pa,
                 w_pb=m_w_pb, w_o=m_w_o, w_ff2_up=m_w_ff2_up, w_ff2_down=m_w_ff2_down, final_g=m_final_g)
    mom_v = dict(c_ctx=v_c_ctx, w_ada=v_w_ada, b_ada=v_b_ada, norm_g=v_norm_g, w_ff1_up=v_w_ff1_up, w_ff1_down=v_w_ff1_down,
                 w_in=v_w_in, b_gate=v_b_gate, rpb=v_rpb, ln_v_g=v_ln_v_g, ln_v_b=v_ln_v_b, w_s=v_w_s, b_s=v_b_s, w_pa=v_w_pa,
                 w_pb=v_w_pb, w_o=v_w_o, w_ff2_up=v_w_ff2_up, w_ff2_down=v_w_ff2_down, final_g=v_final_g)
    order = list(weights)
    mx, my, mc = _here()
    dev = 4 * mx + 2 * my + mc
    chip = 2 * mx + my
    depth, d, n_ada = w_ada.shape
    dq = d // 4

    c_all = _allgather8(jnp.pad(c, ((0, 7), (0, 0))), "gather_c")[:, 0, :]
    cond = jnp.concatenate([c_all, c_ctx[None, :], jnp.zeros((7, d), F32)], axis=0)
    b_shard = lax.dynamic_slice(b_ada, (0, chip * n_ada), (depth, n_ada))
    proj = [_ada_fwd(cond, w_ada[l], b_shard[l:l + 1], f"ada{l}") for l in range(depth)]
    silu_c = proj[0][1]
    mods_sh = _allgather8(jnp.concatenate([p[0] for p in proj], axis=0), "gather_mods")
    mods_all = jnp.transpose(mods_sh[0::2].reshape(4, depth, 16, n_ada), (1, 2, 0, 3)).reshape(depth, 16, N_MOD, d)
    mods = jnp.stack([lax.dynamic_index_in_dim(mods_all, dev, axis=1, keepdims=False), mods_all[:, 8]], axis=1)

    full = _gather_chips([weights[k].astype(MXU) for k in BIG], [SHARD_AXIS[k] for k in BIG], "gather_w")
    wts = dict(zip(BIG, full))
    prm = {k: weights[k] for k in SMALL if k != "norm_g"}
    norm_full = _allgather8(jnp.pad(norm_g.reshape(depth * 3, dq), ((0, 8 - depth * 3), (0, 0))), "gather_norm_g")
    prm["norm_g"] = jnp.transpose(norm_full[0::2, :depth * 3].reshape(4, depth, 3, dq), (1, 2, 0, 3)).reshape(depth, 3, d)

    loss, grad_x, dmods, gw, gp = _local_step(x[0], ctx[0], loss_target[0], mods, wts, prm)
    loss = lax.psum(loss, ("x", "y", "c"))

    small_shapes = [(depth, 2, N_MOD * d)] + [weights[k].shape if k != "norm_g" else (depth, 3, d) for k in SMALL]
    packed = _allgather8(_pack([dmods.reshape(depth, 2, N_MOD * d)] + [gp[k] for k in SMALL]), "gather_small")
    rows = packed.shape[1]
    total = _sum_pieces([packed[i] for i in range(8)], "sum_small")[0].reshape(-1)
    sums = dict(zip(("dmods",) + SMALL, _unpack(total, small_shapes)))
    dmods_dev = _unpack(packed.reshape(8, rows * PACK_LANES), small_shapes[:1])[0]

    g_ada, cc_parts = [], []
    for l in range(depth):
        dm = jnp.concatenate([dmods_dev[:, l, 0], sums["dmods"][l, 1][None], jnp.zeros((7, N_MOD * d), F32)], axis=0)
        dm_sh = lax.dynamic_slice(dm, (0, chip * n_ada), (16, n_ada))
        g_ada.append(_mm(silu_c, dm_sh, "tn", F32, f"ada{l}_dw"))
        cc_parts.append(_mm(dm_sh, w_ada[l], "nt", F32, f"ada{l}_dc")[8:9])
    cc_all = _allgather8(jnp.pad(jnp.concatenate(cc_parts, axis=0), ((0, 8 - depth), (0, 0))), "gather_cctx")
    g_cctx = _cctx_grad(cc_all[0::2, :depth].reshape(4 * depth, d), c_ctx.reshape(1, d), "cctx_grad")

    axes = [SHARD_AXIS[k] for k in BIG]
    from_sibling = _swap_layers([gw[k] for k in BIG], "swap_layer_gw")
    pair = [_sum_pieces([lax.dynamic_slice_in_dim(gw[k], mc, 1, axis=0), r], "pair_" + k, (F32, MXU)) for k, r in zip(BIG, from_sibling)]
    recv = _scatter_chips([p[1] for p in pair], axes, "scatter_gw")
    mine = []
    for k, ax, p, r in zip(BIG, axes, pair, recv):
        size = p[0].shape[ax] // 4
        own = lax.dynamic_slice_in_dim(p[0], chip * size, size, axis=ax)
        mine.append(_sum_pieces([own, r[0], r[1], r[2]], "sum_" + k)[0])
    other = _sibling_swap(mine, "swap_gw")

    pieces = {k: [jnp.concatenate([jnp.where(mc == 0, a, b), jnp.where(mc == 0, b, a)], axis=0)] for k, a, b in zip(BIG, mine, other)}
    pieces["w_ada"] = [jnp.stack(g_ada)]
    pieces["b_ada"] = [sums["dmods"][:, 0], sums["dmods"][:, 1]]
    pieces["c_ctx"] = [g_cctx[0]]
    for k in SMALL:
        pieces[k] = [sums[k]]
    pieces["norm_g"] = [lax.dynamic_slice_in_dim(sums["norm_g"], chip * dq, dq, axis=2)]
    res = {k: _adamw(weights[k], pieces[k], mom_m[k], mom_v[k], "adamw_" + k) for k in order}
    return (loss, grad_x[None], *[res[k][0] for k in order], *[res[k][1] for k in order],
            *[res[k][2] for k in order], *[res[k][3] for k in order])
```

```python
import numpy as np
import jax
import jax.numpy as jnp
from jax import lax
from jax.experimental import pallas as pl
from jax.experimental.pallas import tpu as pltpu

F32 = jnp.float32
MXU = jnp.bfloat16
EPS = 1e-6
GRID_W, HEADS, HEAD_DIM = 64, 8, 64
NA_WIDTH = SG_WIDTH = 512
WIN_H, WIN_W = 8, 16
SG_CHUNK, SG_GROUPS = 128, 4
N_MOD = 9
ROPE_THETA = 10000.0
Q_ROWS, K_ROWS = 4, 12
TQ, TK = Q_ROWS * GRID_W, K_ROWS * GRID_W
TM = 256
LANES = 128
NEG = -1e30
VMEM_LIMIT = 56 * 2 ** 20
ADAM_LR, ADAM_B1, ADAM_B2, ADAM_EPS, ADAM_WD, ADAM_STEP = 0.001, 0.9, 0.999, 1e-08, 0.01, 10
MESH = pl.DeviceIdType.MESH
BIG = ("w_ff1_up", "w_ff1_down", "w_in", "w_pa", "w_pb", "w_o", "w_ff2_up", "w_ff2_down")
SHARD_AXIS = {"w_ff1_up": 2, "w_ff1_down": 1, "w_in": 2, "w_pa": 2, "w_pb": 2, "w_o": 1, "w_ff2_up": 2, "w_ff2_down": 1}


def _call(body, *, name, grid, in_specs, out_specs, out_shape, scratch=()):
    return pl.pallas_call(
        body, name=name, grid=grid, in_specs=in_specs, out_specs=out_specs, out_shape=out_shape,
        scratch_shapes=list(scratch),
        compiler_params=pltpu.CompilerParams(dimension_semantics=("arbitrary",) * len(grid), vmem_limit_bytes=VMEM_LIMIT))


def _pick(n, prefs):
    for p in prefs:
        if n % p == 0:
            return p
    return n


def _rows(tm, n, col=0):
    return pl.BlockSpec((tm, n), lambda i: (i, col))


def _fixed(shape):
    return pl.BlockSpec(shape, lambda *_: (0,) * len(shape))


def _sds(shape, dtype):
    return jax.ShapeDtypeStruct(shape, dtype)


def _mm(a, b, mode, out_dtype, name):
    if mode == "tn":
        r, m = a.shape
        n = b.shape[1]
        tm = _pick(m, (1024, 1408, 704, 512, 256, 128))
        tn = _pick(n, (512, 256, 128))
        tr = _pick(r, (1280, 640, 512, 256, 128))

        def body(a_ref, b_ref, o_ref):
            @pl.when(pl.program_id(2) == 0)
            def _():
                o_ref[...] = jnp.zeros_like(o_ref)

            o_ref[...] += lax.dot_general(a_ref[...].astype(MXU), b_ref[...].astype(MXU), (((0,), (0,)), ((), ())),
                                          preferred_element_type=F32)

        return _call(body, name=name, grid=(m // tm, n // tn, r // tr),
                     in_specs=[pl.BlockSpec((tr, tm), lambda i, j, k: (k, i)), pl.BlockSpec((tr, tn), lambda i, j, k: (k, j))],
                     out_specs=pl.BlockSpec((tm, tn), lambda i, j, k: (i, j)), out_shape=_sds((m, n), F32))(a, b)
    m, k = a.shape
    n = b.shape[1] if mode == "nn" else b.shape[0]
    tm = _pick(m, (1280, 640, 512, 256, 128) if k <= 2816 else (640, 512, 256, 128))
    tn = _pick(n, (512, 1408, 256, 128))
    dims = (((1,), (0,)), ((), ())) if mode == "nn" else (((1,), (1,)), ((), ()))

    def body(a_ref, b_ref, o_ref):
        o_ref[...] = lax.dot_general(a_ref[...].astype(MXU), b_ref[...].astype(MXU), dims,
                                     preferred_element_type=F32).astype(o_ref.dtype)

    b_spec = pl.BlockSpec((k, tn), lambda i, j: (0, j)) if mode == "nn" else pl.BlockSpec((tn, k), lambda i, j: (j, 0))
    return _call(body, name=name, grid=(m // tm, n // tn), in_specs=[pl.BlockSpec((tm, k), lambda i, j: (i, 0)), b_spec],
                 out_specs=pl.BlockSpec((tm, tn), lambda i, j: (i, j)), out_shape=_sds((m, n), out_dtype))(a, b)


def _ctx_rows(i, tm, s):
    return (i * tm + lax.broadcasted_iota(jnp.int32, (tm, 1), 0)) >= s


def _mod_row(m_ref, k, ctx):
    return jnp.where(ctx, m_ref[1, k:k + 1, :], m_ref[0, k:k + 1, :])


def _stream_sums(i, tm, s, refs_and_vals):
    @pl.when((i + 1) * tm <= s)
    def _():
        for ref, val in refs_and_vals:
            ref[0] += jnp.sum(val, axis=0, keepdims=True)

    @pl.when((i + 1) * tm > s)
    def _():
        ctx = _ctx_rows(i, tm, s)
        for ref, val in refs_and_vals:
            ref[0] += jnp.sum(jnp.where(ctx, 0.0, val), axis=0, keepdims=True)
            ref[1] += jnp.sum(jnp.where(ctx, val, 0.0), axis=0, keepdims=True)


def _norm_mm(xs, g, mods, k_shift, k_scale, w, s, name, glu):
    t, d = xs.shape
    n = w.shape[1] // 2 if glu else w.shape[1]
    tm = _pick(t, (1280, 640, 256))
    tn = _pick(n, (256, 128)) if glu else _pick(n, (512, 256, 128))
    nj = n // tn

    def body(x_ref, g_ref, m_ref, *refs):
        i, j = pl.program_id(0), pl.program_id(1)
        w_refs, h_ref, o_refs = refs[:2 if glu else 1], refs[2 if glu else 1], refs[3 if glu else 2:]

        @pl.when(j == 0)
        def _():
            x = x_ref[...]
            rstd = lax.rsqrt(jnp.mean(x * x, axis=-1, keepdims=True) + EPS)
            ctx = _ctx_rows(i, tm, s)
            h = x * rstd * g_ref[...] * (1.0 + _mod_row(m_ref, k_scale, ctx)) + _mod_row(m_ref, k_shift, ctx)
            h_ref[...] = h.astype(h_ref.dtype)

        h = h_ref[...]
        a = _nn(h, w_refs[0][...])
        o_refs[0][...] = a.astype(o_refs[0].dtype)
        if glu:
            b = _nn(h, w_refs[1][...])
            o_refs[1][...] = b.astype(o_refs[1].dtype)
            o_refs[2][...] = (a * jax.nn.sigmoid(a) * b).astype(o_refs[2].dtype)

    tile = pl.BlockSpec((tm, tn), lambda i, j: (i, j))
    row = pl.BlockSpec((tm, d), lambda i, j: (i, 0))
    w_specs = [pl.BlockSpec((d, tn), lambda i, j: (0, j))] + ([pl.BlockSpec((d, tn), lambda i, j: (0, j + nj))] if glu else [])
    n_out = 3 if glu else 1
    return _call(body, name=name, grid=(t // tm, nj),
                 in_specs=[row, _fixed((1, d)), _fixed((2, N_MOD, d))] + w_specs,
                 out_specs=[row] + [tile] * n_out,
                 out_shape=[_sds((t, d), MXU)] + [_sds((t, n), MXU)] * n_out)(xs, g, mods, *([w, w] if glu else [w]))


def _mm_res(a, w, xs, mods, k_gate, coef, s, name):
    t, k = a.shape
    d = w.shape[1]
    tm = _pick(t, (1280, 640, 256))
    tn = _pick(d, (512, 256, 128))

    def body(a_ref, w_ref, x_ref, m_ref, y_ref, o_ref):
        y = _nn(a_ref[...], w_ref[...])
        y_ref[...] = y.astype(y_ref.dtype)
        gate = _mod_row(m_ref, k_gate, _ctx_rows(pl.program_id(0), tm, s))
        o_ref[...] = x_ref[...] + (coef * gate) * y

    tile = pl.BlockSpec((tm, tn), lambda i, j: (i, j))
    return _call(body, name=name, grid=(t // tm, d // tn),
                 in_specs=[pl.BlockSpec((tm, k), lambda i, j: (i, 0)), pl.BlockSpec((k, tn), lambda i, j: (0, j)), tile,
                           pl.BlockSpec((2, N_MOD, tn), lambda i, j: (0, 0, j))],
                 out_specs=[tile, tile], out_shape=[_sds((t, d), MXU), _sds((t, d), F32)])(a, w, xs, mods)


def _resb_mm(dxn, y, mods, k_gate, coef, w, s, name, ups=None):
    t, d = dxn.shape
    n = w.shape[0]
    tm = _pick(t, (1280, 640, 256))
    tn = _pick(n, (256, 128)) if ups else _pick(n, (512, 256, 128))

    def body(dx_ref, y_ref, m_ref, w_ref, *refs):
        i, j = pl.program_id(0), pl.program_id(1)
        u_refs, (dy_ref, dgt_ref), o_refs = (refs[:2], refs[2:4], refs[4:]) if ups else ((), refs[:2], refs[2:])

        @pl.when((i == 0) & (j == 0))
        def _():
            dgt_ref[...] = jnp.zeros_like(dgt_ref)

        @pl.when(j == 0)
        def _():
            dx = dx_ref[...]
            gate = _mod_row(m_ref, k_gate, _ctx_rows(i, tm, s))
            dy_ref[...] = ((coef * gate) * dx).astype(dy_ref.dtype)
            _stream_sums(i, tm, s, [(dgt_ref, coef * y_ref[...].astype(F32) * dx)])

        dact = _nt(dy_ref[...], w_ref[...])
        if ups:
            a, b = u_refs[0][...].astype(F32), u_refs[1][...].astype(F32)
            sg = jax.nn.sigmoid(a)
            o_refs[0][...] = (dact * b * sg * (1.0 + a * (1.0 - sg))).astype(o_refs[0].dtype)
            o_refs[1][...] = (dact * a * sg).astype(o_refs[1].dtype)
        else:
            o_refs[0][...] = dact.astype(o_refs[0].dtype)

    tile = pl.BlockSpec((tm, tn), lambda i, j: (i, j))
    row = pl.BlockSpec((tm, d), lambda i, j: (i, 0))
    n_out = 2 if ups else 1
    return _call(body, name=name, grid=(t // tm, n // tn),
                 in_specs=[row, row, _fixed((2, N_MOD, d)), pl.BlockSpec((tn, d), lambda i, j: (j, 0))] + ([tile, tile] if ups else []),
                 out_specs=[row, _fixed((2, 1, d))] + [tile] * n_out,
                 out_shape=[_sds((t, d), MXU), _sds((2, 1, d), F32)] + [_sds((t, n), MXU)] * n_out)(
                     dxn, y, mods, w, *(ups or ()))


def _mm_normb(a_list, w, xs, dres, g, mods, k_scale, s, name):
    t, d = xs.shape
    ka = a_list[0].shape[1]
    tm = _pick(t, (640, 256))
    tk = _pick(ka, (1408, 1536, 1024, 512, 256, 128))
    nk1 = ka // tk
    n_a = len(a_list)
    nk = nk1 * n_a

    def body(*refs):
        a_refs, (w_ref, x_ref, dr_ref, g_ref, m_ref, dx_ref, dsh_ref, dsc_ref, dg_ref, acc) = refs[:n_a], refs[n_a:]
        i, k = pl.program_id(0), pl.program_id(1)

        @pl.when((i == 0) & (k == 0))
        def _():
            dsh_ref[...] = jnp.zeros_like(dsh_ref)
            dsc_ref[...] = jnp.zeros_like(dsc_ref)
            dg_ref[...] = jnp.zeros_like(dg_ref)

        @pl.when(k == 0)
        def _():
            acc[...] = jnp.zeros_like(acc)

        for q in range(n_a):
            @pl.when((k >= q * nk1) & (k < (q + 1) * nk1))
            def _():
                acc[...] += _nt(a_refs[q][...], w_ref[...])

        @pl.when(k == nk - 1)
        def _():
            x = x_ref[...]
            dh = acc[...]
            rstd = lax.rsqrt(jnp.mean(x * x, axis=-1, keepdims=True) + EPS)
            xhat = x * rstd
            gg = g_ref[...]
            _stream_sums(i, tm, s, [(dsh_ref, dh), (dsc_ref, dh * (xhat * gg))])
            dy = dh * (1.0 + _mod_row(m_ref, k_scale, _ctx_rows(i, tm, s)))
            dg_ref[...] += jnp.sum(dy * xhat, axis=0, keepdims=True)
            dxh = dy * gg
            dx_ref[...] = dr_ref[...] + rstd * (dxh - xhat * jnp.mean(dxh * xhat, axis=-1, keepdims=True))

    row = pl.BlockSpec((tm, d), lambda i, k: (i, 0))
    a_specs = [pl.BlockSpec((tm, tk), lambda i, k, q=q: (i, jnp.clip(k - q * nk1, 0, nk1 - 1))) for q in range(n_a)]
    return _call(body, name=name, grid=(t // tm, nk),
                 in_specs=a_specs + [pl.BlockSpec((d, tk), lambda i, k: (0, k)), row, row, _fixed((1, d)), _fixed((2, N_MOD, d))],
                 out_specs=[row, _fixed((2, 1, d)), _fixed((2, 1, d)), _fixed((1, d))],
                 out_shape=[_sds((t, d), F32), _sds((2, 1, d), F32), _sds((2, 1, d), F32), _sds((1, d), F32)],
                 scratch=[pltpu.VMEM((tm, d), F32)])(*a_list, w, xs, dres, g, mods)


def _rope_tables(s, ctx_len):
    n_freq = HEAD_DIM // 4
    tok = jnp.arange(s)
    freqs = ROPE_THETA ** (-jnp.arange(n_freq, dtype=F32) / n_freq)
    ang = jnp.concatenate([(tok // GRID_W).astype(F32)[:, None] * freqs, (tok % GRID_W).astype(F32)[:, None] * freqs], axis=-1)
    cos = jnp.repeat(jnp.cos(ang), 2, axis=-1)
    sin = jnp.repeat(jnp.sin(ang), 2, axis=-1) * jnp.tile(jnp.array([-1.0, 1.0], F32), HEAD_DIM // 2)
    cos = jnp.concatenate([jnp.tile(cos, (1, HEADS)), jnp.ones((ctx_len, NA_WIDTH), F32)], axis=0)
    sin = jnp.concatenate([jnp.tile(sin, (1, HEADS)), jnp.zeros((ctx_len, NA_WIDTH), F32)], axis=0)
    return cos, sin


def _swap_pairs(x):
    n = x.shape[-1]
    lane = lax.broadcasted_iota(jnp.int32, x.shape, 1)
    return jnp.where(lane % 2 == 0, pltpu.roll(x, n - 1, 1), pltpu.roll(x, 1, 1))


def _rope(p, cos, sin, name):
    t = p.shape[0]
    w = NA_WIDTH

    def body(q_ref, k_ref, v_ref, c_ref, s_ref, qo_ref, ko_ref, vo_ref):
        c, s = c_ref[...], s_ref[...]
        q, k = q_ref[...].astype(F32), k_ref[...].astype(F32)
        qo_ref[...] = (q * c + _swap_pairs(q) * s).astype(qo_ref.dtype)
        ko_ref[...] = (k * c + _swap_pairs(k) * s).astype(ko_ref.dtype)
        vo_ref[...] = v_ref[...].astype(vo_ref.dtype)

    return _call(body, name=name, grid=(t // TM,),
                 in_specs=[_rows(TM, w, 0), _rows(TM, w, 1), _rows(TM, w, 2), _rows(TM, w), _rows(TM, w)],
                 out_specs=[_rows(TM, w)] * 3, out_shape=[_sds((t, w), MXU)] * 3)(p, p, p, cos, sin)


def _rope_bwd(dq, dk, dv, cos, sin, name):
    t = dq.shape[0]

    def body(dq_ref, dk_ref, dv_ref, c_ref, s_ref, qo_ref, ko_ref, vo_ref):
        c, s = c_ref[...], s_ref[...]
        a, b = dq_ref[...], dk_ref[0]
        qo_ref[...] = (a * c + _swap_pairs(a * s)).astype(qo_ref.dtype)
        ko_ref[...] = (b * c + _swap_pairs(b * s)).astype(ko_ref.dtype)
        vo_ref[...] = dv_ref[0].astype(vo_ref.dtype)

    tile = pl.BlockSpec((TM, LANES), lambda i, j: (i, j))
    pair = pl.BlockSpec((1, TM, LANES), lambda i, j: (j, i, 0))
    return _call(body, name=name, grid=(t // TM, NA_WIDTH // LANES), in_specs=[tile, pair, pair, tile, tile],
                 out_specs=[tile] * 3, out_shape=[_sds((t, NA_WIDTH), MXU)] * 3)(dq, dk, dv, cos, sin)


def _na_geometry(r_grid):
    rows = []
    for r0, ks in ((0, 0), (Q_ROWS, 0), (r_grid - Q_ROWS, r_grid - K_ROWS)):
        dr = np.zeros((Q_ROWS, K_ROWS), np.int32)
        vr = np.zeros((Q_ROWS, K_ROWS), bool)
        for a in range(Q_ROWS):
            r = r0 + a
            rs = min(max(r - WIN_H // 2, 0), r_grid - WIN_H)
            for i in range(K_ROWS):
                kr = ks + i
                vr[a, i] = rs <= kr <= rs + WIN_H - 1
                dr[a, i] = kr - r + WIN_H - 1
        rows.append((dr, vr))
    c = np.arange(GRID_W)
    cs = np.clip(c - WIN_W // 2, 0, GRID_W - WIN_W)
    kc = np.arange(GRID_W)
    vc = (kc[None, :] >= cs[:, None]) & (kc[None, :] <= cs[:, None] + WIN_W - 1)
    dc = kc[None, :] - c[:, None] + WIN_W - 1
    return rows, dc, vc


def _bias_table(rpb, r_grid):
    rows, _, vc = _na_geometry(r_grid)
    n_dr, n_dc, skew, off = 2 * WIN_H - 1, 2 * WIN_W - 1, 2 * GRID_W - 1, GRID_W - WIN_W
    u = jnp.pad(rpb, ((0, 0), (0, 0), (off, skew - off - n_dc)))
    toep = jnp.tile(u, (1, 1, GRID_W + 1))[:, :, :2 * GRID_W * GRID_W].reshape(HEADS, n_dr, GRID_W, 2 * GRID_W)
    toep = jnp.pad(toep[:, :, ::-1, :GRID_W], ((0, 0), (Q_ROWS, Q_ROWS), (0, 0), (0, 0)))
    tabs = []
    for dr, vr in rows:
        per_row = []
        for a in range(Q_ROWS):
            lo = int(dr[a, 0]) + Q_ROWS
            valid = vr[a][:, None, None] & vc[None, :, :]
            per_row.append(jnp.where(valid[None], toep[:, lo:lo + K_ROWS], NEG))
        t = jnp.stack(per_row, axis=1)
        tabs.append(jnp.transpose(t, (0, 1, 3, 2, 4)).reshape(HEADS, TQ, TK))
    tabs.append(jnp.full((HEADS, TQ, TK), NEG, F32))
    return jnp.stack(tabs)


def _variant(g, ngx):
    return jnp.where(g == 0, 0, jnp.where(g >= ngx, 3, jnp.where(g == ngx - 1, 2, 1)))


def _key_start(g, r_grid):
    return pl.multiple_of(jnp.clip(g * Q_ROWS - WIN_H // 2, 0, r_grid - K_ROWS) * GRID_W, TQ)


def _nt(a, b):
    return lax.dot_general(a, b, (((1,), (1,)), ((), ())), preferred_element_type=F32)


def _tn(a, b):
    return lax.dot_general(a, b, (((0,), (0,)), ((), ())), preferred_element_type=F32)


def _nn(a, b):
    return jnp.dot(a, b, preferred_element_type=F32)


def _head_mask(h):
    lane = lax.broadcasted_iota(jnp.int32, (1, LANES), 1)
    return ((lane >= HEAD_DIM * h) & (lane < HEAD_DIM * (h + 1))).astype(F32)


def _softmax_parts(qm, knb, kcx, bias):
    s_nb = _nt(qm, knb) + bias
    s_cx = _nt(qm, kcx)
    m = jnp.maximum(jnp.max(s_nb, axis=-1, keepdims=True), jnp.max(s_cx, axis=-1, keepdims=True))
    e_nb = jnp.exp(s_nb - m)
    e_cx = jnp.exp(s_cx - m)
    inv = 1.0 / (jnp.sum(e_nb, axis=-1, keepdims=True) + jnp.sum(e_cx, axis=-1, keepdims=True))
    return e_nb * inv, e_cx * inv


def _na_specs(t, ngx):
    q_spec = pl.BlockSpec((TQ, LANES), lambda hp, g: (g, hp))
    kv_spec = pl.BlockSpec((t, LANES), lambda hp, g: (0, hp))
    b_spec = pl.BlockSpec((1, 2, TQ, TK), lambda hp, g: (_variant(g, ngx), hp, 0, 0))
    return q_spec, kv_spec, b_spec


def _na_fwd(q, k, v, bias, s, name):
    t = q.shape[0]
    ctx_len = t - s
    r_grid = s // GRID_W
    q_spec, kv_spec, b_spec = _na_specs(t, s // TQ)

    def body(q_ref, k_ref, v_ref, b_ref, o_ref):
        start = _key_start(pl.program_id(1), r_grid)
        qf = q_ref[...].astype(F32) * (HEAD_DIM ** -0.5)
        knb, vnb = k_ref[pl.ds(start, TK), :], v_ref[pl.ds(start, TK), :]
        kcx, vcx = k_ref[pl.ds(s, ctx_len), :], v_ref[pl.ds(s, ctx_len), :]
        acc = jnp.zeros((TQ, LANES), F32)
        for h in range(2):
            mask = _head_mask(h)
            p_nb, p_cx = _softmax_parts((qf * mask).astype(MXU), knb, kcx, b_ref[0, h])
            acc += (_nn(p_nb.astype(MXU), vnb) + _nn(p_cx.astype(MXU), vcx)) * mask
        o_ref[...] = acc.astype(o_ref.dtype)

    return _call(body, name=name, grid=(NA_WIDTH // LANES, t // TQ), in_specs=[q_spec, kv_spec, kv_spec, b_spec],
                 out_specs=q_spec, out_shape=_sds((t, NA_WIDTH), MXU))(q, k, v, bias)


def _na_bwd(q, k, v, do, bias, s, name):
    t = q.shape[0]
    ctx_len = t - s
    r_grid = s // GRID_W
    ng, ngx = t // TQ, s // TQ
    q_spec, kv_spec, b_spec = _na_specs(t, ngx)

    def body(q_ref, k_ref, v_ref, do_ref, b_ref, dq_ref, dk_hbm, dv_hbm, db_ref, dk_acc, dv_acc):
        hp, g = pl.program_id(0), pl.program_id(1)
        start = _key_start(g, r_grid)

        @pl.when(g == 0)
        def _():
            dk_acc[...] = jnp.zeros_like(dk_acc)
            dv_acc[...] = jnp.zeros_like(dv_acc)

        @pl.when((g == 0) | (g == 1) | (g == ngx - 1) | (g == ngx))
        def _():
            db_ref[...] = jnp.zeros_like(db_ref)

        qf = q_ref[...].astype(F32) * (HEAD_DIM ** -0.5)
        do = do_ref[...].astype(F32)
        knb, vnb = k_ref[pl.ds(start, TK), :], v_ref[pl.ds(start, TK), :]
        kcx, vcx = k_ref[pl.ds(s, ctx_len), :], v_ref[pl.ds(s, ctx_len), :]
        dq = jnp.zeros((TQ, LANES), F32)
        dk_nb = jnp.zeros((TK, LANES), F32)
        dv_nb = jnp.zeros((TK, LANES), F32)
        dk_cx = jnp.zeros((ctx_len, LANES), F32)
        dv_cx = jnp.zeros((ctx_len, LANES), F32)
        for h in range(2):
            mask = _head_mask(h)
            qm = (qf * mask).astype(MXU)
            dom = (do * mask).astype(MXU)
            p_nb, p_cx = _softmax_parts(qm, knb, kcx, b_ref[0, h])
            dp_nb = _nt(dom, vnb)
            dp_cx = _nt(dom, vcx)
            delta = jnp.sum(p_nb * dp_nb, axis=-1, keepdims=True) + jnp.sum(p_cx * dp_cx, axis=-1, keepdims=True)
            ds_nb = p_nb * (dp_nb - delta)
            ds_cx = p_cx * (dp_cx - delta)
            db_ref[0, h] += ds_nb
            ds_nb, ds_cx = ds_nb.astype(MXU), ds_cx.astype(MXU)
            dq += (_nn(ds_nb, knb) + _nn(ds_cx, kcx)) * (mask * (HEAD_DIM ** -0.5))
            dk_nb += _tn(ds_nb, qm)
            dk_cx += _tn(ds_cx, qm)
            dv_nb += _tn(p_nb.astype(MXU), dom)
            dv_cx += _tn(p_cx.astype(MXU), dom)
        dq_ref[...] = dq
        dk_acc[pl.ds(start, TK), :] += dk_nb
        dv_acc[pl.ds(start, TK), :] += dv_nb
        dk_acc[pl.ds(s, ctx_len), :] += dk_cx
        dv_acc[pl.ds(s, ctx_len), :] += dv_cx

        @pl.when(g == ng - 1)
        def _():
            pltpu.sync_copy(dk_acc, dk_hbm.at[hp])
            pltpu.sync_copy(dv_acc, dv_hbm.at[hp])

    n_pairs = NA_WIDTH // LANES
    hbm = pl.BlockSpec(memory_space=pl.ANY)
    return _call(body, name=name, grid=(n_pairs, ng), in_specs=[q_spec, kv_spec, kv_spec, q_spec, b_spec],
                 out_specs=[q_spec, hbm, hbm, b_spec],
                 out_shape=[_sds((t, NA_WIDTH), F32), _sds((n_pairs, t, LANES), F32), _sds((n_pairs, t, LANES), F32),
                            _sds((4, HEADS, TQ, TK), F32)],
                 scratch=[pltpu.VMEM((t, LANES), F32), pltpu.VMEM((t, LANES), F32)])(q, k, v, do, bias)


def _rpb_grad(dbias, r_grid, name):
    rows, _, _ = _na_geometry(r_grid)
    n_blk = 3 * Q_ROWS * K_ROWS
    z = dbias[:3].reshape(3, HEADS, Q_ROWS, GRID_W, K_ROWS, GRID_W)
    z = jnp.transpose(z, (1, 0, 2, 4, 3, 5)).reshape(HEADS, n_blk, GRID_W, GRID_W)
    z = jnp.pad(z[:, :, ::-1, :], ((0, 0), (0, 0), (0, 0), (0, GRID_W))).reshape(HEADS, n_blk, 2 * GRID_W * GRID_W)
    skew = 2 * GRID_W - 1
    z = jnp.pad(z, ((0, 0), (0, 0), (0, (GRID_W + 1) * skew - 2 * GRID_W * GRID_W))).reshape(HEADS, n_blk, GRID_W + 1, skew)
    z = jnp.pad(z, ((0, 0), (0, 0), (0, 72 - (GRID_W + 1)), (0, 1)))
    members = [[] for _ in range(2 * WIN_H - 1)]
    for vi, (dr, vr) in enumerate(rows):
        for a in range(Q_ROWS):
            for i in range(K_ROWS):
                if vr[a, i]:
                    members[dr[a, i]].append((vi * Q_ROWS + a) * K_ROWS + i)

    def body(z_ref, o_ref):
        zs = jnp.sum(z_ref[0], axis=1)
        out = []
        for mem in members:
            acc = jnp.zeros((1, LANES), F32)
            for j in mem:
                acc = acc + zs[j:j + 1, :]
            out.append(acc)
        out.append(jnp.zeros((1, LANES), F32))
        o_ref[0] = jnp.concatenate(out, axis=0)

    o = _call(body, name=name, grid=(HEADS,), in_specs=[pl.BlockSpec((1, n_blk, 72, LANES), lambda h: (h, 0, 0, 0))],
              out_specs=pl.BlockSpec((1, 16, LANES), lambda h: (h, 0, 0)), out_shape=_sds((HEADS, 16, LANES), F32))(z)
    off = GRID_W - 1 - (WIN_W - 1)
    return o[:, :2 * WIN_H - 1, off:off + 2 * WIN_W - 1]


_GELU_K, _GELU_C = 0.7978845608028654, 0.044715


def _gelu(x):
    return 0.5 * x * (1.0 + jnp.tanh(_GELU_K * (x + _GELU_C * x * x * x)))


def _gelu_grad(x):
    th = jnp.tanh(_GELU_K * (x + _GELU_C * x * x * x))
    return 0.5 * (1.0 + th) + 0.5 * x * (1.0 - th * th) * (_GELU_K * (1.0 + 3.0 * _GELU_C * x * x))


def _ln_stats(v):
    mu = jnp.mean(v, axis=-1, keepdims=True)
    vc = v - mu
    rstd = lax.rsqrt(jnp.mean(vc * vc, axis=-1, keepdims=True) + EPS)
    return vc * rstd, rstd


def _gmlp(p, ln_g, ln_b, w_s, b_s, name):
    t = p.shape[0]
    w = SG_WIDTH
    cw = w // SG_GROUPS

    def body(u_ref, v_ref, g_ref, b_ref, ws_ref, bs_ref, o_ref):
        xhat, _ = _ln_stats(_gelu(v_ref[...].astype(F32)))
        vn = (xhat * g_ref[...] + b_ref[...]).astype(MXU)
        ug = _gelu(u_ref[...].astype(F32))
        for ci in range(TM // SG_CHUNK):
            rs = slice(ci * SG_CHUNK, (ci + 1) * SG_CHUNK)
            for gi in range(SG_GROUPS):
                cs = slice(gi * cw, (gi + 1) * cw)
                sg = _nn(ws_ref[gi].astype(MXU), vn[rs, cs]) + bs_ref[gi]
                o_ref[rs, cs] = (ug[rs, cs] * sg).astype(o_ref.dtype)

    return _call(body, name=name, grid=(t // TM,),
                 in_specs=[_rows(TM, w, 3), _rows(TM, w, 4), _fixed((1, w)), _fixed((1, w)),
                           _fixed((SG_GROUPS, SG_CHUNK, SG_CHUNK)), _fixed((SG_GROUPS, SG_CHUNK, 1))],
                 out_specs=_rows(TM, w), out_shape=_sds((t, w), MXU))(p, p, ln_g, ln_b, w_s, b_s)


def _gmlp_bwd(p, dob, ln_g, ln_b, w_s, b_s, name):
    t = p.shape[0]
    w = SG_WIDTH
    cw = w // SG_GROUPS

    def body(u_ref, v_ref, do_ref, g_ref, b_ref, ws_ref, bs_ref, du_ref, dv_ref, dws_ref, dbs_ref, dg_ref, db_ref, dvn_ref):
        @pl.when(pl.program_id(0) == 0)
        def _():
            dws_ref[...] = jnp.zeros_like(dws_ref)
            dbs_ref[...] = jnp.zeros_like(dbs_ref)
            dg_ref[...] = jnp.zeros_like(dg_ref)
            db_ref[...] = jnp.zeros_like(db_ref)

        u, v = u_ref[...].astype(F32), v_ref[...].astype(F32)
        xhat, rstd = _ln_stats(_gelu(v))
        vn = (xhat * g_ref[...] + b_ref[...]).astype(MXU)
        ug = _gelu(u)
        dob = do_ref[...].astype(F32)
        for ci in range(TM // SG_CHUNK):
            rs = slice(ci * SG_CHUNK, (ci + 1) * SG_CHUNK)
            for gi in range(SG_GROUPS):
                cs = slice(gi * cw, (gi + 1) * cw)
                wsg = ws_ref[gi].astype(MXU)
                sg = _nn(wsg, vn[rs, cs]) + bs_ref[gi]
                du_ref[rs, cs] = (dob[rs, cs] * sg * _gelu_grad(u[rs, cs])).astype(du_ref.dtype)
                ds = dob[rs, cs] * ug[rs, cs]
                dbs_ref[gi] += jnp.sum(ds, axis=-1, keepdims=True)
                ds = ds.astype(MXU)
                dws_ref[gi] += _nt(ds, vn[rs, cs])
                dvn_ref[rs, cs] = _tn(wsg, ds)
        dvn = dvn_ref[...]
        dg_ref[...] += jnp.sum(dvn * xhat, axis=0, keepdims=True)
        db_ref[...] += jnp.sum(dvn, axis=0, keepdims=True)
        dxh = dvn * g_ref[...]
        dvg = rstd * (dxh - jnp.mean(dxh, axis=-1, keepdims=True) - xhat * jnp.mean(dxh * xhat, axis=-1, keepdims=True))
        dv_ref[...] = (dvg * _gelu_grad(v)).astype(dv_ref.dtype)

    return _call(body, name=name, grid=(t // TM,),
                 in_specs=[_rows(TM, w, 3), _rows(TM, w, 4), _rows(TM, w), _fixed((1, w)), _fixed((1, w)),
                           _fixed((SG_GROUPS, SG_CHUNK, SG_CHUNK)), _fixed((SG_GROUPS, SG_CHUNK, 1))],
                 out_specs=[_rows(TM, w), _rows(TM, w), _fixed((SG_GROUPS, SG_CHUNK, SG_CHUNK)),
                            _fixed((SG_GROUPS, SG_CHUNK, 1)), _fixed((1, w)), _fixed((1, w))],
                 out_shape=[_sds((t, w), MXU), _sds((t, w), MXU), _sds((SG_GROUPS, SG_CHUNK, SG_CHUNK), F32),
                            _sds((SG_GROUPS, SG_CHUNK, 1), F32), _sds((1, w), F32), _sds((1, w), F32)],
                 scratch=[pltpu.VMEM((TM, w), F32)])(p, p, dob, ln_g, ln_b, w_s, b_s)


def _merge(pa, pb, p, b_gate, name):
    t, d = pa.shape
    hw = NA_WIDTH
    nh = d // hw
    c0 = (NA_WIDTH * 3 + SG_WIDTH * 2) // hw

    def body(pa_ref, pb_ref, la_ref, lb_ref, ba_ref, bb_ref, o_ref):
        ga = jax.nn.sigmoid(la_ref[...].astype(F32) + ba_ref[...])
        gb = jax.nn.sigmoid(lb_ref[...].astype(F32) + bb_ref[...])
        o_ref[...] = (ga * pa_ref[...].astype(F32) + gb * pb_ref[...].astype(F32)).astype(o_ref.dtype)

    tile = pl.BlockSpec((TM, hw), lambda i, j: (i, j))
    return _call(body, name=name, grid=(t // TM, nh),
                 in_specs=[tile, tile, pl.BlockSpec((TM, hw), lambda i, j: (i, c0 + j)),
                           pl.BlockSpec((TM, hw), lambda i, j: (i, c0 + nh + j)),
                           pl.BlockSpec((1, hw), lambda i, j: (0, j)), pl.BlockSpec((1, hw), lambda i, j: (0, nh + j))],
                 out_specs=tile, out_shape=_sds((t, d), MXU))(pa, pb, p, p, b_gate, b_gate)


def _merge_bwd(dmg, pa, pb, p, b_gate, name):
    t, d = pa.shape
    hw = NA_WIDTH
    nh = d // hw
    c0 = (NA_WIDTH * 3 + SG_WIDTH * 2) // hw

    def body(dm_ref, pa_ref, pb_ref, la_ref, lb_ref, ba_ref, bb_ref, dpa_ref, dpb_ref, dla_ref, dlb_ref, dba_ref, dbb_ref):
        @pl.when(pl.program_id(1) == 0)
        def _():
            dba_ref[...] = jnp.zeros_like(dba_ref)
            dbb_ref[...] = jnp.zeros_like(dbb_ref)

        dm = dm_ref[...].astype(F32)
        ga = jax.nn.sigmoid(la_ref[...].astype(F32) + ba_ref[...])
        gb = jax.nn.sigmoid(lb_ref[...].astype(F32) + bb_ref[...])
        dpa_ref[...] = (dm * ga).astype(dpa_ref.dtype)
        dpb_ref[...] = (dm * gb).astype(dpb_ref.dtype)
        dla = dm * pa_ref[...].astype(F32) * ga * (1.0 - ga)
        dlb = dm * pb_ref[...].astype(F32) * gb * (1.0 - gb)
        dla_ref[...] = dla.astype(dla_ref.dtype)
        dlb_ref[...] = dlb.astype(dlb_ref.dtype)
        dba_ref[...] += jnp.sum(dla, axis=0, keepdims=True)
        dbb_ref[...] += jnp.sum(dlb, axis=0, keepdims=True)

    tile = pl.BlockSpec((TM, hw), lambda j, i: (i, j))
    bias_a = pl.BlockSpec((1, hw), lambda j, i: (0, j))
    bias_b = pl.BlockSpec((1, hw), lambda j, i: (0, nh + j))
    return _call(body, name=name, grid=(nh, t // TM),
                 in_specs=[tile, tile, tile, pl.BlockSpec((TM, hw), lambda j, i: (i, c0 + j)),
                           pl.BlockSpec((TM, hw), lambda j, i: (i, c0 + nh + j)), bias_a, bias_b],
                 out_specs=[tile, tile, tile, tile, bias_a, bias_a],
                 out_shape=[_sds((t, d), MXU)] * 4 + [_sds((1, d), F32)] * 2)(dmg, pa, pb, p, p, b_gate, b_gate)


def _final(xs, tgt, g, name):
    t, d = xs.shape
    nx = tgt.shape[0] // TM

    def body(x_ref, t_ref, g_ref, l_ref, dx_ref, dg_ref):
        i = pl.program_id(0)

        @pl.when(i == 0)
        def _():
            l_ref[...] = jnp.zeros_like(l_ref)
            dg_ref[...] = jnp.zeros_like(dg_ref)

        @pl.when(i < nx)
        def _():
            x = x_ref[...]
            rstd = lax.rsqrt(jnp.mean(x * x, axis=-1, keepdims=True) + EPS)
            xhat = x * rstd
            err = xhat * g_ref[...] - t_ref[...]
            l_ref[...] += 0.5 * jnp.sum(jnp.mean(err * err, axis=-1, keepdims=True))
            dy = err * (1.0 / d)
            dg_ref[...] += jnp.sum(dy * xhat, axis=0, keepdims=True)
            dxh = dy * g_ref[...]
            dx_ref[...] = rstd * (dxh - xhat * jnp.mean(dxh * xhat, axis=-1, keepdims=True))

        @pl.when(i >= nx)
        def _():
            dx_ref[...] = jnp.zeros_like(dx_ref)

    return _call(body, name=name, grid=(t // TM,),
                 in_specs=[_rows(TM, d), pl.BlockSpec((TM, d), lambda i: (jnp.minimum(i, nx - 1), 0)), _fixed((1, d))],
                 out_specs=[_fixed((1, LANES)), _rows(TM, d), _fixed((1, d))],
                 out_shape=[_sds((1, LANES), F32), _sds((t, d), F32), _sds((1, d), F32)])(xs, tgt, g)


def _view2d(a):
    return a.reshape(1, -1) if a.ndim == 1 else a.reshape(-1, a.shape[-1])


def _ew(fn, arrays, out_dtypes, name):
    shape = arrays[0].shape
    views = [_view2d(a) for a in arrays]
    r, c = views[0].shape
    tr = r
    for cand in (1024, 512, 256, 128, 64, 32, 16):
        if r % cand == 0 and cand * c * 4 <= 2 ** 20:
            tr = cand
            break

    def body(*refs):
        outs = fn(*[ref[...] for ref in refs[:len(views)]])
        for ref, o in zip(refs[len(views):], outs):
            ref[...] = o.astype(ref.dtype)

    res = _call(body, name=name, grid=(r // tr,), in_specs=[_rows(tr, c)] * len(views), out_specs=[_rows(tr, c)] * len(out_dtypes),
                out_shape=[_sds((r, c), dt) for dt in out_dtypes])(*views)
    return [o.reshape(shape) for o in res]


def _sum_pieces(pieces, name, out_dtypes=(F32,)):
    def fn(*vals):
        acc = vals[0].astype(F32)
        for v in vals[1:]:
            acc = acc + v.astype(F32)
        return (acc,) * len(out_dtypes)

    return _ew(fn, pieces, list(out_dtypes), name)


def _adamw(w, g_pieces, m, v, name):
    n_g = len(g_pieces)

    def fn(w_, *rest):
        g = rest[0]
        for piece in rest[1:n_g]:
            g = g + piece
        m_, v_ = rest[n_g], rest[n_g + 1]
        m2 = ADAM_B1 * m_ + (1.0 - ADAM_B1) * g
        v2 = ADAM_B2 * v_ + (1.0 - ADAM_B2) * (g * g)
        m_hat = m2 / (1.0 - ADAM_B1 ** ADAM_STEP)
        v_hat = v2 / (1.0 - ADAM_B2 ** ADAM_STEP)
        delta = -ADAM_LR * (m_hat / (jnp.sqrt(v_hat) + ADAM_EPS) + ADAM_WD * w_)
        return g, delta, m2, v2

    return _ew(fn, [w, *g_pieces, m, v], [F32] * 4, name)


def _ada_fwd(cond, w, b, name):
    r, d = cond.shape
    n = w.shape[1]
    tn = _pick(n, (1152, 768, 512, 384, 256, 128))

    def body(c_ref, w_ref, b_ref, o_ref, s_ref):
        c = c_ref[...]
        sc = c * jax.nn.sigmoid(c)
        s_ref[...] = sc
        o_ref[...] = _nn(sc.astype(MXU), w_ref[...].astype(MXU)) + b_ref[...]

    return _call(body, name=name, grid=(n // tn,),
                 in_specs=[_fixed((r, d)), pl.BlockSpec((d, tn), lambda j: (0, j)), pl.BlockSpec((1, tn), lambda j: (0, j))],
                 out_specs=[pl.BlockSpec((r, tn), lambda j: (0, j)), _fixed((r, d))],
                 out_shape=[_sds((r, n), F32), _sds((r, d), F32)])(cond, w, b)


def _cctx_grad(parts, c_ctx, name):
    n, d = parts.shape

    def body(p_ref, c_ref, o_ref):
        c = c_ref[...]
        sg = jax.nn.sigmoid(c)
        acc = p_ref[0:1, :]
        for j in range(1, n):
            acc = acc + p_ref[j:j + 1, :]
        o_ref[...] = acc * (sg * (1.0 + c * (1.0 - sg)))

    return _call(body, name=name, grid=(1,), in_specs=[_fixed((n, d)), _fixed((1, d))], out_specs=_fixed((1, d)),
                 out_shape=_sds((1, d), F32))(parts, c_ctx)


def _here():
    return lax.axis_index("x"), lax.axis_index("y"), lax.axis_index("c")


def _flip(v, bit):
    return 1 - v if bit else v


def _allgather8(xb, name):
    r, n = xb.shape

    def body(x_ref, out_ref, send_sems, recv_sems, local_sem):
        x, y, c = _here()
        me = 4 * x + 2 * y + c
        local = pltpu.make_async_copy(x_ref, out_ref.at[me], local_sem)
        local.start()
        sends = []
        for k in range(1, 8):
            peer = (_flip(x, k & 4), _flip(y, k & 2), _flip(c, k & 1))
            cp = pltpu.make_async_remote_copy(src_ref=x_ref, dst_ref=out_ref.at[me], send_sem=send_sems.at[k - 1],
                                              recv_sem=recv_sems.at[k - 1], device_id=peer, device_id_type=MESH)
            cp.start()
            sends.append(cp)
        for k in range(1, 8):
            peer = (_flip(x, k & 4), _flip(y, k & 2), _flip(c, k & 1))
            src = 4 * peer[0] + 2 * peer[1] + peer[2]
            pltpu.make_async_remote_copy(src_ref=x_ref, dst_ref=out_ref.at[src], send_sem=send_sems.at[k - 1],
                                         recv_sem=recv_sems.at[k - 1], device_id=peer, device_id_type=MESH).wait_recv()
        for cp in sends:
            cp.wait_send()
        local.wait()

    vmem = pl.BlockSpec(memory_space=pltpu.VMEM)
    return pl.pallas_call(
        body, name=name, out_shape=_sds((8, r, n), xb.dtype), in_specs=[vmem], out_specs=vmem,
        scratch_shapes=[pltpu.SemaphoreType.DMA((7,)), pltpu.SemaphoreType.DMA((7,)), pltpu.SemaphoreType.DMA(())],
        compiler_params=pltpu.CompilerParams(vmem_limit_bytes=VMEM_LIMIT))(xb)


def _shard_of(ref, axis, j, size):
    sl = pl.ds(j * size, size)
    return ref.at[:, sl, :] if axis == 1 else ref.at[:, :, sl]


def _piece(ref, axis, j, size, layer):
    lay, sl = pl.ds(layer, 1), pl.ds(j * size, size)
    return ref.at[lay, sl, :] if axis == 1 else ref.at[lay, :, sl]


def _gather_chips(shards, axes, name):
    n = len(shards)
    fulls = []
    for a, ax in zip(shards, axes):
        assert a.shape[0] == 2
        shp = list(a.shape)
        shp[ax] *= 4
        fulls.append(_sds(tuple(shp), a.dtype))

    def body(*refs):
        ins, outs = refs[:n], refs[n:2 * n]
        ici_send, ici_recv, d2d_send, d2d_recv, local_sems = refs[2 * n:]
        x, y, c = _here()
        chips = [(_flip(x, k & 2), _flip(y, k & 1)) for k in range(1, 4)]
        local, sends = [], []
        for a in range(n):
            size = ins[a].shape[axes[a]]
            cp = pltpu.make_async_copy(ins[a], _shard_of(outs[a], axes[a], 2 * x + y, size), local_sems.at[a])
            cp.start()
            local.append(cp)
            for j, (px, py) in enumerate(chips):
                cp = pltpu.make_async_remote_copy(src_ref=ins[a].at[pl.ds(c, 1)], dst_ref=_piece(outs[a], axes[a], 2 * x + y, size, c),
                                                  send_sem=ici_send.at[3 * a + j], recv_sem=ici_recv.at[3 * a + j],
                                                  device_id=(px, py, c), device_id_type=MESH)
                cp.start()
                sends.append(cp)
        for a in range(n):
            size = ins[a].shape[axes[a]]
            for j, (px, py) in enumerate(chips):
                landed = _piece(outs[a], axes[a], 2 * px + py, size, c)
                pltpu.make_async_remote_copy(src_ref=ins[a].at[pl.ds(c, 1)], dst_ref=landed, send_sem=ici_send.at[3 * a + j],
                                             recv_sem=ici_recv.at[3 * a + j], device_id=(px, py, c), device_id_type=MESH).wait_recv()
                cp = pltpu.make_async_remote_copy(src_ref=landed, dst_ref=landed, send_sem=d2d_send.at[3 * a + j],
                                                  recv_sem=d2d_recv.at[3 * a + j], device_id=(x, y, 1 - c), device_id_type=MESH)
                cp.start()
                sends.append(cp)
        for a in range(n):
            size = ins[a].shape[axes[a]]
            for j, (px, py) in enumerate(chips):
                passed = _piece(outs[a], axes[a], 2 * px + py, size, 1 - c)
                pltpu.make_async_remote_copy(src_ref=passed, dst_ref=passed, send_sem=d2d_send.at[3 * a + j],
                                             recv_sem=d2d_recv.at[3 * a + j], device_id=(x, y, 1 - c), device_id_type=MESH).wait_recv()
        for cp in sends:
            cp.wait_send()
        for cp in local:
            cp.wait()

    hbm = pl.BlockSpec(memory_space=pl.ANY)
    return pl.pallas_call(
        body, name=name, out_shape=fulls, in_specs=[hbm] * n, out_specs=[hbm] * n,
        scratch_shapes=[pltpu.SemaphoreType.DMA((3 * n,))] * 4 + [pltpu.SemaphoreType.DMA((n,))])(*shards)


def _swap_layers(arrays, name):
    n = len(arrays)

    def body(*refs):
        ins, outs = refs[:n], refs[n:2 * n]
        send_sems, recv_sems = refs[2 * n:]
        x, y, c = _here()
        copies = []
        for a in range(n):
            cp = pltpu.make_async_remote_copy(src_ref=ins[a].at[pl.ds(1 - c, 1)], dst_ref=outs[a], send_sem=send_sems.at[a],
                                              recv_sem=recv_sems.at[a], device_id=(x, y, 1 - c), device_id_type=MESH)
            cp.start()
            copies.append(cp)
        for cp in copies:
            cp.wait()

    hbm = pl.BlockSpec(memory_space=pl.ANY)
    return pl.pallas_call(
        body, name=name, out_shape=[_sds((1, *a.shape[1:]), a.dtype) for a in arrays], in_specs=[hbm] * n, out_specs=[hbm] * n,
        scratch_shapes=[pltpu.SemaphoreType.DMA((n,)), pltpu.SemaphoreType.DMA((n,))])(*arrays)


def _scatter_chips(fulls, axes, name):
    n = len(fulls)
    recvs = []
    for a, ax in zip(fulls, axes):
        shp = list(a.shape)
        shp[ax] //= 4
        recvs.append(_sds((3, *shp), a.dtype))

    def body(*refs):
        ins, outs = refs[:n], refs[n:2 * n]
        send_sems, recv_sems = refs[2 * n:]
        x, y, c = _here()
        sends = []
        for a in range(n):
            size = ins[a].shape[axes[a]] // 4
            for k in range(1, 4):
                peer = (_flip(x, k & 2), _flip(y, k & 1), c)
                cp = pltpu.make_async_remote_copy(src_ref=_shard_of(ins[a], axes[a], 2 * peer[0] + peer[1], size),
                                                  dst_ref=outs[a].at[k - 1],
                                                  send_sem=send_sems.at[3 * a + k - 1], recv_sem=recv_sems.at[3 * a + k - 1],
                                                  device_id=peer, device_id_type=MESH)
                cp.start()
                sends.append(cp)
        for cp in sends:
            cp.wait_recv()
        for cp in sends:
            cp.wait_send()

    hbm = pl.BlockSpec(memory_space=pl.ANY)
    return pl.pallas_call(
        body, name=name, out_shape=recvs, in_specs=[hbm] * n, out_specs=[hbm] * n,
        scratch_shapes=[pltpu.SemaphoreType.DMA((3 * n,)), pltpu.SemaphoreType.DMA((3 * n,))])(*fulls)


def _sibling_swap(arrays, name):
    n = len(arrays)

    def body(*refs):
        ins, outs = refs[:n], refs[n:2 * n]
        send_sems, recv_sems = refs[2 * n:]
        x, y, c = _here()
        copies = []
        for a in range(n):
            cp = pltpu.make_async_remote_copy(src_ref=ins[a], dst_ref=outs[a], send_sem=send_sems.at[a], recv_sem=recv_sems.at[a],
                                              device_id=(x, y, 1 - c), device_id_type=MESH)
            cp.start()
            copies.append(cp)
        for cp in copies:
            cp.wait()

    hbm = pl.BlockSpec(memory_space=pl.ANY)
    return pl.pallas_call(
        body, name=name, out_shape=[_sds(a.shape, a.dtype) for a in arrays], in_specs=[hbm] * n, out_specs=[hbm] * n,
        scratch_shapes=[pltpu.SemaphoreType.DMA((n,)), pltpu.SemaphoreType.DMA((n,))])(*arrays)


def _ffn_fwd(xs, g, mods, k0, w_up, w_down, s, tag):
    h, ua, ub, act = _norm_mm(xs, g, mods, k0, k0 + 1, w_up, s, tag + "_up", True)
    y, xn = _mm_res(act, w_down, xs, mods, k0 + 2, 0.5, s, tag + "_down")
    return xn, (xs, h, ua, ub, act, y)


def _ffn_bwd(dxn, saved, g, mods, k0, w_up, w_down, s, tag):
    xs, h, ua, ub, act, y = saved
    dy, dgate, dua, dub = _resb_mm(dxn, y, mods, k0 + 2, 0.5, w_down, s, tag + "_down_dx", (ua, ub))
    d_down = _mm(act, dy, "tn", F32, tag + "_down_dw")
    d_up = jnp.concatenate([_mm(h, dua, "tn", F32, tag + "_upa_dw"), _mm(h, dub, "tn", F32, tag + "_upb_dw")], axis=1)
    dx, dsh, dsc, dg = _mm_normb([dua, dub], w_up, xs, dxn, g, mods, k0 + 1, s, tag + "_up_dx")
    return dx, d_up, d_down, dg, [dsh, dsc, dgate]


def _mix_fwd(xs, g, mods, wl, pl_, tabs, s, tag):
    cos, sin = tabs
    h, p = _norm_mm(xs, g, mods, 3, 4, wl["w_in"], s, tag + "_in", False)
    q, k, v = _rope(p, cos, sin, tag + "_rope")
    bias = _bias_table(pl_["rpb"], s // GRID_W)
    oa = _na_fwd(q, k, v, bias, s, tag + "_na")
    ob = _gmlp(p, pl_["ln_v_g"], pl_["ln_v_b"], pl_["w_s"], pl_["b_s"], tag + "_sg")
    pa = _mm(oa, wl["w_pa"], "nn", MXU, tag + "_pa")
    pb = _mm(ob, wl["w_pb"], "nn", MXU, tag + "_pb")
    mg = _merge(pa, pb, p, pl_["b_gate"], tag + "_merge")
    y, xn = _mm_res(mg, wl["w_o"], xs, mods, 5, 1.0, s, tag + "_o")
    return xn, (xs, h, p, q, k, v, bias, oa, ob, pa, pb, mg, y)


def _mix_bwd(dxn, saved, g, mods, wl, pl_, tabs, s, tag):
    xs, h, p, q, k, v, bias, oa, ob, pa, pb, mg, y = saved
    cos, sin = tabs
    gw, gp = {}, {}
    dy, dgate, dmg = _resb_mm(dxn, y, mods, 5, 1.0, wl["w_o"], s, tag + "_o_dx")
    gw["w_o"] = _mm(mg, dy, "tn", F32, tag + "_o_dw")
    dpa, dpb, dla, dlb, dba, dbb = _merge_bwd(dmg, pa, pb, p, pl_["b_gate"], tag + "_merge_b")
    gp["b_gate"] = jnp.concatenate([dba, dbb], axis=1)
    gw["w_pa"] = _mm(oa, dpa, "tn", F32, tag + "_pa_dw")
    doa = _mm(dpa, wl["w_pa"], "nt", MXU, tag + "_pa_dx")
    gw["w_pb"] = _mm(ob, dpb, "tn", F32, tag + "_pb_dw")
    dob = _mm(dpb, wl["w_pb"], "nt", MXU, tag + "_pb_dx")
    du, dvs, gp["w_s"], gp["b_s"], gp["ln_v_g"], gp["ln_v_b"] = _gmlp_bwd(
        p, dob, pl_["ln_v_g"], pl_["ln_v_b"], pl_["w_s"], pl_["b_s"], tag + "_sg_b")
    dqr, dkr, dv, dbias = _na_bwd(q, k, v, doa, bias, s, tag + "_na_b")
    gp["rpb"] = _rpb_grad(dbias, s // GRID_W, tag + "_rpb")
    dq, dk, dvv = _rope_bwd(dqr, dkr, dv, cos, sin, tag + "_rope_b")
    dp = jnp.concatenate([dq, dk, dvv, du, dvs, dla, dlb], axis=1)
    gw["w_in"] = _mm(h, dp, "tn", F32, tag + "_in_dw")
    dx, dsh, dsc, dg = _mm_normb([dp], wl["w_in"], xs, dxn, g, mods, 4, s, tag + "_in_dx")
    return dx, gw, gp, dg, [dsh, dsc, dgate]


def _local_step(x, ctx, tgt, mods, wts, prm):
    s, d = x.shape
    depth = mods.shape[0]
    tabs = _rope_tables(s, ctx.shape[0])
    xs = jnp.concatenate([x, ctx], axis=0)
    saved = []
    for l in range(depth):
        wl = {k: v[l] for k, v in wts.items()}
        pl_ = _layer_params(prm, l)
        xs, s1 = _ffn_fwd(xs, pl_["g"][0], mods[l], 0, wl["w_ff1_up"], wl["w_ff1_down"], s, f"l{l}_ff1")
        xs, s2 = _mix_fwd(xs, pl_["g"][1], mods[l], wl, pl_, tabs, s, f"l{l}_mix")
        xs, s3 = _ffn_fwd(xs, pl_["g"][2], mods[l], 6, wl["w_ff2_up"], wl["w_ff2_down"], s, f"l{l}_ff2")
        saved.append((s1, s2, s3))
    loss, dxs, d_final_g = _final(xs, tgt, prm["final_g"].reshape(1, d), "final")
    gw = {k: [None] * depth for k in wts}
    gp = {k: [None] * depth for k in ("norm_g", "b_gate", "rpb", "ln_v_g", "ln_v_b", "w_s", "b_s")}
    dmods = [None] * depth
    for l in reversed(range(depth)):
        wl = {k: v[l] for k, v in wts.items()}
        pl_ = _layer_params(prm, l)
        s1, s2, s3 = saved[l]
        dxs, gw["w_ff2_up"][l], gw["w_ff2_down"][l], dg2, dm2 = _ffn_bwd(
            dxs, s3, pl_["g"][2], mods[l], 6, wl["w_ff2_up"], wl["w_ff2_down"], s, f"l{l}_ff2")
        dxs, gwm, gpm, dg1, dm1 = _mix_bwd(dxs, s2, pl_["g"][1], mods[l], wl, pl_, tabs, s, f"l{l}_mix")
        dxs, gw["w_ff1_up"][l], gw["w_ff1_down"][l], dg0, dm0 = _ffn_bwd(
            dxs, s1, pl_["g"][0], mods[l], 0, wl["w_ff1_up"], wl["w_ff1_down"], s, f"l{l}_ff1")
        for k, v in gwm.items():
            gw[k][l] = v
        gp["b_gate"][l] = gpm["b_gate"][0]
        gp["rpb"][l] = gpm["rpb"]
        gp["ln_v_g"][l] = gpm["ln_v_g"][0]
        gp["ln_v_b"][l] = gpm["ln_v_b"][0]
        gp["w_s"][l] = gpm["w_s"]
        gp["b_s"][l] = gpm["b_s"][..., 0]
        gp["norm_g"][l] = jnp.concatenate([dg0, dg1, dg2], axis=0)
        dmods[l] = jnp.concatenate(dm0 + dm1 + dm2, axis=1)
    gw = {k: jnp.stack(v) for k, v in gw.items()}
    gp = {k: jnp.stack(v) for k, v in gp.items()}
    gp["final_g"] = d_final_g[0]
    return loss[0, 0], dxs[:s], jnp.stack(dmods), gw, gp


def _layer_params(prm, l):
    d = prm["norm_g"].shape[-1]
    return {
        "g": [prm["norm_g"][l, i].reshape(1, d) for i in range(3)],
        "b_gate": prm["b_gate"][l].reshape(1, -1),
        "rpb": prm["rpb"][l],
        "ln_v_g": prm["ln_v_g"][l].reshape(1, -1),
        "ln_v_b": prm["ln_v_b"][l].reshape(1, -1),
        "w_s": prm["w_s"][l],
        "b_s": prm["b_s"][l][..., None],
    }


SMALL = ("norm_g", "b_gate", "rpb", "ln_v_g", "ln_v_b", "w_s", "b_s", "final_g")
PACK_LANES = 1024


def _pack(parts):
    flat = jnp.concatenate([p.reshape(-1) for p in parts])
    rows = -(-flat.shape[0] // PACK_LANES)
    rows = -(-rows // 8) * 8
    return jnp.pad(flat, (0, rows * PACK_LANES - flat.shape[0])).reshape(rows, PACK_LANES)


def _unpack(flat, shapes):
    out, off = [], 0
    for shp in shapes:
        n = int(np.prod(shp))
        out.append(flat[..., off:off + n].reshape(*flat.shape[:-1], *shp))
        off += n
    return out


def kernel(x, c, ctx, c_ctx, w_ada, b_ada, norm_g, w_ff1_up, w_ff1_down, w_in, b_gate, rpb, ln_v_g, ln_v_b, w_s, b_s, w_pa, w_pb, w_o, w_ff2_up, w_ff2_down, final_g, loss_target, m_c_ctx, m_w_ada, m_b_ada, m_norm_g, m_w_ff1_up, m_w_ff1_down, m_w_in, m_b_gate, m_rpb, m_ln_v_g, m_ln_v_b, m_w_s, m_b_s, m_w_pa, m_w_pb, m_w_o, m_w_ff2_up, m_w_ff2_down, m_final_g, v_c_ctx, v_w_ada, v_b_ada, v_norm_g, v_w_ff1_up, v_w_ff1_down, v_w_in, v_b_gate, v_rpb, v_ln_v_g, v_ln_v_b, v_w_s, v_b_s, v_w_pa, v_w_pb, v_w_o, v_w_ff2_up, v_w_ff2_down, v_final_g):
    weights = dict(c_ctx=c_ctx, w_ada=w_ada, b_ada=b_ada, norm_g=norm_g, w_ff1_up=w_ff1_up, w_ff1_down=w_ff1_down, w_in=w_in,
                   b_gate=b_gate, rpb=rpb, ln_v_g=ln_v_g, ln_v_b=ln_v_b, w_s=w_s, b_s=b_s, w_pa=w_pa, w_pb=w_pb, w_o=w_o,
                   w_ff2_up=w_ff2_up, w_ff2_down=w_ff2_down, final_g=final_g)
    mom_m = dict(c_ctx=m_c_ctx, w_ada=m_w_ada, b_ada=m_b_ada, norm_g=m_norm_g, w_ff1_up=m_w_ff1_up, w_ff1_down=m_w_ff1_down,
                 w_in=m_w_in, b_gate=m_b_gate, rpb=m_rpb, ln_v_g=m_ln_v_g, ln_v_b=m_ln_v_b, w_s=m_w_s, b_s=m_b_s, w_pa=m_w_pa,
                 w_pb=m_w_pb, w_o=m_w_o, w_ff2_up=m_w_ff2_up, w_ff2_down=m_w_ff2_down, final_g=m_final_g)
    mom_v = dict(c_ctx=v_c_ctx, w_ada=v_w_ada, b_ada=v_b_ada, norm_g=v_norm_g, w_ff1_up=v_w_ff1_up, w_ff1_down=v_w_ff1_down,
                 w_in=v_w_in, b_gate=v_b_gate, rpb=v_rpb, ln_v_g=v_ln_v_g, ln_v_b=v_ln_v_b, w_s=v_w_s, b_s=v_b_s, w_pa=v_w_pa,
                 w_pb=v_w_pb, w_o=v_w_o, w_ff2_up=v_w_ff2_up, w_ff2_down=v_w_ff2_down, final_g=v_final_g)
    order = list(weights)
    mx, my, mc = _here()
    dev = 4 * mx + 2 * my + mc
    chip = 2 * mx + my
    depth, d, n_ada = w_ada.shape
    dq = d // 4

    c_all = _allgather8(jnp.pad(c, ((0, 7), (0, 0))), "gather_c")[:, 0, :]
    cond = jnp.concatenate([c_all, c_ctx[None, :], jnp.zeros((7, d), F32)], axis=0)
    b_shard = lax.dynamic_slice(b_ada, (0, chip * n_ada), (depth, n_ada))
    proj = [_ada_fwd(cond, w_ada[l], b_shard[l:l + 1], f"ada{l}") for l in range(depth)]
    silu_c = proj[0][1]
    mods_sh = _allgather8(jnp.concatenate([p[0] for p in proj], axis=0), "gather_mods")
    mods_all = jnp.transpose(mods_sh[0::2].reshape(4, depth, 16, n_ada), (1, 2, 0, 3)).reshape(depth, 16, N_MOD, d)
    mods = jnp.stack([lax.dynamic_index_in_dim(mods_all, dev, axis=1, keepdims=False), mods_all[:, 8]], axis=1)

    full = _gather_chips([weights[k].astype(MXU) for k in BIG], [SHARD_AXIS[k] for k in BIG], "gather_w")
    wts = dict(zip(BIG, full))
    prm = {k: weights[k] for k in SMALL if k != "norm_g"}
    norm_full = _allgather8(jnp.pad(norm_g.reshape(depth * 3, dq), ((0, 8 - depth * 3), (0, 0))), "gather_norm_g")
    prm["norm_g"] = jnp.transpose(norm_full[0::2, :depth * 3].reshape(4, depth, 3, dq), (1, 2, 0, 3)).reshape(depth, 3, d)

    loss, grad_x, dmods, gw, gp = _local_step(x[0], ctx[0], loss_target[0], mods, wts, prm)
    loss = lax.psum(loss, ("x", "y", "c"))

    small_shapes = [(depth, 2, N_MOD * d)] + [weights[k].shape if k != "norm_g" else (depth, 3, d) for k in SMALL]
    packed = _allgather8(_pack([dmods.reshape(depth, 2, N_MOD * d)] + [gp[k] for k in SMALL]), "gather_small")
    rows = packed.shape[1]
    total = _sum_pieces([packed[i] for i in range(8)], "sum_small")[0].reshape(-1)
    sums = dict(zip(("dmods",) + SMALL, _unpack(total, small_shapes)))
    dmods_dev = _unpack(packed.reshape(8, rows * PACK_LANES), small_shapes[:1])[0]

    g_ada, cc_parts = [], []
    for l in range(depth):
        dm = jnp.concatenate([dmods_dev[:, l, 0], sums["dmods"][l, 1][None], jnp.zeros((7, N_MOD * d), F32)], axis=0)
        dm_sh = lax.dynamic_slice(dm, (0, chip * n_ada), (16, n_ada))
        g_ada.append(_mm(silu_c, dm_sh, "tn", F32, f"ada{l}_dw"))
        cc_parts.append(_mm(dm_sh, w_ada[l], "nt", F32, f"ada{l}_dc")[8:9])
    cc_all = _allgather8(jnp.pad(jnp.concatenate(cc_parts, axis=0), ((0, 8 - depth), (0, 0))), "gather_cctx")
    g_cctx = _cctx_grad(cc_all[0::2, :depth].reshape(4 * depth, d), c_ctx.reshape(1, d), "cctx_grad")

    axes = [SHARD_AXIS[k] for k in BIG]
    from_sibling = _swap_layers([gw[k] for k in BIG], "swap_layer_gw")
    pair = [_sum_pieces([lax.dynamic_slice_in_dim(gw[k], mc, 1, axis=0), r], "pair_" + k, (F32, MXU)) for k, r in zip(BIG, from_sibling)]
    recv = _scatter_chips([p[1] for p in pair], axes, "scatter_gw")
    mine = []
    for k, ax, p, r in zip(BIG, axes, pair, recv):
        size = p[0].shape[ax] // 4
        own = lax.dynamic_slice_in_dim(p[0], chip * size, size, axis=ax)
        mine.append(_sum_pieces([own, r[0], r[1], r[2]], "sum_" + k)[0])
    other = _sibling_swap(mine, "swap_gw")

    pieces = {k: [jnp.concatenate([jnp.where(mc == 0, a, b), jnp.where(mc == 0, b, a)], axis=0)] for k, a, b in zip(BIG, mine, other)}
    pieces["w_ada"] = [jnp.stack(g_ada)]
    pieces["b_ada"] = [sums["dmods"][:, 0], sums["dmods"][:, 1]]
    pieces["c_ctx"] = [g_cctx[0]]
    for k in SMALL:
        pieces[k] = [sums[k]]
    pieces["norm_g"] = [lax.dynamic_slice_in_dim(sums["norm_g"], chip * dq, dq, axis=2)]
    res = {k: _adamw(weights[k], pieces[k], mom_m[k], mom_v[k], "adamw_" + k) for k in order}
    return (loss, grad_x[None], *[res[k][0] for k in order], *[res[k][1] for k in order],
            *[res[k][2] for k in order], *[res[k][3] for k in order])
```

```python
import numpy as np
import jax
import jax.numpy as jnp
from jax import lax
from jax.experimental import pallas as pl
from jax.experimental.pallas import tpu as pltpu

F32 = jnp.float32
MXU = jnp.bfloat16
EPS = 1e-6
GRID_W, HEADS, HEAD_DIM = 64, 8, 64
NA_WIDTH = SG_WIDTH = 512
WIN_H, WIN_W = 8, 16
SG_CHUNK, SG_GROUPS = 128, 4
N_MOD = 9
ROPE_THETA = 10000.0
Q_ROWS, K_ROWS = 4, 12
TQ, TK = Q_ROWS * GRID_W, K_ROWS * GRID_W
TM = 256
LANES = 128
NEG = -1e30
VMEM_LIMIT = 56 * 2 ** 20
ADAM_LR, ADAM_B1, ADAM_B2, ADAM_EPS, ADAM_WD, ADAM_STEP = 0.001, 0.9, 0.999, 1e-08, 0.01, 10
MESH = pl.DeviceIdType.MESH
BIG = ("w_ff1_up", "w_ff1_down", "w_in", "w_pa", "w_pb", "w_o", "w_ff2_up", "w_ff2_down")
SHARD_AXIS = {"w_ff1_up": 2, "w_ff1_down": 1, "w_in": 2, "w_pa": 2, "w_pb": 2, "w_o": 1, "w_ff2_up": 2, "w_ff2_down": 1}


def _call(body, *, name, grid, in_specs, out_specs, out_shape, scratch=()):
    return pl.pallas_call(
        body, name=name, grid=grid, in_specs=in_specs, out_specs=out_specs, out_shape=out_shape,
        scratch_shapes=list(scratch),
        compiler_params=pltpu.CompilerParams(dimension_semantics=("arbitrary",) * len(grid), vmem_limit_bytes=VMEM_LIMIT))


def _pick(n, prefs):
    for p in prefs:
        if n % p == 0:
            return p
    return n


def _row_tile(t):
    return _pick(t, (640, 256))


def _rows(tm, n, col=0):
    return pl.BlockSpec((tm, n), lambda i: (i, col))


def _fixed(shape):
    return pl.BlockSpec(shape, lambda *_: (0,) * len(shape))


def _sds(shape, dtype):
    return jax.ShapeDtypeStruct(shape, dtype)


def _mm(a, b, mode, out_dtype, name):
    if mode == "tn":
        r, m = a.shape
        n = b.shape[1]
        tm = _pick(m, (1024, 1408, 704, 512, 256, 128))
        tn = _pick(n, (512, 1408, 256, 128))
        tr = _pick(r, (1280, 640, 512, 256, 128))

        def body(a_ref, b_ref, o_ref):
            @pl.when(pl.program_id(2) == 0)
            def _():
                o_ref[...] = jnp.zeros_like(o_ref)

            o_ref[...] += lax.dot_general(a_ref[...].astype(MXU), b_ref[...].astype(MXU), (((0,), (0,)), ((), ())),
                                          preferred_element_type=F32)

        return _call(body, name=name, grid=(m // tm, n // tn, r // tr),
                     in_specs=[pl.BlockSpec((tr, tm), lambda i, j, k: (k, i)), pl.BlockSpec((tr, tn), lambda i, j, k: (k, j))],
                     out_specs=pl.BlockSpec((tm, tn), lambda i, j, k: (i, j)), out_shape=_sds((m, n), F32))(a, b)
    m, k = a.shape
    n = b.shape[1] if mode == "nn" else b.shape[0]
    tm = _pick(m, (1280, 640, 512, 256, 128) if k <= 2816 else (640, 512, 256, 128))
    tn = _pick(n, (512, 1408, 256, 128))
    dims = (((1,), (0,)), ((), ())) if mode == "nn" else (((1,), (1,)), ((), ()))

    def body(a_ref, b_ref, o_ref):
        o_ref[...] = lax.dot_general(a_ref[...].astype(MXU), b_ref[...].astype(MXU), dims,
                                     preferred_element_type=F32).astype(o_ref.dtype)

    b_spec = pl.BlockSpec((k, tn), lambda i, j: (0, j)) if mode == "nn" else pl.BlockSpec((tn, k), lambda i, j: (j, 0))
    return _call(body, name=name, grid=(m // tm, n // tn), in_specs=[pl.BlockSpec((tm, k), lambda i, j: (i, 0)), b_spec],
                 out_specs=pl.BlockSpec((tm, tn), lambda i, j: (i, j)), out_shape=_sds((m, n), out_dtype))(a, b)


def _ctx_rows(i, tm, s):
    return (i * tm + lax.broadcasted_iota(jnp.int32, (tm, 1), 0)) >= s


def _mod_row(m_ref, k, ctx):
    return jnp.where(ctx, m_ref[1, k:k + 1, :], m_ref[0, k:k + 1, :])


def _stream_sums(i, tm, s, refs_and_vals):
    @pl.when((i + 1) * tm <= s)
    def _():
        for ref, val in refs_and_vals:
            ref[0] += jnp.sum(val, axis=0, keepdims=True)

    @pl.when((i + 1) * tm > s)
    def _():
        ctx = _ctx_rows(i, tm, s)
        for ref, val in refs_and_vals:
            ref[0] += jnp.sum(jnp.where(ctx, 0.0, val), axis=0, keepdims=True)
            ref[1] += jnp.sum(jnp.where(ctx, val, 0.0), axis=0, keepdims=True)


def _norm_mm(xs, g, mods, k_shift, k_scale, w, s, name, glu):
    t, d = xs.shape
    n = w.shape[1] // 2 if glu else w.shape[1]
    tm = _pick(t, (1280, 640, 256))
    tn = _pick(n, (256, 128)) if glu else _pick(n, (512, 256, 128))
    nj = n // tn

    def body(x_ref, g_ref, m_ref, *refs):
        i, j = pl.program_id(0), pl.program_id(1)
        w_refs, h_ref, o_refs = refs[:2 if glu else 1], refs[2 if glu else 1], refs[3 if glu else 2:]

        @pl.when(j == 0)
        def _():
            x = x_ref[...]
            rstd = lax.rsqrt(jnp.mean(x * x, axis=-1, keepdims=True) + EPS)
            ctx = _ctx_rows(i, tm, s)
            h = x * rstd * g_ref[...] * (1.0 + _mod_row(m_ref, k_scale, ctx)) + _mod_row(m_ref, k_shift, ctx)
            h_ref[...] = h.astype(h_ref.dtype)

        h = h_ref[...]
        a = _nn(h, w_refs[0][...])
        o_refs[0][...] = a.astype(o_refs[0].dtype)
        if glu:
            b = _nn(h, w_refs[1][...])
            o_refs[1][...] = b.astype(o_refs[1].dtype)
            o_refs[2][...] = (a * jax.nn.sigmoid(a) * b).astype(o_refs[2].dtype)

    tile = pl.BlockSpec((tm, tn), lambda i, j: (i, j))
    row = pl.BlockSpec((tm, d), lambda i, j: (i, 0))
    w_specs = [pl.BlockSpec((d, tn), lambda i, j: (0, j))] + ([pl.BlockSpec((d, tn), lambda i, j: (0, j + nj))] if glu else [])
    n_out = 3 if glu else 1
    return _call(body, name=name, grid=(t // tm, nj),
                 in_specs=[row, _fixed((1, d)), _fixed((2, N_MOD, d))] + w_specs,
                 out_specs=[row] + [tile] * n_out,
                 out_shape=[_sds((t, d), MXU)] + [_sds((t, n), MXU)] * n_out)(xs, g, mods, *([w, w] if glu else [w]))


def _mm_res(a, w, xs, mods, k_gate, coef, s, name):
    t, k = a.shape
    d = w.shape[1]
    tm = _pick(t, (1280, 640, 256))
    tn = _pick(d, (512, 256, 128))

    def body(a_ref, w_ref, x_ref, m_ref, y_ref, o_ref):
        y = _nn(a_ref[...], w_ref[...])
        y_ref[...] = y.astype(y_ref.dtype)
        gate = _mod_row(m_ref, k_gate, _ctx_rows(pl.program_id(0), tm, s))
        o_ref[...] = x_ref[...] + (coef * gate) * y

    tile = pl.BlockSpec((tm, tn), lambda i, j: (i, j))
    return _call(body, name=name, grid=(t // tm, d // tn),
                 in_specs=[pl.BlockSpec((tm, k), lambda i, j: (i, 0)), pl.BlockSpec((k, tn), lambda i, j: (0, j)), tile,
                           pl.BlockSpec((2, N_MOD, tn), lambda i, j: (0, 0, j))],
                 out_specs=[tile, tile], out_shape=[_sds((t, d), MXU), _sds((t, d), F32)])(a, w, xs, mods)


def _resb_mm(dxn, y, mods, k_gate, coef, w, s, name, ups=None):
    t, d = dxn.shape
    n = w.shape[0]
    tm = _pick(t, (1280, 640, 256))
    tn = _pick(n, (256, 128)) if ups else _pick(n, (512, 256, 128))

    def body(dx_ref, y_ref, m_ref, w_ref, *refs):
        i, j = pl.program_id(0), pl.program_id(1)
        u_refs, (dy_ref, dgt_ref), o_refs = (refs[:2], refs[2:4], refs[4:]) if ups else ((), refs[:2], refs[2:])

        @pl.when((i == 0) & (j == 0))
        def _():
            dgt_ref[...] = jnp.zeros_like(dgt_ref)

        @pl.when(j == 0)
        def _():
            dx = dx_ref[...]
            gate = _mod_row(m_ref, k_gate, _ctx_rows(i, tm, s))
            dy_ref[...] = ((coef * gate) * dx).astype(dy_ref.dtype)
            _stream_sums(i, tm, s, [(dgt_ref, coef * y_ref[...].astype(F32) * dx)])

        dact = _nt(dy_ref[...], w_ref[...])
        if ups:
            a, b = u_refs[0][...].astype(F32), u_refs[1][...].astype(F32)
            sg = jax.nn.sigmoid(a)
            o_refs[0][...] = (dact * b * sg * (1.0 + a * (1.0 - sg))).astype(o_refs[0].dtype)
            o_refs[1][...] = (dact * a * sg).astype(o_refs[1].dtype)
        else:
            o_refs[0][...] = dact.astype(o_refs[0].dtype)

    tile = pl.BlockSpec((tm, tn), lambda i, j: (i, j))
    row = pl.BlockSpec((tm, d), lambda i, j: (i, 0))
    n_out = 2 if ups else 1
    return _call(body, name=name, grid=(t // tm, n // tn),
                 in_specs=[row, row, _fixed((2, N_MOD, d)), pl.BlockSpec((tn, d), lambda i, j: (j, 0))] + ([tile, tile] if ups else []),
                 out_specs=[row, _fixed((2, 1, d))] + [tile] * n_out,
                 out_shape=[_sds((t, d), MXU), _sds((2, 1, d), F32)] + [_sds((t, n), MXU)] * n_out)(
                     dxn, y, mods, w, *(ups or ()))


def _mm_normb(a_list, w, xs, dres, g, mods, k_scale, s, name):
    t, d = xs.shape
    ka = a_list[0].shape[1]
    tm = _pick(t, (640, 256))
    tk = _pick(ka, (1408, 1536, 1024, 512, 256, 128))
    nk1 = ka // tk
    n_a = len(a_list)
    nk = nk1 * n_a

    def body(*refs):
        a_refs, (w_ref, x_ref, dr_ref, g_ref, m_ref, dx_ref, dsh_ref, dsc_ref, dg_ref, acc) = refs[:n_a], refs[n_a:]
        i, k = pl.program_id(0), pl.program_id(1)

        @pl.when((i == 0) & (k == 0))
        def _():
            dsh_ref[...] = jnp.zeros_like(dsh_ref)
            dsc_ref[...] = jnp.zeros_like(dsc_ref)
            dg_ref[...] = jnp.zeros_like(dg_ref)

        @pl.when(k == 0)
        def _():
            acc[...] = jnp.zeros_like(acc)

        for q in range(n_a):
            @pl.when((k >= q * nk1) & (k < (q + 1) * nk1))
            def _():
                acc[...] += _nt(a_refs[q][...], w_ref[...])

        @pl.when(k == nk - 1)
        def _():
            x = x_ref[...]
            dh = acc[...]
            rstd = lax.rsqrt(jnp.mean(x * x, axis=-1, keepdims=True) + EPS)
            xhat = x * rstd
            gg = g_ref[...]
            _stream_sums(i, tm, s, [(dsh_ref, dh), (dsc_ref, dh * (xhat * gg))])
            dy = dh * (1.0 + _mod_row(m_ref, k_scale, _ctx_rows(i, tm, s)))
            dg_ref[...] += jnp.sum(dy * xhat, axis=0, keepdims=True)
            dxh = dy * gg
            dx_ref[...] = dr_ref[...] + rstd * (dxh - xhat * jnp.mean(dxh * xhat, axis=-1, keepdims=True))

    row = pl.BlockSpec((tm, d), lambda i, k: (i, 0))
    a_specs = [pl.BlockSpec((tm, tk), lambda i, k, q=q: (i, jnp.clip(k - q * nk1, 0, nk1 - 1))) for q in range(n_a)]
    return _call(body, name=name, grid=(t // tm, nk),
                 in_specs=a_specs + [pl.BlockSpec((d, tk), lambda i, k: (0, k)), row, row, _fixed((1, d)), _fixed((2, N_MOD, d))],
                 out_specs=[row, _fixed((2, 1, d)), _fixed((2, 1, d)), _fixed((1, d))],
                 out_shape=[_sds((t, d), F32), _sds((2, 1, d), F32), _sds((2, 1, d), F32), _sds((1, d), F32)],
                 scratch=[pltpu.VMEM((tm, d), F32)])(*a_list, w, xs, dres, g, mods)


def _rope_tables(s, ctx_len):
    n_freq = HEAD_DIM // 4
    tok = jnp.arange(s)
    freqs = ROPE_THETA ** (-jnp.arange(n_freq, dtype=F32) / n_freq)
    ang = jnp.concatenate([(tok // GRID_W).astype(F32)[:, None] * freqs, (tok % GRID_W).astype(F32)[:, None] * freqs], axis=-1)
    cos = jnp.repeat(jnp.cos(ang), 2, axis=-1)
    sin = jnp.repeat(jnp.sin(ang), 2, axis=-1) * jnp.tile(jnp.array([-1.0, 1.0], F32), HEAD_DIM // 2)
    cos = jnp.concatenate([jnp.tile(cos, (1, HEADS)), jnp.ones((ctx_len, NA_WIDTH), F32)], axis=0)
    sin = jnp.concatenate([jnp.tile(sin, (1, HEADS)), jnp.zeros((ctx_len, NA_WIDTH), F32)], axis=0)
    return cos, sin


def _swap_pairs(x):
    n = x.shape[-1]
    lane = lax.broadcasted_iota(jnp.int32, x.shape, 1)
    return jnp.where(lane % 2 == 0, pltpu.roll(x, n - 1, 1), pltpu.roll(x, 1, 1))


def _rope(p, cos, sin, name):
    t = p.shape[0]
    w = NA_WIDTH
    te = _row_tile(t)

    def body(q_ref, k_ref, v_ref, c_ref, s_ref, qo_ref, ko_ref, vo_ref):
        c, s = c_ref[...], s_ref[...]
        q, k = q_ref[...].astype(F32), k_ref[...].astype(F32)
        qo_ref[...] = (q * c + _swap_pairs(q) * s).astype(qo_ref.dtype)
        ko_ref[...] = (k * c + _swap_pairs(k) * s).astype(ko_ref.dtype)
        vo_ref[...] = v_ref[...].astype(vo_ref.dtype)

    return _call(body, name=name, grid=(t // te,),
                 in_specs=[_rows(te, w, 0), _rows(te, w, 1), _rows(te, w, 2), _rows(te, w), _rows(te, w)],
                 out_specs=[_rows(te, w)] * 3, out_shape=[_sds((t, w), MXU)] * 3)(p, p, p, cos, sin)


def _rope_bwd(dq, dk, dv, cos, sin, name):
    t = dq.shape[0]
    te = _row_tile(t)
    n_pairs = NA_WIDTH // LANES

    def body(dq_ref, dk_ref, dv_ref, c_ref, s_ref, qo_ref, ko_ref, vo_ref):
        a = dq_ref[...]
        qo_ref[...] = (a * c_ref[...] + _swap_pairs(a * s_ref[...])).astype(qo_ref.dtype)
        for hp in range(n_pairs):
            cols = slice(hp * LANES, (hp + 1) * LANES)
            b = dk_ref[hp]
            ko_ref[:, cols] = (b * c_ref[:, cols] + _swap_pairs(b * s_ref[:, cols])).astype(ko_ref.dtype)
            vo_ref[:, cols] = dv_ref[hp].astype(vo_ref.dtype)

    pairs = pl.BlockSpec((n_pairs, te, LANES), lambda i: (0, i, 0))
    return _call(body, name=name, grid=(t // te,),
                 in_specs=[_rows(te, NA_WIDTH), pairs, pairs, _rows(te, NA_WIDTH), _rows(te, NA_WIDTH)],
                 out_specs=[_rows(te, NA_WIDTH)] * 3, out_shape=[_sds((t, NA_WIDTH), MXU)] * 3)(dq, dk, dv, cos, sin)


def _na_geometry(r_grid):
    rows = []
    for r0, ks in ((0, 0), (Q_ROWS, 0), (r_grid - Q_ROWS, r_grid - K_ROWS)):
        dr = np.zeros((Q_ROWS, K_ROWS), np.int32)
        vr = np.zeros((Q_ROWS, K_ROWS), bool)
        for a in range(Q_ROWS):
            r = r0 + a
            rs = min(max(r - WIN_H // 2, 0), r_grid - WIN_H)
            for i in range(K_ROWS):
                kr = ks + i
                vr[a, i] = rs <= kr <= rs + WIN_H - 1
                dr[a, i] = kr - r + WIN_H - 1
        rows.append((dr, vr))
    c = np.arange(GRID_W)
    cs = np.clip(c - WIN_W // 2, 0, GRID_W - WIN_W)
    kc = np.arange(GRID_W)
    vc = (kc[None, :] >= cs[:, None]) & (kc[None, :] <= cs[:, None] + WIN_W - 1)
    dc = kc[None, :] - c[:, None] + WIN_W - 1
    return rows, dc, vc


def _bias_table(rpb, r_grid):
    rows, _, vc = _na_geometry(r_grid)
    n_dr, n_dc, skew, off = 2 * WIN_H - 1, 2 * WIN_W - 1, 2 * GRID_W - 1, GRID_W - WIN_W
    u = jnp.pad(rpb, ((0, 0), (0, 0), (off, skew - off - n_dc)))
    toep = jnp.tile(u, (1, 1, GRID_W + 1))[:, :, :2 * GRID_W * GRID_W].reshape(HEADS, n_dr, GRID_W, 2 * GRID_W)
    toep = jnp.pad(toep[:, :, ::-1, :GRID_W], ((0, 0), (Q_ROWS, Q_ROWS), (0, 0), (0, 0)))
    tabs = []
    for dr, vr in rows:
        per_row = []
        for a in range(Q_ROWS):
            lo = int(dr[a, 0]) + Q_ROWS
            valid = vr[a][:, None, None] & vc[None, :, :]
            per_row.append(jnp.where(valid[None], toep[:, lo:lo + K_ROWS], NEG))
        t = jnp.stack(per_row, axis=1)
        tabs.append(jnp.transpose(t, (0, 1, 3, 2, 4)).reshape(HEADS, TQ, TK))
    tabs.append(jnp.full((HEADS, TQ, TK), NEG, F32))
    return jnp.stack(tabs)


def _variant(g, ngx):
    return jnp.where(g == 0, 0, jnp.where(g >= ngx, 3, jnp.where(g == ngx - 1, 2, 1)))


def _key_start(g, r_grid):
    return pl.multiple_of(jnp.clip(g * Q_ROWS - WIN_H // 2, 0, r_grid - K_ROWS) * GRID_W, TQ)


def _nt(a, b):
    return lax.dot_general(a, b, (((1,), (1,)), ((), ())), preferred_element_type=F32)


def _tn(a, b):
    return lax.dot_general(a, b, (((0,), (0,)), ((), ())), preferred_element_type=F32)


def _nn(a, b):
    return jnp.dot(a, b, preferred_element_type=F32)


def _head_mask(h):
    lane = lax.broadcasted_iota(jnp.int32, (1, LANES), 1)
    return ((lane >= HEAD_DIM * h) & (lane < HEAD_DIM * (h + 1))).astype(F32)


def _softmax_parts(qm, knb, kcx, bias):
    s_nb = _nt(qm, knb) + bias
    s_cx = _nt(qm, kcx)
    m = jnp.maximum(jnp.max(s_nb, axis=-1, keepdims=True), jnp.max(s_cx, axis=-1, keepdims=True))
    e_nb = jnp.exp(s_nb - m)
    e_cx = jnp.exp(s_cx - m)
    inv = 1.0 / (jnp.sum(e_nb, axis=-1, keepdims=True) + jnp.sum(e_cx, axis=-1, keepdims=True))
    return e_nb * inv, e_cx * inv


def _na_specs(t, ngx):
    q_spec = pl.BlockSpec((TQ, LANES), lambda hp, g: (g, hp))
    kv_spec = pl.BlockSpec((t, LANES), lambda hp, g: (0, hp))
    b_spec = pl.BlockSpec((1, 2, TQ, TK), lambda hp, g: (_variant(g, ngx), hp, 0, 0))
    return q_spec, kv_spec, b_spec


def _na_fwd(q, k, v, bias, s, name):
    t = q.shape[0]
    ctx_len = t - s
    r_grid = s // GRID_W
    q_spec, kv_spec, b_spec = _na_specs(t, s // TQ)

    def body(q_ref, k_ref, v_ref, b_ref, o_ref):
        start = _key_start(pl.program_id(1), r_grid)
        qf = q_ref[...].astype(F32) * (HEAD_DIM ** -0.5)
        knb, vnb = k_ref[pl.ds(start, TK), :], v_ref[pl.ds(start, TK), :]
        kcx, vcx = k_ref[pl.ds(s, ctx_len), :], v_ref[pl.ds(s, ctx_len), :]
        acc = jnp.zeros((TQ, LANES), F32)
        for h in range(2):
            mask = _head_mask(h)
            p_nb, p_cx = _softmax_parts((qf * mask).astype(MXU), knb, kcx, b_ref[0, h])
            acc += (_nn(p_nb.astype(MXU), vnb) + _nn(p_cx.astype(MXU), vcx)) * mask
        o_ref[...] = acc.astype(o_ref.dtype)

    return _call(body, name=name, grid=(NA_WIDTH // LANES, t // TQ), in_specs=[q_spec, kv_spec, kv_spec, b_spec],
                 out_specs=q_spec, out_shape=_sds((t, NA_WIDTH), MXU))(q, k, v, bias)


def _na_bwd(q, k, v, do, bias, s, name):
    t = q.shape[0]
    ctx_len = t - s
    r_grid = s // GRID_W
    ng, ngx = t // TQ, s // TQ
    q_spec, kv_spec, b_spec = _na_specs(t, ngx)

    def body(q_ref, k_ref, v_ref, do_ref, b_ref, dq_ref, dk_hbm, dv_hbm, db_ref, dk_acc, dv_acc):
        hp, g = pl.program_id(0), pl.program_id(1)
        start = _key_start(g, r_grid)

        @pl.when(g == 0)
        def _():
            dk_acc[...] = jnp.zeros_like(dk_acc)
            dv_acc[...] = jnp.zeros_like(dv_acc)

        @pl.when((g == 0) | (g == 1) | (g == ngx - 1) | (g == ngx))
        def _():
            db_ref[...] = jnp.zeros_like(db_ref)

        qf = q_ref[...].astype(F32) * (HEAD_DIM ** -0.5)
        do = do_ref[...].astype(F32)
        knb, vnb = k_ref[pl.ds(start, TK), :], v_ref[pl.ds(start, TK), :]
        kcx, vcx = k_ref[pl.ds(s, ctx_len), :], v_ref[pl.ds(s, ctx_len), :]
        dq = jnp.zeros((TQ, LANES), F32)
        dk_nb = jnp.zeros((TK, LANES), F32)
        dv_nb = jnp.zeros((TK, LANES), F32)
        dk_cx = jnp.zeros((ctx_len, LANES), F32)
        dv_cx = jnp.zeros((ctx_len, LANES), F32)
        for h in range(2):
            mask = _head_mask(h)
            qm = (qf * mask).astype(MXU)
            dom = (do * mask).astype(MXU)
            p_nb, p_cx = _softmax_parts(qm, knb, kcx, b_ref[0, h])
            dp_nb = _nt(dom, vnb)
            dp_cx = _nt(dom, vcx)
            delta = jnp.sum(p_nb * dp_nb, axis=-1, keepdims=True) + jnp.sum(p_cx * dp_cx, axis=-1, keepdims=True)
            ds_nb = p_nb * (dp_nb - delta)
            ds_cx = p_cx * (dp_cx - delta)
            db_ref[0, h] += ds_nb
            ds_nb, ds_cx = ds_nb.astype(MXU), ds_cx.astype(MXU)
            dq += (_nn(ds_nb, knb) + _nn(ds_cx, kcx)) * (mask * (HEAD_DIM ** -0.5))
            dk_nb += _tn(ds_nb, qm)
            dk_cx += _tn(ds_cx, qm)
            dv_nb += _tn(p_nb.astype(MXU), dom)
            dv_cx += _tn(p_cx.astype(MXU), dom)
        dq_ref[...] = dq
        dk_acc[pl.ds(start, TK), :] += dk_nb
        dv_acc[pl.ds(start, TK), :] += dv_nb
        dk_acc[pl.ds(s, ctx_len), :] += dk_cx
        dv_acc[pl.ds(s, ctx_len), :] += dv_cx

        @pl.when(g == ng - 1)
        def _():
            pltpu.sync_copy(dk_acc, dk_hbm.at[hp])
            pltpu.sync_copy(dv_acc, dv_hbm.at[hp])

    n_pairs = NA_WIDTH // LANES
    hbm = pl.BlockSpec(memory_space=pl.ANY)
    return _call(body, name=name, grid=(n_pairs, ng), in_specs=[q_spec, kv_spec, kv_spec, q_spec, b_spec],
                 out_specs=[q_spec, hbm, hbm, b_spec],
                 out_shape=[_sds((t, NA_WIDTH), F32), _sds((n_pairs, t, LANES), F32), _sds((n_pairs, t, LANES), F32),
                            _sds((4, HEADS, TQ, TK), F32)],
                 scratch=[pltpu.VMEM((t, LANES), F32), pltpu.VMEM((t, LANES), F32)])(q, k, v, do, bias)


def _rpb_grad(dbias, r_grid, name):
    rows, _, _ = _na_geometry(r_grid)
    n_blk = 3 * Q_ROWS * K_ROWS
    z = dbias[:3].reshape(3, HEADS, Q_ROWS, GRID_W, K_ROWS, GRID_W)
    z = jnp.transpose(z, (1, 0, 2, 4, 3, 5)).reshape(HEADS, n_blk, GRID_W, GRID_W)
    z = jnp.pad(z[:, :, ::-1, :], ((0, 0), (0, 0), (0, 0), (0, GRID_W))).reshape(HEADS, n_blk, 2 * GRID_W * GRID_W)
    skew = 2 * GRID_W - 1
    z = jnp.pad(z, ((0, 0), (0, 0), (0, (GRID_W + 1) * skew - 2 * GRID_W * GRID_W))).reshape(HEADS, n_blk, GRID_W + 1, skew)
    z = jnp.pad(z, ((0, 0), (0, 0), (0, 72 - (GRID_W + 1)), (0, 1)))
    members = [[] for _ in range(2 * WIN_H - 1)]
    for vi, (dr, vr) in enumerate(rows):
        for a in range(Q_ROWS):
            for i in range(K_ROWS):
                if vr[a, i]:
                    members[dr[a, i]].append((vi * Q_ROWS + a) * K_ROWS + i)

    def body(z_ref, o_ref):
        zs = jnp.sum(z_ref[0], axis=1)
        out = []
        for mem in members:
            acc = jnp.zeros((1, LANES), F32)
            for j in mem:
                acc = acc + zs[j:j + 1, :]
            out.append(acc)
        out.append(jnp.zeros((1, LANES), F32))
        o_ref[0] = jnp.concatenate(out, axis=0)

    o = _call(body, name=name, grid=(HEADS,), in_specs=[pl.BlockSpec((1, n_blk, 72, LANES), lambda h: (h, 0, 0, 0))],
              out_specs=pl.BlockSpec((1, 16, LANES), lambda h: (h, 0, 0)), out_shape=_sds((HEADS, 16, LANES), F32))(z)
    off = GRID_W - 1 - (WIN_W - 1)
    return o[:, :2 * WIN_H - 1, off:off + 2 * WIN_W - 1]


_GELU_K, _GELU_C = 0.7978845608028654, 0.044715


def _gelu(x):
    return 0.5 * x * (1.0 + jnp.tanh(_GELU_K * (x + _GELU_C * x * x * x)))


def _gelu_grad(x):
    th = jnp.tanh(_GELU_K * (x + _GELU_C * x * x * x))
    return 0.5 * (1.0 + th) + 0.5 * x * (1.0 - th * th) * (_GELU_K * (1.0 + 3.0 * _GELU_C * x * x))


def _ln_stats(v):
    mu = jnp.mean(v, axis=-1, keepdims=True)
    vc = v - mu
    rstd = lax.rsqrt(jnp.mean(vc * vc, axis=-1, keepdims=True) + EPS)
    return vc * rstd, rstd


def _gmlp(p, ln_g, ln_b, w_s, b_s, name):
    t = p.shape[0]
    te = _row_tile(t)
    w = SG_WIDTH
    cw = w // SG_GROUPS

    def body(u_ref, v_ref, g_ref, b_ref, ws_ref, bs_ref, o_ref):
        xhat, _ = _ln_stats(_gelu(v_ref[...].astype(F32)))
        vn = (xhat * g_ref[...] + b_ref[...]).astype(MXU)
        ug = _gelu(u_ref[...].astype(F32))
        for ci in range(te // SG_CHUNK):
            rs = slice(ci * SG_CHUNK, (ci + 1) * SG_CHUNK)
            for gi in range(SG_GROUPS):
                cs = slice(gi * cw, (gi + 1) * cw)
                sg = _nn(ws_ref[gi].astype(MXU), vn[rs, cs]) + bs_ref[gi]
                o_ref[rs, cs] = (ug[rs, cs] * sg).astype(o_ref.dtype)

    return _call(body, name=name, grid=(t // te,),
                 in_specs=[_rows(te, w, 3), _rows(te, w, 4), _fixed((1, w)), _fixed((1, w)),
                           _fixed((SG_GROUPS, SG_CHUNK, SG_CHUNK)), _fixed((SG_GROUPS, SG_CHUNK, 1))],
                 out_specs=_rows(te, w), out_shape=_sds((t, w), MXU))(p, p, ln_g, ln_b, w_s, b_s)


def _gmlp_bwd(p, dob, ln_g, ln_b, w_s, b_s, name):
    t = p.shape[0]
    te = _row_tile(t)
    w = SG_WIDTH
    cw = w // SG_GROUPS

    def body(u_ref, v_ref, do_ref, g_ref, b_ref, ws_ref, bs_ref, du_ref, dv_ref, dws_ref, dbs_ref, dg_ref, db_ref, dvn_ref):
        @pl.when(pl.program_id(0) == 0)
        def _():
            dws_ref[...] = jnp.zeros_like(dws_ref)
            dbs_ref[...] = jnp.zeros_like(dbs_ref)
            dg_ref[...] = jnp.zeros_like(dg_ref)
            db_ref[...] = jnp.zeros_like(db_ref)

        u, v = u_ref[...].astype(F32), v_ref[...].astype(F32)
        xhat, rstd = _ln_stats(_gelu(v))
        vn = (xhat * g_ref[...] + b_ref[...]).astype(MXU)
        ug = _gelu(u)
        dob = do_ref[...].astype(F32)
        for ci in range(te // SG_CHUNK):
            rs = slice(ci * SG_CHUNK, (ci + 1) * SG_CHUNK)
            for gi in range(SG_GROUPS):
                cs = slice(gi * cw, (gi + 1) * cw)
                wsg = ws_ref[gi].astype(MXU)
                sg = _nn(wsg, vn[rs, cs]) + bs_ref[gi]
                du_ref[rs, cs] = (dob[rs, cs] * sg * _gelu_grad(u[rs, cs])).astype(du_ref.dtype)
                ds = dob[rs, cs] * ug[rs, cs]
                dbs_ref[gi] += jnp.sum(ds, axis=-1, keepdims=True)
                ds = ds.astype(MXU)
                dws_ref[gi] += _nt(ds, vn[rs, cs])
                dvn_ref[rs, cs] = _tn(wsg, ds)
        dvn = dvn_ref[...]
        dg_ref[...] += jnp.sum(dvn * xhat, axis=0, keepdims=True)
        db_ref[...] += jnp.sum(dvn, axis=0, keepdims=True)
        dxh = dvn * g_ref[...]
        dvg = rstd * (dxh - jnp.mean(dxh, axis=-1, keepdims=True) - xhat * jnp.mean(dxh * xhat, axis=-1, keepdims=True))
        dv_ref[...] = (dvg * _gelu_grad(v)).astype(dv_ref.dtype)

    return _call(body, name=name, grid=(t // te,),
                 in_specs=[_rows(te, w, 3), _rows(te, w, 4), _rows(te, w), _fixed((1, w)), _fixed((1, w)),
                           _fixed((SG_GROUPS, SG_CHUNK, SG_CHUNK)), _fixed((SG_GROUPS, SG_CHUNK, 1))],
                 out_specs=[_rows(te, w), _rows(te, w), _fixed((SG_GROUPS, SG_CHUNK, SG_CHUNK)),
                            _fixed((SG_GROUPS, SG_CHUNK, 1)), _fixed((1, w)), _fixed((1, w))],
                 out_shape=[_sds((t, w), MXU), _sds((t, w), MXU), _sds((SG_GROUPS, SG_CHUNK, SG_CHUNK), F32),
                            _sds((SG_GROUPS, SG_CHUNK, 1), F32), _sds((1, w), F32), _sds((1, w), F32)],
                 scratch=[pltpu.VMEM((te, w), F32)])(p, p, dob, ln_g, ln_b, w_s, b_s)


def _merge(pa, pb, p, b_gate, name):
    t, d = pa.shape
    te = _row_tile(t)
    hw = NA_WIDTH
    nh = d // hw
    c0 = (NA_WIDTH * 3 + SG_WIDTH * 2) // hw

    def body(pa_ref, pb_ref, la_ref, lb_ref, ba_ref, bb_ref, o_ref):
        ga = jax.nn.sigmoid(la_ref[...].astype(F32) + ba_ref[...])
        gb = jax.nn.sigmoid(lb_ref[...].astype(F32) + bb_ref[...])
        o_ref[...] = (ga * pa_ref[...].astype(F32) + gb * pb_ref[...].astype(F32)).astype(o_ref.dtype)

    tile = pl.BlockSpec((te, hw), lambda i, j: (i, j))
    return _call(body, name=name, grid=(t // te, nh),
                 in_specs=[tile, tile, pl.BlockSpec((te, hw), lambda i, j: (i, c0 + j)),
                           pl.BlockSpec((te, hw), lambda i, j: (i, c0 + nh + j)),
                           pl.BlockSpec((1, hw), lambda i, j: (0, j)), pl.BlockSpec((1, hw), lambda i, j: (0, nh + j))],
                 out_specs=tile, out_shape=_sds((t, d), MXU))(pa, pb, p, p, b_gate, b_gate)


def _merge_bwd(dmg, pa, pb, p, b_gate, name):
    t, d = pa.shape
    te = _row_tile(t)
    hw = NA_WIDTH
    nh = d // hw
    c0 = (NA_WIDTH * 3 + SG_WIDTH * 2) // hw

    def body(dm_ref, pa_ref, pb_ref, la_ref, lb_ref, ba_ref, bb_ref, dpa_ref, dpb_ref, dla_ref, dlb_ref, dba_ref, dbb_ref):
        @pl.when(pl.program_id(1) == 0)
        def _():
            dba_ref[...] = jnp.zeros_like(dba_ref)
            dbb_ref[...] = jnp.zeros_like(dbb_ref)

        dm = dm_ref[...].astype(F32)
        ga = jax.nn.sigmoid(la_ref[...].astype(F32) + ba_ref[...])
        gb = jax.nn.sigmoid(lb_ref[...].astype(F32) + bb_ref[...])
        dpa_ref[...] = (dm * ga).astype(dpa_ref.dtype)
        dpb_ref[...] = (dm * gb).astype(dpb_ref.dtype)
        dla = dm * pa_ref[...].astype(F32) * ga * (1.0 - ga)
        dlb = dm * pb_ref[...].astype(F32) * gb * (1.0 - gb)
        dla_ref[...] = dla.astype(dla_ref.dtype)
        dlb_ref[...] = dlb.astype(dlb_ref.dtype)
        dba_ref[...] += jnp.sum(dla, axis=0, keepdims=True)
        dbb_ref[...] += jnp.sum(dlb, axis=0, keepdims=True)

    tile = pl.BlockSpec((te, hw), lambda j, i: (i, j))
    bias_a = pl.BlockSpec((1, hw), lambda j, i: (0, j))
    bias_b = pl.BlockSpec((1, hw), lambda j, i: (0, nh + j))
    return _call(body, name=name, grid=(nh, t // te),
                 in_specs=[tile, tile, tile, pl.BlockSpec((te, hw), lambda j, i: (i, c0 + j)),
                           pl.BlockSpec((te, hw), lambda j, i: (i, c0 + nh + j)), bias_a, bias_b],
                 out_specs=[tile, tile, tile, tile, bias_a, bias_a],
                 out_shape=[_sds((t, d), MXU)] * 4 + [_sds((1, d), F32)] * 2)(dmg, pa, pb, p, p, b_gate, b_gate)


def _final(xs, tgt, g, name):
    t, d = xs.shape
    nx = tgt.shape[0] // TM

    def body(x_ref, t_ref, g_ref, l_ref, dx_ref, dg_ref):
        i = pl.program_id(0)

        @pl.when(i == 0)
        def _():
            l_ref[...] = jnp.zeros_like(l_ref)
            dg_ref[...] = jnp.zeros_like(dg_ref)

        @pl.when(i < nx)
        def _():
            x = x_ref[...]
            rstd = lax.rsqrt(jnp.mean(x * x, axis=-1, keepdims=True) + EPS)
            xhat = x * rstd
            err = xhat * g_ref[...] - t_ref[...]
            l_ref[...] += 0.5 * jnp.sum(jnp.mean(err * err, axis=-1, keepdims=True))
            dy = err * (1.0 / d)
            dg_ref[...] += jnp.sum(dy * xhat, axis=0, keepdims=True)
            dxh = dy * g_ref[...]
            dx_ref[...] = rstd * (dxh - xhat * jnp.mean(dxh * xhat, axis=-1, keepdims=True))

        @pl.when(i >= nx)
        def _():
            dx_ref[...] = jnp.zeros_like(dx_ref)

    return _call(body, name=name, grid=(t // TM,),
                 in_specs=[_rows(TM, d), pl.BlockSpec((TM, d), lambda i: (jnp.minimum(i, nx - 1), 0)), _fixed((1, d))],
                 out_specs=[_fixed((1, LANES)), _rows(TM, d), _fixed((1, d))],
                 out_shape=[_sds((1, LANES), F32), _sds((t, d), F32), _sds((1, d), F32)])(xs, tgt, g)


def _view2d(a):
    return a.reshape(1, -1) if a.ndim == 1 else a.reshape(-1, a.shape[-1])


def _ew(fn, arrays, out_dtypes, name):
    shape = arrays[0].shape
    views = [_view2d(a) for a in arrays]
    r, c = views[0].shape
    tr = r
    for cand in (1024, 512, 256, 128, 64, 32, 16):
        if r % cand == 0 and cand * c * 4 <= 2 ** 20:
            tr = cand
            break

    def body(*refs):
        outs = fn(*[ref[...] for ref in refs[:len(views)]])
        for ref, o in zip(refs[len(views):], outs):
            ref[...] = o.astype(ref.dtype)

    res = _call(body, name=name, grid=(r // tr,), in_specs=[_rows(tr, c)] * len(views), out_specs=[_rows(tr, c)] * len(out_dtypes),
                out_shape=[_sds((r, c), dt) for dt in out_dtypes])(*views)
    return [o.reshape(shape) for o in res]


def _sum_pieces(pieces, name, out_dtypes=(F32,)):
    def fn(*vals):
        acc = vals[0].astype(F32)
        for v in vals[1:]:
            acc = acc + v.astype(F32)
        return (acc,) * len(out_dtypes)

    return _ew(fn, pieces, list(out_dtypes), name)


def _adamw(w, g_pieces, m, v, name):
    n_g = len(g_pieces)

    def fn(w_, *rest):
        g = rest[0]
        for piece in rest[1:n_g]:
            g = g + piece
        m_, v_ = rest[n_g], rest[n_g + 1]
        m2 = ADAM_B1 * m_ + (1.0 - ADAM_B1) * g
        v2 = ADAM_B2 * v_ + (1.0 - ADAM_B2) * (g * g)
        m_hat = m2 / (1.0 - ADAM_B1 ** ADAM_STEP)
        v_hat = v2 / (1.0 - ADAM_B2 ** ADAM_STEP)
        delta = -ADAM_LR * (m_hat / (jnp.sqrt(v_hat) + ADAM_EPS) + ADAM_WD * w_)
        return g, delta, m2, v2

    return _ew(fn, [w, *g_pieces, m, v], [F32] * 4, name)


def _ada_fwd(cond, w, b, name):
    r, d = cond.shape
    n = w.shape[1]
    tn = _pick(n, (1152, 768, 512, 384, 256, 128))

    def body(c_ref, w_ref, b_ref, o_ref, s_ref):
        c = c_ref[...]
        sc = c * jax.nn.sigmoid(c)
        s_ref[...] = sc
        o_ref[...] = _nn(sc.astype(MXU), w_ref[...].astype(MXU)) + b_ref[...]

    return _call(body, name=name, grid=(n // tn,),
                 in_specs=[_fixed((r, d)), pl.BlockSpec((d, tn), lambda j: (0, j)), pl.BlockSpec((1, tn), lambda j: (0, j))],
                 out_specs=[pl.BlockSpec((r, tn), lambda j: (0, j)), _fixed((r, d))],
                 out_shape=[_sds((r, n), F32), _sds((r, d), F32)])(cond, w, b)


def _cctx_grad(parts, c_ctx, name):
    n, d = parts.shape

    def body(p_ref, c_ref, o_ref):
        c = c_ref[...]
        sg = jax.nn.sigmoid(c)
        acc = p_ref[0:1, :]
        for j in range(1, n):
            acc = acc + p_ref[j:j + 1, :]
        o_ref[...] = acc * (sg * (1.0 + c * (1.0 - sg)))

    return _call(body, name=name, grid=(1,), in_specs=[_fixed((n, d)), _fixed((1, d))], out_specs=_fixed((1, d)),
                 out_shape=_sds((1, d), F32))(parts, c_ctx)


def _here():
    return lax.axis_index("x"), lax.axis_index("y"), lax.axis_index("c")


def _flip(v, bit):
    return 1 - v if bit else v


def _allgather8(xb, name):
    r, n = xb.shape

    def body(x_ref, out_ref, send_sems, recv_sems, local_sem):
        x, y, c = _here()
        me = 4 * x + 2 * y + c
        local = pltpu.make_async_copy(x_ref, out_ref.at[me], local_sem)
        local.start()
        sends = []
        for k in range(1, 8):
            peer = (_flip(x, k & 4), _flip(y, k & 2), _flip(c, k & 1))
            cp = pltpu.make_async_remote_copy(src_ref=x_ref, dst_ref=out_ref.at[me], send_sem=send_sems.at[k - 1],
                                              recv_sem=recv_sems.at[k - 1], device_id=peer, device_id_type=MESH)
            cp.start()
            sends.append(cp)
        for k in range(1, 8):
            peer = (_flip(x, k & 4), _flip(y, k & 2), _flip(c, k & 1))
            src = 4 * peer[0] + 2 * peer[1] + peer[2]
            pltpu.make_async_remote_copy(src_ref=x_ref, dst_ref=out_ref.at[src], send_sem=send_sems.at[k - 1],
                                         recv_sem=recv_sems.at[k - 1], device_id=peer, device_id_type=MESH).wait_recv()
        for cp in sends:
            cp.wait_send()
        local.wait()

    vmem = pl.BlockSpec(memory_space=pltpu.VMEM)
    return pl.pallas_call(
        body, name=name, out_shape=_sds((8, r, n), xb.dtype), in_specs=[vmem], out_specs=vmem,
        scratch_shapes=[pltpu.SemaphoreType.DMA((7,)), pltpu.SemaphoreType.DMA((7,)), pltpu.SemaphoreType.DMA(())],
        compiler_params=pltpu.CompilerParams(vmem_limit_bytes=VMEM_LIMIT))(xb)


def _shard_of(ref, axis, j, size):
    sl = pl.ds(j * size, size)
    return ref.at[:, sl, :] if axis == 1 else ref.at[:, :, sl]


def _piece(ref, axis, j, size, layer):
    lay, sl = pl.ds(layer, 1), pl.ds(j * size, size)
    return ref.at[lay, sl, :] if axis == 1 else ref.at[lay, :, sl]


def _gather_chips(shards, axes, name):
    n = len(shards)
    fulls = []
    for a, ax in zip(shards, axes):
        assert a.shape[0] == 2
        shp = list(a.shape)
        shp[ax] *= 4
        fulls.append(_sds(tuple(shp), a.dtype))

    def body(*refs):
        ins, outs = refs[:n], refs[n:2 * n]
        ici_send, ici_recv, d2d_send, d2d_recv, local_sems = refs[2 * n:]
        x, y, c = _here()
        chips = [(_flip(x, k & 2), _flip(y, k & 1)) for k in range(1, 4)]
        local, sends = [], []
        for a in range(n):
            size = ins[a].shape[axes[a]]
            cp = pltpu.make_async_copy(ins[a], _shard_of(outs[a], axes[a], 2 * x + y, size), local_sems.at[a])
            cp.start()
            local.append(cp)
            for j, (px, py) in enumerate(chips):
                cp = pltpu.make_async_remote_copy(src_ref=ins[a].at[pl.ds(c, 1)], dst_ref=_piece(outs[a], axes[a], 2 * x + y, size, c),
                                                  send_sem=ici_send.at[3 * a + j], recv_sem=ici_recv.at[3 * a + j],
                                                  device_id=(px, py, c), device_id_type=MESH)
                cp.start()
                sends.append(cp)
        for a in range(n):
            size = ins[a].shape[axes[a]]
            for j, (px, py) in enumerate(chips):
                landed = _piece(outs[a], axes[a], 2 * px + py, size, c)
                pltpu.make_async_remote_copy(src_ref=ins[a].at[pl.ds(c, 1)], dst_ref=landed, send_sem=ici_send.at[3 * a + j],
                                             recv_sem=ici_recv.at[3 * a + j], device_id=(px, py, c), device_id_type=MESH).wait_recv()
                cp = pltpu.make_async_remote_copy(src_ref=landed, dst_ref=landed, send_sem=d2d_send.at[3 * a + j],
                                                  recv_sem=d2d_recv.at[3 * a + j], device_id=(x, y, 1 - c), device_id_type=MESH)
                cp.start()
                sends.append(cp)
        for a in range(n):
            size = ins[a].shape[axes[a]]
            for j, (px, py) in enumerate(chips):
                passed = _piece(outs[a], axes[a], 2 * px + py, size, 1 - c)
                pltpu.make_async_remote_copy(src_ref=passed, dst_ref=passed, send_sem=d2d_send.at[3 * a + j],
                                             recv_sem=d2d_recv.at[3 * a + j], device_id=(x, y, 1 - c), device_id_type=MESH).wait_recv()
        for cp in sends:
            cp.wait_send()
        for cp in local:
            cp.wait()

    hbm = pl.BlockSpec(memory_space=pl.ANY)
    return pl.pallas_call(
        body, name=name, out_shape=fulls, in_specs=[hbm] * n, out_specs=[hbm] * n,
        scratch_shapes=[pltpu.SemaphoreType.DMA((3 * n,))] * 4 + [pltpu.SemaphoreType.DMA((n,))])(*shards)


def _swap_layers(arrays, name):
    n = len(arrays)

    def body(*refs):
        ins, outs = refs[:n], refs[n:2 * n]
        send_sems, recv_sems = refs[2 * n:]
        x, y, c = _here()
        copies = []
        for a in range(n):
            cp = pltpu.make_async_remote_copy(src_ref=ins[a].at[pl.ds(1 - c, 1)], dst_ref=outs[a], send_sem=send_sems.at[a],
                                              recv_sem=recv_sems.at[a], device_id=(x, y, 1 - c), device_id_type=MESH)
            cp.start()
            copies.append(cp)
        for cp in copies:
            cp.wait()

    hbm = pl.BlockSpec(memory_space=pl.ANY)
    return pl.pallas_call(
        body, name=name, out_shape=[_sds((1, *a.shape[1:]), a.dtype) for a in arrays], in_specs=[hbm] * n, out_specs=[hbm] * n,
        scratch_shapes=[pltpu.SemaphoreType.DMA((n,)), pltpu.SemaphoreType.DMA((n,))])(*arrays)


def _scatter_chips(fulls, axes, name):
    n = len(fulls)
    recvs = []
    for a, ax in zip(fulls, axes):
        shp = list(a.shape)
        shp[ax] //= 4
        recvs.append(_sds((3, *shp), a.dtype))

    def body(*refs):
        ins, outs = refs[:n], refs[n:2 * n]
        send_sems, recv_sems = refs[2 * n:]
        x, y, c = _here()
        sends = []
        for a in range(n):
            size = ins[a].shape[axes[a]] // 4
            for k in range(1, 4):
                peer = (_flip(x, k & 2), _flip(y, k & 1), c)
                cp = pltpu.make_async_remote_copy(src_ref=_shard_of(ins[a], axes[a], 2 * peer[0] + peer[1], size),
                                                  dst_ref=outs[a].at[k - 1],
                                                  send_sem=send_sems.at[3 * a + k - 1], recv_sem=recv_sems.at[3 * a + k - 1],
                                                  device_id=peer, device_id_type=MESH)
                cp.start()
                sends.append(cp)
        for cp in sends:
            cp.wait_recv()
        for cp in sends:
            cp.wait_send()

    hbm = pl.BlockSpec(memory_space=pl.ANY)
    return pl.pallas_call(
        body, name=name, out_shape=recvs, in_specs=[hbm] * n, out_specs=[hbm] * n,
        scratch_shapes=[pltpu.SemaphoreType.DMA((3 * n,)), pltpu.SemaphoreType.DMA((3 * n,))])(*fulls)


def _sibling_swap(arrays, name):
    n = len(arrays)

    def body(*refs):
        ins, outs = refs[:n], refs[n:2 * n]
        send_sems, recv_sems = refs[2 * n:]
        x, y, c = _here()
        copies = []
        for a in range(n):
            cp = pltpu.make_async_remote_copy(src_ref=ins[a], dst_ref=outs[a], send_sem=send_sems.at[a], recv_sem=recv_sems.at[a],
                                              device_id=(x, y, 1 - c), device_id_type=MESH)
            cp.start()
            copies.append(cp)
        for cp in copies:
            cp.wait()

    hbm = pl.BlockSpec(memory_space=pl.ANY)
    return pl.pallas_call(
        body, name=name, out_shape=[_sds(a.shape, a.dtype) for a in arrays], in_specs=[hbm] * n, out_specs=[hbm] * n,
        scratch_shapes=[pltpu.SemaphoreType.DMA((n,)), pltpu.SemaphoreType.DMA((n,))])(*arrays)


def _ffn_fwd(xs, g, mods, k0, w_up, w_down, s, tag):
    h, ua, ub, act = _norm_mm(xs, g, mods, k0, k0 + 1, w_up, s, tag + "_up", True)
    y, xn = _mm_res(act, w_down, xs, mods, k0 + 2, 0.5, s, tag + "_down")
    return xn, (xs, h, ua, ub, act, y)


def _ffn_bwd(dxn, saved, g, mods, k0, w_up, w_down, s, tag):
    xs, h, ua, ub, act, y = saved
    dy, dgate, dua, dub = _resb_mm(dxn, y, mods, k0 + 2, 0.5, w_down, s, tag + "_down_dx", (ua, ub))
    d_down = _mm(act, dy, "tn", F32, tag + "_down_dw")
    d_up = jnp.concatenate([_mm(h, dua, "tn", F32, tag + "_upa_dw"), _mm(h, dub, "tn", F32, tag + "_upb_dw")], axis=1)
    dx, dsh, dsc, dg = _mm_normb([dua, dub], w_up, xs, dxn, g, mods, k0 + 1, s, tag + "_up_dx")
    return dx, d_up, d_down, dg, [dsh, dsc, dgate]


def _mix_fwd(xs, g, mods, wl, pl_, tabs, s, tag):
    cos, sin = tabs
    h, p = _norm_mm(xs, g, mods, 3, 4, wl["w_in"], s, tag + "_in", False)
    q, k, v = _rope(p, cos, sin, tag + "_rope")
    bias = _bias_table(pl_["rpb"], s // GRID_W)
    oa = _na_fwd(q, k, v, bias, s, tag + "_na")
    ob = _gmlp(p, pl_["ln_v_g"], pl_["ln_v_b"], pl_["w_s"], pl_["b_s"], tag + "_sg")
    pa = _mm(oa, wl["w_pa"], "nn", MXU, tag + "_pa")
    pb = _mm(ob, wl["w_pb"], "nn", MXU, tag + "_pb")
    mg = _merge(pa, pb, p, pl_["b_gate"], tag + "_merge")
    y, xn = _mm_res(mg, wl["w_o"], xs, mods, 5, 1.0, s, tag + "_o")
    return xn, (xs, h, p, q, k, v, bias, oa, ob, pa, pb, mg, y)


def _mix_bwd(dxn, saved, g, mods, wl, pl_, tabs, s, tag):
    xs, h, p, q, k, v, bias, oa, ob, pa, pb, mg, y = saved
    cos, sin = tabs
    gw, gp = {}, {}
    dy, dgate, dmg = _resb_mm(dxn, y, mods, 5, 1.0, wl["w_o"], s, tag + "_o_dx")
    gw["w_o"] = _mm(mg, dy, "tn", F32, tag + "_o_dw")
    dpa, dpb, dla, dlb, dba, dbb = _merge_bwd(dmg, pa, pb, p, pl_["b_gate"], tag + "_merge_b")
    gp["b_gate"] = jnp.concatenate([dba, dbb], axis=1)
    gw["w_pa"] = _mm(oa, dpa, "tn", F32, tag + "_pa_dw")
    doa = _mm(dpa, wl["w_pa"], "nt", MXU, tag + "_pa_dx")
    gw["w_pb"] = _mm(ob, dpb, "tn", F32, tag + "_pb_dw")
    dob = _mm(dpb, wl["w_pb"], "nt", MXU, tag + "_pb_dx")
    du, dvs, gp["w_s"], gp["b_s"], gp["ln_v_g"], gp["ln_v_b"] = _gmlp_bwd(
        p, dob, pl_["ln_v_g"], pl_["ln_v_b"], pl_["w_s"], pl_["b_s"], tag + "_sg_b")
    dqr, dkr, dv, dbias = _na_bwd(q, k, v, doa, bias, s, tag + "_na_b")
    gp["rpb"] = _rpb_grad(dbias, s // GRID_W, tag + "_rpb")
    dq, dk, dvv = _rope_bwd(dqr, dkr, dv, cos, sin, tag + "_rope_b")
    dp = jnp.concatenate([dq, dk, dvv, du, dvs, dla, dlb], axis=1)
    gw["w_in"] = _mm(h, dp, "tn", F32, tag + "_in_dw")
    dx, dsh, dsc, dg = _mm_normb([dp], wl["w_in"], xs, dxn, g, mods, 4, s, tag + "_in_dx")
    return dx, gw, gp, dg, [dsh, dsc, dgate]


def _local_step(x, ctx, tgt, mods, wts, prm):
    s, d = x.shape
    depth = mods.shape[0]
    tabs = _rope_tables(s, ctx.shape[0])
    xs = jnp.concatenate([x, ctx], axis=0)
    saved = []
    for l in range(depth):
        wl = {k: v[l] for k, v in wts.items()}
        pl_ = _layer_params(prm, l)
        xs, s1 = _ffn_fwd(xs, pl_["g"][0], mods[l], 0, wl["w_ff1_up"], wl["w_ff1_down"], s, f"l{l}_ff1")
        xs, s2 = _mix_fwd(xs, pl_["g"][1], mods[l], wl, pl_, tabs, s, f"l{l}_mix")
        xs, s3 = _ffn_fwd(xs, pl_["g"][2], mods[l], 6, wl["w_ff2_up"], wl["w_ff2_down"], s, f"l{l}_ff2")
        saved.append((s1, s2, s3))
    loss, dxs, d_final_g = _final(xs, tgt, prm["final_g"].reshape(1, d), "final")
    gw = {k: [None] * depth for k in wts}
    gp = {k: [None] * depth for k in ("norm_g", "b_gate", "rpb", "ln_v_g", "ln_v_b", "w_s", "b_s")}
    dmods = [None] * depth
    for l in reversed(range(depth)):
        wl = {k: v[l] for k, v in wts.items()}
        pl_ = _layer_params(prm, l)
        s1, s2, s3 = saved[l]
        dxs, gw["w_ff2_up"][l], gw["w_ff2_down"][l], dg2, dm2 = _ffn_bwd(
            dxs, s3, pl_["g"][2], mods[l], 6, wl["w_ff2_up"], wl["w_ff2_down"], s, f"l{l}_ff2")
        dxs, gwm, gpm, dg1, dm1 = _mix_bwd(dxs, s2, pl_["g"][1], mods[l], wl, pl_, tabs, s, f"l{l}_mix")
        dxs, gw["w_ff1_up"][l], gw["w_ff1_down"][l], dg0, dm0 = _ffn_bwd(
            dxs, s1, pl_["g"][0], mods[l], 0, wl["w_ff1_up"], wl["w_ff1_down"], s, f"l{l}_ff1")
        for k, v in gwm.items():
            gw[k][l] = v
        gp["b_gate"][l] = gpm["b_gate"][0]
        gp["rpb"][l] = gpm["rpb"]
        gp["ln_v_g"][l] = gpm["ln_v_g"][0]
        gp["ln_v_b"][l] = gpm["ln_v_b"][0]
        gp["w_s"][l] = gpm["w_s"]
        gp["b_s"][l] = gpm["b_s"][..., 0]
        gp["norm_g"][l] = jnp.concatenate([dg0, dg1, dg2], axis=0)
        dmods[l] = jnp.concatenate(dm0 + dm1 + dm2, axis=1)
    gw = {k: jnp.stack(v) for k, v in gw.items()}
    gp = {k: jnp.stack(v) for k, v in gp.items()}
    gp["final_g"] = d_final_g[0]
    return loss[0, 0], dxs[:s], jnp.stack(dmods), gw, gp


def _layer_params(prm, l):
    d = prm["norm_g"].shape[-1]
    return {
        "g": [prm["norm_g"][l, i].reshape(1, d) for i in range(3)],
        "b_gate": prm["b_gate"][l].reshape(1, -1),
        "rpb": prm["rpb"][l],
        "ln_v_g": prm["ln_v_g"][l].reshape(1, -1),
        "ln_v_b": prm["ln_v_b"][l].reshape(1, -1),
        "w_s": prm["w_s"][l],
        "b_s": prm["b_s"][l][..., None],
    }


SMALL = ("norm_g", "b_gate", "rpb", "ln_v_g", "ln_v_b", "w_s", "b_s", "final_g")
PACK_LANES = 1024


def _pack(parts):
    flat = jnp.concatenate([p.reshape(-1) for p in parts])
    rows = -(-flat.shape[0] // PACK_LANES)
    rows = -(-rows // 8) * 8
    return jnp.pad(flat, (0, rows * PACK_LANES - flat.shape[0])).reshape(rows, PACK_LANES)


def _unpack(flat, shapes):
    out, off = [], 0
    for shp in shapes:
        n = int(np.prod(shp))
        out.append(flat[..., off:off + n].reshape(*flat.shape[:-1], *shp))
        off += n
    return out


def kernel(x, c, ctx, c_ctx, w_ada, b_ada, norm_g, w_ff1_up, w_ff1_down, w_in, b_gate, rpb, ln_v_g, ln_v_b, w_s, b_s, w_pa, w_pb, w_o, w_ff2_up, w_ff2_down, final_g, loss_target, m_c_ctx, m_w_ada, m_b_ada, m_norm_g, m_w_ff1_up, m_w_ff1_down, m_w_in, m_b_gate, m_rpb, m_ln_v_g, m_ln_v_b, m_w_s, m_b_s, m_w_pa, m_w_pb, m_w_o, m_w_ff2_up, m_w_ff2_down, m_final_g, v_c_ctx, v_w_ada, v_b_ada, v_norm_g, v_w_ff1_up, v_w_ff1_down, v_w_in, v_b_gate, v_rpb, v_ln_v_g, v_ln_v_b, v_w_s, v_b_s, v_w_pa, v_w_pb, v_w_o, v_w_ff2_up, v_w_ff2_down, v_final_g):
    weights = dict(c_ctx=c_ctx, w_ada=w_ada, b_ada=b_ada, norm_g=norm_g, w_ff1_up=w_ff1_up, w_ff1_down=w_ff1_down, w_in=w_in,
                   b_gate=b_gate, rpb=rpb, ln_v_g=ln_v_g, ln_v_b=ln_v_b, w_s=w_s, b_s=b_s, w_pa=w_pa, w_pb=w_pb, w_o=w_o,
                   w_ff2_up=w_ff2_up, w_ff2_down=w_ff2_down, final_g=final_g)
    mom_m = dict(c_ctx=m_c_ctx, w_ada=m_w_ada, b_ada=m_b_ada, norm_g=m_norm_g, w_ff1_up=m_w_ff1_up, w_ff1_down=m_w_ff1_down,
                 w_in=m_w_in, b_gate=m_b_gate, rpb=m_rpb, ln_v_g=m_ln_v_g, ln_v_b=m_ln_v_b, w_s=m_w_s, b_s=m_b_s, w_pa=m_w_pa,
                 w_pb=m_w_pb, w_o=m_w_o, w_ff2_up=m_w_ff2_up, w_ff2_down=m_w_ff2_down, final_g=m_final_g)
    mom_v = dict(c_ctx=v_c_ctx, w_ada=v_w_ada, b_ada=v_b_ada, norm_g=v_norm_g, w_ff1_up=v_w_ff1_up, w_ff1_down=v_w_ff1_down,
                 w_in=v_w_in, b_gate=v_b_gate, rpb=v_rpb, ln_v_g=v_ln_v_g, ln_v_b=v_ln_v_b, w_s=v_w_s, b_s=v_b_s, w_pa=v_w_pa,
                 w_pb=v_w_pb, w_o=v_w_o, w_ff2_up=v_w_ff2_up, w_ff2_down=v_w_ff2_down, final_g=v_final_g)
    order = list(weights)
    mx, my, mc = _here()
    dev = 4 * mx + 2 * my + mc
    chip = 2 * mx + my
    depth, d, n_ada = w_ada.shape
    dq = d // 4

    c_all = _allgather8(jnp.pad(c, ((0, 7), (0, 0))), "gather_c")[:, 0, :]
    cond = jnp.concatenate([c_all, c_ctx[None, :], jnp.zeros((7, d), F32)], axis=0)
    b_shard = lax.dynamic_slice(b_ada, (0, chip * n_ada), (depth, n_ada))
    proj = [_ada_fwd(cond, w_ada[l], b_shard[l:l + 1], f"ada{l}") for l in range(depth)]
    silu_c = proj[0][1]
    mods_sh = _allgather8(jnp.concatenate([p[0] for p in proj], axis=0), "gather_mods")
    mods_all = jnp.transpose(mods_sh[0::2].reshape(4, depth, 16, n_ada), (1, 2, 0, 3)).reshape(depth, 16, N_MOD, d)
    mods = jnp.stack([lax.dynamic_index_in_dim(mods_all, dev, axis=1, keepdims=False), mods_all[:, 8]], axis=1)

    full = _gather_chips([weights[k].astype(MXU) for k in BIG], [SHARD_AXIS[k] for k in BIG], "gather_w")
    wts = dict(zip(BIG, full))
    prm = {k: weights[k] for k in SMALL if k != "norm_g"}
    norm_full = _allgather8(jnp.pad(norm_g.reshape(depth * 3, dq), ((0, 8 - depth * 3), (0, 0))), "gather_norm_g")
    prm["norm_g"] = jnp.transpose(norm_full[0::2, :depth * 3].reshape(4, depth, 3, dq), (1, 2, 0, 3)).reshape(depth, 3, d)

    loss, grad_x, dmods, gw, gp = _local_step(x[0], ctx[0], loss_target[0], mods, wts, prm)
    loss = lax.psum(loss, ("x", "y", "c"))

    small_shapes = [(depth, 2, N_MOD * d)] + [weights[k].shape if k != "norm_g" else (depth, 3, d) for k in SMALL]
    packed = _allgather8(_pack([dmods.reshape(depth, 2, N_MOD * d)] + [gp[k] for k in SMALL]), "gather_small")
    rows = packed.shape[1]
    total = _sum_pieces([packed[i] for i in range(8)], "sum_small")[0].reshape(-1)
    sums = dict(zip(("dmods",) + SMALL, _unpack(total, small_shapes)))
    dmods_dev = _unpack(packed.reshape(8, rows * PACK_LANES), small_shapes[:1])[0]

    g_ada, cc_parts = [], []
    for l in range(depth):
        dm = jnp.concatenate([dmods_dev[:, l, 0], sums["dmods"][l, 1][None], jnp.zeros((7, N_MOD * d), F32)], axis=0)
        dm_sh = lax.dynamic_slice(dm, (0, chip * n_ada), (16, n_ada))
        g_ada.append(_mm(silu_c, dm_sh, "tn", F32, f"ada{l}_dw"))
        cc_parts.append(_mm(dm_sh, w_ada[l], "nt", F32, f"ada{l}_dc")[8:9])
    cc_all = _allgather8(jnp.pad(jnp.concatenate(cc_parts, axis=0), ((0, 8 - depth), (0, 0))), "gather_cctx")
    g_cctx = _cctx_grad(cc_all[0::2, :depth].reshape(4 * depth, d), c_ctx.reshape(1, d), "cctx_grad")

    axes = [SHARD_AXIS[k] for k in BIG]
    from_sibling = _swap_layers([gw[k] for k in BIG], "swap_layer_gw")
    pair = [_sum_pieces([lax.dynamic_slice_in_dim(gw[k], mc, 1, axis=0), r], "pair_" + k, (F32, MXU)) for k, r in zip(BIG, from_sibling)]
    recv = _scatter_chips([p[1] for p in pair], axes, "scatter_gw")
    mine = []
    for k, ax, p, r in zip(BIG, axes, pair, recv):
        size = p[0].shape[ax] // 4
        own = lax.dynamic_slice_in_dim(p[0], chip * size, size, axis=ax)
        mine.append(_sum_pieces([own, r[0], r[1], r[2]], "sum_" + k)[0])
    other = _sibling_swap(mine, "swap_gw")

    pieces = {k: [jnp.concatenate([jnp.where(mc == 0, a, b), jnp.where(mc == 0, b, a)], axis=0)] for k, a, b in zip(BIG, mine, other)}
    pieces["w_ada"] = [jnp.stack(g_ada)]
    pieces["b_ada"] = [sums["dmods"][:, 0], sums["dmods"][:, 1]]
    pieces["c_ctx"] = [g_cctx[0]]
    for k in SMALL:
        pieces[k] = [sums[k]]
    pieces["norm_g"] = [lax.dynamic_slice_in_dim(sums["norm_g"], chip * dq, dq, axis=2)]
    res = {k: _adamw(weights[k], pieces[k], mom_m[k], mom_v[k], "adamw_" + k) for k in order}
    return (loss, grad_x[None], *[res[k][0] for k in order], *[res[k][1] for k in order],
            *[res[k][2] for k in order], *[res[k][3] for k in order])
```

```python
import numpy as np
import jax
import jax.numpy as jnp
from jax import lax
from jax.experimental import pallas as pl
from jax.experimental.pallas import tpu as pltpu

F32 = jnp.float32
MXU = jnp.bfloat16
EPS = 1e-6
GRID_W, HEADS, HEAD_DIM = 64, 8, 64
NA_WIDTH = SG_WIDTH = 512
WIN_H, WIN_W = 8, 16
SG_CHUNK, SG_GROUPS = 128, 4
N_MOD = 9
ROPE_THETA = 10000.0
Q_ROWS, K_ROWS = 4, 12
TQ, TK = Q_ROWS * GRID_W, K_ROWS * GRID_W
TM = 256
LANES = 128
NEG = -1e30
VMEM_LIMIT = 56 * 2 ** 20
ADAM_LR, ADAM_B1, ADAM_B2, ADAM_EPS, ADAM_WD, ADAM_STEP = 0.001, 0.9, 0.999, 1e-08, 0.01, 10
MESH = pl.DeviceIdType.MESH
BIG = ("w_ff1_up", "w_ff1_down", "w_in", "w_pa", "w_pb", "w_o", "w_ff2_up", "w_ff2_down")
SHARD_AXIS = {"w_ff1_up": 2, "w_ff1_down": 1, "w_in": 2, "w_pa": 2, "w_pb": 2, "w_o": 1, "w_ff2_up": 2, "w_ff2_down": 1}


def _call(body, *, name, grid, in_specs, out_specs, out_shape, scratch=(), aliases=None):
    return pl.pallas_call(
        body, name=name, grid=grid, in_specs=in_specs, out_specs=out_specs, out_shape=out_shape,
        scratch_shapes=list(scratch), input_output_aliases=aliases or {},
        compiler_params=pltpu.CompilerParams(dimension_semantics=("arbitrary",) * len(grid), vmem_limit_bytes=VMEM_LIMIT))


def _w_dims(w):
    return w[0].shape[1:] if isinstance(w, tuple) else w.shape


def _w_arr(w):
    return w[0] if isinstance(w, tuple) else w


def _w_spec(w, block, index):
    if isinstance(w, tuple):
        layer = w[1]
        return pl.BlockSpec((None, *block), lambda *ids: (layer, *index(*ids)))
    return pl.BlockSpec(block, index)


def _pick(n, prefs):
    for p in prefs:
        if n % p == 0:
            return p
    return n


def _row_tile(t):
    return _pick(t, (640, 256))


def _rows(tm, n, col=0):
    return pl.BlockSpec((tm, n), lambda i: (i, col))


def _fixed(shape):
    return pl.BlockSpec(shape, lambda *_: (0,) * len(shape))


def _sds(shape, dtype):
    return jax.ShapeDtypeStruct(shape, dtype)


def _mm(a, b, mode, out_dtype, name, into=None):
    if mode == "tn":
        r, m = a.shape
        n = b.shape[1]
        tm = _pick(m, (1024, 1408, 704, 512, 256, 128))
        tn = _pick(n, (512, 1408, 256, 128))
        tr = _pick(r, (1280, 640, 512, 256, 128))

        def body(a_ref, b_ref, *rest):
            o_ref = rest[-1]

            @pl.when(pl.program_id(2) == 0)
            def _():
                o_ref[...] = jnp.zeros_like(o_ref)

            o_ref[...] += lax.dot_general(a_ref[...].astype(MXU), b_ref[...].astype(MXU), (((0,), (0,)), ((), ())),
                                          preferred_element_type=F32)

        in_specs = [pl.BlockSpec((tr, tm), lambda i, j, k: (k, i)), pl.BlockSpec((tr, tn), lambda i, j, k: (k, j))]
        if into is None:
            return _call(body, name=name, grid=(m // tm, n // tn, r // tr), in_specs=in_specs,
                         out_specs=pl.BlockSpec((tm, tn), lambda i, j, k: (i, j)), out_shape=_sds((m, n), F32))(a, b)
        buf, shape, layer, col0 = into
        out_spec = pl.BlockSpec((None, tm, tn), lambda i, j, k: (layer, i, j + col0 // tn))
        if buf is None:
            return _call(body, name=name, grid=(m // tm, n // tn, r // tr), in_specs=in_specs, out_specs=out_spec,
                         out_shape=_sds(shape, F32))(a, b)
        return _call(body, name=name, grid=(m // tm, n // tn, r // tr), in_specs=in_specs + [pl.BlockSpec(memory_space=pl.ANY)],
                     out_specs=out_spec, out_shape=_sds(shape, F32), aliases={2: 0})(a, b, buf)
    m, k = a.shape
    n = _w_dims(b)[1] if mode == "nn" else _w_dims(b)[0]
    tm = _pick(m, (1280, 640, 512, 256, 128) if k <= 2816 else (640, 512, 256, 128))
    tn = _pick(n, (512, 1408, 256, 128))
    dims = (((1,), (0,)), ((), ())) if mode == "nn" else (((1,), (1,)), ((), ()))

    def body(a_ref, b_ref, o_ref):
        o_ref[...] = lax.dot_general(a_ref[...].astype(MXU), b_ref[...].astype(MXU), dims,
                                     preferred_element_type=F32).astype(o_ref.dtype)

    b_spec = _w_spec(b, (k, tn), lambda i, j: (0, j)) if mode == "nn" else _w_spec(b, (tn, k), lambda i, j: (j, 0))
    return _call(body, name=name, grid=(m // tm, n // tn), in_specs=[pl.BlockSpec((tm, k), lambda i, j: (i, 0)), b_spec],
                 out_specs=pl.BlockSpec((tm, tn), lambda i, j: (i, j)), out_shape=_sds((m, n), out_dtype))(a, _w_arr(b))


def _ctx_rows(i, tm, s):
    return (i * tm + lax.broadcasted_iota(jnp.int32, (tm, 1), 0)) >= s


def _mod_row(m_ref, k, ctx):
    return jnp.where(ctx, m_ref[1, k:k + 1, :], m_ref[0, k:k + 1, :])


def _stream_sums(i, tm, s, refs_and_vals):
    @pl.when((i + 1) * tm <= s)
    def _():
        for ref, val in refs_and_vals:
            ref[0] += jnp.sum(val, axis=0, keepdims=True)

    @pl.when((i + 1) * tm > s)
    def _():
        ctx = _ctx_rows(i, tm, s)
        for ref, val in refs_and_vals:
            ref[0] += jnp.sum(jnp.where(ctx, 0.0, val), axis=0, keepdims=True)
            ref[1] += jnp.sum(jnp.where(ctx, val, 0.0), axis=0, keepdims=True)


def _norm_mm(xs, g, mods, k_shift, k_scale, w, s, name, glu):
    t, d = xs.shape
    n = _w_dims(w)[1] // 2 if glu else _w_dims(w)[1]
    tm = _pick(t, (1280, 640, 256))
    tn = _pick(n, (256, 128)) if glu else _pick(n, (512, 256, 128))
    nj = n // tn

    def body(x_ref, g_ref, m_ref, *refs):
        i, j = pl.program_id(0), pl.program_id(1)
        w_refs, h_ref, o_refs = refs[:2 if glu else 1], refs[2 if glu else 1], refs[3 if glu else 2:]

        @pl.when(j == 0)
        def _():
            x = x_ref[...]
            rstd = lax.rsqrt(jnp.mean(x * x, axis=-1, keepdims=True) + EPS)
            ctx = _ctx_rows(i, tm, s)
            h = x * rstd * g_ref[...] * (1.0 + _mod_row(m_ref, k_scale, ctx)) + _mod_row(m_ref, k_shift, ctx)
            h_ref[...] = h.astype(h_ref.dtype)

        h = h_ref[...]
        a = _nn(h, w_refs[0][...])
        o_refs[0][...] = a.astype(o_refs[0].dtype)
        if glu:
            b = _nn(h, w_refs[1][...])
            o_refs[1][...] = b.astype(o_refs[1].dtype)
            o_refs[2][...] = (a * jax.nn.sigmoid(a) * b).astype(o_refs[2].dtype)

    tile = pl.BlockSpec((tm, tn), lambda i, j: (i, j))
    row = pl.BlockSpec((tm, d), lambda i, j: (i, 0))
    w_specs = [_w_spec(w, (d, tn), lambda i, j: (0, j))] + ([_w_spec(w, (d, tn), lambda i, j: (0, j + nj))] if glu else [])
    n_out = 3 if glu else 1
    return _call(body, name=name, grid=(t // tm, nj),
                 in_specs=[row, _fixed((1, d)), _fixed((2, N_MOD, d))] + w_specs,
                 out_specs=[row] + [tile] * n_out,
                 out_shape=[_sds((t, d), MXU)] + [_sds((t, n), MXU)] * n_out)(xs, g, mods, *([_w_arr(w)] * (2 if glu else 1)))


def _mm_res(a, w, xs, mods, k_gate, coef, s, name):
    t, k = a.shape
    d = _w_dims(w)[1]
    tm = _pick(t, (1280, 640, 256))
    tn = _pick(d, (512, 256, 128))

    def body(a_ref, w_ref, x_ref, m_ref, y_ref, o_ref):
        y = _nn(a_ref[...], w_ref[...])
        y_ref[...] = y.astype(y_ref.dtype)
        gate = _mod_row(m_ref, k_gate, _ctx_rows(pl.program_id(0), tm, s))
        o_ref[...] = x_ref[...] + (coef * gate) * y

    tile = pl.BlockSpec((tm, tn), lambda i, j: (i, j))
    return _call(body, name=name, grid=(t // tm, d // tn),
                 in_specs=[pl.BlockSpec((tm, k), lambda i, j: (i, 0)), _w_spec(w, (k, tn), lambda i, j: (0, j)), tile,
                           pl.BlockSpec((2, N_MOD, tn), lambda i, j: (0, 0, j))],
                 out_specs=[tile, tile], out_shape=[_sds((t, d), MXU), _sds((t, d), F32)])(a, _w_arr(w), xs, mods)


def _resb_mm(dxn, y, mods, k_gate, coef, w, s, name, ups=None):
    t, d = dxn.shape
    n = _w_dims(w)[0]
    tm = _pick(t, (1280, 640, 256))
    tn = _pick(n, (256, 128)) if ups else _pick(n, (512, 256, 128))

    def body(dx_ref, y_ref, m_ref, w_ref, *refs):
        i, j = pl.program_id(0), pl.program_id(1)
        u_refs, (dy_ref, dgt_ref), o_refs = (refs[:2], refs[2:4], refs[4:]) if ups else ((), refs[:2], refs[2:])

        @pl.when((i == 0) & (j == 0))
        def _():
            dgt_ref[...] = jnp.zeros_like(dgt_ref)

        @pl.when(j == 0)
        def _():
            dx = dx_ref[...]
            gate = _mod_row(m_ref, k_gate, _ctx_rows(i, tm, s))
            dy_ref[...] = ((coef * gate) * dx).astype(dy_ref.dtype)
            _stream_sums(i, tm, s, [(dgt_ref, coef * y_ref[...].astype(F32) * dx)])

        dact = _nt(dy_ref[...], w_ref[...])
        if ups:
            a, b = u_refs[0][...].astype(F32), u_refs[1][...].astype(F32)
            sg = jax.nn.sigmoid(a)
            o_refs[0][...] = (dact * b * sg * (1.0 + a * (1.0 - sg))).astype(o_refs[0].dtype)
            o_refs[1][...] = (dact * a * sg).astype(o_refs[1].dtype)
        else:
            o_refs[0][...] = dact.astype(o_refs[0].dtype)

    tile = pl.BlockSpec((tm, tn), lambda i, j: (i, j))
    row = pl.BlockSpec((tm, d), lambda i, j: (i, 0))
    n_out = 2 if ups else 1
    return _call(body, name=name, grid=(t // tm, n // tn),
                 in_specs=[row, row, _fixed((2, N_MOD, d)), _w_spec(w, (tn, d), lambda i, j: (j, 0))] + ([tile, tile] if ups else []),
                 out_specs=[row, _fixed((2, 1, d))] + [tile] * n_out,
                 out_shape=[_sds((t, d), MXU), _sds((2, 1, d), F32)] + [_sds((t, n), MXU)] * n_out)(
                     dxn, y, mods, _w_arr(w), *(ups or ()))


def _mm_normb(a_list, w, xs, dres, g, mods, k_scale, s, name):
    t, d = xs.shape
    ka = a_list[0].shape[1]
    tm = _pick(t, (640, 256))
    tk = _pick(ka, (1408, 1536, 1024, 512, 256, 128))
    nk1 = ka // tk
    n_a = len(a_list)
    nk = nk1 * n_a

    def body(*refs):
        a_refs, (w_ref, x_ref, dr_ref, g_ref, m_ref, dx_ref, dsh_ref, dsc_ref, dg_ref, acc) = refs[:n_a], refs[n_a:]
        i, k = pl.program_id(0), pl.program_id(1)

        @pl.when((i == 0) & (k == 0))
        def _():
            dsh_ref[...] = jnp.zeros_like(dsh_ref)
            dsc_ref[...] = jnp.zeros_like(dsc_ref)
            dg_ref[...] = jnp.zeros_like(dg_ref)

        @pl.when(k == 0)
        def _():
            acc[...] = jnp.zeros_like(acc)

        for q in range(n_a):
            @pl.when((k >= q * nk1) & (k < (q + 1) * nk1))
            def _():
                acc[...] += _nt(a_refs[q][...], w_ref[...])

        @pl.when(k == nk - 1)
        def _():
            x = x_ref[...]
            dh = acc[...]
            rstd = lax.rsqrt(jnp.mean(x * x, axis=-1, keepdims=True) + EPS)
            xhat = x * rstd
            gg = g_ref[...]
            _stream_sums(i, tm, s, [(dsh_ref, dh), (dsc_ref, dh * (xhat * gg))])
            dy = dh * (1.0 + _mod_row(m_ref, k_scale, _ctx_rows(i, tm, s)))
            dg_ref[...] += jnp.sum(dy * xhat, axis=0, keepdims=True)
            dxh = dy * gg
            dx_ref[...] = dr_ref[...] + rstd * (dxh - xhat * jnp.mean(dxh * xhat, axis=-1, keepdims=True))

    row = pl.BlockSpec((tm, d), lambda i, k: (i, 0))
    a_specs = [pl.BlockSpec((tm, tk), lambda i, k, q=q: (i, jnp.clip(k - q * nk1, 0, nk1 - 1))) for q in range(n_a)]
    return _call(body, name=name, grid=(t // tm, nk),
                 in_specs=a_specs + [_w_spec(w, (d, tk), lambda i, k: (0, k)), row, row, _fixed((1, d)), _fixed((2, N_MOD, d))],
                 out_specs=[row, _fixed((2, 1, d)), _fixed((2, 1, d)), _fixed((1, d))],
                 out_shape=[_sds((t, d), F32), _sds((2, 1, d), F32), _sds((2, 1, d), F32), _sds((1, d), F32)],
                 scratch=[pltpu.VMEM((tm, d), F32)])(*a_list, _w_arr(w), xs, dres, g, mods)


def _rope_tables(s, ctx_len):
    n_freq = HEAD_DIM // 4
    tok = jnp.arange(s)
    freqs = ROPE_THETA ** (-jnp.arange(n_freq, dtype=F32) / n_freq)
    ang = jnp.concatenate([(tok // GRID_W).astype(F32)[:, None] * freqs, (tok % GRID_W).astype(F32)[:, None] * freqs], axis=-1)
    cos = jnp.repeat(jnp.cos(ang), 2, axis=-1)
    sin = jnp.repeat(jnp.sin(ang), 2, axis=-1) * jnp.tile(jnp.array([-1.0, 1.0], F32), HEAD_DIM // 2)
    cos = jnp.concatenate([jnp.tile(cos, (1, HEADS)), jnp.ones((ctx_len, NA_WIDTH), F32)], axis=0)
    sin = jnp.concatenate([jnp.tile(sin, (1, HEADS)), jnp.zeros((ctx_len, NA_WIDTH), F32)], axis=0)
    return cos, sin


def _swap_pairs(x):
    n = x.shape[-1]
    lane = lax.broadcasted_iota(jnp.int32, x.shape, 1)
    return jnp.where(lane % 2 == 0, pltpu.roll(x, n - 1, 1), pltpu.roll(x, 1, 1))


def _rope(p, cos, sin, name):
    t = p.shape[0]
    w = NA_WIDTH
    te = _row_tile(t)

    def body(q_ref, k_ref, v_ref, c_ref, s_ref, qo_ref, ko_ref, vo_ref):
        c, s = c_ref[...], s_ref[...]
        q, k = q_ref[...].astype(F32), k_ref[...].astype(F32)
        qo_ref[...] = (q * c + _swap_pairs(q) * s).astype(qo_ref.dtype)
        ko_ref[...] = (k * c + _swap_pairs(k) * s).astype(ko_ref.dtype)
        vo_ref[...] = v_ref[...].astype(vo_ref.dtype)

    return _call(body, name=name, grid=(t // te,),
                 in_specs=[_rows(te, w, 0), _rows(te, w, 1), _rows(te, w, 2), _rows(te, w), _rows(te, w)],
                 out_specs=[_rows(te, w)] * 3, out_shape=[_sds((t, w), MXU)] * 3)(p, p, p, cos, sin)


def _rope_bwd(dq, dk, dv, cos, sin, name):
    t = dq.shape[0]
    te = _row_tile(t)
    n_pairs = NA_WIDTH // LANES

    def body(dq_ref, dk_ref, dv_ref, c_ref, s_ref, qo_ref, ko_ref, vo_ref):
        a = dq_ref[...]
        qo_ref[...] = (a * c_ref[...] + _swap_pairs(a * s_ref[...])).astype(qo_ref.dtype)
        for hp in range(n_pairs):
            cols = slice(hp * LANES, (hp + 1) * LANES)
            b = dk_ref[hp]
            ko_ref[:, cols] = (b * c_ref[:, cols] + _swap_pairs(b * s_ref[:, cols])).astype(ko_ref.dtype)
            vo_ref[:, cols] = dv_ref[hp].astype(vo_ref.dtype)

    pairs = pl.BlockSpec((n_pairs, te, LANES), lambda i: (0, i, 0))
    return _call(body, name=name, grid=(t // te,),
                 in_specs=[_rows(te, NA_WIDTH), pairs, pairs, _rows(te, NA_WIDTH), _rows(te, NA_WIDTH)],
                 out_specs=[_rows(te, NA_WIDTH)] * 3, out_shape=[_sds((t, NA_WIDTH), MXU)] * 3)(dq, dk, dv, cos, sin)


def _na_geometry(r_grid):
    rows = []
    for r0, ks in ((0, 0), (Q_ROWS, 0), (r_grid - Q_ROWS, r_grid - K_ROWS)):
        dr = np.zeros((Q_ROWS, K_ROWS), np.int32)
        vr = np.zeros((Q_ROWS, K_ROWS), bool)
        for a in range(Q_ROWS):
            r = r0 + a
            rs = min(max(r - WIN_H // 2, 0), r_grid - WIN_H)
            for i in range(K_ROWS):
                kr = ks + i
                vr[a, i] = rs <= kr <= rs + WIN_H - 1
                dr[a, i] = kr - r + WIN_H - 1
        rows.append((dr, vr))
    c = np.arange(GRID_W)
    cs = np.clip(c - WIN_W // 2, 0, GRID_W - WIN_W)
    kc = np.arange(GRID_W)
    vc = (kc[None, :] >= cs[:, None]) & (kc[None, :] <= cs[:, None] + WIN_W - 1)
    dc = kc[None, :] - c[:, None] + WIN_W - 1
    return rows, dc, vc


def _bias_table(rpb, r_grid):
    rows, _, vc = _na_geometry(r_grid)
    n_dc, off = 2 * WIN_W - 1, GRID_W - WIN_W
    u = jnp.pad(rpb, ((0, 0), (0, 0), (off, 2 * GRID_W - 1 - off - n_dc)))
    toep = jnp.stack([u[:, :, GRID_W - 1 - c:2 * GRID_W - 1 - c] for c in range(GRID_W)], axis=1)
    toep = jnp.pad(toep, ((0, 0), (0, 0), (Q_ROWS, Q_ROWS), (0, 0)))
    tabs = []
    for dr, vr in rows:
        per_row = []
        for a in range(Q_ROWS):
            lo = int(dr[a, 0]) + Q_ROWS
            valid = vc[:, None, :] & vr[a][None, :, None]
            per_row.append(jnp.where(valid[None], toep[:, :, lo:lo + K_ROWS, :], NEG))
        tabs.append(jnp.stack(per_row, axis=1).reshape(HEADS, TQ, TK))
    tabs.append(jnp.full((HEADS, TQ, TK), NEG, F32))
    return jnp.stack(tabs)


def _variant(g, ngx):
    return jnp.where(g == 0, 0, jnp.where(g >= ngx, 3, jnp.where(g == ngx - 1, 2, 1)))


def _key_start(g, r_grid):
    return pl.multiple_of(jnp.clip(g * Q_ROWS - WIN_H // 2, 0, r_grid - K_ROWS) * GRID_W, TQ)


def _nt(a, b):
    return lax.dot_general(a, b, (((1,), (1,)), ((), ())), preferred_element_type=F32)


def _tn(a, b):
    return lax.dot_general(a, b, (((0,), (0,)), ((), ())), preferred_element_type=F32)


def _nn(a, b):
    return jnp.dot(a, b, preferred_element_type=F32)


def _head_mask(h):
    lane = lax.broadcasted_iota(jnp.int32, (1, LANES), 1)
    return ((lane >= HEAD_DIM * h) & (lane < HEAD_DIM * (h + 1))).astype(F32)


def _softmax_parts(qm, knb, kcx, bias):
    s_nb = _nt(qm, knb) + bias
    s_cx = _nt(qm, kcx)
    m = jnp.maximum(jnp.max(s_nb, axis=-1, keepdims=True), jnp.max(s_cx, axis=-1, keepdims=True))
    e_nb = jnp.exp(s_nb - m)
    e_cx = jnp.exp(s_cx - m)
    inv = 1.0 / (jnp.sum(e_nb, axis=-1, keepdims=True) + jnp.sum(e_cx, axis=-1, keepdims=True))
    return e_nb * inv, e_cx * inv


def _na_specs(t, ngx):
    q_spec = pl.BlockSpec((TQ, LANES), lambda hp, g: (g, hp))
    kv_spec = pl.BlockSpec((t, LANES), lambda hp, g: (0, hp))
    b_spec = pl.BlockSpec((1, 2, TQ, TK), lambda hp, g: (_variant(g, ngx), hp, 0, 0))
    return q_spec, kv_spec, b_spec


def _na_fwd(q, k, v, bias, s, name):
    t = q.shape[0]
    ctx_len = t - s
    r_grid = s // GRID_W
    q_spec, kv_spec, b_spec = _na_specs(t, s // TQ)

    def body(q_ref, k_ref, v_ref, b_ref, o_ref):
        start = _key_start(pl.program_id(1), r_grid)
        qf = q_ref[...].astype(F32) * (HEAD_DIM ** -0.5)
        knb, vnb = k_ref[pl.ds(start, TK), :], v_ref[pl.ds(start, TK), :]
        kcx, vcx = k_ref[pl.ds(s, ctx_len), :], v_ref[pl.ds(s, ctx_len), :]
        acc = jnp.zeros((TQ, LANES), F32)
        for h in range(2):
            mask = _head_mask(h)
            p_nb, p_cx = _softmax_parts((qf * mask).astype(MXU), knb, kcx, b_ref[0, h])
            acc += (_nn(p_nb.astype(MXU), vnb) + _nn(p_cx.astype(MXU), vcx)) * mask
        o_ref[...] = acc.astype(o_ref.dtype)

    return _call(body, name=name, grid=(NA_WIDTH // LANES, t // TQ), in_specs=[q_spec, kv_spec, kv_spec, b_spec],
                 out_specs=q_spec, out_shape=_sds((t, NA_WIDTH), MXU))(q, k, v, bias)


def _na_bwd(q, k, v, do, bias, s, name):
    t = q.shape[0]
    ctx_len = t - s
    r_grid = s // GRID_W
    ng, ngx = t // TQ, s // TQ
    q_spec, kv_spec, b_spec = _na_specs(t, ngx)

    def body(q_ref, k_ref, v_ref, do_ref, b_ref, dq_ref, dk_hbm, dv_hbm, db_ref, dk_acc, dv_acc):
        hp, g = pl.program_id(0), pl.program_id(1)
        start = _key_start(g, r_grid)

        @pl.when(g == 0)
        def _():
            dk_acc[...] = jnp.zeros_like(dk_acc)
            dv_acc[...] = jnp.zeros_like(dv_acc)

        @pl.when((g == 0) | (g == 1) | (g == ngx - 1) | (g == ngx))
        def _():
            db_ref[...] = jnp.zeros_like(db_ref)

        qf = q_ref[...].astype(F32) * (HEAD_DIM ** -0.5)
        do = do_ref[...].astype(F32)
        knb, vnb = k_ref[pl.ds(start, TK), :], v_ref[pl.ds(start, TK), :]
        kcx, vcx = k_ref[pl.ds(s, ctx_len), :], v_ref[pl.ds(s, ctx_len), :]
        dq = jnp.zeros((TQ, LANES), F32)
        dk_nb = jnp.zeros((TK, LANES), F32)
        dv_nb = jnp.zeros((TK, LANES), F32)
        dk_cx = jnp.zeros((ctx_len, LANES), F32)
        dv_cx = jnp.zeros((ctx_len, LANES), F32)
        for h in range(2):
            mask = _head_mask(h)
            qm = (qf * mask).astype(MXU)
            dom = (do * mask).astype(MXU)
            p_nb, p_cx = _softmax_parts(qm, knb, kcx, b_ref[0, h])
            dp_nb = _nt(dom, vnb)
            dp_cx = _nt(dom, vcx)
            delta = jnp.sum(p_nb * dp_nb, axis=-1, keepdims=True) + jnp.sum(p_cx * dp_cx, axis=-1, keepdims=True)
            ds_nb = p_nb * (dp_nb - delta)
            ds_cx = p_cx * (dp_cx - delta)
            db_ref[0, h] += ds_nb
            ds_nb, ds_cx = ds_nb.astype(MXU), ds_cx.astype(MXU)
            dq += (_nn(ds_nb, knb) + _nn(ds_cx, kcx)) * (mask * (HEAD_DIM ** -0.5))
            dk_nb += _tn(ds_nb, qm)
            dk_cx += _tn(ds_cx, qm)
            dv_nb += _tn(p_nb.astype(MXU), dom)
            dv_cx += _tn(p_cx.astype(MXU), dom)
        dq_ref[...] = dq
        dk_acc[pl.ds(start, TK), :] += dk_nb
        dv_acc[pl.ds(start, TK), :] += dv_nb
        dk_acc[pl.ds(s, ctx_len), :] += dk_cx
        dv_acc[pl.ds(s, ctx_len), :] += dv_cx

        @pl.when(g == ng - 1)
        def _():
            pltpu.sync_copy(dk_acc, dk_hbm.at[hp])
            pltpu.sync_copy(dv_acc, dv_hbm.at[hp])

    n_pairs = NA_WIDTH // LANES
    hbm = pl.BlockSpec(memory_space=pl.ANY)
    return _call(body, name=name, grid=(n_pairs, ng), in_specs=[q_spec, kv_spec, kv_spec, q_spec, b_spec],
                 out_specs=[q_spec, hbm, hbm, b_spec],
                 out_shape=[_sds((t, NA_WIDTH), F32), _sds((n_pairs, t, LANES), F32), _sds((n_pairs, t, LANES), F32),
                            _sds((4, HEADS, TQ, TK), F32)],
                 scratch=[pltpu.VMEM((t, LANES), F32), pltpu.VMEM((t, LANES), F32)])(q, k, v, do, bias)


def _rpb_grad(dbias, r_grid, name):
    rows, _, _ = _na_geometry(r_grid)
    n_dr, half, skew, lanes = 2 * WIN_H - 1, WIN_W - 1, TK + 2, 896
    z = dbias[:3].reshape(3, HEADS, Q_ROWS, GRID_W, TK)
    z = jnp.pad(z, ((0, 0),) * 4 + ((0, 1),)).reshape(3, HEADS, Q_ROWS, GRID_W * (TK + 1))
    z = jnp.pad(z, ((0, 0),) * 3 + ((0, GRID_W),)).reshape(3, HEADS, Q_ROWS, GRID_W, skew)
    z = jnp.pad(z, ((0, 0),) * 4 + ((0, lanes - skew),))

    def body(z_ref, o_ref):
        sums = [jnp.sum(z_ref[v, 0, a], axis=0, keepdims=True) for v in range(3) for a in range(Q_ROWS)]
        zs = jnp.concatenate(sums + [jnp.zeros((16 - 3 * Q_ROWS, lanes), F32)], axis=0)
        acc = [jnp.zeros((1, lanes), F32) for _ in range(n_dr)]
        for i in range(K_ROWS):
            if i == 0:
                at0 = pltpu.roll(zs, lanes - (skew - half), 1) + pltpu.roll(zs, half, 1)
            else:
                at0 = pltpu.roll(zs, lanes - (i * GRID_W - half), 1)
            for v, (dr, vr) in enumerate(rows):
                for a in range(Q_ROWS):
                    if vr[a, i]:
                        acc[dr[a, i]] = acc[dr[a, i]] + at0[v * Q_ROWS + a:v * Q_ROWS + a + 1, :]
        o_ref[0] = jnp.concatenate(acc + [jnp.zeros((1, lanes), F32)], axis=0)

    o = _call(body, name=name, grid=(HEADS,),
              in_specs=[pl.BlockSpec((3, 1, Q_ROWS, GRID_W, lanes), lambda h: (0, h, 0, 0, 0))],
              out_specs=pl.BlockSpec((1, 16, lanes), lambda h: (h, 0, 0)), out_shape=_sds((HEADS, 16, lanes), F32))(z)
    return o[:, :n_dr, :2 * WIN_W - 1]


_GELU_K, _GELU_C = 0.7978845608028654, 0.044715


def _gelu(x):
    return 0.5 * x * (1.0 + jnp.tanh(_GELU_K * (x + _GELU_C * x * x * x)))


def _gelu_grad(x):
    th = jnp.tanh(_GELU_K * (x + _GELU_C * x * x * x))
    return 0.5 * (1.0 + th) + 0.5 * x * (1.0 - th * th) * (_GELU_K * (1.0 + 3.0 * _GELU_C * x * x))


def _ln_stats(v):
    mu = jnp.mean(v, axis=-1, keepdims=True)
    vc = v - mu
    rstd = lax.rsqrt(jnp.mean(vc * vc, axis=-1, keepdims=True) + EPS)
    return vc * rstd, rstd


def _gmlp(p, ln_g, ln_b, w_s, b_s, name):
    t = p.shape[0]
    te = _row_tile(t)
    w = SG_WIDTH
    cw = w // SG_GROUPS

    def body(u_ref, v_ref, g_ref, b_ref, ws_ref, bs_ref, o_ref):
        xhat, _ = _ln_stats(_gelu(v_ref[...].astype(F32)))
        vn = (xhat * g_ref[...] + b_ref[...]).astype(MXU)
        ug = _gelu(u_ref[...].astype(F32))
        for ci in range(te // SG_CHUNK):
            rs = slice(ci * SG_CHUNK, (ci + 1) * SG_CHUNK)
            for gi in range(SG_GROUPS):
                cs = slice(gi * cw, (gi + 1) * cw)
                sg = _nn(ws_ref[gi].astype(MXU), vn[rs, cs]) + bs_ref[gi]
                o_ref[rs, cs] = (ug[rs, cs] * sg).astype(o_ref.dtype)

    return _call(body, name=name, grid=(t // te,),
                 in_specs=[_rows(te, w, 3), _rows(te, w, 4), _fixed((1, w)), _fixed((1, w)),
                           _fixed((SG_GROUPS, SG_CHUNK, SG_CHUNK)), _fixed((SG_GROUPS, SG_CHUNK, 1))],
                 out_specs=_rows(te, w), out_shape=_sds((t, w), MXU))(p, p, ln_g, ln_b, w_s, b_s)


def _gmlp_bwd(p, dob, ln_g, ln_b, w_s, b_s, name):
    t = p.shape[0]
    te = _row_tile(t)
    w = SG_WIDTH
    cw = w // SG_GROUPS

    def body(u_ref, v_ref, do_ref, g_ref, b_ref, ws_ref, bs_ref, du_ref, dv_ref, dws_ref, dbs_ref, dg_ref, db_ref, dvn_ref):
        @pl.when(pl.program_id(0) == 0)
        def _():
            dws_ref[...] = jnp.zeros_like(dws_ref)
            dbs_ref[...] = jnp.zeros_like(dbs_ref)
            dg_ref[...] = jnp.zeros_like(dg_ref)
            db_ref[...] = jnp.zeros_like(db_ref)

        u, v = u_ref[...].astype(F32), v_ref[...].astype(F32)
        xhat, rstd = _ln_stats(_gelu(v))
        vn = (xhat * g_ref[...] + b_ref[...]).astype(MXU)
        ug = _gelu(u)
        dob = do_ref[...].astype(F32)
        for ci in range(te // SG_CHUNK):
            rs = slice(ci * SG_CHUNK, (ci + 1) * SG_CHUNK)
            for gi in range(SG_GROUPS):
                cs = slice(gi * cw, (gi + 1) * cw)
                wsg = ws_ref[gi].astype(MXU)
                sg = _nn(wsg, vn[rs, cs]) + bs_ref[gi]
                du_ref[rs, cs] = (dob[rs, cs] * sg * _gelu_grad(u[rs, cs])).astype(du_ref.dtype)
                ds = dob[rs, cs] * ug[rs, cs]
                dbs_ref[gi] += jnp.sum(ds, axis=-1, keepdims=True)
                ds = ds.astype(MXU)
                dws_ref[gi] += _nt(ds, vn[rs, cs])
                dvn_ref[rs, cs] = _tn(wsg, ds)
        dvn = dvn_ref[...]
        dg_ref[...] += jnp.sum(dvn * xhat, axis=0, keepdims=True)
        db_ref[...] += jnp.sum(dvn, axis=0, keepdims=True)
        dxh = dvn * g_ref[...]
        dvg = rstd * (dxh - jnp.mean(dxh, axis=-1, keepdims=True) - xhat * jnp.mean(dxh * xhat, axis=-1, keepdims=True))
        dv_ref[...] = (dvg * _gelu_grad(v)).astype(dv_ref.dtype)

    return _call(body, name=name, grid=(t // te,),
                 in_specs=[_rows(te, w, 3), _rows(te, w, 4), _rows(te, w), _fixed((1, w)), _fixed((1, w)),
                           _fixed((SG_GROUPS, SG_CHUNK, SG_CHUNK)), _fixed((SG_GROUPS, SG_CHUNK, 1))],
                 out_specs=[_rows(te, w), _rows(te, w), _fixed((SG_GROUPS, SG_CHUNK, SG_CHUNK)),
                            _fixed((SG_GROUPS, SG_CHUNK, 1)), _fixed((1, w)), _fixed((1, w))],
                 out_shape=[_sds((t, w), MXU), _sds((t, w), MXU), _sds((SG_GROUPS, SG_CHUNK, SG_CHUNK), F32),
                            _sds((SG_GROUPS, SG_CHUNK, 1), F32), _sds((1, w), F32), _sds((1, w), F32)],
                 scratch=[pltpu.VMEM((te, w), F32)])(p, p, dob, ln_g, ln_b, w_s, b_s)


def _merge(pa, pb, p, b_gate, name):
    t, d = pa.shape
    te = _row_tile(t)
    hw = NA_WIDTH
    nh = d // hw
    c0 = (NA_WIDTH * 3 + SG_WIDTH * 2) // hw

    def body(pa_ref, pb_ref, la_ref, lb_ref, ba_ref, bb_ref, o_ref):
        ga = jax.nn.sigmoid(la_ref[...].astype(F32) + ba_ref[...])
        gb = jax.nn.sigmoid(lb_ref[...].astype(F32) + bb_ref[...])
        o_ref[...] = (ga * pa_ref[...].astype(F32) + gb * pb_ref[...].astype(F32)).astype(o_ref.dtype)

    tile = pl.BlockSpec((te, hw), lambda i, j: (i, j))
    return _call(body, name=name, grid=(t // te, nh),
                 in_specs=[tile, tile, pl.BlockSpec((te, hw), lambda i, j: (i, c0 + j)),
                           pl.BlockSpec((te, hw), lambda i, j: (i, c0 + nh + j)),
                           pl.BlockSpec((1, hw), lambda i, j: (0, j)), pl.BlockSpec((1, hw), lambda i, j: (0, nh + j))],
                 out_specs=tile, out_shape=_sds((t, d), MXU))(pa, pb, p, p, b_gate, b_gate)


def _merge_bwd(dmg, pa, pb, p, b_gate, name):
    t, d = pa.shape
    te = _row_tile(t)
    hw = NA_WIDTH
    nh = d // hw
    c0 = (NA_WIDTH * 3 + SG_WIDTH * 2) // hw

    def body(dm_ref, pa_ref, pb_ref, la_ref, lb_ref, ba_ref, bb_ref, dpa_ref, dpb_ref, dla_ref, dlb_ref, dba_ref, dbb_ref):
        @pl.when(pl.program_id(1) == 0)
        def _():
            dba_ref[...] = jnp.zeros_like(dba_ref)
            dbb_ref[...] = jnp.zeros_like(dbb_ref)

        dm = dm_ref[...].astype(F32)
        ga = jax.nn.sigmoid(la_ref[...].astype(F32) + ba_ref[...])
        gb = jax.nn.sigmoid(lb_ref[...].astype(F32) + bb_ref[...])
        dpa_ref[...] = (dm * ga).astype(dpa_ref.dtype)
        dpb_ref[...] = (dm * gb).astype(dpb_ref.dtype)
        dla = dm * pa_ref[...].astype(F32) * ga * (1.0 - ga)
        dlb = dm * pb_ref[...].astype(F32) * gb * (1.0 - gb)
        dla_ref[...] = dla.astype(dla_ref.dtype)
        dlb_ref[...] = dlb.astype(dlb_ref.dtype)
        dba_ref[...] += jnp.sum(dla, axis=0, keepdims=True)
        dbb_ref[...] += jnp.sum(dlb, axis=0, keepdims=True)

    tile = pl.BlockSpec((te, hw), lambda j, i: (i, j))
    bias_a = pl.BlockSpec((1, hw), lambda j, i: (0, j))
    bias_b = pl.BlockSpec((1, hw), lambda j, i: (0, nh + j))
    return _call(body, name=name, grid=(nh, t // te),
                 in_specs=[tile, tile, tile, pl.BlockSpec((te, hw), lambda j, i: (i, c0 + j)),
                           pl.BlockSpec((te, hw), lambda j, i: (i, c0 + nh + j)), bias_a, bias_b],
                 out_specs=[tile, tile, tile, tile, bias_a, bias_a],
                 out_shape=[_sds((t, d), MXU)] * 4 + [_sds((1, d), F32)] * 2)(dmg, pa, pb, p, p, b_gate, b_gate)


def _final(xs, tgt, g, name):
    t, d = xs.shape
    nx = tgt.shape[0] // TM

    def body(x_ref, t_ref, g_ref, l_ref, dx_ref, dg_ref):
        i = pl.program_id(0)

        @pl.when(i == 0)
        def _():
            l_ref[...] = jnp.zeros_like(l_ref)
            dg_ref[...] = jnp.zeros_like(dg_ref)

        @pl.when(i < nx)
        def _():
            x = x_ref[...]
            rstd = lax.rsqrt(jnp.mean(x * x, axis=-1, keepdims=True) + EPS)
            xhat = x * rstd
            err = xhat * g_ref[...] - t_ref[...]
            l_ref[...] += 0.5 * jnp.sum(jnp.mean(err * err, axis=-1, keepdims=True))
            dy = err * (1.0 / d)
            dg_ref[...] += jnp.sum(dy * xhat, axis=0, keepdims=True)
            dxh = dy * g_ref[...]
            dx_ref[...] = rstd * (dxh - xhat * jnp.mean(dxh * xhat, axis=-1, keepdims=True))

        @pl.when(i >= nx)
        def _():
            dx_ref[...] = jnp.zeros_like(dx_ref)

    return _call(body, name=name, grid=(t // TM,),
                 in_specs=[_rows(TM, d), pl.BlockSpec((TM, d), lambda i: (jnp.minimum(i, nx - 1), 0)), _fixed((1, d))],
                 out_specs=[_fixed((1, LANES)), _rows(TM, d), _fixed((1, d))],
                 out_shape=[_sds((1, LANES), F32), _sds((t, d), F32), _sds((1, d), F32)])(xs, tgt, g)


def _view2d(a):
    return a.reshape(1, -1) if a.ndim == 1 else a.reshape(-1, a.shape[-1])


def _tile_rows(r, c):
    for cand in (1024, 512, 256, 128, 64, 32, 16):
        if r % cand == 0 and cand * c * 4 <= 2 ** 20:
            return cand
    return r


def _pair_sum(g, recv, layer, name):
    _, a, b = g.shape
    tr = _tile_rows(a, b)

    def body(l_ref, g_ref, r_ref, o32_ref, o16_ref):
        acc = g_ref[...] + r_ref[...]
        o32_ref[...] = acc
        o16_ref[...] = acc.astype(o16_ref.dtype)

    first = pl.BlockSpec((None, tr, b), lambda i, l: (0, i, 0))
    return pl.pallas_call(
        body, name=name, out_shape=[_sds((1, a, b), F32), _sds((1, a, b), MXU)],
        grid_spec=pltpu.PrefetchScalarGridSpec(
            num_scalar_prefetch=1, grid=(a // tr,),
            in_specs=[pl.BlockSpec((None, tr, b), lambda i, l: (l[0], i, 0)), first], out_specs=[first, first]),
        compiler_params=pltpu.CompilerParams(dimension_semantics=("arbitrary",), vmem_limit_bytes=VMEM_LIMIT))(layer, g, recv)


def _ew(fn, arrays, out_dtypes, name):
    shape = arrays[0].shape
    views = [_view2d(a) for a in arrays]
    r, c = views[0].shape
    tr = _tile_rows(r, c)

    def body(*refs):
        outs = fn(*[ref[...] for ref in refs[:len(views)]])
        for ref, o in zip(refs[len(views):], outs):
            ref[...] = o.astype(ref.dtype)

    res = _call(body, name=name, grid=(r // tr,), in_specs=[_rows(tr, c)] * len(views), out_specs=[_rows(tr, c)] * len(out_dtypes),
                out_shape=[_sds((r, c), dt) for dt in out_dtypes])(*views)
    return [o.reshape(shape) for o in res]


def _sum_pieces(pieces, name, out_dtypes=(F32,)):
    def fn(*vals):
        acc = vals[0].astype(F32)
        for v in vals[1:]:
            acc = acc + v.astype(F32)
        return (acc,) * len(out_dtypes)

    return _ew(fn, pieces, list(out_dtypes), name)


def _adamw(w, g_pieces, m, v, name):
    n_g = len(g_pieces)

    def fn(w_, *rest):
        g = rest[0]
        for piece in rest[1:n_g]:
            g = g + piece
        m_, v_ = rest[n_g], rest[n_g + 1]
        m2 = ADAM_B1 * m_ + (1.0 - ADAM_B1) * g
        v2 = ADAM_B2 * v_ + (1.0 - ADAM_B2) * (g * g)
        m_hat = m2 / (1.0 - ADAM_B1 ** ADAM_STEP)
        v_hat = v2 / (1.0 - ADAM_B2 ** ADAM_STEP)
        delta = -ADAM_LR * (m_hat / (jnp.sqrt(v_hat) + ADAM_EPS) + ADAM_WD * w_)
        return g, delta, m2, v2

    return _ew(fn, [w, *g_pieces, m, v], [F32] * 4, name)


def _ada_fwd(cond, w, b, name):
    r, d = cond.shape
    n = w.shape[1]
    tn = _pick(n, (1152, 768, 512, 384, 256, 128))

    def body(c_ref, w_ref, b_ref, o_ref, s_ref):
        c = c_ref[...]
        sc = c * jax.nn.sigmoid(c)
        s_ref[...] = sc
        o_ref[...] = _nn(sc.astype(MXU), w_ref[...].astype(MXU)) + b_ref[...]

    return _call(body, name=name, grid=(n // tn,),
                 in_specs=[_fixed((r, d)), pl.BlockSpec((d, tn), lambda j: (0, j)), pl.BlockSpec((1, tn), lambda j: (0, j))],
                 out_specs=[pl.BlockSpec((r, tn), lambda j: (0, j)), _fixed((r, d))],
                 out_shape=[_sds((r, n), F32), _sds((r, d), F32)])(cond, w, b)


def _cctx_grad(parts, c_ctx, name):
    n, d = parts.shape

    def body(p_ref, c_ref, o_ref):
        c = c_ref[...]
        sg = jax.nn.sigmoid(c)
        acc = p_ref[0:1, :]
        for j in range(1, n):
            acc = acc + p_ref[j:j + 1, :]
        o_ref[...] = acc * (sg * (1.0 + c * (1.0 - sg)))

    return _call(body, name=name, grid=(1,), in_specs=[_fixed((n, d)), _fixed((1, d))], out_specs=_fixed((1, d)),
                 out_shape=_sds((1, d), F32))(parts, c_ctx)


def _here():
    return lax.axis_index("x"), lax.axis_index("y"), lax.axis_index("c")


def _flip(v, bit):
    return 1 - v if bit else v


def _allgather8(xb, name):
    r, n = xb.shape

    def body(x_ref, out_ref, send_sems, recv_sems, local_sem):
        x, y, c = _here()
        me = 4 * x + 2 * y + c
        local = pltpu.make_async_copy(x_ref, out_ref.at[me], local_sem)
        local.start()
        sends = []
        for k in range(1, 8):
            peer = (_flip(x, k & 4), _flip(y, k & 2), _flip(c, k & 1))
            cp = pltpu.make_async_remote_copy(src_ref=x_ref, dst_ref=out_ref.at[me], send_sem=send_sems.at[k - 1],
                                              recv_sem=recv_sems.at[k - 1], device_id=peer, device_id_type=MESH)
            cp.start()
            sends.append(cp)
        for k in range(1, 8):
            peer = (_flip(x, k & 4), _flip(y, k & 2), _flip(c, k & 1))
            src = 4 * peer[0] + 2 * peer[1] + peer[2]
            pltpu.make_async_remote_copy(src_ref=x_ref, dst_ref=out_ref.at[src], send_sem=send_sems.at[k - 1],
                                         recv_sem=recv_sems.at[k - 1], device_id=peer, device_id_type=MESH).wait_recv()
        for cp in sends:
            cp.wait_send()
        local.wait()

    vmem = pl.BlockSpec(memory_space=pltpu.VMEM)
    return pl.pallas_call(
        body, name=name, out_shape=_sds((8, r, n), xb.dtype), in_specs=[vmem], out_specs=vmem,
        scratch_shapes=[pltpu.SemaphoreType.DMA((7,)), pltpu.SemaphoreType.DMA((7,)), pltpu.SemaphoreType.DMA(())],
        compiler_params=pltpu.CompilerParams(vmem_limit_bytes=VMEM_LIMIT))(xb)


def _shard_of(ref, axis, j, size):
    sl = pl.ds(j * size, size)
    return ref.at[:, sl, :] if axis == 1 else ref.at[:, :, sl]


def _piece(ref, axis, j, size, layer):
    lay, sl = pl.ds(layer, 1), pl.ds(j * size, size)
    return ref.at[lay, sl, :] if axis == 1 else ref.at[lay, :, sl]


def _gather_chips(shards, axes, name):
    n = len(shards)
    fulls = []
    for a, ax in zip(shards, axes):
        assert a.shape[0] == 2
        shp = list(a.shape)
        shp[ax] *= 4
        fulls.append(_sds(tuple(shp), a.dtype))

    def body(*refs):
        ins, outs = refs[:n], refs[n:2 * n]
        ici_send, ici_recv, d2d_send, d2d_recv, local_sems = refs[2 * n:]
        x, y, c = _here()
        chips = [(_flip(x, k & 2), _flip(y, k & 1)) for k in range(1, 4)]
        local, sends = [], []
        for a in range(n):
            size = ins[a].shape[axes[a]]
            cp = pltpu.make_async_copy(ins[a], _shard_of(outs[a], axes[a], 2 * x + y, size), local_sems.at[a])
            cp.start()
            local.append(cp)
            for j, (px, py) in enumerate(chips):
                cp = pltpu.make_async_remote_copy(src_ref=ins[a].at[pl.ds(c, 1)], dst_ref=_piece(outs[a], axes[a], 2 * x + y, size, c),
                                                  send_sem=ici_send.at[3 * a + j], recv_sem=ici_recv.at[3 * a + j],
                                                  device_id=(px, py, c), device_id_type=MESH)
                cp.start()
                sends.append(cp)
        for a in range(n):
            size = ins[a].shape[axes[a]]
            for j, (px, py) in enumerate(chips):
                landed = _piece(outs[a], axes[a], 2 * px + py, size, c)
                pltpu.make_async_remote_copy(src_ref=ins[a].at[pl.ds(c, 1)], dst_ref=landed, send_sem=ici_send.at[3 * a + j],
                                             recv_sem=ici_recv.at[3 * a + j], device_id=(px, py, c), device_id_type=MESH).wait_recv()
                cp = pltpu.make_async_remote_copy(src_ref=landed, dst_ref=landed, send_sem=d2d_send.at[3 * a + j],
                                                  recv_sem=d2d_recv.at[3 * a + j], device_id=(x, y, 1 - c), device_id_type=MESH)
                cp.start()
                sends.append(cp)
        for a in range(n):
            size = ins[a].shape[axes[a]]
            for j, (px, py) in enumerate(chips):
                passed = _piece(outs[a], axes[a], 2 * px + py, size, 1 - c)
                pltpu.make_async_remote_copy(src_ref=passed, dst_ref=passed, send_sem=d2d_send.at[3 * a + j],
                                             recv_sem=d2d_recv.at[3 * a + j], device_id=(x, y, 1 - c), device_id_type=MESH).wait_recv()
        for cp in sends:
            cp.wait_send()
        for cp in local:
            cp.wait()

    hbm = pl.BlockSpec(memory_space=pl.ANY)
    return pl.pallas_call(
        body, name=name, out_shape=fulls, in_specs=[hbm] * n, out_specs=[hbm] * n,
        scratch_shapes=[pltpu.SemaphoreType.DMA((3 * n,))] * 4 + [pltpu.SemaphoreType.DMA((n,))])(*shards)


def _swap_layers(arrays, name):
    n = len(arrays)

    def body(*refs):
        ins, outs = refs[:n], refs[n:2 * n]
        send_sems, recv_sems = refs[2 * n:]
        x, y, c = _here()
        copies = []
        for a in range(n):
            cp = pltpu.make_async_remote_copy(src_ref=ins[a].at[pl.ds(1 - c, 1)], dst_ref=outs[a], send_sem=send_sems.at[a],
                                              recv_sem=recv_sems.at[a], device_id=(x, y, 1 - c), device_id_type=MESH)
            cp.start()
            copies.append(cp)
        for cp in copies:
            cp.wait()

    hbm = pl.BlockSpec(memory_space=pl.ANY)
    return pl.pallas_call(
        body, name=name, out_shape=[_sds((1, *a.shape[1:]), a.dtype) for a in arrays], in_specs=[hbm] * n, out_specs=[hbm] * n,
        scratch_shapes=[pltpu.SemaphoreType.DMA((n,)), pltpu.SemaphoreType.DMA((n,))])(*arrays)


def _scatter_chips(fulls, axes, name):
    n = len(fulls)
    recvs = []
    for a, ax in zip(fulls, axes):
        shp = list(a.shape)
        shp[ax] //= 4
        recvs.append(_sds((3, *shp), a.dtype))

    def body(*refs):
        ins, outs = refs[:n], refs[n:2 * n]
        send_sems, recv_sems = refs[2 * n:]
        x, y, c = _here()
        sends = []
        for a in range(n):
            size = ins[a].shape[axes[a]] // 4
            for k in range(1, 4):
                peer = (_flip(x, k & 2), _flip(y, k & 1), c)
                cp = pltpu.make_async_remote_copy(src_ref=_shard_of(ins[a], axes[a], 2 * peer[0] + peer[1], size),
                                                  dst_ref=outs[a].at[k - 1],
                                                  send_sem=send_sems.at[3 * a + k - 1], recv_sem=recv_sems.at[3 * a + k - 1],
                                                  device_id=peer, device_id_type=MESH)
                cp.start()
                sends.append(cp)
        for cp in sends:
            cp.wait_recv()
        for cp in sends:
            cp.wait_send()

    hbm = pl.BlockSpec(memory_space=pl.ANY)
    return pl.pallas_call(
        body, name=name, out_shape=recvs, in_specs=[hbm] * n, out_specs=[hbm] * n,
        scratch_shapes=[pltpu.SemaphoreType.DMA((3 * n,)), pltpu.SemaphoreType.DMA((3 * n,))])(*fulls)


def _sibling_swap(arrays, name):
    n = len(arrays)

    def body(*refs):
        ins, outs = refs[:n], refs[n:2 * n]
        send_sems, recv_sems = refs[2 * n:]
        x, y, c = _here()
        copies = []
        for a in range(n):
            cp = pltpu.make_async_remote_copy(src_ref=ins[a], dst_ref=outs[a], send_sem=send_sems.at[a], recv_sem=recv_sems.at[a],
                                              device_id=(x, y, 1 - c), device_id_type=MESH)
            cp.start()
            copies.append(cp)
        for cp in copies:
            cp.wait()

    hbm = pl.BlockSpec(memory_space=pl.ANY)
    return pl.pallas_call(
        body, name=name, out_shape=[_sds(a.shape, a.dtype) for a in arrays], in_specs=[hbm] * n, out_specs=[hbm] * n,
        scratch_shapes=[pltpu.SemaphoreType.DMA((n,)), pltpu.SemaphoreType.DMA((n,))])(*arrays)


def _ffn_fwd(xs, g, mods, k0, w_up, w_down, s, tag):
    h, ua, ub, act = _norm_mm(xs, g, mods, k0, k0 + 1, w_up, s, tag + "_up", True)
    y, xn = _mm_res(act, w_down, xs, mods, k0 + 2, 0.5, s, tag + "_down")
    return xn, (xs, h, ua, ub, act, y)


def _dw(gw, key, a, b, name, col0=0, n_total=None):
    shape = (gw["depth"], a.shape[1], n_total or b.shape[1])
    gw[key] = _mm(a, b, "tn", F32, name, into=(gw.get(key), shape, gw["layer"], col0))


def _ffn_bwd(dxn, saved, g, mods, k0, w_up, w_down, s, tag, gw, up_key, down_key):
    xs, h, ua, ub, act, y = saved
    dy, dgate, dua, dub = _resb_mm(dxn, y, mods, k0 + 2, 0.5, w_down, s, tag + "_down_dx", (ua, ub))
    _dw(gw, down_key, act, dy, tag + "_down_dw")
    f = dua.shape[1]
    _dw(gw, up_key, h, dua, tag + "_upa_dw", 0, 2 * f)
    _dw(gw, up_key, h, dub, tag + "_upb_dw", f, 2 * f)
    dx, dsh, dsc, dg = _mm_normb([dua, dub], w_up, xs, dxn, g, mods, k0 + 1, s, tag + "_up_dx")
    return dx, dg, [dsh, dsc, dgate]


def _mix_fwd(xs, g, mods, wl, pl_, tabs, s, tag):
    cos, sin = tabs
    h, p = _norm_mm(xs, g, mods, 3, 4, wl["w_in"], s, tag + "_in", False)
    q, k, v = _rope(p, cos, sin, tag + "_rope")
    bias = _bias_table(pl_["rpb"], s // GRID_W)
    oa = _na_fwd(q, k, v, bias, s, tag + "_na")
    ob = _gmlp(p, pl_["ln_v_g"], pl_["ln_v_b"], pl_["w_s"], pl_["b_s"], tag + "_sg")
    pa = _mm(oa, wl["w_pa"], "nn", MXU, tag + "_pa")
    pb = _mm(ob, wl["w_pb"], "nn", MXU, tag + "_pb")
    mg = _merge(pa, pb, p, pl_["b_gate"], tag + "_merge")
    y, xn = _mm_res(mg, wl["w_o"], xs, mods, 5, 1.0, s, tag + "_o")
    return xn, (xs, h, p, q, k, v, bias, oa, ob, pa, pb, mg, y)


def _mix_bwd(dxn, saved, g, mods, wl, pl_, tabs, s, tag, gw):
    xs, h, p, q, k, v, bias, oa, ob, pa, pb, mg, y = saved
    cos, sin = tabs
    gp = {}
    dy, dgate, dmg = _resb_mm(dxn, y, mods, 5, 1.0, wl["w_o"], s, tag + "_o_dx")
    _dw(gw, "w_o", mg, dy, tag + "_o_dw")
    dpa, dpb, dla, dlb, dba, dbb = _merge_bwd(dmg, pa, pb, p, pl_["b_gate"], tag + "_merge_b")
    gp["b_gate"] = jnp.concatenate([dba, dbb], axis=1)
    _dw(gw, "w_pa", oa, dpa, tag + "_pa_dw")
    doa = _mm(dpa, wl["w_pa"], "nt", MXU, tag + "_pa_dx")
    _dw(gw, "w_pb", ob, dpb, tag + "_pb_dw")
    dob = _mm(dpb, wl["w_pb"], "nt", MXU, tag + "_pb_dx")
    du, dvs, gp["w_s"], gp["b_s"], gp["ln_v_g"], gp["ln_v_b"] = _gmlp_bwd(
        p, dob, pl_["ln_v_g"], pl_["ln_v_b"], pl_["w_s"], pl_["b_s"], tag + "_sg_b")
    dqr, dkr, dv, dbias = _na_bwd(q, k, v, doa, bias, s, tag + "_na_b")
    gp["rpb"] = _rpb_grad(dbias, s // GRID_W, tag + "_rpb")
    dq, dk, dvv = _rope_bwd(dqr, dkr, dv, cos, sin, tag + "_rope_b")
    dp = jnp.concatenate([dq, dk, dvv, du, dvs, dla, dlb], axis=1)
    _dw(gw, "w_in", h, dp, tag + "_in_dw")
    dx, dsh, dsc, dg = _mm_normb([dp], wl["w_in"], xs, dxn, g, mods, 4, s, tag + "_in_dx")
    return dx, gp, dg, [dsh, dsc, dgate]


def _local_step(x, ctx, tgt, mods, wts, prm):
    s, d = x.shape
    depth = mods.shape[0]
    tabs = _rope_tables(s, ctx.shape[0])
    xs = jnp.concatenate([x, ctx], axis=0)
    saved = []
    for l in range(depth):
        wl = {k: (v, l) for k, v in wts.items()}
        pl_ = _layer_params(prm, l)
        xs, s1 = _ffn_fwd(xs, pl_["g"][0], mods[l], 0, wl["w_ff1_up"], wl["w_ff1_down"], s, f"l{l}_ff1")
        xs, s2 = _mix_fwd(xs, pl_["g"][1], mods[l], wl, pl_, tabs, s, f"l{l}_mix")
        xs, s3 = _ffn_fwd(xs, pl_["g"][2], mods[l], 6, wl["w_ff2_up"], wl["w_ff2_down"], s, f"l{l}_ff2")
        saved.append((s1, s2, s3))
    loss, dxs, d_final_g = _final(xs, tgt, prm["final_g"].reshape(1, d), "final")
    gw = {"depth": depth}
    gp = {k: [None] * depth for k in ("norm_g", "b_gate", "rpb", "ln_v_g", "ln_v_b", "w_s", "b_s")}
    dmods = [None] * depth
    for l in reversed(range(depth)):
        wl = {k: (v, l) for k, v in wts.items()}
        pl_ = _layer_params(prm, l)
        s1, s2, s3 = saved[l]
        gw["layer"] = l
        dxs, dg2, dm2 = _ffn_bwd(dxs, s3, pl_["g"][2], mods[l], 6, wl["w_ff2_up"], wl["w_ff2_down"], s, f"l{l}_ff2",
                                 gw, "w_ff2_up", "w_ff2_down")
        dxs, gpm, dg1, dm1 = _mix_bwd(dxs, s2, pl_["g"][1], mods[l], wl, pl_, tabs, s, f"l{l}_mix", gw)
        dxs, dg0, dm0 = _ffn_bwd(dxs, s1, pl_["g"][0], mods[l], 0, wl["w_ff1_up"], wl["w_ff1_down"], s, f"l{l}_ff1",
                                 gw, "w_ff1_up", "w_ff1_down")
        gp["b_gate"][l] = gpm["b_gate"][0]
        gp["rpb"][l] = gpm["rpb"]
        gp["ln_v_g"][l] = gpm["ln_v_g"][0]
        gp["ln_v_b"][l] = gpm["ln_v_b"][0]
        gp["w_s"][l] = gpm["w_s"]
        gp["b_s"][l] = gpm["b_s"][..., 0]
        gp["norm_g"][l] = jnp.concatenate([dg0, dg1, dg2], axis=0)
        dmods[l] = jnp.concatenate(dm0 + dm1 + dm2, axis=1)
    gw = {k: gw[k] for k in wts}
    gp = {k: jnp.stack(v) for k, v in gp.items()}
    gp["final_g"] = d_final_g[0]
    return loss[0, 0], dxs[:s], jnp.stack(dmods), gw, gp


def _layer_params(prm, l):
    d = prm["norm_g"].shape[-1]
    return {
        "g": [prm["norm_g"][l, i].reshape(1, d) for i in range(3)],
        "b_gate": prm["b_gate"][l].reshape(1, -1),
        "rpb": prm["rpb"][l],
        "ln_v_g": prm["ln_v_g"][l].reshape(1, -1),
        "ln_v_b": prm["ln_v_b"][l].reshape(1, -1),
        "w_s": prm["w_s"][l],
        "b_s": prm["b_s"][l][..., None],
    }


SMALL = ("norm_g", "b_gate", "rpb", "ln_v_g", "ln_v_b", "w_s", "b_s", "final_g")
PACK_LANES = 1024


def _pack(parts):
    flat = jnp.concatenate([p.reshape(-1) for p in parts])
    rows = -(-flat.shape[0] // PACK_LANES)
    rows = -(-rows // 8) * 8
    return jnp.pad(flat, (0, rows * PACK_LANES - flat.shape[0])).reshape(rows, PACK_LANES)


def _unpack(flat, shapes):
    out, off = [], 0
    for shp in shapes:
        n = int(np.prod(shp))
        out.append(flat[..., off:off + n].reshape(*flat.shape[:-1], *shp))
        off += n
    return out


def kernel(x, c, ctx, c_ctx, w_ada, b_ada, norm_g, w_ff1_up, w_ff1_down, w_in, b_gate, rpb, ln_v_g, ln_v_b, w_s, b_s, w_pa, w_pb, w_o, w_ff2_up, w_ff2_down, final_g, loss_target, m_c_ctx, m_w_ada, m_b_ada, m_norm_g, m_w_ff1_up, m_w_ff1_down, m_w_in, m_b_gate, m_rpb, m_ln_v_g, m_ln_v_b, m_w_s, m_b_s, m_w_pa, m_w_pb, m_w_o, m_w_ff2_up, m_w_ff2_down, m_final_g, v_c_ctx, v_w_ada, v_b_ada, v_norm_g, v_w_ff1_up, v_w_ff1_down, v_w_in, v_b_gate, v_rpb, v_ln_v_g, v_ln_v_b, v_w_s, v_b_s, v_w_pa, v_w_pb, v_w_o, v_w_ff2_up, v_w_ff2_down, v_final_g):
    weights = dict(c_ctx=c_ctx, w_ada=w_ada, b_ada=b_ada, norm_g=norm_g, w_ff1_up=w_ff1_up, w_ff1_down=w_ff1_down, w_in=w_in,
                   b_gate=b_gate, rpb=rpb, ln_v_g=ln_v_g, ln_v_b=ln_v_b, w_s=w_s, b_s=b_s, w_pa=w_pa, w_pb=w_pb, w_o=w_o,
                   w_ff2_up=w_ff2_up, w_ff2_down=w_ff2_down, final_g=final_g)
    mom_m = dict(c_ctx=m_c_ctx, w_ada=m_w_ada, b_ada=m_b_ada, norm_g=m_norm_g, w_ff1_up=m_w_ff1_up, w_ff1_down=m_w_ff1_down,
                 w_in=m_w_in, b_gate=m_b_gate, rpb=m_rpb, ln_v_g=m_ln_v_g, ln_v_b=m_ln_v_b, w_s=m_w_s, b_s=m_b_s, w_pa=m_w_pa,
                 w_pb=m_w_pb, w_o=m_w_o, w_ff2_up=m_w_ff2_up, w_ff2_down=m_w_ff2_down, final_g=m_final_g)
    mom_v = dict(c_ctx=v_c_ctx, w_ada=v_w_ada, b_ada=v_b_ada, norm_g=v_norm_g, w_ff1_up=v_w_ff1_up, w_ff1_down=v_w_ff1_down,
                 w_in=v_w_in, b_gate=v_b_gate, rpb=v_rpb, ln_v_g=v_ln_v_g, ln_v_b=v_ln_v_b, w_s=v_w_s, b_s=v_b_s, w_pa=v_w_pa,
                 w_pb=v_w_pb, w_o=v_w_o, w_ff2_up=v_w_ff2_up, w_ff2_down=v_w_ff2_down, final_g=v_final_g)
    order = list(weights)
    mx, my, mc = _here()
    dev = 4 * mx + 2 * my + mc
    chip = 2 * mx + my
    depth, d, n_ada = w_ada.shape
    dq = d // 4

    c_all = _allgather8(jnp.pad(c, ((0, 7), (0, 0))), "gather_c")[:, 0, :]
    cond = jnp.concatenate([c_all, c_ctx[None, :], jnp.zeros((7, d), F32)], axis=0)
    b_shard = lax.dynamic_slice(b_ada, (0, chip * n_ada), (depth, n_ada))
    proj = [_ada_fwd(cond, w_ada[l], b_shard[l:l + 1], f"ada{l}") for l in range(depth)]
    silu_c = proj[0][1]
    mods_sh = _allgather8(jnp.concatenate([p[0] for p in proj], axis=0), "gather_mods")
    mods_all = jnp.transpose(mods_sh[0::2].reshape(4, depth, 16, n_ada), (1, 2, 0, 3)).reshape(depth, 16, N_MOD, d)
    mods = jnp.stack([lax.dynamic_index_in_dim(mods_all, dev, axis=1, keepdims=False), mods_all[:, 8]], axis=1)

    full = _gather_chips([weights[k].astype(MXU) for k in BIG], [SHARD_AXIS[k] for k in BIG], "gather_w")
    wts = dict(zip(BIG, full))
    prm = {k: weights[k] for k in SMALL if k != "norm_g"}
    norm_full = _allgather8(jnp.pad(norm_g.reshape(depth * 3, dq), ((0, 8 - depth * 3), (0, 0))), "gather_norm_g")
    prm["norm_g"] = jnp.transpose(norm_full[0::2, :depth * 3].reshape(4, depth, 3, dq), (1, 2, 0, 3)).reshape(depth, 3, d)

    loss, grad_x, dmods, gw, gp = _local_step(x[0], ctx[0], loss_target[0], mods, wts, prm)
    loss = lax.psum(loss, ("x", "y", "c"))

    small_shapes = [(depth, 2, N_MOD * d)] + [weights[k].shape if k != "norm_g" else (depth, 3, d) for k in SMALL]
    packed = _allgather8(_pack([dmods.reshape(depth, 2, N_MOD * d)] + [gp[k] for k in SMALL]), "gather_small")
    rows = packed.shape[1]
    total = _sum_pieces([packed[i] for i in range(8)], "sum_small")[0].reshape(-1)
    sums = dict(zip(("dmods",) + SMALL, _unpack(total, small_shapes)))
    dmods_dev = _unpack(packed.reshape(8, rows * PACK_LANES), small_shapes[:1])[0]

    g_ada, cc_parts = [], []
    for l in range(depth):
        dm = jnp.concatenate([dmods_dev[:, l, 0], sums["dmods"][l, 1][None], jnp.zeros((7, N_MOD * d), F32)], axis=0)
        dm_sh = lax.dynamic_slice(dm, (0, chip * n_ada), (16, n_ada))
        g_ada.append(_mm(silu_c, dm_sh, "tn", F32, f"ada{l}_dw"))
        cc_parts.append(_mm(dm_sh, w_ada[l], "nt", F32, f"ada{l}_dc")[8:9])
    cc_all = _allgather8(jnp.pad(jnp.concatenate(cc_parts, axis=0), ((0, 8 - depth), (0, 0))), "gather_cctx")
    g_cctx = _cctx_grad(cc_all[0::2, :depth].reshape(4 * depth, d), c_ctx.reshape(1, d), "cctx_grad")

    axes = [SHARD_AXIS[k] for k in BIG]
    from_sibling = _swap_layers([gw[k] for k in BIG], "swap_layer_gw")
    my_layer = jnp.reshape(mc, (1,)).astype(jnp.int32)
    pair = [_pair_sum(gw[k], r, my_layer, "pair_" + k) for k, r in zip(BIG, from_sibling)]
    recv = _scatter_chips([p[1] for p in pair], axes, "scatter_gw")
    mine = []
    for k, ax, p, r in zip(BIG, axes, pair, recv):
        size = p[0].shape[ax] // 4
        own = lax.dynamic_slice_in_dim(p[0], chip * size, size, axis=ax)
        mine.append(_sum_pieces([own, r[0], r[1], r[2]], "sum_" + k)[0])
    other = _sibling_swap(mine, "swap_gw")

    pieces = {k: [jnp.concatenate([jnp.where(mc == 0, a, b), jnp.where(mc == 0, b, a)], axis=0)] for k, a, b in zip(BIG, mine, other)}
    pieces["w_ada"] = [jnp.stack(g_ada)]
    pieces["b_ada"] = [sums["dmods"][:, 0], sums["dmods"][:, 1]]
    pieces["c_ctx"] = [g_cctx[0]]
    for k in SMALL:
        pieces[k] = [sums[k]]
    pieces["norm_g"] = [lax.dynamic_slice_in_dim(sums["norm_g"], chip * dq, dq, axis=2)]
    res = {k: _adamw(weights[k], pieces[k], mom_m[k], mom_v[k], "adamw_" + k) for k in order}
    return (loss, grad_x[None], *[res[k][0] for k in order], *[res[k][1] for k in order],
            *[res[k][2] for k in order], *[res[k][3] for k in order])
```

```python
import numpy as np
import jax
import jax.numpy as jnp
from jax import lax
from jax.experimental import pallas as pl
from jax.experimental.pallas import tpu as pltpu

F32 = jnp.float32
MXU = jnp.bfloat16
EPS = 1e-6
GRID_W, HEADS, HEAD_DIM = 64, 8, 64
NA_WIDTH = SG_WIDTH = 512
WIN_H, WIN_W = 8, 16
SG_CHUNK, SG_GROUPS = 128, 4
N_MOD = 9
ROPE_THETA = 10000.0
Q_ROWS, K_ROWS = 4, 12
TQ, TK = Q_ROWS * GRID_W, K_ROWS * GRID_W
TM = 256
LANES = 128
NEG = -1e30
VMEM_LIMIT = 56 * 2 ** 20
ADAM_LR, ADAM_B1, ADAM_B2, ADAM_EPS, ADAM_WD, ADAM_STEP = 0.001, 0.9, 0.999, 1e-08, 0.01, 10
MESH = pl.DeviceIdType.MESH
BIG = ("w_ff1_up", "w_ff1_down", "w_in", "w_pa", "w_pb", "w_o", "w_ff2_up", "w_ff2_down")
SHARD_AXIS = {"w_ff1_up": 2, "w_ff1_down": 1, "w_in": 2, "w_pa": 2, "w_pb": 2, "w_o": 1, "w_ff2_up": 2, "w_ff2_down": 1}


def _call(body, *, name, grid, in_specs, out_specs, out_shape, scratch=(), aliases=None):
    return pl.pallas_call(
        body, name=name, grid=grid, in_specs=in_specs, out_specs=out_specs, out_shape=out_shape,
        scratch_shapes=list(scratch), input_output_aliases=aliases or {},
        compiler_params=pltpu.CompilerParams(dimension_semantics=("arbitrary",) * len(grid), vmem_limit_bytes=VMEM_LIMIT))


def _w_dims(w):
    return w[0].shape[1:] if isinstance(w, tuple) else w.shape


def _w_arr(w):
    return w[0] if isinstance(w, tuple) else w


def _w_spec(w, block, index):
    if isinstance(w, tuple):
        layer = w[1]
        return pl.BlockSpec((None, *block), lambda *ids: (layer, *index(*ids)))
    return pl.BlockSpec(block, index)


def _pick(n, prefs):
    for p in prefs:
        if n % p == 0:
            return p
    return n


def _row_tile(t):
    return _pick(t, (640, 256))


def _rows(tm, n, col=0):
    return pl.BlockSpec((tm, n), lambda i: (i, col))


def _fixed(shape):
    return pl.BlockSpec(shape, lambda *_: (0,) * len(shape))


def _sds(shape, dtype):
    return jax.ShapeDtypeStruct(shape, dtype)


def _mm(a, b, mode, out_dtype, name, into=None):
    if mode == "tn":
        r, m = a.shape
        n = b.shape[1]
        tm = _pick(m, (1024, 1408, 704, 512, 256, 128))
        tn = _pick(n, (512, 1408, 256, 128))
        tr = _pick(r, (1280, 640, 512, 256, 128))

        def body(a_ref, b_ref, *rest):
            o_ref = rest[-1]

            @pl.when(pl.program_id(2) == 0)
            def _():
                o_ref[...] = jnp.zeros_like(o_ref)

            o_ref[...] += lax.dot_general(a_ref[...].astype(MXU), b_ref[...].astype(MXU), (((0,), (0,)), ((), ())),
                                          preferred_element_type=F32)

        in_specs = [pl.BlockSpec((tr, tm), lambda i, j, k: (k, i)), pl.BlockSpec((tr, tn), lambda i, j, k: (k, j))]
        if into is None:
            return _call(body, name=name, grid=(m // tm, n // tn, r // tr), in_specs=in_specs,
                         out_specs=pl.BlockSpec((tm, tn), lambda i, j, k: (i, j)), out_shape=_sds((m, n), F32))(a, b)
        buf, shape, layer, col0 = into
        out_spec = pl.BlockSpec((None, tm, tn), lambda i, j, k: (layer, i, j + col0 // tn))
        if buf is None:
            return _call(body, name=name, grid=(m // tm, n // tn, r // tr), in_specs=in_specs, out_specs=out_spec,
                         out_shape=_sds(shape, F32))(a, b)
        return _call(body, name=name, grid=(m // tm, n // tn, r // tr), in_specs=in_specs + [pl.BlockSpec(memory_space=pl.ANY)],
                     out_specs=out_spec, out_shape=_sds(shape, F32), aliases={2: 0})(a, b, buf)
    m, k = a.shape
    n = _w_dims(b)[1] if mode == "nn" else _w_dims(b)[0]
    tm = _pick(m, (1280, 640, 512, 256, 128) if k <= 2816 else (640, 512, 256, 128))
    tn = _pick(n, (512, 1408, 256, 128))
    dims = (((1,), (0,)), ((), ())) if mode == "nn" else (((1,), (1,)), ((), ()))

    def body(a_ref, b_ref, o_ref):
        o_ref[...] = lax.dot_general(a_ref[...].astype(MXU), b_ref[...].astype(MXU), dims,
                                     preferred_element_type=F32).astype(o_ref.dtype)

    b_spec = _w_spec(b, (k, tn), lambda i, j: (0, j)) if mode == "nn" else _w_spec(b, (tn, k), lambda i, j: (j, 0))
    return _call(body, name=name, grid=(m // tm, n // tn), in_specs=[pl.BlockSpec((tm, k), lambda i, j: (i, 0)), b_spec],
                 out_specs=pl.BlockSpec((tm, tn), lambda i, j: (i, j)), out_shape=_sds((m, n), out_dtype))(a, _w_arr(b))


def _row_chunks(tm):
    rc = _pick(tm, (256, 128))
    return [slice(r, r + rc) for r in range(0, tm, rc)]


def _ctx_rows(i, tm, s):
    return (i * tm + lax.broadcasted_iota(jnp.int32, (tm, 1), 0)) >= s


def _mod_row(m_ref, k, ctx):
    return jnp.where(ctx, m_ref[1, k:k + 1, :], m_ref[0, k:k + 1, :])


def _stream_sums(i, tm, s, refs_and_vals):
    @pl.when((i + 1) * tm <= s)
    def _():
        for ref, val in refs_and_vals:
            ref[0] += jnp.sum(val, axis=0, keepdims=True)

    @pl.when((i + 1) * tm > s)
    def _():
        ctx = _ctx_rows(i, tm, s)
        for ref, val in refs_and_vals:
            ref[0] += jnp.sum(jnp.where(ctx, 0.0, val), axis=0, keepdims=True)
            ref[1] += jnp.sum(jnp.where(ctx, val, 0.0), axis=0, keepdims=True)


def _norm_mm(xs, g, mods, k_shift, k_scale, w, s, name, glu):
    t, d = xs.shape
    n = _w_dims(w)[1] // 2 if glu else _w_dims(w)[1]
    tm = _pick(t, (1280, 640, 256))
    tn = _pick(n, (256, 128)) if glu else _pick(n, (512, 256, 128))
    nj = n // tn

    def body(x_ref, g_ref, m_ref, *refs):
        i, j = pl.program_id(0), pl.program_id(1)
        w_refs, h_ref, o_refs = refs[:2 if glu else 1], refs[2 if glu else 1], refs[3 if glu else 2:]

        @pl.when(j == 0)
        def _():
            x = x_ref[...]
            rstd = lax.rsqrt(jnp.mean(x * x, axis=-1, keepdims=True) + EPS)
            ctx = _ctx_rows(i, tm, s)
            h = x * rstd * g_ref[...] * (1.0 + _mod_row(m_ref, k_scale, ctx)) + _mod_row(m_ref, k_shift, ctx)
            h_ref[...] = h.astype(h_ref.dtype)

        for rows in _row_chunks(tm):
            h = h_ref[rows, :]
            a = _nn(h, w_refs[0][...])
            o_refs[0][rows, :] = a.astype(o_refs[0].dtype)
            if glu:
                b = _nn(h, w_refs[1][...])
                o_refs[1][rows, :] = b.astype(o_refs[1].dtype)
                o_refs[2][rows, :] = (a * jax.nn.sigmoid(a) * b).astype(o_refs[2].dtype)

    tile = pl.BlockSpec((tm, tn), lambda i, j: (i, j))
    row = pl.BlockSpec((tm, d), lambda i, j: (i, 0))
    w_specs = [_w_spec(w, (d, tn), lambda i, j: (0, j))] + ([_w_spec(w, (d, tn), lambda i, j: (0, j + nj))] if glu else [])
    n_out = 3 if glu else 1
    return _call(body, name=name, grid=(t // tm, nj),
                 in_specs=[row, _fixed((1, d)), _fixed((2, N_MOD, d))] + w_specs,
                 out_specs=[row] + [tile] * n_out,
                 out_shape=[_sds((t, d), MXU)] + [_sds((t, n), MXU)] * n_out)(xs, g, mods, *([_w_arr(w)] * (2 if glu else 1)))


def _mm_res(a, w, xs, mods, k_gate, coef, s, name):
    t, k = a.shape
    d = _w_dims(w)[1]
    tm = _pick(t, (1280, 640, 256))
    tn = _pick(d, (512, 256, 128))

    def body(a_ref, w_ref, x_ref, m_ref, y_ref, o_ref):
        y = _nn(a_ref[...], w_ref[...])
        y_ref[...] = y.astype(y_ref.dtype)
        gate = _mod_row(m_ref, k_gate, _ctx_rows(pl.program_id(0), tm, s))
        o_ref[...] = x_ref[...] + (coef * gate) * y

    tile = pl.BlockSpec((tm, tn), lambda i, j: (i, j))
    return _call(body, name=name, grid=(t // tm, d // tn),
                 in_specs=[pl.BlockSpec((tm, k), lambda i, j: (i, 0)), _w_spec(w, (k, tn), lambda i, j: (0, j)), tile,
                           pl.BlockSpec((2, N_MOD, tn), lambda i, j: (0, 0, j))],
                 out_specs=[tile, tile], out_shape=[_sds((t, d), MXU), _sds((t, d), F32)])(a, _w_arr(w), xs, mods)


def _resb_mm(dxn, y, mods, k_gate, coef, w, s, name, ups=None):
    t, d = dxn.shape
    n = _w_dims(w)[0]
    tm = _pick(t, (1280, 640, 256))
    tn = _pick(n, (256, 128)) if ups else _pick(n, (512, 256, 128))

    def body(dx_ref, y_ref, m_ref, w_ref, *refs):
        i, j = pl.program_id(0), pl.program_id(1)
        u_refs, (dy_ref, dgt_ref), o_refs = (refs[:2], refs[2:4], refs[4:]) if ups else ((), refs[:2], refs[2:])

        @pl.when((i == 0) & (j == 0))
        def _():
            dgt_ref[...] = jnp.zeros_like(dgt_ref)

        @pl.when(j == 0)
        def _():
            dx = dx_ref[...]
            gate = _mod_row(m_ref, k_gate, _ctx_rows(i, tm, s))
            dy_ref[...] = ((coef * gate) * dx).astype(dy_ref.dtype)
            _stream_sums(i, tm, s, [(dgt_ref, coef * y_ref[...].astype(F32) * dx)])

        dact = _nt(dy_ref[...], w_ref[...])
        if ups:
            a, b = u_refs[0][...].astype(F32), u_refs[1][...].astype(F32)
            sg = jax.nn.sigmoid(a)
            o_refs[0][...] = (dact * b * sg * (1.0 + a * (1.0 - sg))).astype(o_refs[0].dtype)
            o_refs[1][...] = (dact * a * sg).astype(o_refs[1].dtype)
        else:
            o_refs[0][...] = dact.astype(o_refs[0].dtype)

    tile = pl.BlockSpec((tm, tn), lambda i, j: (i, j))
    row = pl.BlockSpec((tm, d), lambda i, j: (i, 0))
    n_out = 2 if ups else 1
    return _call(body, name=name, grid=(t // tm, n // tn),
                 in_specs=[row, row, _fixed((2, N_MOD, d)), _w_spec(w, (tn, d), lambda i, j: (j, 0))] + ([tile, tile] if ups else []),
                 out_specs=[row, _fixed((2, 1, d))] + [tile] * n_out,
                 out_shape=[_sds((t, d), MXU), _sds((2, 1, d), F32)] + [_sds((t, n), MXU)] * n_out)(
                     dxn, y, mods, _w_arr(w), *(ups or ()))


def _mm_normb(a_list, w, xs, dres, g, mods, k_scale, s, name):
    t, d = xs.shape
    ka = a_list[0].shape[1]
    tm = _pick(t, (640, 256))
    tk = _pick(ka, (1408, 1536, 1024, 512, 256, 128))
    nk1 = ka // tk
    n_a = len(a_list)
    nk = nk1 * n_a

    def body(*refs):
        a_refs, (w_ref, x_ref, dr_ref, g_ref, m_ref, dx_ref, dsh_ref, dsc_ref, dg_ref, acc) = refs[:n_a], refs[n_a:]
        i, k = pl.program_id(0), pl.program_id(1)

        @pl.when((i == 0) & (k == 0))
        def _():
            dsh_ref[...] = jnp.zeros_like(dsh_ref)
            dsc_ref[...] = jnp.zeros_like(dsc_ref)
            dg_ref[...] = jnp.zeros_like(dg_ref)

        @pl.when(k == 0)
        def _():
            acc[...] = jnp.zeros_like(acc)

        for q in range(n_a):
            @pl.when((k >= q * nk1) & (k < (q + 1) * nk1))
            def _():
                acc[...] += _nt(a_refs[q][...], w_ref[...])

        @pl.when(k == nk - 1)
        def _():
            x = x_ref[...]
            dh = acc[...]
            rstd = lax.rsqrt(jnp.mean(x * x, axis=-1, keepdims=True) + EPS)
            xhat = x * rstd
            gg = g_ref[...]
            _stream_sums(i, tm, s, [(dsh_ref, dh), (dsc_ref, dh * (xhat * gg))])
            dy = dh * (1.0 + _mod_row(m_ref, k_scale, _ctx_rows(i, tm, s)))
            dg_ref[...] += jnp.sum(dy * xhat, axis=0, keepdims=True)
            dxh = dy * gg
            dx_ref[...] = dr_ref[...] + rstd * (dxh - xhat * jnp.mean(dxh * xhat, axis=-1, keepdims=True))

    row = pl.BlockSpec((tm, d), lambda i, k: (i, 0))
    a_specs = [pl.BlockSpec((tm, tk), lambda i, k, q=q: (i, jnp.clip(k - q * nk1, 0, nk1 - 1))) for q in range(n_a)]
    return _call(body, name=name, grid=(t // tm, nk),
                 in_specs=a_specs + [_w_spec(w, (d, tk), lambda i, k: (0, k)), row, row, _fixed((1, d)), _fixed((2, N_MOD, d))],
                 out_specs=[row, _fixed((2, 1, d)), _fixed((2, 1, d)), _fixed((1, d))],
                 out_shape=[_sds((t, d), F32), _sds((2, 1, d), F32), _sds((2, 1, d), F32), _sds((1, d), F32)],
                 scratch=[pltpu.VMEM((tm, d), F32)])(*a_list, _w_arr(w), xs, dres, g, mods)


def _rope_tables(s, ctx_len):
    n_freq = HEAD_DIM // 4
    tok = jnp.arange(s)
    freqs = ROPE_THETA ** (-jnp.arange(n_freq, dtype=F32) / n_freq)
    ang = jnp.concatenate([(tok // GRID_W).astype(F32)[:, None] * freqs, (tok % GRID_W).astype(F32)[:, None] * freqs], axis=-1)
    cos = jnp.repeat(jnp.cos(ang), 2, axis=-1)
    sin = jnp.repeat(jnp.sin(ang), 2, axis=-1) * jnp.tile(jnp.array([-1.0, 1.0], F32), HEAD_DIM // 2)
    cos = jnp.concatenate([jnp.tile(cos, (1, LANES // HEAD_DIM)), jnp.ones((ctx_len, LANES), F32)], axis=0)
    sin = jnp.concatenate([jnp.tile(sin, (1, LANES // HEAD_DIM)), jnp.zeros((ctx_len, LANES), F32)], axis=0)
    return cos, sin


def _swap_pairs(x):
    n = x.shape[-1]
    lane = lax.broadcasted_iota(jnp.int32, x.shape, 1)
    return jnp.where(lane % 2 == 0, pltpu.roll(x, n - 1, 1), pltpu.roll(x, 1, 1))


def _rope(p, cos, sin, name):
    t = p.shape[0]
    w = NA_WIDTH
    te = _row_tile(t)

    def body(q_ref, k_ref, v_ref, c_ref, s_ref, qo_ref, ko_ref, vo_ref):
        c, s = c_ref[...], s_ref[...]
        for hp in range(w // LANES):
            cols = slice(hp * LANES, (hp + 1) * LANES)
            q, k = q_ref[:, cols].astype(F32), k_ref[:, cols].astype(F32)
            qo_ref[:, cols] = (q * c + _swap_pairs(q) * s).astype(qo_ref.dtype)
            ko_ref[:, cols] = (k * c + _swap_pairs(k) * s).astype(ko_ref.dtype)
        vo_ref[...] = v_ref[...].astype(vo_ref.dtype)

    return _call(body, name=name, grid=(t // te,),
                 in_specs=[_rows(te, w, 0), _rows(te, w, 1), _rows(te, w, 2), _rows(te, LANES), _rows(te, LANES)],
                 out_specs=[_rows(te, w)] * 3, out_shape=[_sds((t, w), MXU)] * 3)(p, p, p, cos, sin)


def _rope_bwd(dq, dk, dv, cos, sin, name):
    t = dq.shape[0]
    te = _row_tile(t)
    n_pairs = NA_WIDTH // LANES

    def body(dq_ref, dk_ref, dv_ref, c_ref, s_ref, qo_ref, ko_ref, vo_ref):
        c, s = c_ref[...], s_ref[...]
        for hp in range(n_pairs):
            cols = slice(hp * LANES, (hp + 1) * LANES)
            a, b = dq_ref[:, cols], dk_ref[hp]
            qo_ref[:, cols] = (a * c + _swap_pairs(a * s)).astype(qo_ref.dtype)
            ko_ref[:, cols] = (b * c + _swap_pairs(b * s)).astype(ko_ref.dtype)
            vo_ref[:, cols] = dv_ref[hp].astype(vo_ref.dtype)

    pairs = pl.BlockSpec((n_pairs, te, LANES), lambda i: (0, i, 0))
    return _call(body, name=name, grid=(t // te,),
                 in_specs=[_rows(te, NA_WIDTH), pairs, pairs, _rows(te, LANES), _rows(te, LANES)],
                 out_specs=[_rows(te, NA_WIDTH)] * 3, out_shape=[_sds((t, NA_WIDTH), MXU)] * 3)(dq, dk, dv, cos, sin)


def _na_geometry(r_grid):
    rows = []
    for r0, ks in ((0, 0), (Q_ROWS, 0), (r_grid - Q_ROWS, r_grid - K_ROWS)):
        dr = np.zeros((Q_ROWS, K_ROWS), np.int32)
        vr = np.zeros((Q_ROWS, K_ROWS), bool)
        for a in range(Q_ROWS):
            r = r0 + a
            rs = min(max(r - WIN_H // 2, 0), r_grid - WIN_H)
            for i in range(K_ROWS):
                kr = ks + i
                vr[a, i] = rs <= kr <= rs + WIN_H - 1
                dr[a, i] = kr - r + WIN_H - 1
        rows.append((dr, vr))
    c = np.arange(GRID_W)
    cs = np.clip(c - WIN_W // 2, 0, GRID_W - WIN_W)
    kc = np.arange(GRID_W)
    vc = (kc[None, :] >= cs[:, None]) & (kc[None, :] <= cs[:, None] + WIN_W - 1)
    dc = kc[None, :] - c[:, None] + WIN_W - 1
    return rows, dc, vc


def _bias_table(rpb, r_grid):
    rows, _, vc = _na_geometry(r_grid)
    n_dc, off = 2 * WIN_W - 1, GRID_W - WIN_W
    u = jnp.pad(rpb, ((0, 0), (0, 0), (off, 2 * GRID_W - 1 - off - n_dc)))
    toep = jnp.stack([u[:, :, GRID_W - 1 - c:2 * GRID_W - 1 - c] for c in range(GRID_W)], axis=1)
    toep = jnp.pad(toep, ((0, 0), (0, 0), (Q_ROWS, Q_ROWS), (0, 0)))
    tabs = []
    for dr, vr in rows:
        per_row = []
        for a in range(Q_ROWS):
            lo = int(dr[a, 0]) + Q_ROWS
            valid = vc[:, None, :] & vr[a][None, :, None]
            per_row.append(jnp.where(valid[None], toep[:, :, lo:lo + K_ROWS, :], NEG))
        tabs.append(jnp.stack(per_row, axis=1).reshape(HEADS, TQ, TK))
    tabs.append(jnp.full((HEADS, TQ, TK), NEG, F32))
    return jnp.stack(tabs)


def _variant(g, ngx):
    return jnp.where(g == 0, 0, jnp.where(g >= ngx, 3, jnp.where(g == ngx - 1, 2, 1)))


def _key_start(g, r_grid):
    return pl.multiple_of(jnp.clip(g * Q_ROWS - WIN_H // 2, 0, r_grid - K_ROWS) * GRID_W, TQ)


def _nt(a, b):
    return lax.dot_general(a, b, (((1,), (1,)), ((), ())), preferred_element_type=F32)


def _tn(a, b):
    return lax.dot_general(a, b, (((0,), (0,)), ((), ())), preferred_element_type=F32)


def _nn(a, b):
    return jnp.dot(a, b, preferred_element_type=F32)


def _head_mask(h):
    lane = lax.broadcasted_iota(jnp.int32, (1, LANES), 1)
    return ((lane >= HEAD_DIM * h) & (lane < HEAD_DIM * (h + 1))).astype(F32)


def _softmax_parts(qm, knb, kcx, bias):
    s_nb = _nt(qm, knb) + bias
    s_cx = _nt(qm, kcx)
    m = jnp.maximum(jnp.max(s_nb, axis=-1, keepdims=True), jnp.max(s_cx, axis=-1, keepdims=True))
    e_nb = jnp.exp(s_nb - m)
    e_cx = jnp.exp(s_cx - m)
    inv = 1.0 / (jnp.sum(e_nb, axis=-1, keepdims=True) + jnp.sum(e_cx, axis=-1, keepdims=True))
    return e_nb, e_cx, inv


def _na_specs(t, ngx):
    q_spec = pl.BlockSpec((TQ, LANES), lambda hp, g: (g, hp))
    kv_spec = pl.BlockSpec((t, LANES), lambda hp, g: (0, hp))
    b_spec = pl.BlockSpec((1, 2, TQ, TK), lambda hp, g: (_variant(g, ngx), hp, 0, 0))
    return q_spec, kv_spec, b_spec


def _na_fwd(q, k, v, bias, s, name):
    t = q.shape[0]
    ctx_len = t - s
    r_grid = s // GRID_W
    q_spec, kv_spec, b_spec = _na_specs(t, s // TQ)

    def body(q_ref, k_ref, v_ref, b_ref, o_ref):
        start = _key_start(pl.program_id(1), r_grid)
        qf = q_ref[...].astype(F32) * (HEAD_DIM ** -0.5)
        knb, vnb = k_ref[pl.ds(start, TK), :], v_ref[pl.ds(start, TK), :]
        kcx, vcx = k_ref[pl.ds(s, ctx_len), :], v_ref[pl.ds(s, ctx_len), :]
        acc = jnp.zeros((TQ, LANES), F32)
        for h in range(2):
            mask = _head_mask(h)
            e_nb, e_cx, inv = _softmax_parts((qf * mask).astype(MXU), knb, kcx, b_ref[0, h])
            acc += (_nn(e_nb.astype(MXU), vnb) + _nn(e_cx.astype(MXU), vcx)) * (inv * mask)
        o_ref[...] = acc.astype(o_ref.dtype)

    return _call(body, name=name, grid=(NA_WIDTH // LANES, t // TQ), in_specs=[q_spec, kv_spec, kv_spec, b_spec],
                 out_specs=q_spec, out_shape=_sds((t, NA_WIDTH), MXU))(q, k, v, bias)


def _na_bwd(q, k, v, do, bias, s, name):
    t = q.shape[0]
    ctx_len = t - s
    r_grid = s // GRID_W
    ng, ngx = t // TQ, s // TQ
    q_spec, kv_spec, b_spec = _na_specs(t, ngx)

    def body(q_ref, k_ref, v_ref, do_ref, b_ref, dq_ref, dk_hbm, dv_hbm, db_ref, dk_acc, dv_acc):
        hp, g = pl.program_id(0), pl.program_id(1)
        start = _key_start(g, r_grid)

        @pl.when(g == 0)
        def _():
            dk_acc[...] = jnp.zeros_like(dk_acc)
            dv_acc[...] = jnp.zeros_like(dv_acc)

        @pl.when((g == 0) | (g == 1) | (g == ngx - 1) | (g == ngx))
        def _():
            db_ref[...] = jnp.zeros_like(db_ref)

        qf = q_ref[...].astype(F32) * (HEAD_DIM ** -0.5)
        do = do_ref[...].astype(F32)
        knb, vnb = k_ref[pl.ds(start, TK), :], v_ref[pl.ds(start, TK), :]
        kcx, vcx = k_ref[pl.ds(s, ctx_len), :], v_ref[pl.ds(s, ctx_len), :]
        dq = jnp.zeros((TQ, LANES), F32)
        dk_nb = jnp.zeros((TK, LANES), F32)
        dv_nb = jnp.zeros((TK, LANES), F32)
        dk_cx = jnp.zeros((ctx_len, LANES), F32)
        dv_cx = jnp.zeros((ctx_len, LANES), F32)
        for h in range(2):
            mask = _head_mask(h)
            qm = (qf * mask).astype(MXU)
            dom = (do * mask).astype(MXU)
            e_nb, e_cx, inv = _softmax_parts(qm, knb, kcx, b_ref[0, h])
            dp_nb = _nt(dom, vnb)
            dp_cx = _nt(dom, vcx)
            delta = inv * (jnp.sum(e_nb * dp_nb, axis=-1, keepdims=True) + jnp.sum(e_cx * dp_cx, axis=-1, keepdims=True))
            ds_nb = e_nb * (inv * (dp_nb - delta))
            ds_cx = e_cx * (inv * (dp_cx - delta))
            db_ref[0, h] += ds_nb
            ds_nb, ds_cx = ds_nb.astype(MXU), ds_cx.astype(MXU)
            dq += (_nn(ds_nb, knb) + _nn(ds_cx, kcx)) * (mask * (HEAD_DIM ** -0.5))
            dk_nb += _tn(ds_nb, qm)
            dk_cx += _tn(ds_cx, qm)
            dom = (do * (inv * mask)).astype(MXU)
            dv_nb += _tn(e_nb.astype(MXU), dom)
            dv_cx += _tn(e_cx.astype(MXU), dom)
        dq_ref[...] = dq
        dk_acc[pl.ds(start, TK), :] += dk_nb
        dv_acc[pl.ds(start, TK), :] += dv_nb
        dk_acc[pl.ds(s, ctx_len), :] += dk_cx
        dv_acc[pl.ds(s, ctx_len), :] += dv_cx

        @pl.when(g == ng - 1)
        def _():
            pltpu.sync_copy(dk_acc, dk_hbm.at[hp])
            pltpu.sync_copy(dv_acc, dv_hbm.at[hp])

    n_pairs = NA_WIDTH // LANES
    hbm = pl.BlockSpec(memory_space=pl.ANY)
    return _call(body, name=name, grid=(n_pairs, ng), in_specs=[q_spec, kv_spec, kv_spec, q_spec, b_spec],
                 out_specs=[q_spec, hbm, hbm, b_spec],
                 out_shape=[_sds((t, NA_WIDTH), F32), _sds((n_pairs, t, LANES), F32), _sds((n_pairs, t, LANES), F32),
                            _sds((4, HEADS, TQ, TK), F32)],
                 scratch=[pltpu.VMEM((t, LANES), F32), pltpu.VMEM((t, LANES), F32)])(q, k, v, do, bias)


def _rpb_grad(dbias, r_grid, name):
    rows, _, _ = _na_geometry(r_grid)
    n_dr, half, skew, lanes = 2 * WIN_H - 1, WIN_W - 1, TK + 2, 896
    z = dbias[:3].reshape(3, HEADS, Q_ROWS, GRID_W, TK)
    z = jnp.pad(z, ((0, 0),) * 4 + ((0, 1),)).reshape(3, HEADS, Q_ROWS, GRID_W * (TK + 1))
    z = jnp.pad(z, ((0, 0),) * 3 + ((0, GRID_W),)).reshape(3, HEADS, Q_ROWS, GRID_W, skew)
    z = jnp.pad(z, ((0, 0),) * 4 + ((0, lanes - skew),))

    def body(z_ref, o_ref):
        sums = [jnp.sum(z_ref[v, 0, a], axis=0, keepdims=True) for v in range(3) for a in range(Q_ROWS)]
        zs = jnp.concatenate(sums + [jnp.zeros((16 - 3 * Q_ROWS, lanes), F32)], axis=0)
        acc = [jnp.zeros((1, lanes), F32) for _ in range(n_dr)]
        for i in range(K_ROWS):
            if i == 0:
                at0 = pltpu.roll(zs, lanes - (skew - half), 1) + pltpu.roll(zs, half, 1)
            else:
                at0 = pltpu.roll(zs, lanes - (i * GRID_W - half), 1)
            for v, (dr, vr) in enumerate(rows):
                for a in range(Q_ROWS):
                    if vr[a, i]:
                        acc[dr[a, i]] = acc[dr[a, i]] + at0[v * Q_ROWS + a:v * Q_ROWS + a + 1, :]
        o_ref[0] = jnp.concatenate(acc + [jnp.zeros((1, lanes), F32)], axis=0)

    o = _call(body, name=name, grid=(HEADS,),
              in_specs=[pl.BlockSpec((3, 1, Q_ROWS, GRID_W, lanes), lambda h: (0, h, 0, 0, 0))],
              out_specs=pl.BlockSpec((1, 16, lanes), lambda h: (h, 0, 0)), out_shape=_sds((HEADS, 16, lanes), F32))(z)
    return o[:, :n_dr, :2 * WIN_W - 1]


_GELU_K, _GELU_C = 0.7978845608028654, 0.044715


def _gelu(x):
    return 0.5 * x * (1.0 + jnp.tanh(_GELU_K * (x + _GELU_C * x * x * x)))


def _gelu_grad(x):
    th = jnp.tanh(_GELU_K * (x + _GELU_C * x * x * x))
    return 0.5 * (1.0 + th) + 0.5 * x * (1.0 - th * th) * (_GELU_K * (1.0 + 3.0 * _GELU_C * x * x))


def _ln_stats(v):
    mu = jnp.mean(v, axis=-1, keepdims=True)
    vc = v - mu
    rstd = lax.rsqrt(jnp.mean(vc * vc, axis=-1, keepdims=True) + EPS)
    return vc * rstd, rstd


def _gmlp(p, ln_g, ln_b, w_s, b_s, name):
    t = p.shape[0]
    te = _row_tile(t)
    w = SG_WIDTH
    cw = w // SG_GROUPS

    def body(u_ref, v_ref, g_ref, b_ref, ws_ref, bs_ref, o_ref):
        xhat, _ = _ln_stats(_gelu(v_ref[...].astype(F32)))
        vn = (xhat * g_ref[...] + b_ref[...]).astype(MXU)
        ug = _gelu(u_ref[...].astype(F32))
        for ci in range(te // SG_CHUNK):
            rs = slice(ci * SG_CHUNK, (ci + 1) * SG_CHUNK)
            for gi in range(SG_GROUPS):
                cs = slice(gi * cw, (gi + 1) * cw)
                sg = _nn(ws_ref[gi].astype(MXU), vn[rs, cs]) + bs_ref[gi]
                o_ref[rs, cs] = (ug[rs, cs] * sg).astype(o_ref.dtype)

    return _call(body, name=name, grid=(t // te,),
                 in_specs=[_rows(te, w, 3), _rows(te, w, 4), _fixed((1, w)), _fixed((1, w)),
                           _fixed((SG_GROUPS, SG_CHUNK, SG_CHUNK)), _fixed((SG_GROUPS, SG_CHUNK, 1))],
                 out_specs=_rows(te, w), out_shape=_sds((t, w), MXU))(p, p, ln_g, ln_b, w_s, b_s)


def _gmlp_bwd(p, dob, ln_g, ln_b, w_s, b_s, name):
    t = p.shape[0]
    te = _row_tile(t)
    w = SG_WIDTH
    cw = w // SG_GROUPS

    def body(u_ref, v_ref, do_ref, g_ref, b_ref, ws_ref, bs_ref, du_ref, dv_ref, dws_ref, dbs_ref, dg_ref, db_ref, dvn_ref):
        @pl.when(pl.program_id(0) == 0)
        def _():
            dws_ref[...] = jnp.zeros_like(dws_ref)
            dbs_ref[...] = jnp.zeros_like(dbs_ref)
            dg_ref[...] = jnp.zeros_like(dg_ref)
            db_ref[...] = jnp.zeros_like(db_ref)

        u, v = u_ref[...].astype(F32), v_ref[...].astype(F32)
        xhat, rstd = _ln_stats(_gelu(v))
        vn = (xhat * g_ref[...] + b_ref[...]).astype(MXU)
        ug = _gelu(u)
        dob = do_ref[...].astype(F32)
        for ci in range(te // SG_CHUNK):
            rs = slice(ci * SG_CHUNK, (ci + 1) * SG_CHUNK)
            for gi in range(SG_GROUPS):
                cs = slice(gi * cw, (gi + 1) * cw)
                wsg = ws_ref[gi].astype(MXU)
                sg = _nn(wsg, vn[rs, cs]) + bs_ref[gi]
                du_ref[rs, cs] = (dob[rs, cs] * sg * _gelu_grad(u[rs, cs])).astype(du_ref.dtype)
                ds = dob[rs, cs] * ug[rs, cs]
                dbs_ref[gi] += jnp.sum(ds, axis=-1, keepdims=True)
                ds = ds.astype(MXU)
                dws_ref[gi] += _nt(ds, vn[rs, cs])
                dvn_ref[rs, cs] = _tn(wsg, ds)
        dvn = dvn_ref[...]
        dg_ref[...] += jnp.sum(dvn * xhat, axis=0, keepdims=True)
        db_ref[...] += jnp.sum(dvn, axis=0, keepdims=True)
        dxh = dvn * g_ref[...]
        dvg = rstd * (dxh - jnp.mean(dxh, axis=-1, keepdims=True) - xhat * jnp.mean(dxh * xhat, axis=-1, keepdims=True))
        dv_ref[...] = (dvg * _gelu_grad(v)).astype(dv_ref.dtype)

    return _call(body, name=name, grid=(t // te,),
                 in_specs=[_rows(te, w, 3), _rows(te, w, 4), _rows(te, w), _fixed((1, w)), _fixed((1, w)),
                           _fixed((SG_GROUPS, SG_CHUNK, SG_CHUNK)), _fixed((SG_GROUPS, SG_CHUNK, 1))],
                 out_specs=[_rows(te, w), _rows(te, w), _fixed((SG_GROUPS, SG_CHUNK, SG_CHUNK)),
                            _fixed((SG_GROUPS, SG_CHUNK, 1)), _fixed((1, w)), _fixed((1, w))],
                 out_shape=[_sds((t, w), MXU), _sds((t, w), MXU), _sds((SG_GROUPS, SG_CHUNK, SG_CHUNK), F32),
                            _sds((SG_GROUPS, SG_CHUNK, 1), F32), _sds((1, w), F32), _sds((1, w), F32)],
                 scratch=[pltpu.VMEM((te, w), F32)])(p, p, dob, ln_g, ln_b, w_s, b_s)


def _merge(pa, pb, p, b_gate, name):
    t, d = pa.shape
    te = _row_tile(t)
    hw = NA_WIDTH
    nh = d // hw
    c0 = (NA_WIDTH * 3 + SG_WIDTH * 2) // hw

    def body(pa_ref, pb_ref, la_ref, lb_ref, ba_ref, bb_ref, o_ref):
        ga = jax.nn.sigmoid(la_ref[...].astype(F32) + ba_ref[...])
        gb = jax.nn.sigmoid(lb_ref[...].astype(F32) + bb_ref[...])
        o_ref[...] = (ga * pa_ref[...].astype(F32) + gb * pb_ref[...].astype(F32)).astype(o_ref.dtype)

    tile = pl.BlockSpec((te, hw), lambda i, j: (i, j))
    return _call(body, name=name, grid=(t // te, nh),
                 in_specs=[tile, tile, pl.BlockSpec((te, hw), lambda i, j: (i, c0 + j)),
                           pl.BlockSpec((te, hw), lambda i, j: (i, c0 + nh + j)),
                           pl.BlockSpec((1, hw), lambda i, j: (0, j)), pl.BlockSpec((1, hw), lambda i, j: (0, nh + j))],
                 out_specs=tile, out_shape=_sds((t, d), MXU))(pa, pb, p, p, b_gate, b_gate)


def _merge_bwd(dmg, pa, pb, p, b_gate, name):
    t, d = pa.shape
    te = _row_tile(t)
    hw = NA_WIDTH
    nh = d // hw
    c0 = (NA_WIDTH * 3 + SG_WIDTH * 2) // hw

    def body(dm_ref, pa_ref, pb_ref, la_ref, lb_ref, ba_ref, bb_ref, dpa_ref, dpb_ref, dla_ref, dlb_ref, dba_ref, dbb_ref):
        @pl.when(pl.program_id(1) == 0)
        def _():
            dba_ref[...] = jnp.zeros_like(dba_ref)
            dbb_ref[...] = jnp.zeros_like(dbb_ref)

        dm = dm_ref[...].astype(F32)
        ga = jax.nn.sigmoid(la_ref[...].astype(F32) + ba_ref[...])
        gb = jax.nn.sigmoid(lb_ref[...].astype(F32) + bb_ref[...])
        dpa_ref[...] = (dm * ga).astype(dpa_ref.dtype)
        dpb_ref[...] = (dm * gb).astype(dpb_ref.dtype)
        dla = dm * pa_ref[...].astype(F32) * ga * (1.0 - ga)
        dlb = dm * pb_ref[...].astype(F32) * gb * (1.0 - gb)
        dla_ref[...] = dla.astype(dla_ref.dtype)
        dlb_ref[...] = dlb.astype(dlb_ref.dtype)
        dba_ref[...] += jnp.sum(dla, axis=0, keepdims=True)
        dbb_ref[...] += jnp.sum(dlb, axis=0, keepdims=True)

    tile = pl.BlockSpec((te, hw), lambda j, i: (i, j))
    bias_a = pl.BlockSpec((1, hw), lambda j, i: (0, j))
    bias_b = pl.BlockSpec((1, hw), lambda j, i: (0, nh + j))
    return _call(body, name=name, grid=(nh, t // te),
                 in_specs=[tile, tile, tile, pl.BlockSpec((te, hw), lambda j, i: (i, c0 + j)),
                           pl.BlockSpec((te, hw), lambda j, i: (i, c0 + nh + j)), bias_a, bias_b],
                 out_specs=[tile, tile, tile, tile, bias_a, bias_a],
                 out_shape=[_sds((t, d), MXU)] * 4 + [_sds((1, d), F32)] * 2)(dmg, pa, pb, p, p, b_gate, b_gate)


def _final(xs, tgt, g, name):
    t, d = xs.shape
    nx = tgt.shape[0] // TM

    def body(x_ref, t_ref, g_ref, l_ref, dx_ref, dg_ref):
        i = pl.program_id(0)

        @pl.when(i == 0)
        def _():
            l_ref[...] = jnp.zeros_like(l_ref)
            dg_ref[...] = jnp.zeros_like(dg_ref)

        @pl.when(i < nx)
        def _():
            x = x_ref[...]
            rstd = lax.rsqrt(jnp.mean(x * x, axis=-1, keepdims=True) + EPS)
            xhat = x * rstd
            err = xhat * g_ref[...] - t_ref[...]
            l_ref[...] += 0.5 * jnp.sum(jnp.mean(err * err, axis=-1, keepdims=True))
            dy = err * (1.0 / d)
            dg_ref[...] += jnp.sum(dy * xhat, axis=0, keepdims=True)
            dxh = dy * g_ref[...]
            dx_ref[...] = rstd * (dxh - xhat * jnp.mean(dxh * xhat, axis=-1, keepdims=True))

        @pl.when(i >= nx)
        def _():
            dx_ref[...] = jnp.zeros_like(dx_ref)

    return _call(body, name=name, grid=(t // TM,),
                 in_specs=[_rows(TM, d), pl.BlockSpec((TM, d), lambda i: (jnp.minimum(i, nx - 1), 0)), _fixed((1, d))],
                 out_specs=[_fixed((1, LANES)), _rows(TM, d), _fixed((1, d))],
                 out_shape=[_sds((1, LANES), F32), _sds((t, d), F32), _sds((1, d), F32)])(xs, tgt, g)


def _view2d(a):
    return a.reshape(1, -1) if a.ndim == 1 else a.reshape(-1, a.shape[-1])


def _tile_rows(r, c):
    for cand in (1024, 512, 256, 128, 64, 32, 16):
        if r % cand == 0 and cand * c * 4 <= 2 ** 20:
            return cand
    return r


def _pair_sum(g, recv, layer, name):
    _, a, b = g.shape
    tr = _tile_rows(a, b)

    def body(l_ref, g_ref, r_ref, o32_ref, o16_ref):
        acc = g_ref[...] + r_ref[...]
        o32_ref[...] = acc
        o16_ref[...] = acc.astype(o16_ref.dtype)

    first = pl.BlockSpec((None, tr, b), lambda i, l: (0, i, 0))
    return pl.pallas_call(
        body, name=name, out_shape=[_sds((1, a, b), F32), _sds((1, a, b), MXU)],
        grid_spec=pltpu.PrefetchScalarGridSpec(
            num_scalar_prefetch=1, grid=(a // tr,),
            in_specs=[pl.BlockSpec((None, tr, b), lambda i, l: (l[0], i, 0)), first], out_specs=[first, first]),
        compiler_params=pltpu.CompilerParams(dimension_semantics=("arbitrary",), vmem_limit_bytes=VMEM_LIMIT))(layer, g, recv)


def _ew(fn, arrays, out_dtypes, name):
    shape = arrays[0].shape
    views = [_view2d(a) for a in arrays]
    r, c = views[0].shape
    tr = _tile_rows(r, c)

    def body(*refs):
        outs = fn(*[ref[...] for ref in refs[:len(views)]])
        for ref, o in zip(refs[len(views):], outs):
            ref[...] = o.astype(ref.dtype)

    res = _call(body, name=name, grid=(r // tr,), in_specs=[_rows(tr, c)] * len(views), out_specs=[_rows(tr, c)] * len(out_dtypes),
                out_shape=[_sds((r, c), dt) for dt in out_dtypes])(*views)
    return [o.reshape(shape) for o in res]


def _sum_pieces(pieces, name, out_dtypes=(F32,)):
    def fn(*vals):
        acc = vals[0].astype(F32)
        for v in vals[1:]:
            acc = acc + v.astype(F32)
        return (acc,) * len(out_dtypes)

    return _ew(fn, pieces, list(out_dtypes), name)


def _adamw(w, g_pieces, m, v, name):
    n_g = len(g_pieces)

    def fn(w_, *rest):
        g = rest[0]
        for piece in rest[1:n_g]:
            g = g + piece
        m_, v_ = rest[n_g], rest[n_g + 1]
        m2 = ADAM_B1 * m_ + (1.0 - ADAM_B1) * g
        v2 = ADAM_B2 * v_ + (1.0 - ADAM_B2) * (g * g)
        m_hat = m2 / (1.0 - ADAM_B1 ** ADAM_STEP)
        v_hat = v2 / (1.0 - ADAM_B2 ** ADAM_STEP)
        delta = -ADAM_LR * (m_hat / (jnp.sqrt(v_hat) + ADAM_EPS) + ADAM_WD * w_)
        return g, delta, m2, v2

    return _ew(fn, [w, *g_pieces, m, v], [F32] * 4, name)


def _ada_fwd(cond, w, b, name):
    r, d = cond.shape
    n = w.shape[1]
    tn = _pick(n, (1152, 768, 512, 384, 256, 128))

    def body(c_ref, w_ref, b_ref, o_ref, s_ref):
        c = c_ref[...]
        sc = c * jax.nn.sigmoid(c)
        s_ref[...] = sc
        o_ref[...] = _nn(sc.astype(MXU), w_ref[...].astype(MXU)) + b_ref[...]

    return _call(body, name=name, grid=(n // tn,),
                 in_specs=[_fixed((r, d)), pl.BlockSpec((d, tn), lambda j: (0, j)), pl.BlockSpec((1, tn), lambda j: (0, j))],
                 out_specs=[pl.BlockSpec((r, tn), lambda j: (0, j)), _fixed((r, d))],
                 out_shape=[_sds((r, n), F32), _sds((r, d), F32)])(cond, w, b)


def _cctx_grad(parts, c_ctx, name):
    n, d = parts.shape

    def body(p_ref, c_ref, o_ref):
        c = c_ref[...]
        sg = jax.nn.sigmoid(c)
        acc = p_ref[0:1, :]
        for j in range(1, n):
            acc = acc + p_ref[j:j + 1, :]
        o_ref[...] = acc * (sg * (1.0 + c * (1.0 - sg)))

    return _call(body, name=name, grid=(1,), in_specs=[_fixed((n, d)), _fixed((1, d))], out_specs=_fixed((1, d)),
                 out_shape=_sds((1, d), F32))(parts, c_ctx)


def _here():
    return lax.axis_index("x"), lax.axis_index("y"), lax.axis_index("c")


def _flip(v, bit):
    return 1 - v if bit else v


def _allgather8(xb, name):
    r, n = xb.shape

    def body(x_ref, out_ref, send_sems, recv_sems, local_sem):
        x, y, c = _here()
        me = 4 * x + 2 * y + c
        local = pltpu.make_async_copy(x_ref, out_ref.at[me], local_sem)
        local.start()
        sends = []
        for k in range(1, 8):
            peer = (_flip(x, k & 4), _flip(y, k & 2), _flip(c, k & 1))
            cp = pltpu.make_async_remote_copy(src_ref=x_ref, dst_ref=out_ref.at[me], send_sem=send_sems.at[k - 1],
                                              recv_sem=recv_sems.at[k - 1], device_id=peer, device_id_type=MESH)
            cp.start()
            sends.append(cp)
        for k in range(1, 8):
            peer = (_flip(x, k & 4), _flip(y, k & 2), _flip(c, k & 1))
            src = 4 * peer[0] + 2 * peer[1] + peer[2]
            pltpu.make_async_remote_copy(src_ref=x_ref, dst_ref=out_ref.at[src], send_sem=send_sems.at[k - 1],
                                         recv_sem=recv_sems.at[k - 1], device_id=peer, device_id_type=MESH).wait_recv()
        for cp in sends:
            cp.wait_send()
        local.wait()

    vmem = pl.BlockSpec(memory_space=pltpu.VMEM)
    return pl.pallas_call(
        body, name=name, out_shape=_sds((8, r, n), xb.dtype), in_specs=[vmem], out_specs=vmem,
        scratch_shapes=[pltpu.SemaphoreType.DMA((7,)), pltpu.SemaphoreType.DMA((7,)), pltpu.SemaphoreType.DMA(())],
        compiler_params=pltpu.CompilerParams(vmem_limit_bytes=VMEM_LIMIT))(xb)


def _shard_of(ref, axis, j, size):
    sl = pl.ds(j * size, size)
    return ref.at[:, sl, :] if axis == 1 else ref.at[:, :, sl]


def _piece(ref, axis, j, size, layer):
    lay, sl = pl.ds(layer, 1), pl.ds(j * size, size)
    return ref.at[lay, sl, :] if axis == 1 else ref.at[lay, :, sl]


def _gather_chips(shards, axes, name):
    n = len(shards)
    fulls = []
    for a, ax in zip(shards, axes):
        assert a.shape[0] == 2
        shp = list(a.shape)
        shp[ax] *= 4
        fulls.append(_sds(tuple(shp), a.dtype))

    def body(*refs):
        ins, outs = refs[:n], refs[n:2 * n]
        ici_send, ici_recv, d2d_send, d2d_recv, local_sems = refs[2 * n:]
        x, y, c = _here()
        chips = [(_flip(x, k & 2), _flip(y, k & 1)) for k in range(1, 4)]
        local, sends = [], []
        for a in range(n):
            size = ins[a].shape[axes[a]]
            cp = pltpu.make_async_copy(ins[a], _shard_of(outs[a], axes[a], 2 * x + y, size), local_sems.at[a])
            cp.start()
            local.append(cp)
            for j, (px, py) in enumerate(chips):
                cp = pltpu.make_async_remote_copy(src_ref=ins[a].at[pl.ds(c, 1)], dst_ref=_piece(outs[a], axes[a], 2 * x + y, size, c),
                                                  send_sem=ici_send.at[3 * a + j], recv_sem=ici_recv.at[3 * a + j],
                                                  device_id=(px, py, c), device_id_type=MESH)
                cp.start()
                sends.append(cp)
        for a in range(n):
            size = ins[a].shape[axes[a]]
            for j, (px, py) in enumerate(chips):
                landed = _piece(outs[a], axes[a], 2 * px + py, size, c)
                pltpu.make_async_remote_copy(src_ref=ins[a].at[pl.ds(c, 1)], dst_ref=landed, send_sem=ici_send.at[3 * a + j],
                                             recv_sem=ici_recv.at[3 * a + j], device_id=(px, py, c), device_id_type=MESH).wait_recv()
                cp = pltpu.make_async_remote_copy(src_ref=landed, dst_ref=landed, send_sem=d2d_send.at[3 * a + j],
                                                  recv_sem=d2d_recv.at[3 * a + j], device_id=(x, y, 1 - c), device_id_type=MESH)
                cp.start()
                sends.append(cp)
        for a in range(n):
            size = ins[a].shape[axes[a]]
            for j, (px, py) in enumerate(chips):
                passed = _piece(outs[a], axes[a], 2 * px + py, size, 1 - c)
                pltpu.make_async_remote_copy(src_ref=passed, dst_ref=passed, send_sem=d2d_send.at[3 * a + j],
                                             recv_sem=d2d_recv.at[3 * a + j], device_id=(x, y, 1 - c), device_id_type=MESH).wait_recv()
        for cp in sends:
            cp.wait_send()
        for cp in local:
            cp.wait()

    hbm = pl.BlockSpec(memory_space=pl.ANY)
    return pl.pallas_call(
        body, name=name, out_shape=fulls, in_specs=[hbm] * n, out_specs=[hbm] * n,
        scratch_shapes=[pltpu.SemaphoreType.DMA((3 * n,))] * 4 + [pltpu.SemaphoreType.DMA((n,))])(*shards)


def _swap_layers(arrays, name):
    n = len(arrays)

    def body(*refs):
        ins, outs = refs[:n], refs[n:2 * n]
        send_sems, recv_sems = refs[2 * n:]
        x, y, c = _here()
        copies = []
        for a in range(n):
            cp = pltpu.make_async_remote_copy(src_ref=ins[a].at[pl.ds(1 - c, 1)], dst_ref=outs[a], send_sem=send_sems.at[a],
                                              recv_sem=recv_sems.at[a], device_id=(x, y, 1 - c), device_id_type=MESH)
            cp.start()
            copies.append(cp)
        for cp in copies:
            cp.wait()

    hbm = pl.BlockSpec(memory_space=pl.ANY)
    return pl.pallas_call(
        body, name=name, out_shape=[_sds((1, *a.shape[1:]), a.dtype) for a in arrays], in_specs=[hbm] * n, out_specs=[hbm] * n,
        scratch_shapes=[pltpu.SemaphoreType.DMA((n,)), pltpu.SemaphoreType.DMA((n,))])(*arrays)


def _scatter_chips(fulls, axes, name):
    n = len(fulls)
    recvs = []
    for a, ax in zip(fulls, axes):
        shp = list(a.shape)
        shp[ax] //= 4
        recvs.append(_sds((3, *shp), a.dtype))

    def body(*refs):
        ins, outs = refs[:n], refs[n:2 * n]
        send_sems, recv_sems = refs[2 * n:]
        x, y, c = _here()
        sends = []
        for a in range(n):
            size = ins[a].shape[axes[a]] // 4
            for k in range(1, 4):
                peer = (_flip(x, k & 2), _flip(y, k & 1), c)
                cp = pltpu.make_async_remote_copy(src_ref=_shard_of(ins[a], axes[a], 2 * peer[0] + peer[1], size),
                                                  dst_ref=outs[a].at[k - 1],
                                                  send_sem=send_sems.at[3 * a + k - 1], recv_sem=recv_sems.at[3 * a + k - 1],
                                                  device_id=peer, device_id_type=MESH)
                cp.start()
                sends.append(cp)
        for cp in sends:
            cp.wait_recv()
        for cp in sends:
            cp.wait_send()

    hbm = pl.BlockSpec(memory_space=pl.ANY)
    return pl.pallas_call(
        body, name=name, out_shape=recvs, in_specs=[hbm] * n, out_specs=[hbm] * n,
        scratch_shapes=[pltpu.SemaphoreType.DMA((3 * n,)), pltpu.SemaphoreType.DMA((3 * n,))])(*fulls)


def _sibling_swap(arrays, name):
    n = len(arrays)

    def body(*refs):
        ins, outs = refs[:n], refs[n:2 * n]
        send_sems, recv_sems = refs[2 * n:]
        x, y, c = _here()
        copies = []
        for a in range(n):
            cp = pltpu.make_async_remote_copy(src_ref=ins[a], dst_ref=outs[a], send_sem=send_sems.at[a], recv_sem=recv_sems.at[a],
                                              device_id=(x, y, 1 - c), device_id_type=MESH)
            cp.start()
            copies.append(cp)
        for cp in copies:
            cp.wait()

    hbm = pl.BlockSpec(memory_space=pl.ANY)
    return pl.pallas_call(
        body, name=name, out_shape=[_sds(a.shape, a.dtype) for a in arrays], in_specs=[hbm] * n, out_specs=[hbm] * n,
        scratch_shapes=[pltpu.SemaphoreType.DMA((n,)), pltpu.SemaphoreType.DMA((n,))])(*arrays)


def _ffn_fwd(xs, g, mods, k0, w_up, w_down, s, tag):
    h, ua, ub, act = _norm_mm(xs, g, mods, k0, k0 + 1, w_up, s, tag + "_up", True)
    y, xn = _mm_res(act, w_down, xs, mods, k0 + 2, 0.5, s, tag + "_down")
    return xn, (xs, h, ua, ub, act, y)


def _dw(gw, key, a, b, name, col0=0, n_total=None):
    shape = (gw["depth"], a.shape[1], n_total or b.shape[1])
    gw[key] = _mm(a, b, "tn", F32, name, into=(gw.get(key), shape, gw["layer"], col0))


def _ffn_bwd(dxn, saved, g, mods, k0, w_up, w_down, s, tag, gw, up_key, down_key):
    xs, h, ua, ub, act, y = saved
    dy, dgate, dua, dub = _resb_mm(dxn, y, mods, k0 + 2, 0.5, w_down, s, tag + "_down_dx", (ua, ub))
    _dw(gw, down_key, act, dy, tag + "_down_dw")
    f = dua.shape[1]
    _dw(gw, up_key, h, dua, tag + "_upa_dw", 0, 2 * f)
    _dw(gw, up_key, h, dub, tag + "_upb_dw", f, 2 * f)
    dx, dsh, dsc, dg = _mm_normb([dua, dub], w_up, xs, dxn, g, mods, k0 + 1, s, tag + "_up_dx")
    return dx, dg, [dsh, dsc, dgate]


def _mix_fwd(xs, g, mods, wl, pl_, tabs, s, tag):
    cos, sin = tabs
    h, p = _norm_mm(xs, g, mods, 3, 4, wl["w_in"], s, tag + "_in", False)
    q, k, v = _rope(p, cos, sin, tag + "_rope")
    bias = _bias_table(pl_["rpb"], s // GRID_W)
    oa = _na_fwd(q, k, v, bias, s, tag + "_na")
    ob = _gmlp(p, pl_["ln_v_g"], pl_["ln_v_b"], pl_["w_s"], pl_["b_s"], tag + "_sg")
    pa = _mm(oa, wl["w_pa"], "nn", MXU, tag + "_pa")
    pb = _mm(ob, wl["w_pb"], "nn", MXU, tag + "_pb")
    mg = _merge(pa, pb, p, pl_["b_gate"], tag + "_merge")
    y, xn = _mm_res(mg, wl["w_o"], xs, mods, 5, 1.0, s, tag + "_o")
    return xn, (xs, h, p, q, k, v, bias, oa, ob, pa, pb, mg, y)


def _mix_bwd(dxn, saved, g, mods, wl, pl_, tabs, s, tag, gw):
    xs, h, p, q, k, v, bias, oa, ob, pa, pb, mg, y = saved
    cos, sin = tabs
    gp = {}
    dy, dgate, dmg = _resb_mm(dxn, y, mods, 5, 1.0, wl["w_o"], s, tag + "_o_dx")
    _dw(gw, "w_o", mg, dy, tag + "_o_dw")
    dpa, dpb, dla, dlb, dba, dbb = _merge_bwd(dmg, pa, pb, p, pl_["b_gate"], tag + "_merge_b")
    gp["b_gate"] = jnp.concatenate([dba, dbb], axis=1)
    _dw(gw, "w_pa", oa, dpa, tag + "_pa_dw")
    doa = _mm(dpa, wl["w_pa"], "nt", MXU, tag + "_pa_dx")
    _dw(gw, "w_pb", ob, dpb, tag + "_pb_dw")
    dob = _mm(dpb, wl["w_pb"], "nt", MXU, tag + "_pb_dx")
    du, dvs, gp["w_s"], gp["b_s"], gp["ln_v_g"], gp["ln_v_b"] = _gmlp_bwd(
        p, dob, pl_["ln_v_g"], pl_["ln_v_b"], pl_["w_s"], pl_["b_s"], tag + "_sg_b")
    dqr, dkr, dv, dbias = _na_bwd(q, k, v, doa, bias, s, tag + "_na_b")
    gp["rpb"] = _rpb_grad(dbias, s // GRID_W, tag + "_rpb")
    dq, dk, dvv = _rope_bwd(dqr, dkr, dv, cos, sin, tag + "_rope_b")
    dp = jnp.concatenate([dq, dk, dvv, du, dvs, dla, dlb], axis=1)
    _dw(gw, "w_in", h, dp, tag + "_in_dw")
    dx, dsh, dsc, dg = _mm_normb([dp], wl["w_in"], xs, dxn, g, mods, 4, s, tag + "_in_dx")
    return dx, gp, dg, [dsh, dsc, dgate]


def _local_step(x, ctx, tgt, mods, wts, prm):
    s, d = x.shape
    depth = mods.shape[0]
    tabs = _rope_tables(s, ctx.shape[0])
    xs = jnp.concatenate([x, ctx], axis=0)
    saved = []
    for l in range(depth):
        wl = {k: (v, l) for k, v in wts.items()}
        pl_ = _layer_params(prm, l)
        xs, s1 = _ffn_fwd(xs, pl_["g"][0], mods[l], 0, wl["w_ff1_up"], wl["w_ff1_down"], s, f"l{l}_ff1")
        xs, s2 = _mix_fwd(xs, pl_["g"][1], mods[l], wl, pl_, tabs, s, f"l{l}_mix")
        xs, s3 = _ffn_fwd(xs, pl_["g"][2], mods[l], 6, wl["w_ff2_up"], wl["w_ff2_down"], s, f"l{l}_ff2")
        saved.append((s1, s2, s3))
    loss, dxs, d_final_g = _final(xs, tgt, prm["final_g"].reshape(1, d), "final")
    gw = {"depth": depth}
    gp = {k: [None] * depth for k in ("norm_g", "b_gate", "rpb", "ln_v_g", "ln_v_b", "w_s", "b_s")}
    dmods = [None] * depth
    for l in reversed(range(depth)):
        wl = {k: (v, l) for k, v in wts.items()}
        pl_ = _layer_params(prm, l)
        s1, s2, s3 = saved[l]
        gw["layer"] = l
        dxs, dg2, dm2 = _ffn_bwd(dxs, s3, pl_["g"][2], mods[l], 6, wl["w_ff2_up"], wl["w_ff2_down"], s, f"l{l}_ff2",
                                 gw, "w_ff2_up", "w_ff2_down")
        dxs, gpm, dg1, dm1 = _mix_bwd(dxs, s2, pl_["g"][1], mods[l], wl, pl_, tabs, s, f"l{l}_mix", gw)
        dxs, dg0, dm0 = _ffn_bwd(dxs, s1, pl_["g"][0], mods[l], 0, wl["w_ff1_up"], wl["w_ff1_down"], s, f"l{l}_ff1",
                                 gw, "w_ff1_up", "w_ff1_down")
        gp["b_gate"][l] = gpm["b_gate"][0]
        gp["rpb"][l] = gpm["rpb"]
        gp["ln_v_g"][l] = gpm["ln_v_g"][0]
        gp["ln_v_b"][l] = gpm["ln_v_b"][0]
        gp["w_s"][l] = gpm["w_s"]
        gp["b_s"][l] = gpm["b_s"][..., 0]
        gp["norm_g"][l] = jnp.concatenate([dg0, dg1, dg2], axis=0)
        dmods[l] = jnp.concatenate(dm0 + dm1 + dm2, axis=1)
    gw = {k: gw[k] for k in wts}
    gp = {k: jnp.stack(v) for k, v in gp.items()}
    gp["final_g"] = d_final_g[0]
    return loss[0, 0], dxs[:s], jnp.stack(dmods), gw, gp


def _layer_params(prm, l):
    d = prm["norm_g"].shape[-1]
    return {
        "g": [prm["norm_g"][l, i].reshape(1, d) for i in range(3)],
        "b_gate": prm["b_gate"][l].reshape(1, -1),
        "rpb": prm["rpb"][l],
        "ln_v_g": prm["ln_v_g"][l].reshape(1, -1),
        "ln_v_b": prm["ln_v_b"][l].reshape(1, -1),
        "w_s": prm["w_s"][l],
        "b_s": prm["b_s"][l][..., None],
    }


SMALL = ("norm_g", "b_gate", "rpb", "ln_v_g", "ln_v_b", "w_s", "b_s", "final_g")
PACK_LANES = 1024


def _pack(parts):
    flat = jnp.concatenate([p.reshape(-1) for p in parts])
    rows = -(-flat.shape[0] // PACK_LANES)
    rows = -(-rows // 8) * 8
    return jnp.pad(flat, (0, rows * PACK_LANES - flat.shape[0])).reshape(rows, PACK_LANES)


def _unpack(flat, shapes):
    out, off = [], 0
    for shp in shapes:
        n = int(np.prod(shp))
        out.append(flat[..., off:off + n].reshape(*flat.shape[:-1], *shp))
        off += n
    return out


def kernel(x, c, ctx, c_ctx, w_ada, b_ada, norm_g, w_ff1_up, w_ff1_down, w_in, b_gate, rpb, ln_v_g, ln_v_b, w_s, b_s, w_pa, w_pb, w_o, w_ff2_up, w_ff2_down, final_g, loss_target, m_c_ctx, m_w_ada, m_b_ada, m_norm_g, m_w_ff1_up, m_w_ff1_down, m_w_in, m_b_gate, m_rpb, m_ln_v_g, m_ln_v_b, m_w_s, m_b_s, m_w_pa, m_w_pb, m_w_o, m_w_ff2_up, m_w_ff2_down, m_final_g, v_c_ctx, v_w_ada, v_b_ada, v_norm_g, v_w_ff1_up, v_w_ff1_down, v_w_in, v_b_gate, v_rpb, v_ln_v_g, v_ln_v_b, v_w_s, v_b_s, v_w_pa, v_w_pb, v_w_o, v_w_ff2_up, v_w_ff2_down, v_final_g):
    weights = dict(c_ctx=c_ctx, w_ada=w_ada, b_ada=b_ada, norm_g=norm_g, w_ff1_up=w_ff1_up, w_ff1_down=w_ff1_down, w_in=w_in,
                   b_gate=b_gate, rpb=rpb, ln_v_g=ln_v_g, ln_v_b=ln_v_b, w_s=w_s, b_s=b_s, w_pa=w_pa, w_pb=w_pb, w_o=w_o,
                   w_ff2_up=w_ff2_up, w_ff2_down=w_ff2_down, final_g=final_g)
    mom_m = dict(c_ctx=m_c_ctx, w_ada=m_w_ada, b_ada=m_b_ada, norm_g=m_norm_g, w_ff1_up=m_w_ff1_up, w_ff1_down=m_w_ff1_down,
                 w_in=m_w_in, b_gate=m_b_gate, rpb=m_rpb, ln_v_g=m_ln_v_g, ln_v_b=m_ln_v_b, w_s=m_w_s, b_s=m_b_s, w_pa=m_w_pa,
                 w_pb=m_w_pb, w_o=m_w_o, w_ff2_up=m_w_ff2_up, w_ff2_down=m_w_ff2_down, final_g=m_final_g)
    mom_v = dict(c_ctx=v_c_ctx, w_ada=v_w_ada, b_ada=v_b_ada, norm_g=v_norm_g, w_ff1_up=v_w_ff1_up, w_ff1_down=v_w_ff1_down,
                 w_in=v_w_in, b_gate=v_b_gate, rpb=v_rpb, ln_v_g=v_ln_v_g, ln_v_b=v_ln_v_b, w_s=v_w_s, b_s=v_b_s, w_pa=v_w_pa,
                 w_pb=v_w_pb, w_o=v_w_o, w_ff2_up=v_w_ff2_up, w_ff2_down=v_w_ff2_down, final_g=v_final_g)
    order = list(weights)
    mx, my, mc = _here()
    dev = 4 * mx + 2 * my + mc
    chip = 2 * mx + my
    depth, d, n_ada = w_ada.shape
    dq = d // 4

    c_all = _allgather8(jnp.pad(c, ((0, 7), (0, 0))), "gather_c")[:, 0, :]
    cond = jnp.concatenate([c_all, c_ctx[None, :], jnp.zeros((7, d), F32)], axis=0)
    b_shard = lax.dynamic_slice(b_ada, (0, chip * n_ada), (depth, n_ada))
    proj = [_ada_fwd(cond, w_ada[l], b_shard[l:l + 1], f"ada{l}") for l in range(depth)]
    silu_c = proj[0][1]
    mods_sh = _allgather8(jnp.concatenate([p[0] for p in proj], axis=0), "gather_mods")
    mods_all = jnp.transpose(mods_sh[0::2].reshape(4, depth, 16, n_ada), (1, 2, 0, 3)).reshape(depth, 16, N_MOD, d)
    mods = jnp.stack([lax.dynamic_index_in_dim(mods_all, dev, axis=1, keepdims=False), mods_all[:, 8]], axis=1)

    full = _gather_chips([weights[k].astype(MXU) for k in BIG], [SHARD_AXIS[k] for k in BIG], "gather_w")
    wts = dict(zip(BIG, full))
    prm = {k: weights[k] for k in SMALL if k != "norm_g"}
    norm_full = _allgather8(jnp.pad(norm_g.reshape(depth * 3, dq), ((0, 8 - depth * 3), (0, 0))), "gather_norm_g")
    prm["norm_g"] = jnp.transpose(norm_full[0::2, :depth * 3].reshape(4, depth, 3, dq), (1, 2, 0, 3)).reshape(depth, 3, d)

    loss, grad_x, dmods, gw, gp = _local_step(x[0], ctx[0], loss_target[0], mods, wts, prm)
    loss = lax.psum(loss, ("x", "y", "c"))

    small_shapes = [(depth, 2, N_MOD * d)] + [weights[k].shape if k != "norm_g" else (depth, 3, d) for k in SMALL]
    packed = _allgather8(_pack([dmods.reshape(depth, 2, N_MOD * d)] + [gp[k] for k in SMALL]), "gather_small")
    rows = packed.shape[1]
    total = _sum_pieces([packed[i] for i in range(8)], "sum_small")[0].reshape(-1)
    sums = dict(zip(("dmods",) + SMALL, _unpack(total, small_shapes)))
    dmods_dev = _unpack(packed.reshape(8, rows * PACK_LANES), small_shapes[:1])[0]

    g_ada, cc_parts = [], []
    for l in range(depth):
        dm = jnp.concatenate([dmods_dev[:, l, 0], sums["dmods"][l, 1][None], jnp.zeros((7, N_MOD * d), F32)], axis=0)
        dm_sh = lax.dynamic_slice(dm, (0, chip * n_ada), (16, n_ada))
        g_ada.append(_mm(silu_c, dm_sh, "tn", F32, f"ada{l}_dw"))
        cc_parts.append(_mm(dm_sh, w_ada[l], "nt", F32, f"ada{l}_dc")[8:9])
    cc_all = _allgather8(jnp.pad(jnp.concatenate(cc_parts, axis=0), ((0, 8 - depth), (0, 0))), "gather_cctx")
    g_cctx = _cctx_grad(cc_all[0::2, :depth].reshape(4 * depth, d), c_ctx.reshape(1, d), "cctx_grad")

    axes = [SHARD_AXIS[k] for k in BIG]
    from_sibling = _swap_layers([gw[k] for k in BIG], "swap_layer_gw")
    my_layer = jnp.reshape(mc, (1,)).astype(jnp.int32)
    pair = [_pair_sum(gw[k], r, my_layer, "pair_" + k) for k, r in zip(BIG, from_sibling)]
    recv = _scatter_chips([p[1] for p in pair], axes, "scatter_gw")
    mine = []
    for k, ax, p, r in zip(BIG, axes, pair, recv):
        size = p[0].shape[ax] // 4
        own = lax.dynamic_slice_in_dim(p[0], chip * size, size, axis=ax)
        mine.append(_sum_pieces([own, r[0], r[1], r[2]], "sum_" + k)[0])
    other = _sibling_swap(mine, "swap_gw")

    pieces = {k: [jnp.concatenate([jnp.where(mc == 0, a, b), jnp.where(mc == 0, b, a)], axis=0)] for k, a, b in zip(BIG, mine, other)}
    pieces["w_ada"] = [jnp.stack(g_ada)]
    pieces["b_ada"] = [sums["dmods"][:, 0], sums["dmods"][:, 1]]
    pieces["c_ctx"] = [g_cctx[0]]
    for k in SMALL:
        pieces[k] = [sums[k]]
    pieces["norm_g"] = [lax.dynamic_slice_in_dim(sums["norm_g"], chip * dq, dq, axis=2)]
    res = {k: _adamw(weights[k], pieces[k], mom_m[k], mom_v[k], "adamw_" + k) for k in order}
    return (loss, grad_x[None], *[res[k][0] for k in order], *[res[k][1] for k in order],
            *[res[k][2] for k in order], *[res[k][3] for k in order])
```

```python
import numpy as np
import jax
import jax.numpy as jnp
from jax import lax
from jax.experimental import pallas as pl
from jax.experimental.pallas import tpu as pltpu

F32 = jnp.float32
MXU = jnp.bfloat16
EPS = 1e-6
GRID_W, HEADS, HEAD_DIM = 64, 8, 64
NA_WIDTH = SG_WIDTH = 512
WIN_H, WIN_W = 8, 16
SG_CHUNK, SG_GROUPS = 128, 4
N_MOD = 9
ROPE_THETA = 10000.0
Q_ROWS, K_ROWS = 4, 12
TQ, TK = Q_ROWS * GRID_W, K_ROWS * GRID_W
TM = 256
LANES = 128
NEG = -1e30
VMEM_LIMIT = 56 * 2 ** 20
ADAM_LR, ADAM_B1, ADAM_B2, ADAM_EPS, ADAM_WD, ADAM_STEP = 0.001, 0.9, 0.999, 1e-08, 0.01, 10
MESH = pl.DeviceIdType.MESH
BIG = ("w_ff1_up", "w_ff1_down", "w_in", "w_pa", "w_pb", "w_o", "w_ff2_up", "w_ff2_down")
SHARD_AXIS = {"w_ff1_up": 2, "w_ff1_down": 1, "w_in": 2, "w_pa": 2, "w_pb": 2, "w_o": 1, "w_ff2_up": 2, "w_ff2_down": 1}


def _call(body, *, name, grid, in_specs, out_specs, out_shape, scratch=(), aliases=None):
    return pl.pallas_call(
        body, name=name, grid=grid, in_specs=in_specs, out_specs=out_specs, out_shape=out_shape,
        scratch_shapes=list(scratch), input_output_aliases=aliases or {},
        compiler_params=pltpu.CompilerParams(dimension_semantics=("arbitrary",) * len(grid), vmem_limit_bytes=VMEM_LIMIT))


def _w_dims(w):
    return w[0].shape[1:] if isinstance(w, tuple) else w.shape


def _w_arr(w):
    return w[0] if isinstance(w, tuple) else w


def _w_spec(w, block, index):
    if isinstance(w, tuple):
        layer = w[1]
        return pl.BlockSpec((None, *block), lambda *ids: (layer, *index(*ids)))
    return pl.BlockSpec(block, index)


def _pick(n, prefs):
    for p in prefs:
        if n % p == 0:
            return p
    return n


def _row_tile(t):
    return _pick(t, (640, 256))


def _rows(tm, n, col=0):
    return pl.BlockSpec((tm, n), lambda i: (i, col))


def _fixed(shape):
    return pl.BlockSpec(shape, lambda *_: (0,) * len(shape))


def _sds(shape, dtype):
    return jax.ShapeDtypeStruct(shape, dtype)


def _mm(a, b, mode, out_dtype, name, into=None):
    if mode == "tn":
        r, m = a.shape
        n = b.shape[1]
        tm = _pick(m, (1024, 1408, 704, 512, 256, 128))
        tn = _pick(n, (512, 1408, 256, 128))
        tr = _pick(r, (1280, 640, 512, 256, 128))

        def body(a_ref, b_ref, *rest):
            o_ref = rest[-1]

            @pl.when(pl.program_id(2) == 0)
            def _():
                o_ref[...] = jnp.zeros_like(o_ref)

            o_ref[...] += lax.dot_general(a_ref[...].astype(MXU), b_ref[...].astype(MXU), (((0,), (0,)), ((), ())),
                                          preferred_element_type=F32)

        in_specs = [pl.BlockSpec((tr, tm), lambda i, j, k: (k, i)), pl.BlockSpec((tr, tn), lambda i, j, k: (k, j))]
        if into is None:
            return _call(body, name=name, grid=(m // tm, n // tn, r // tr), in_specs=in_specs,
                         out_specs=pl.BlockSpec((tm, tn), lambda i, j, k: (i, j)), out_shape=_sds((m, n), F32))(a, b)
        buf, shape, layer, col0 = into
        out_spec = pl.BlockSpec((None, tm, tn), lambda i, j, k: (layer, i, j + col0 // tn))
        if buf is None:
            return _call(body, name=name, grid=(m // tm, n // tn, r // tr), in_specs=in_specs, out_specs=out_spec,
                         out_shape=_sds(shape, F32))(a, b)
        return _call(body, name=name, grid=(m // tm, n // tn, r // tr), in_specs=in_specs + [pl.BlockSpec(memory_space=pl.ANY)],
                     out_specs=out_spec, out_shape=_sds(shape, F32), aliases={2: 0})(a, b, buf)
    m, k = a.shape
    n = _w_dims(b)[1] if mode == "nn" else _w_dims(b)[0]
    tm = _pick(m, (1280, 640, 512, 256, 128) if k <= 2816 else (640, 512, 256, 128))
    tn = _pick(n, (512, 1408, 256, 128))
    dims = (((1,), (0,)), ((), ())) if mode == "nn" else (((1,), (1,)), ((), ()))

    def body(a_ref, b_ref, o_ref):
        o_ref[...] = lax.dot_general(a_ref[...].astype(MXU), b_ref[...].astype(MXU), dims,
                                     preferred_element_type=F32).astype(o_ref.dtype)

    b_spec = _w_spec(b, (k, tn), lambda i, j: (0, j)) if mode == "nn" else _w_spec(b, (tn, k), lambda i, j: (j, 0))
    return _call(body, name=name, grid=(m // tm, n // tn), in_specs=[pl.BlockSpec((tm, k), lambda i, j: (i, 0)), b_spec],
                 out_specs=pl.BlockSpec((tm, tn), lambda i, j: (i, j)), out_shape=_sds((m, n), out_dtype))(a, _w_arr(b))


def _row_chunks(tm):
    rc = _pick(tm, (256, 128))
    return [slice(r, r + rc) for r in range(0, tm, rc)]


def _ctx_rows(i, tm, s):
    return (i * tm + lax.broadcasted_iota(jnp.int32, (tm, 1), 0)) >= s


def _mod_row(m_ref, k, ctx):
    return jnp.where(ctx, m_ref[1, k:k + 1, :], m_ref[0, k:k + 1, :])


def _stream_sums(i, tm, s, refs_and_vals):
    @pl.when((i + 1) * tm <= s)
    def _():
        for ref, val in refs_and_vals:
            ref[0] += jnp.sum(val, axis=0, keepdims=True)

    @pl.when((i + 1) * tm > s)
    def _():
        ctx = _ctx_rows(i, tm, s)
        for ref, val in refs_and_vals:
            ref[0] += jnp.sum(jnp.where(ctx, 0.0, val), axis=0, keepdims=True)
            ref[1] += jnp.sum(jnp.where(ctx, val, 0.0), axis=0, keepdims=True)


def _norm_mm(xs, g, mods, k_shift, k_scale, w, s, name, glu):
    t, d = xs.shape
    n = _w_dims(w)[1] // 2 if glu else _w_dims(w)[1]
    tm = _pick(t, (1280, 640, 256))
    tn = _pick(n, (256, 128)) if glu else _pick(n, (512, 256, 128))
    nj = n // tn

    def body(x_ref, g_ref, m_ref, *refs):
        i, j = pl.program_id(0), pl.program_id(1)
        w_refs, h_ref, o_refs = refs[:2 if glu else 1], refs[2 if glu else 1], refs[3 if glu else 2:]

        @pl.when(j == 0)
        def _():
            x = x_ref[...]
            rstd = lax.rsqrt(jnp.mean(x * x, axis=-1, keepdims=True) + EPS)
            ctx = _ctx_rows(i, tm, s)
            h = x * rstd * g_ref[...] * (1.0 + _mod_row(m_ref, k_scale, ctx)) + _mod_row(m_ref, k_shift, ctx)
            h_ref[...] = h.astype(h_ref.dtype)

        for rows in _row_chunks(tm):
            h = h_ref[rows, :]
            a = _nn(h, w_refs[0][...])
            o_refs[0][rows, :] = a.astype(o_refs[0].dtype)
            if glu:
                b = _nn(h, w_refs[1][...])
                o_refs[1][rows, :] = b.astype(o_refs[1].dtype)
                o_refs[2][rows, :] = (a * jax.nn.sigmoid(a) * b).astype(o_refs[2].dtype)

    tile = pl.BlockSpec((tm, tn), lambda i, j: (i, j))
    row = pl.BlockSpec((tm, d), lambda i, j: (i, 0))
    w_specs = [_w_spec(w, (d, tn), lambda i, j: (0, j))] + ([_w_spec(w, (d, tn), lambda i, j: (0, j + nj))] if glu else [])
    n_out = 3 if glu else 1
    return _call(body, name=name, grid=(t // tm, nj),
                 in_specs=[row, _fixed((1, d)), _fixed((2, N_MOD, d))] + w_specs,
                 out_specs=[row] + [tile] * n_out,
                 out_shape=[_sds((t, d), MXU)] + [_sds((t, n), MXU)] * n_out)(xs, g, mods, *([_w_arr(w)] * (2 if glu else 1)))


def _mm_res(a, w, xs, mods, k_gate, coef, s, name):
    t, k = a.shape
    d = _w_dims(w)[1]
    tm = _pick(t, (1280, 640, 256))
    tn = _pick(d, (512, 256, 128))

    def body(a_ref, w_ref, x_ref, m_ref, y_ref, o_ref):
        y = _nn(a_ref[...], w_ref[...])
        y_ref[...] = y.astype(y_ref.dtype)
        gate = _mod_row(m_ref, k_gate, _ctx_rows(pl.program_id(0), tm, s))
        o_ref[...] = x_ref[...] + (coef * gate) * y

    tile = pl.BlockSpec((tm, tn), lambda i, j: (i, j))
    return _call(body, name=name, grid=(t // tm, d // tn),
                 in_specs=[pl.BlockSpec((tm, k), lambda i, j: (i, 0)), _w_spec(w, (k, tn), lambda i, j: (0, j)), tile,
                           pl.BlockSpec((2, N_MOD, tn), lambda i, j: (0, 0, j))],
                 out_specs=[tile, tile], out_shape=[_sds((t, d), MXU), _sds((t, d), F32)])(a, _w_arr(w), xs, mods)


def _resb_mm(dxn, y, mods, k_gate, coef, w, s, name, ups=None):
    t, d = dxn.shape
    n = _w_dims(w)[0]
    tm = _pick(t, (1280, 640, 256))
    tn = _pick(n, (256, 128)) if ups else _pick(n, (512, 256, 128))

    def body(dx_ref, y_ref, m_ref, w_ref, *refs):
        i, j = pl.program_id(0), pl.program_id(1)
        u_refs, (dy_ref, dgt_ref), o_refs = (refs[:2], refs[2:4], refs[4:]) if ups else ((), refs[:2], refs[2:])

        @pl.when((i == 0) & (j == 0))
        def _():
            dgt_ref[...] = jnp.zeros_like(dgt_ref)

        @pl.when(j == 0)
        def _():
            dx = dx_ref[...]
            gate = _mod_row(m_ref, k_gate, _ctx_rows(i, tm, s))
            dy_ref[...] = ((coef * gate) * dx).astype(dy_ref.dtype)
            _stream_sums(i, tm, s, [(dgt_ref, coef * y_ref[...].astype(F32) * dx)])

        dact = _nt(dy_ref[...], w_ref[...])
        if ups:
            a, b = u_refs[0][...].astype(F32), u_refs[1][...].astype(F32)
            sg = jax.nn.sigmoid(a)
            o_refs[0][...] = (dact * b * sg * (1.0 + a * (1.0 - sg))).astype(o_refs[0].dtype)
            o_refs[1][...] = (dact * a * sg).astype(o_refs[1].dtype)
        else:
            o_refs[0][...] = dact.astype(o_refs[0].dtype)

    tile = pl.BlockSpec((tm, tn), lambda i, j: (i, j))
    row = pl.BlockSpec((tm, d), lambda i, j: (i, 0))
    n_out = 2 if ups else 1
    return _call(body, name=name, grid=(t // tm, n // tn),
                 in_specs=[row, row, _fixed((2, N_MOD, d)), _w_spec(w, (tn, d), lambda i, j: (j, 0))] + ([tile, tile] if ups else []),
                 out_specs=[row, _fixed((2, 1, d))] + [tile] * n_out,
                 out_shape=[_sds((t, d), MXU), _sds((2, 1, d), F32)] + [_sds((t, n), MXU)] * n_out)(
                     dxn, y, mods, _w_arr(w), *(ups or ()))


def _mm_normb(a_list, w, xs, dres, g, mods, k_scale, s, name):
    t, d = xs.shape
    ka = a_list[0].shape[1]
    tm = _pick(t, (640, 256))
    tk = _pick(ka, (1408, 1536, 1024, 512, 256, 128))
    nk1 = ka // tk
    n_a = len(a_list)
    nk = nk1 * n_a

    def body(*refs):
        a_refs, (w_ref, x_ref, dr_ref, g_ref, m_ref, dx_ref, dsh_ref, dsc_ref, dg_ref, acc) = refs[:n_a], refs[n_a:]
        i, k = pl.program_id(0), pl.program_id(1)

        @pl.when((i == 0) & (k == 0))
        def _():
            dsh_ref[...] = jnp.zeros_like(dsh_ref)
            dsc_ref[...] = jnp.zeros_like(dsc_ref)
            dg_ref[...] = jnp.zeros_like(dg_ref)

        @pl.when(k == 0)
        def _():
            acc[...] = jnp.zeros_like(acc)

        for q in range(n_a):
            @pl.when((k >= q * nk1) & (k < (q + 1) * nk1))
            def _():
                acc[...] += _nt(a_refs[q][...], w_ref[...])

        @pl.when(k == nk - 1)
        def _():
            x = x_ref[...]
            dh = acc[...]
            rstd = lax.rsqrt(jnp.mean(x * x, axis=-1, keepdims=True) + EPS)
            xhat = x * rstd
            gg = g_ref[...]
            _stream_sums(i, tm, s, [(dsh_ref, dh), (dsc_ref, dh * (xhat * gg))])
            dy = dh * (1.0 + _mod_row(m_ref, k_scale, _ctx_rows(i, tm, s)))
            dg_ref[...] += jnp.sum(dy * xhat, axis=0, keepdims=True)
            dxh = dy * gg
            dx_ref[...] = dr_ref[...] + rstd * (dxh - xhat * jnp.mean(dxh * xhat, axis=-1, keepdims=True))

    row = pl.BlockSpec((tm, d), lambda i, k: (i, 0))
    a_specs = [pl.BlockSpec((tm, tk), lambda i, k, q=q: (i, jnp.clip(k - q * nk1, 0, nk1 - 1))) for q in range(n_a)]
    return _call(body, name=name, grid=(t // tm, nk),
                 in_specs=a_specs + [_w_spec(w, (d, tk), lambda i, k: (0, k)), row, row, _fixed((1, d)), _fixed((2, N_MOD, d))],
                 out_specs=[row, _fixed((2, 1, d)), _fixed((2, 1, d)), _fixed((1, d))],
                 out_shape=[_sds((t, d), F32), _sds((2, 1, d), F32), _sds((2, 1, d), F32), _sds((1, d), F32)],
                 scratch=[pltpu.VMEM((tm, d), F32)])(*a_list, _w_arr(w), xs, dres, g, mods)


def _rope_tables(s, ctx_len):
    n_freq = HEAD_DIM // 4
    tok = jnp.arange(s)
    freqs = ROPE_THETA ** (-jnp.arange(n_freq, dtype=F32) / n_freq)
    ang = jnp.concatenate([(tok // GRID_W).astype(F32)[:, None] * freqs, (tok % GRID_W).astype(F32)[:, None] * freqs], axis=-1)
    cos = jnp.repeat(jnp.cos(ang), 2, axis=-1)
    sin = jnp.repeat(jnp.sin(ang), 2, axis=-1) * jnp.tile(jnp.array([-1.0, 1.0], F32), HEAD_DIM // 2)
    cos = jnp.concatenate([jnp.tile(cos, (1, LANES // HEAD_DIM)), jnp.ones((ctx_len, LANES), F32)], axis=0)
    sin = jnp.concatenate([jnp.tile(sin, (1, LANES // HEAD_DIM)), jnp.zeros((ctx_len, LANES), F32)], axis=0)
    return cos, sin


def _swap_pairs(x):
    n = x.shape[-1]
    lane = lax.broadcasted_iota(jnp.int32, x.shape, 1)
    return jnp.where(lane % 2 == 0, pltpu.roll(x, n - 1, 1), pltpu.roll(x, 1, 1))


def _rope(p, cos, sin, name):
    t = p.shape[0]
    w = NA_WIDTH
    te = _row_tile(t)

    def body(q_ref, k_ref, v_ref, c_ref, s_ref, qo_ref, ko_ref, vo_ref):
        c, s = c_ref[...], s_ref[...]
        for hp in range(w // LANES):
            cols = slice(hp * LANES, (hp + 1) * LANES)
            q, k = q_ref[:, cols].astype(F32), k_ref[:, cols].astype(F32)
            qo_ref[:, cols] = (q * c + _swap_pairs(q) * s).astype(qo_ref.dtype)
            ko_ref[:, cols] = (k * c + _swap_pairs(k) * s).astype(ko_ref.dtype)
        vo_ref[...] = v_ref[...].astype(vo_ref.dtype)

    return _call(body, name=name, grid=(t // te,),
                 in_specs=[_rows(te, w, 0), _rows(te, w, 1), _rows(te, w, 2), _rows(te, LANES), _rows(te, LANES)],
                 out_specs=[_rows(te, w)] * 3, out_shape=[_sds((t, w), MXU)] * 3)(p, p, p, cos, sin)


def _rope_bwd(dq, dk, dv, cos, sin, name):
    t = dq.shape[0]
    te = _row_tile(t)
    n_pairs = NA_WIDTH // LANES

    def body(dq_ref, dk_ref, dv_ref, c_ref, s_ref, qo_ref, ko_ref, vo_ref):
        c, s = c_ref[...], s_ref[...]
        for hp in range(n_pairs):
            cols = slice(hp * LANES, (hp + 1) * LANES)
            a, b = dq_ref[:, cols], dk_ref[hp]
            qo_ref[:, cols] = (a * c + _swap_pairs(a * s)).astype(qo_ref.dtype)
            ko_ref[:, cols] = (b * c + _swap_pairs(b * s)).astype(ko_ref.dtype)
            vo_ref[:, cols] = dv_ref[hp].astype(vo_ref.dtype)

    pairs = pl.BlockSpec((n_pairs, te, LANES), lambda i: (0, i, 0))
    return _call(body, name=name, grid=(t // te,),
                 in_specs=[_rows(te, NA_WIDTH), pairs, pairs, _rows(te, LANES), _rows(te, LANES)],
                 out_specs=[_rows(te, NA_WIDTH)] * 3, out_shape=[_sds((t, NA_WIDTH), MXU)] * 3)(dq, dk, dv, cos, sin)


def _na_geometry(r_grid):
    rows = []
    for r0, ks in ((0, 0), (Q_ROWS, 0), (r_grid - Q_ROWS, r_grid - K_ROWS)):
        dr = np.zeros((Q_ROWS, K_ROWS), np.int32)
        vr = np.zeros((Q_ROWS, K_ROWS), bool)
        for a in range(Q_ROWS):
            r = r0 + a
            rs = min(max(r - WIN_H // 2, 0), r_grid - WIN_H)
            for i in range(K_ROWS):
                kr = ks + i
                vr[a, i] = rs <= kr <= rs + WIN_H - 1
                dr[a, i] = kr - r + WIN_H - 1
        rows.append((dr, vr))
    c = np.arange(GRID_W)
    cs = np.clip(c - WIN_W // 2, 0, GRID_W - WIN_W)
    kc = np.arange(GRID_W)
    vc = (kc[None, :] >= cs[:, None]) & (kc[None, :] <= cs[:, None] + WIN_W - 1)
    dc = kc[None, :] - c[:, None] + WIN_W - 1
    return rows, dc, vc


def _bias_table(rpb, r_grid):
    rows, _, vc = _na_geometry(r_grid)
    n_dc, off = 2 * WIN_W - 1, GRID_W - WIN_W
    u = jnp.pad(rpb, ((0, 0), (0, 0), (off, 2 * GRID_W - 1 - off - n_dc)))
    toep = jnp.stack([u[:, :, GRID_W - 1 - c:2 * GRID_W - 1 - c] for c in range(GRID_W)], axis=1)
    toep = jnp.pad(toep, ((0, 0), (0, 0), (Q_ROWS, Q_ROWS), (0, 0)))
    tabs = []
    for dr, vr in rows:
        per_row = []
        for a in range(Q_ROWS):
            lo = int(dr[a, 0]) + Q_ROWS
            blocks = [jnp.where(vc[None], toep[:, :, lo + i, :], NEG) if vr[a, i] else jnp.full((HEADS, GRID_W, GRID_W), NEG, F32)
                      for i in range(K_ROWS)]
            per_row.append(jnp.concatenate(blocks, axis=-1))
        tabs.append(jnp.stack(per_row, axis=1).reshape(HEADS, TQ, TK))
    tabs.append(jnp.full((HEADS, TQ, TK), NEG, F32))
    return jnp.stack(tabs)


def _variant(g, ngx):
    return jnp.where(g == 0, 0, jnp.where(g >= ngx, 3, jnp.where(g == ngx - 1, 2, 1)))


def _key_start(g, r_grid):
    return pl.multiple_of(jnp.clip(g * Q_ROWS - WIN_H // 2, 0, r_grid - K_ROWS) * GRID_W, TQ)


def _nt(a, b):
    return lax.dot_general(a, b, (((1,), (1,)), ((), ())), preferred_element_type=F32)


def _tn(a, b):
    return lax.dot_general(a, b, (((0,), (0,)), ((), ())), preferred_element_type=F32)


def _nn(a, b):
    return jnp.dot(a, b, preferred_element_type=F32)


def _head_mask(h):
    lane = lax.broadcasted_iota(jnp.int32, (1, LANES), 1)
    return ((lane >= HEAD_DIM * h) & (lane < HEAD_DIM * (h + 1))).astype(F32)


def _softmax_parts(qm, knb, kcx, bias):
    s_nb = _nt(qm, knb) + bias
    s_cx = _nt(qm, kcx)
    m = jnp.maximum(jnp.max(s_nb, axis=-1, keepdims=True), jnp.max(s_cx, axis=-1, keepdims=True))
    e_nb = jnp.exp(s_nb - m)
    e_cx = jnp.exp(s_cx - m)
    inv = 1.0 / (jnp.sum(e_nb, axis=-1, keepdims=True) + jnp.sum(e_cx, axis=-1, keepdims=True))
    return e_nb, e_cx, inv


def _na_specs(t, ngx):
    q_spec = pl.BlockSpec((TQ, LANES), lambda hp, g: (g, hp))
    kv_spec = pl.BlockSpec((t, LANES), lambda hp, g: (0, hp))
    b_spec = pl.BlockSpec((1, 2, TQ, TK), lambda hp, g: (_variant(g, ngx), hp, 0, 0))
    return q_spec, kv_spec, b_spec


def _na_fwd(q, k, v, bias, s, name):
    t = q.shape[0]
    ctx_len = t - s
    r_grid = s // GRID_W
    q_spec, kv_spec, b_spec = _na_specs(t, s // TQ)

    def body(q_ref, k_ref, v_ref, b_ref, o_ref):
        start = _key_start(pl.program_id(1), r_grid)
        qf = q_ref[...].astype(F32) * (HEAD_DIM ** -0.5)
        knb, vnb = k_ref[pl.ds(start, TK), :], v_ref[pl.ds(start, TK), :]
        kcx, vcx = k_ref[pl.ds(s, ctx_len), :], v_ref[pl.ds(s, ctx_len), :]
        acc = jnp.zeros((TQ, LANES), F32)
        for h in range(2):
            mask = _head_mask(h)
            e_nb, e_cx, inv = _softmax_parts((qf * mask).astype(MXU), knb, kcx, b_ref[0, h])
            acc += (_nn(e_nb.astype(MXU), vnb) + _nn(e_cx.astype(MXU), vcx)) * (inv * mask)
        o_ref[...] = acc.astype(o_ref.dtype)

    return _call(body, name=name, grid=(NA_WIDTH // LANES, t // TQ), in_specs=[q_spec, kv_spec, kv_spec, b_spec],
                 out_specs=q_spec, out_shape=_sds((t, NA_WIDTH), MXU))(q, k, v, bias)


def _na_bwd(q, k, v, do, bias, s, name):
    t = q.shape[0]
    ctx_len = t - s
    r_grid = s // GRID_W
    ng, ngx = t // TQ, s // TQ
    q_spec, kv_spec, b_spec = _na_specs(t, ngx)

    def body(q_ref, k_ref, v_ref, do_ref, b_ref, dq_ref, dk_hbm, dv_hbm, db_ref, dk_acc, dv_acc):
        hp, g = pl.program_id(0), pl.program_id(1)
        start = _key_start(g, r_grid)

        @pl.when(g == 0)
        def _():
            dk_acc[...] = jnp.zeros_like(dk_acc)
            dv_acc[...] = jnp.zeros_like(dv_acc)

        @pl.when((g == 0) | (g == 1) | (g == ngx - 1) | (g == ngx))
        def _():
            db_ref[...] = jnp.zeros_like(db_ref)

        qf = q_ref[...].astype(F32) * (HEAD_DIM ** -0.5)
        do = do_ref[...].astype(F32)
        knb, vnb = k_ref[pl.ds(start, TK), :], v_ref[pl.ds(start, TK), :]
        kcx, vcx = k_ref[pl.ds(s, ctx_len), :], v_ref[pl.ds(s, ctx_len), :]
        dq = jnp.zeros((TQ, LANES), F32)
        dk_nb = jnp.zeros((TK, LANES), F32)
        dv_nb = jnp.zeros((TK, LANES), F32)
        dk_cx = jnp.zeros((ctx_len, LANES), F32)
        dv_cx = jnp.zeros((ctx_len, LANES), F32)
        for h in range(2):
            mask = _head_mask(h)
            qm = (qf * mask).astype(MXU)
            dom = (do * mask).astype(MXU)
            e_nb, e_cx, inv = _softmax_parts(qm, knb, kcx, b_ref[0, h])
            dp_nb = _nt(dom, vnb)
            dp_cx = _nt(dom, vcx)
            delta = inv * (jnp.sum(e_nb * dp_nb, axis=-1, keepdims=True) + jnp.sum(e_cx * dp_cx, axis=-1, keepdims=True))
            ds_nb = e_nb * (inv * (dp_nb - delta))
            ds_cx = e_cx * (inv * (dp_cx - delta))
            db_ref[0, h] += ds_nb
            ds_nb, ds_cx = ds_nb.astype(MXU), ds_cx.astype(MXU)
            dq += (_nn(ds_nb, knb) + _nn(ds_cx, kcx)) * (mask * (HEAD_DIM ** -0.5))
            dk_nb += _tn(ds_nb, qm)
            dk_cx += _tn(ds_cx, qm)
            dom = (do * (inv * mask)).astype(MXU)
            dv_nb += _tn(e_nb.astype(MXU), dom)
            dv_cx += _tn(e_cx.astype(MXU), dom)
        dq_ref[...] = dq
        dk_acc[pl.ds(start, TK), :] += dk_nb
        dv_acc[pl.ds(start, TK), :] += dv_nb
        dk_acc[pl.ds(s, ctx_len), :] += dk_cx
        dv_acc[pl.ds(s, ctx_len), :] += dv_cx

        @pl.when(g == ng - 1)
        def _():
            pltpu.sync_copy(dk_acc, dk_hbm.at[hp])
            pltpu.sync_copy(dv_acc, dv_hbm.at[hp])

    n_pairs = NA_WIDTH // LANES
    hbm = pl.BlockSpec(memory_space=pl.ANY)
    return _call(body, name=name, grid=(n_pairs, ng), in_specs=[q_spec, kv_spec, kv_spec, q_spec, b_spec],
                 out_specs=[q_spec, hbm, hbm, b_spec],
                 out_shape=[_sds((t, NA_WIDTH), F32), _sds((n_pairs, t, LANES), F32), _sds((n_pairs, t, LANES), F32),
                            _sds((4, HEADS, TQ, TK), F32)],
                 scratch=[pltpu.VMEM((t, LANES), F32), pltpu.VMEM((t, LANES), F32)])(q, k, v, do, bias)


def _rpb_grad(dbias, r_grid, name):
    rows, _, _ = _na_geometry(r_grid)
    n_dr, half, skew, lanes = 2 * WIN_H - 1, WIN_W - 1, TK + 2, 896
    z = dbias[:3].reshape(3, HEADS, Q_ROWS, GRID_W, TK)
    z = jnp.pad(z, ((0, 0),) * 4 + ((0, 1),)).reshape(3, HEADS, Q_ROWS, GRID_W * (TK + 1))
    z = jnp.pad(z, ((0, 0),) * 3 + ((0, GRID_W),)).reshape(3, HEADS, Q_ROWS, GRID_W, skew)
    z = jnp.pad(z, ((0, 0),) * 4 + ((0, lanes - skew),))

    def body(z_ref, o_ref):
        sums = [jnp.sum(z_ref[v, 0, a], axis=0, keepdims=True) for v in range(3) for a in range(Q_ROWS)]
        zs = jnp.concatenate(sums + [jnp.zeros((16 - 3 * Q_ROWS, lanes), F32)], axis=0)
        acc = [jnp.zeros((1, lanes), F32) for _ in range(n_dr)]
        for i in range(K_ROWS):
            if i == 0:
                at0 = pltpu.roll(zs, lanes - (skew - half), 1) + pltpu.roll(zs, half, 1)
            else:
                at0 = pltpu.roll(zs, lanes - (i * GRID_W - half), 1)
            for v, (dr, vr) in enumerate(rows):
                for a in range(Q_ROWS):
                    if vr[a, i]:
                        acc[dr[a, i]] = acc[dr[a, i]] + at0[v * Q_ROWS + a:v * Q_ROWS + a + 1, :]
        o_ref[0] = jnp.concatenate(acc + [jnp.zeros((1, lanes), F32)], axis=0)

    o = _call(body, name=name, grid=(HEADS,),
              in_specs=[pl.BlockSpec((3, 1, Q_ROWS, GRID_W, lanes), lambda h: (0, h, 0, 0, 0))],
              out_specs=pl.BlockSpec((1, 16, lanes), lambda h: (h, 0, 0)), out_shape=_sds((HEADS, 16, lanes), F32))(z)
    return o[:, :n_dr, :2 * WIN_W - 1]


_GELU_K, _GELU_C = 0.7978845608028654, 0.044715


def _gelu(x):
    return 0.5 * x * (1.0 + jnp.tanh(_GELU_K * (x + _GELU_C * x * x * x)))


def _gelu_grad(x):
    th = jnp.tanh(_GELU_K * (x + _GELU_C * x * x * x))
    return 0.5 * (1.0 + th) + 0.5 * x * (1.0 - th * th) * (_GELU_K * (1.0 + 3.0 * _GELU_C * x * x))


def _ln_stats(v):
    mu = jnp.mean(v, axis=-1, keepdims=True)
    vc = v - mu
    rstd = lax.rsqrt(jnp.mean(vc * vc, axis=-1, keepdims=True) + EPS)
    return vc * rstd, rstd


def _gmlp(p, ln_g, ln_b, w_s, b_s, name):
    t = p.shape[0]
    te = _row_tile(t)
    w = SG_WIDTH
    cw = w // SG_GROUPS

    def body(u_ref, v_ref, g_ref, b_ref, ws_ref, bs_ref, o_ref):
        xhat, _ = _ln_stats(_gelu(v_ref[...].astype(F32)))
        vn = (xhat * g_ref[...] + b_ref[...]).astype(MXU)
        ug = _gelu(u_ref[...].astype(F32))
        for ci in range(te // SG_CHUNK):
            rs = slice(ci * SG_CHUNK, (ci + 1) * SG_CHUNK)
            for gi in range(SG_GROUPS):
                cs = slice(gi * cw, (gi + 1) * cw)
                sg = _nn(ws_ref[gi].astype(MXU), vn[rs, cs]) + bs_ref[gi]
                o_ref[rs, cs] = (ug[rs, cs] * sg).astype(o_ref.dtype)

    return _call(body, name=name, grid=(t // te,),
                 in_specs=[_rows(te, w, 3), _rows(te, w, 4), _fixed((1, w)), _fixed((1, w)),
                           _fixed((SG_GROUPS, SG_CHUNK, SG_CHUNK)), _fixed((SG_GROUPS, SG_CHUNK, 1))],
                 out_specs=_rows(te, w), out_shape=_sds((t, w), MXU))(p, p, ln_g, ln_b, w_s, b_s)


def _gmlp_bwd(p, dob, ln_g, ln_b, w_s, b_s, name):
    t = p.shape[0]
    te = _row_tile(t)
    w = SG_WIDTH
    cw = w // SG_GROUPS

    def body(u_ref, v_ref, do_ref, g_ref, b_ref, ws_ref, bs_ref, du_ref, dv_ref, dws_ref, dbs_ref, dg_ref, db_ref, dvn_ref):
        @pl.when(pl.program_id(0) == 0)
        def _():
            dws_ref[...] = jnp.zeros_like(dws_ref)
            dbs_ref[...] = jnp.zeros_like(dbs_ref)
            dg_ref[...] = jnp.zeros_like(dg_ref)
            db_ref[...] = jnp.zeros_like(db_ref)

        u, v = u_ref[...].astype(F32), v_ref[...].astype(F32)
        xhat, rstd = _ln_stats(_gelu(v))
        vn = (xhat * g_ref[...] + b_ref[...]).astype(MXU)
        ug = _gelu(u)
        dob = do_ref[...].astype(F32)
        for ci in range(te // SG_CHUNK):
            rs = slice(ci * SG_CHUNK, (ci + 1) * SG_CHUNK)
            for gi in range(SG_GROUPS):
                cs = slice(gi * cw, (gi + 1) * cw)
                wsg = ws_ref[gi].astype(MXU)
                sg = _nn(wsg, vn[rs, cs]) + bs_ref[gi]
                du_ref[rs, cs] = (dob[rs, cs] * sg * _gelu_grad(u[rs, cs])).astype(du_ref.dtype)
                ds = dob[rs, cs] * ug[rs, cs]
                dbs_ref[gi] += jnp.sum(ds, axis=-1, keepdims=True)
                ds = ds.astype(MXU)
                dws_ref[gi] += _nt(ds, vn[rs, cs])
                dvn_ref[rs, cs] = _tn(wsg, ds)
        dvn = dvn_ref[...]
        dg_ref[...] += jnp.sum(dvn * xhat, axis=0, keepdims=True)
        db_ref[...] += jnp.sum(dvn, axis=0, keepdims=True)
        dxh = dvn * g_ref[...]
        dvg = rstd * (dxh - jnp.mean(dxh, axis=-1, keepdims=True) - xhat * jnp.mean(dxh * xhat, axis=-1, keepdims=True))
        dv_ref[...] = (dvg * _gelu_grad(v)).astype(dv_ref.dtype)

    return _call(body, name=name, grid=(t // te,),
                 in_specs=[_rows(te, w, 3), _rows(te, w, 4), _rows(te, w), _fixed((1, w)), _fixed((1, w)),
                           _fixed((SG_GROUPS, SG_CHUNK, SG_CHUNK)), _fixed((SG_GROUPS, SG_CHUNK, 1))],
                 out_specs=[_rows(te, w), _rows(te, w), _fixed((SG_GROUPS, SG_CHUNK, SG_CHUNK)),
                            _fixed((SG_GROUPS, SG_CHUNK, 1)), _fixed((1, w)), _fixed((1, w))],
                 out_shape=[_sds((t, w), MXU), _sds((t, w), MXU), _sds((SG_GROUPS, SG_CHUNK, SG_CHUNK), F32),
                            _sds((SG_GROUPS, SG_CHUNK, 1), F32), _sds((1, w), F32), _sds((1, w), F32)],
                 scratch=[pltpu.VMEM((te, w), F32)])(p, p, dob, ln_g, ln_b, w_s, b_s)


def _merge(pa, pb, p, b_gate, name):
    t, d = pa.shape
    te = _row_tile(t)
    hw = NA_WIDTH
    nh = d // hw
    c0 = (NA_WIDTH * 3 + SG_WIDTH * 2) // hw

    def body(pa_ref, pb_ref, la_ref, lb_ref, ba_ref, bb_ref, o_ref):
        ga = jax.nn.sigmoid(la_ref[...].astype(F32) + ba_ref[...])
        gb = jax.nn.sigmoid(lb_ref[...].astype(F32) + bb_ref[...])
        o_ref[...] = (ga * pa_ref[...].astype(F32) + gb * pb_ref[...].astype(F32)).astype(o_ref.dtype)

    tile = pl.BlockSpec((te, hw), lambda i, j: (i, j))
    return _call(body, name=name, grid=(t // te, nh),
                 in_specs=[tile, tile, pl.BlockSpec((te, hw), lambda i, j: (i, c0 + j)),
                           pl.BlockSpec((te, hw), lambda i, j: (i, c0 + nh + j)),
                           pl.BlockSpec((1, hw), lambda i, j: (0, j)), pl.BlockSpec((1, hw), lambda i, j: (0, nh + j))],
                 out_specs=tile, out_shape=_sds((t, d), MXU))(pa, pb, p, p, b_gate, b_gate)


def _merge_bwd(dmg, pa, pb, p, b_gate, name):
    t, d = pa.shape
    te = _row_tile(t)
    hw = NA_WIDTH
    nh = d // hw
    c0 = (NA_WIDTH * 3 + SG_WIDTH * 2) // hw

    def body(dm_ref, pa_ref, pb_ref, la_ref, lb_ref, ba_ref, bb_ref, dpa_ref, dpb_ref, dla_ref, dlb_ref, dba_ref, dbb_ref):
        @pl.when(pl.program_id(1) == 0)
        def _():
            dba_ref[...] = jnp.zeros_like(dba_ref)
            dbb_ref[...] = jnp.zeros_like(dbb_ref)

        dm = dm_ref[...].astype(F32)
        ga = jax.nn.sigmoid(la_ref[...].astype(F32) + ba_ref[...])
        gb = jax.nn.sigmoid(lb_ref[...].astype(F32) + bb_ref[...])
        dpa_ref[...] = (dm * ga).astype(dpa_ref.dtype)
        dpb_ref[...] = (dm * gb).astype(dpb_ref.dtype)
        dla = dm * pa_ref[...].astype(F32) * ga * (1.0 - ga)
        dlb = dm * pb_ref[...].astype(F32) * gb * (1.0 - gb)
        dla_ref[...] = dla.astype(dla_ref.dtype)
        dlb_ref[...] = dlb.astype(dlb_ref.dtype)
        dba_ref[...] += jnp.sum(dla, axis=0, keepdims=True)
        dbb_ref[...] += jnp.sum(dlb, axis=0, keepdims=True)

    tile = pl.BlockSpec((te, hw), lambda j, i: (i, j))
    bias_a = pl.BlockSpec((1, hw), lambda j, i: (0, j))
    bias_b = pl.BlockSpec((1, hw), lambda j, i: (0, nh + j))
    return _call(body, name=name, grid=(nh, t // te),
                 in_specs=[tile, tile, tile, pl.BlockSpec((te, hw), lambda j, i: (i, c0 + j)),
                           pl.BlockSpec((te, hw), lambda j, i: (i, c0 + nh + j)), bias_a, bias_b],
                 out_specs=[tile, tile, tile, tile, bias_a, bias_a],
                 out_shape=[_sds((t, d), MXU)] * 4 + [_sds((1, d), F32)] * 2)(dmg, pa, pb, p, p, b_gate, b_gate)


def _final(xs, tgt, g, name):
    t, d = xs.shape
    nx = tgt.shape[0] // TM

    def body(x_ref, t_ref, g_ref, l_ref, dx_ref, dg_ref):
        i = pl.program_id(0)

        @pl.when(i == 0)
        def _():
            l_ref[...] = jnp.zeros_like(l_ref)
            dg_ref[...] = jnp.zeros_like(dg_ref)

        @pl.when(i < nx)
        def _():
            x = x_ref[...]
            rstd = lax.rsqrt(jnp.mean(x * x, axis=-1, keepdims=True) + EPS)
            xhat = x * rstd
            err = xhat * g_ref[...] - t_ref[...]
            l_ref[...] += 0.5 * jnp.sum(jnp.mean(err * err, axis=-1, keepdims=True))
            dy = err * (1.0 / d)
            dg_ref[...] += jnp.sum(dy * xhat, axis=0, keepdims=True)
            dxh = dy * g_ref[...]
            dx_ref[...] = rstd * (dxh - xhat * jnp.mean(dxh * xhat, axis=-1, keepdims=True))

        @pl.when(i >= nx)
        def _():
            dx_ref[...] = jnp.zeros_like(dx_ref)

    return _call(body, name=name, grid=(t // TM,),
                 in_specs=[_rows(TM, d), pl.BlockSpec((TM, d), lambda i: (jnp.minimum(i, nx - 1), 0)), _fixed((1, d))],
                 out_specs=[_fixed((1, LANES)), _rows(TM, d), _fixed((1, d))],
                 out_shape=[_sds((1, LANES), F32), _sds((t, d), F32), _sds((1, d), F32)])(xs, tgt, g)


def _view2d(a):
    return a.reshape(1, -1) if a.ndim == 1 else a.reshape(-1, a.shape[-1])


def _tile_rows(r, c):
    for cand in (1024, 512, 256, 128, 64, 32, 16):
        if r % cand == 0 and cand * c * 4 <= 2 ** 20:
            return cand
    return r


def _pair_sum(g, recv, layer, name):
    _, a, b = g.shape
    tr = _tile_rows(a, b)

    def body(l_ref, g_ref, r_ref, o32_ref, o16_ref):
        acc = g_ref[...] + r_ref[...]
        o32_ref[...] = acc
        o16_ref[...] = acc.astype(o16_ref.dtype)

    first = pl.BlockSpec((None, tr, b), lambda i, l: (0, i, 0))
    return pl.pallas_call(
        body, name=name, out_shape=[_sds((1, a, b), F32), _sds((1, a, b), MXU)],
        grid_spec=pltpu.PrefetchScalarGridSpec(
            num_scalar_prefetch=1, grid=(a // tr,),
            in_specs=[pl.BlockSpec((None, tr, b), lambda i, l: (l[0], i, 0)), first], out_specs=[first, first]),
        compiler_params=pltpu.CompilerParams(dimension_semantics=("arbitrary",), vmem_limit_bytes=VMEM_LIMIT))(layer, g, recv)


def _ew(fn, arrays, out_dtypes, name):
    shape = arrays[0].shape
    views = [_view2d(a) for a in arrays]
    r, c = views[0].shape
    tr = _tile_rows(r, c)

    def body(*refs):
        outs = fn(*[ref[...] for ref in refs[:len(views)]])
        for ref, o in zip(refs[len(views):], outs):
            ref[...] = o.astype(ref.dtype)

    res = _call(body, name=name, grid=(r // tr,), in_specs=[_rows(tr, c)] * len(views), out_specs=[_rows(tr, c)] * len(out_dtypes),
                out_shape=[_sds((r, c), dt) for dt in out_dtypes])(*views)
    return [o.reshape(shape) for o in res]


def _sum_pieces(pieces, name, out_dtypes=(F32,)):
    def fn(*vals):
        acc = vals[0].astype(F32)
        for v in vals[1:]:
            acc = acc + v.astype(F32)
        return (acc,) * len(out_dtypes)

    return _ew(fn, pieces, list(out_dtypes), name)


def _adamw(w, g_pieces, m, v, name):
    n_g = len(g_pieces)

    def fn(w_, *rest):
        g = rest[0]
        for piece in rest[1:n_g]:
            g = g + piece
        m_, v_ = rest[n_g], rest[n_g + 1]
        m2 = ADAM_B1 * m_ + (1.0 - ADAM_B1) * g
        v2 = ADAM_B2 * v_ + (1.0 - ADAM_B2) * (g * g)
        m_hat = m2 / (1.0 - ADAM_B1 ** ADAM_STEP)
        v_hat = v2 / (1.0 - ADAM_B2 ** ADAM_STEP)
        delta = -ADAM_LR * (m_hat / (jnp.sqrt(v_hat) + ADAM_EPS) + ADAM_WD * w_)
        return g, delta, m2, v2

    return _ew(fn, [w, *g_pieces, m, v], [F32] * 4, name)


def _ada_fwd(cond, w, b, name):
    r, d = cond.shape
    n = w.shape[1]
    tn = _pick(n, (1152, 768, 512, 384, 256, 128))

    def body(c_ref, w_ref, b_ref, o_ref, s_ref):
        c = c_ref[...]
        sc = c * jax.nn.sigmoid(c)
        s_ref[...] = sc
        o_ref[...] = _nn(sc.astype(MXU), w_ref[...].astype(MXU)) + b_ref[...]

    return _call(body, name=name, grid=(n // tn,),
                 in_specs=[_fixed((r, d)), pl.BlockSpec((d, tn), lambda j: (0, j)), pl.BlockSpec((1, tn), lambda j: (0, j))],
                 out_specs=[pl.BlockSpec((r, tn), lambda j: (0, j)), _fixed((r, d))],
                 out_shape=[_sds((r, n), F32), _sds((r, d), F32)])(cond, w, b)


def _cctx_grad(parts, c_ctx, name):
    n, d = parts.shape

    def body(p_ref, c_ref, o_ref):
        c = c_ref[...]
        sg = jax.nn.sigmoid(c)
        acc = p_ref[0:1, :]
        for j in range(1, n):
            acc = acc + p_ref[j:j + 1, :]
        o_ref[...] = acc * (sg * (1.0 + c * (1.0 - sg)))

    return _call(body, name=name, grid=(1,), in_specs=[_fixed((n, d)), _fixed((1, d))], out_specs=_fixed((1, d)),
                 out_shape=_sds((1, d), F32))(parts, c_ctx)


def _here():
    return lax.axis_index("x"), lax.axis_index("y"), lax.axis_index("c")


def _flip(v, bit):
    return 1 - v if bit else v


def _allgather8(xb, name):
    r, n = xb.shape

    def body(x_ref, out_ref, send_sems, recv_sems, local_sem):
        x, y, c = _here()
        me = 4 * x + 2 * y + c
        local = pltpu.make_async_copy(x_ref, out_ref.at[me], local_sem)
        local.start()
        sends = []
        for k in range(1, 8):
            peer = (_flip(x, k & 4), _flip(y, k & 2), _flip(c, k & 1))
            cp = pltpu.make_async_remote_copy(src_ref=x_ref, dst_ref=out_ref.at[me], send_sem=send_sems.at[k - 1],
                                              recv_sem=recv_sems.at[k - 1], device_id=peer, device_id_type=MESH)
            cp.start()
            sends.append(cp)
        for k in range(1, 8):
            peer = (_flip(x, k & 4), _flip(y, k & 2), _flip(c, k & 1))
            src = 4 * peer[0] + 2 * peer[1] + peer[2]
            pltpu.make_async_remote_copy(src_ref=x_ref, dst_ref=out_ref.at[src], send_sem=send_sems.at[k - 1],
                                         recv_sem=recv_sems.at[k - 1], device_id=peer, device_id_type=MESH).wait_recv()
        for cp in sends:
            cp.wait_send()
        local.wait()

    vmem = pl.BlockSpec(memory_space=pltpu.VMEM)
    return pl.pallas_call(
        body, name=name, out_shape=_sds((8, r, n), xb.dtype), in_specs=[vmem], out_specs=vmem,
        scratch_shapes=[pltpu.SemaphoreType.DMA((7,)), pltpu.SemaphoreType.DMA((7,)), pltpu.SemaphoreType.DMA(())],
        compiler_params=pltpu.CompilerParams(vmem_limit_bytes=VMEM_LIMIT))(xb)


def _shard_of(ref, axis, j, size):
    sl = pl.ds(j * size, size)
    return ref.at[:, sl, :] if axis == 1 else ref.at[:, :, sl]


def _piece(ref, axis, j, size, layer):
    lay, sl = pl.ds(layer, 1), pl.ds(j * size, size)
    return ref.at[lay, sl, :] if axis == 1 else ref.at[lay, :, sl]


def _gather_chips(shards, axes, name):
    n = len(shards)
    fulls = []
    for a, ax in zip(shards, axes):
        assert a.shape[0] == 2
        shp = list(a.shape)
        shp[ax] *= 4
        fulls.append(_sds(tuple(shp), a.dtype))

    def body(*refs):
        ins, outs = refs[:n], refs[n:2 * n]
        ici_send, ici_recv, d2d_send, d2d_recv, local_sems = refs[2 * n:]
        x, y, c = _here()
        chips = [(_flip(x, k & 2), _flip(y, k & 1)) for k in range(1, 4)]
        local, sends = [], []
        for a in range(n):
            size = ins[a].shape[axes[a]]
            cp = pltpu.make_async_copy(ins[a], _shard_of(outs[a], axes[a], 2 * x + y, size), local_sems.at[a])
            cp.start()
            local.append(cp)
            for j, (px, py) in enumerate(chips):
                cp = pltpu.make_async_remote_copy(src_ref=ins[a].at[pl.ds(c, 1)], dst_ref=_piece(outs[a], axes[a], 2 * x + y, size, c),
                                                  send_sem=ici_send.at[3 * a + j], recv_sem=ici_recv.at[3 * a + j],
                                                  device_id=(px, py, c), device_id_type=MESH)
                cp.start()
                sends.append(cp)
        for a in range(n):
            size = ins[a].shape[axes[a]]
            for j, (px, py) in enumerate(chips):
                landed = _piece(outs[a], axes[a], 2 * px + py, size, c)
                pltpu.make_async_remote_copy(src_ref=ins[a].at[pl.ds(c, 1)], dst_ref=landed, send_sem=ici_send.at[3 * a + j],
                                             recv_sem=ici_recv.at[3 * a + j], device_id=(px, py, c), device_id_type=MESH).wait_recv()
                cp = pltpu.make_async_remote_copy(src_ref=landed, dst_ref=landed, send_sem=d2d_send.at[3 * a + j],
                                                  recv_sem=d2d_recv.at[3 * a + j], device_id=(x, y, 1 - c), device_id_type=MESH)
                cp.start()
                sends.append(cp)
        for a in range(n):
            size = ins[a].shape[axes[a]]
            for j, (px, py) in enumerate(chips):
                passed = _piece(outs[a], axes[a], 2 * px + py, size, 1 - c)
                pltpu.make_async_remote_copy(src_ref=passed, dst_ref=passed, send_sem=d2d_send.at[3 * a + j],
                                             recv_sem=d2d_recv.at[3 * a + j], device_id=(x, y, 1 - c), device_id_type=MESH).wait_recv()
        for cp in sends:
            cp.wait_send()
        for cp in local:
            cp.wait()

    hbm = pl.BlockSpec(memory_space=pl.ANY)
    return pl.pallas_call(
        body, name=name, out_shape=fulls, in_specs=[hbm] * n, out_specs=[hbm] * n,
        scratch_shapes=[pltpu.SemaphoreType.DMA((3 * n,))] * 4 + [pltpu.SemaphoreType.DMA((n,))])(*shards)


def _swap_layers(arrays, name):
    n = len(arrays)

    def body(*refs):
        ins, outs = refs[:n], refs[n:2 * n]
        send_sems, recv_sems = refs[2 * n:]
        x, y, c = _here()
        copies = []
        for a in range(n):
            cp = pltpu.make_async_remote_copy(src_ref=ins[a].at[pl.ds(1 - c, 1)], dst_ref=outs[a], send_sem=send_sems.at[a],
                                              recv_sem=recv_sems.at[a], device_id=(x, y, 1 - c), device_id_type=MESH)
            cp.start()
            copies.append(cp)
        for cp in copies:
            cp.wait()

    hbm = pl.BlockSpec(memory_space=pl.ANY)
    return pl.pallas_call(
        body, name=name, out_shape=[_sds((1, *a.shape[1:]), a.dtype) for a in arrays], in_specs=[hbm] * n, out_specs=[hbm] * n,
        scratch_shapes=[pltpu.SemaphoreType.DMA((n,)), pltpu.SemaphoreType.DMA((n,))])(*arrays)


def _scatter_chips(fulls, axes, name):
    n = len(fulls)
    recvs = []
    for a, ax in zip(fulls, axes):
        shp = list(a.shape)
        shp[ax] //= 4
        recvs.append(_sds((3, *shp), a.dtype))

    def body(*refs):
        ins, outs = refs[:n], refs[n:2 * n]
        send_sems, recv_sems = refs[2 * n:]
        x, y, c = _here()
        sends = []
        for a in range(n):
            size = ins[a].shape[axes[a]] // 4
            for k in range(1, 4):
                peer = (_flip(x, k & 2), _flip(y, k & 1), c)
                cp = pltpu.make_async_remote_copy(src_ref=_shard_of(ins[a], axes[a], 2 * peer[0] + peer[1], size),
                                                  dst_ref=outs[a].at[k - 1],
                                                  send_sem=send_sems.at[3 * a + k - 1], recv_sem=recv_sems.at[3 * a + k - 1],
                                                  device_id=peer, device_id_type=MESH)
                cp.start()
                sends.append(cp)
        for cp in sends:
            cp.wait_recv()
        for cp in sends:
            cp.wait_send()

    hbm = pl.BlockSpec(memory_space=pl.ANY)
    return pl.pallas_call(
        body, name=name, out_shape=recvs, in_specs=[hbm] * n, out_specs=[hbm] * n,
        scratch_shapes=[pltpu.SemaphoreType.DMA((3 * n,)), pltpu.SemaphoreType.DMA((3 * n,))])(*fulls)


def _join_layers(arrays, name):
    n = len(arrays)

    def body(*refs):
        ins, outs = refs[:n], refs[n:2 * n]
        send_sems, recv_sems, local_sems = refs[2 * n:]
        x, y, c = _here()
        started = []
        for a in range(n):
            local = pltpu.make_async_copy(ins[a], outs[a].at[pl.ds(c, 1)], local_sems.at[a])
            local.start()
            send = pltpu.make_async_remote_copy(src_ref=ins[a], dst_ref=outs[a].at[pl.ds(c, 1)], send_sem=send_sems.at[a],
                                                recv_sem=recv_sems.at[a], device_id=(x, y, 1 - c), device_id_type=MESH)
            send.start()
            started.append((local, send))
        for a, (local, send) in enumerate(started):
            pltpu.make_async_remote_copy(src_ref=ins[a], dst_ref=outs[a].at[pl.ds(1 - c, 1)], send_sem=send_sems.at[a],
                                         recv_sem=recv_sems.at[a], device_id=(x, y, 1 - c), device_id_type=MESH).wait_recv()
            send.wait_send()
            local.wait()

    hbm = pl.BlockSpec(memory_space=pl.ANY)
    return pl.pallas_call(
        body, name=name, out_shape=[_sds((2, *a.shape[1:]), a.dtype) for a in arrays], in_specs=[hbm] * n, out_specs=[hbm] * n,
        scratch_shapes=[pltpu.SemaphoreType.DMA((n,))] * 3)(*arrays)


def _ffn_fwd(xs, g, mods, k0, w_up, w_down, s, tag):
    h, ua, ub, act = _norm_mm(xs, g, mods, k0, k0 + 1, w_up, s, tag + "_up", True)
    y, xn = _mm_res(act, w_down, xs, mods, k0 + 2, 0.5, s, tag + "_down")
    return xn, (xs, h, ua, ub, act, y)


def _dw(gw, key, a, b, name, col0=0, n_total=None):
    shape = (gw["depth"], a.shape[1], n_total or b.shape[1])
    gw[key] = _mm(a, b, "tn", F32, name, into=(gw.get(key), shape, gw["layer"], col0))


def _ffn_bwd(dxn, saved, g, mods, k0, w_up, w_down, s, tag, gw, up_key, down_key):
    xs, h, ua, ub, act, y = saved
    dy, dgate, dua, dub = _resb_mm(dxn, y, mods, k0 + 2, 0.5, w_down, s, tag + "_down_dx", (ua, ub))
    _dw(gw, down_key, act, dy, tag + "_down_dw")
    f = dua.shape[1]
    _dw(gw, up_key, h, dua, tag + "_upa_dw", 0, 2 * f)
    _dw(gw, up_key, h, dub, tag + "_upb_dw", f, 2 * f)
    dx, dsh, dsc, dg = _mm_normb([dua, dub], w_up, xs, dxn, g, mods, k0 + 1, s, tag + "_up_dx")
    return dx, dg, [dsh, dsc, dgate]


def _mix_fwd(xs, g, mods, wl, pl_, tabs, s, tag):
    cos, sin = tabs
    h, p = _norm_mm(xs, g, mods, 3, 4, wl["w_in"], s, tag + "_in", False)
    q, k, v = _rope(p, cos, sin, tag + "_rope")
    bias = _bias_table(pl_["rpb"], s // GRID_W)
    oa = _na_fwd(q, k, v, bias, s, tag + "_na")
    ob = _gmlp(p, pl_["ln_v_g"], pl_["ln_v_b"], pl_["w_s"], pl_["b_s"], tag + "_sg")
    pa = _mm(oa, wl["w_pa"], "nn", MXU, tag + "_pa")
    pb = _mm(ob, wl["w_pb"], "nn", MXU, tag + "_pb")
    mg = _merge(pa, pb, p, pl_["b_gate"], tag + "_merge")
    y, xn = _mm_res(mg, wl["w_o"], xs, mods, 5, 1.0, s, tag + "_o")
    return xn, (xs, h, p, q, k, v, bias, oa, ob, pa, pb, mg, y)


def _mix_bwd(dxn, saved, g, mods, wl, pl_, tabs, s, tag, gw):
    xs, h, p, q, k, v, bias, oa, ob, pa, pb, mg, y = saved
    cos, sin = tabs
    gp = {}
    dy, dgate, dmg = _resb_mm(dxn, y, mods, 5, 1.0, wl["w_o"], s, tag + "_o_dx")
    _dw(gw, "w_o", mg, dy, tag + "_o_dw")
    dpa, dpb, dla, dlb, dba, dbb = _merge_bwd(dmg, pa, pb, p, pl_["b_gate"], tag + "_merge_b")
    gp["b_gate"] = jnp.concatenate([dba, dbb], axis=1)
    _dw(gw, "w_pa", oa, dpa, tag + "_pa_dw")
    doa = _mm(dpa, wl["w_pa"], "nt", MXU, tag + "_pa_dx")
    _dw(gw, "w_pb", ob, dpb, tag + "_pb_dw")
    dob = _mm(dpb, wl["w_pb"], "nt", MXU, tag + "_pb_dx")
    du, dvs, gp["w_s"], gp["b_s"], gp["ln_v_g"], gp["ln_v_b"] = _gmlp_bwd(
        p, dob, pl_["ln_v_g"], pl_["ln_v_b"], pl_["w_s"], pl_["b_s"], tag + "_sg_b")
    dqr, dkr, dv, dbias = _na_bwd(q, k, v, doa, bias, s, tag + "_na_b")
    gp["rpb"] = _rpb_grad(dbias, s // GRID_W, tag + "_rpb")
    dq, dk, dvv = _rope_bwd(dqr, dkr, dv, cos, sin, tag + "_rope_b")
    dp = jnp.concatenate([dq, dk, dvv, du, dvs, dla, dlb], axis=1)
    _dw(gw, "w_in", h, dp, tag + "_in_dw")
    dx, dsh, dsc, dg = _mm_normb([dp], wl["w_in"], xs, dxn, g, mods, 4, s, tag + "_in_dx")
    return dx, gp, dg, [dsh, dsc, dgate]


def _local_step(x, ctx, tgt, mods, wts, prm):
    s, d = x.shape
    depth = mods.shape[0]
    tabs = _rope_tables(s, ctx.shape[0])
    xs = jnp.concatenate([x, ctx], axis=0)
    saved = []
    for l in range(depth):
        wl = {k: (v, l) for k, v in wts.items()}
        pl_ = _layer_params(prm, l)
        xs, s1 = _ffn_fwd(xs, pl_["g"][0], mods[l], 0, wl["w_ff1_up"], wl["w_ff1_down"], s, f"l{l}_ff1")
        xs, s2 = _mix_fwd(xs, pl_["g"][1], mods[l], wl, pl_, tabs, s, f"l{l}_mix")
        xs, s3 = _ffn_fwd(xs, pl_["g"][2], mods[l], 6, wl["w_ff2_up"], wl["w_ff2_down"], s, f"l{l}_ff2")
        saved.append((s1, s2, s3))
    loss, dxs, d_final_g = _final(xs, tgt, prm["final_g"].reshape(1, d), "final")
    gw = {"depth": depth}
    gp = {k: [None] * depth for k in ("norm_g", "b_gate", "rpb", "ln_v_g", "ln_v_b", "w_s", "b_s")}
    dmods = [None] * depth
    for l in reversed(range(depth)):
        wl = {k: (v, l) for k, v in wts.items()}
        pl_ = _layer_params(prm, l)
        s1, s2, s3 = saved[l]
        gw["layer"] = l
        dxs, dg2, dm2 = _ffn_bwd(dxs, s3, pl_["g"][2], mods[l], 6, wl["w_ff2_up"], wl["w_ff2_down"], s, f"l{l}_ff2",
                                 gw, "w_ff2_up", "w_ff2_down")
        dxs, gpm, dg1, dm1 = _mix_bwd(dxs, s2, pl_["g"][1], mods[l], wl, pl_, tabs, s, f"l{l}_mix", gw)
        dxs, dg0, dm0 = _ffn_bwd(dxs, s1, pl_["g"][0], mods[l], 0, wl["w_ff1_up"], wl["w_ff1_down"], s, f"l{l}_ff1",
                                 gw, "w_ff1_up", "w_ff1_down")
        gp["b_gate"][l] = gpm["b_gate"][0]
        gp["rpb"][l] = gpm["rpb"]
        gp["ln_v_g"][l] = gpm["ln_v_g"][0]
        gp["ln_v_b"][l] = gpm["ln_v_b"][0]
        gp["w_s"][l] = gpm["w_s"]
        gp["b_s"][l] = gpm["b_s"][..., 0]
        gp["norm_g"][l] = jnp.concatenate([dg0, dg1, dg2], axis=0)
        dmods[l] = jnp.concatenate(dm0 + dm1 + dm2, axis=1)
    gw = {k: gw[k] for k in wts}
    gp = {k: jnp.stack(v) for k, v in gp.items()}
    gp["final_g"] = d_final_g[0]
    return loss[0, 0], dxs[:s], jnp.stack(dmods), gw, gp


def _layer_params(prm, l):
    d = prm["norm_g"].shape[-1]
    return {
        "g": [prm["norm_g"][l, i].reshape(1, d) for i in range(3)],
        "b_gate": prm["b_gate"][l].reshape(1, -1),
        "rpb": prm["rpb"][l],
        "ln_v_g": prm["ln_v_g"][l].reshape(1, -1),
        "ln_v_b": prm["ln_v_b"][l].reshape(1, -1),
        "w_s": prm["w_s"][l],
        "b_s": prm["b_s"][l][..., None],
    }


SMALL = ("norm_g", "b_gate", "rpb", "ln_v_g", "ln_v_b", "w_s", "b_s", "final_g")
PACK_LANES = 1024


def _pack(parts):
    flat = jnp.concatenate([p.reshape(-1) for p in parts])
    rows = -(-flat.shape[0] // PACK_LANES)
    rows = -(-rows // 8) * 8
    return jnp.pad(flat, (0, rows * PACK_LANES - flat.shape[0])).reshape(rows, PACK_LANES)


def _unpack(flat, shapes):
    out, off = [], 0
    for shp in shapes:
        n = int(np.prod(shp))
        out.append(flat[..., off:off + n].reshape(*flat.shape[:-1], *shp))
        off += n
    return out


def kernel(x, c, ctx, c_ctx, w_ada, b_ada, norm_g, w_ff1_up, w_ff1_down, w_in, b_gate, rpb, ln_v_g, ln_v_b, w_s, b_s, w_pa, w_pb, w_o, w_ff2_up, w_ff2_down, final_g, loss_target, m_c_ctx, m_w_ada, m_b_ada, m_norm_g, m_w_ff1_up, m_w_ff1_down, m_w_in, m_b_gate, m_rpb, m_ln_v_g, m_ln_v_b, m_w_s, m_b_s, m_w_pa, m_w_pb, m_w_o, m_w_ff2_up, m_w_ff2_down, m_final_g, v_c_ctx, v_w_ada, v_b_ada, v_norm_g, v_w_ff1_up, v_w_ff1_down, v_w_in, v_b_gate, v_rpb, v_ln_v_g, v_ln_v_b, v_w_s, v_b_s, v_w_pa, v_w_pb, v_w_o, v_w_ff2_up, v_w_ff2_down, v_final_g):
    weights = dict(c_ctx=c_ctx, w_ada=w_ada, b_ada=b_ada, norm_g=norm_g, w_ff1_up=w_ff1_up, w_ff1_down=w_ff1_down, w_in=w_in,
                   b_gate=b_gate, rpb=rpb, ln_v_g=ln_v_g, ln_v_b=ln_v_b, w_s=w_s, b_s=b_s, w_pa=w_pa, w_pb=w_pb, w_o=w_o,
                   w_ff2_up=w_ff2_up, w_ff2_down=w_ff2_down, final_g=final_g)
    mom_m = dict(c_ctx=m_c_ctx, w_ada=m_w_ada, b_ada=m_b_ada, norm_g=m_norm_g, w_ff1_up=m_w_ff1_up, w_ff1_down=m_w_ff1_down,
                 w_in=m_w_in, b_gate=m_b_gate, rpb=m_rpb, ln_v_g=m_ln_v_g, ln_v_b=m_ln_v_b, w_s=m_w_s, b_s=m_b_s, w_pa=m_w_pa,
                 w_pb=m_w_pb, w_o=m_w_o, w_ff2_up=m_w_ff2_up, w_ff2_down=m_w_ff2_down, final_g=m_final_g)
    mom_v = dict(c_ctx=v_c_ctx, w_ada=v_w_ada, b_ada=v_b_ada, norm_g=v_norm_g, w_ff1_up=v_w_ff1_up, w_ff1_down=v_w_ff1_down,
                 w_in=v_w_in, b_gate=v_b_gate, rpb=v_rpb, ln_v_g=v_ln_v_g, ln_v_b=v_ln_v_b, w_s=v_w_s, b_s=v_b_s, w_pa=v_w_pa,
                 w_pb=v_w_pb, w_o=v_w_o, w_ff2_up=v_w_ff2_up, w_ff2_down=v_w_ff2_down, final_g=v_final_g)
    order = list(weights)
    mx, my, mc = _here()
    dev = 4 * mx + 2 * my + mc
    chip = 2 * mx + my
    depth, d, n_ada = w_ada.shape
    dq = d // 4

    c_all = _allgather8(jnp.pad(c, ((0, 7), (0, 0))), "gather_c")[:, 0, :]
    cond = jnp.concatenate([c_all, c_ctx[None, :], jnp.zeros((7, d), F32)], axis=0)
    b_shard = lax.dynamic_slice(b_ada, (0, chip * n_ada), (depth, n_ada))
    proj = [_ada_fwd(cond, w_ada[l], b_shard[l:l + 1], f"ada{l}") for l in range(depth)]
    silu_c = proj[0][1]
    mods_sh = _allgather8(jnp.concatenate([p[0] for p in proj], axis=0), "gather_mods")
    mods_all = jnp.transpose(mods_sh[0::2].reshape(4, depth, 16, n_ada), (1, 2, 0, 3)).reshape(depth, 16, N_MOD, d)
    mods = jnp.stack([lax.dynamic_index_in_dim(mods_all, dev, axis=1, keepdims=False), mods_all[:, 8]], axis=1)

    full = _gather_chips([weights[k].astype(MXU) for k in BIG], [SHARD_AXIS[k] for k in BIG], "gather_w")
    wts = dict(zip(BIG, full))
    prm = {k: weights[k] for k in SMALL if k != "norm_g"}
    norm_full = _allgather8(jnp.pad(norm_g.reshape(depth * 3, dq), ((0, 8 - depth * 3), (0, 0))), "gather_norm_g")
    prm["norm_g"] = jnp.transpose(norm_full[0::2, :depth * 3].reshape(4, depth, 3, dq), (1, 2, 0, 3)).reshape(depth, 3, d)

    loss, grad_x, dmods, gw, gp = _local_step(x[0], ctx[0], loss_target[0], mods, wts, prm)
    loss = lax.psum(loss, ("x", "y", "c"))

    small_shapes = [(depth, 2, N_MOD * d)] + [weights[k].shape if k != "norm_g" else (depth, 3, d) for k in SMALL]
    packed = _allgather8(_pack([dmods.reshape(depth, 2, N_MOD * d)] + [gp[k] for k in SMALL]), "gather_small")
    rows = packed.shape[1]
    total = _sum_pieces([packed[i] for i in range(8)], "sum_small")[0].reshape(-1)
    sums = dict(zip(("dmods",) + SMALL, _unpack(total, small_shapes)))
    dmods_dev = _unpack(packed.reshape(8, rows * PACK_LANES), small_shapes[:1])[0]

    g_ada, cc_parts = [], []
    for l in range(depth):
        dm = jnp.concatenate([dmods_dev[:, l, 0], sums["dmods"][l, 1][None], jnp.zeros((7, N_MOD * d), F32)], axis=0)
        dm_sh = lax.dynamic_slice(dm, (0, chip * n_ada), (16, n_ada))
        g_ada.append(_mm(silu_c, dm_sh, "tn", F32, f"ada{l}_dw"))
        cc_parts.append(_mm(dm_sh, w_ada[l], "nt", F32, f"ada{l}_dc")[8:9])
    cc_all = _allgather8(jnp.pad(jnp.concatenate(cc_parts, axis=0), ((0, 8 - depth), (0, 0))), "gather_cctx")
    g_cctx = _cctx_grad(cc_all[0::2, :depth].reshape(4 * depth, d), c_ctx.reshape(1, d), "cctx_grad")

    axes = [SHARD_AXIS[k] for k in BIG]
    from_sibling = _swap_layers([gw[k] for k in BIG], "swap_layer_gw")
    my_layer = jnp.reshape(mc, (1,)).astype(jnp.int32)
    pair = [_pair_sum(gw[k], r, my_layer, "pair_" + k) for k, r in zip(BIG, from_sibling)]
    recv = _scatter_chips([p[1] for p in pair], axes, "scatter_gw")
    mine = []
    for k, ax, p, r in zip(BIG, axes, pair, recv):
        size = p[0].shape[ax] // 4
        own = lax.dynamic_slice_in_dim(p[0], chip * size, size, axis=ax)
        mine.append(_sum_pieces([own, r[0], r[1], r[2]], "sum_" + k)[0])
    joined = _join_layers(mine, "join_gw")

    pieces = {k: [g] for k, g in zip(BIG, joined)}
    pieces["w_ada"] = [jnp.stack(g_ada)]
    pieces["b_ada"] = [sums["dmods"][:, 0], sums["dmods"][:, 1]]
    pieces["c_ctx"] = [g_cctx[0]]
    for k in SMALL:
        pieces[k] = [sums[k]]
    pieces["norm_g"] = [lax.dynamic_slice_in_dim(sums["norm_g"], chip * dq, dq, axis=2)]
    res = {k: _adamw(weights[k], pieces[k], mom_m[k], mom_v[k], "adamw_" + k) for k in order}
    return (loss, grad_x[None], *[res[k][0] for k in order], *[res[k][1] for k in order],
            *[res[k][2] for k in order], *[res[k][3] for k in order])
```

```python
import numpy as np
import jax
import jax.numpy as jnp
from jax import lax
from jax.experimental import pallas as pl
from jax.experimental.pallas import tpu as pltpu

F32 = jnp.float32
MXU = jnp.bfloat16
EPS = 1e-6
GRID_W, HEADS, HEAD_DIM = 64, 8, 64
NA_WIDTH = SG_WIDTH = 512
WIN_H, WIN_W = 8, 16
SG_CHUNK, SG_GROUPS = 128, 4
N_MOD = 9
ROPE_THETA = 10000.0
Q_ROWS, K_ROWS = 4, 12
TQ, TK = Q_ROWS * GRID_W, K_ROWS * GRID_W
TM = 256
LANES = 128
NEG = -1e30
VMEM_LIMIT = 56 * 2 ** 20
ADAM_LR, ADAM_B1, ADAM_B2, ADAM_EPS, ADAM_WD, ADAM_STEP = 0.001, 0.9, 0.999, 1e-08, 0.01, 10
MESH = pl.DeviceIdType.MESH
BIG = ("w_ff1_up", "w_ff1_down", "w_in", "w_pa", "w_pb", "w_o", "w_ff2_up", "w_ff2_down")
SHARD_AXIS = {"w_ff1_up": 2, "w_ff1_down": 1, "w_in": 2, "w_pa": 2, "w_pb": 2, "w_o": 1, "w_ff2_up": 2, "w_ff2_down": 1}


def _call(body, *, name, grid, in_specs, out_specs, out_shape, scratch=(), aliases=None):
    return pl.pallas_call(
        body, name=name, grid=grid, in_specs=in_specs, out_specs=out_specs, out_shape=out_shape,
        scratch_shapes=list(scratch), input_output_aliases=aliases or {},
        compiler_params=pltpu.CompilerParams(dimension_semantics=("arbitrary",) * len(grid), vmem_limit_bytes=VMEM_LIMIT))


def _w_dims(w):
    return w[0].shape[1:] if isinstance(w, tuple) else w.shape


def _w_arr(w):
    return w[0] if isinstance(w, tuple) else w


def _w_spec(w, block, index):
    if isinstance(w, tuple):
        layer = w[1]
        return pl.BlockSpec((None, *block), lambda *ids: (layer, *index(*ids)))
    return pl.BlockSpec(block, index)


def _pick(n, prefs):
    for p in prefs:
        if n % p == 0:
            return p
    return n


def _row_tile(t):
    return _pick(t, (640, 256))


def _rows(tm, n, col=0):
    return pl.BlockSpec((tm, n), lambda i: (i, col))


def _fixed(shape):
    return pl.BlockSpec(shape, lambda *_: (0,) * len(shape))


def _sds(shape, dtype):
    return jax.ShapeDtypeStruct(shape, dtype)


def _mm(a, b, mode, out_dtype, name, into=None):
    if mode == "tn":
        r, m = a.shape
        n = b.shape[1]
        tm = _pick(m, (1024, 1408, 704, 512, 256, 128))
        tn = _pick(n, (512, 1408, 256, 128))
        tr = _pick(r, (1280, 640, 512, 256, 128))

        def body(a_ref, b_ref, *rest):
            o_ref = rest[-1]

            @pl.when(pl.program_id(2) == 0)
            def _():
                o_ref[...] = jnp.zeros_like(o_ref)

            o_ref[...] += lax.dot_general(a_ref[...].astype(MXU), b_ref[...].astype(MXU), (((0,), (0,)), ((), ())),
                                          preferred_element_type=F32)

        in_specs = [pl.BlockSpec((tr, tm), lambda i, j, k: (k, i)), pl.BlockSpec((tr, tn), lambda i, j, k: (k, j))]
        if into is None:
            return _call(body, name=name, grid=(m // tm, n // tn, r // tr), in_specs=in_specs,
                         out_specs=pl.BlockSpec((tm, tn), lambda i, j, k: (i, j)), out_shape=_sds((m, n), F32))(a, b)
        buf, shape, layer, col0 = into
        out_spec = pl.BlockSpec((None, tm, tn), lambda i, j, k: (layer, i, j + col0 // tn))
        if buf is None:
            return _call(body, name=name, grid=(m // tm, n // tn, r // tr), in_specs=in_specs, out_specs=out_spec,
                         out_shape=_sds(shape, F32))(a, b)
        return _call(body, name=name, grid=(m // tm, n // tn, r // tr), in_specs=in_specs + [pl.BlockSpec(memory_space=pl.ANY)],
                     out_specs=out_spec, out_shape=_sds(shape, F32), aliases={2: 0})(a, b, buf)
    m, k = a.shape
    n = _w_dims(b)[1] if mode == "nn" else _w_dims(b)[0]
    tm = _pick(m, (1280, 640, 512, 256, 128) if k <= 2816 else (640, 512, 256, 128))
    tn = _pick(n, (512, 1408, 256, 128))
    dims = (((1,), (0,)), ((), ())) if mode == "nn" else (((1,), (1,)), ((), ()))

    def body(a_ref, b_ref, o_ref):
        o_ref[...] = lax.dot_general(a_ref[...].astype(MXU), b_ref[...].astype(MXU), dims,
                                     preferred_element_type=F32).astype(o_ref.dtype)

    b_spec = _w_spec(b, (k, tn), lambda i, j: (0, j)) if mode == "nn" else _w_spec(b, (tn, k), lambda i, j: (j, 0))
    return _call(body, name=name, grid=(m // tm, n // tn), in_specs=[pl.BlockSpec((tm, k), lambda i, j: (i, 0)), b_spec],
                 out_specs=pl.BlockSpec((tm, tn), lambda i, j: (i, j)), out_shape=_sds((m, n), out_dtype))(a, _w_arr(b))


def _row_chunks(tm):
    rc = _pick(tm, (256, 128))
    return [slice(r, r + rc) for r in range(0, tm, rc)]


def _ctx_rows(i, tm, s):
    return (i * tm + lax.broadcasted_iota(jnp.int32, (tm, 1), 0)) >= s


def _mod_row(m_ref, k, ctx):
    return jnp.where(ctx, m_ref[1, k:k + 1, :], m_ref[0, k:k + 1, :])


def _stream_sums(i, tm, s, refs_and_vals):
    @pl.when((i + 1) * tm <= s)
    def _():
        for ref, val in refs_and_vals:
            ref[0] += jnp.sum(val, axis=0, keepdims=True)

    @pl.when((i + 1) * tm > s)
    def _():
        ctx = _ctx_rows(i, tm, s)
        for ref, val in refs_and_vals:
            ref[0] += jnp.sum(jnp.where(ctx, 0.0, val), axis=0, keepdims=True)
            ref[1] += jnp.sum(jnp.where(ctx, val, 0.0), axis=0, keepdims=True)


def _norm_mm(xs, g, mods, k_shift, k_scale, w, s, name, glu):
    t, d = xs.shape
    n = _w_dims(w)[1] // 2 if glu else _w_dims(w)[1]
    tm = _pick(t, (1280, 640, 256))
    tn = _pick(n, (256, 128)) if glu else _pick(n, (512, 256, 128))
    nj = n // tn

    def body(x_ref, g_ref, m_ref, *refs):
        i, j = pl.program_id(0), pl.program_id(1)
        w_refs, h_ref, o_refs = refs[:2 if glu else 1], refs[2 if glu else 1], refs[3 if glu else 2:]

        @pl.when(j == 0)
        def _():
            x = x_ref[...]
            rstd = lax.rsqrt(jnp.mean(x * x, axis=-1, keepdims=True) + EPS)
            ctx = _ctx_rows(i, tm, s)
            h = x * rstd * g_ref[...] * (1.0 + _mod_row(m_ref, k_scale, ctx)) + _mod_row(m_ref, k_shift, ctx)
            h_ref[...] = h.astype(h_ref.dtype)

        for rows in _row_chunks(tm):
            h = h_ref[rows, :]
            a = _nn(h, w_refs[0][...])
            o_refs[0][rows, :] = a.astype(o_refs[0].dtype)
            if glu:
                b = _nn(h, w_refs[1][...])
                o_refs[1][rows, :] = b.astype(o_refs[1].dtype)
                o_refs[2][rows, :] = (a * jax.nn.sigmoid(a) * b).astype(o_refs[2].dtype)

    tile = pl.BlockSpec((tm, tn), lambda i, j: (i, j))
    row = pl.BlockSpec((tm, d), lambda i, j: (i, 0))
    w_specs = [_w_spec(w, (d, tn), lambda i, j: (0, j))] + ([_w_spec(w, (d, tn), lambda i, j: (0, j + nj))] if glu else [])
    n_out = 3 if glu else 1
    return _call(body, name=name, grid=(t // tm, nj),
                 in_specs=[row, _fixed((1, d)), _fixed((2, N_MOD, d))] + w_specs,
                 out_specs=[row] + [tile] * n_out,
                 out_shape=[_sds((t, d), MXU)] + [_sds((t, n), MXU)] * n_out)(xs, g, mods, *([_w_arr(w)] * (2 if glu else 1)))


def _mm_res(a, w, xs, mods, k_gate, coef, s, name):
    t, k = a.shape
    d = _w_dims(w)[1]
    tm = _pick(t, (1280, 640, 256))
    tn = _pick(d, (512, 256, 128))

    def body(a_ref, w_ref, x_ref, m_ref, y_ref, o_ref):
        y = _nn(a_ref[...], w_ref[...])
        y_ref[...] = y.astype(y_ref.dtype)
        gate = _mod_row(m_ref, k_gate, _ctx_rows(pl.program_id(0), tm, s))
        o_ref[...] = x_ref[...] + (coef * gate) * y

    tile = pl.BlockSpec((tm, tn), lambda i, j: (i, j))
    return _call(body, name=name, grid=(t // tm, d // tn),
                 in_specs=[pl.BlockSpec((tm, k), lambda i, j: (i, 0)), _w_spec(w, (k, tn), lambda i, j: (0, j)), tile,
                           pl.BlockSpec((2, N_MOD, tn), lambda i, j: (0, 0, j))],
                 out_specs=[tile, tile], out_shape=[_sds((t, d), MXU), _sds((t, d), F32)])(a, _w_arr(w), xs, mods)


def _resb_mm(dxn, y, mods, k_gate, coef, w, s, name, ups=None):
    t, d = dxn.shape
    n = _w_dims(w)[0]
    tm = _pick(t, (1280, 640, 256))
    tn = _pick(n, (256, 128)) if ups else _pick(n, (512, 256, 128))

    def body(dx_ref, y_ref, m_ref, w_ref, *refs):
        i, j = pl.program_id(0), pl.program_id(1)
        u_refs, (dy_ref, dgt_ref), o_refs = (refs[:2], refs[2:4], refs[4:]) if ups else ((), refs[:2], refs[2:])

        @pl.when((i == 0) & (j == 0))
        def _():
            dgt_ref[...] = jnp.zeros_like(dgt_ref)

        @pl.when(j == 0)
        def _():
            dx = dx_ref[...]
            gate = _mod_row(m_ref, k_gate, _ctx_rows(i, tm, s))
            dy_ref[...] = ((coef * gate) * dx).astype(dy_ref.dtype)
            _stream_sums(i, tm, s, [(dgt_ref, coef * y_ref[...].astype(F32) * dx)])

        dact = _nt(dy_ref[...], w_ref[...])
        if ups:
            a, b = u_refs[0][...].astype(F32), u_refs[1][...].astype(F32)
            sg = jax.nn.sigmoid(a)
            o_refs[0][...] = (dact * b * sg * (1.0 + a * (1.0 - sg))).astype(o_refs[0].dtype)
            o_refs[1][...] = (dact * a * sg).astype(o_refs[1].dtype)
        else:
            o_refs[0][...] = dact.astype(o_refs[0].dtype)

    tile = pl.BlockSpec((tm, tn), lambda i, j: (i, j))
    row = pl.BlockSpec((tm, d), lambda i, j: (i, 0))
    n_out = 2 if ups else 1
    return _call(body, name=name, grid=(t // tm, n // tn),
                 in_specs=[row, row, _fixed((2, N_MOD, d)), _w_spec(w, (tn, d), lambda i, j: (j, 0))] + ([tile, tile] if ups else []),
                 out_specs=[row, _fixed((2, 1, d))] + [tile] * n_out,
                 out_shape=[_sds((t, d), MXU), _sds((2, 1, d), F32)] + [_sds((t, n), MXU)] * n_out)(
                     dxn, y, mods, _w_arr(w), *(ups or ()))


def _mm_normb(a_list, w, xs, dres, g, mods, k_scale, s, name):
    t, d = xs.shape
    ka = a_list[0].shape[1]
    tm = _pick(t, (640, 256))
    tk = _pick(ka, (1408, 1536, 1024, 512, 256, 128))
    nk1 = ka // tk
    n_a = len(a_list)
    nk = nk1 * n_a

    def body(*refs):
        a_refs, (w_ref, x_ref, dr_ref, g_ref, m_ref, dx_ref, dsh_ref, dsc_ref, dg_ref, acc) = refs[:n_a], refs[n_a:]
        i, k = pl.program_id(0), pl.program_id(1)

        @pl.when((i == 0) & (k == 0))
        def _():
            dsh_ref[...] = jnp.zeros_like(dsh_ref)
            dsc_ref[...] = jnp.zeros_like(dsc_ref)
            dg_ref[...] = jnp.zeros_like(dg_ref)

        @pl.when(k == 0)
        def _():
            acc[...] = jnp.zeros_like(acc)

        for q in range(n_a):
            @pl.when((k >= q * nk1) & (k < (q + 1) * nk1))
            def _():
                acc[...] += _nt(a_refs[q][...], w_ref[...])

        @pl.when(k == nk - 1)
        def _():
            x = x_ref[...]
            dh = acc[...]
            rstd = lax.rsqrt(jnp.mean(x * x, axis=-1, keepdims=True) + EPS)
            xhat = x * rstd
            gg = g_ref[...]
            _stream_sums(i, tm, s, [(dsh_ref, dh), (dsc_ref, dh * (xhat * gg))])
            dy = dh * (1.0 + _mod_row(m_ref, k_scale, _ctx_rows(i, tm, s)))
            dg_ref[...] += jnp.sum(dy * xhat, axis=0, keepdims=True)
            dxh = dy * gg
            dx_ref[...] = dr_ref[...] + rstd * (dxh - xhat * jnp.mean(dxh * xhat, axis=-1, keepdims=True))

    row = pl.BlockSpec((tm, d), lambda i, k: (i, 0))
    a_specs = [pl.BlockSpec((tm, tk), lambda i, k, q=q: (i, jnp.clip(k - q * nk1, 0, nk1 - 1))) for q in range(n_a)]
    return _call(body, name=name, grid=(t // tm, nk),
                 in_specs=a_specs + [_w_spec(w, (d, tk), lambda i, k: (0, k)), row, row, _fixed((1, d)), _fixed((2, N_MOD, d))],
                 out_specs=[row, _fixed((2, 1, d)), _fixed((2, 1, d)), _fixed((1, d))],
                 out_shape=[_sds((t, d), F32), _sds((2, 1, d), F32), _sds((2, 1, d), F32), _sds((1, d), F32)],
                 scratch=[pltpu.VMEM((tm, d), F32)])(*a_list, _w_arr(w), xs, dres, g, mods)


def _rope_tables(s, ctx_len):
    n_freq = HEAD_DIM // 4
    tok = jnp.arange(s)
    freqs = ROPE_THETA ** (-jnp.arange(n_freq, dtype=F32) / n_freq)
    ang = jnp.concatenate([(tok // GRID_W).astype(F32)[:, None] * freqs, (tok % GRID_W).astype(F32)[:, None] * freqs], axis=-1)
    cos = jnp.repeat(jnp.cos(ang), 2, axis=-1)
    sin = jnp.repeat(jnp.sin(ang), 2, axis=-1) * jnp.tile(jnp.array([-1.0, 1.0], F32), HEAD_DIM // 2)
    cos = jnp.concatenate([jnp.tile(cos, (1, LANES // HEAD_DIM)), jnp.ones((ctx_len, LANES), F32)], axis=0)
    sin = jnp.concatenate([jnp.tile(sin, (1, LANES // HEAD_DIM)), jnp.zeros((ctx_len, LANES), F32)], axis=0)
    return cos, sin


def _swap_pairs(x):
    n = x.shape[-1]
    lane = lax.broadcasted_iota(jnp.int32, x.shape, 1)
    return jnp.where(lane % 2 == 0, pltpu.roll(x, n - 1, 1), pltpu.roll(x, 1, 1))


def _rope(p, cos, sin, name):
    t = p.shape[0]
    w = NA_WIDTH
    te = _row_tile(t)

    def body(q_ref, k_ref, v_ref, c_ref, s_ref, qo_ref, ko_ref, vo_ref):
        c, s = c_ref[...], s_ref[...]
        for hp in range(w // LANES):
            cols = slice(hp * LANES, (hp + 1) * LANES)
            q, k = q_ref[:, cols].astype(F32), k_ref[:, cols].astype(F32)
            qo_ref[:, cols] = (q * c + _swap_pairs(q) * s).astype(qo_ref.dtype)
            ko_ref[:, cols] = (k * c + _swap_pairs(k) * s).astype(ko_ref.dtype)
        vo_ref[...] = v_ref[...].astype(vo_ref.dtype)

    return _call(body, name=name, grid=(t // te,),
                 in_specs=[_rows(te, w, 0), _rows(te, w, 1), _rows(te, w, 2), _rows(te, LANES), _rows(te, LANES)],
                 out_specs=[_rows(te, w)] * 3, out_shape=[_sds((t, w), MXU)] * 3)(p, p, p, cos, sin)


def _rope_bwd(dq, dk, dv, cos, sin, name):
    t = dq.shape[0]
    te = _row_tile(t)
    n_pairs = NA_WIDTH // LANES

    def body(dq_ref, dk_ref, dv_ref, c_ref, s_ref, qo_ref, ko_ref, vo_ref):
        c, s = c_ref[...], s_ref[...]
        for hp in range(n_pairs):
            cols = slice(hp * LANES, (hp + 1) * LANES)
            a, b = dq_ref[:, cols], dk_ref[hp]
            qo_ref[:, cols] = (a * c + _swap_pairs(a * s)).astype(qo_ref.dtype)
            ko_ref[:, cols] = (b * c + _swap_pairs(b * s)).astype(ko_ref.dtype)
            vo_ref[:, cols] = dv_ref[hp].astype(vo_ref.dtype)

    pairs = pl.BlockSpec((n_pairs, te, LANES), lambda i: (0, i, 0))
    return _call(body, name=name, grid=(t // te,),
                 in_specs=[_rows(te, NA_WIDTH), pairs, pairs, _rows(te, LANES), _rows(te, LANES)],
                 out_specs=[_rows(te, NA_WIDTH)] * 3, out_shape=[_sds((t, NA_WIDTH), MXU)] * 3)(dq, dk, dv, cos, sin)


def _na_geometry(r_grid):
    rows = []
    for r0, ks in ((0, 0), (Q_ROWS, 0), (r_grid - Q_ROWS, r_grid - K_ROWS)):
        dr = np.zeros((Q_ROWS, K_ROWS), np.int32)
        vr = np.zeros((Q_ROWS, K_ROWS), bool)
        for a in range(Q_ROWS):
            r = r0 + a
            rs = min(max(r - WIN_H // 2, 0), r_grid - WIN_H)
            for i in range(K_ROWS):
                kr = ks + i
                vr[a, i] = rs <= kr <= rs + WIN_H - 1
                dr[a, i] = kr - r + WIN_H - 1
        rows.append((dr, vr))
    c = np.arange(GRID_W)
    cs = np.clip(c - WIN_W // 2, 0, GRID_W - WIN_W)
    kc = np.arange(GRID_W)
    vc = (kc[None, :] >= cs[:, None]) & (kc[None, :] <= cs[:, None] + WIN_W - 1)
    dc = kc[None, :] - c[:, None] + WIN_W - 1
    return rows, dc, vc


def _bias_table(rpb, r_grid):
    rows, _, vc = _na_geometry(r_grid)
    n_dc, off = 2 * WIN_W - 1, GRID_W - WIN_W
    u = jnp.pad(rpb, ((0, 0), (0, 0), (off, 2 * GRID_W - 1 - off - n_dc)))
    toep = jnp.stack([u[:, :, GRID_W - 1 - c:2 * GRID_W - 1 - c] for c in range(GRID_W)], axis=1)
    toep = jnp.pad(toep, ((0, 0), (0, 0), (Q_ROWS, Q_ROWS), (0, 0)))
    tabs = []
    for dr, vr in rows:
        per_row = []
        for a in range(Q_ROWS):
            lo = int(dr[a, 0]) + Q_ROWS
            blocks = [jnp.where(vc[None], toep[:, :, lo + i, :], NEG) if vr[a, i] else jnp.full((HEADS, GRID_W, GRID_W), NEG, F32)
                      for i in range(K_ROWS)]
            per_row.append(jnp.concatenate(blocks, axis=-1))
        tabs.append(jnp.stack(per_row, axis=1).reshape(HEADS, TQ, TK))
    tabs.append(jnp.full((HEADS, TQ, TK), NEG, F32))
    return jnp.stack(tabs)


def _variant(g, ngx):
    return jnp.where(g == 0, 0, jnp.where(g >= ngx, 3, jnp.where(g == ngx - 1, 2, 1)))


def _key_start(g, r_grid):
    return pl.multiple_of(jnp.clip(g * Q_ROWS - WIN_H // 2, 0, r_grid - K_ROWS) * GRID_W, TQ)


def _nt(a, b):
    return lax.dot_general(a, b, (((1,), (1,)), ((), ())), preferred_element_type=F32)


def _tn(a, b):
    return lax.dot_general(a, b, (((0,), (0,)), ((), ())), preferred_element_type=F32)


def _nn(a, b):
    return jnp.dot(a, b, preferred_element_type=F32)


def _head_mask(h):
    lane = lax.broadcasted_iota(jnp.int32, (1, LANES), 1)
    return ((lane >= HEAD_DIM * h) & (lane < HEAD_DIM * (h + 1))).astype(F32)


def _softmax_parts(qm, knb, kcx, bias):
    s_nb = _nt(qm, knb) + bias
    s_cx = _nt(qm, kcx)
    m = jnp.maximum(jnp.max(s_nb, axis=-1, keepdims=True), jnp.max(s_cx, axis=-1, keepdims=True))
    e_nb = jnp.exp(s_nb - m)
    e_cx = jnp.exp(s_cx - m)
    inv = 1.0 / (jnp.sum(e_nb, axis=-1, keepdims=True) + jnp.sum(e_cx, axis=-1, keepdims=True))
    return e_nb, e_cx, inv


def _na_specs(t, ngx):
    q_spec = pl.BlockSpec((TQ, LANES), lambda hp, g: (g, hp))
    kv_spec = pl.BlockSpec((t, LANES), lambda hp, g: (0, hp))
    b_spec = pl.BlockSpec((1, 2, TQ, TK), lambda hp, g: (_variant(g, ngx), hp, 0, 0))
    return q_spec, kv_spec, b_spec


def _na_fwd(q, k, v, bias, s, name):
    t = q.shape[0]
    ctx_len = t - s
    r_grid = s // GRID_W
    q_spec, kv_spec, b_spec = _na_specs(t, s // TQ)

    def body(q_ref, k_ref, v_ref, b_ref, o_ref):
        start = _key_start(pl.program_id(1), r_grid)
        qf = q_ref[...].astype(F32) * (HEAD_DIM ** -0.5)
        knb, vnb = k_ref[pl.ds(start, TK), :], v_ref[pl.ds(start, TK), :]
        kcx, vcx = k_ref[pl.ds(s, ctx_len), :], v_ref[pl.ds(s, ctx_len), :]
        acc = jnp.zeros((TQ, LANES), F32)
        for h in range(2):
            mask = _head_mask(h)
            e_nb, e_cx, inv = _softmax_parts((qf * mask).astype(MXU), knb, kcx, b_ref[0, h])
            acc += (_nn(e_nb.astype(MXU), vnb) + _nn(e_cx.astype(MXU), vcx)) * (inv * mask)
        o_ref[...] = acc.astype(o_ref.dtype)

    return _call(body, name=name, grid=(NA_WIDTH // LANES, t // TQ), in_specs=[q_spec, kv_spec, kv_spec, b_spec],
                 out_specs=q_spec, out_shape=_sds((t, NA_WIDTH), MXU))(q, k, v, bias)


def _na_bwd(q, k, v, do, bias, s, name):
    t = q.shape[0]
    ctx_len = t - s
    r_grid = s // GRID_W
    ng, ngx = t // TQ, s // TQ
    q_spec, kv_spec, b_spec = _na_specs(t, ngx)

    def body(q_ref, k_ref, v_ref, do_ref, b_ref, dq_ref, dk_hbm, dv_hbm, db_ref, dk_acc, dv_acc):
        hp, g = pl.program_id(0), pl.program_id(1)
        start = _key_start(g, r_grid)

        @pl.when(g == 0)
        def _():
            dk_acc[...] = jnp.zeros_like(dk_acc)
            dv_acc[...] = jnp.zeros_like(dv_acc)

        @pl.when((g == 0) | (g == 1) | (g == ngx - 1) | (g == ngx))
        def _():
            db_ref[...] = jnp.zeros_like(db_ref)

        qf = q_ref[...].astype(F32) * (HEAD_DIM ** -0.5)
        do = do_ref[...].astype(F32)
        knb, vnb = k_ref[pl.ds(start, TK), :], v_ref[pl.ds(start, TK), :]
        kcx, vcx = k_ref[pl.ds(s, ctx_len), :], v_ref[pl.ds(s, ctx_len), :]
        dq = jnp.zeros((TQ, LANES), F32)
        dk_nb = jnp.zeros((TK, LANES), F32)
        dv_nb = jnp.zeros((TK, LANES), F32)
        dk_cx = jnp.zeros((ctx_len, LANES), F32)
        dv_cx = jnp.zeros((ctx_len, LANES), F32)
        for h in range(2):
            mask = _head_mask(h)
            qm = (qf * mask).astype(MXU)
            dom = (do * mask).astype(MXU)
            e_nb, e_cx, inv = _softmax_parts(qm, knb, kcx, b_ref[0, h])
            dp_nb = _nt(dom, vnb)
            dp_cx = _nt(dom, vcx)
            delta = inv * (jnp.sum(e_nb * dp_nb, axis=-1, keepdims=True) + jnp.sum(e_cx * dp_cx, axis=-1, keepdims=True))
            ds_nb = e_nb * (inv * (dp_nb - delta))
            ds_cx = e_cx * (inv * (dp_cx - delta))
            db_ref[0, h] += ds_nb
            ds_nb, ds_cx = ds_nb.astype(MXU), ds_cx.astype(MXU)
            dq += (_nn(ds_nb, knb) + _nn(ds_cx, kcx)) * (mask * (HEAD_DIM ** -0.5))
            dk_nb += _tn(ds_nb, qm)
            dk_cx += _tn(ds_cx, qm)
            dom = (do * (inv * mask)).astype(MXU)
            dv_nb += _tn(e_nb.astype(MXU), dom)
            dv_cx += _tn(e_cx.astype(MXU), dom)
        dq_ref[...] = dq
        dk_acc[pl.ds(start, TK), :] += dk_nb
        dv_acc[pl.ds(start, TK), :] += dv_nb
        dk_acc[pl.ds(s, ctx_len), :] += dk_cx
        dv_acc[pl.ds(s, ctx_len), :] += dv_cx

        @pl.when(g == ng - 1)
        def _():
            pltpu.sync_copy(dk_acc, dk_hbm.at[hp])
            pltpu.sync_copy(dv_acc, dv_hbm.at[hp])

    n_pairs = NA_WIDTH // LANES
    hbm = pl.BlockSpec(memory_space=pl.ANY)
    return _call(body, name=name, grid=(n_pairs, ng), in_specs=[q_spec, kv_spec, kv_spec, q_spec, b_spec],
                 out_specs=[q_spec, hbm, hbm, b_spec],
                 out_shape=[_sds((t, NA_WIDTH), F32), _sds((n_pairs, t, LANES), F32), _sds((n_pairs, t, LANES), F32),
                            _sds((4, HEADS, TQ, TK), F32)],
                 scratch=[pltpu.VMEM((t, LANES), F32), pltpu.VMEM((t, LANES), F32)])(q, k, v, do, bias)


def _rpb_grad(dbias, r_grid, name):
    rows, _, _ = _na_geometry(r_grid)
    n_dr, half, skew, lanes = 2 * WIN_H - 1, WIN_W - 1, TK + 2, 896
    z = dbias[:3].reshape(3, HEADS, Q_ROWS, GRID_W, TK)
    z = jnp.pad(z, ((0, 0),) * 4 + ((0, 1),)).reshape(3, HEADS, Q_ROWS, GRID_W * (TK + 1))
    z = jnp.pad(z, ((0, 0),) * 3 + ((0, GRID_W),)).reshape(3, HEADS, Q_ROWS, GRID_W, skew)
    z = jnp.pad(z, ((0, 0),) * 4 + ((0, lanes - skew),))

    def body(z_ref, o_ref):
        sums = [jnp.sum(z_ref[v, 0, a], axis=0, keepdims=True) for v in range(3) for a in range(Q_ROWS)]
        zs = jnp.concatenate(sums + [jnp.zeros((16 - 3 * Q_ROWS, lanes), F32)], axis=0)
        acc = [jnp.zeros((1, lanes), F32) for _ in range(n_dr)]
        for i in range(K_ROWS):
            if i == 0:
                at0 = pltpu.roll(zs, lanes - (skew - half), 1) + pltpu.roll(zs, half, 1)
            else:
                at0 = pltpu.roll(zs, lanes - (i * GRID_W - half), 1)
            for v, (dr, vr) in enumerate(rows):
                for a in range(Q_ROWS):
                    if vr[a, i]:
                        acc[dr[a, i]] = acc[dr[a, i]] + at0[v * Q_ROWS + a:v * Q_ROWS + a + 1, :]
        o_ref[0] = jnp.concatenate(acc + [jnp.zeros((1, lanes), F32)], axis=0)

    o = _call(body, name=name, grid=(HEADS,),
              in_specs=[pl.BlockSpec((3, 1, Q_ROWS, GRID_W, lanes), lambda h: (0, h, 0, 0, 0))],
              out_specs=pl.BlockSpec((1, 16, lanes), lambda h: (h, 0, 0)), out_shape=_sds((HEADS, 16, lanes), F32))(z)
    return o[:, :n_dr, :2 * WIN_W - 1]


_GELU_K, _GELU_C = 0.7978845608028654, 0.044715


def _gelu(x):
    return 0.5 * x * (1.0 + jnp.tanh(_GELU_K * (x + _GELU_C * x * x * x)))


def _gelu_grad(x):
    th = jnp.tanh(_GELU_K * (x + _GELU_C * x * x * x))
    return 0.5 * (1.0 + th) + 0.5 * x * (1.0 - th * th) * (_GELU_K * (1.0 + 3.0 * _GELU_C * x * x))


def _ln_stats(v):
    mu = jnp.mean(v, axis=-1, keepdims=True)
    vc = v - mu
    rstd = lax.rsqrt(jnp.mean(vc * vc, axis=-1, keepdims=True) + EPS)
    return vc * rstd, rstd


def _gmlp(p, ln_g, ln_b, w_s, b_s, name):
    t = p.shape[0]
    te = _row_tile(t)
    w = SG_WIDTH
    cw = w // SG_GROUPS

    def body(u_ref, v_ref, g_ref, b_ref, ws_ref, bs_ref, o_ref):
        xhat, _ = _ln_stats(_gelu(v_ref[...].astype(F32)))
        vn = (xhat * g_ref[...] + b_ref[...]).astype(MXU)
        ug = _gelu(u_ref[...].astype(F32))
        for ci in range(te // SG_CHUNK):
            rs = slice(ci * SG_CHUNK, (ci + 1) * SG_CHUNK)
            for gi in range(SG_GROUPS):
                cs = slice(gi * cw, (gi + 1) * cw)
                sg = _nn(ws_ref[gi].astype(MXU), vn[rs, cs]) + bs_ref[gi]
                o_ref[rs, cs] = (ug[rs, cs] * sg).astype(o_ref.dtype)

    return _call(body, name=name, grid=(t // te,),
                 in_specs=[_rows(te, w, 3), _rows(te, w, 4), _fixed((1, w)), _fixed((1, w)),
                           _fixed((SG_GROUPS, SG_CHUNK, SG_CHUNK)), _fixed((SG_GROUPS, SG_CHUNK, 1))],
                 out_specs=_rows(te, w), out_shape=_sds((t, w), MXU))(p, p, ln_g, ln_b, w_s, b_s)


def _gmlp_bwd(p, dob, ln_g, ln_b, w_s, b_s, name):
    t = p.shape[0]
    te = _row_tile(t)
    w = SG_WIDTH
    cw = w // SG_GROUPS

    def body(u_ref, v_ref, do_ref, g_ref, b_ref, ws_ref, bs_ref, du_ref, dv_ref, dws_ref, dbs_ref, dg_ref, db_ref, dvn_ref):
        @pl.when(pl.program_id(0) == 0)
        def _():
            dws_ref[...] = jnp.zeros_like(dws_ref)
            dbs_ref[...] = jnp.zeros_like(dbs_ref)
            dg_ref[...] = jnp.zeros_like(dg_ref)
            db_ref[...] = jnp.zeros_like(db_ref)

        u, v = u_ref[...].astype(F32), v_ref[...].astype(F32)
        xhat, rstd = _ln_stats(_gelu(v))
        vn = (xhat * g_ref[...] + b_ref[...]).astype(MXU)
        ug = _gelu(u)
        dob = do_ref[...].astype(F32)
        for ci in range(te // SG_CHUNK):
            rs = slice(ci * SG_CHUNK, (ci + 1) * SG_CHUNK)
            for gi in range(SG_GROUPS):
                cs = slice(gi * cw, (gi + 1) * cw)
                wsg = ws_ref[gi].astype(MXU)
                sg = _nn(wsg, vn[rs, cs]) + bs_ref[gi]
                du_ref[rs, cs] = (dob[rs, cs] * sg * _gelu_grad(u[rs, cs])).astype(du_ref.dtype)
                ds = dob[rs, cs] * ug[rs, cs]
                dbs_ref[gi] += jnp.sum(ds, axis=-1, keepdims=True)
                ds = ds.astype(MXU)
                dws_ref[gi] += _nt(ds, vn[rs, cs])
                dvn_ref[rs, cs] = _tn(wsg, ds)
        dvn = dvn_ref[...]
        dg_ref[...] += jnp.sum(dvn * xhat, axis=0, keepdims=True)
        db_ref[...] += jnp.sum(dvn, axis=0, keepdims=True)
        dxh = dvn * g_ref[...]
        dvg = rstd * (dxh - jnp.mean(dxh, axis=-1, keepdims=True) - xhat * jnp.mean(dxh * xhat, axis=-1, keepdims=True))
        dv_ref[...] = (dvg * _gelu_grad(v)).astype(dv_ref.dtype)

    return _call(body, name=name, grid=(t // te,),
                 in_specs=[_rows(te, w, 3), _rows(te, w, 4), _rows(te, w), _fixed((1, w)), _fixed((1, w)),
                           _fixed((SG_GROUPS, SG_CHUNK, SG_CHUNK)), _fixed((SG_GROUPS, SG_CHUNK, 1))],
                 out_specs=[_rows(te, w), _rows(te, w), _fixed((SG_GROUPS, SG_CHUNK, SG_CHUNK)),
                            _fixed((SG_GROUPS, SG_CHUNK, 1)), _fixed((1, w)), _fixed((1, w))],
                 out_shape=[_sds((t, w), MXU), _sds((t, w), MXU), _sds((SG_GROUPS, SG_CHUNK, SG_CHUNK), F32),
                            _sds((SG_GROUPS, SG_CHUNK, 1), F32), _sds((1, w), F32), _sds((1, w), F32)],
                 scratch=[pltpu.VMEM((te, w), F32)])(p, p, dob, ln_g, ln_b, w_s, b_s)


def _merge(pa, pb, p, b_gate, name):
    t, d = pa.shape
    te = _row_tile(t)
    hw = NA_WIDTH
    nh = d // hw
    c0 = (NA_WIDTH * 3 + SG_WIDTH * 2) // hw

    def body(pa_ref, pb_ref, la_ref, lb_ref, ba_ref, bb_ref, o_ref):
        ga = jax.nn.sigmoid(la_ref[...].astype(F32) + ba_ref[...])
        gb = jax.nn.sigmoid(lb_ref[...].astype(F32) + bb_ref[...])
        o_ref[...] = (ga * pa_ref[...].astype(F32) + gb * pb_ref[...].astype(F32)).astype(o_ref.dtype)

    tile = pl.BlockSpec((te, hw), lambda i, j: (i, j))
    return _call(body, name=name, grid=(t // te, nh),
                 in_specs=[tile, tile, pl.BlockSpec((te, hw), lambda i, j: (i, c0 + j)),
                           pl.BlockSpec((te, hw), lambda i, j: (i, c0 + nh + j)),
                           pl.BlockSpec((1, hw), lambda i, j: (0, j)), pl.BlockSpec((1, hw), lambda i, j: (0, nh + j))],
                 out_specs=tile, out_shape=_sds((t, d), MXU))(pa, pb, p, p, b_gate, b_gate)


def _merge_bwd(dmg, pa, pb, p, b_gate, name):
    t, d = pa.shape
    te = _row_tile(t)
    hw = NA_WIDTH
    nh = d // hw
    c0 = (NA_WIDTH * 3 + SG_WIDTH * 2) // hw

    def body(dm_ref, pa_ref, pb_ref, la_ref, lb_ref, ba_ref, bb_ref, dpa_ref, dpb_ref, dla_ref, dlb_ref, dba_ref, dbb_ref):
        @pl.when(pl.program_id(1) == 0)
        def _():
            dba_ref[...] = jnp.zeros_like(dba_ref)
            dbb_ref[...] = jnp.zeros_like(dbb_ref)

        dm = dm_ref[...].astype(F32)
        ga = jax.nn.sigmoid(la_ref[...].astype(F32) + ba_ref[...])
        gb = jax.nn.sigmoid(lb_ref[...].astype(F32) + bb_ref[...])
        dpa_ref[...] = (dm * ga).astype(dpa_ref.dtype)
        dpb_ref[...] = (dm * gb).astype(dpb_ref.dtype)
        dla = dm * pa_ref[...].astype(F32) * ga * (1.0 - ga)
        dlb = dm * pb_ref[...].astype(F32) * gb * (1.0 - gb)
        dla_ref[...] = dla.astype(dla_ref.dtype)
        dlb_ref[...] = dlb.astype(dlb_ref.dtype)
        dba_ref[...] += jnp.sum(dla, axis=0, keepdims=True)
        dbb_ref[...] += jnp.sum(dlb, axis=0, keepdims=True)

    tile = pl.BlockSpec((te, hw), lambda j, i: (i, j))
    bias_a = pl.BlockSpec((1, hw), lambda j, i: (0, j))
    bias_b = pl.BlockSpec((1, hw), lambda j, i: (0, nh + j))
    return _call(body, name=name, grid=(nh, t // te),
                 in_specs=[tile, tile, tile, pl.BlockSpec((te, hw), lambda j, i: (i, c0 + j)),
                           pl.BlockSpec((te, hw), lambda j, i: (i, c0 + nh + j)), bias_a, bias_b],
                 out_specs=[tile, tile, tile, tile, bias_a, bias_a],
                 out_shape=[_sds((t, d), MXU)] * 4 + [_sds((1, d), F32)] * 2)(dmg, pa, pb, p, p, b_gate, b_gate)


def _final(xs, tgt, g, name):
    t, d = xs.shape
    nx = tgt.shape[0] // TM

    def body(x_ref, t_ref, g_ref, l_ref, dx_ref, dg_ref):
        i = pl.program_id(0)

        @pl.when(i == 0)
        def _():
            l_ref[...] = jnp.zeros_like(l_ref)
            dg_ref[...] = jnp.zeros_like(dg_ref)

        @pl.when(i < nx)
        def _():
            x = x_ref[...]
            rstd = lax.rsqrt(jnp.mean(x * x, axis=-1, keepdims=True) + EPS)
            xhat = x * rstd
            err = xhat * g_ref[...] - t_ref[...]
            l_ref[...] += 0.5 * jnp.sum(jnp.mean(err * err, axis=-1, keepdims=True))
            dy = err * (1.0 / d)
            dg_ref[...] += jnp.sum(dy * xhat, axis=0, keepdims=True)
            dxh = dy * g_ref[...]
            dx_ref[...] = rstd * (dxh - xhat * jnp.mean(dxh * xhat, axis=-1, keepdims=True))

        @pl.when(i >= nx)
        def _():
            dx_ref[...] = jnp.zeros_like(dx_ref)

    return _call(body, name=name, grid=(t // TM,),
                 in_specs=[_rows(TM, d), pl.BlockSpec((TM, d), lambda i: (jnp.minimum(i, nx - 1), 0)), _fixed((1, d))],
                 out_specs=[_fixed((1, LANES)), _rows(TM, d), _fixed((1, d))],
                 out_shape=[_sds((1, LANES), F32), _sds((t, d), F32), _sds((1, d), F32)])(xs, tgt, g)


def _view2d(a):
    return a.reshape(1, -1) if a.ndim == 1 else a.reshape(-1, a.shape[-1])


def _tile_rows(r, c):
    for cand in (1024, 512, 256, 128, 64, 32, 16):
        if r % cand == 0 and cand * c * 4 <= 2 ** 20:
            return cand
    return r


def _pair_sum(g, recv, layer, name):
    _, a, b = g.shape
    tr = _tile_rows(a, b)

    def body(l_ref, g_ref, r_ref, o32_ref, o16_ref):
        acc = g_ref[...] + r_ref[...]
        o32_ref[...] = acc
        o16_ref[...] = acc.astype(o16_ref.dtype)

    first = pl.BlockSpec((None, tr, b), lambda i, l: (0, i, 0))
    return pl.pallas_call(
        body, name=name, out_shape=[_sds((1, a, b), F32), _sds((1, a, b), MXU)],
        grid_spec=pltpu.PrefetchScalarGridSpec(
            num_scalar_prefetch=1, grid=(a // tr,),
            in_specs=[pl.BlockSpec((None, tr, b), lambda i, l: (l[0], i, 0)), first], out_specs=[first, first]),
        compiler_params=pltpu.CompilerParams(dimension_semantics=("arbitrary",), vmem_limit_bytes=VMEM_LIMIT))(layer, g, recv)


def _ew(fn, arrays, out_dtypes, name):
    shape = arrays[0].shape
    views = [_view2d(a) for a in arrays]
    r, c = views[0].shape
    tr = _tile_rows(r, c)

    def body(*refs):
        outs = fn(*[ref[...] for ref in refs[:len(views)]])
        for ref, o in zip(refs[len(views):], outs):
            ref[...] = o.astype(ref.dtype)

    res = _call(body, name=name, grid=(r // tr,), in_specs=[_rows(tr, c)] * len(views), out_specs=[_rows(tr, c)] * len(out_dtypes),
                out_shape=[_sds((r, c), dt) for dt in out_dtypes])(*views)
    return [o.reshape(shape) for o in res]


def _sum_pieces(pieces, name, out_dtypes=(F32,)):
    def fn(*vals):
        acc = vals[0].astype(F32)
        for v in vals[1:]:
            acc = acc + v.astype(F32)
        return (acc,) * len(out_dtypes)

    return _ew(fn, pieces, list(out_dtypes), name)


def _adamw(w, g_pieces, m, v, name):
    n_g = len(g_pieces)

    def fn(w_, *rest):
        g = rest[0]
        for piece in rest[1:n_g]:
            g = g + piece
        m_, v_ = rest[n_g], rest[n_g + 1]
        m2 = ADAM_B1 * m_ + (1.0 - ADAM_B1) * g
        v2 = ADAM_B2 * v_ + (1.0 - ADAM_B2) * (g * g)
        m_hat = m2 / (1.0 - ADAM_B1 ** ADAM_STEP)
        v_hat = v2 / (1.0 - ADAM_B2 ** ADAM_STEP)
        delta = -ADAM_LR * (m_hat / (jnp.sqrt(v_hat) + ADAM_EPS) + ADAM_WD * w_)
        return g, delta, m2, v2

    return _ew(fn, [w, *g_pieces, m, v], [F32] * 4, name)


def _ada_fwd(cond, w, b, name):
    r, d = cond.shape
    n = w.shape[1]
    tn = _pick(n, (1152, 768, 512, 384, 256, 128))

    def body(c_ref, w_ref, b_ref, o_ref, s_ref):
        c = c_ref[...]
        sc = c * jax.nn.sigmoid(c)
        s_ref[...] = sc
        o_ref[...] = _nn(sc.astype(MXU), w_ref[...].astype(MXU)) + b_ref[...]

    return _call(body, name=name, grid=(n // tn,),
                 in_specs=[_fixed((r, d)), pl.BlockSpec((d, tn), lambda j: (0, j)), pl.BlockSpec((1, tn), lambda j: (0, j))],
                 out_specs=[pl.BlockSpec((r, tn), lambda j: (0, j)), _fixed((r, d))],
                 out_shape=[_sds((r, n), F32), _sds((r, d), F32)])(cond, w, b)


def _cctx_grad(parts, c_ctx, name):
    n, d = parts.shape

    def body(p_ref, c_ref, o_ref):
        c = c_ref[...]
        sg = jax.nn.sigmoid(c)
        acc = p_ref[0:1, :]
        for j in range(1, n):
            acc = acc + p_ref[j:j + 1, :]
        o_ref[...] = acc * (sg * (1.0 + c * (1.0 - sg)))

    return _call(body, name=name, grid=(1,), in_specs=[_fixed((n, d)), _fixed((1, d))], out_specs=_fixed((1, d)),
                 out_shape=_sds((1, d), F32))(parts, c_ctx)


def _here():
    return lax.axis_index("x"), lax.axis_index("y"), lax.axis_index("c")


def _flip(v, bit):
    return 1 - v if bit else v


def _allgather8(xb, name):
    r, n = xb.shape

    def body(x_ref, out_ref, send_sems, recv_sems, local_sem):
        x, y, c = _here()
        me = 4 * x + 2 * y + c
        local = pltpu.make_async_copy(x_ref, out_ref.at[me], local_sem)
        local.start()
        sends = []
        for k in range(1, 8):
            peer = (_flip(x, k & 4), _flip(y, k & 2), _flip(c, k & 1))
            cp = pltpu.make_async_remote_copy(src_ref=x_ref, dst_ref=out_ref.at[me], send_sem=send_sems.at[k - 1],
                                              recv_sem=recv_sems.at[k - 1], device_id=peer, device_id_type=MESH)
            cp.start()
            sends.append(cp)
        for k in range(1, 8):
            peer = (_flip(x, k & 4), _flip(y, k & 2), _flip(c, k & 1))
            src = 4 * peer[0] + 2 * peer[1] + peer[2]
            pltpu.make_async_remote_copy(src_ref=x_ref, dst_ref=out_ref.at[src], send_sem=send_sems.at[k - 1],
                                         recv_sem=recv_sems.at[k - 1], device_id=peer, device_id_type=MESH).wait_recv()
        for cp in sends:
            cp.wait_send()
        local.wait()

    vmem = pl.BlockSpec(memory_space=pltpu.VMEM)
    return pl.pallas_call(
        body, name=name, out_shape=_sds((8, r, n), xb.dtype), in_specs=[vmem], out_specs=vmem,
        scratch_shapes=[pltpu.SemaphoreType.DMA((7,)), pltpu.SemaphoreType.DMA((7,)), pltpu.SemaphoreType.DMA(())],
        compiler_params=pltpu.CompilerParams(vmem_limit_bytes=VMEM_LIMIT))(xb)


def _shard_of(ref, axis, j, size):
    sl = pl.ds(j * size, size)
    return ref.at[:, sl, :] if axis == 1 else ref.at[:, :, sl]


def _piece(ref, axis, j, size, layer):
    lay, sl = pl.ds(layer, 1), pl.ds(j * size, size)
    return ref.at[lay, sl, :] if axis == 1 else ref.at[lay, :, sl]


def _gather_chips(shards, axes, name):
    n = len(shards)
    fulls = []
    for a, ax in zip(shards, axes):
        assert a.shape[0] == 2
        shp = list(a.shape)
        shp[ax] *= 4
        fulls.append(_sds(tuple(shp), a.dtype))

    def body(*refs):
        ins, outs = refs[:n], refs[n:2 * n]
        ici_send, ici_recv, d2d_send, d2d_recv = refs[2 * n:]
        x, y, c = _here()
        chips = [(_flip(x, k & 2), _flip(y, k & 1)) for k in range(1, 4)]
        sends = []
        for a in range(n):
            size = ins[a].shape[axes[a]]
            for j, (px, py) in enumerate(chips):
                cp = pltpu.make_async_remote_copy(src_ref=ins[a].at[pl.ds(c, 1)], dst_ref=_piece(outs[a], axes[a], 2 * x + y, size, c),
                                                  send_sem=ici_send.at[3 * a + j], recv_sem=ici_recv.at[3 * a + j],
                                                  device_id=(px, py, c), device_id_type=MESH)
                cp.start()
                sends.append(cp)
        for a in range(n):
            size = ins[a].shape[axes[a]]
            for j, (px, py) in enumerate(chips):
                landed = _piece(outs[a], axes[a], 2 * px + py, size, c)
                pltpu.make_async_remote_copy(src_ref=ins[a].at[pl.ds(c, 1)], dst_ref=landed, send_sem=ici_send.at[3 * a + j],
                                             recv_sem=ici_recv.at[3 * a + j], device_id=(px, py, c), device_id_type=MESH).wait_recv()
                cp = pltpu.make_async_remote_copy(src_ref=landed, dst_ref=landed, send_sem=d2d_send.at[3 * a + j],
                                                  recv_sem=d2d_recv.at[3 * a + j], device_id=(x, y, 1 - c), device_id_type=MESH)
                cp.start()
                sends.append(cp)
        for a in range(n):
            size = ins[a].shape[axes[a]]
            for j, (px, py) in enumerate(chips):
                passed = _piece(outs[a], axes[a], 2 * px + py, size, 1 - c)
                pltpu.make_async_remote_copy(src_ref=passed, dst_ref=passed, send_sem=d2d_send.at[3 * a + j],
                                             recv_sem=d2d_recv.at[3 * a + j], device_id=(x, y, 1 - c), device_id_type=MESH).wait_recv()
        for cp in sends:
            cp.wait_send()

    hbm = pl.BlockSpec(memory_space=pl.ANY)
    return pl.pallas_call(
        body, name=name, out_shape=fulls, in_specs=[hbm] * n, out_specs=[hbm] * n,
        scratch_shapes=[pltpu.SemaphoreType.DMA((3 * n,))] * 4)(*shards)


def _swap_layers(arrays, name):
    n = len(arrays)

    def body(*refs):
        ins, outs = refs[:n], refs[n:2 * n]
        send_sems, recv_sems = refs[2 * n:]
        x, y, c = _here()
        copies = []
        for a in range(n):
            cp = pltpu.make_async_remote_copy(src_ref=ins[a].at[pl.ds(1 - c, 1)], dst_ref=outs[a], send_sem=send_sems.at[a],
                                              recv_sem=recv_sems.at[a], device_id=(x, y, 1 - c), device_id_type=MESH)
            cp.start()
            copies.append(cp)
        for cp in copies:
            cp.wait()

    hbm = pl.BlockSpec(memory_space=pl.ANY)
    return pl.pallas_call(
        body, name=name, out_shape=[_sds((1, *a.shape[1:]), a.dtype) for a in arrays], in_specs=[hbm] * n, out_specs=[hbm] * n,
        scratch_shapes=[pltpu.SemaphoreType.DMA((n,)), pltpu.SemaphoreType.DMA((n,))])(*arrays)


def _scatter_chips(fulls, axes, name):
    n = len(fulls)
    recvs = []
    for a, ax in zip(fulls, axes):
        shp = list(a.shape)
        shp[ax] //= 4
        recvs.append(_sds((3, *shp), a.dtype))

    def body(*refs):
        ins, outs = refs[:n], refs[n:2 * n]
        send_sems, recv_sems = refs[2 * n:]
        x, y, c = _here()
        sends = []
        for a in range(n):
            size = ins[a].shape[axes[a]] // 4
            for k in range(1, 4):
                peer = (_flip(x, k & 2), _flip(y, k & 1), c)
                cp = pltpu.make_async_remote_copy(src_ref=_shard_of(ins[a], axes[a], 2 * peer[0] + peer[1], size),
                                                  dst_ref=outs[a].at[k - 1],
                                                  send_sem=send_sems.at[3 * a + k - 1], recv_sem=recv_sems.at[3 * a + k - 1],
                                                  device_id=peer, device_id_type=MESH)
                cp.start()
                sends.append(cp)
        for cp in sends:
            cp.wait_recv()
        for cp in sends:
            cp.wait_send()

    hbm = pl.BlockSpec(memory_space=pl.ANY)
    return pl.pallas_call(
        body, name=name, out_shape=recvs, in_specs=[hbm] * n, out_specs=[hbm] * n,
        scratch_shapes=[pltpu.SemaphoreType.DMA((3 * n,)), pltpu.SemaphoreType.DMA((3 * n,))])(*fulls)


def _sibling_swap(arrays, name):
    n = len(arrays)

    def body(*refs):
        ins, outs = refs[:n], refs[n:2 * n]
        send_sems, recv_sems = refs[2 * n:]
        x, y, c = _here()
        copies = []
        for a in range(n):
            cp = pltpu.make_async_remote_copy(src_ref=ins[a], dst_ref=outs[a], send_sem=send_sems.at[a], recv_sem=recv_sems.at[a],
                                              device_id=(x, y, 1 - c), device_id_type=MESH)
            cp.start()
            copies.append(cp)
        for cp in copies:
            cp.wait()

    hbm = pl.BlockSpec(memory_space=pl.ANY)
    return pl.pallas_call(
        body, name=name, out_shape=[_sds(a.shape, a.dtype) for a in arrays], in_specs=[hbm] * n, out_specs=[hbm] * n,
        scratch_shapes=[pltpu.SemaphoreType.DMA((n,)), pltpu.SemaphoreType.DMA((n,))])(*arrays)


def _ffn_fwd(xs, g, mods, k0, w_up, w_down, s, tag):
    h, ua, ub, act = _norm_mm(xs, g, mods, k0, k0 + 1, w_up, s, tag + "_up", True)
    y, xn = _mm_res(act, w_down, xs, mods, k0 + 2, 0.5, s, tag + "_down")
    return xn, (xs, h, ua, ub, act, y)


def _dw(gw, key, a, b, name, col0=0, n_total=None):
    shape = (gw["depth"], a.shape[1], n_total or b.shape[1])
    gw[key] = _mm(a, b, "tn", F32, name, into=(gw.get(key), shape, gw["layer"], col0))


def _ffn_bwd(dxn, saved, g, mods, k0, w_up, w_down, s, tag, gw, up_key, down_key):
    xs, h, ua, ub, act, y = saved
    dy, dgate, dua, dub = _resb_mm(dxn, y, mods, k0 + 2, 0.5, w_down, s, tag + "_down_dx", (ua, ub))
    _dw(gw, down_key, act, dy, tag + "_down_dw")
    f = dua.shape[1]
    _dw(gw, up_key, h, dua, tag + "_upa_dw", 0, 2 * f)
    _dw(gw, up_key, h, dub, tag + "_upb_dw", f, 2 * f)
    dx, dsh, dsc, dg = _mm_normb([dua, dub], w_up, xs, dxn, g, mods, k0 + 1, s, tag + "_up_dx")
    return dx, dg, [dsh, dsc, dgate]


def _mix_fwd(xs, g, mods, wl, pl_, tabs, s, tag):
    cos, sin = tabs
    h, p = _norm_mm(xs, g, mods, 3, 4, wl["w_in"], s, tag + "_in", False)
    q, k, v = _rope(p, cos, sin, tag + "_rope")
    bias = _bias_table(pl_["rpb"], s // GRID_W)
    oa = _na_fwd(q, k, v, bias, s, tag + "_na")
    ob = _gmlp(p, pl_["ln_v_g"], pl_["ln_v_b"], pl_["w_s"], pl_["b_s"], tag + "_sg")
    pa = _mm(oa, wl["w_pa"], "nn", MXU, tag + "_pa")
    pb = _mm(ob, wl["w_pb"], "nn", MXU, tag + "_pb")
    mg = _merge(pa, pb, p, pl_["b_gate"], tag + "_merge")
    y, xn = _mm_res(mg, wl["w_o"], xs, mods, 5, 1.0, s, tag + "_o")
    return xn, (xs, h, p, q, k, v, bias, oa, ob, pa, pb, mg, y)


def _mix_bwd(dxn, saved, g, mods, wl, pl_, tabs, s, tag, gw):
    xs, h, p, q, k, v, bias, oa, ob, pa, pb, mg, y = saved
    cos, sin = tabs
    gp = {}
    dy, dgate, dmg = _resb_mm(dxn, y, mods, 5, 1.0, wl["w_o"], s, tag + "_o_dx")
    _dw(gw, "w_o", mg, dy, tag + "_o_dw")
    dpa, dpb, dla, dlb, dba, dbb = _merge_bwd(dmg, pa, pb, p, pl_["b_gate"], tag + "_merge_b")
    gp["b_gate"] = jnp.concatenate([dba, dbb], axis=1)
    _dw(gw, "w_pa", oa, dpa, tag + "_pa_dw")
    doa = _mm(dpa, wl["w_pa"], "nt", MXU, tag + "_pa_dx")
    _dw(gw, "w_pb", ob, dpb, tag + "_pb_dw")
    dob = _mm(dpb, wl["w_pb"], "nt", MXU, tag + "_pb_dx")
    du, dvs, gp["w_s"], gp["b_s"], gp["ln_v_g"], gp["ln_v_b"] = _gmlp_bwd(
        p, dob, pl_["ln_v_g"], pl_["ln_v_b"], pl_["w_s"], pl_["b_s"], tag + "_sg_b")
    dqr, dkr, dv, dbias = _na_bwd(q, k, v, doa, bias, s, tag + "_na_b")
    gp["rpb"] = _rpb_grad(dbias, s // GRID_W, tag + "_rpb")
    dq, dk, dvv = _rope_bwd(dqr, dkr, dv, cos, sin, tag + "_rope_b")
    dp = jnp.concatenate([dq, dk, dvv, du, dvs, dla, dlb], axis=1)
    _dw(gw, "w_in", h, dp, tag + "_in_dw")
    dx, dsh, dsc, dg = _mm_normb([dp], wl["w_in"], xs, dxn, g, mods, 4, s, tag + "_in_dx")
    return dx, gp, dg, [dsh, dsc, dgate]


def _local_step(x, ctx, tgt, mods, wts, prm):
    s, d = x.shape
    depth = mods.shape[0]
    tabs = _rope_tables(s, ctx.shape[0])
    xs = jnp.concatenate([x, ctx], axis=0)
    saved = []
    for l in range(depth):
        wl = {k: (v, l) for k, v in wts.items()}
        pl_ = _layer_params(prm, l)
        xs, s1 = _ffn_fwd(xs, pl_["g"][0], mods[l], 0, wl["w_ff1_up"], wl["w_ff1_down"], s, f"l{l}_ff1")
        xs, s2 = _mix_fwd(xs, pl_["g"][1], mods[l], wl, pl_, tabs, s, f"l{l}_mix")
        xs, s3 = _ffn_fwd(xs, pl_["g"][2], mods[l], 6, wl["w_ff2_up"], wl["w_ff2_down"], s, f"l{l}_ff2")
        saved.append((s1, s2, s3))
    loss, dxs, d_final_g = _final(xs, tgt, prm["final_g"].reshape(1, d), "final")
    gw = {"depth": depth}
    gp = {k: [None] * depth for k in ("norm_g", "b_gate", "rpb", "ln_v_g", "ln_v_b", "w_s", "b_s")}
    dmods = [None] * depth
    for l in reversed(range(depth)):
        wl = {k: (v, l) for k, v in wts.items()}
        pl_ = _layer_params(prm, l)
        s1, s2, s3 = saved[l]
        gw["layer"] = l
        dxs, dg2, dm2 = _ffn_bwd(dxs, s3, pl_["g"][2], mods[l], 6, wl["w_ff2_up"], wl["w_ff2_down"], s, f"l{l}_ff2",
                                 gw, "w_ff2_up", "w_ff2_down")
        dxs, gpm, dg1, dm1 = _mix_bwd(dxs, s2, pl_["g"][1], mods[l], wl, pl_, tabs, s, f"l{l}_mix", gw)
        dxs, dg0, dm0 = _ffn_bwd(dxs, s1, pl_["g"][0], mods[l], 0, wl["w_ff1_up"], wl["w_ff1_down"], s, f"l{l}_ff1",
                                 gw, "w_ff1_up", "w_ff1_down")
        gp["b_gate"][l] = gpm["b_gate"][0]
        gp["rpb"][l] = gpm["rpb"]
        gp["ln_v_g"][l] = gpm["ln_v_g"][0]
        gp["ln_v_b"][l] = gpm["ln_v_b"][0]
        gp["w_s"][l] = gpm["w_s"]
        gp["b_s"][l] = gpm["b_s"][..., 0]
        gp["norm_g"][l] = jnp.concatenate([dg0, dg1, dg2], axis=0)
        dmods[l] = jnp.concatenate(dm0 + dm1 + dm2, axis=1)
    gw = {k: gw[k] for k in wts}
    gp = {k: jnp.stack(v) for k, v in gp.items()}
    gp["final_g"] = d_final_g[0]
    return loss[0, 0], dxs[:s], jnp.stack(dmods), gw, gp


def _layer_params(prm, l):
    d = prm["norm_g"].shape[-1]
    return {
        "g": [prm["norm_g"][l, i].reshape(1, d) for i in range(3)],
        "b_gate": prm["b_gate"][l].reshape(1, -1),
        "rpb": prm["rpb"][l],
        "ln_v_g": prm["ln_v_g"][l].reshape(1, -1),
        "ln_v_b": prm["ln_v_b"][l].reshape(1, -1),
        "w_s": prm["w_s"][l],
        "b_s": prm["b_s"][l][..., None],
    }


SMALL = ("norm_g", "b_gate", "rpb", "ln_v_g", "ln_v_b", "w_s", "b_s", "final_g")
PACK_LANES = 1024


def _pack(parts):
    flat = jnp.concatenate([p.reshape(-1) for p in parts])
    rows = -(-flat.shape[0] // PACK_LANES)
    rows = -(-rows // 8) * 8
    return jnp.pad(flat, (0, rows * PACK_LANES - flat.shape[0])).reshape(rows, PACK_LANES)


def _unpack(flat, shapes):
    out, off = [], 0
    for shp in shapes:
        n = int(np.prod(shp))
        out.append(flat[..., off:off + n].reshape(*flat.shape[:-1], *shp))
        off += n
    return out


def kernel(x, c, ctx, c_ctx, w_ada, b_ada, norm_g, w_ff1_up, w_ff1_down, w_in, b_gate, rpb, ln_v_g, ln_v_b, w_s, b_s, w_pa, w_pb, w_o, w_ff2_up, w_ff2_down, final_g, loss_target, m_c_ctx, m_w_ada, m_b_ada, m_norm_g, m_w_ff1_up, m_w_ff1_down, m_w_in, m_b_gate, m_rpb, m_ln_v_g, m_ln_v_b, m_w_s, m_b_s, m_w_pa, m_w_pb, m_w_o, m_w_ff2_up, m_w_ff2_down, m_final_g, v_c_ctx, v_w_ada, v_b_ada, v_norm_g, v_w_ff1_up, v_w_ff1_down, v_w_in, v_b_gate, v_rpb, v_ln_v_g, v_ln_v_b, v_w_s, v_b_s, v_w_pa, v_w_pb, v_w_o, v_w_ff2_up, v_w_ff2_down, v_final_g):
    weights = dict(c_ctx=c_ctx, w_ada=w_ada, b_ada=b_ada, norm_g=norm_g, w_ff1_up=w_ff1_up, w_ff1_down=w_ff1_down, w_in=w_in,
                   b_gate=b_gate, rpb=rpb, ln_v_g=ln_v_g, ln_v_b=ln_v_b, w_s=w_s, b_s=b_s, w_pa=w_pa, w_pb=w_pb, w_o=w_o,
                   w_ff2_up=w_ff2_up, w_ff2_down=w_ff2_down, final_g=final_g)
    mom_m = dict(c_ctx=m_c_ctx, w_ada=m_w_ada, b_ada=m_b_ada, norm_g=m_norm_g, w_ff1_up=m_w_ff1_up, w_ff1_down=m_w_ff1_down,
                 w_in=m_w_in, b_gate=m_b_gate, rpb=m_rpb, ln_v_g=m_ln_v_g, ln_v_b=m_ln_v_b, w_s=m_w_s, b_s=m_b_s, w_pa=m_w_pa,
                 w_pb=m_w_pb, w_o=m_w_o, w_ff2_up=m_w_ff2_up, w_ff2_down=m_w_ff2_down, final_g=m_final_g)
    mom_v = dict(c_ctx=v_c_ctx, w_ada=v_w_ada, b_ada=v_b_ada, norm_g=v_norm_g, w_ff1_up=v_w_ff1_up, w_ff1_down=v_w_ff1_down,
                 w_in=v_w_in, b_gate=v_b_gate, rpb=v_rpb, ln_v_g=v_ln_v_g, ln_v_b=v_ln_v_b, w_s=v_w_s, b_s=v_b_s, w_pa=v_w_pa,
                 w_pb=v_w_pb, w_o=v_w_o, w_ff2_up=v_w_ff2_up, w_ff2_down=v_w_ff2_down, final_g=v_final_g)
    order = list(weights)
    mx, my, mc = _here()
    dev = 4 * mx + 2 * my + mc
    chip = 2 * mx + my
    depth, d, n_ada = w_ada.shape
    dq = d // 4

    c_all = _allgather8(jnp.pad(c, ((0, 7), (0, 0))), "gather_c")[:, 0, :]
    cond = jnp.concatenate([c_all, c_ctx[None, :], jnp.zeros((7, d), F32)], axis=0)
    b_shard = lax.dynamic_slice(b_ada, (0, chip * n_ada), (depth, n_ada))
    proj = [_ada_fwd(cond, w_ada[l], b_shard[l:l + 1], f"ada{l}") for l in range(depth)]
    silu_c = proj[0][1]
    mods_sh = _allgather8(jnp.concatenate([p[0] for p in proj], axis=0), "gather_mods")
    mods_all = jnp.transpose(mods_sh[0::2].reshape(4, depth, 16, n_ada), (1, 2, 0, 3)).reshape(depth, 16, N_MOD, d)
    mods = jnp.stack([lax.dynamic_index_in_dim(mods_all, dev, axis=1, keepdims=False), mods_all[:, 8]], axis=1)

    shards = [weights[k].astype(MXU) for k in BIG]
    full = _gather_chips(shards, [SHARD_AXIS[k] for k in BIG], "gather_w")
    full = [lax.dynamic_update_slice_in_dim(f, sh, chip * sh.shape[SHARD_AXIS[k]], axis=SHARD_AXIS[k])
            for k, f, sh in zip(BIG, full, shards)]
    wts = dict(zip(BIG, full))
    prm = {k: weights[k] for k in SMALL if k != "norm_g"}
    norm_full = _allgather8(jnp.pad(norm_g.reshape(depth * 3, dq), ((0, 8 - depth * 3), (0, 0))), "gather_norm_g")
    prm["norm_g"] = jnp.transpose(norm_full[0::2, :depth * 3].reshape(4, depth, 3, dq), (1, 2, 0, 3)).reshape(depth, 3, d)

    loss, grad_x, dmods, gw, gp = _local_step(x[0], ctx[0], loss_target[0], mods, wts, prm)
    loss = lax.psum(loss, ("x", "y", "c"))

    small_shapes = [(depth, 2, N_MOD * d)] + [weights[k].shape if k != "norm_g" else (depth, 3, d) for k in SMALL]
    packed = _allgather8(_pack([dmods.reshape(depth, 2, N_MOD * d)] + [gp[k] for k in SMALL]), "gather_small")
    rows = packed.shape[1]
    total = _sum_pieces([packed[i] for i in range(8)], "sum_small")[0].reshape(-1)
    sums = dict(zip(("dmods",) + SMALL, _unpack(total, small_shapes)))
    dmods_dev = _unpack(packed.reshape(8, rows * PACK_LANES), small_shapes[:1])[0]

    g_ada, cc_parts = [], []
    for l in range(depth):
        dm = jnp.concatenate([dmods_dev[:, l, 0], sums["dmods"][l, 1][None], jnp.zeros((7, N_MOD * d), F32)], axis=0)
        dm_sh = lax.dynamic_slice(dm, (0, chip * n_ada), (16, n_ada))
        g_ada.append(_mm(silu_c, dm_sh, "tn", F32, f"ada{l}_dw"))
        cc_parts.append(_mm(dm_sh, w_ada[l], "nt", F32, f"ada{l}_dc")[8:9])
    cc_all = _allgather8(jnp.pad(jnp.concatenate(cc_parts, axis=0), ((0, 8 - depth), (0, 0))), "gather_cctx")
    g_cctx = _cctx_grad(cc_all[0::2, :depth].reshape(4 * depth, d), c_ctx.reshape(1, d), "cctx_grad")

    axes = [SHARD_AXIS[k] for k in BIG]
    from_sibling = _swap_layers([gw[k] for k in BIG], "swap_layer_gw")
    my_layer = jnp.reshape(mc, (1,)).astype(jnp.int32)
    pair = [_pair_sum(gw[k], r, my_layer, "pair_" + k) for k, r in zip(BIG, from_sibling)]
    recv = _scatter_chips([p[1] for p in pair], axes, "scatter_gw")
    mine = []
    for k, ax, p, r in zip(BIG, axes, pair, recv):
        size = p[0].shape[ax] // 4
        own = lax.dynamic_slice_in_dim(p[0], chip * size, size, axis=ax)
        mine.append(_sum_pieces([own, r[0], r[1], r[2]], "sum_" + k)[0])
    other = _sibling_swap(mine, "swap_gw")

    pieces = {k: [jnp.concatenate([jnp.where(mc == 0, a, b), jnp.where(mc == 0, b, a)], axis=0)] for k, a, b in zip(BIG, mine, other)}
    pieces["w_ada"] = [jnp.stack(g_ada)]
    pieces["b_ada"] = [sums["dmods"][:, 0], sums["dmods"][:, 1]]
    pieces["c_ctx"] = [g_cctx[0]]
    for k in SMALL:
        pieces[k] = [sums[k]]
    pieces["norm_g"] = [lax.dynamic_slice_in_dim(sums["norm_g"], chip * dq, dq, axis=2)]
    res = {k: _adamw(weights[k], pieces[k], mom_m[k], mom_v[k], "adamw_" + k) for k in order}
    return (loss, grad_x[None], *[res[k][0] for k in order], *[res[k][1] for k in order],
            *[res[k][2] for k in order], *[res[k][3] for k in order])
```

```python
import numpy as np
import jax
import jax.numpy as jnp
from jax import lax
from jax.experimental import pallas as pl
from jax.experimental.pallas import tpu as pltpu

F32 = jnp.float32
MXU = jnp.bfloat16
EPS = 1e-6
GRID_W, HEADS, HEAD_DIM = 64, 8, 64
NA_WIDTH = SG_WIDTH = 512
WIN_H, WIN_W = 8, 16
SG_CHUNK, SG_GROUPS = 128, 4
N_MOD = 9
ROPE_THETA = 10000.0
Q_ROWS, K_ROWS = 4, 12
TQ, TK = Q_ROWS * GRID_W, K_ROWS * GRID_W
TM = 256
LANES = 128
NEG = -1e30
VMEM_LIMIT = 56 * 2 ** 20
ADAM_LR, ADAM_B1, ADAM_B2, ADAM_EPS, ADAM_WD, ADAM_STEP = 0.001, 0.9, 0.999, 1e-08, 0.01, 10
MESH = pl.DeviceIdType.MESH
BIG = ("w_ff1_up", "w_ff1_down", "w_in", "w_pa", "w_pb", "w_o", "w_ff2_up", "w_ff2_down")
SHARD_AXIS = {"w_ff1_up": 2, "w_ff1_down": 1, "w_in": 2, "w_pa": 2, "w_pb": 2, "w_o": 1, "w_ff2_up": 2, "w_ff2_down": 1}


def _call(body, *, name, grid, in_specs, out_specs, out_shape, scratch=(), aliases=None):
    return pl.pallas_call(
        body, name=name, grid=grid, in_specs=in_specs, out_specs=out_specs, out_shape=out_shape,
        scratch_shapes=list(scratch), input_output_aliases=aliases or {},
        compiler_params=pltpu.CompilerParams(dimension_semantics=("arbitrary",) * len(grid), vmem_limit_bytes=VMEM_LIMIT))


def _w_dims(w):
    return w[0].shape[1:] if isinstance(w, tuple) else w.shape


def _w_arr(w):
    return w[0] if isinstance(w, tuple) else w


def _w_spec(w, block, index):
    if isinstance(w, tuple):
        layer = w[1]
        return pl.BlockSpec((None, *block), lambda *ids: (layer, *index(*ids)))
    return pl.BlockSpec(block, index)


def _pick(n, prefs):
    for p in prefs:
        if n % p == 0:
            return p
    return n


def _row_tile(t):
    return _pick(t, (640, 256))


def _rows(tm, n, col=0):
    return pl.BlockSpec((tm, n), lambda i: (i, col))


def _fixed(shape):
    return pl.BlockSpec(shape, lambda *_: (0,) * len(shape))


def _sds(shape, dtype):
    return jax.ShapeDtypeStruct(shape, dtype)


def _mm(a, b, mode, out_dtype, name, into=None):
    if mode == "tn":
        r, m = a.shape
        n = b.shape[1]
        tm = _pick(m, (1024, 1408, 704, 512, 256, 128))
        tn = _pick(n, (512, 1408, 256, 128))
        tr = _pick(r, (1280, 640, 512, 256, 128))

        def body(a_ref, b_ref, *rest):
            o_ref = rest[-1]

            @pl.when(pl.program_id(2) == 0)
            def _():
                o_ref[...] = jnp.zeros_like(o_ref)

            o_ref[...] += lax.dot_general(a_ref[...].astype(MXU), b_ref[...].astype(MXU), (((0,), (0,)), ((), ())),
                                          preferred_element_type=F32)

        in_specs = [pl.BlockSpec((tr, tm), lambda i, j, k: (k, i)), pl.BlockSpec((tr, tn), lambda i, j, k: (k, j))]
        if into is None:
            return _call(body, name=name, grid=(m // tm, n // tn, r // tr), in_specs=in_specs,
                         out_specs=pl.BlockSpec((tm, tn), lambda i, j, k: (i, j)), out_shape=_sds((m, n), F32))(a, b)
        buf, shape, layer, col0 = into
        out_spec = pl.BlockSpec((None, tm, tn), lambda i, j, k: (layer, i, j + col0 // tn))
        if buf is None:
            return _call(body, name=name, grid=(m // tm, n // tn, r // tr), in_specs=in_specs, out_specs=out_spec,
                         out_shape=_sds(shape, F32))(a, b)
        return _call(body, name=name, grid=(m // tm, n // tn, r // tr), in_specs=in_specs + [pl.BlockSpec(memory_space=pl.ANY)],
                     out_specs=out_spec, out_shape=_sds(shape, F32), aliases={2: 0})(a, b, buf)
    m, k = a.shape
    n = _w_dims(b)[1] if mode == "nn" else _w_dims(b)[0]
    tm = _pick(m, (1280, 640, 512, 256, 128) if k <= 2816 else (640, 512, 256, 128))
    tn = _pick(n, (512, 1408, 256, 128))
    dims = (((1,), (0,)), ((), ())) if mode == "nn" else (((1,), (1,)), ((), ()))

    def body(a_ref, b_ref, o_ref):
        o_ref[...] = lax.dot_general(a_ref[...].astype(MXU), b_ref[...].astype(MXU), dims,
                                     preferred_element_type=F32).astype(o_ref.dtype)

    b_spec = _w_spec(b, (k, tn), lambda i, j: (0, j)) if mode == "nn" else _w_spec(b, (tn, k), lambda i, j: (j, 0))
    return _call(body, name=name, grid=(m // tm, n // tn), in_specs=[pl.BlockSpec((tm, k), lambda i, j: (i, 0)), b_spec],
                 out_specs=pl.BlockSpec((tm, tn), lambda i, j: (i, j)), out_shape=_sds((m, n), out_dtype))(a, _w_arr(b))


def _row_chunks(tm):
    rc = _pick(tm, (256, 128))
    return [slice(r, r + rc) for r in range(0, tm, rc)]


def _ctx_rows(i, tm, s):
    return (i * tm + lax.broadcasted_iota(jnp.int32, (tm, 1), 0)) >= s


def _mod_row(m_ref, k, ctx):
    return jnp.where(ctx, m_ref[1, k:k + 1, :], m_ref[0, k:k + 1, :])


def _stream_sums(i, tm, s, refs_and_vals):
    @pl.when((i + 1) * tm <= s)
    def _():
        for ref, val in refs_and_vals:
            ref[0] += jnp.sum(val, axis=0, keepdims=True)

    @pl.when((i + 1) * tm > s)
    def _():
        ctx = _ctx_rows(i, tm, s)
        for ref, val in refs_and_vals:
            ref[0] += jnp.sum(jnp.where(ctx, 0.0, val), axis=0, keepdims=True)
            ref[1] += jnp.sum(jnp.where(ctx, val, 0.0), axis=0, keepdims=True)


def _norm_mm(xs, g, mods, k_shift, k_scale, w, s, name, glu):
    t, d = xs.shape
    n = _w_dims(w)[1] // 2 if glu else _w_dims(w)[1]
    tm = _pick(t, (1280, 640, 256))
    tn = _pick(n, (256, 128)) if glu else _pick(n, (512, 256, 128))
    nj = n // tn

    def body(x_ref, g_ref, m_ref, *refs):
        i, j = pl.program_id(0), pl.program_id(1)
        w_refs, h_ref, o_refs = refs[:2 if glu else 1], refs[2 if glu else 1], refs[3 if glu else 2:]

        @pl.when(j == 0)
        def _():
            x = x_ref[...]
            rstd = lax.rsqrt(jnp.mean(x * x, axis=-1, keepdims=True) + EPS)
            ctx = _ctx_rows(i, tm, s)
            h = x * rstd * g_ref[...] * (1.0 + _mod_row(m_ref, k_scale, ctx)) + _mod_row(m_ref, k_shift, ctx)
            h_ref[...] = h.astype(h_ref.dtype)

        for rows in _row_chunks(tm):
            h = h_ref[rows, :]
            a = _nn(h, w_refs[0][...])
            o_refs[0][rows, :] = a.astype(o_refs[0].dtype)
            if glu:
                b = _nn(h, w_refs[1][...])
                o_refs[1][rows, :] = b.astype(o_refs[1].dtype)
                o_refs[2][rows, :] = (a * jax.nn.sigmoid(a) * b).astype(o_refs[2].dtype)

    tile = pl.BlockSpec((tm, tn), lambda i, j: (i, j))
    row = pl.BlockSpec((tm, d), lambda i, j: (i, 0))
    w_specs = [_w_spec(w, (d, tn), lambda i, j: (0, j))] + ([_w_spec(w, (d, tn), lambda i, j: (0, j + nj))] if glu else [])
    n_out = 3 if glu else 1
    return _call(body, name=name, grid=(t // tm, nj),
                 in_specs=[row, _fixed((1, d)), _fixed((2, N_MOD, d))] + w_specs,
                 out_specs=[row] + [tile] * n_out,
                 out_shape=[_sds((t, d), MXU)] + [_sds((t, n), MXU)] * n_out)(xs, g, mods, *([_w_arr(w)] * (2 if glu else 1)))


def _mm_res(a, w, xs, mods, k_gate, coef, s, name):
    t, k = a.shape
    d = _w_dims(w)[1]
    tm = _pick(t, (1280, 640, 256))
    tn = _pick(d, (512, 256, 128))

    def body(a_ref, w_ref, x_ref, m_ref, y_ref, o_ref):
        y = _nn(a_ref[...], w_ref[...])
        y_ref[...] = y.astype(y_ref.dtype)
        gate = _mod_row(m_ref, k_gate, _ctx_rows(pl.program_id(0), tm, s))
        o_ref[...] = x_ref[...] + (coef * gate) * y

    tile = pl.BlockSpec((tm, tn), lambda i, j: (i, j))
    return _call(body, name=name, grid=(t // tm, d // tn),
                 in_specs=[pl.BlockSpec((tm, k), lambda i, j: (i, 0)), _w_spec(w, (k, tn), lambda i, j: (0, j)), tile,
                           pl.BlockSpec((2, N_MOD, tn), lambda i, j: (0, 0, j))],
                 out_specs=[tile, tile], out_shape=[_sds((t, d), MXU), _sds((t, d), F32)])(a, _w_arr(w), xs, mods)


def _resb_mm(dxn, y, mods, k_gate, coef, w, s, name, ups=None):
    t, d = dxn.shape
    n = _w_dims(w)[0]
    tm = _pick(t, (1280, 640, 256))
    tn = _pick(n, (256, 128)) if ups else _pick(n, (512, 256, 128))

    def body(dx_ref, y_ref, m_ref, w_ref, *refs):
        i, j = pl.program_id(0), pl.program_id(1)
        u_refs, (dy_ref, dgt_ref), o_refs = (refs[:2], refs[2:4], refs[4:]) if ups else ((), refs[:2], refs[2:])

        @pl.when((i == 0) & (j == 0))
        def _():
            dgt_ref[...] = jnp.zeros_like(dgt_ref)

        @pl.when(j == 0)
        def _():
            dx = dx_ref[...]
            gate = _mod_row(m_ref, k_gate, _ctx_rows(i, tm, s))
            dy_ref[...] = ((coef * gate) * dx).astype(dy_ref.dtype)
            _stream_sums(i, tm, s, [(dgt_ref, coef * y_ref[...].astype(F32) * dx)])

        dact = _nt(dy_ref[...], w_ref[...])
        if ups:
            a, b = u_refs[0][...].astype(F32), u_refs[1][...].astype(F32)
            sg = jax.nn.sigmoid(a)
            o_refs[0][...] = (dact * b * sg * (1.0 + a * (1.0 - sg))).astype(o_refs[0].dtype)
            o_refs[1][...] = (dact * a * sg).astype(o_refs[1].dtype)
        else:
            o_refs[0][...] = dact.astype(o_refs[0].dtype)

    tile = pl.BlockSpec((tm, tn), lambda i, j: (i, j))
    row = pl.BlockSpec((tm, d), lambda i, j: (i, 0))
    n_out = 2 if ups else 1
    return _call(body, name=name, grid=(t // tm, n // tn),
                 in_specs=[row, row, _fixed((2, N_MOD, d)), _w_spec(w, (tn, d), lambda i, j: (j, 0))] + ([tile, tile] if ups else []),
                 out_specs=[row, _fixed((2, 1, d))] + [tile] * n_out,
                 out_shape=[_sds((t, d), MXU), _sds((2, 1, d), F32)] + [_sds((t, n), MXU)] * n_out)(
                     dxn, y, mods, _w_arr(w), *(ups or ()))


def _mm_normb(a_list, w, xs, dres, g, mods, k_scale, s, name):
    t, d = xs.shape
    ka = a_list[0].shape[1]
    tm = _pick(t, (640, 256))
    tk = _pick(ka, (1408, 1536, 1024, 512, 256, 128))
    nk1 = ka // tk
    n_a = len(a_list)
    nk = nk1 * n_a

    def body(*refs):
        a_refs, (w_ref, x_ref, dr_ref, g_ref, m_ref, dx_ref, dsh_ref, dsc_ref, dg_ref, acc) = refs[:n_a], refs[n_a:]
        i, k = pl.program_id(0), pl.program_id(1)

        @pl.when((i == 0) & (k == 0))
        def _():
            dsh_ref[...] = jnp.zeros_like(dsh_ref)
            dsc_ref[...] = jnp.zeros_like(dsc_ref)
            dg_ref[...] = jnp.zeros_like(dg_ref)

        @pl.when(k == 0)
        def _():
            acc[...] = jnp.zeros_like(acc)

        for q in range(n_a):
            @pl.when((k >= q * nk1) & (k < (q + 1) * nk1))
            def _():
                acc[...] += _nt(a_refs[q][...], w_ref[...])

        @pl.when(k == nk - 1)
        def _():
            x = x_ref[...]
            dh = acc[...]
            rstd = lax.rsqrt(jnp.mean(x * x, axis=-1, keepdims=True) + EPS)
            xhat = x * rstd
            gg = g_ref[...]
            _stream_sums(i, tm, s, [(dsh_ref, dh), (dsc_ref, dh * (xhat * gg))])
            dy = dh * (1.0 + _mod_row(m_ref, k_scale, _ctx_rows(i, tm, s)))
            dg_ref[...] += jnp.sum(dy * xhat, axis=0, keepdims=True)
            dxh = dy * gg
            dx_ref[...] = dr_ref[...] + rstd * (dxh - xhat * jnp.mean(dxh * xhat, axis=-1, keepdims=True))

    row = pl.BlockSpec((tm, d), lambda i, k: (i, 0))
    a_specs = [pl.BlockSpec((tm, tk), lambda i, k, q=q: (i, jnp.clip(k - q * nk1, 0, nk1 - 1))) for q in range(n_a)]
    return _call(body, name=name, grid=(t // tm, nk),
                 in_specs=a_specs + [_w_spec(w, (d, tk), lambda i, k: (0, k)), row, row, _fixed((1, d)), _fixed((2, N_MOD, d))],
                 out_specs=[row, _fixed((2, 1, d)), _fixed((2, 1, d)), _fixed((1, d))],
                 out_shape=[_sds((t, d), F32), _sds((2, 1, d), F32), _sds((2, 1, d), F32), _sds((1, d), F32)],
                 scratch=[pltpu.VMEM((tm, d), F32)])(*a_list, _w_arr(w), xs, dres, g, mods)


def _rope_tables(s, ctx_len):
    n_freq = HEAD_DIM // 4
    tok = jnp.arange(s)
    freqs = ROPE_THETA ** (-jnp.arange(n_freq, dtype=F32) / n_freq)
    ang = jnp.concatenate([(tok // GRID_W).astype(F32)[:, None] * freqs, (tok % GRID_W).astype(F32)[:, None] * freqs], axis=-1)
    cos = jnp.repeat(jnp.cos(ang), 2, axis=-1)
    sin = jnp.repeat(jnp.sin(ang), 2, axis=-1) * jnp.tile(jnp.array([-1.0, 1.0], F32), HEAD_DIM // 2)
    cos = jnp.concatenate([jnp.tile(cos, (1, LANES // HEAD_DIM)), jnp.ones((ctx_len, LANES), F32)], axis=0)
    sin = jnp.concatenate([jnp.tile(sin, (1, LANES // HEAD_DIM)), jnp.zeros((ctx_len, LANES), F32)], axis=0)
    return cos, sin


def _swap_pairs(x):
    n = x.shape[-1]
    lane = lax.broadcasted_iota(jnp.int32, x.shape, 1)
    return jnp.where(lane % 2 == 0, pltpu.roll(x, n - 1, 1), pltpu.roll(x, 1, 1))


def _rope(p, cos, sin, name):
    t = p.shape[0]
    w = NA_WIDTH
    te = _row_tile(t)

    def body(q_ref, k_ref, v_ref, c_ref, s_ref, qo_ref, ko_ref, vo_ref):
        c, s = c_ref[...], s_ref[...]
        for hp in range(w // LANES):
            cols = slice(hp * LANES, (hp + 1) * LANES)
            q, k = q_ref[:, cols].astype(F32), k_ref[:, cols].astype(F32)
            qo_ref[:, cols] = (q * c + _swap_pairs(q) * s).astype(qo_ref.dtype)
            ko_ref[:, cols] = (k * c + _swap_pairs(k) * s).astype(ko_ref.dtype)
        vo_ref[...] = v_ref[...].astype(vo_ref.dtype)

    return _call(body, name=name, grid=(t // te,),
                 in_specs=[_rows(te, w, 0), _rows(te, w, 1), _rows(te, w, 2), _rows(te, LANES), _rows(te, LANES)],
                 out_specs=[_rows(te, w)] * 3, out_shape=[_sds((t, w), MXU)] * 3)(p, p, p, cos, sin)


def _rope_bwd(dq, dk, dv, cos, sin, name):
    t = dq.shape[0]
    te = _row_tile(t)
    n_pairs = NA_WIDTH // LANES

    def body(dq_ref, dk_ref, dv_ref, c_ref, s_ref, qo_ref, ko_ref, vo_ref):
        c, s = c_ref[...], s_ref[...]
        for hp in range(n_pairs):
            cols = slice(hp * LANES, (hp + 1) * LANES)
            a, b = dq_ref[:, cols], dk_ref[hp]
            qo_ref[:, cols] = (a * c + _swap_pairs(a * s)).astype(qo_ref.dtype)
            ko_ref[:, cols] = (b * c + _swap_pairs(b * s)).astype(ko_ref.dtype)
            vo_ref[:, cols] = dv_ref[hp].astype(vo_ref.dtype)

    pairs = pl.BlockSpec((n_pairs, te, LANES), lambda i: (0, i, 0))
    return _call(body, name=name, grid=(t // te,),
                 in_specs=[_rows(te, NA_WIDTH), pairs, pairs, _rows(te, LANES), _rows(te, LANES)],
                 out_specs=[_rows(te, NA_WIDTH)] * 3, out_shape=[_sds((t, NA_WIDTH), MXU)] * 3)(dq, dk, dv, cos, sin)


def _na_geometry(r_grid):
    rows = []
    for r0, ks in ((0, 0), (Q_ROWS, 0), (r_grid - Q_ROWS, r_grid - K_ROWS)):
        dr = np.zeros((Q_ROWS, K_ROWS), np.int32)
        vr = np.zeros((Q_ROWS, K_ROWS), bool)
        for a in range(Q_ROWS):
            r = r0 + a
            rs = min(max(r - WIN_H // 2, 0), r_grid - WIN_H)
            for i in range(K_ROWS):
                kr = ks + i
                vr[a, i] = rs <= kr <= rs + WIN_H - 1
                dr[a, i] = kr - r + WIN_H - 1
        rows.append((dr, vr))
    c = np.arange(GRID_W)
    cs = np.clip(c - WIN_W // 2, 0, GRID_W - WIN_W)
    kc = np.arange(GRID_W)
    vc = (kc[None, :] >= cs[:, None]) & (kc[None, :] <= cs[:, None] + WIN_W - 1)
    dc = kc[None, :] - c[:, None] + WIN_W - 1
    return rows, dc, vc


def _bias_table(rpb, r_grid):
    rows, _, vc = _na_geometry(r_grid)
    n_dc, off = 2 * WIN_W - 1, GRID_W - WIN_W
    u = jnp.pad(rpb, ((0, 0), (0, 0), (off, 2 * GRID_W - 1 - off - n_dc)))
    toep = jnp.stack([u[:, :, GRID_W - 1 - c:2 * GRID_W - 1 - c] for c in range(GRID_W)], axis=1)
    toep = jnp.pad(toep, ((0, 0), (0, 0), (Q_ROWS, Q_ROWS), (0, 0)))
    tabs = []
    for dr, vr in rows:
        per_row = []
        for a in range(Q_ROWS):
            lo = int(dr[a, 0]) + Q_ROWS
            blocks = [jnp.where(vc[None], toep[:, :, lo + i, :], NEG) if vr[a, i] else jnp.full((HEADS, GRID_W, GRID_W), NEG, F32)
                      for i in range(K_ROWS)]
            per_row.append(jnp.concatenate(blocks, axis=-1))
        tabs.append(jnp.stack(per_row, axis=1).reshape(HEADS, TQ, TK))
    tabs.append(jnp.full((HEADS, TQ, TK), NEG, F32))
    return jnp.stack(tabs)


def _variant(g, ngx):
    return jnp.where(g == 0, 0, jnp.where(g >= ngx, 3, jnp.where(g == ngx - 1, 2, 1)))


def _key_start(g, r_grid):
    return pl.multiple_of(jnp.clip(g * Q_ROWS - WIN_H // 2, 0, r_grid - K_ROWS) * GRID_W, TQ)


def _nt(a, b):
    return lax.dot_general(a, b, (((1,), (1,)), ((), ())), preferred_element_type=F32)


def _tn(a, b):
    return lax.dot_general(a, b, (((0,), (0,)), ((), ())), preferred_element_type=F32)


def _nn(a, b):
    return jnp.dot(a, b, preferred_element_type=F32)


def _head_mask(h):
    lane = lax.broadcasted_iota(jnp.int32, (1, LANES), 1)
    return ((lane >= HEAD_DIM * h) & (lane < HEAD_DIM * (h + 1))).astype(F32)


def _softmax_parts(qm, knb, kcx, bias):
    s_nb = _nt(qm, knb) + bias
    s_cx = _nt(qm, kcx)
    m = jnp.maximum(jnp.max(s_nb, axis=-1, keepdims=True), jnp.max(s_cx, axis=-1, keepdims=True))
    e_nb = jnp.exp(s_nb - m)
    e_cx = jnp.exp(s_cx - m)
    inv = 1.0 / (jnp.sum(e_nb, axis=-1, keepdims=True) + jnp.sum(e_cx, axis=-1, keepdims=True))
    return e_nb, e_cx, inv


def _na_specs(t, ngx):
    q_spec = pl.BlockSpec((TQ, LANES), lambda hp, g: (g, hp))
    kv_spec = pl.BlockSpec((t, LANES), lambda hp, g: (0, hp))
    b_spec = pl.BlockSpec((1, 2, TQ, TK), lambda hp, g: (_variant(g, ngx), hp, 0, 0))
    return q_spec, kv_spec, b_spec


def _na_fwd(q, k, v, bias, s, name):
    t = q.shape[0]
    ctx_len = t - s
    r_grid = s // GRID_W
    q_spec, kv_spec, b_spec = _na_specs(t, s // TQ)

    def body(q_ref, k_ref, v_ref, b_ref, o_ref):
        start = _key_start(pl.program_id(1), r_grid)
        qf = q_ref[...].astype(F32) * (HEAD_DIM ** -0.5)
        knb, vnb = k_ref[pl.ds(start, TK), :], v_ref[pl.ds(start, TK), :]
        kcx, vcx = k_ref[pl.ds(s, ctx_len), :], v_ref[pl.ds(s, ctx_len), :]
        acc = jnp.zeros((TQ, LANES), F32)
        for h in range(2):
            mask = _head_mask(h)
            e_nb, e_cx, inv = _softmax_parts((qf * mask).astype(MXU), knb, kcx, b_ref[0, h])
            acc += (_nn(e_nb.astype(MXU), vnb) + _nn(e_cx.astype(MXU), vcx)) * (inv * mask)
        o_ref[...] = acc.astype(o_ref.dtype)

    return _call(body, name=name, grid=(NA_WIDTH // LANES, t // TQ), in_specs=[q_spec, kv_spec, kv_spec, b_spec],
                 out_specs=q_spec, out_shape=_sds((t, NA_WIDTH), MXU))(q, k, v, bias)


def _na_bwd(q, k, v, do, bias, s, name):
    t = q.shape[0]
    ctx_len = t - s
    r_grid = s // GRID_W
    ng, ngx = t // TQ, s // TQ
    q_spec, kv_spec, b_spec = _na_specs(t, ngx)

    def body(q_ref, k_ref, v_ref, do_ref, b_ref, dq_ref, dk_hbm, dv_hbm, db_ref, dk_acc, dv_acc):
        hp, g = pl.program_id(0), pl.program_id(1)
        start = _key_start(g, r_grid)

        @pl.when(g == 0)
        def _():
            dk_acc[...] = jnp.zeros_like(dk_acc)
            dv_acc[...] = jnp.zeros_like(dv_acc)

        @pl.when((g == 0) | (g == 1) | (g == ngx - 1) | (g == ngx))
        def _():
            db_ref[...] = jnp.zeros_like(db_ref)

        qf = q_ref[...].astype(F32) * (HEAD_DIM ** -0.5)
        do = do_ref[...].astype(F32)
        knb, vnb = k_ref[pl.ds(start, TK), :], v_ref[pl.ds(start, TK), :]
        kcx, vcx = k_ref[pl.ds(s, ctx_len), :], v_ref[pl.ds(s, ctx_len), :]
        dq = jnp.zeros((TQ, LANES), F32)
        dk_nb = jnp.zeros((TK, LANES), F32)
        dv_nb = jnp.zeros((TK, LANES), F32)
        dk_cx = jnp.zeros((ctx_len, LANES), F32)
        dv_cx = jnp.zeros((ctx_len, LANES), F32)
        for h in range(2):
            mask = _head_mask(h)
            qm = (qf * mask).astype(MXU)
            dom = (do * mask).astype(MXU)
            e_nb, e_cx, inv = _softmax_parts(qm, knb, kcx, b_ref[0, h])
            dp_nb = _nt(dom, vnb)
            dp_cx = _nt(dom, vcx)
            delta = inv * (jnp.sum(e_nb * dp_nb, axis=-1, keepdims=True) + jnp.sum(e_cx * dp_cx, axis=-1, keepdims=True))
            ds_nb = e_nb * (inv * (dp_nb - delta))
            ds_cx = e_cx * (inv * (dp_cx - delta))
            db_ref[0, h] += ds_nb
            ds_nb, ds_cx = ds_nb.astype(MXU), ds_cx.astype(MXU)
            dq += (_nn(ds_nb, knb) + _nn(ds_cx, kcx)) * (mask * (HEAD_DIM ** -0.5))
            dk_nb += _tn(ds_nb, qm)
            dk_cx += _tn(ds_cx, qm)
            dom = (do * (inv * mask)).astype(MXU)
            dv_nb += _tn(e_nb.astype(MXU), dom)
            dv_cx += _tn(e_cx.astype(MXU), dom)
        dq_ref[...] = dq
        dk_acc[pl.ds(start, TK), :] += dk_nb
        dv_acc[pl.ds(start, TK), :] += dv_nb
        dk_acc[pl.ds(s, ctx_len), :] += dk_cx
        dv_acc[pl.ds(s, ctx_len), :] += dv_cx

        @pl.when(g == ng - 1)
        def _():
            pltpu.sync_copy(dk_acc, dk_hbm.at[hp])
            pltpu.sync_copy(dv_acc, dv_hbm.at[hp])

    n_pairs = NA_WIDTH // LANES
    hbm = pl.BlockSpec(memory_space=pl.ANY)
    return _call(body, name=name, grid=(n_pairs, ng), in_specs=[q_spec, kv_spec, kv_spec, q_spec, b_spec],
                 out_specs=[q_spec, hbm, hbm, b_spec],
                 out_shape=[_sds((t, NA_WIDTH), F32), _sds((n_pairs, t, LANES), F32), _sds((n_pairs, t, LANES), F32),
                            _sds((4, HEADS, TQ, TK), F32)],
                 scratch=[pltpu.VMEM((t, LANES), F32), pltpu.VMEM((t, LANES), F32)])(q, k, v, do, bias)


def _rpb_grad(dbias, r_grid, name):
    rows, _, _ = _na_geometry(r_grid)
    n_dr, half, skew, lanes = 2 * WIN_H - 1, WIN_W - 1, TK + 2, 896
    z = dbias[:3].reshape(3, HEADS, Q_ROWS, GRID_W, TK)
    z = jnp.pad(z, ((0, 0),) * 4 + ((0, 1),)).reshape(3, HEADS, Q_ROWS, GRID_W * (TK + 1))
    z = jnp.pad(z, ((0, 0),) * 3 + ((0, GRID_W),)).reshape(3, HEADS, Q_ROWS, GRID_W, skew)
    z = jnp.pad(z, ((0, 0),) * 4 + ((0, lanes - skew),))

    def body(z_ref, o_ref):
        sums = [jnp.sum(z_ref[v, 0, a], axis=0, keepdims=True) for v in range(3) for a in range(Q_ROWS)]
        zs = jnp.concatenate(sums + [jnp.zeros((16 - 3 * Q_ROWS, lanes), F32)], axis=0)
        acc = [jnp.zeros((1, lanes), F32) for _ in range(n_dr)]
        for i in range(K_ROWS):
            if i == 0:
                at0 = pltpu.roll(zs, lanes - (skew - half), 1) + pltpu.roll(zs, half, 1)
            else:
                at0 = pltpu.roll(zs, lanes - (i * GRID_W - half), 1)
            for v, (dr, vr) in enumerate(rows):
                for a in range(Q_ROWS):
                    if vr[a, i]:
                        acc[dr[a, i]] = acc[dr[a, i]] + at0[v * Q_ROWS + a:v * Q_ROWS + a + 1, :]
        o_ref[0] = jnp.concatenate(acc + [jnp.zeros((1, lanes), F32)], axis=0)

    o = _call(body, name=name, grid=(HEADS,),
              in_specs=[pl.BlockSpec((3, 1, Q_ROWS, GRID_W, lanes), lambda h: (0, h, 0, 0, 0))],
              out_specs=pl.BlockSpec((1, 16, lanes), lambda h: (h, 0, 0)), out_shape=_sds((HEADS, 16, lanes), F32))(z)
    return o[:, :n_dr, :2 * WIN_W - 1]


_GELU_K, _GELU_C = 0.7978845608028654, 0.044715


def _gelu(x):
    return 0.5 * x * (1.0 + jnp.tanh(_GELU_K * (x + _GELU_C * x * x * x)))


def _gelu_grad(x):
    th = jnp.tanh(_GELU_K * (x + _GELU_C * x * x * x))
    return 0.5 * (1.0 + th) + 0.5 * x * (1.0 - th * th) * (_GELU_K * (1.0 + 3.0 * _GELU_C * x * x))


def _ln_stats(v):
    mu = jnp.mean(v, axis=-1, keepdims=True)
    vc = v - mu
    rstd = lax.rsqrt(jnp.mean(vc * vc, axis=-1, keepdims=True) + EPS)
    return vc * rstd, rstd


def _gmlp(p, ln_g, ln_b, w_s, b_s, name):
    t = p.shape[0]
    te = _row_tile(t)
    w = SG_WIDTH
    cw = w // SG_GROUPS

    def body(u_ref, v_ref, g_ref, b_ref, ws_ref, bs_ref, o_ref):
        xhat, _ = _ln_stats(_gelu(v_ref[...].astype(F32)))
        vn = (xhat * g_ref[...] + b_ref[...]).astype(MXU)
        ug = _gelu(u_ref[...].astype(F32))
        for ci in range(te // SG_CHUNK):
            rs = slice(ci * SG_CHUNK, (ci + 1) * SG_CHUNK)
            for gi in range(SG_GROUPS):
                cs = slice(gi * cw, (gi + 1) * cw)
                sg = _nn(ws_ref[gi].astype(MXU), vn[rs, cs]) + bs_ref[gi]
                o_ref[rs, cs] = (ug[rs, cs] * sg).astype(o_ref.dtype)

    return _call(body, name=name, grid=(t // te,),
                 in_specs=[_rows(te, w, 3), _rows(te, w, 4), _fixed((1, w)), _fixed((1, w)),
                           _fixed((SG_GROUPS, SG_CHUNK, SG_CHUNK)), _fixed((SG_GROUPS, SG_CHUNK, 1))],
                 out_specs=_rows(te, w), out_shape=_sds((t, w), MXU))(p, p, ln_g, ln_b, w_s, b_s)


def _gmlp_bwd(p, dob, ln_g, ln_b, w_s, b_s, name):
    t = p.shape[0]
    te = _row_tile(t)
    w = SG_WIDTH
    cw = w // SG_GROUPS

    def body(u_ref, v_ref, do_ref, g_ref, b_ref, ws_ref, bs_ref, du_ref, dv_ref, dws_ref, dbs_ref, dg_ref, db_ref, dvn_ref):
        @pl.when(pl.program_id(0) == 0)
        def _():
            dws_ref[...] = jnp.zeros_like(dws_ref)
            dbs_ref[...] = jnp.zeros_like(dbs_ref)
            dg_ref[...] = jnp.zeros_like(dg_ref)
            db_ref[...] = jnp.zeros_like(db_ref)

        u, v = u_ref[...].astype(F32), v_ref[...].astype(F32)
        xhat, rstd = _ln_stats(_gelu(v))
        vn = (xhat * g_ref[...] + b_ref[...]).astype(MXU)
        ug = _gelu(u)
        dob = do_ref[...].astype(F32)
        for ci in range(te // SG_CHUNK):
            rs = slice(ci * SG_CHUNK, (ci + 1) * SG_CHUNK)
            for gi in range(SG_GROUPS):
                cs = slice(gi * cw, (gi + 1) * cw)
                wsg = ws_ref[gi].astype(MXU)
                sg = _nn(wsg, vn[rs, cs]) + bs_ref[gi]
                du_ref[rs, cs] = (dob[rs, cs] * sg * _gelu_grad(u[rs, cs])).astype(du_ref.dtype)
                ds = dob[rs, cs] * ug[rs, cs]
                dbs_ref[gi] += jnp.sum(ds, axis=-1, keepdims=True)
                ds = ds.astype(MXU)
                dws_ref[gi] += _nt(ds, vn[rs, cs])
                dvn_ref[rs, cs] = _tn(wsg, ds)
        dvn = dvn_ref[...]
        dg_ref[...] += jnp.sum(dvn * xhat, axis=0, keepdims=True)
        db_ref[...] += jnp.sum(dvn, axis=0, keepdims=True)
        dxh = dvn * g_ref[...]
        dvg = rstd * (dxh - jnp.mean(dxh, axis=-1, keepdims=True) - xhat * jnp.mean(dxh * xhat, axis=-1, keepdims=True))
        dv_ref[...] = (dvg * _gelu_grad(v)).astype(dv_ref.dtype)

    return _call(body, name=name, grid=(t // te,),
                 in_specs=[_rows(te, w, 3), _rows(te, w, 4), _rows(te, w), _fixed((1, w)), _fixed((1, w)),
                           _fixed((SG_GROUPS, SG_CHUNK, SG_CHUNK)), _fixed((SG_GROUPS, SG_CHUNK, 1))],
                 out_specs=[_rows(te, w), _rows(te, w), _fixed((SG_GROUPS, SG_CHUNK, SG_CHUNK)),
                            _fixed((SG_GROUPS, SG_CHUNK, 1)), _fixed((1, w)), _fixed((1, w))],
                 out_shape=[_sds((t, w), MXU), _sds((t, w), MXU), _sds((SG_GROUPS, SG_CHUNK, SG_CHUNK), F32),
                            _sds((SG_GROUPS, SG_CHUNK, 1), F32), _sds((1, w), F32), _sds((1, w), F32)],
                 scratch=[pltpu.VMEM((te, w), F32)])(p, p, dob, ln_g, ln_b, w_s, b_s)


def _merge(pa, pb, p, b_gate, name):
    t, d = pa.shape
    te = _row_tile(t)
    hw = NA_WIDTH
    nh = d // hw
    c0 = (NA_WIDTH * 3 + SG_WIDTH * 2) // hw

    def body(pa_ref, pb_ref, la_ref, lb_ref, ba_ref, bb_ref, o_ref):
        ga = jax.nn.sigmoid(la_ref[...].astype(F32) + ba_ref[...])
        gb = jax.nn.sigmoid(lb_ref[...].astype(F32) + bb_ref[...])
        o_ref[...] = (ga * pa_ref[...].astype(F32) + gb * pb_ref[...].astype(F32)).astype(o_ref.dtype)

    tile = pl.BlockSpec((te, hw), lambda i, j: (i, j))
    return _call(body, name=name, grid=(t // te, nh),
                 in_specs=[tile, tile, pl.BlockSpec((te, hw), lambda i, j: (i, c0 + j)),
                           pl.BlockSpec((te, hw), lambda i, j: (i, c0 + nh + j)),
                           pl.BlockSpec((1, hw), lambda i, j: (0, j)), pl.BlockSpec((1, hw), lambda i, j: (0, nh + j))],
                 out_specs=tile, out_shape=_sds((t, d), MXU))(pa, pb, p, p, b_gate, b_gate)


def _merge_bwd(dmg, pa, pb, p, b_gate, name):
    t, d = pa.shape
    te = _row_tile(t)
    hw = NA_WIDTH
    nh = d // hw
    c0 = (NA_WIDTH * 3 + SG_WIDTH * 2) // hw

    def body(dm_ref, pa_ref, pb_ref, la_ref, lb_ref, ba_ref, bb_ref, dpa_ref, dpb_ref, dla_ref, dlb_ref, dba_ref, dbb_ref):
        @pl.when(pl.program_id(1) == 0)
        def _():
            dba_ref[...] = jnp.zeros_like(dba_ref)
            dbb_ref[...] = jnp.zeros_like(dbb_ref)

        dm = dm_ref[...].astype(F32)
        ga = jax.nn.sigmoid(la_ref[...].astype(F32) + ba_ref[...])
        gb = jax.nn.sigmoid(lb_ref[...].astype(F32) + bb_ref[...])
        dpa_ref[...] = (dm * ga).astype(dpa_ref.dtype)
        dpb_ref[...] = (dm * gb).astype(dpb_ref.dtype)
        dla = dm * pa_ref[...].astype(F32) * ga * (1.0 - ga)
        dlb = dm * pb_ref[...].astype(F32) * gb * (1.0 - gb)
        dla_ref[...] = dla.astype(dla_ref.dtype)
        dlb_ref[...] = dlb.astype(dlb_ref.dtype)
        dba_ref[...] += jnp.sum(dla, axis=0, keepdims=True)
        dbb_ref[...] += jnp.sum(dlb, axis=0, keepdims=True)

    tile = pl.BlockSpec((te, hw), lambda j, i: (i, j))
    bias_a = pl.BlockSpec((1, hw), lambda j, i: (0, j))
    bias_b = pl.BlockSpec((1, hw), lambda j, i: (0, nh + j))
    return _call(body, name=name, grid=(nh, t // te),
                 in_specs=[tile, tile, tile, pl.BlockSpec((te, hw), lambda j, i: (i, c0 + j)),
                           pl.BlockSpec((te, hw), lambda j, i: (i, c0 + nh + j)), bias_a, bias_b],
                 out_specs=[tile, tile, tile, tile, bias_a, bias_a],
                 out_shape=[_sds((t, d), MXU)] * 4 + [_sds((1, d), F32)] * 2)(dmg, pa, pb, p, p, b_gate, b_gate)


def _final(xs, tgt, g, name):
    t, d = xs.shape
    nx = tgt.shape[0] // TM

    def body(x_ref, t_ref, g_ref, l_ref, dx_ref, dg_ref):
        i = pl.program_id(0)

        @pl.when(i == 0)
        def _():
            l_ref[...] = jnp.zeros_like(l_ref)
            dg_ref[...] = jnp.zeros_like(dg_ref)

        @pl.when(i < nx)
        def _():
            x = x_ref[...]
            rstd = lax.rsqrt(jnp.mean(x * x, axis=-1, keepdims=True) + EPS)
            xhat = x * rstd
            err = xhat * g_ref[...] - t_ref[...]
            l_ref[...] += 0.5 * jnp.sum(jnp.mean(err * err, axis=-1, keepdims=True))
            dy = err * (1.0 / d)
            dg_ref[...] += jnp.sum(dy * xhat, axis=0, keepdims=True)
            dxh = dy * g_ref[...]
            dx_ref[...] = rstd * (dxh - xhat * jnp.mean(dxh * xhat, axis=-1, keepdims=True))

        @pl.when(i >= nx)
        def _():
            dx_ref[...] = jnp.zeros_like(dx_ref)

    return _call(body, name=name, grid=(t // TM,),
                 in_specs=[_rows(TM, d), pl.BlockSpec((TM, d), lambda i: (jnp.minimum(i, nx - 1), 0)), _fixed((1, d))],
                 out_specs=[_fixed((1, LANES)), _rows(TM, d), _fixed((1, d))],
                 out_shape=[_sds((1, LANES), F32), _sds((t, d), F32), _sds((1, d), F32)])(xs, tgt, g)


def _view2d(a):
    return a.reshape(1, -1) if a.ndim == 1 else a.reshape(-1, a.shape[-1])


def _tile_rows(r, c):
    for cand in (1024, 512, 256, 128, 64, 32, 16):
        if r % cand == 0 and cand * c * 4 <= 2 ** 20:
            return cand
    return r


def _pair_sum(g, recv, layer, name):
    _, a, b = g.shape
    tr = _tile_rows(a, b)

    def body(l_ref, g_ref, r_ref, o32_ref, o16_ref):
        acc = g_ref[...] + r_ref[...]
        o32_ref[...] = acc
        o16_ref[...] = acc.astype(o16_ref.dtype)

    first = pl.BlockSpec((None, tr, b), lambda i, l: (0, i, 0))
    return pl.pallas_call(
        body, name=name, out_shape=[_sds((1, a, b), F32), _sds((1, a, b), MXU)],
        grid_spec=pltpu.PrefetchScalarGridSpec(
            num_scalar_prefetch=1, grid=(a // tr,),
            in_specs=[pl.BlockSpec((None, tr, b), lambda i, l: (l[0], i, 0)), first], out_specs=[first, first]),
        compiler_params=pltpu.CompilerParams(dimension_semantics=("arbitrary",), vmem_limit_bytes=VMEM_LIMIT))(layer, g, recv)


def _ew(fn, arrays, out_dtypes, name):
    shape = arrays[0].shape
    views = [_view2d(a) for a in arrays]
    r, c = views[0].shape
    tr = _tile_rows(r, c)

    def body(*refs):
        outs = fn(*[ref[...] for ref in refs[:len(views)]])
        for ref, o in zip(refs[len(views):], outs):
            ref[...] = o.astype(ref.dtype)

    res = _call(body, name=name, grid=(r // tr,), in_specs=[_rows(tr, c)] * len(views), out_specs=[_rows(tr, c)] * len(out_dtypes),
                out_shape=[_sds((r, c), dt) for dt in out_dtypes])(*views)
    return [o.reshape(shape) for o in res]


def _sum_pieces(pieces, name, out_dtypes=(F32,)):
    def fn(*vals):
        acc = vals[0].astype(F32)
        for v in vals[1:]:
            acc = acc + v.astype(F32)
        return (acc,) * len(out_dtypes)

    return _ew(fn, pieces, list(out_dtypes), name)


def _adam_update(w, g, m, v):
    m2 = ADAM_B1 * m + (1.0 - ADAM_B1) * g
    v2 = ADAM_B2 * v + (1.0 - ADAM_B2) * (g * g)
    m_hat = m2 / (1.0 - ADAM_B1 ** ADAM_STEP)
    v_hat = v2 / (1.0 - ADAM_B2 ** ADAM_STEP)
    delta = -ADAM_LR * (m_hat / (jnp.sqrt(v_hat) + ADAM_EPS) + ADAM_WD * w)
    return g, delta, m2, v2


def _adamw(w, g_pieces, m, v, name):
    n_g = len(g_pieces)

    def fn(w_, *rest):
        g = rest[0]
        for piece in rest[1:n_g]:
            g = g + piece
        return _adam_update(w_, g, rest[n_g], rest[n_g + 1])

    return _ew(fn, [w, *g_pieces, m, v], [F32] * 4, name)


def _adamw_layers(w, mine, other, m, v, layer, name):
    _, a, b = w.shape
    tr = _tile_rows(a, b)
    nb = a // tr

    def body(l_ref, w_ref, mine_ref, other_ref, m_ref, v_ref, *o_refs):
        g = jnp.where(pl.program_id(0) // nb == l_ref[0], mine_ref[...], other_ref[...])
        for ref, val in zip(o_refs, _adam_update(w_ref[...], g, m_ref[...], v_ref[...])):
            ref[...] = val

    both = pl.BlockSpec((None, tr, b), lambda i, l: (i // nb, i % nb, 0))
    one = pl.BlockSpec((None, tr, b), lambda i, l: (0, i % nb, 0))
    return pl.pallas_call(
        body, name=name, out_shape=[_sds(w.shape, F32)] * 4,
        grid_spec=pltpu.PrefetchScalarGridSpec(num_scalar_prefetch=1, grid=(2 * nb,), in_specs=[both, one, one, both, both],
                                               out_specs=[both] * 4),
        compiler_params=pltpu.CompilerParams(dimension_semantics=("arbitrary",), vmem_limit_bytes=VMEM_LIMIT))(
            layer, w, mine, other, m, v)


def _place_shard(full, shard, chip, axis, name):
    _, a, b = shard.shape
    tr = _tile_rows(a, b)
    nb = a // tr

    def body(c_ref, s_ref, f_ref, o_ref):
        o_ref[...] = s_ref[...]

    if axis == 1:
        out_spec = pl.BlockSpec((None, tr, b), lambda l, i, c: (l, c[0] * nb + i, 0))
    else:
        out_spec = pl.BlockSpec((None, tr, b), lambda l, i, c: (l, i, c[0]))
    return pl.pallas_call(
        body, name=name, out_shape=_sds(full.shape, full.dtype), input_output_aliases={2: 0},
        grid_spec=pltpu.PrefetchScalarGridSpec(
            num_scalar_prefetch=1, grid=(2, nb),
            in_specs=[pl.BlockSpec((None, tr, b), lambda l, i, c: (l, i, 0)), pl.BlockSpec(memory_space=pl.ANY)],
            out_specs=out_spec),
        compiler_params=pltpu.CompilerParams(dimension_semantics=("arbitrary", "arbitrary"), vmem_limit_bytes=VMEM_LIMIT))(
            chip, shard, full)


def _ada_fwd(cond, w, b, name):
    r, d = cond.shape
    n = w.shape[1]
    tn = _pick(n, (1152, 768, 512, 384, 256, 128))

    def body(c_ref, w_ref, b_ref, o_ref, s_ref):
        c = c_ref[...]
        sc = c * jax.nn.sigmoid(c)
        s_ref[...] = sc
        o_ref[...] = _nn(sc.astype(MXU), w_ref[...].astype(MXU)) + b_ref[...]

    return _call(body, name=name, grid=(n // tn,),
                 in_specs=[_fixed((r, d)), pl.BlockSpec((d, tn), lambda j: (0, j)), pl.BlockSpec((1, tn), lambda j: (0, j))],
                 out_specs=[pl.BlockSpec((r, tn), lambda j: (0, j)), _fixed((r, d))],
                 out_shape=[_sds((r, n), F32), _sds((r, d), F32)])(cond, w, b)


def _cctx_grad(parts, c_ctx, name):
    n, d = parts.shape

    def body(p_ref, c_ref, o_ref):
        c = c_ref[...]
        sg = jax.nn.sigmoid(c)
        acc = p_ref[0:1, :]
        for j in range(1, n):
            acc = acc + p_ref[j:j + 1, :]
        o_ref[...] = acc * (sg * (1.0 + c * (1.0 - sg)))

    return _call(body, name=name, grid=(1,), in_specs=[_fixed((n, d)), _fixed((1, d))], out_specs=_fixed((1, d)),
                 out_shape=_sds((1, d), F32))(parts, c_ctx)


def _here():
    return lax.axis_index("x"), lax.axis_index("y"), lax.axis_index("c")


def _flip(v, bit):
    return 1 - v if bit else v


def _allgather8(xb, name):
    r, n = xb.shape

    def body(x_ref, out_ref, send_sems, recv_sems, local_sem):
        x, y, c = _here()
        me = 4 * x + 2 * y + c
        local = pltpu.make_async_copy(x_ref, out_ref.at[me], local_sem)
        local.start()
        sends = []
        for k in range(1, 8):
            peer = (_flip(x, k & 4), _flip(y, k & 2), _flip(c, k & 1))
            cp = pltpu.make_async_remote_copy(src_ref=x_ref, dst_ref=out_ref.at[me], send_sem=send_sems.at[k - 1],
                                              recv_sem=recv_sems.at[k - 1], device_id=peer, device_id_type=MESH)
            cp.start()
            sends.append(cp)
        for k in range(1, 8):
            peer = (_flip(x, k & 4), _flip(y, k & 2), _flip(c, k & 1))
            src = 4 * peer[0] + 2 * peer[1] + peer[2]
            pltpu.make_async_remote_copy(src_ref=x_ref, dst_ref=out_ref.at[src], send_sem=send_sems.at[k - 1],
                                         recv_sem=recv_sems.at[k - 1], device_id=peer, device_id_type=MESH).wait_recv()
        for cp in sends:
            cp.wait_send()
        local.wait()

    vmem = pl.BlockSpec(memory_space=pltpu.VMEM)
    return pl.pallas_call(
        body, name=name, out_shape=_sds((8, r, n), xb.dtype), in_specs=[vmem], out_specs=vmem,
        scratch_shapes=[pltpu.SemaphoreType.DMA((7,)), pltpu.SemaphoreType.DMA((7,)), pltpu.SemaphoreType.DMA(())],
        compiler_params=pltpu.CompilerParams(vmem_limit_bytes=VMEM_LIMIT))(xb)


def _shard_of(ref, axis, j, size):
    sl = pl.ds(j * size, size)
    return ref.at[:, sl, :] if axis == 1 else ref.at[:, :, sl]


def _piece(ref, axis, j, size, layer):
    lay, sl = pl.ds(layer, 1), pl.ds(j * size, size)
    return ref.at[lay, sl, :] if axis == 1 else ref.at[lay, :, sl]


def _gather_chips(shards, axes, name):
    n = len(shards)
    fulls = []
    for a, ax in zip(shards, axes):
        assert a.shape[0] == 2
        shp = list(a.shape)
        shp[ax] *= 4
        fulls.append(_sds(tuple(shp), a.dtype))

    def body(*refs):
        ins, outs = refs[:n], refs[n:2 * n]
        ici_send, ici_recv, d2d_send, d2d_recv = refs[2 * n:]
        x, y, c = _here()
        chips = [(_flip(x, k & 2), _flip(y, k & 1)) for k in range(1, 4)]
        sends = []
        for a in range(n):
            size = ins[a].shape[axes[a]]
            for j, (px, py) in enumerate(chips):
                cp = pltpu.make_async_remote_copy(src_ref=ins[a].at[pl.ds(c, 1)], dst_ref=_piece(outs[a], axes[a], 2 * x + y, size, c),
                                                  send_sem=ici_send.at[3 * a + j], recv_sem=ici_recv.at[3 * a + j],
                                                  device_id=(px, py, c), device_id_type=MESH)
                cp.start()
                sends.append(cp)
        for a in range(n):
            size = ins[a].shape[axes[a]]
            for j, (px, py) in enumerate(chips):
                landed = _piece(outs[a], axes[a], 2 * px + py, size, c)
                pltpu.make_async_remote_copy(src_ref=ins[a].at[pl.ds(c, 1)], dst_ref=landed, send_sem=ici_send.at[3 * a + j],
                                             recv_sem=ici_recv.at[3 * a + j], device_id=(px, py, c), device_id_type=MESH).wait_recv()
                cp = pltpu.make_async_remote_copy(src_ref=landed, dst_ref=landed, send_sem=d2d_send.at[3 * a + j],
                                                  recv_sem=d2d_recv.at[3 * a + j], device_id=(x, y, 1 - c), device_id_type=MESH)
                cp.start()
                sends.append(cp)
        for a in range(n):
            size = ins[a].shape[axes[a]]
            for j, (px, py) in enumerate(chips):
                passed = _piece(outs[a], axes[a], 2 * px + py, size, 1 - c)
                pltpu.make_async_remote_copy(src_ref=passed, dst_ref=passed, send_sem=d2d_send.at[3 * a + j],
                                             recv_sem=d2d_recv.at[3 * a + j], device_id=(x, y, 1 - c), device_id_type=MESH).wait_recv()
        for cp in sends:
            cp.wait_send()

    hbm = pl.BlockSpec(memory_space=pl.ANY)
    return pl.pallas_call(
        body, name=name, out_shape=fulls, in_specs=[hbm] * n, out_specs=[hbm] * n,
        scratch_shapes=[pltpu.SemaphoreType.DMA((3 * n,))] * 4)(*shards)


def _swap_layers(arrays, name):
    n = len(arrays)

    def body(*refs):
        ins, outs = refs[:n], refs[n:2 * n]
        send_sems, recv_sems = refs[2 * n:]
        x, y, c = _here()
        copies = []
        for a in range(n):
            cp = pltpu.make_async_remote_copy(src_ref=ins[a].at[pl.ds(1 - c, 1)], dst_ref=outs[a], send_sem=send_sems.at[a],
                                              recv_sem=recv_sems.at[a], device_id=(x, y, 1 - c), device_id_type=MESH)
            cp.start()
            copies.append(cp)
        for cp in copies:
            cp.wait()

    hbm = pl.BlockSpec(memory_space=pl.ANY)
    return pl.pallas_call(
        body, name=name, out_shape=[_sds((1, *a.shape[1:]), a.dtype) for a in arrays], in_specs=[hbm] * n, out_specs=[hbm] * n,
        scratch_shapes=[pltpu.SemaphoreType.DMA((n,)), pltpu.SemaphoreType.DMA((n,))])(*arrays)


def _scatter_chips(fulls, axes, name):
    n = len(fulls)
    recvs = []
    for a, ax in zip(fulls, axes):
        shp = list(a.shape)
        shp[ax] //= 4
        recvs.append(_sds((3, *shp), a.dtype))

    def body(*refs):
        ins, outs = refs[:n], refs[n:2 * n]
        send_sems, recv_sems = refs[2 * n:]
        x, y, c = _here()
        sends = []
        for a in range(n):
            size = ins[a].shape[axes[a]] // 4
            for k in range(1, 4):
                peer = (_flip(x, k & 2), _flip(y, k & 1), c)
                cp = pltpu.make_async_remote_copy(src_ref=_shard_of(ins[a], axes[a], 2 * peer[0] + peer[1], size),
                                                  dst_ref=outs[a].at[k - 1],
                                                  send_sem=send_sems.at[3 * a + k - 1], recv_sem=recv_sems.at[3 * a + k - 1],
                                                  device_id=peer, device_id_type=MESH)
                cp.start()
                sends.append(cp)
        for cp in sends:
            cp.wait_recv()
        for cp in sends:
            cp.wait_send()

    hbm = pl.BlockSpec(memory_space=pl.ANY)
    return pl.pallas_call(
        body, name=name, out_shape=recvs, in_specs=[hbm] * n, out_specs=[hbm] * n,
        scratch_shapes=[pltpu.SemaphoreType.DMA((3 * n,)), pltpu.SemaphoreType.DMA((3 * n,))])(*fulls)


def _sibling_swap(arrays, name):
    n = len(arrays)

    def body(*refs):
        ins, outs = refs[:n], refs[n:2 * n]
        send_sems, recv_sems = refs[2 * n:]
        x, y, c = _here()
        copies = []
        for a in range(n):
            cp = pltpu.make_async_remote_copy(src_ref=ins[a], dst_ref=outs[a], send_sem=send_sems.at[a], recv_sem=recv_sems.at[a],
                                              device_id=(x, y, 1 - c), device_id_type=MESH)
            cp.start()
            copies.append(cp)
        for cp in copies:
            cp.wait()

    hbm = pl.BlockSpec(memory_space=pl.ANY)
    return pl.pallas_call(
        body, name=name, out_shape=[_sds(a.shape, a.dtype) for a in arrays], in_specs=[hbm] * n, out_specs=[hbm] * n,
        scratch_shapes=[pltpu.SemaphoreType.DMA((n,)), pltpu.SemaphoreType.DMA((n,))])(*arrays)


def _ffn_fwd(xs, g, mods, k0, w_up, w_down, s, tag):
    h, ua, ub, act = _norm_mm(xs, g, mods, k0, k0 + 1, w_up, s, tag + "_up", True)
    y, xn = _mm_res(act, w_down, xs, mods, k0 + 2, 0.5, s, tag + "_down")
    return xn, (xs, h, ua, ub, act, y)


def _dw(gw, key, a, b, name, col0=0, n_total=None):
    shape = (gw["depth"], a.shape[1], n_total or b.shape[1])
    gw[key] = _mm(a, b, "tn", F32, name, into=(gw.get(key), shape, gw["layer"], col0))


def _ffn_bwd(dxn, saved, g, mods, k0, w_up, w_down, s, tag, gw, up_key, down_key):
    xs, h, ua, ub, act, y = saved
    dy, dgate, dua, dub = _resb_mm(dxn, y, mods, k0 + 2, 0.5, w_down, s, tag + "_down_dx", (ua, ub))
    _dw(gw, down_key, act, dy, tag + "_down_dw")
    f = dua.shape[1]
    _dw(gw, up_key, h, dua, tag + "_upa_dw", 0, 2 * f)
    _dw(gw, up_key, h, dub, tag + "_upb_dw", f, 2 * f)
    dx, dsh, dsc, dg = _mm_normb([dua, dub], w_up, xs, dxn, g, mods, k0 + 1, s, tag + "_up_dx")
    return dx, dg, [dsh, dsc, dgate]


def _mix_fwd(xs, g, mods, wl, pl_, tabs, s, tag):
    cos, sin = tabs
    h, p = _norm_mm(xs, g, mods, 3, 4, wl["w_in"], s, tag + "_in", False)
    q, k, v = _rope(p, cos, sin, tag + "_rope")
    bias = _bias_table(pl_["rpb"], s // GRID_W)
    oa = _na_fwd(q, k, v, bias, s, tag + "_na")
    ob = _gmlp(p, pl_["ln_v_g"], pl_["ln_v_b"], pl_["w_s"], pl_["b_s"], tag + "_sg")
    pa = _mm(oa, wl["w_pa"], "nn", MXU, tag + "_pa")
    pb = _mm(ob, wl["w_pb"], "nn", MXU, tag + "_pb")
    mg = _merge(pa, pb, p, pl_["b_gate"], tag + "_merge")
    y, xn = _mm_res(mg, wl["w_o"], xs, mods, 5, 1.0, s, tag + "_o")
    return xn, (xs, h, p, q, k, v, bias, oa, ob, pa, pb, mg, y)


def _mix_bwd(dxn, saved, g, mods, wl, pl_, tabs, s, tag, gw):
    xs, h, p, q, k, v, bias, oa, ob, pa, pb, mg, y = saved
    cos, sin = tabs
    gp = {}
    dy, dgate, dmg = _resb_mm(dxn, y, mods, 5, 1.0, wl["w_o"], s, tag + "_o_dx")
    _dw(gw, "w_o", mg, dy, tag + "_o_dw")
    dpa, dpb, dla, dlb, dba, dbb = _merge_bwd(dmg, pa, pb, p, pl_["b_gate"], tag + "_merge_b")
    gp["b_gate"] = jnp.concatenate([dba, dbb], axis=1)
    _dw(gw, "w_pa", oa, dpa, tag + "_pa_dw")
    doa = _mm(dpa, wl["w_pa"], "nt", MXU, tag + "_pa_dx")
    _dw(gw, "w_pb", ob, dpb, tag + "_pb_dw")
    dob = _mm(dpb, wl["w_pb"], "nt", MXU, tag + "_pb_dx")
    du, dvs, gp["w_s"], gp["b_s"], gp["ln_v_g"], gp["ln_v_b"] = _gmlp_bwd(
        p, dob, pl_["ln_v_g"], pl_["ln_v_b"], pl_["w_s"], pl_["b_s"], tag + "_sg_b")
    dqr, dkr, dv, dbias = _na_bwd(q, k, v, doa, bias, s, tag + "_na_b")
    gp["rpb"] = _rpb_grad(dbias, s // GRID_W, tag + "_rpb")
    dq, dk, dvv = _rope_bwd(dqr, dkr, dv, cos, sin, tag + "_rope_b")
    dp = jnp.concatenate([dq, dk, dvv, du, dvs, dla, dlb], axis=1)
    _dw(gw, "w_in", h, dp, tag + "_in_dw")
    dx, dsh, dsc, dg = _mm_normb([dp], wl["w_in"], xs, dxn, g, mods, 4, s, tag + "_in_dx")
    return dx, gp, dg, [dsh, dsc, dgate]


def _local_step(x, ctx, tgt, mods, wts, prm):
    s, d = x.shape
    depth = mods.shape[0]
    tabs = _rope_tables(s, ctx.shape[0])
    xs = jnp.concatenate([x, ctx], axis=0)
    saved = []
    for l in range(depth):
        wl = {k: (v, l) for k, v in wts.items()}
        pl_ = _layer_params(prm, l)
        xs, s1 = _ffn_fwd(xs, pl_["g"][0], mods[l], 0, wl["w_ff1_up"], wl["w_ff1_down"], s, f"l{l}_ff1")
        xs, s2 = _mix_fwd(xs, pl_["g"][1], mods[l], wl, pl_, tabs, s, f"l{l}_mix")
        xs, s3 = _ffn_fwd(xs, pl_["g"][2], mods[l], 6, wl["w_ff2_up"], wl["w_ff2_down"], s, f"l{l}_ff2")
        saved.append((s1, s2, s3))
    loss, dxs, d_final_g = _final(xs, tgt, prm["final_g"].reshape(1, d), "final")
    gw = {"depth": depth}
    gp = {k: [None] * depth for k in ("norm_g", "b_gate", "rpb", "ln_v_g", "ln_v_b", "w_s", "b_s")}
    dmods = [None] * depth
    for l in reversed(range(depth)):
        wl = {k: (v, l) for k, v in wts.items()}
        pl_ = _layer_params(prm, l)
        s1, s2, s3 = saved[l]
        gw["layer"] = l
        dxs, dg2, dm2 = _ffn_bwd(dxs, s3, pl_["g"][2], mods[l], 6, wl["w_ff2_up"], wl["w_ff2_down"], s, f"l{l}_ff2",
                                 gw, "w_ff2_up", "w_ff2_down")
        dxs, gpm, dg1, dm1 = _mix_bwd(dxs, s2, pl_["g"][1], mods[l], wl, pl_, tabs, s, f"l{l}_mix", gw)
        dxs, dg0, dm0 = _ffn_bwd(dxs, s1, pl_["g"][0], mods[l], 0, wl["w_ff1_up"], wl["w_ff1_down"], s, f"l{l}_ff1",
                                 gw, "w_ff1_up", "w_ff1_down")
        gp["b_gate"][l] = gpm["b_gate"][0]
        gp["rpb"][l] = gpm["rpb"]
        gp["ln_v_g"][l] = gpm["ln_v_g"][0]
        gp["ln_v_b"][l] = gpm["ln_v_b"][0]
        gp["w_s"][l] = gpm["w_s"]
        gp["b_s"][l] = gpm["b_s"][..., 0]
        gp["norm_g"][l] = jnp.concatenate([dg0, dg1, dg2], axis=0)
        dmods[l] = jnp.concatenate(dm0 + dm1 + dm2, axis=1)
    gw = {k: gw[k] for k in wts}
    gp = {k: jnp.stack(v) for k, v in gp.items()}
    gp["final_g"] = d_final_g[0]
    return loss[0, 0], dxs[:s], jnp.stack(dmods), gw, gp


def _layer_params(prm, l):
    d = prm["norm_g"].shape[-1]
    return {
        "g": [prm["norm_g"][l, i].reshape(1, d) for i in range(3)],
        "b_gate": prm["b_gate"][l].reshape(1, -1),
        "rpb": prm["rpb"][l],
        "ln_v_g": prm["ln_v_g"][l].reshape(1, -1),
        "ln_v_b": prm["ln_v_b"][l].reshape(1, -1),
        "w_s": prm["w_s"][l],
        "b_s": prm["b_s"][l][..., None],
    }


SMALL = ("norm_g", "b_gate", "rpb", "ln_v_g", "ln_v_b", "w_s", "b_s", "final_g")
PACK_LANES = 1024


def _pack(parts):
    flat = jnp.concatenate([p.reshape(-1) for p in parts])
    rows = -(-flat.shape[0] // PACK_LANES)
    rows = -(-rows // 8) * 8
    return jnp.pad(flat, (0, rows * PACK_LANES - flat.shape[0])).reshape(rows, PACK_LANES)


def _unpack(flat, shapes):
    out, off = [], 0
    for shp in shapes:
        n = int(np.prod(shp))
        out.append(flat[..., off:off + n].reshape(*flat.shape[:-1], *shp))
        off += n
    return out


def kernel(x, c, ctx, c_ctx, w_ada, b_ada, norm_g, w_ff1_up, w_ff1_down, w_in, b_gate, rpb, ln_v_g, ln_v_b, w_s, b_s, w_pa, w_pb, w_o, w_ff2_up, w_ff2_down, final_g, loss_target, m_c_ctx, m_w_ada, m_b_ada, m_norm_g, m_w_ff1_up, m_w_ff1_down, m_w_in, m_b_gate, m_rpb, m_ln_v_g, m_ln_v_b, m_w_s, m_b_s, m_w_pa, m_w_pb, m_w_o, m_w_ff2_up, m_w_ff2_down, m_final_g, v_c_ctx, v_w_ada, v_b_ada, v_norm_g, v_w_ff1_up, v_w_ff1_down, v_w_in, v_b_gate, v_rpb, v_ln_v_g, v_ln_v_b, v_w_s, v_b_s, v_w_pa, v_w_pb, v_w_o, v_w_ff2_up, v_w_ff2_down, v_final_g):
    weights = dict(c_ctx=c_ctx, w_ada=w_ada, b_ada=b_ada, norm_g=norm_g, w_ff1_up=w_ff1_up, w_ff1_down=w_ff1_down, w_in=w_in,
                   b_gate=b_gate, rpb=rpb, ln_v_g=ln_v_g, ln_v_b=ln_v_b, w_s=w_s, b_s=b_s, w_pa=w_pa, w_pb=w_pb, w_o=w_o,
                   w_ff2_up=w_ff2_up, w_ff2_down=w_ff2_down, final_g=final_g)
    mom_m = dict(c_ctx=m_c_ctx, w_ada=m_w_ada, b_ada=m_b_ada, norm_g=m_norm_g, w_ff1_up=m_w_ff1_up, w_ff1_down=m_w_ff1_down,
                 w_in=m_w_in, b_gate=m_b_gate, rpb=m_rpb, ln_v_g=m_ln_v_g, ln_v_b=m_ln_v_b, w_s=m_w_s, b_s=m_b_s, w_pa=m_w_pa,
                 w_pb=m_w_pb, w_o=m_w_o, w_ff2_up=m_w_ff2_up, w_ff2_down=m_w_ff2_down, final_g=m_final_g)
    mom_v = dict(c_ctx=v_c_ctx, w_ada=v_w_ada, b_ada=v_b_ada, norm_g=v_norm_g, w_ff1_up=v_w_ff1_up, w_ff1_down=v_w_ff1_down,
                 w_in=v_w_in, b_gate=v_b_gate, rpb=v_rpb, ln_v_g=v_ln_v_g, ln_v_b=v_ln_v_b, w_s=v_w_s, b_s=v_b_s, w_pa=v_w_pa,
                 w_pb=v_w_pb, w_o=v_w_o, w_ff2_up=v_w_ff2_up, w_ff2_down=v_w_ff2_down, final_g=v_final_g)
    order = list(weights)
    mx, my, mc = _here()
    dev = 4 * mx + 2 * my + mc
    chip = 2 * mx + my
    depth, d, n_ada = w_ada.shape
    dq = d // 4

    c_all = _allgather8(jnp.pad(c, ((0, 7), (0, 0))), "gather_c")[:, 0, :]
    cond = jnp.concatenate([c_all, c_ctx[None, :], jnp.zeros((7, d), F32)], axis=0)
    b_shard = lax.dynamic_slice(b_ada, (0, chip * n_ada), (depth, n_ada))
    proj = [_ada_fwd(cond, w_ada[l], b_shard[l:l + 1], f"ada{l}") for l in range(depth)]
    silu_c = proj[0][1]
    mods_sh = _allgather8(jnp.concatenate([p[0] for p in proj], axis=0), "gather_mods")
    mods_all = jnp.transpose(mods_sh[0::2].reshape(4, depth, 16, n_ada), (1, 2, 0, 3)).reshape(depth, 16, N_MOD, d)
    mods = jnp.stack([lax.dynamic_index_in_dim(mods_all, dev, axis=1, keepdims=False), mods_all[:, 8]], axis=1)

    shards = [weights[k].astype(MXU) for k in BIG]
    full = _gather_chips(shards, [SHARD_AXIS[k] for k in BIG], "gather_w")
    my_chip = jnp.reshape(chip, (1,)).astype(jnp.int32)
    full = [_place_shard(f, sh, my_chip, SHARD_AXIS[k], "place_" + k) for k, f, sh in zip(BIG, full, shards)]
    wts = dict(zip(BIG, full))
    prm = {k: weights[k] for k in SMALL if k != "norm_g"}
    norm_full = _allgather8(jnp.pad(norm_g.reshape(depth * 3, dq), ((0, 8 - depth * 3), (0, 0))), "gather_norm_g")
    prm["norm_g"] = jnp.transpose(norm_full[0::2, :depth * 3].reshape(4, depth, 3, dq), (1, 2, 0, 3)).reshape(depth, 3, d)

    loss, grad_x, dmods, gw, gp = _local_step(x[0], ctx[0], loss_target[0], mods, wts, prm)
    loss = lax.psum(loss, ("x", "y", "c"))

    small_shapes = [(depth, 2, N_MOD * d)] + [weights[k].shape if k != "norm_g" else (depth, 3, d) for k in SMALL]
    packed = _allgather8(_pack([dmods.reshape(depth, 2, N_MOD * d)] + [gp[k] for k in SMALL]), "gather_small")
    rows = packed.shape[1]
    total = _sum_pieces([packed[i] for i in range(8)], "sum_small")[0].reshape(-1)
    sums = dict(zip(("dmods",) + SMALL, _unpack(total, small_shapes)))
    dmods_dev = _unpack(packed.reshape(8, rows * PACK_LANES), small_shapes[:1])[0]

    g_ada, cc_parts = [], []
    for l in range(depth):
        dm = jnp.concatenate([dmods_dev[:, l, 0], sums["dmods"][l, 1][None], jnp.zeros((7, N_MOD * d), F32)], axis=0)
        dm_sh = lax.dynamic_slice(dm, (0, chip * n_ada), (16, n_ada))
        g_ada.append(_mm(silu_c, dm_sh, "tn", F32, f"ada{l}_dw"))
        cc_parts.append(_mm(dm_sh, w_ada[l], "nt", F32, f"ada{l}_dc")[8:9])
    cc_all = _allgather8(jnp.pad(jnp.concatenate(cc_parts, axis=0), ((0, 8 - depth), (0, 0))), "gather_cctx")
    g_cctx = _cctx_grad(cc_all[0::2, :depth].reshape(4 * depth, d), c_ctx.reshape(1, d), "cctx_grad")

    axes = [SHARD_AXIS[k] for k in BIG]
    from_sibling = _swap_layers([gw[k] for k in BIG], "swap_layer_gw")
    my_layer = jnp.reshape(mc, (1,)).astype(jnp.int32)
    pair = [_pair_sum(gw[k], r, my_layer, "pair_" + k) for k, r in zip(BIG, from_sibling)]
    recv = _scatter_chips([p[1] for p in pair], axes, "scatter_gw")
    mine = []
    for k, ax, p, r in zip(BIG, axes, pair, recv):
        size = p[0].shape[ax] // 4
        own = lax.dynamic_slice_in_dim(p[0], chip * size, size, axis=ax)
        mine.append(_sum_pieces([own, r[0], r[1], r[2]], "sum_" + k)[0])
    other = _sibling_swap(mine, "swap_gw")

    res = {k: _adamw_layers(weights[k], a, b, mom_m[k], mom_v[k], my_layer, "adamw_" + k) for k, a, b in zip(BIG, mine, other)}
    pieces = {"w_ada": [jnp.stack(g_ada)]}
    pieces["b_ada"] = [sums["dmods"][:, 0], sums["dmods"][:, 1]]
    pieces["c_ctx"] = [g_cctx[0]]
    for k in SMALL:
        pieces[k] = [sums[k]]
    pieces["norm_g"] = [lax.dynamic_slice_in_dim(sums["norm_g"], chip * dq, dq, axis=2)]
    res.update({k: _adamw(weights[k], pieces[k], mom_m[k], mom_v[k], "adamw_" + k) for k in pieces})
    return (loss, grad_x[None], *[res[k][0] for k in order], *[res[k][1] for k in order],
            *[res[k][2] for k in order], *[res[k][3] for k in order])
```

```python
import numpy as np
import jax
import jax.numpy as jnp
from jax import lax
from jax.experimental import pallas as pl
from jax.experimental.pallas import tpu as pltpu

F32 = jnp.float32
MXU = jnp.bfloat16
EPS = 1e-6
GRID_W, HEADS, HEAD_DIM = 64, 8, 64
NA_WIDTH = SG_WIDTH = 512
WIN_H, WIN_W = 8, 16
SG_CHUNK, SG_GROUPS = 128, 4
N_MOD = 9
ROPE_THETA = 10000.0
Q_ROWS, K_ROWS = 4, 12
TQ, TK = Q_ROWS * GRID_W, K_ROWS * GRID_W
TM = 256
LANES = 128
NEG = -1e30
VMEM_LIMIT = 56 * 2 ** 20
ADAM_LR, ADAM_B1, ADAM_B2, ADAM_EPS, ADAM_WD, ADAM_STEP = 0.001, 0.9, 0.999, 1e-08, 0.01, 10
MESH = pl.DeviceIdType.MESH
BIG = ("w_ff1_up", "w_ff1_down", "w_in", "w_pa", "w_pb", "w_o", "w_ff2_up", "w_ff2_down")
SHARD_AXIS = {"w_ff1_up": 2, "w_ff1_down": 1, "w_in": 2, "w_pa": 2, "w_pb": 2, "w_o": 1, "w_ff2_up": 2, "w_ff2_down": 1}


def _call(body, *, name, grid, in_specs, out_specs, out_shape, scratch=(), aliases=None):
    return pl.pallas_call(
        body, name=name, grid=grid, in_specs=in_specs, out_specs=out_specs, out_shape=out_shape,
        scratch_shapes=list(scratch), input_output_aliases=aliases or {},
        compiler_params=pltpu.CompilerParams(dimension_semantics=("arbitrary",) * len(grid), vmem_limit_bytes=VMEM_LIMIT))


def _w_dims(w):
    return w[0].shape[1:] if isinstance(w, tuple) else w.shape


def _w_arr(w):
    return w[0] if isinstance(w, tuple) else w


def _w_spec(w, block, index):
    if isinstance(w, tuple):
        layer = w[1]
        return pl.BlockSpec((None, *block), lambda *ids: (layer, *index(*ids)))
    return pl.BlockSpec(block, index)


def _pick(n, prefs):
    for p in prefs:
        if n % p == 0:
            return p
    return n


def _row_tile(t):
    return _pick(t, (640, 256))


def _rows(tm, n, col=0):
    return pl.BlockSpec((tm, n), lambda i: (i, col))


def _fixed(shape):
    return pl.BlockSpec(shape, lambda *_: (0,) * len(shape))


def _sds(shape, dtype):
    return jax.ShapeDtypeStruct(shape, dtype)


def _mm(a, b, mode, out_dtype, name, into=None):
    if mode == "tn":
        r, m = a.shape
        n = b.shape[1]
        tm = _pick(m, (1024, 1408, 704, 512, 256, 128))
        tn = _pick(n, (512, 1408, 256, 128))
        tr = _pick(r, (1280, 640, 512, 256, 128))

        def body(a_ref, b_ref, *rest):
            o_ref = rest[-1]

            @pl.when(pl.program_id(2) == 0)
            def _():
                o_ref[...] = jnp.zeros_like(o_ref)

            o_ref[...] += lax.dot_general(a_ref[...].astype(MXU), b_ref[...].astype(MXU), (((0,), (0,)), ((), ())),
                                          preferred_element_type=F32)

        in_specs = [pl.BlockSpec((tr, tm), lambda i, j, k: (k, i)), pl.BlockSpec((tr, tn), lambda i, j, k: (k, j))]
        if into is None:
            return _call(body, name=name, grid=(m // tm, n // tn, r // tr), in_specs=in_specs,
                         out_specs=pl.BlockSpec((tm, tn), lambda i, j, k: (i, j)), out_shape=_sds((m, n), F32))(a, b)
        buf, shape, layer, col0 = into
        out_spec = pl.BlockSpec((None, tm, tn), lambda i, j, k: (layer, i, j + col0 // tn))
        if buf is None:
            return _call(body, name=name, grid=(m // tm, n // tn, r // tr), in_specs=in_specs, out_specs=out_spec,
                         out_shape=_sds(shape, F32))(a, b)
        return _call(body, name=name, grid=(m // tm, n // tn, r // tr), in_specs=in_specs + [pl.BlockSpec(memory_space=pl.ANY)],
                     out_specs=out_spec, out_shape=_sds(shape, F32), aliases={2: 0})(a, b, buf)
    m, k = a.shape
    n = _w_dims(b)[1] if mode == "nn" else _w_dims(b)[0]
    tm = _pick(m, (1280, 640, 512, 256, 128) if k <= 2816 else (640, 512, 256, 128))
    tn = _pick(n, (512, 1408, 256, 128))
    dims = (((1,), (0,)), ((), ())) if mode == "nn" else (((1,), (1,)), ((), ()))

    def body(a_ref, b_ref, o_ref):
        o_ref[...] = lax.dot_general(a_ref[...].astype(MXU), b_ref[...].astype(MXU), dims,
                                     preferred_element_type=F32).astype(o_ref.dtype)

    b_spec = _w_spec(b, (k, tn), lambda i, j: (0, j)) if mode == "nn" else _w_spec(b, (tn, k), lambda i, j: (j, 0))
    return _call(body, name=name, grid=(m // tm, n // tn), in_specs=[pl.BlockSpec((tm, k), lambda i, j: (i, 0)), b_spec],
                 out_specs=pl.BlockSpec((tm, tn), lambda i, j: (i, j)), out_shape=_sds((m, n), out_dtype))(a, _w_arr(b))


def _row_chunks(tm):
    rc = _pick(tm, (256, 128))
    return [slice(r, r + rc) for r in range(0, tm, rc)]


def _ctx_rows(i, tm, s):
    return (i * tm + lax.broadcasted_iota(jnp.int32, (tm, 1), 0)) >= s


def _mod_row(m_ref, k, ctx):
    return jnp.where(ctx, m_ref[1, k:k + 1, :], m_ref[0, k:k + 1, :])


def _stream_sums(i, tm, s, refs_and_vals):
    @pl.when((i + 1) * tm <= s)
    def _():
        for ref, val in refs_and_vals:
            ref[0] += jnp.sum(val, axis=0, keepdims=True)

    @pl.when((i + 1) * tm > s)
    def _():
        ctx = _ctx_rows(i, tm, s)
        for ref, val in refs_and_vals:
            ref[0] += jnp.sum(jnp.where(ctx, 0.0, val), axis=0, keepdims=True)
            ref[1] += jnp.sum(jnp.where(ctx, val, 0.0), axis=0, keepdims=True)


def _norm_mm(xs, g, mods, k_shift, k_scale, w, s, name, glu):
    t, d = xs.shape
    n = _w_dims(w)[1] // 2 if glu else _w_dims(w)[1]
    tm = _pick(t, (1280, 640, 256))
    tn = _pick(n, (256, 128)) if glu else _pick(n, (512, 256, 128))
    nj = n // tn

    def body(x_ref, g_ref, m_ref, *refs):
        i, j = pl.program_id(0), pl.program_id(1)
        w_refs, h_ref, o_refs = refs[:2 if glu else 1], refs[2 if glu else 1], refs[3 if glu else 2:]

        @pl.when(j == 0)
        def _():
            x = x_ref[...]
            rstd = lax.rsqrt(jnp.mean(x * x, axis=-1, keepdims=True) + EPS)
            ctx = _ctx_rows(i, tm, s)
            h = x * rstd * g_ref[...] * (1.0 + _mod_row(m_ref, k_scale, ctx)) + _mod_row(m_ref, k_shift, ctx)
            h_ref[...] = h.astype(h_ref.dtype)

        for rows in _row_chunks(tm):
            h = h_ref[rows, :]
            a = _nn(h, w_refs[0][...])
            o_refs[0][rows, :] = a.astype(o_refs[0].dtype)
            if glu:
                b = _nn(h, w_refs[1][...])
                o_refs[1][rows, :] = b.astype(o_refs[1].dtype)
                o_refs[2][rows, :] = (a * jax.nn.sigmoid(a) * b).astype(o_refs[2].dtype)

    tile = pl.BlockSpec((tm, tn), lambda i, j: (i, j))
    row = pl.BlockSpec((tm, d), lambda i, j: (i, 0))
    w_specs = [_w_spec(w, (d, tn), lambda i, j: (0, j))] + ([_w_spec(w, (d, tn), lambda i, j: (0, j + nj))] if glu else [])
    n_out = 3 if glu else 1
    return _call(body, name=name, grid=(t // tm, nj),
                 in_specs=[row, _fixed((1, d)), _fixed((2, N_MOD, d))] + w_specs,
                 out_specs=[row] + [tile] * n_out,
                 out_shape=[_sds((t, d), MXU)] + [_sds((t, n), MXU)] * n_out)(xs, g, mods, *([_w_arr(w)] * (2 if glu else 1)))


def _mm_res(a, w, xs, mods, k_gate, coef, s, name):
    t, k = a.shape
    d = _w_dims(w)[1]
    tm = _pick(t, (1280, 640, 256))
    tn = _pick(d, (512, 256, 128))

    def body(a_ref, w_ref, x_ref, m_ref, y_ref, o_ref):
        y = _nn(a_ref[...], w_ref[...])
        y_ref[...] = y.astype(y_ref.dtype)
        gate = _mod_row(m_ref, k_gate, _ctx_rows(pl.program_id(0), tm, s))
        o_ref[...] = x_ref[...] + (coef * gate) * y

    tile = pl.BlockSpec((tm, tn), lambda i, j: (i, j))
    return _call(body, name=name, grid=(t // tm, d // tn),
                 in_specs=[pl.BlockSpec((tm, k), lambda i, j: (i, 0)), _w_spec(w, (k, tn), lambda i, j: (0, j)), tile,
                           pl.BlockSpec((2, N_MOD, tn), lambda i, j: (0, 0, j))],
                 out_specs=[tile, tile], out_shape=[_sds((t, d), MXU), _sds((t, d), F32)])(a, _w_arr(w), xs, mods)


def _resb_mm(dxn, y, mods, k_gate, coef, w, s, name, ups=None):
    t, d = dxn.shape
    n = _w_dims(w)[0]
    tm = _pick(t, (1280, 640, 256))
    tn = _pick(n, (256, 128)) if ups else _pick(n, (512, 256, 128))

    def body(dx_ref, y_ref, m_ref, w_ref, *refs):
        i, j = pl.program_id(0), pl.program_id(1)
        u_refs, (dy_ref, dgt_ref), o_refs = (refs[:2], refs[2:4], refs[4:]) if ups else ((), refs[:2], refs[2:])

        @pl.when((i == 0) & (j == 0))
        def _():
            dgt_ref[...] = jnp.zeros_like(dgt_ref)

        @pl.when(j == 0)
        def _():
            dx = dx_ref[...]
            gate = _mod_row(m_ref, k_gate, _ctx_rows(i, tm, s))
            dy_ref[...] = ((coef * gate) * dx).astype(dy_ref.dtype)
            _stream_sums(i, tm, s, [(dgt_ref, coef * y_ref[...].astype(F32) * dx)])

        dact = _nt(dy_ref[...], w_ref[...])
        if ups:
            a, b = u_refs[0][...].astype(F32), u_refs[1][...].astype(F32)
            sg = jax.nn.sigmoid(a)
            o_refs[0][...] = (dact * b * sg * (1.0 + a * (1.0 - sg))).astype(o_refs[0].dtype)
            o_refs[1][...] = (dact * a * sg).astype(o_refs[1].dtype)
        else:
            o_refs[0][...] = dact.astype(o_refs[0].dtype)

    tile = pl.BlockSpec((tm, tn), lambda i, j: (i, j))
    row = pl.BlockSpec((tm, d), lambda i, j: (i, 0))
    n_out = 2 if ups else 1
    return _call(body, name=name, grid=(t // tm, n // tn),
                 in_specs=[row, row, _fixed((2, N_MOD, d)), _w_spec(w, (tn, d), lambda i, j: (j, 0))] + ([tile, tile] if ups else []),
                 out_specs=[row, _fixed((2, 1, d))] + [tile] * n_out,
                 out_shape=[_sds((t, d), MXU), _sds((2, 1, d), F32)] + [_sds((t, n), MXU)] * n_out)(
                     dxn, y, mods, _w_arr(w), *(ups or ()))


def _mm_normb(a_list, w, xs, dres, g, mods, k_scale, s, name):
    t, d = xs.shape
    tm = _pick(t, (640, 256))
    n_a = len(a_list)
    widths = [a.shape[1] for a in a_list]
    tk = next(c for c in (1408, 1536, 1024, 512, 256, 128) if all(wd % c == 0 for wd in widths))
    counts = [wd // tk for wd in widths]
    starts = [sum(counts[:q]) for q in range(n_a)]
    nk = sum(counts)

    def body(*refs):
        a_refs, (w_ref, x_ref, dr_ref, g_ref, m_ref, dx_ref, dsh_ref, dsc_ref, dg_ref, acc) = refs[:n_a], refs[n_a:]
        i, k = pl.program_id(0), pl.program_id(1)

        @pl.when((i == 0) & (k == 0))
        def _():
            dsh_ref[...] = jnp.zeros_like(dsh_ref)
            dsc_ref[...] = jnp.zeros_like(dsc_ref)
            dg_ref[...] = jnp.zeros_like(dg_ref)

        @pl.when(k == 0)
        def _():
            acc[...] = jnp.zeros_like(acc)

        for q in range(n_a):
            @pl.when((k >= starts[q]) & (k < starts[q] + counts[q]))
            def _():
                acc[...] += _nt(a_refs[q][...], w_ref[...])

        @pl.when(k == nk - 1)
        def _():
            x = x_ref[...]
            dh = acc[...]
            rstd = lax.rsqrt(jnp.mean(x * x, axis=-1, keepdims=True) + EPS)
            xhat = x * rstd
            gg = g_ref[...]
            _stream_sums(i, tm, s, [(dsh_ref, dh), (dsc_ref, dh * (xhat * gg))])
            dy = dh * (1.0 + _mod_row(m_ref, k_scale, _ctx_rows(i, tm, s)))
            dg_ref[...] += jnp.sum(dy * xhat, axis=0, keepdims=True)
            dxh = dy * gg
            dx_ref[...] = dr_ref[...] + rstd * (dxh - xhat * jnp.mean(dxh * xhat, axis=-1, keepdims=True))

    row = pl.BlockSpec((tm, d), lambda i, k: (i, 0))
    a_specs = [pl.BlockSpec((tm, tk), lambda i, k, q=q: (i, jnp.clip(k - starts[q], 0, counts[q] - 1))) for q in range(n_a)]
    return _call(body, name=name, grid=(t // tm, nk),
                 in_specs=a_specs + [_w_spec(w, (d, tk), lambda i, k: (0, k)), row, row, _fixed((1, d)), _fixed((2, N_MOD, d))],
                 out_specs=[row, _fixed((2, 1, d)), _fixed((2, 1, d)), _fixed((1, d))],
                 out_shape=[_sds((t, d), F32), _sds((2, 1, d), F32), _sds((2, 1, d), F32), _sds((1, d), F32)],
                 scratch=[pltpu.VMEM((tm, d), F32)])(*a_list, _w_arr(w), xs, dres, g, mods)


def _rope_tables(s, ctx_len):
    n_freq = HEAD_DIM // 4
    tok = jnp.arange(s)
    freqs = ROPE_THETA ** (-jnp.arange(n_freq, dtype=F32) / n_freq)
    ang = jnp.concatenate([(tok // GRID_W).astype(F32)[:, None] * freqs, (tok % GRID_W).astype(F32)[:, None] * freqs], axis=-1)
    cos = jnp.repeat(jnp.cos(ang), 2, axis=-1)
    sin = jnp.repeat(jnp.sin(ang), 2, axis=-1) * jnp.tile(jnp.array([-1.0, 1.0], F32), HEAD_DIM // 2)
    cos = jnp.concatenate([jnp.tile(cos, (1, LANES // HEAD_DIM)), jnp.ones((ctx_len, LANES), F32)], axis=0)
    sin = jnp.concatenate([jnp.tile(sin, (1, LANES // HEAD_DIM)), jnp.zeros((ctx_len, LANES), F32)], axis=0)
    return cos, sin


def _swap_pairs(x):
    n = x.shape[-1]
    lane = lax.broadcasted_iota(jnp.int32, x.shape, 1)
    return jnp.where(lane % 2 == 0, pltpu.roll(x, n - 1, 1), pltpu.roll(x, 1, 1))


def _rope(p, cos, sin, name):
    t = p.shape[0]
    w = NA_WIDTH
    te = _row_tile(t)

    def body(q_ref, k_ref, v_ref, c_ref, s_ref, qo_ref, ko_ref, vo_ref):
        c, s = c_ref[...], s_ref[...]
        for hp in range(w // LANES):
            cols = slice(hp * LANES, (hp + 1) * LANES)
            q, k = q_ref[:, cols].astype(F32), k_ref[:, cols].astype(F32)
            qo_ref[:, cols] = (q * c + _swap_pairs(q) * s).astype(qo_ref.dtype)
            ko_ref[:, cols] = (k * c + _swap_pairs(k) * s).astype(ko_ref.dtype)
        vo_ref[...] = v_ref[...].astype(vo_ref.dtype)

    return _call(body, name=name, grid=(t // te,),
                 in_specs=[_rows(te, w, 0), _rows(te, w, 1), _rows(te, w, 2), _rows(te, LANES), _rows(te, LANES)],
                 out_specs=[_rows(te, w)] * 3, out_shape=[_sds((t, w), MXU)] * 3)(p, p, p, cos, sin)


def _rope_bwd(dq, dk, dv, cos, sin, name):
    t = dq.shape[0]
    te = _row_tile(t)
    n_pairs = NA_WIDTH // LANES

    def body(dq_ref, dk_ref, dv_ref, c_ref, s_ref, qo_ref, ko_ref, vo_ref):
        c, s = c_ref[...], s_ref[...]
        for hp in range(n_pairs):
            cols = slice(hp * LANES, (hp + 1) * LANES)
            a, b = dq_ref[:, cols], dk_ref[hp]
            qo_ref[:, cols] = (a * c + _swap_pairs(a * s)).astype(qo_ref.dtype)
            ko_ref[:, cols] = (b * c + _swap_pairs(b * s)).astype(ko_ref.dtype)
            vo_ref[:, cols] = dv_ref[hp].astype(vo_ref.dtype)

    pairs = pl.BlockSpec((n_pairs, te, LANES), lambda i: (0, i, 0))
    return _call(body, name=name, grid=(t // te,),
                 in_specs=[_rows(te, NA_WIDTH), pairs, pairs, _rows(te, LANES), _rows(te, LANES)],
                 out_specs=[_rows(te, NA_WIDTH)] * 3, out_shape=[_sds((t, NA_WIDTH), MXU)] * 3)(dq, dk, dv, cos, sin)


def _na_geometry(r_grid):
    rows = []
    for r0, ks in ((0, 0), (Q_ROWS, 0), (r_grid - Q_ROWS, r_grid - K_ROWS)):
        dr = np.zeros((Q_ROWS, K_ROWS), np.int32)
        vr = np.zeros((Q_ROWS, K_ROWS), bool)
        for a in range(Q_ROWS):
            r = r0 + a
            rs = min(max(r - WIN_H // 2, 0), r_grid - WIN_H)
            for i in range(K_ROWS):
                kr = ks + i
                vr[a, i] = rs <= kr <= rs + WIN_H - 1
                dr[a, i] = kr - r + WIN_H - 1
        rows.append((dr, vr))
    c = np.arange(GRID_W)
    cs = np.clip(c - WIN_W // 2, 0, GRID_W - WIN_W)
    kc = np.arange(GRID_W)
    vc = (kc[None, :] >= cs[:, None]) & (kc[None, :] <= cs[:, None] + WIN_W - 1)
    dc = kc[None, :] - c[:, None] + WIN_W - 1
    return rows, dc, vc


def _bias_table(rpb, r_grid):
    rows, _, vc = _na_geometry(r_grid)
    n_dc, off = 2 * WIN_W - 1, GRID_W - WIN_W
    u = jnp.pad(rpb, ((0, 0), (0, 0), (off, 2 * GRID_W - 1 - off - n_dc)))
    toep = jnp.stack([u[:, :, GRID_W - 1 - c:2 * GRID_W - 1 - c] for c in range(GRID_W)], axis=1)
    toep = jnp.pad(toep, ((0, 0), (0, 0), (Q_ROWS, Q_ROWS), (0, 0)))
    tabs = []
    for dr, vr in rows:
        per_row = []
        for a in range(Q_ROWS):
            lo = int(dr[a, 0]) + Q_ROWS
            blocks = [jnp.where(vc[None], toep[:, :, lo + i, :], NEG) if vr[a, i] else jnp.full((HEADS, GRID_W, GRID_W), NEG, F32)
                      for i in range(K_ROWS)]
            per_row.append(jnp.concatenate(blocks, axis=-1))
        tabs.append(jnp.stack(per_row, axis=1).reshape(HEADS, TQ, TK))
    tabs.append(jnp.full((HEADS, TQ, TK), NEG, F32))
    return jnp.stack(tabs)


def _variant(g, ngx):
    return jnp.where(g == 0, 0, jnp.where(g >= ngx, 3, jnp.where(g == ngx - 1, 2, 1)))


def _key_start(g, r_grid):
    return pl.multiple_of(jnp.clip(g * Q_ROWS - WIN_H // 2, 0, r_grid - K_ROWS) * GRID_W, TQ)


def _nt(a, b):
    return lax.dot_general(a, b, (((1,), (1,)), ((), ())), preferred_element_type=F32)


def _tn(a, b):
    return lax.dot_general(a, b, (((0,), (0,)), ((), ())), preferred_element_type=F32)


def _nn(a, b):
    return jnp.dot(a, b, preferred_element_type=F32)


def _head_mask(h):
    lane = lax.broadcasted_iota(jnp.int32, (1, LANES), 1)
    return ((lane >= HEAD_DIM * h) & (lane < HEAD_DIM * (h + 1))).astype(F32)


def _softmax_parts(qm, knb, kcx, bias):
    s_nb = _nt(qm, knb) + bias
    s_cx = _nt(qm, kcx)
    m = jnp.maximum(jnp.max(s_nb, axis=-1, keepdims=True), jnp.max(s_cx, axis=-1, keepdims=True))
    e_nb = jnp.exp(s_nb - m)
    e_cx = jnp.exp(s_cx - m)
    inv = 1.0 / (jnp.sum(e_nb, axis=-1, keepdims=True) + jnp.sum(e_cx, axis=-1, keepdims=True))
    return e_nb, e_cx, inv


def _na_specs(t, ngx):
    q_spec = pl.BlockSpec((TQ, LANES), lambda hp, g: (g, hp))
    kv_spec = pl.BlockSpec((t, LANES), lambda hp, g: (0, hp))
    b_spec = pl.BlockSpec((1, 2, TQ, TK), lambda hp, g: (_variant(g, ngx), hp, 0, 0))
    return q_spec, kv_spec, b_spec


def _na_fwd(q, k, v, bias, s, name):
    t = q.shape[0]
    ctx_len = t - s
    r_grid = s // GRID_W
    q_spec, kv_spec, b_spec = _na_specs(t, s // TQ)

    def body(q_ref, k_ref, v_ref, b_ref, o_ref):
        start = _key_start(pl.program_id(1), r_grid)
        qf = q_ref[...].astype(F32) * (HEAD_DIM ** -0.5)
        knb, vnb = k_ref[pl.ds(start, TK), :], v_ref[pl.ds(start, TK), :]
        kcx, vcx = k_ref[pl.ds(s, ctx_len), :], v_ref[pl.ds(s, ctx_len), :]
        acc = jnp.zeros((TQ, LANES), F32)
        for h in range(2):
            mask = _head_mask(h)
            e_nb, e_cx, inv = _softmax_parts((qf * mask).astype(MXU), knb, kcx, b_ref[0, h])
            acc += (_nn(e_nb.astype(MXU), vnb) + _nn(e_cx.astype(MXU), vcx)) * (inv * mask)
        o_ref[...] = acc.astype(o_ref.dtype)

    return _call(body, name=name, grid=(NA_WIDTH // LANES, t // TQ), in_specs=[q_spec, kv_spec, kv_spec, b_spec],
                 out_specs=q_spec, out_shape=_sds((t, NA_WIDTH), MXU))(q, k, v, bias)


def _na_bwd(q, k, v, do, bias, s, name):
    t = q.shape[0]
    ctx_len = t - s
    r_grid = s // GRID_W
    ng, ngx = t // TQ, s // TQ
    q_spec, kv_spec, b_spec = _na_specs(t, ngx)

    def body(q_ref, k_ref, v_ref, do_ref, b_ref, dq_ref, dk_hbm, dv_hbm, db_ref, dk_acc, dv_acc):
        hp, g = pl.program_id(0), pl.program_id(1)
        start = _key_start(g, r_grid)

        @pl.when(g == 0)
        def _():
            dk_acc[...] = jnp.zeros_like(dk_acc)
            dv_acc[...] = jnp.zeros_like(dv_acc)

        @pl.when((g == 0) | (g == 1) | (g == ngx - 1) | (g == ngx))
        def _():
            db_ref[...] = jnp.zeros_like(db_ref)

        qf = q_ref[...].astype(F32) * (HEAD_DIM ** -0.5)
        do = do_ref[...].astype(F32)
        knb, vnb = k_ref[pl.ds(start, TK), :], v_ref[pl.ds(start, TK), :]
        kcx, vcx = k_ref[pl.ds(s, ctx_len), :], v_ref[pl.ds(s, ctx_len), :]
        dq = jnp.zeros((TQ, LANES), F32)
        dk_nb = jnp.zeros((TK, LANES), F32)
        dv_nb = jnp.zeros((TK, LANES), F32)
        dk_cx = jnp.zeros((ctx_len, LANES), F32)
        dv_cx = jnp.zeros((ctx_len, LANES), F32)
        for h in range(2):
            mask = _head_mask(h)
            qm = (qf * mask).astype(MXU)
            dom = (do * mask).astype(MXU)
            e_nb, e_cx, inv = _softmax_parts(qm, knb, kcx, b_ref[0, h])
            dp_nb = _nt(dom, vnb)
            dp_cx = _nt(dom, vcx)
            delta = inv * (jnp.sum(e_nb * dp_nb, axis=-1, keepdims=True) + jnp.sum(e_cx * dp_cx, axis=-1, keepdims=True))
            ds_nb = e_nb * (inv * (dp_nb - delta))
            ds_cx = e_cx * (inv * (dp_cx - delta))
            db_ref[0, h] += ds_nb
            ds_nb, ds_cx = ds_nb.astype(MXU), ds_cx.astype(MXU)
            dq += (_nn(ds_nb, knb) + _nn(ds_cx, kcx)) * (mask * (HEAD_DIM ** -0.5))
            dk_nb += _tn(ds_nb, qm)
            dk_cx += _tn(ds_cx, qm)
            dom = (do * (inv * mask)).astype(MXU)
            dv_nb += _tn(e_nb.astype(MXU), dom)
            dv_cx += _tn(e_cx.astype(MXU), dom)
        dq_ref[...] = dq
        dk_acc[pl.ds(start, TK), :] += dk_nb
        dv_acc[pl.ds(start, TK), :] += dv_nb
        dk_acc[pl.ds(s, ctx_len), :] += dk_cx
        dv_acc[pl.ds(s, ctx_len), :] += dv_cx

        @pl.when(g == ng - 1)
        def _():
            pltpu.sync_copy(dk_acc, dk_hbm.at[hp])
            pltpu.sync_copy(dv_acc, dv_hbm.at[hp])

    n_pairs = NA_WIDTH // LANES
    hbm = pl.BlockSpec(memory_space=pl.ANY)
    return _call(body, name=name, grid=(n_pairs, ng), in_specs=[q_spec, kv_spec, kv_spec, q_spec, b_spec],
                 out_specs=[q_spec, hbm, hbm, b_spec],
                 out_shape=[_sds((t, NA_WIDTH), F32), _sds((n_pairs, t, LANES), F32), _sds((n_pairs, t, LANES), F32),
                            _sds((4, HEADS, TQ, TK), F32)],
                 scratch=[pltpu.VMEM((t, LANES), F32), pltpu.VMEM((t, LANES), F32)])(q, k, v, do, bias)


def _rpb_grad(dbias, r_grid, name):
    rows, _, _ = _na_geometry(r_grid)
    n_dr, half, skew, lanes = 2 * WIN_H - 1, WIN_W - 1, TK + 2, 896
    z = dbias[:3].reshape(3, HEADS, Q_ROWS, GRID_W, TK)
    z = jnp.pad(z, ((0, 0),) * 4 + ((0, 1),)).reshape(3, HEADS, Q_ROWS, GRID_W * (TK + 1))
    z = jnp.pad(z, ((0, 0),) * 3 + ((0, GRID_W),)).reshape(3, HEADS, Q_ROWS, GRID_W, skew)
    z = jnp.pad(z, ((0, 0),) * 4 + ((0, lanes - skew),))

    def body(z_ref, o_ref):
        sums = [jnp.sum(z_ref[v, 0, a], axis=0, keepdims=True) for v in range(3) for a in range(Q_ROWS)]
        zs = jnp.concatenate(sums + [jnp.zeros((16 - 3 * Q_ROWS, lanes), F32)], axis=0)
        acc = [jnp.zeros((1, lanes), F32) for _ in range(n_dr)]
        for i in range(K_ROWS):
            if i == 0:
                at0 = pltpu.roll(zs, lanes - (skew - half), 1) + pltpu.roll(zs, half, 1)
            else:
                at0 = pltpu.roll(zs, lanes - (i * GRID_W - half), 1)
            for v, (dr, vr) in enumerate(rows):
                for a in range(Q_ROWS):
                    if vr[a, i]:
                        acc[dr[a, i]] = acc[dr[a, i]] + at0[v * Q_ROWS + a:v * Q_ROWS + a + 1, :]
        o_ref[0] = jnp.concatenate(acc + [jnp.zeros((1, lanes), F32)], axis=0)

    o = _call(body, name=name, grid=(HEADS,),
              in_specs=[pl.BlockSpec((3, 1, Q_ROWS, GRID_W, lanes), lambda h: (0, h, 0, 0, 0))],
              out_specs=pl.BlockSpec((1, 16, lanes), lambda h: (h, 0, 0)), out_shape=_sds((HEADS, 16, lanes), F32))(z)
    return o[:, :n_dr, :2 * WIN_W - 1]


_GELU_K, _GELU_C = 0.7978845608028654, 0.044715


def _gelu(x):
    return 0.5 * x * (1.0 + jnp.tanh(_GELU_K * (x + _GELU_C * x * x * x)))


def _gelu_grad(x):
    th = jnp.tanh(_GELU_K * (x + _GELU_C * x * x * x))
    return 0.5 * (1.0 + th) + 0.5 * x * (1.0 - th * th) * (_GELU_K * (1.0 + 3.0 * _GELU_C * x * x))


def _ln_stats(v):
    mu = jnp.mean(v, axis=-1, keepdims=True)
    vc = v - mu
    rstd = lax.rsqrt(jnp.mean(vc * vc, axis=-1, keepdims=True) + EPS)
    return vc * rstd, rstd


def _gmlp(p, ln_g, ln_b, w_s, b_s, name):
    t = p.shape[0]
    te = _row_tile(t)
    w = SG_WIDTH
    cw = w // SG_GROUPS

    def body(u_ref, v_ref, g_ref, b_ref, ws_ref, bs_ref, o_ref):
        xhat, _ = _ln_stats(_gelu(v_ref[...].astype(F32)))
        vn = (xhat * g_ref[...] + b_ref[...]).astype(MXU)
        ug = _gelu(u_ref[...].astype(F32))
        for ci in range(te // SG_CHUNK):
            rs = slice(ci * SG_CHUNK, (ci + 1) * SG_CHUNK)
            for gi in range(SG_GROUPS):
                cs = slice(gi * cw, (gi + 1) * cw)
                sg = _nn(ws_ref[gi].astype(MXU), vn[rs, cs]) + bs_ref[gi]
                o_ref[rs, cs] = (ug[rs, cs] * sg).astype(o_ref.dtype)

    return _call(body, name=name, grid=(t // te,),
                 in_specs=[_rows(te, w, 3), _rows(te, w, 4), _fixed((1, w)), _fixed((1, w)),
                           _fixed((SG_GROUPS, SG_CHUNK, SG_CHUNK)), _fixed((SG_GROUPS, SG_CHUNK, 1))],
                 out_specs=_rows(te, w), out_shape=_sds((t, w), MXU))(p, p, ln_g, ln_b, w_s, b_s)


def _gmlp_bwd(p, dob, ln_g, ln_b, w_s, b_s, name):
    t = p.shape[0]
    te = _row_tile(t)
    w = SG_WIDTH
    cw = w // SG_GROUPS

    def body(u_ref, v_ref, do_ref, g_ref, b_ref, ws_ref, bs_ref, du_ref, dv_ref, dws_ref, dbs_ref, dg_ref, db_ref, dvn_ref):
        @pl.when(pl.program_id(0) == 0)
        def _():
            dws_ref[...] = jnp.zeros_like(dws_ref)
            dbs_ref[...] = jnp.zeros_like(dbs_ref)
            dg_ref[...] = jnp.zeros_like(dg_ref)
            db_ref[...] = jnp.zeros_like(db_ref)

        u, v = u_ref[...].astype(F32), v_ref[...].astype(F32)
        xhat, rstd = _ln_stats(_gelu(v))
        vn = (xhat * g_ref[...] + b_ref[...]).astype(MXU)
        ug = _gelu(u)
        dob = do_ref[...].astype(F32)
        for ci in range(te // SG_CHUNK):
            rs = slice(ci * SG_CHUNK, (ci + 1) * SG_CHUNK)
            for gi in range(SG_GROUPS):
                cs = slice(gi * cw, (gi + 1) * cw)
                wsg = ws_ref[gi].astype(MXU)
                sg = _nn(wsg, vn[rs, cs]) + bs_ref[gi]
                du_ref[rs, cs] = (dob[rs, cs] * sg * _gelu_grad(u[rs, cs])).astype(du_ref.dtype)
                ds = dob[rs, cs] * ug[rs, cs]
                dbs_ref[gi] += jnp.sum(ds, axis=-1, keepdims=True)
                ds = ds.astype(MXU)
                dws_ref[gi] += _nt(ds, vn[rs, cs])
                dvn_ref[rs, cs] = _tn(wsg, ds)
        dvn = dvn_ref[...]
        dg_ref[...] += jnp.sum(dvn * xhat, axis=0, keepdims=True)
        db_ref[...] += jnp.sum(dvn, axis=0, keepdims=True)
        dxh = dvn * g_ref[...]
        dvg = rstd * (dxh - jnp.mean(dxh, axis=-1, keepdims=True) - xhat * jnp.mean(dxh * xhat, axis=-1, keepdims=True))
        dv_ref[...] = (dvg * _gelu_grad(v)).astype(dv_ref.dtype)

    return _call(body, name=name, grid=(t // te,),
                 in_specs=[_rows(te, w, 3), _rows(te, w, 4), _rows(te, w), _fixed((1, w)), _fixed((1, w)),
                           _fixed((SG_GROUPS, SG_CHUNK, SG_CHUNK)), _fixed((SG_GROUPS, SG_CHUNK, 1))],
                 out_specs=[_rows(te, w), _rows(te, w), _fixed((SG_GROUPS, SG_CHUNK, SG_CHUNK)),
                            _fixed((SG_GROUPS, SG_CHUNK, 1)), _fixed((1, w)), _fixed((1, w))],
                 out_shape=[_sds((t, w), MXU), _sds((t, w), MXU), _sds((SG_GROUPS, SG_CHUNK, SG_CHUNK), F32),
                            _sds((SG_GROUPS, SG_CHUNK, 1), F32), _sds((1, w), F32), _sds((1, w), F32)],
                 scratch=[pltpu.VMEM((te, w), F32)])(p, p, dob, ln_g, ln_b, w_s, b_s)


def _merge(pa, pb, p, b_gate, name):
    t, d = pa.shape
    te = _row_tile(t)
    hw = NA_WIDTH
    nh = d // hw
    c0 = (NA_WIDTH * 3 + SG_WIDTH * 2) // hw

    def body(pa_ref, pb_ref, la_ref, lb_ref, ba_ref, bb_ref, o_ref):
        ga = jax.nn.sigmoid(la_ref[...].astype(F32) + ba_ref[...])
        gb = jax.nn.sigmoid(lb_ref[...].astype(F32) + bb_ref[...])
        o_ref[...] = (ga * pa_ref[...].astype(F32) + gb * pb_ref[...].astype(F32)).astype(o_ref.dtype)

    tile = pl.BlockSpec((te, hw), lambda i, j: (i, j))
    return _call(body, name=name, grid=(t // te, nh),
                 in_specs=[tile, tile, pl.BlockSpec((te, hw), lambda i, j: (i, c0 + j)),
                           pl.BlockSpec((te, hw), lambda i, j: (i, c0 + nh + j)),
                           pl.BlockSpec((1, hw), lambda i, j: (0, j)), pl.BlockSpec((1, hw), lambda i, j: (0, nh + j))],
                 out_specs=tile, out_shape=_sds((t, d), MXU))(pa, pb, p, p, b_gate, b_gate)


def _merge_bwd(dmg, pa, pb, p, b_gate, name):
    t, d = pa.shape
    te = _row_tile(t)
    hw = NA_WIDTH
    nh = d // hw
    c0 = (NA_WIDTH * 3 + SG_WIDTH * 2) // hw

    def body(dm_ref, pa_ref, pb_ref, la_ref, lb_ref, ba_ref, bb_ref, dpa_ref, dpb_ref, dla_ref, dlb_ref, dba_ref, dbb_ref):
        @pl.when(pl.program_id(1) == 0)
        def _():
            dba_ref[...] = jnp.zeros_like(dba_ref)
            dbb_ref[...] = jnp.zeros_like(dbb_ref)

        dm = dm_ref[...].astype(F32)
        ga = jax.nn.sigmoid(la_ref[...].astype(F32) + ba_ref[...])
        gb = jax.nn.sigmoid(lb_ref[...].astype(F32) + bb_ref[...])
        dpa_ref[...] = (dm * ga).astype(dpa_ref.dtype)
        dpb_ref[...] = (dm * gb).astype(dpb_ref.dtype)
        dla = dm * pa_ref[...].astype(F32) * ga * (1.0 - ga)
        dlb = dm * pb_ref[...].astype(F32) * gb * (1.0 - gb)
        dla_ref[...] = dla.astype(dla_ref.dtype)
        dlb_ref[...] = dlb.astype(dlb_ref.dtype)
        dba_ref[...] += jnp.sum(dla, axis=0, keepdims=True)
        dbb_ref[...] += jnp.sum(dlb, axis=0, keepdims=True)

    tile = pl.BlockSpec((te, hw), lambda j, i: (i, j))
    bias_a = pl.BlockSpec((1, hw), lambda j, i: (0, j))
    bias_b = pl.BlockSpec((1, hw), lambda j, i: (0, nh + j))
    return _call(body, name=name, grid=(nh, t // te),
                 in_specs=[tile, tile, tile, pl.BlockSpec((te, hw), lambda j, i: (i, c0 + j)),
                           pl.BlockSpec((te, hw), lambda j, i: (i, c0 + nh + j)), bias_a, bias_b],
                 out_specs=[tile, tile, tile, tile, bias_a, bias_a],
                 out_shape=[_sds((t, d), MXU)] * 4 + [_sds((1, d), F32)] * 2)(dmg, pa, pb, p, p, b_gate, b_gate)


def _final(xs, tgt, g, name):
    t, d = xs.shape
    nx = tgt.shape[0] // TM

    def body(x_ref, t_ref, g_ref, l_ref, dx_ref, dg_ref):
        i = pl.program_id(0)

        @pl.when(i == 0)
        def _():
            l_ref[...] = jnp.zeros_like(l_ref)
            dg_ref[...] = jnp.zeros_like(dg_ref)

        @pl.when(i < nx)
        def _():
            x = x_ref[...]
            rstd = lax.rsqrt(jnp.mean(x * x, axis=-1, keepdims=True) + EPS)
            xhat = x * rstd
            err = xhat * g_ref[...] - t_ref[...]
            l_ref[...] += 0.5 * jnp.sum(jnp.mean(err * err, axis=-1, keepdims=True))
            dy = err * (1.0 / d)
            dg_ref[...] += jnp.sum(dy * xhat, axis=0, keepdims=True)
            dxh = dy * g_ref[...]
            dx_ref[...] = rstd * (dxh - xhat * jnp.mean(dxh * xhat, axis=-1, keepdims=True))

        @pl.when(i >= nx)
        def _():
            dx_ref[...] = jnp.zeros_like(dx_ref)

    return _call(body, name=name, grid=(t // TM,),
                 in_specs=[_rows(TM, d), pl.BlockSpec((TM, d), lambda i: (jnp.minimum(i, nx - 1), 0)), _fixed((1, d))],
                 out_specs=[_fixed((1, LANES)), _rows(TM, d), _fixed((1, d))],
                 out_shape=[_sds((1, LANES), F32), _sds((t, d), F32), _sds((1, d), F32)])(xs, tgt, g)


def _view2d(a):
    return a.reshape(1, -1) if a.ndim == 1 else a.reshape(-1, a.shape[-1])


def _tile_rows(r, c):
    for cand in (1024, 512, 256, 128, 64, 32, 16):
        if r % cand == 0 and cand * c * 4 <= 2 ** 20:
            return cand
    return r


def _pair_sum(g, recv, layer, name):
    _, a, b = g.shape
    tr = _tile_rows(a, b)

    def body(l_ref, g_ref, r_ref, o32_ref, o16_ref):
        acc = g_ref[...] + r_ref[...]
        o32_ref[...] = acc
        o16_ref[...] = acc.astype(o16_ref.dtype)

    first = pl.BlockSpec((None, tr, b), lambda i, l: (0, i, 0))
    return pl.pallas_call(
        body, name=name, out_shape=[_sds((1, a, b), F32), _sds((1, a, b), MXU)],
        grid_spec=pltpu.PrefetchScalarGridSpec(
            num_scalar_prefetch=1, grid=(a // tr,),
            in_specs=[pl.BlockSpec((None, tr, b), lambda i, l: (l[0], i, 0)), first], out_specs=[first, first]),
        compiler_params=pltpu.CompilerParams(dimension_semantics=("arbitrary",), vmem_limit_bytes=VMEM_LIMIT))(layer, g, recv)


def _ew(fn, arrays, out_dtypes, name):
    shape = arrays[0].shape
    views = [_view2d(a) for a in arrays]
    r, c = views[0].shape
    tr = _tile_rows(r, c)

    def body(*refs):
        outs = fn(*[ref[...] for ref in refs[:len(views)]])
        for ref, o in zip(refs[len(views):], outs):
            ref[...] = o.astype(ref.dtype)

    res = _call(body, name=name, grid=(r // tr,), in_specs=[_rows(tr, c)] * len(views), out_specs=[_rows(tr, c)] * len(out_dtypes),
                out_shape=[_sds((r, c), dt) for dt in out_dtypes])(*views)
    return [o.reshape(shape) for o in res]


def _sum_pieces(pieces, name, out_dtypes=(F32,)):
    def fn(*vals):
        acc = vals[0].astype(F32)
        for v in vals[1:]:
            acc = acc + v.astype(F32)
        return (acc,) * len(out_dtypes)

    return _ew(fn, pieces, list(out_dtypes), name)


def _adam_update(w, g, m, v):
    m2 = ADAM_B1 * m + (1.0 - ADAM_B1) * g
    v2 = ADAM_B2 * v + (1.0 - ADAM_B2) * (g * g)
    m_hat = m2 / (1.0 - ADAM_B1 ** ADAM_STEP)
    v_hat = v2 / (1.0 - ADAM_B2 ** ADAM_STEP)
    delta = -ADAM_LR * (m_hat / (jnp.sqrt(v_hat) + ADAM_EPS) + ADAM_WD * w)
    return g, delta, m2, v2


def _adamw(w, g_pieces, m, v, name):
    n_g = len(g_pieces)

    def fn(w_, *rest):
        g = rest[0]
        for piece in rest[1:n_g]:
            g = g + piece
        return _adam_update(w_, g, rest[n_g], rest[n_g + 1])

    return _ew(fn, [w, *g_pieces, m, v], [F32] * 4, name)


def _adamw_layers(w, mine, other, m, v, layer, name):
    _, a, b = w.shape
    tr = _tile_rows(a, b)
    nb = a // tr

    def body(l_ref, w_ref, mine_ref, other_ref, m_ref, v_ref, *o_refs):
        g = jnp.where(pl.program_id(0) // nb == l_ref[0], mine_ref[...], other_ref[...])
        for ref, val in zip(o_refs, _adam_update(w_ref[...], g, m_ref[...], v_ref[...])):
            ref[...] = val

    both = pl.BlockSpec((None, tr, b), lambda i, l: (i // nb, i % nb, 0))
    one = pl.BlockSpec((None, tr, b), lambda i, l: (0, i % nb, 0))
    return pl.pallas_call(
        body, name=name, out_shape=[_sds(w.shape, F32)] * 4,
        grid_spec=pltpu.PrefetchScalarGridSpec(num_scalar_prefetch=1, grid=(2 * nb,), in_specs=[both, one, one, both, both],
                                               out_specs=[both] * 4),
        compiler_params=pltpu.CompilerParams(dimension_semantics=("arbitrary",), vmem_limit_bytes=VMEM_LIMIT))(
            layer, w, mine, other, m, v)


def _place_shard(full, shard, chip, axis, name):
    _, a, b = shard.shape
    tr = _tile_rows(a, b)
    nb = a // tr

    def body(c_ref, s_ref, f_ref, o_ref):
        o_ref[...] = s_ref[...]

    if axis == 1:
        out_spec = pl.BlockSpec((None, tr, b), lambda l, i, c: (l, c[0] * nb + i, 0))
    else:
        out_spec = pl.BlockSpec((None, tr, b), lambda l, i, c: (l, i, c[0]))
    return pl.pallas_call(
        body, name=name, out_shape=_sds(full.shape, full.dtype), input_output_aliases={2: 0},
        grid_spec=pltpu.PrefetchScalarGridSpec(
            num_scalar_prefetch=1, grid=(2, nb),
            in_specs=[pl.BlockSpec((None, tr, b), lambda l, i, c: (l, i, 0)), pl.BlockSpec(memory_space=pl.ANY)],
            out_specs=out_spec),
        compiler_params=pltpu.CompilerParams(dimension_semantics=("arbitrary", "arbitrary"), vmem_limit_bytes=VMEM_LIMIT))(
            chip, shard, full)


def _ada_fwd(cond, w, b, name):
    r, d = cond.shape
    n = w.shape[1]
    tn = _pick(n, (1152, 768, 512, 384, 256, 128))

    def body(c_ref, w_ref, b_ref, o_ref, s_ref):
        c = c_ref[...]
        sc = c * jax.nn.sigmoid(c)
        s_ref[...] = sc
        o_ref[...] = _nn(sc.astype(MXU), w_ref[...].astype(MXU)) + b_ref[...]

    return _call(body, name=name, grid=(n // tn,),
                 in_specs=[_fixed((r, d)), pl.BlockSpec((d, tn), lambda j: (0, j)), pl.BlockSpec((1, tn), lambda j: (0, j))],
                 out_specs=[pl.BlockSpec((r, tn), lambda j: (0, j)), _fixed((r, d))],
                 out_shape=[_sds((r, n), F32), _sds((r, d), F32)])(cond, w, b)


def _cctx_grad(parts, c_ctx, name):
    n, d = parts.shape

    def body(p_ref, c_ref, o_ref):
        c = c_ref[...]
        sg = jax.nn.sigmoid(c)
        acc = p_ref[0:1, :]
        for j in range(1, n):
            acc = acc + p_ref[j:j + 1, :]
        o_ref[...] = acc * (sg * (1.0 + c * (1.0 - sg)))

    return _call(body, name=name, grid=(1,), in_specs=[_fixed((n, d)), _fixed((1, d))], out_specs=_fixed((1, d)),
                 out_shape=_sds((1, d), F32))(parts, c_ctx)


def _here():
    return lax.axis_index("x"), lax.axis_index("y"), lax.axis_index("c")


def _flip(v, bit):
    return 1 - v if bit else v


def _allgather8(xb, name):
    r, n = xb.shape

    def body(x_ref, out_ref, send_sems, recv_sems, local_sem):
        x, y, c = _here()
        me = 4 * x + 2 * y + c
        local = pltpu.make_async_copy(x_ref, out_ref.at[me], local_sem)
        local.start()
        sends = []
        for k in range(1, 8):
            peer = (_flip(x, k & 4), _flip(y, k & 2), _flip(c, k & 1))
            cp = pltpu.make_async_remote_copy(src_ref=x_ref, dst_ref=out_ref.at[me], send_sem=send_sems.at[k - 1],
                                              recv_sem=recv_sems.at[k - 1], device_id=peer, device_id_type=MESH)
            cp.start()
            sends.append(cp)
        for k in range(1, 8):
            peer = (_flip(x, k & 4), _flip(y, k & 2), _flip(c, k & 1))
            src = 4 * peer[0] + 2 * peer[1] + peer[2]
            pltpu.make_async_remote_copy(src_ref=x_ref, dst_ref=out_ref.at[src], send_sem=send_sems.at[k - 1],
                                         recv_sem=recv_sems.at[k - 1], device_id=peer, device_id_type=MESH).wait_recv()
        for cp in sends:
            cp.wait_send()
        local.wait()

    vmem = pl.BlockSpec(memory_space=pltpu.VMEM)
    return pl.pallas_call(
        body, name=name, out_shape=_sds((8, r, n), xb.dtype), in_specs=[vmem], out_specs=vmem,
        scratch_shapes=[pltpu.SemaphoreType.DMA((7,)), pltpu.SemaphoreType.DMA((7,)), pltpu.SemaphoreType.DMA(())],
        compiler_params=pltpu.CompilerParams(vmem_limit_bytes=VMEM_LIMIT))(xb)


def _shard_of(ref, axis, j, size):
    sl = pl.ds(j * size, size)
    return ref.at[:, sl, :] if axis == 1 else ref.at[:, :, sl]


def _piece(ref, axis, j, size, layer):
    lay, sl = pl.ds(layer, 1), pl.ds(j * size, size)
    return ref.at[lay, sl, :] if axis == 1 else ref.at[lay, :, sl]


def _gather_chips(shards, axes, name):
    n = len(shards)
    fulls = []
    for a, ax in zip(shards, axes):
        assert a.shape[0] == 2
        shp = list(a.shape)
        shp[ax] *= 4
        fulls.append(_sds(tuple(shp), a.dtype))

    def body(*refs):
        ins, outs = refs[:n], refs[n:2 * n]
        ici_send, ici_recv, d2d_send, d2d_recv = refs[2 * n:]
        x, y, c = _here()
        chips = [(_flip(x, k & 2), _flip(y, k & 1)) for k in range(1, 4)]
        sends = []
        for a in range(n):
            size = ins[a].shape[axes[a]]
            for j, (px, py) in enumerate(chips):
                cp = pltpu.make_async_remote_copy(src_ref=ins[a].at[pl.ds(c, 1)], dst_ref=_piece(outs[a], axes[a], 2 * x + y, size, c),
                                                  send_sem=ici_send.at[3 * a + j], recv_sem=ici_recv.at[3 * a + j],
                                                  device_id=(px, py, c), device_id_type=MESH)
                cp.start()
                sends.append(cp)
        for a in range(n):
            size = ins[a].shape[axes[a]]
            for j, (px, py) in enumerate(chips):
                landed = _piece(outs[a], axes[a], 2 * px + py, size, c)
                pltpu.make_async_remote_copy(src_ref=ins[a].at[pl.ds(c, 1)], dst_ref=landed, send_sem=ici_send.at[3 * a + j],
                                             recv_sem=ici_recv.at[3 * a + j], device_id=(px, py, c), device_id_type=MESH).wait_recv()
                cp = pltpu.make_async_remote_copy(src_ref=landed, dst_ref=landed, send_sem=d2d_send.at[3 * a + j],
                                                  recv_sem=d2d_recv.at[3 * a + j], device_id=(x, y, 1 - c), device_id_type=MESH)
                cp.start()
                sends.append(cp)
        for a in range(n):
            size = ins[a].shape[axes[a]]
            for j, (px, py) in enumerate(chips):
                passed = _piece(outs[a], axes[a], 2 * px + py, size, 1 - c)
                pltpu.make_async_remote_copy(src_ref=passed, dst_ref=passed, send_sem=d2d_send.at[3 * a + j],
                                             recv_sem=d2d_recv.at[3 * a + j], device_id=(x, y, 1 - c), device_id_type=MESH).wait_recv()
        for cp in sends:
            cp.wait_send()

    hbm = pl.BlockSpec(memory_space=pl.ANY)
    return pl.pallas_call(
        body, name=name, out_shape=fulls, in_specs=[hbm] * n, out_specs=[hbm] * n,
        scratch_shapes=[pltpu.SemaphoreType.DMA((3 * n,))] * 4)(*shards)


def _swap_layers(arrays, name):
    n = len(arrays)

    def body(*refs):
        ins, outs = refs[:n], refs[n:2 * n]
        send_sems, recv_sems = refs[2 * n:]
        x, y, c = _here()
        copies = []
        for a in range(n):
            cp = pltpu.make_async_remote_copy(src_ref=ins[a].at[pl.ds(1 - c, 1)], dst_ref=outs[a], send_sem=send_sems.at[a],
                                              recv_sem=recv_sems.at[a], device_id=(x, y, 1 - c), device_id_type=MESH)
            cp.start()
            copies.append(cp)
        for cp in copies:
            cp.wait()

    hbm = pl.BlockSpec(memory_space=pl.ANY)
    return pl.pallas_call(
        body, name=name, out_shape=[_sds((1, *a.shape[1:]), a.dtype) for a in arrays], in_specs=[hbm] * n, out_specs=[hbm] * n,
        scratch_shapes=[pltpu.SemaphoreType.DMA((n,)), pltpu.SemaphoreType.DMA((n,))])(*arrays)


def _scatter_chips(fulls, axes, name):
    n = len(fulls)
    recvs = []
    for a, ax in zip(fulls, axes):
        shp = list(a.shape)
        shp[ax] //= 4
        recvs.append(_sds((3, *shp), a.dtype))

    def body(*refs):
        ins, outs = refs[:n], refs[n:2 * n]
        send_sems, recv_sems = refs[2 * n:]
        x, y, c = _here()
        sends = []
        for a in range(n):
            size = ins[a].shape[axes[a]] // 4
            for k in range(1, 4):
                peer = (_flip(x, k & 2), _flip(y, k & 1), c)
                cp = pltpu.make_async_remote_copy(src_ref=_shard_of(ins[a], axes[a], 2 * peer[0] + peer[1], size),
                                                  dst_ref=outs[a].at[k - 1],
                                                  send_sem=send_sems.at[3 * a + k - 1], recv_sem=recv_sems.at[3 * a + k - 1],
                                                  device_id=peer, device_id_type=MESH)
                cp.start()
                sends.append(cp)
        for cp in sends:
            cp.wait_recv()
        for cp in sends:
            cp.wait_send()

    hbm = pl.BlockSpec(memory_space=pl.ANY)
    return pl.pallas_call(
        body, name=name, out_shape=recvs, in_specs=[hbm] * n, out_specs=[hbm] * n,
        scratch_shapes=[pltpu.SemaphoreType.DMA((3 * n,)), pltpu.SemaphoreType.DMA((3 * n,))])(*fulls)


def _sibling_swap(arrays, name):
    n = len(arrays)

    def body(*refs):
        ins, outs = refs[:n], refs[n:2 * n]
        send_sems, recv_sems = refs[2 * n:]
        x, y, c = _here()
        copies = []
        for a in range(n):
            cp = pltpu.make_async_remote_copy(src_ref=ins[a], dst_ref=outs[a], send_sem=send_sems.at[a], recv_sem=recv_sems.at[a],
                                              device_id=(x, y, 1 - c), device_id_type=MESH)
            cp.start()
            copies.append(cp)
        for cp in copies:
            cp.wait()

    hbm = pl.BlockSpec(memory_space=pl.ANY)
    return pl.pallas_call(
        body, name=name, out_shape=[_sds(a.shape, a.dtype) for a in arrays], in_specs=[hbm] * n, out_specs=[hbm] * n,
        scratch_shapes=[pltpu.SemaphoreType.DMA((n,)), pltpu.SemaphoreType.DMA((n,))])(*arrays)


def _ffn_fwd(xs, g, mods, k0, w_up, w_down, s, tag):
    h, ua, ub, act = _norm_mm(xs, g, mods, k0, k0 + 1, w_up, s, tag + "_up", True)
    y, xn = _mm_res(act, w_down, xs, mods, k0 + 2, 0.5, s, tag + "_down")
    return xn, (xs, h, ua, ub, act, y)


def _dw(gw, key, a, b, name, col0=0, n_total=None):
    shape = (gw["depth"], a.shape[1], n_total or b.shape[1])
    gw[key] = _mm(a, b, "tn", F32, name, into=(gw.get(key), shape, gw["layer"], col0))


def _ffn_bwd(dxn, saved, g, mods, k0, w_up, w_down, s, tag, gw, up_key, down_key):
    xs, h, ua, ub, act, y = saved
    dy, dgate, dua, dub = _resb_mm(dxn, y, mods, k0 + 2, 0.5, w_down, s, tag + "_down_dx", (ua, ub))
    _dw(gw, down_key, act, dy, tag + "_down_dw")
    f = dua.shape[1]
    _dw(gw, up_key, h, dua, tag + "_upa_dw", 0, 2 * f)
    _dw(gw, up_key, h, dub, tag + "_upb_dw", f, 2 * f)
    dx, dsh, dsc, dg = _mm_normb([dua, dub], w_up, xs, dxn, g, mods, k0 + 1, s, tag + "_up_dx")
    return dx, dg, [dsh, dsc, dgate]


def _mix_fwd(xs, g, mods, wl, pl_, tabs, s, tag):
    cos, sin = tabs
    h, p = _norm_mm(xs, g, mods, 3, 4, wl["w_in"], s, tag + "_in", False)
    q, k, v = _rope(p, cos, sin, tag + "_rope")
    bias = _bias_table(pl_["rpb"], s // GRID_W)
    oa = _na_fwd(q, k, v, bias, s, tag + "_na")
    ob = _gmlp(p, pl_["ln_v_g"], pl_["ln_v_b"], pl_["w_s"], pl_["b_s"], tag + "_sg")
    pa = _mm(oa, wl["w_pa"], "nn", MXU, tag + "_pa")
    pb = _mm(ob, wl["w_pb"], "nn", MXU, tag + "_pb")
    mg = _merge(pa, pb, p, pl_["b_gate"], tag + "_merge")
    y, xn = _mm_res(mg, wl["w_o"], xs, mods, 5, 1.0, s, tag + "_o")
    return xn, (xs, h, p, q, k, v, bias, oa, ob, pa, pb, mg, y)


def _mix_bwd(dxn, saved, g, mods, wl, pl_, tabs, s, tag, gw):
    xs, h, p, q, k, v, bias, oa, ob, pa, pb, mg, y = saved
    cos, sin = tabs
    gp = {}
    dy, dgate, dmg = _resb_mm(dxn, y, mods, 5, 1.0, wl["w_o"], s, tag + "_o_dx")
    _dw(gw, "w_o", mg, dy, tag + "_o_dw")
    dpa, dpb, dla, dlb, dba, dbb = _merge_bwd(dmg, pa, pb, p, pl_["b_gate"], tag + "_merge_b")
    gp["b_gate"] = jnp.concatenate([dba, dbb], axis=1)
    _dw(gw, "w_pa", oa, dpa, tag + "_pa_dw")
    doa = _mm(dpa, wl["w_pa"], "nt", MXU, tag + "_pa_dx")
    _dw(gw, "w_pb", ob, dpb, tag + "_pb_dw")
    dob = _mm(dpb, wl["w_pb"], "nt", MXU, tag + "_pb_dx")
    du, dvs, gp["w_s"], gp["b_s"], gp["ln_v_g"], gp["ln_v_b"] = _gmlp_bwd(
        p, dob, pl_["ln_v_g"], pl_["ln_v_b"], pl_["w_s"], pl_["b_s"], tag + "_sg_b")
    dqr, dkr, dv, dbias = _na_bwd(q, k, v, doa, bias, s, tag + "_na_b")
    gp["rpb"] = _rpb_grad(dbias, s // GRID_W, tag + "_rpb")
    dq, dk, dvv = _rope_bwd(dqr, dkr, dv, cos, sin, tag + "_rope_b")
    dp = [dq, dk, dvv, du, dvs, dla, dlb]
    n_in = sum(piece.shape[1] for piece in dp)
    for j, piece in enumerate(dp):
        _dw(gw, "w_in", h, piece, tag + f"_in_dw{j}", sum(q.shape[1] for q in dp[:j]), n_in)
    dx, dsh, dsc, dg = _mm_normb(dp, wl["w_in"], xs, dxn, g, mods, 4, s, tag + "_in_dx")
    return dx, gp, dg, [dsh, dsc, dgate]


def _local_step(x, ctx, tgt, mods, wts, prm):
    s, d = x.shape
    depth = mods.shape[0]
    tabs = _rope_tables(s, ctx.shape[0])
    xs = jnp.concatenate([x, ctx], axis=0)
    saved = []
    for l in range(depth):
        wl = {k: (v, l) for k, v in wts.items()}
        pl_ = _layer_params(prm, l)
        xs, s1 = _ffn_fwd(xs, pl_["g"][0], mods[l], 0, wl["w_ff1_up"], wl["w_ff1_down"], s, f"l{l}_ff1")
        xs, s2 = _mix_fwd(xs, pl_["g"][1], mods[l], wl, pl_, tabs, s, f"l{l}_mix")
        xs, s3 = _ffn_fwd(xs, pl_["g"][2], mods[l], 6, wl["w_ff2_up"], wl["w_ff2_down"], s, f"l{l}_ff2")
        saved.append((s1, s2, s3))
    loss, dxs, d_final_g = _final(xs, tgt, prm["final_g"].reshape(1, d), "final")
    gw = {"depth": depth}
    gp = {k: [None] * depth for k in ("norm_g", "b_gate", "rpb", "ln_v_g", "ln_v_b", "w_s", "b_s")}
    dmods = [None] * depth
    for l in reversed(range(depth)):
        wl = {k: (v, l) for k, v in wts.items()}
        pl_ = _layer_params(prm, l)
        s1, s2, s3 = saved[l]
        gw["layer"] = l
        dxs, dg2, dm2 = _ffn_bwd(dxs, s3, pl_["g"][2], mods[l], 6, wl["w_ff2_up"], wl["w_ff2_down"], s, f"l{l}_ff2",
                                 gw, "w_ff2_up", "w_ff2_down")
        dxs, gpm, dg1, dm1 = _mix_bwd(dxs, s2, pl_["g"][1], mods[l], wl, pl_, tabs, s, f"l{l}_mix", gw)
        dxs, dg0, dm0 = _ffn_bwd(dxs, s1, pl_["g"][0], mods[l], 0, wl["w_ff1_up"], wl["w_ff1_down"], s, f"l{l}_ff1",
                                 gw, "w_ff1_up", "w_ff1_down")
        gp["b_gate"][l] = gpm["b_gate"][0]
        gp["rpb"][l] = gpm["rpb"]
        gp["ln_v_g"][l] = gpm["ln_v_g"][0]
        gp["ln_v_b"][l] = gpm["ln_v_b"][0]
        gp["w_s"][l] = gpm["w_s"]
        gp["b_s"][l] = gpm["b_s"][..., 0]
        gp["norm_g"][l] = jnp.concatenate([dg0, dg1, dg2], axis=0)
        dmods[l] = jnp.concatenate(dm0 + dm1 + dm2, axis=1)
    gw = {k: gw[k] for k in wts}
    gp = {k: jnp.stack(v) for k, v in gp.items()}
    gp["final_g"] = d_final_g[0]
    return loss[0, 0], dxs[:s], jnp.stack(dmods), gw, gp


def _layer_params(prm, l):
    d = prm["norm_g"].shape[-1]
    return {
        "g": [prm["norm_g"][l, i].reshape(1, d) for i in range(3)],
        "b_gate": prm["b_gate"][l].reshape(1, -1),
        "rpb": prm["rpb"][l],
        "ln_v_g": prm["ln_v_g"][l].reshape(1, -1),
        "ln_v_b": prm["ln_v_b"][l].reshape(1, -1),
        "w_s": prm["w_s"][l],
        "b_s": prm["b_s"][l][..., None],
    }


SMALL = ("norm_g", "b_gate", "rpb", "ln_v_g", "ln_v_b", "w_s", "b_s", "final_g")
PACK_LANES = 1024


def _pack(parts):
    flat = jnp.concatenate([p.reshape(-1) for p in parts])
    rows = -(-flat.shape[0] // PACK_LANES)
    rows = -(-rows // 8) * 8
    return jnp.pad(flat, (0, rows * PACK_LANES - flat.shape[0])).reshape(rows, PACK_LANES)


def _unpack(flat, shapes):
    out, off = [], 0
    for shp in shapes:
        n = int(np.prod(shp))
        out.append(flat[..., off:off + n].reshape(*flat.shape[:-1], *shp))
        off += n
    return out


def kernel(x, c, ctx, c_ctx, w_ada, b_ada, norm_g, w_ff1_up, w_ff1_down, w_in, b_gate, rpb, ln_v_g, ln_v_b, w_s, b_s, w_pa, w_pb, w_o, w_ff2_up, w_ff2_down, final_g, loss_target, m_c_ctx, m_w_ada, m_b_ada, m_norm_g, m_w_ff1_up, m_w_ff1_down, m_w_in, m_b_gate, m_rpb, m_ln_v_g, m_ln_v_b, m_w_s, m_b_s, m_w_pa, m_w_pb, m_w_o, m_w_ff2_up, m_w_ff2_down, m_final_g, v_c_ctx, v_w_ada, v_b_ada, v_norm_g, v_w_ff1_up, v_w_ff1_down, v_w_in, v_b_gate, v_rpb, v_ln_v_g, v_ln_v_b, v_w_s, v_b_s, v_w_pa, v_w_pb, v_w_o, v_w_ff2_up, v_w_ff2_down, v_final_g):
    weights = dict(c_ctx=c_ctx, w_ada=w_ada, b_ada=b_ada, norm_g=norm_g, w_ff1_up=w_ff1_up, w_ff1_down=w_ff1_down, w_in=w_in,
                   b_gate=b_gate, rpb=rpb, ln_v_g=ln_v_g, ln_v_b=ln_v_b, w_s=w_s, b_s=b_s, w_pa=w_pa, w_pb=w_pb, w_o=w_o,
                   w_ff2_up=w_ff2_up, w_ff2_down=w_ff2_down, final_g=final_g)
    mom_m = dict(c_ctx=m_c_ctx, w_ada=m_w_ada, b_ada=m_b_ada, norm_g=m_norm_g, w_ff1_up=m_w_ff1_up, w_ff1_down=m_w_ff1_down,
                 w_in=m_w_in, b_gate=m_b_gate, rpb=m_rpb, ln_v_g=m_ln_v_g, ln_v_b=m_ln_v_b, w_s=m_w_s, b_s=m_b_s, w_pa=m_w_pa,
                 w_pb=m_w_pb, w_o=m_w_o, w_ff2_up=m_w_ff2_up, w_ff2_down=m_w_ff2_down, final_g=m_final_g)
    mom_v = dict(c_ctx=v_c_ctx, w_ada=v_w_ada, b_ada=v_b_ada, norm_g=v_norm_g, w_ff1_up=v_w_ff1_up, w_ff1_down=v_w_ff1_down,
                 w_in=v_w_in, b_gate=v_b_gate, rpb=v_rpb, ln_v_g=v_ln_v_g, ln_v_b=v_ln_v_b, w_s=v_w_s, b_s=v_b_s, w_pa=v_w_pa,
                 w_pb=v_w_pb, w_o=v_w_o, w_ff2_up=v_w_ff2_up, w_ff2_down=v_w_ff2_down, final_g=v_final_g)
    order = list(weights)
    mx, my, mc = _here()
    dev = 4 * mx + 2 * my + mc
    chip = 2 * mx + my
    depth, d, n_ada = w_ada.shape
    dq = d // 4

    c_all = _allgather8(jnp.pad(c, ((0, 7), (0, 0))), "gather_c")[:, 0, :]
    cond = jnp.concatenate([c_all, c_ctx[None, :], jnp.zeros((7, d), F32)], axis=0)
    b_shard = lax.dynamic_slice(b_ada, (0, chip * n_ada), (depth, n_ada))
    proj = [_ada_fwd(cond, w_ada[l], b_shard[l:l + 1], f"ada{l}") for l in range(depth)]
    silu_c = proj[0][1]
    mods_sh = _allgather8(jnp.concatenate([p[0] for p in proj], axis=0), "gather_mods")
    mods_all = jnp.transpose(mods_sh[0::2].reshape(4, depth, 16, n_ada), (1, 2, 0, 3)).reshape(depth, 16, N_MOD, d)
    mods = jnp.stack([lax.dynamic_index_in_dim(mods_all, dev, axis=1, keepdims=False), mods_all[:, 8]], axis=1)

    shards = [weights[k].astype(MXU) for k in BIG]
    full = _gather_chips(shards, [SHARD_AXIS[k] for k in BIG], "gather_w")
    my_chip = jnp.reshape(chip, (1,)).astype(jnp.int32)
    full = [_place_shard(f, sh, my_chip, SHARD_AXIS[k], "place_" + k) for k, f, sh in zip(BIG, full, shards)]
    wts = dict(zip(BIG, full))
    prm = {k: weights[k] for k in SMALL if k != "norm_g"}
    norm_full = _allgather8(jnp.pad(norm_g.reshape(depth * 3, dq), ((0, 8 - depth * 3), (0, 0))), "gather_norm_g")
    prm["norm_g"] = jnp.transpose(norm_full[0::2, :depth * 3].reshape(4, depth, 3, dq), (1, 2, 0, 3)).reshape(depth, 3, d)

    loss, grad_x, dmods, gw, gp = _local_step(x[0], ctx[0], loss_target[0], mods, wts, prm)
    loss = lax.psum(loss, ("x", "y", "c"))

    small_shapes = [(depth, 2, N_MOD * d)] + [weights[k].shape if k != "norm_g" else (depth, 3, d) for k in SMALL]
    packed = _allgather8(_pack([dmods.reshape(depth, 2, N_MOD * d)] + [gp[k] for k in SMALL]), "gather_small")
    rows = packed.shape[1]
    total = _sum_pieces([packed[i] for i in range(8)], "sum_small")[0].reshape(-1)
    sums = dict(zip(("dmods",) + SMALL, _unpack(total, small_shapes)))
    dmods_dev = _unpack(packed.reshape(8, rows * PACK_LANES), small_shapes[:1])[0]

    g_ada, cc_parts = [], []
    for l in range(depth):
        dm = jnp.concatenate([dmods_dev[:, l, 0], sums["dmods"][l, 1][None], jnp.zeros((7, N_MOD * d), F32)], axis=0)
        dm_sh = lax.dynamic_slice(dm, (0, chip * n_ada), (16, n_ada))
        g_ada.append(_mm(silu_c, dm_sh, "tn", F32, f"ada{l}_dw"))
        cc_parts.append(_mm(dm_sh, w_ada[l], "nt", F32, f"ada{l}_dc")[8:9])
    cc_all = _allgather8(jnp.pad(jnp.concatenate(cc_parts, axis=0), ((0, 8 - depth), (0, 0))), "gather_cctx")
    g_cctx = _cctx_grad(cc_all[0::2, :depth].reshape(4 * depth, d), c_ctx.reshape(1, d), "cctx_grad")

    axes = [SHARD_AXIS[k] for k in BIG]
    from_sibling = _swap_layers([gw[k] for k in BIG], "swap_layer_gw")
    my_layer = jnp.reshape(mc, (1,)).astype(jnp.int32)
    pair = [_pair_sum(gw[k], r, my_layer, "pair_" + k) for k, r in zip(BIG, from_sibling)]
    recv = _scatter_chips([p[1] for p in pair], axes, "scatter_gw")
    mine = []
    for k, ax, p, r in zip(BIG, axes, pair, recv):
        size = p[0].shape[ax] // 4
        own = lax.dynamic_slice_in_dim(p[0], chip * size, size, axis=ax)
        mine.append(_sum_pieces([own, r[0], r[1], r[2]], "sum_" + k)[0])
    other = _sibling_swap(mine, "swap_gw")

    res = {k: _adamw_layers(weights[k], a, b, mom_m[k], mom_v[k], my_layer, "adamw_" + k) for k, a, b in zip(BIG, mine, other)}
    pieces = {"w_ada": [jnp.stack(g_ada)]}
    pieces["b_ada"] = [sums["dmods"][:, 0], sums["dmods"][:, 1]]
    pieces["c_ctx"] = [g_cctx[0]]
    for k in SMALL:
        pieces[k] = [sums[k]]
    pieces["norm_g"] = [lax.dynamic_slice_in_dim(sums["norm_g"], chip * dq, dq, axis=2)]
    res.update({k: _adamw(weights[k], pieces[k], mom_m[k], mom_v[k], "adamw_" + k) for k in pieces})
    return (loss, grad_x[None], *[res[k][0] for k in order], *[res[k][1] for k in order],
            *[res[k][2] for k in order], *[res[k][3] for k in order])
```

```python
import numpy as np
import jax
import jax.numpy as jnp
from jax import lax
from jax.experimental import pallas as pl
from jax.experimental.pallas import tpu as pltpu

F32 = jnp.float32
MXU = jnp.bfloat16
EPS = 1e-6
GRID_W, HEADS, HEAD_DIM = 64, 8, 64
NA_WIDTH = SG_WIDTH = 512
WIN_H, WIN_W = 8, 16
SG_CHUNK, SG_GROUPS = 128, 4
N_MOD = 9
ROPE_THETA = 10000.0
Q_ROWS, K_ROWS = 4, 12
TQ, TK = Q_ROWS * GRID_W, K_ROWS * GRID_W
TM = 256
LANES = 128
NEG = -1e30
VMEM_LIMIT = 56 * 2 ** 20
ADAM_LR, ADAM_B1, ADAM_B2, ADAM_EPS, ADAM_WD, ADAM_STEP = 0.001, 0.9, 0.999, 1e-08, 0.01, 10
MESH = pl.DeviceIdType.MESH
BIG = ("w_ff1_up", "w_ff1_down", "w_in", "w_pa", "w_pb", "w_o", "w_ff2_up", "w_ff2_down")
SHARD_AXIS = {"w_ff1_up": 2, "w_ff1_down": 1, "w_in": 2, "w_pa": 2, "w_pb": 2, "w_o": 1, "w_ff2_up": 2, "w_ff2_down": 1}


def _call(body, *, name, grid, in_specs, out_specs, out_shape, scratch=(), aliases=None):
    return pl.pallas_call(
        body, name=name, grid=grid, in_specs=in_specs, out_specs=out_specs, out_shape=out_shape,
        scratch_shapes=list(scratch), input_output_aliases=aliases or {},
        compiler_params=pltpu.CompilerParams(dimension_semantics=("arbitrary",) * len(grid), vmem_limit_bytes=VMEM_LIMIT))


def _w_dims(w):
    return w[0].shape[1:] if isinstance(w, tuple) else w.shape


def _w_arr(w):
    return w[0] if isinstance(w, tuple) else w


def _w_spec(w, block, index):
    if isinstance(w, tuple):
        layer = w[1]
        return pl.BlockSpec((None, *block), lambda *ids: (layer, *index(*ids)))
    return pl.BlockSpec(block, index)


def _pick(n, prefs):
    for p in prefs:
        if n % p == 0:
            return p
    return n


def _row_tile(t):
    return _pick(t, (640, 256))


def _rows(tm, n, col=0):
    return pl.BlockSpec((tm, n), lambda i: (i, col))


def _fixed(shape):
    return pl.BlockSpec(shape, lambda *_: (0,) * len(shape))


def _sds(shape, dtype):
    return jax.ShapeDtypeStruct(shape, dtype)


def _mm(a, b, mode, out_dtype, name, into=None):
    if mode == "tn":
        r, m = a.shape
        n = b.shape[1]
        tm = _pick(m, (1024, 1408, 704, 512, 256, 128))
        tn = _pick(n, (512, 1408, 256, 128))
        tr = _pick(r, (1280, 640, 512, 256, 128))

        def body(a_ref, b_ref, *rest):
            o_ref = rest[-1]

            @pl.when(pl.program_id(2) == 0)
            def _():
                o_ref[...] = jnp.zeros_like(o_ref)

            o_ref[...] += lax.dot_general(a_ref[...].astype(MXU), b_ref[...].astype(MXU), (((0,), (0,)), ((), ())),
                                          preferred_element_type=F32)

        in_specs = [pl.BlockSpec((tr, tm), lambda i, j, k: (k, i)), pl.BlockSpec((tr, tn), lambda i, j, k: (k, j))]
        if into is None:
            return _call(body, name=name, grid=(m // tm, n // tn, r // tr), in_specs=in_specs,
                         out_specs=pl.BlockSpec((tm, tn), lambda i, j, k: (i, j)), out_shape=_sds((m, n), F32))(a, b)
        buf, shape, layer, col0 = into
        out_spec = pl.BlockSpec((None, tm, tn), lambda i, j, k: (layer, i, j + col0 // tn))
        if buf is None:
            return _call(body, name=name, grid=(m // tm, n // tn, r // tr), in_specs=in_specs, out_specs=out_spec,
                         out_shape=_sds(shape, F32))(a, b)
        return _call(body, name=name, grid=(m // tm, n // tn, r // tr), in_specs=in_specs + [pl.BlockSpec(memory_space=pl.ANY)],
                     out_specs=out_spec, out_shape=_sds(shape, F32), aliases={2: 0})(a, b, buf)
    m, k = a.shape
    n = _w_dims(b)[1] if mode == "nn" else _w_dims(b)[0]
    tm = _pick(m, (1280, 640, 512, 256, 128) if k <= 2816 else (640, 512, 256, 128))
    tn = _pick(n, (512, 1408, 256, 128))
    dims = (((1,), (0,)), ((), ())) if mode == "nn" else (((1,), (1,)), ((), ()))

    def body(a_ref, b_ref, o_ref):
        o_ref[...] = lax.dot_general(a_ref[...].astype(MXU), b_ref[...].astype(MXU), dims,
                                     preferred_element_type=F32).astype(o_ref.dtype)

    b_spec = _w_spec(b, (k, tn), lambda i, j: (0, j)) if mode == "nn" else _w_spec(b, (tn, k), lambda i, j: (j, 0))
    return _call(body, name=name, grid=(m // tm, n // tn), in_specs=[pl.BlockSpec((tm, k), lambda i, j: (i, 0)), b_spec],
                 out_specs=pl.BlockSpec((tm, tn), lambda i, j: (i, j)), out_shape=_sds((m, n), out_dtype))(a, _w_arr(b))


def _row_chunks(tm):
    rc = _pick(tm, (256, 128))
    return [slice(r, r + rc) for r in range(0, tm, rc)]


def _ctx_rows(i, tm, s):
    return (i * tm + lax.broadcasted_iota(jnp.int32, (tm, 1), 0)) >= s


def _mod_row(m_ref, k, ctx):
    return jnp.where(ctx, m_ref[1, k:k + 1, :], m_ref[0, k:k + 1, :])


def _stream_sums(i, tm, s, refs_and_vals):
    @pl.when((i + 1) * tm <= s)
    def _():
        for ref, val in refs_and_vals:
            ref[0] += jnp.sum(val, axis=0, keepdims=True)

    @pl.when((i + 1) * tm > s)
    def _():
        ctx = _ctx_rows(i, tm, s)
        for ref, val in refs_and_vals:
            ref[0] += jnp.sum(jnp.where(ctx, 0.0, val), axis=0, keepdims=True)
            ref[1] += jnp.sum(jnp.where(ctx, val, 0.0), axis=0, keepdims=True)


def _norm_mm(xs, g, mods, k_shift, k_scale, w, s, name, glu):
    t, d = xs.shape
    n = _w_dims(w)[1] // 2 if glu else _w_dims(w)[1]
    tm = _pick(t, (1280, 640, 256))
    tn = _pick(n, (256, 128)) if glu else _pick(n, (512, 256, 128))
    nj = n // tn

    def body(x_ref, g_ref, m_ref, *refs):
        i, j = pl.program_id(0), pl.program_id(1)
        w_refs, h_ref, o_refs = refs[:2 if glu else 1], refs[2 if glu else 1], refs[3 if glu else 2:]

        @pl.when(j == 0)
        def _():
            x = x_ref[...]
            rstd = lax.rsqrt(jnp.mean(x * x, axis=-1, keepdims=True) + EPS)
            ctx = _ctx_rows(i, tm, s)
            h = x * rstd * g_ref[...] * (1.0 + _mod_row(m_ref, k_scale, ctx)) + _mod_row(m_ref, k_shift, ctx)
            h_ref[...] = h.astype(h_ref.dtype)

        for rows in _row_chunks(tm):
            h = h_ref[rows, :]
            a = _nn(h, w_refs[0][...])
            o_refs[0][rows, :] = a.astype(o_refs[0].dtype)
            if glu:
                b = _nn(h, w_refs[1][...])
                o_refs[1][rows, :] = b.astype(o_refs[1].dtype)
                o_refs[2][rows, :] = (a * jax.nn.sigmoid(a) * b).astype(o_refs[2].dtype)

    tile = pl.BlockSpec((tm, tn), lambda i, j: (i, j))
    row = pl.BlockSpec((tm, d), lambda i, j: (i, 0))
    w_specs = [_w_spec(w, (d, tn), lambda i, j: (0, j))] + ([_w_spec(w, (d, tn), lambda i, j: (0, j + nj))] if glu else [])
    n_out = 3 if glu else 1
    return _call(body, name=name, grid=(t // tm, nj),
                 in_specs=[row, _fixed((1, d)), _fixed((2, N_MOD, d))] + w_specs,
                 out_specs=[row] + [tile] * n_out,
                 out_shape=[_sds((t, d), MXU)] + [_sds((t, n), MXU)] * n_out)(xs, g, mods, *([_w_arr(w)] * (2 if glu else 1)))


def _mm_res(a, w, xs, mods, k_gate, coef, s, name):
    t, k = a.shape
    d = _w_dims(w)[1]
    tm = _pick(t, (1280, 640, 256))
    tn = _pick(d, (512, 256, 128))

    def body(a_ref, w_ref, x_ref, m_ref, y_ref, o_ref):
        y = _nn(a_ref[...], w_ref[...])
        y_ref[...] = y.astype(y_ref.dtype)
        gate = _mod_row(m_ref, k_gate, _ctx_rows(pl.program_id(0), tm, s))
        o_ref[...] = x_ref[...] + (coef * gate) * y

    tile = pl.BlockSpec((tm, tn), lambda i, j: (i, j))
    return _call(body, name=name, grid=(t // tm, d // tn),
                 in_specs=[pl.BlockSpec((tm, k), lambda i, j: (i, 0)), _w_spec(w, (k, tn), lambda i, j: (0, j)), tile,
                           pl.BlockSpec((2, N_MOD, tn), lambda i, j: (0, 0, j))],
                 out_specs=[tile, tile], out_shape=[_sds((t, d), MXU), _sds((t, d), F32)])(a, _w_arr(w), xs, mods)


def _resb_mm(dxn, y, mods, k_gate, coef, w, s, name, ups=None):
    t, d = dxn.shape
    n = _w_dims(w)[0]
    tm = _pick(t, (1280, 640, 256))
    tn = _pick(n, (256, 128)) if ups else _pick(n, (512, 256, 128))

    def body(dx_ref, y_ref, m_ref, w_ref, *refs):
        i, j = pl.program_id(0), pl.program_id(1)
        u_refs, (dy_ref, dgt_ref), o_refs = (refs[:2], refs[2:4], refs[4:]) if ups else ((), refs[:2], refs[2:])

        @pl.when((i == 0) & (j == 0))
        def _():
            dgt_ref[...] = jnp.zeros_like(dgt_ref)

        @pl.when(j == 0)
        def _():
            dx = dx_ref[...]
            gate = _mod_row(m_ref, k_gate, _ctx_rows(i, tm, s))
            dy_ref[...] = ((coef * gate) * dx).astype(dy_ref.dtype)
            _stream_sums(i, tm, s, [(dgt_ref, coef * y_ref[...].astype(F32) * dx)])

        dact = _nt(dy_ref[...], w_ref[...])
        if ups:
            a, b = u_refs[0][...].astype(F32), u_refs[1][...].astype(F32)
            sg = jax.nn.sigmoid(a)
            o_refs[0][...] = (dact * b * sg * (1.0 + a * (1.0 - sg))).astype(o_refs[0].dtype)
            o_refs[1][...] = (dact * a * sg).astype(o_refs[1].dtype)
        else:
            o_refs[0][...] = dact.astype(o_refs[0].dtype)

    tile = pl.BlockSpec((tm, tn), lambda i, j: (i, j))
    row = pl.BlockSpec((tm, d), lambda i, j: (i, 0))
    n_out = 2 if ups else 1
    return _call(body, name=name, grid=(t // tm, n // tn),
                 in_specs=[row, row, _fixed((2, N_MOD, d)), _w_spec(w, (tn, d), lambda i, j: (j, 0))] + ([tile, tile] if ups else []),
                 out_specs=[row, _fixed((2, 1, d))] + [tile] * n_out,
                 out_shape=[_sds((t, d), MXU), _sds((2, 1, d), F32)] + [_sds((t, n), MXU)] * n_out)(
                     dxn, y, mods, _w_arr(w), *(ups or ()))


def _mm_normb(a_list, w, xs, dres, g, mods, k_scale, s, name):
    t, d = xs.shape
    tm = _pick(t, (640, 256))
    n_a = len(a_list)
    widths = [a.shape[1] for a in a_list]
    tk = next(c for c in (1408, 1536, 1024, 512, 256, 128) if all(wd % c == 0 for wd in widths))
    counts = [wd // tk for wd in widths]
    starts = [sum(counts[:q]) for q in range(n_a)]
    nk = sum(counts)
    k_all = sum(widths)
    resident = 4 * k_all * (d + tm) + 24 * tm * d <= 48 * 2 ** 20

    def finish(i, dh, x_ref, dr_ref, g_ref, m_ref, dx_ref, dsh_ref, dsc_ref, dg_ref):
        x = x_ref[...]
        rstd = lax.rsqrt(jnp.mean(x * x, axis=-1, keepdims=True) + EPS)
        xhat = x * rstd
        gg = g_ref[...]
        _stream_sums(i, tm, s, [(dsh_ref, dh), (dsc_ref, dh * (xhat * gg))])
        dy = dh * (1.0 + _mod_row(m_ref, k_scale, _ctx_rows(i, tm, s)))
        dg_ref[...] += jnp.sum(dy * xhat, axis=0, keepdims=True)
        dxh = dy * gg
        dx_ref[...] = dr_ref[...] + rstd * (dxh - xhat * jnp.mean(dxh * xhat, axis=-1, keepdims=True))

    out_shape = [_sds((t, d), F32), _sds((2, 1, d), F32), _sds((2, 1, d), F32), _sds((1, d), F32)]
    if resident:
        def body_resident(*refs):
            a_refs, (w_ref, x_ref, dr_ref, g_ref, m_ref, dx_ref, dsh_ref, dsc_ref, dg_ref) = refs[:n_a], refs[n_a:]
            i = pl.program_id(0)

            @pl.when(i == 0)
            def _():
                dsh_ref[...] = jnp.zeros_like(dsh_ref)
                dsc_ref[...] = jnp.zeros_like(dsc_ref)
                dg_ref[...] = jnp.zeros_like(dg_ref)

            dh, off = None, 0
            for q in range(n_a):
                part = _nt(a_refs[q][...], w_ref[:, off:off + widths[q]])
                dh = part if dh is None else dh + part
                off += widths[q]
            finish(i, dh, x_ref, dr_ref, g_ref, m_ref, dx_ref, dsh_ref, dsc_ref, dg_ref)

        rows = pl.BlockSpec((tm, d), lambda i: (i, 0))
        return _call(body_resident, name=name, grid=(t // tm,),
                     in_specs=[_rows(tm, wd) for wd in widths] + [_w_spec(w, (d, k_all), lambda i: (0, 0)), rows, rows,
                                                                  _fixed((1, d)), _fixed((2, N_MOD, d))],
                     out_specs=[rows, _fixed((2, 1, d)), _fixed((2, 1, d)), _fixed((1, d))],
                     out_shape=out_shape)(*a_list, _w_arr(w), xs, dres, g, mods)

    def body(*refs):
        a_refs, (w_ref, x_ref, dr_ref, g_ref, m_ref, dx_ref, dsh_ref, dsc_ref, dg_ref, acc) = refs[:n_a], refs[n_a:]
        i, k = pl.program_id(0), pl.program_id(1)

        @pl.when((i == 0) & (k == 0))
        def _():
            dsh_ref[...] = jnp.zeros_like(dsh_ref)
            dsc_ref[...] = jnp.zeros_like(dsc_ref)
            dg_ref[...] = jnp.zeros_like(dg_ref)

        @pl.when(k == 0)
        def _():
            acc[...] = jnp.zeros_like(acc)

        for q in range(n_a):
            @pl.when((k >= starts[q]) & (k < starts[q] + counts[q]))
            def _():
                acc[...] += _nt(a_refs[q][...], w_ref[...])

        @pl.when(k == nk - 1)
        def _():
            finish(i, acc[...], x_ref, dr_ref, g_ref, m_ref, dx_ref, dsh_ref, dsc_ref, dg_ref)

    row = pl.BlockSpec((tm, d), lambda i, k: (i, 0))
    a_specs = [pl.BlockSpec((tm, tk), lambda i, k, q=q: (i, jnp.clip(k - starts[q], 0, counts[q] - 1))) for q in range(n_a)]
    return _call(body, name=name, grid=(t // tm, nk),
                 in_specs=a_specs + [_w_spec(w, (d, tk), lambda i, k: (0, k)), row, row, _fixed((1, d)), _fixed((2, N_MOD, d))],
                 out_specs=[row, _fixed((2, 1, d)), _fixed((2, 1, d)), _fixed((1, d))], out_shape=out_shape,
                 scratch=[pltpu.VMEM((tm, d), F32)])(*a_list, _w_arr(w), xs, dres, g, mods)


def _rope_tables(s, ctx_len):
    n_freq = HEAD_DIM // 4
    tok = jnp.arange(s)
    freqs = ROPE_THETA ** (-jnp.arange(n_freq, dtype=F32) / n_freq)
    ang = jnp.concatenate([(tok // GRID_W).astype(F32)[:, None] * freqs, (tok % GRID_W).astype(F32)[:, None] * freqs], axis=-1)
    cos = jnp.repeat(jnp.cos(ang), 2, axis=-1)
    sin = jnp.repeat(jnp.sin(ang), 2, axis=-1) * jnp.tile(jnp.array([-1.0, 1.0], F32), HEAD_DIM // 2)
    cos = jnp.concatenate([jnp.tile(cos, (1, LANES // HEAD_DIM)), jnp.ones((ctx_len, LANES), F32)], axis=0)
    sin = jnp.concatenate([jnp.tile(sin, (1, LANES // HEAD_DIM)), jnp.zeros((ctx_len, LANES), F32)], axis=0)
    return cos, sin


def _swap_pairs(x):
    n = x.shape[-1]
    lane = lax.broadcasted_iota(jnp.int32, x.shape, 1)
    return jnp.where(lane % 2 == 0, pltpu.roll(x, n - 1, 1), pltpu.roll(x, 1, 1))


def _rope(p, cos, sin, name):
    t = p.shape[0]
    w = NA_WIDTH
    te = _row_tile(t)

    def body(q_ref, k_ref, v_ref, c_ref, s_ref, qo_ref, ko_ref, vo_ref):
        c, s = c_ref[...], s_ref[...]
        for hp in range(w // LANES):
            cols = slice(hp * LANES, (hp + 1) * LANES)
            q, k = q_ref[:, cols].astype(F32), k_ref[:, cols].astype(F32)
            qo_ref[:, cols] = (q * c + _swap_pairs(q) * s).astype(qo_ref.dtype)
            ko_ref[:, cols] = (k * c + _swap_pairs(k) * s).astype(ko_ref.dtype)
        vo_ref[...] = v_ref[...].astype(vo_ref.dtype)

    return _call(body, name=name, grid=(t // te,),
                 in_specs=[_rows(te, w, 0), _rows(te, w, 1), _rows(te, w, 2), _rows(te, LANES), _rows(te, LANES)],
                 out_specs=[_rows(te, w)] * 3, out_shape=[_sds((t, w), MXU)] * 3)(p, p, p, cos, sin)


def _rope_bwd(dq, dk, dv, cos, sin, name):
    t = dq.shape[0]
    te = _row_tile(t)
    n_pairs = NA_WIDTH // LANES

    def body(dq_ref, dk_ref, dv_ref, c_ref, s_ref, qo_ref, ko_ref, vo_ref):
        c, s = c_ref[...], s_ref[...]
        for hp in range(n_pairs):
            cols = slice(hp * LANES, (hp + 1) * LANES)
            a, b = dq_ref[:, cols], dk_ref[hp]
            qo_ref[:, cols] = (a * c + _swap_pairs(a * s)).astype(qo_ref.dtype)
            ko_ref[:, cols] = (b * c + _swap_pairs(b * s)).astype(ko_ref.dtype)
            vo_ref[:, cols] = dv_ref[hp].astype(vo_ref.dtype)

    pairs = pl.BlockSpec((n_pairs, te, LANES), lambda i: (0, i, 0))
    return _call(body, name=name, grid=(t // te,),
                 in_specs=[_rows(te, NA_WIDTH), pairs, pairs, _rows(te, LANES), _rows(te, LANES)],
                 out_specs=[_rows(te, NA_WIDTH)] * 3, out_shape=[_sds((t, NA_WIDTH), MXU)] * 3)(dq, dk, dv, cos, sin)


def _na_geometry(r_grid):
    rows = []
    for r0, ks in ((0, 0), (Q_ROWS, 0), (r_grid - Q_ROWS, r_grid - K_ROWS)):
        dr = np.zeros((Q_ROWS, K_ROWS), np.int32)
        vr = np.zeros((Q_ROWS, K_ROWS), bool)
        for a in range(Q_ROWS):
            r = r0 + a
            rs = min(max(r - WIN_H // 2, 0), r_grid - WIN_H)
            for i in range(K_ROWS):
                kr = ks + i
                vr[a, i] = rs <= kr <= rs + WIN_H - 1
                dr[a, i] = kr - r + WIN_H - 1
        rows.append((dr, vr))
    c = np.arange(GRID_W)
    cs = np.clip(c - WIN_W // 2, 0, GRID_W - WIN_W)
    kc = np.arange(GRID_W)
    vc = (kc[None, :] >= cs[:, None]) & (kc[None, :] <= cs[:, None] + WIN_W - 1)
    dc = kc[None, :] - c[:, None] + WIN_W - 1
    return rows, dc, vc


def _bias_table(rpb, r_grid):
    rows, _, vc = _na_geometry(r_grid)
    n_dc, off = 2 * WIN_W - 1, GRID_W - WIN_W
    u = jnp.pad(rpb, ((0, 0), (0, 0), (off, 2 * GRID_W - 1 - off - n_dc)))
    toep = jnp.stack([u[:, :, GRID_W - 1 - c:2 * GRID_W - 1 - c] for c in range(GRID_W)], axis=1)
    toep = jnp.pad(toep, ((0, 0), (0, 0), (Q_ROWS, Q_ROWS), (0, 0)))
    tabs = []
    for dr, vr in rows:
        per_row = []
        for a in range(Q_ROWS):
            lo = int(dr[a, 0]) + Q_ROWS
            blocks = [jnp.where(vc[None], toep[:, :, lo + i, :], NEG) if vr[a, i] else jnp.full((HEADS, GRID_W, GRID_W), NEG, F32)
                      for i in range(K_ROWS)]
            per_row.append(jnp.concatenate(blocks, axis=-1))
        tabs.append(jnp.stack(per_row, axis=1).reshape(HEADS, TQ, TK))
    tabs.append(jnp.full((HEADS, TQ, TK), NEG, F32))
    return jnp.stack(tabs)


def _variant(g, ngx):
    return jnp.where(g == 0, 0, jnp.where(g >= ngx, 3, jnp.where(g == ngx - 1, 2, 1)))


def _key_start(g, r_grid):
    return pl.multiple_of(jnp.clip(g * Q_ROWS - WIN_H // 2, 0, r_grid - K_ROWS) * GRID_W, TQ)


def _nt(a, b):
    return lax.dot_general(a, b, (((1,), (1,)), ((), ())), preferred_element_type=F32)


def _tn(a, b):
    return lax.dot_general(a, b, (((0,), (0,)), ((), ())), preferred_element_type=F32)


def _nn(a, b):
    return jnp.dot(a, b, preferred_element_type=F32)


def _head_mask(h):
    lane = lax.broadcasted_iota(jnp.int32, (1, LANES), 1)
    return ((lane >= HEAD_DIM * h) & (lane < HEAD_DIM * (h + 1))).astype(F32)


def _softmax_parts(qm, knb, kcx, bias):
    s_nb = _nt(qm, knb) + bias
    s_cx = _nt(qm, kcx)
    m = jnp.maximum(jnp.max(s_nb, axis=-1, keepdims=True), jnp.max(s_cx, axis=-1, keepdims=True))
    e_nb = jnp.exp(s_nb - m)
    e_cx = jnp.exp(s_cx - m)
    inv = 1.0 / (jnp.sum(e_nb, axis=-1, keepdims=True) + jnp.sum(e_cx, axis=-1, keepdims=True))
    return e_nb, e_cx, inv


def _na_specs(t, ngx):
    q_spec = pl.BlockSpec((TQ, LANES), lambda hp, g: (g, hp))
    kv_spec = pl.BlockSpec((t, LANES), lambda hp, g: (0, hp))
    b_spec = pl.BlockSpec((1, 2, TQ, TK), lambda hp, g: (_variant(g, ngx), hp, 0, 0))
    return q_spec, kv_spec, b_spec


def _na_fwd(q, k, v, bias, s, name):
    t = q.shape[0]
    ctx_len = t - s
    r_grid = s // GRID_W
    q_spec, kv_spec, b_spec = _na_specs(t, s // TQ)

    def body(q_ref, k_ref, v_ref, b_ref, o_ref):
        start = _key_start(pl.program_id(1), r_grid)
        qf = q_ref[...].astype(F32) * (HEAD_DIM ** -0.5)
        knb, vnb = k_ref[pl.ds(start, TK), :], v_ref[pl.ds(start, TK), :]
        kcx, vcx = k_ref[pl.ds(s, ctx_len), :], v_ref[pl.ds(s, ctx_len), :]
        acc = jnp.zeros((TQ, LANES), F32)
        for h in range(2):
            mask = _head_mask(h)
            e_nb, e_cx, inv = _softmax_parts((qf * mask).astype(MXU), knb, kcx, b_ref[0, h])
            acc += (_nn(e_nb.astype(MXU), vnb) + _nn(e_cx.astype(MXU), vcx)) * (inv * mask)
        o_ref[...] = acc.astype(o_ref.dtype)

    return _call(body, name=name, grid=(NA_WIDTH // LANES, t // TQ), in_specs=[q_spec, kv_spec, kv_spec, b_spec],
                 out_specs=q_spec, out_shape=_sds((t, NA_WIDTH), MXU))(q, k, v, bias)


def _na_bwd(q, k, v, do, bias, s, name):
    t = q.shape[0]
    ctx_len = t - s
    r_grid = s // GRID_W
    ng, ngx = t // TQ, s // TQ
    q_spec, kv_spec, b_spec = _na_specs(t, ngx)

    def body(q_ref, k_ref, v_ref, do_ref, b_ref, dq_ref, dk_hbm, dv_hbm, db_ref, dk_acc, dv_acc):
        hp, g = pl.program_id(0), pl.program_id(1)
        start = _key_start(g, r_grid)

        @pl.when(g == 0)
        def _():
            dk_acc[...] = jnp.zeros_like(dk_acc)
            dv_acc[...] = jnp.zeros_like(dv_acc)

        @pl.when((g == 0) | (g == 1) | (g == ngx - 1) | (g == ngx))
        def _():
            db_ref[...] = jnp.zeros_like(db_ref)

        qf = q_ref[...].astype(F32) * (HEAD_DIM ** -0.5)
        do = do_ref[...].astype(F32)
        knb, vnb = k_ref[pl.ds(start, TK), :], v_ref[pl.ds(start, TK), :]
        kcx, vcx = k_ref[pl.ds(s, ctx_len), :], v_ref[pl.ds(s, ctx_len), :]
        dq = jnp.zeros((TQ, LANES), F32)
        dk_nb = jnp.zeros((TK, LANES), F32)
        dv_nb = jnp.zeros((TK, LANES), F32)
        dk_cx = jnp.zeros((ctx_len, LANES), F32)
        dv_cx = jnp.zeros((ctx_len, LANES), F32)
        for h in range(2):
            mask = _head_mask(h)
            qm = (qf * mask).astype(MXU)
            dom = (do * mask).astype(MXU)
            e_nb, e_cx, inv = _softmax_parts(qm, knb, kcx, b_ref[0, h])
            dp_nb = _nt(dom, vnb)
            dp_cx = _nt(dom, vcx)
            delta = inv * (jnp.sum(e_nb * dp_nb, axis=-1, keepdims=True) + jnp.sum(e_cx * dp_cx, axis=-1, keepdims=True))
            ds_nb = e_nb * (inv * (dp_nb - delta))
            ds_cx = e_cx * (inv * (dp_cx - delta))
            db_ref[0, h] += ds_nb
            ds_nb, ds_cx = ds_nb.astype(MXU), ds_cx.astype(MXU)
            dq += (_nn(ds_nb, knb) + _nn(ds_cx, kcx)) * (mask * (HEAD_DIM ** -0.5))
            dk_nb += _tn(ds_nb, qm)
            dk_cx += _tn(ds_cx, qm)
            dom = (do * (inv * mask)).astype(MXU)
            dv_nb += _tn(e_nb.astype(MXU), dom)
            dv_cx += _tn(e_cx.astype(MXU), dom)
        dq_ref[...] = dq
        dk_acc[pl.ds(start, TK), :] += dk_nb
        dv_acc[pl.ds(start, TK), :] += dv_nb
        dk_acc[pl.ds(s, ctx_len), :] += dk_cx
        dv_acc[pl.ds(s, ctx_len), :] += dv_cx

        @pl.when(g == ng - 1)
        def _():
            pltpu.sync_copy(dk_acc, dk_hbm.at[hp])
            pltpu.sync_copy(dv_acc, dv_hbm.at[hp])

    n_pairs = NA_WIDTH // LANES
    hbm = pl.BlockSpec(memory_space=pl.ANY)
    return _call(body, name=name, grid=(n_pairs, ng), in_specs=[q_spec, kv_spec, kv_spec, q_spec, b_spec],
                 out_specs=[q_spec, hbm, hbm, b_spec],
                 out_shape=[_sds((t, NA_WIDTH), F32), _sds((n_pairs, t, LANES), F32), _sds((n_pairs, t, LANES), F32),
                            _sds((4, HEADS, TQ, TK), F32)],
                 scratch=[pltpu.VMEM((t, LANES), F32), pltpu.VMEM((t, LANES), F32)])(q, k, v, do, bias)


def _rpb_grad(dbias, r_grid, name):
    rows, _, _ = _na_geometry(r_grid)
    n_dr, half, skew, lanes = 2 * WIN_H - 1, WIN_W - 1, TK + 2, 896
    z = dbias[:3].reshape(3, HEADS, Q_ROWS, GRID_W, TK)
    z = jnp.pad(z, ((0, 0),) * 4 + ((0, 1),)).reshape(3, HEADS, Q_ROWS, GRID_W * (TK + 1))
    z = jnp.pad(z, ((0, 0),) * 3 + ((0, GRID_W),)).reshape(3, HEADS, Q_ROWS, GRID_W, skew)
    z = jnp.pad(z, ((0, 0),) * 4 + ((0, lanes - skew),))

    def body(z_ref, o_ref):
        sums = [jnp.sum(z_ref[v, 0, a], axis=0, keepdims=True) for v in range(3) for a in range(Q_ROWS)]
        zs = jnp.concatenate(sums + [jnp.zeros((16 - 3 * Q_ROWS, lanes), F32)], axis=0)
        acc = [jnp.zeros((1, lanes), F32) for _ in range(n_dr)]
        for i in range(K_ROWS):
            if i == 0:
                at0 = pltpu.roll(zs, lanes - (skew - half), 1) + pltpu.roll(zs, half, 1)
            else:
                at0 = pltpu.roll(zs, lanes - (i * GRID_W - half), 1)
            for v, (dr, vr) in enumerate(rows):
                for a in range(Q_ROWS):
                    if vr[a, i]:
                        acc[dr[a, i]] = acc[dr[a, i]] + at0[v * Q_ROWS + a:v * Q_ROWS + a + 1, :]
        o_ref[0] = jnp.concatenate(acc + [jnp.zeros((1, lanes), F32)], axis=0)

    o = _call(body, name=name, grid=(HEADS,),
              in_specs=[pl.BlockSpec((3, 1, Q_ROWS, GRID_W, lanes), lambda h: (0, h, 0, 0, 0))],
              out_specs=pl.BlockSpec((1, 16, lanes), lambda h: (h, 0, 0)), out_shape=_sds((HEADS, 16, lanes), F32))(z)
    return o[:, :n_dr, :2 * WIN_W - 1]


_GELU_K, _GELU_C = 0.7978845608028654, 0.044715


def _gelu(x):
    return 0.5 * x * (1.0 + jnp.tanh(_GELU_K * (x + _GELU_C * x * x * x)))


def _gelu_grad(x):
    th = jnp.tanh(_GELU_K * (x + _GELU_C * x * x * x))
    return 0.5 * (1.0 + th) + 0.5 * x * (1.0 - th * th) * (_GELU_K * (1.0 + 3.0 * _GELU_C * x * x))


def _ln_stats(v):
    mu = jnp.mean(v, axis=-1, keepdims=True)
    vc = v - mu
    rstd = lax.rsqrt(jnp.mean(vc * vc, axis=-1, keepdims=True) + EPS)
    return vc * rstd, rstd


def _gmlp(p, ln_g, ln_b, w_s, b_s, name):
    t = p.shape[0]
    te = _row_tile(t)
    w = SG_WIDTH
    cw = w // SG_GROUPS

    def body(u_ref, v_ref, g_ref, b_ref, ws_ref, bs_ref, o_ref):
        xhat, _ = _ln_stats(_gelu(v_ref[...].astype(F32)))
        vn = (xhat * g_ref[...] + b_ref[...]).astype(MXU)
        ug = _gelu(u_ref[...].astype(F32))
        for ci in range(te // SG_CHUNK):
            rs = slice(ci * SG_CHUNK, (ci + 1) * SG_CHUNK)
            for gi in range(SG_GROUPS):
                cs = slice(gi * cw, (gi + 1) * cw)
                sg = _nn(ws_ref[gi].astype(MXU), vn[rs, cs]) + bs_ref[gi]
                o_ref[rs, cs] = (ug[rs, cs] * sg).astype(o_ref.dtype)

    return _call(body, name=name, grid=(t // te,),
                 in_specs=[_rows(te, w, 3), _rows(te, w, 4), _fixed((1, w)), _fixed((1, w)),
                           _fixed((SG_GROUPS, SG_CHUNK, SG_CHUNK)), _fixed((SG_GROUPS, SG_CHUNK, 1))],
                 out_specs=_rows(te, w), out_shape=_sds((t, w), MXU))(p, p, ln_g, ln_b, w_s, b_s)


def _gmlp_bwd(p, dob, ln_g, ln_b, w_s, b_s, name):
    t = p.shape[0]
    te = _row_tile(t)
    w = SG_WIDTH
    cw = w // SG_GROUPS

    def body(u_ref, v_ref, do_ref, g_ref, b_ref, ws_ref, bs_ref, du_ref, dv_ref, dws_ref, dbs_ref, dg_ref, db_ref, dvn_ref):
        @pl.when(pl.program_id(0) == 0)
        def _():
            dws_ref[...] = jnp.zeros_like(dws_ref)
            dbs_ref[...] = jnp.zeros_like(dbs_ref)
            dg_ref[...] = jnp.zeros_like(dg_ref)
            db_ref[...] = jnp.zeros_like(db_ref)

        u, v = u_ref[...].astype(F32), v_ref[...].astype(F32)
        xhat, rstd = _ln_stats(_gelu(v))
        vn = (xhat * g_ref[...] + b_ref[...]).astype(MXU)
        ug = _gelu(u)
        dob = do_ref[...].astype(F32)
        for ci in range(te // SG_CHUNK):
            rs = slice(ci * SG_CHUNK, (ci + 1) * SG_CHUNK)
            for gi in range(SG_GROUPS):
                cs = slice(gi * cw, (gi + 1) * cw)
                wsg = ws_ref[gi].astype(MXU)
                sg = _nn(wsg, vn[rs, cs]) + bs_ref[gi]
                du_ref[rs, cs] = (dob[rs, cs] * sg * _gelu_grad(u[rs, cs])).astype(du_ref.dtype)
                ds = dob[rs, cs] * ug[rs, cs]
                dbs_ref[gi] += jnp.sum(ds, axis=-1, keepdims=True)
                ds = ds.astype(MXU)
                dws_ref[gi] += _nt(ds, vn[rs, cs])
                dvn_ref[rs, cs] = _tn(wsg, ds)
        dvn = dvn_ref[...]
        dg_ref[...] += jnp.sum(dvn * xhat, axis=0, keepdims=True)
        db_ref[...] += jnp.sum(dvn, axis=0, keepdims=True)
        dxh = dvn * g_ref[...]
        dvg = rstd * (dxh - jnp.mean(dxh, axis=-1, keepdims=True) - xhat * jnp.mean(dxh * xhat, axis=-1, keepdims=True))
        dv_ref[...] = (dvg * _gelu_grad(v)).astype(dv_ref.dtype)

    return _call(body, name=name, grid=(t // te,),
                 in_specs=[_rows(te, w, 3), _rows(te, w, 4), _rows(te, w), _fixed((1, w)), _fixed((1, w)),
                           _fixed((SG_GROUPS, SG_CHUNK, SG_CHUNK)), _fixed((SG_GROUPS, SG_CHUNK, 1))],
                 out_specs=[_rows(te, w), _rows(te, w), _fixed((SG_GROUPS, SG_CHUNK, SG_CHUNK)),
                            _fixed((SG_GROUPS, SG_CHUNK, 1)), _fixed((1, w)), _fixed((1, w))],
                 out_shape=[_sds((t, w), MXU), _sds((t, w), MXU), _sds((SG_GROUPS, SG_CHUNK, SG_CHUNK), F32),
                            _sds((SG_GROUPS, SG_CHUNK, 1), F32), _sds((1, w), F32), _sds((1, w), F32)],
                 scratch=[pltpu.VMEM((te, w), F32)])(p, p, dob, ln_g, ln_b, w_s, b_s)


def _merge(pa, pb, p, b_gate, name):
    t, d = pa.shape
    te = _row_tile(t)
    hw = NA_WIDTH
    nh = d // hw
    c0 = (NA_WIDTH * 3 + SG_WIDTH * 2) // hw

    def body(pa_ref, pb_ref, la_ref, lb_ref, ba_ref, bb_ref, o_ref):
        ga = jax.nn.sigmoid(la_ref[...].astype(F32) + ba_ref[...])
        gb = jax.nn.sigmoid(lb_ref[...].astype(F32) + bb_ref[...])
        o_ref[...] = (ga * pa_ref[...].astype(F32) + gb * pb_ref[...].astype(F32)).astype(o_ref.dtype)

    tile = pl.BlockSpec((te, hw), lambda i, j: (i, j))
    return _call(body, name=name, grid=(t // te, nh),
                 in_specs=[tile, tile, pl.BlockSpec((te, hw), lambda i, j: (i, c0 + j)),
                           pl.BlockSpec((te, hw), lambda i, j: (i, c0 + nh + j)),
                           pl.BlockSpec((1, hw), lambda i, j: (0, j)), pl.BlockSpec((1, hw), lambda i, j: (0, nh + j))],
                 out_specs=tile, out_shape=_sds((t, d), MXU))(pa, pb, p, p, b_gate, b_gate)


def _merge_bwd(dmg, pa, pb, p, b_gate, name):
    t, d = pa.shape
    te = _row_tile(t)
    hw = NA_WIDTH
    nh = d // hw
    c0 = (NA_WIDTH * 3 + SG_WIDTH * 2) // hw

    def body(dm_ref, pa_ref, pb_ref, la_ref, lb_ref, ba_ref, bb_ref, dpa_ref, dpb_ref, dla_ref, dlb_ref, dba_ref, dbb_ref):
        @pl.when(pl.program_id(1) == 0)
        def _():
            dba_ref[...] = jnp.zeros_like(dba_ref)
            dbb_ref[...] = jnp.zeros_like(dbb_ref)

        dm = dm_ref[...].astype(F32)
        ga = jax.nn.sigmoid(la_ref[...].astype(F32) + ba_ref[...])
        gb = jax.nn.sigmoid(lb_ref[...].astype(F32) + bb_ref[...])
        dpa_ref[...] = (dm * ga).astype(dpa_ref.dtype)
        dpb_ref[...] = (dm * gb).astype(dpb_ref.dtype)
        dla = dm * pa_ref[...].astype(F32) * ga * (1.0 - ga)
        dlb = dm * pb_ref[...].astype(F32) * gb * (1.0 - gb)
        dla_ref[...] = dla.astype(dla_ref.dtype)
        dlb_ref[...] = dlb.astype(dlb_ref.dtype)
        dba_ref[...] += jnp.sum(dla, axis=0, keepdims=True)
        dbb_ref[...] += jnp.sum(dlb, axis=0, keepdims=True)

    tile = pl.BlockSpec((te, hw), lambda j, i: (i, j))
    bias_a = pl.BlockSpec((1, hw), lambda j, i: (0, j))
    bias_b = pl.BlockSpec((1, hw), lambda j, i: (0, nh + j))
    return _call(body, name=name, grid=(nh, t // te),
                 in_specs=[tile, tile, tile, pl.BlockSpec((te, hw), lambda j, i: (i, c0 + j)),
                           pl.BlockSpec((te, hw), lambda j, i: (i, c0 + nh + j)), bias_a, bias_b],
                 out_specs=[tile, tile, tile, tile, bias_a, bias_a],
                 out_shape=[_sds((t, d), MXU)] * 4 + [_sds((1, d), F32)] * 2)(dmg, pa, pb, p, p, b_gate, b_gate)


def _final(xs, tgt, g, name):
    t, d = xs.shape
    nx = tgt.shape[0] // TM

    def body(x_ref, t_ref, g_ref, l_ref, dx_ref, dg_ref):
        i = pl.program_id(0)

        @pl.when(i == 0)
        def _():
            l_ref[...] = jnp.zeros_like(l_ref)
            dg_ref[...] = jnp.zeros_like(dg_ref)

        @pl.when(i < nx)
        def _():
            x = x_ref[...]
            rstd = lax.rsqrt(jnp.mean(x * x, axis=-1, keepdims=True) + EPS)
            xhat = x * rstd
            err = xhat * g_ref[...] - t_ref[...]
            l_ref[...] += 0.5 * jnp.sum(jnp.mean(err * err, axis=-1, keepdims=True))
            dy = err * (1.0 / d)
            dg_ref[...] += jnp.sum(dy * xhat, axis=0, keepdims=True)
            dxh = dy * g_ref[...]
            dx_ref[...] = rstd * (dxh - xhat * jnp.mean(dxh * xhat, axis=-1, keepdims=True))

        @pl.when(i >= nx)
        def _():
            dx_ref[...] = jnp.zeros_like(dx_ref)

    return _call(body, name=name, grid=(t // TM,),
                 in_specs=[_rows(TM, d), pl.BlockSpec((TM, d), lambda i: (jnp.minimum(i, nx - 1), 0)), _fixed((1, d))],
                 out_specs=[_fixed((1, LANES)), _rows(TM, d), _fixed((1, d))],
                 out_shape=[_sds((1, LANES), F32), _sds((t, d), F32), _sds((1, d), F32)])(xs, tgt, g)


def _view2d(a):
    return a.reshape(1, -1) if a.ndim == 1 else a.reshape(-1, a.shape[-1])


def _tile_rows(r, c):
    for cand in (1024, 512, 256, 128, 64, 32, 16):
        if r % cand == 0 and cand * c * 4 <= 2 ** 20:
            return cand
    return r


def _pair_sum(g, recv, layer, name):
    _, a, b = g.shape
    tr = _tile_rows(a, b)

    def body(l_ref, g_ref, r_ref, o32_ref, o16_ref):
        acc = g_ref[...] + r_ref[...]
        o32_ref[...] = acc
        o16_ref[...] = acc.astype(o16_ref.dtype)

    first = pl.BlockSpec((None, tr, b), lambda i, l: (0, i, 0))
    return pl.pallas_call(
        body, name=name, out_shape=[_sds((1, a, b), F32), _sds((1, a, b), MXU)],
        grid_spec=pltpu.PrefetchScalarGridSpec(
            num_scalar_prefetch=1, grid=(a // tr,),
            in_specs=[pl.BlockSpec((None, tr, b), lambda i, l: (l[0], i, 0)), first], out_specs=[first, first]),
        compiler_params=pltpu.CompilerParams(dimension_semantics=("arbitrary",), vmem_limit_bytes=VMEM_LIMIT))(layer, g, recv)


def _ew(fn, arrays, out_dtypes, name):
    shape = arrays[0].shape
    views = [_view2d(a) for a in arrays]
    r, c = views[0].shape
    tr = _tile_rows(r, c)

    def body(*refs):
        outs = fn(*[ref[...] for ref in refs[:len(views)]])
        for ref, o in zip(refs[len(views):], outs):
            ref[...] = o.astype(ref.dtype)

    res = _call(body, name=name, grid=(r // tr,), in_specs=[_rows(tr, c)] * len(views), out_specs=[_rows(tr, c)] * len(out_dtypes),
                out_shape=[_sds((r, c), dt) for dt in out_dtypes])(*views)
    return [o.reshape(shape) for o in res]


def _sum_pieces(pieces, name, out_dtypes=(F32,)):
    def fn(*vals):
        acc = vals[0].astype(F32)
        for v in vals[1:]:
            acc = acc + v.astype(F32)
        return (acc,) * len(out_dtypes)

    return _ew(fn, pieces, list(out_dtypes), name)


def _adam_update(w, g, m, v):
    m2 = ADAM_B1 * m + (1.0 - ADAM_B1) * g
    v2 = ADAM_B2 * v + (1.0 - ADAM_B2) * (g * g)
    m_hat = m2 / (1.0 - ADAM_B1 ** ADAM_STEP)
    v_hat = v2 / (1.0 - ADAM_B2 ** ADAM_STEP)
    delta = -ADAM_LR * (m_hat / (jnp.sqrt(v_hat) + ADAM_EPS) + ADAM_WD * w)
    return g, delta, m2, v2


def _adamw(w, g_pieces, m, v, name):
    n_g = len(g_pieces)

    def fn(w_, *rest):
        g = rest[0]
        for piece in rest[1:n_g]:
            g = g + piece
        return _adam_update(w_, g, rest[n_g], rest[n_g + 1])

    return _ew(fn, [w, *g_pieces, m, v], [F32] * 4, name)


def _adamw_layers(w, mine, other, m, v, layer, name):
    _, a, b = w.shape
    tr = _tile_rows(a, b)
    nb = a // tr

    def body(l_ref, w_ref, mine_ref, other_ref, m_ref, v_ref, *o_refs):
        g = jnp.where(pl.program_id(0) // nb == l_ref[0], mine_ref[...], other_ref[...])
        for ref, val in zip(o_refs, _adam_update(w_ref[...], g, m_ref[...], v_ref[...])):
            ref[...] = val

    both = pl.BlockSpec((None, tr, b), lambda i, l: (i // nb, i % nb, 0))
    one = pl.BlockSpec((None, tr, b), lambda i, l: (0, i % nb, 0))
    return pl.pallas_call(
        body, name=name, out_shape=[_sds(w.shape, F32)] * 4,
        grid_spec=pltpu.PrefetchScalarGridSpec(num_scalar_prefetch=1, grid=(2 * nb,), in_specs=[both, one, one, both, both],
                                               out_specs=[both] * 4),
        compiler_params=pltpu.CompilerParams(dimension_semantics=("arbitrary",), vmem_limit_bytes=VMEM_LIMIT))(
            layer, w, mine, other, m, v)


def _place_shard(full, shard, chip, axis, name):
    _, a, b = shard.shape
    tr = _tile_rows(a, b)
    nb = a // tr

    def body(c_ref, s_ref, f_ref, o_ref):
        o_ref[...] = s_ref[...]

    if axis == 1:
        out_spec = pl.BlockSpec((None, tr, b), lambda l, i, c: (l, c[0] * nb + i, 0))
    else:
        out_spec = pl.BlockSpec((None, tr, b), lambda l, i, c: (l, i, c[0]))
    return pl.pallas_call(
        body, name=name, out_shape=_sds(full.shape, full.dtype), input_output_aliases={2: 0},
        grid_spec=pltpu.PrefetchScalarGridSpec(
            num_scalar_prefetch=1, grid=(2, nb),
            in_specs=[pl.BlockSpec((None, tr, b), lambda l, i, c: (l, i, 0)), pl.BlockSpec(memory_space=pl.ANY)],
            out_specs=out_spec),
        compiler_params=pltpu.CompilerParams(dimension_semantics=("arbitrary", "arbitrary"), vmem_limit_bytes=VMEM_LIMIT))(
            chip, shard, full)


def _ada_fwd(cond, w, b, name):
    r, d = cond.shape
    n = w.shape[1]
    tn = _pick(n, (1152, 768, 512, 384, 256, 128))

    def body(c_ref, w_ref, b_ref, o_ref, s_ref):
        c = c_ref[...]
        sc = c * jax.nn.sigmoid(c)
        s_ref[...] = sc
        o_ref[...] = _nn(sc.astype(MXU), w_ref[...].astype(MXU)) + b_ref[...]

    return _call(body, name=name, grid=(n // tn,),
                 in_specs=[_fixed((r, d)), pl.BlockSpec((d, tn), lambda j: (0, j)), pl.BlockSpec((1, tn), lambda j: (0, j))],
                 out_specs=[pl.BlockSpec((r, tn), lambda j: (0, j)), _fixed((r, d))],
                 out_shape=[_sds((r, n), F32), _sds((r, d), F32)])(cond, w, b)


def _cctx_grad(parts, c_ctx, name):
    n, d = parts.shape

    def body(p_ref, c_ref, o_ref):
        c = c_ref[...]
        sg = jax.nn.sigmoid(c)
        acc = p_ref[0:1, :]
        for j in range(1, n):
            acc = acc + p_ref[j:j + 1, :]
        o_ref[...] = acc * (sg * (1.0 + c * (1.0 - sg)))

    return _call(body, name=name, grid=(1,), in_specs=[_fixed((n, d)), _fixed((1, d))], out_specs=_fixed((1, d)),
                 out_shape=_sds((1, d), F32))(parts, c_ctx)


def _here():
    return lax.axis_index("x"), lax.axis_index("y"), lax.axis_index("c")


def _flip(v, bit):
    return 1 - v if bit else v


def _allgather8(xb, name):
    r, n = xb.shape

    def body(x_ref, out_ref, send_sems, recv_sems, local_sem):
        x, y, c = _here()
        me = 4 * x + 2 * y + c
        local = pltpu.make_async_copy(x_ref, out_ref.at[me], local_sem)
        local.start()
        sends = []
        for k in range(1, 8):
            peer = (_flip(x, k & 4), _flip(y, k & 2), _flip(c, k & 1))
            cp = pltpu.make_async_remote_copy(src_ref=x_ref, dst_ref=out_ref.at[me], send_sem=send_sems.at[k - 1],
                                              recv_sem=recv_sems.at[k - 1], device_id=peer, device_id_type=MESH)
            cp.start()
            sends.append(cp)
        for k in range(1, 8):
            peer = (_flip(x, k & 4), _flip(y, k & 2), _flip(c, k & 1))
            src = 4 * peer[0] + 2 * peer[1] + peer[2]
            pltpu.make_async_remote_copy(src_ref=x_ref, dst_ref=out_ref.at[src], send_sem=send_sems.at[k - 1],
                                         recv_sem=recv_sems.at[k - 1], device_id=peer, device_id_type=MESH).wait_recv()
        for cp in sends:
            cp.wait_send()
        local.wait()

    vmem = pl.BlockSpec(memory_space=pltpu.VMEM)
    return pl.pallas_call(
        body, name=name, out_shape=_sds((8, r, n), xb.dtype), in_specs=[vmem], out_specs=vmem,
        scratch_shapes=[pltpu.SemaphoreType.DMA((7,)), pltpu.SemaphoreType.DMA((7,)), pltpu.SemaphoreType.DMA(())],
        compiler_params=pltpu.CompilerParams(vmem_limit_bytes=VMEM_LIMIT))(xb)


def _shard_of(ref, axis, j, size):
    sl = pl.ds(j * size, size)
    return ref.at[:, sl, :] if axis == 1 else ref.at[:, :, sl]


def _piece(ref, axis, j, size, layer):
    lay, sl = pl.ds(layer, 1), pl.ds(j * size, size)
    return ref.at[lay, sl, :] if axis == 1 else ref.at[lay, :, sl]


def _gather_chips(shards, axes, name):
    n = len(shards)
    fulls = []
    for a, ax in zip(shards, axes):
        assert a.shape[0] == 2
        shp = list(a.shape)
        shp[ax] *= 4
        fulls.append(_sds(tuple(shp), a.dtype))

    def body(*refs):
        ins, outs = refs[:n], refs[n:2 * n]
        ici_send, ici_recv, d2d_send, d2d_recv = refs[2 * n:]
        x, y, c = _here()
        chips = [(_flip(x, k & 2), _flip(y, k & 1)) for k in range(1, 4)]
        sends = []
        for a in range(n):
            size = ins[a].shape[axes[a]]
            for j, (px, py) in enumerate(chips):
                cp = pltpu.make_async_remote_copy(src_ref=ins[a].at[pl.ds(c, 1)], dst_ref=_piece(outs[a], axes[a], 2 * x + y, size, c),
                                                  send_sem=ici_send.at[3 * a + j], recv_sem=ici_recv.at[3 * a + j],
                                                  device_id=(px, py, c), device_id_type=MESH)
                cp.start()
                sends.append(cp)
        for a in range(n):
            size = ins[a].shape[axes[a]]
            for j, (px, py) in enumerate(chips):
                landed = _piece(outs[a], axes[a], 2 * px + py, size, c)
                pltpu.make_async_remote_copy(src_ref=ins[a].at[pl.ds(c, 1)], dst_ref=landed, send_sem=ici_send.at[3 * a + j],
                                             recv_sem=ici_recv.at[3 * a + j], device_id=(px, py, c), device_id_type=MESH).wait_recv()
                cp = pltpu.make_async_remote_copy(src_ref=landed, dst_ref=landed, send_sem=d2d_send.at[3 * a + j],
                                                  recv_sem=d2d_recv.at[3 * a + j], device_id=(x, y, 1 - c), device_id_type=MESH)
                cp.start()
                sends.append(cp)
        for a in range(n):
            size = ins[a].shape[axes[a]]
            for j, (px, py) in enumerate(chips):
                passed = _piece(outs[a], axes[a], 2 * px + py, size, 1 - c)
                pltpu.make_async_remote_copy(src_ref=passed, dst_ref=passed, send_sem=d2d_send.at[3 * a + j],
                                             recv_sem=d2d_recv.at[3 * a + j], device_id=(x, y, 1 - c), device_id_type=MESH).wait_recv()
        for cp in sends:
            cp.wait_send()

    hbm = pl.BlockSpec(memory_space=pl.ANY)
    return pl.pallas_call(
        body, name=name, out_shape=fulls, in_specs=[hbm] * n, out_specs=[hbm] * n,
        scratch_shapes=[pltpu.SemaphoreType.DMA((3 * n,))] * 4)(*shards)


def _swap_layers(arrays, name):
    n = len(arrays)

    def body(*refs):
        ins, outs = refs[:n], refs[n:2 * n]
        send_sems, recv_sems = refs[2 * n:]
        x, y, c = _here()
        copies = []
        for a in range(n):
            cp = pltpu.make_async_remote_copy(src_ref=ins[a].at[pl.ds(1 - c, 1)], dst_ref=outs[a], send_sem=send_sems.at[a],
                                              recv_sem=recv_sems.at[a], device_id=(x, y, 1 - c), device_id_type=MESH)
            cp.start()
            copies.append(cp)
        for cp in copies:
            cp.wait()

    hbm = pl.BlockSpec(memory_space=pl.ANY)
    return pl.pallas_call(
        body, name=name, out_shape=[_sds((1, *a.shape[1:]), a.dtype) for a in arrays], in_specs=[hbm] * n, out_specs=[hbm] * n,
        scratch_shapes=[pltpu.SemaphoreType.DMA((n,)), pltpu.SemaphoreType.DMA((n,))])(*arrays)


def _scatter_chips(fulls, axes, name):
    n = len(fulls)
    recvs = []
    for a, ax in zip(fulls, axes):
        shp = list(a.shape)
        shp[ax] //= 4
        recvs.append(_sds((3, *shp), a.dtype))

    def body(*refs):
        ins, outs = refs[:n], refs[n:2 * n]
        send_sems, recv_sems = refs[2 * n:]
        x, y, c = _here()
        sends = []
        for a in range(n):
            size = ins[a].shape[axes[a]] // 4
            for k in range(1, 4):
                peer = (_flip(x, k & 2), _flip(y, k & 1), c)
                cp = pltpu.make_async_remote_copy(src_ref=_shard_of(ins[a], axes[a], 2 * peer[0] + peer[1], size),
                                                  dst_ref=outs[a].at[k - 1],
                                                  send_sem=send_sems.at[3 * a + k - 1], recv_sem=recv_sems.at[3 * a + k - 1],
                                                  device_id=peer, device_id_type=MESH)
                cp.start()
                sends.append(cp)
        for cp in sends:
            cp.wait_recv()
        for cp in sends:
            cp.wait_send()

    hbm = pl.BlockSpec(memory_space=pl.ANY)
    return pl.pallas_call(
        body, name=name, out_shape=recvs, in_specs=[hbm] * n, out_specs=[hbm] * n,
        scratch_shapes=[pltpu.SemaphoreType.DMA((3 * n,)), pltpu.SemaphoreType.DMA((3 * n,))])(*fulls)


def _sibling_swap(arrays, name):
    n = len(arrays)

    def body(*refs):
        ins, outs = refs[:n], refs[n:2 * n]
        send_sems, recv_sems = refs[2 * n:]
        x, y, c = _here()
        copies = []
        for a in range(n):
            cp = pltpu.make_async_remote_copy(src_ref=ins[a], dst_ref=outs[a], send_sem=send_sems.at[a], recv_sem=recv_sems.at[a],
                                              device_id=(x, y, 1 - c), device_id_type=MESH)
            cp.start()
            copies.append(cp)
        for cp in copies:
            cp.wait()

    hbm = pl.BlockSpec(memory_space=pl.ANY)
    return pl.pallas_call(
        body, name=name, out_shape=[_sds(a.shape, a.dtype) for a in arrays], in_specs=[hbm] * n, out_specs=[hbm] * n,
        scratch_shapes=[pltpu.SemaphoreType.DMA((n,)), pltpu.SemaphoreType.DMA((n,))])(*arrays)


def _ffn_fwd(xs, g, mods, k0, w_up, w_down, s, tag):
    h, ua, ub, act = _norm_mm(xs, g, mods, k0, k0 + 1, w_up, s, tag + "_up", True)
    y, xn = _mm_res(act, w_down, xs, mods, k0 + 2, 0.5, s, tag + "_down")
    return xn, (xs, h, ua, ub, act, y)


def _dw(gw, key, a, b, name, col0=0, n_total=None):
    shape = (gw["depth"], a.shape[1], n_total or b.shape[1])
    gw[key] = _mm(a, b, "tn", F32, name, into=(gw.get(key), shape, gw["layer"], col0))


def _ffn_bwd(dxn, saved, g, mods, k0, w_up, w_down, s, tag, gw, up_key, down_key):
    xs, h, ua, ub, act, y = saved
    dy, dgate, dua, dub = _resb_mm(dxn, y, mods, k0 + 2, 0.5, w_down, s, tag + "_down_dx", (ua, ub))
    _dw(gw, down_key, act, dy, tag + "_down_dw")
    f = dua.shape[1]
    _dw(gw, up_key, h, dua, tag + "_upa_dw", 0, 2 * f)
    _dw(gw, up_key, h, dub, tag + "_upb_dw", f, 2 * f)
    dx, dsh, dsc, dg = _mm_normb([dua, dub], w_up, xs, dxn, g, mods, k0 + 1, s, tag + "_up_dx")
    return dx, dg, [dsh, dsc, dgate]


def _mix_fwd(xs, g, mods, wl, pl_, tabs, s, tag):
    cos, sin = tabs
    h, p = _norm_mm(xs, g, mods, 3, 4, wl["w_in"], s, tag + "_in", False)
    q, k, v = _rope(p, cos, sin, tag + "_rope")
    bias = _bias_table(pl_["rpb"], s // GRID_W)
    oa = _na_fwd(q, k, v, bias, s, tag + "_na")
    ob = _gmlp(p, pl_["ln_v_g"], pl_["ln_v_b"], pl_["w_s"], pl_["b_s"], tag + "_sg")
    pa = _mm(oa, wl["w_pa"], "nn", MXU, tag + "_pa")
    pb = _mm(ob, wl["w_pb"], "nn", MXU, tag + "_pb")
    mg = _merge(pa, pb, p, pl_["b_gate"], tag + "_merge")
    y, xn = _mm_res(mg, wl["w_o"], xs, mods, 5, 1.0, s, tag + "_o")
    return xn, (xs, h, p, q, k, v, bias, oa, ob, pa, pb, mg, y)


def _mix_bwd(dxn, saved, g, mods, wl, pl_, tabs, s, tag, gw):
    xs, h, p, q, k, v, bias, oa, ob, pa, pb, mg, y = saved
    cos, sin = tabs
    gp = {}
    dy, dgate, dmg = _resb_mm(dxn, y, mods, 5, 1.0, wl["w_o"], s, tag + "_o_dx")
    _dw(gw, "w_o", mg, dy, tag + "_o_dw")
    dpa, dpb, dla, dlb, dba, dbb = _merge_bwd(dmg, pa, pb, p, pl_["b_gate"], tag + "_merge_b")
    gp["b_gate"] = jnp.concatenate([dba, dbb], axis=1)
    _dw(gw, "w_pa", oa, dpa, tag + "_pa_dw")
    doa = _mm(dpa, wl["w_pa"], "nt", MXU, tag + "_pa_dx")
    _dw(gw, "w_pb", ob, dpb, tag + "_pb_dw")
    dob = _mm(dpb, wl["w_pb"], "nt", MXU, tag + "_pb_dx")
    du, dvs, gp["w_s"], gp["b_s"], gp["ln_v_g"], gp["ln_v_b"] = _gmlp_bwd(
        p, dob, pl_["ln_v_g"], pl_["ln_v_b"], pl_["w_s"], pl_["b_s"], tag + "_sg_b")
    dqr, dkr, dv, dbias = _na_bwd(q, k, v, doa, bias, s, tag + "_na_b")
    gp["rpb"] = _rpb_grad(dbias, s // GRID_W, tag + "_rpb")
    dq, dk, dvv = _rope_bwd(dqr, dkr, dv, cos, sin, tag + "_rope_b")
    dp = [dq, dk, dvv, du, dvs, dla, dlb]
    n_in = sum(piece.shape[1] for piece in dp)
    for j, piece in enumerate(dp):
        _dw(gw, "w_in", h, piece, tag + f"_in_dw{j}", sum(q.shape[1] for q in dp[:j]), n_in)
    dx, dsh, dsc, dg = _mm_normb(dp, wl["w_in"], xs, dxn, g, mods, 4, s, tag + "_in_dx")
    return dx, gp, dg, [dsh, dsc, dgate]


def _local_step(x, ctx, tgt, mods, wts, prm):
    s, d = x.shape
    depth = mods.shape[0]
    tabs = _rope_tables(s, ctx.shape[0])
    xs = jnp.concatenate([x, ctx], axis=0)
    saved = []
    for l in range(depth):
        wl = {k: (v, l) for k, v in wts.items()}
        pl_ = _layer_params(prm, l)
        xs, s1 = _ffn_fwd(xs, pl_["g"][0], mods[l], 0, wl["w_ff1_up"], wl["w_ff1_down"], s, f"l{l}_ff1")
        xs, s2 = _mix_fwd(xs, pl_["g"][1], mods[l], wl, pl_, tabs, s, f"l{l}_mix")
        xs, s3 = _ffn_fwd(xs, pl_["g"][2], mods[l], 6, wl["w_ff2_up"], wl["w_ff2_down"], s, f"l{l}_ff2")
        saved.append((s1, s2, s3))
    loss, dxs, d_final_g = _final(xs, tgt, prm["final_g"].reshape(1, d), "final")
    gw = {"depth": depth}
    gp = {k: [None] * depth for k in ("norm_g", "b_gate", "rpb", "ln_v_g", "ln_v_b", "w_s", "b_s")}
    dmods = [None] * depth
    for l in reversed(range(depth)):
        wl = {k: (v, l) for k, v in wts.items()}
        pl_ = _layer_params(prm, l)
        s1, s2, s3 = saved[l]
        gw["layer"] = l
        dxs, dg2, dm2 = _ffn_bwd(dxs, s3, pl_["g"][2], mods[l], 6, wl["w_ff2_up"], wl["w_ff2_down"], s, f"l{l}_ff2",
                                 gw, "w_ff2_up", "w_ff2_down")
        dxs, gpm, dg1, dm1 = _mix_bwd(dxs, s2, pl_["g"][1], mods[l], wl, pl_, tabs, s, f"l{l}_mix", gw)
        dxs, dg0, dm0 = _ffn_bwd(dxs, s1, pl_["g"][0], mods[l], 0, wl["w_ff1_up"], wl["w_ff1_down"], s, f"l{l}_ff1",
                                 gw, "w_ff1_up", "w_ff1_down")
        gp["b_gate"][l] = gpm["b_gate"][0]
        gp["rpb"][l] = gpm["rpb"]
        gp["ln_v_g"][l] = gpm["ln_v_g"][0]
        gp["ln_v_b"][l] = gpm["ln_v_b"][0]
        gp["w_s"][l] = gpm["w_s"]
        gp["b_s"][l] = gpm["b_s"][..., 0]
        gp["norm_g"][l] = jnp.concatenate([dg0, dg1, dg2], axis=0)
        dmods[l] = jnp.concatenate(dm0 + dm1 + dm2, axis=1)
    gw = {k: gw[k] for k in wts}
    gp = {k: jnp.stack(v) for k, v in gp.items()}
    gp["final_g"] = d_final_g[0]
    return loss[0, 0], dxs[:s], jnp.stack(dmods), gw, gp


def _layer_params(prm, l):
    d = prm["norm_g"].shape[-1]
    return {
        "g": [prm["norm_g"][l, i].reshape(1, d) for i in range(3)],
        "b_gate": prm["b_gate"][l].reshape(1, -1),
        "rpb": prm["rpb"][l],
        "ln_v_g": prm["ln_v_g"][l].reshape(1, -1),
        "ln_v_b": prm["ln_v_b"][l].reshape(1, -1),
        "w_s": prm["w_s"][l],
        "b_s": prm["b_s"][l][..., None],
    }


SMALL = ("norm_g", "b_gate", "rpb", "ln_v_g", "ln_v_b", "w_s", "b_s", "final_g")
PACK_LANES = 1024


def _pack(parts):
    flat = jnp.concatenate([p.reshape(-1) for p in parts])
    rows = -(-flat.shape[0] // PACK_LANES)
    rows = -(-rows // 8) * 8
    return jnp.pad(flat, (0, rows * PACK_LANES - flat.shape[0])).reshape(rows, PACK_LANES)


def _unpack(flat, shapes):
    out, off = [], 0
    for shp in shapes:
        n = int(np.prod(shp))
        out.append(flat[..., off:off + n].reshape(*flat.shape[:-1], *shp))
        off += n
    return out


def kernel(x, c, ctx, c_ctx, w_ada, b_ada, norm_g, w_ff1_up, w_ff1_down, w_in, b_gate, rpb, ln_v_g, ln_v_b, w_s, b_s, w_pa, w_pb, w_o, w_ff2_up, w_ff2_down, final_g, loss_target, m_c_ctx, m_w_ada, m_b_ada, m_norm_g, m_w_ff1_up, m_w_ff1_down, m_w_in, m_b_gate, m_rpb, m_ln_v_g, m_ln_v_b, m_w_s, m_b_s, m_w_pa, m_w_pb, m_w_o, m_w_ff2_up, m_w_ff2_down, m_final_g, v_c_ctx, v_w_ada, v_b_ada, v_norm_g, v_w_ff1_up, v_w_ff1_down, v_w_in, v_b_gate, v_rpb, v_ln_v_g, v_ln_v_b, v_w_s, v_b_s, v_w_pa, v_w_pb, v_w_o, v_w_ff2_up, v_w_ff2_down, v_final_g):
    weights = dict(c_ctx=c_ctx, w_ada=w_ada, b_ada=b_ada, norm_g=norm_g, w_ff1_up=w_ff1_up, w_ff1_down=w_ff1_down, w_in=w_in,
                   b_gate=b_gate, rpb=rpb, ln_v_g=ln_v_g, ln_v_b=ln_v_b, w_s=w_s, b_s=b_s, w_pa=w_pa, w_pb=w_pb, w_o=w_o,
                   w_ff2_up=w_ff2_up, w_ff2_down=w_ff2_down, final_g=final_g)
    mom_m = dict(c_ctx=m_c_ctx, w_ada=m_w_ada, b_ada=m_b_ada, norm_g=m_norm_g, w_ff1_up=m_w_ff1_up, w_ff1_down=m_w_ff1_down,
                 w_in=m_w_in, b_gate=m_b_gate, rpb=m_rpb, ln_v_g=m_ln_v_g, ln_v_b=m_ln_v_b, w_s=m_w_s, b_s=m_b_s, w_pa=m_w_pa,
                 w_pb=m_w_pb, w_o=m_w_o, w_ff2_up=m_w_ff2_up, w_ff2_down=m_w_ff2_down, final_g=m_final_g)
    mom_v = dict(c_ctx=v_c_ctx, w_ada=v_w_ada, b_ada=v_b_ada, norm_g=v_norm_g, w_ff1_up=v_w_ff1_up, w_ff1_down=v_w_ff1_down,
                 w_in=v_w_in, b_gate=v_b_gate, rpb=v_rpb, ln_v_g=v_ln_v_g, ln_v_b=v_ln_v_b, w_s=v_w_s, b_s=v_b_s, w_pa=v_w_pa,
                 w_pb=v_w_pb, w_o=v_w_o, w_ff2_up=v_w_ff2_up, w_ff2_down=v_w_ff2_down, final_g=v_final_g)
    order = list(weights)
    mx, my, mc = _here()
    dev = 4 * mx + 2 * my + mc
    chip = 2 * mx + my
    depth, d, n_ada = w_ada.shape
    dq = d // 4

    c_all = _allgather8(jnp.pad(c, ((0, 7), (0, 0))), "gather_c")[:, 0, :]
    cond = jnp.concatenate([c_all, c_ctx[None, :], jnp.zeros((7, d), F32)], axis=0)
    b_shard = lax.dynamic_slice(b_ada, (0, chip * n_ada), (depth, n_ada))
    proj = [_ada_fwd(cond, w_ada[l], b_shard[l:l + 1], f"ada{l}") for l in range(depth)]
    silu_c = proj[0][1]
    mods_sh = _allgather8(jnp.concatenate([p[0] for p in proj], axis=0), "gather_mods")
    mods_all = jnp.transpose(mods_sh[0::2].reshape(4, depth, 16, n_ada), (1, 2, 0, 3)).reshape(depth, 16, N_MOD, d)
    mods = jnp.stack([lax.dynamic_index_in_dim(mods_all, dev, axis=1, keepdims=False), mods_all[:, 8]], axis=1)

    shards = [weights[k].astype(MXU) for k in BIG]
    full = _gather_chips(shards, [SHARD_AXIS[k] for k in BIG], "gather_w")
    my_chip = jnp.reshape(chip, (1,)).astype(jnp.int32)
    full = [_place_shard(f, sh, my_chip, SHARD_AXIS[k], "place_" + k) for k, f, sh in zip(BIG, full, shards)]
    wts = dict(zip(BIG, full))
    prm = {k: weights[k] for k in SMALL if k != "norm_g"}
    norm_full = _allgather8(jnp.pad(norm_g.reshape(depth * 3, dq), ((0, 8 - depth * 3), (0, 0))), "gather_norm_g")
    prm["norm_g"] = jnp.transpose(norm_full[0::2, :depth * 3].reshape(4, depth, 3, dq), (1, 2, 0, 3)).reshape(depth, 3, d)

    loss, grad_x, dmods, gw, gp = _local_step(x[0], ctx[0], loss_target[0], mods, wts, prm)
    loss = lax.psum(loss, ("x", "y", "c"))

    small_shapes = [(depth, 2, N_MOD * d)] + [weights[k].shape if k != "norm_g" else (depth, 3, d) for k in SMALL]
    packed = _allgather8(_pack([dmods.reshape(depth, 2, N_MOD * d)] + [gp[k] for k in SMALL]), "gather_small")
    rows = packed.shape[1]
    total = _sum_pieces([packed[i] for i in range(8)], "sum_small")[0].reshape(-1)
    sums = dict(zip(("dmods",) + SMALL, _unpack(total, small_shapes)))
    dmods_dev = _unpack(packed.reshape(8, rows * PACK_LANES), small_shapes[:1])[0]

    g_ada, cc_parts = [], []
    for l in range(depth):
        dm = jnp.concatenate([dmods_dev[:, l, 0], sums["dmods"][l, 1][None], jnp.zeros((7, N_MOD * d), F32)], axis=0)
        dm_sh = lax.dynamic_slice(dm, (0, chip * n_ada), (16, n_ada))
        g_ada.append(_mm(silu_c, dm_sh, "tn", F32, f"ada{l}_dw"))
        cc_parts.append(_mm(dm_sh, w_ada[l], "nt", F32, f"ada{l}_dc")[8:9])
    cc_all = _allgather8(jnp.pad(jnp.concatenate(cc_parts, axis=0), ((0, 8 - depth), (0, 0))), "gather_cctx")
    g_cctx = _cctx_grad(cc_all[0::2, :depth].reshape(4 * depth, d), c_ctx.reshape(1, d), "cctx_grad")

    axes = [SHARD_AXIS[k] for k in BIG]
    from_sibling = _swap_layers([gw[k] for k in BIG], "swap_layer_gw")
    my_layer = jnp.reshape(mc, (1,)).astype(jnp.int32)
    pair = [_pair_sum(gw[k], r, my_layer, "pair_" + k) for k, r in zip(BIG, from_sibling)]
    recv = _scatter_chips([p[1] for p in pair], axes, "scatter_gw")
    mine = []
    for k, ax, p, r in zip(BIG, axes, pair, recv):
        size = p[0].shape[ax] // 4
        own = lax.dynamic_slice_in_dim(p[0], chip * size, size, axis=ax)
        mine.append(_sum_pieces([own, r[0], r[1], r[2]], "sum_" + k)[0])
    other = _sibling_swap(mine, "swap_gw")

    res = {k: _adamw_layers(weights[k], a, b, mom_m[k], mom_v[k], my_layer, "adamw_" + k) for k, a, b in zip(BIG, mine, other)}
    pieces = {"w_ada": [jnp.stack(g_ada)]}
    pieces["b_ada"] = [sums["dmods"][:, 0], sums["dmods"][:, 1]]
    pieces["c_ctx"] = [g_cctx[0]]
    for k in SMALL:
        pieces[k] = [sums[k]]
    pieces["norm_g"] = [lax.dynamic_slice_in_dim(sums["norm_g"], chip * dq, dq, axis=2)]
    res.update({k: _adamw(weights[k], pieces[k], mom_m[k], mom_v[k], "adamw_" + k) for k in pieces})
    return (loss, grad_x[None], *[res[k][0] for k in order], *[res[k][1] for k in order],
            *[res[k][2] for k in order], *[res[k][3] for k in order])
```

```python
import numpy as np
import jax
import jax.numpy as jnp
from jax import lax
from jax.experimental import pallas as pl
from jax.experimental.pallas import tpu as pltpu

F32 = jnp.float32
MXU = jnp.bfloat16
EPS = 1e-6
GRID_W, HEADS, HEAD_DIM = 64, 8, 64
NA_WIDTH = SG_WIDTH = 512
WIN_H, WIN_W = 8, 16
SG_CHUNK, SG_GROUPS = 128, 4
N_MOD = 9
ROPE_THETA = 10000.0
Q_ROWS, K_ROWS = 4, 12
TQ, TK = Q_ROWS * GRID_W, K_ROWS * GRID_W
TM = 256
LANES = 128
NEG = -1e30
VMEM_LIMIT = 56 * 2 ** 20
ADAM_LR, ADAM_B1, ADAM_B2, ADAM_EPS, ADAM_WD, ADAM_STEP = 0.001, 0.9, 0.999, 1e-08, 0.01, 10
MESH = pl.DeviceIdType.MESH
BIG = ("w_ff1_up", "w_ff1_down", "w_in", "w_pa", "w_pb", "w_o", "w_ff2_up", "w_ff2_down")
SHARD_AXIS = {"w_ff1_up": 2, "w_ff1_down": 1, "w_in": 2, "w_pa": 2, "w_pb": 2, "w_o": 1, "w_ff2_up": 2, "w_ff2_down": 1}


def _call(body, *, name, grid, in_specs, out_specs, out_shape, scratch=(), aliases=None):
    return pl.pallas_call(
        body, name=name, grid=grid, in_specs=in_specs, out_specs=out_specs, out_shape=out_shape,
        scratch_shapes=list(scratch), input_output_aliases=aliases or {},
        compiler_params=pltpu.CompilerParams(dimension_semantics=("arbitrary",) * len(grid), vmem_limit_bytes=VMEM_LIMIT))


def _w_dims(w):
    return w[0].shape[1:] if isinstance(w, tuple) else w.shape


def _w_arr(w):
    return w[0] if isinstance(w, tuple) else w


def _w_spec(w, block, index):
    if isinstance(w, tuple):
        layer = w[1]
        return pl.BlockSpec((None, *block), lambda *ids: (layer, *index(*ids)))
    return pl.BlockSpec(block, index)


def _pick(n, prefs):
    for p in prefs:
        if n % p == 0:
            return p
    return n


def _row_tile(t):
    return _pick(t, (640, 256))


def _rows(tm, n, col=0):
    return pl.BlockSpec((tm, n), lambda i: (i, col))


def _fixed(shape):
    return pl.BlockSpec(shape, lambda *_: (0,) * len(shape))


def _sds(shape, dtype):
    return jax.ShapeDtypeStruct(shape, dtype)


def _mm(a, b, mode, out_dtype, name, into=None):
    if mode == "tn":
        r, m = a.shape
        n = b.shape[1]
        tm = _pick(m, (1024, 1408, 704, 512, 256, 128))
        tn = _pick(n, (512, 1408, 256, 128))
        tr = _pick(r, (1280, 640, 512, 256, 128))

        def body(a_ref, b_ref, *rest):
            o_ref = rest[-1]

            @pl.when(pl.program_id(2) == 0)
            def _():
                o_ref[...] = jnp.zeros_like(o_ref)

            o_ref[...] += lax.dot_general(a_ref[...].astype(MXU), b_ref[...].astype(MXU), (((0,), (0,)), ((), ())),
                                          preferred_element_type=F32)

        in_specs = [pl.BlockSpec((tr, tm), lambda i, j, k: (k, i)), pl.BlockSpec((tr, tn), lambda i, j, k: (k, j))]
        if into is None:
            return _call(body, name=name, grid=(m // tm, n // tn, r // tr), in_specs=in_specs,
                         out_specs=pl.BlockSpec((tm, tn), lambda i, j, k: (i, j)), out_shape=_sds((m, n), F32))(a, b)
        buf, shape, layer, col0 = into
        out_spec = pl.BlockSpec((None, tm, tn), lambda i, j, k: (layer, i, j + col0 // tn))
        if buf is None:
            return _call(body, name=name, grid=(m // tm, n // tn, r // tr), in_specs=in_specs, out_specs=out_spec,
                         out_shape=_sds(shape, F32))(a, b)
        return _call(body, name=name, grid=(m // tm, n // tn, r // tr), in_specs=in_specs + [pl.BlockSpec(memory_space=pl.ANY)],
                     out_specs=out_spec, out_shape=_sds(shape, F32), aliases={2: 0})(a, b, buf)
    m, k = a.shape
    n = _w_dims(b)[1] if mode == "nn" else _w_dims(b)[0]
    tm = _pick(m, (1280, 640, 512, 256, 128) if k <= 2816 else (640, 512, 256, 128))
    tn = _pick(n, (512, 1408, 256, 128))
    dims = (((1,), (0,)), ((), ())) if mode == "nn" else (((1,), (1,)), ((), ()))

    def body(a_ref, b_ref, o_ref):
        o_ref[...] = lax.dot_general(a_ref[...].astype(MXU), b_ref[...].astype(MXU), dims,
                                     preferred_element_type=F32).astype(o_ref.dtype)

    b_spec = _w_spec(b, (k, tn), lambda i, j: (0, j)) if mode == "nn" else _w_spec(b, (tn, k), lambda i, j: (j, 0))
    return _call(body, name=name, grid=(m // tm, n // tn), in_specs=[pl.BlockSpec((tm, k), lambda i, j: (i, 0)), b_spec],
                 out_specs=pl.BlockSpec((tm, tn), lambda i, j: (i, j)), out_shape=_sds((m, n), out_dtype))(a, _w_arr(b))


def _row_chunks(tm):
    rc = _pick(tm, (256, 128))
    return [slice(r, r + rc) for r in range(0, tm, rc)]


def _ctx_rows(i, tm, s):
    return (i * tm + lax.broadcasted_iota(jnp.int32, (tm, 1), 0)) >= s


def _mod_row(m_ref, k, ctx):
    return jnp.where(ctx, m_ref[1, k:k + 1, :], m_ref[0, k:k + 1, :])


def _stream_sums(i, tm, s, refs_and_vals):
    @pl.when((i + 1) * tm <= s)
    def _():
        for ref, val in refs_and_vals:
            ref[0] += jnp.sum(val, axis=0, keepdims=True)

    @pl.when((i + 1) * tm > s)
    def _():
        ctx = _ctx_rows(i, tm, s)
        for ref, val in refs_and_vals:
            ref[0] += jnp.sum(jnp.where(ctx, 0.0, val), axis=0, keepdims=True)
            ref[1] += jnp.sum(jnp.where(ctx, val, 0.0), axis=0, keepdims=True)


def _norm_mm(xs, g, mods, k_shift, k_scale, w, s, name, glu):
    t, d = xs.shape
    n = _w_dims(w)[1] // 2 if glu else _w_dims(w)[1]
    tm = _pick(t, (1280, 640, 256))
    tn = _pick(n, (256, 128)) if glu else _pick(n, (512, 256, 128))
    nj = n // tn

    def body(x_ref, g_ref, m_ref, *refs):
        i, j = pl.program_id(0), pl.program_id(1)
        w_refs, h_ref, o_refs = refs[:2 if glu else 1], refs[2 if glu else 1], refs[3 if glu else 2:]

        @pl.when(j == 0)
        def _():
            x = x_ref[...]
            rstd = lax.rsqrt(jnp.mean(x * x, axis=-1, keepdims=True) + EPS)
            ctx = _ctx_rows(i, tm, s)
            h = x * rstd * g_ref[...] * (1.0 + _mod_row(m_ref, k_scale, ctx)) + _mod_row(m_ref, k_shift, ctx)
            h_ref[...] = h.astype(h_ref.dtype)

        for rows in _row_chunks(tm):
            h = h_ref[rows, :]
            a = _nn(h, w_refs[0][...])
            if glu:
                b = _nn(h, w_refs[1][...])
                sg = jax.nn.sigmoid(a)
                silu = a * sg
                o_refs[0][rows, :] = (b * sg * (1.0 + a * (1.0 - sg))).astype(o_refs[0].dtype)
                o_refs[1][rows, :] = silu.astype(o_refs[1].dtype)
                o_refs[2][rows, :] = (silu * b).astype(o_refs[2].dtype)
            else:
                o_refs[0][rows, :] = a.astype(o_refs[0].dtype)

    tile = pl.BlockSpec((tm, tn), lambda i, j: (i, j))
    row = pl.BlockSpec((tm, d), lambda i, j: (i, 0))
    w_specs = [_w_spec(w, (d, tn), lambda i, j: (0, j))] + ([_w_spec(w, (d, tn), lambda i, j: (0, j + nj))] if glu else [])
    n_out = 3 if glu else 1
    return _call(body, name=name, grid=(t // tm, nj),
                 in_specs=[row, _fixed((1, d)), _fixed((2, N_MOD, d))] + w_specs,
                 out_specs=[row] + [tile] * n_out,
                 out_shape=[_sds((t, d), MXU)] + [_sds((t, n), MXU)] * n_out)(xs, g, mods, *([_w_arr(w)] * (2 if glu else 1)))


def _mm_res(a, w, xs, mods, k_gate, coef, s, name):
    t, k = a.shape
    d = _w_dims(w)[1]
    tm = _pick(t, (1280, 640, 256))
    tn = _pick(d, (512, 256, 128))

    def body(a_ref, w_ref, x_ref, m_ref, y_ref, o_ref):
        y = _nn(a_ref[...], w_ref[...])
        y_ref[...] = y.astype(y_ref.dtype)
        gate = _mod_row(m_ref, k_gate, _ctx_rows(pl.program_id(0), tm, s))
        o_ref[...] = x_ref[...] + (coef * gate) * y

    tile = pl.BlockSpec((tm, tn), lambda i, j: (i, j))
    return _call(body, name=name, grid=(t // tm, d // tn),
                 in_specs=[pl.BlockSpec((tm, k), lambda i, j: (i, 0)), _w_spec(w, (k, tn), lambda i, j: (0, j)), tile,
                           pl.BlockSpec((2, N_MOD, tn), lambda i, j: (0, 0, j))],
                 out_specs=[tile, tile], out_shape=[_sds((t, d), MXU), _sds((t, d), F32)])(a, _w_arr(w), xs, mods)


def _resb_mm(dxn, y, mods, k_gate, coef, w, s, name, ups=None):
    t, d = dxn.shape
    n = _w_dims(w)[0]
    tm = _pick(t, (1280, 640, 256))
    tn = _pick(n, (256, 128)) if ups else _pick(n, (512, 256, 128))

    def body(dx_ref, y_ref, m_ref, w_ref, *refs):
        i, j = pl.program_id(0), pl.program_id(1)
        u_refs, (dy_ref, dgt_ref), o_refs = (refs[:2], refs[2:4], refs[4:]) if ups else ((), refs[:2], refs[2:])

        @pl.when((i == 0) & (j == 0))
        def _():
            dgt_ref[...] = jnp.zeros_like(dgt_ref)

        @pl.when(j == 0)
        def _():
            dx = dx_ref[...]
            gate = _mod_row(m_ref, k_gate, _ctx_rows(i, tm, s))
            dy_ref[...] = ((coef * gate) * dx).astype(dy_ref.dtype)
            _stream_sums(i, tm, s, [(dgt_ref, coef * y_ref[...].astype(F32) * dx)])

        dact = _nt(dy_ref[...], w_ref[...])
        if ups:
            o_refs[0][...] = (dact * u_refs[0][...].astype(F32)).astype(o_refs[0].dtype)
            o_refs[1][...] = (dact * u_refs[1][...].astype(F32)).astype(o_refs[1].dtype)
        else:
            o_refs[0][...] = dact.astype(o_refs[0].dtype)

    tile = pl.BlockSpec((tm, tn), lambda i, j: (i, j))
    row = pl.BlockSpec((tm, d), lambda i, j: (i, 0))
    n_out = 2 if ups else 1
    return _call(body, name=name, grid=(t // tm, n // tn),
                 in_specs=[row, row, _fixed((2, N_MOD, d)), _w_spec(w, (tn, d), lambda i, j: (j, 0))] + ([tile, tile] if ups else []),
                 out_specs=[row, _fixed((2, 1, d))] + [tile] * n_out,
                 out_shape=[_sds((t, d), MXU), _sds((2, 1, d), F32)] + [_sds((t, n), MXU)] * n_out)(
                     dxn, y, mods, _w_arr(w), *(ups or ()))


def _mm_normb(a_list, w, xs, dres, g, mods, k_scale, s, name):
    t, d = xs.shape
    tm = _pick(t, (640, 256))
    n_a = len(a_list)
    widths = [a.shape[1] for a in a_list]
    tk = next(c for c in (1408, 1536, 1024, 512, 256, 128) if all(wd % c == 0 for wd in widths))
    counts = [wd // tk for wd in widths]
    starts = [sum(counts[:q]) for q in range(n_a)]
    nk = sum(counts)
    k_all = sum(widths)
    resident = 4 * k_all * (d + tm) + 24 * tm * d <= 48 * 2 ** 20

    def finish(i, dh, x_ref, dr_ref, g_ref, m_ref, dx_ref, dsh_ref, dsc_ref, dg_ref):
        x = x_ref[...]
        rstd = lax.rsqrt(jnp.mean(x * x, axis=-1, keepdims=True) + EPS)
        xhat = x * rstd
        gg = g_ref[...]
        _stream_sums(i, tm, s, [(dsh_ref, dh), (dsc_ref, dh * (xhat * gg))])
        dy = dh * (1.0 + _mod_row(m_ref, k_scale, _ctx_rows(i, tm, s)))
        dg_ref[...] += jnp.sum(dy * xhat, axis=0, keepdims=True)
        dxh = dy * gg
        dx_ref[...] = dr_ref[...] + rstd * (dxh - xhat * jnp.mean(dxh * xhat, axis=-1, keepdims=True))

    out_shape = [_sds((t, d), F32), _sds((2, 1, d), F32), _sds((2, 1, d), F32), _sds((1, d), F32)]
    if resident:
        def body_resident(*refs):
            a_refs, (w_ref, x_ref, dr_ref, g_ref, m_ref, dx_ref, dsh_ref, dsc_ref, dg_ref) = refs[:n_a], refs[n_a:]
            i = pl.program_id(0)

            @pl.when(i == 0)
            def _():
                dsh_ref[...] = jnp.zeros_like(dsh_ref)
                dsc_ref[...] = jnp.zeros_like(dsc_ref)
                dg_ref[...] = jnp.zeros_like(dg_ref)

            dh, off = None, 0
            for q in range(n_a):
                part = _nt(a_refs[q][...], w_ref[:, off:off + widths[q]])
                dh = part if dh is None else dh + part
                off += widths[q]
            finish(i, dh, x_ref, dr_ref, g_ref, m_ref, dx_ref, dsh_ref, dsc_ref, dg_ref)

        rows = pl.BlockSpec((tm, d), lambda i: (i, 0))
        return _call(body_resident, name=name, grid=(t // tm,),
                     in_specs=[_rows(tm, wd) for wd in widths] + [_w_spec(w, (d, k_all), lambda i: (0, 0)), rows, rows,
                                                                  _fixed((1, d)), _fixed((2, N_MOD, d))],
                     out_specs=[rows, _fixed((2, 1, d)), _fixed((2, 1, d)), _fixed((1, d))],
                     out_shape=out_shape)(*a_list, _w_arr(w), xs, dres, g, mods)

    def body(*refs):
        a_refs, (w_ref, x_ref, dr_ref, g_ref, m_ref, dx_ref, dsh_ref, dsc_ref, dg_ref, acc) = refs[:n_a], refs[n_a:]
        i, k = pl.program_id(0), pl.program_id(1)

        @pl.when((i == 0) & (k == 0))
        def _():
            dsh_ref[...] = jnp.zeros_like(dsh_ref)
            dsc_ref[...] = jnp.zeros_like(dsc_ref)
            dg_ref[...] = jnp.zeros_like(dg_ref)

        @pl.when(k == 0)
        def _():
            acc[...] = jnp.zeros_like(acc)

        for q in range(n_a):
            @pl.when((k >= starts[q]) & (k < starts[q] + counts[q]))
            def _():
                acc[...] += _nt(a_refs[q][...], w_ref[...])

        @pl.when(k == nk - 1)
        def _():
            finish(i, acc[...], x_ref, dr_ref, g_ref, m_ref, dx_ref, dsh_ref, dsc_ref, dg_ref)

    row = pl.BlockSpec((tm, d), lambda i, k: (i, 0))
    a_specs = [pl.BlockSpec((tm, tk), lambda i, k, q=q: (i, jnp.clip(k - starts[q], 0, counts[q] - 1))) for q in range(n_a)]
    return _call(body, name=name, grid=(t // tm, nk),
                 in_specs=a_specs + [_w_spec(w, (d, tk), lambda i, k: (0, k)), row, row, _fixed((1, d)), _fixed((2, N_MOD, d))],
                 out_specs=[row, _fixed((2, 1, d)), _fixed((2, 1, d)), _fixed((1, d))], out_shape=out_shape,
                 scratch=[pltpu.VMEM((tm, d), F32)])(*a_list, _w_arr(w), xs, dres, g, mods)


def _rope_tables(s, ctx_len):
    n_freq = HEAD_DIM // 4
    tok = jnp.arange(s)
    freqs = ROPE_THETA ** (-jnp.arange(n_freq, dtype=F32) / n_freq)
    ang = jnp.concatenate([(tok // GRID_W).astype(F32)[:, None] * freqs, (tok % GRID_W).astype(F32)[:, None] * freqs], axis=-1)
    cos = jnp.repeat(jnp.cos(ang), 2, axis=-1)
    sin = jnp.repeat(jnp.sin(ang), 2, axis=-1) * jnp.tile(jnp.array([-1.0, 1.0], F32), HEAD_DIM // 2)
    cos = jnp.concatenate([jnp.tile(cos, (1, LANES // HEAD_DIM)), jnp.ones((ctx_len, LANES), F32)], axis=0)
    sin = jnp.concatenate([jnp.tile(sin, (1, LANES // HEAD_DIM)), jnp.zeros((ctx_len, LANES), F32)], axis=0)
    return cos, sin


def _swap_pairs(x):
    n = x.shape[-1]
    lane = lax.broadcasted_iota(jnp.int32, x.shape, 1)
    return jnp.where(lane % 2 == 0, pltpu.roll(x, n - 1, 1), pltpu.roll(x, 1, 1))


def _rope(p, cos, sin, name):
    t = p.shape[0]
    w = NA_WIDTH
    te = _row_tile(t)

    def body(q_ref, k_ref, v_ref, c_ref, s_ref, qo_ref, ko_ref, vo_ref):
        c, s = c_ref[...], s_ref[...]
        for hp in range(w // LANES):
            cols = slice(hp * LANES, (hp + 1) * LANES)
            q, k = q_ref[:, cols].astype(F32), k_ref[:, cols].astype(F32)
            qo_ref[:, cols] = (q * c + _swap_pairs(q) * s).astype(qo_ref.dtype)
            ko_ref[:, cols] = (k * c + _swap_pairs(k) * s).astype(ko_ref.dtype)
        vo_ref[...] = v_ref[...].astype(vo_ref.dtype)

    return _call(body, name=name, grid=(t // te,),
                 in_specs=[_rows(te, w, 0), _rows(te, w, 1), _rows(te, w, 2), _rows(te, LANES), _rows(te, LANES)],
                 out_specs=[_rows(te, w)] * 3, out_shape=[_sds((t, w), MXU)] * 3)(p, p, p, cos, sin)


def _rope_bwd(dq, dk, dv, cos, sin, name):
    t = dq.shape[0]
    te = _row_tile(t)
    n_pairs = NA_WIDTH // LANES

    def body(dq_ref, dk_ref, dv_ref, c_ref, s_ref, qo_ref, ko_ref, vo_ref):
        c, s = c_ref[...], s_ref[...]
        for hp in range(n_pairs):
            cols = slice(hp * LANES, (hp + 1) * LANES)
            a, b = dq_ref[:, cols], dk_ref[hp]
            qo_ref[:, cols] = (a * c + _swap_pairs(a * s)).astype(qo_ref.dtype)
            ko_ref[:, cols] = (b * c + _swap_pairs(b * s)).astype(ko_ref.dtype)
            vo_ref[:, cols] = dv_ref[hp].astype(vo_ref.dtype)

    pairs = pl.BlockSpec((n_pairs, te, LANES), lambda i: (0, i, 0))
    return _call(body, name=name, grid=(t // te,),
                 in_specs=[_rows(te, NA_WIDTH), pairs, pairs, _rows(te, LANES), _rows(te, LANES)],
                 out_specs=[_rows(te, NA_WIDTH)] * 3, out_shape=[_sds((t, NA_WIDTH), MXU)] * 3)(dq, dk, dv, cos, sin)


def _na_geometry(r_grid):
    rows = []
    for r0, ks in ((0, 0), (Q_ROWS, 0), (r_grid - Q_ROWS, r_grid - K_ROWS)):
        dr = np.zeros((Q_ROWS, K_ROWS), np.int32)
        vr = np.zeros((Q_ROWS, K_ROWS), bool)
        for a in range(Q_ROWS):
            r = r0 + a
            rs = min(max(r - WIN_H // 2, 0), r_grid - WIN_H)
            for i in range(K_ROWS):
                kr = ks + i
                vr[a, i] = rs <= kr <= rs + WIN_H - 1
                dr[a, i] = kr - r + WIN_H - 1
        rows.append((dr, vr))
    c = np.arange(GRID_W)
    cs = np.clip(c - WIN_W // 2, 0, GRID_W - WIN_W)
    kc = np.arange(GRID_W)
    vc = (kc[None, :] >= cs[:, None]) & (kc[None, :] <= cs[:, None] + WIN_W - 1)
    dc = kc[None, :] - c[:, None] + WIN_W - 1
    return rows, dc, vc


def _bias_table(rpb, r_grid):
    rows, _, vc = _na_geometry(r_grid)
    n_dc, off = 2 * WIN_W - 1, GRID_W - WIN_W
    u = jnp.pad(rpb, ((0, 0), (0, 0), (off, 2 * GRID_W - 1 - off - n_dc)))
    toep = jnp.stack([u[:, :, GRID_W - 1 - c:2 * GRID_W - 1 - c] for c in range(GRID_W)], axis=1)
    toep = jnp.pad(toep, ((0, 0), (0, 0), (Q_ROWS, Q_ROWS), (0, 0)))
    tabs = []
    for dr, vr in rows:
        per_row = []
        for a in range(Q_ROWS):
            lo = int(dr[a, 0]) + Q_ROWS
            blocks = [jnp.where(vc[None], toep[:, :, lo + i, :], NEG) if vr[a, i] else jnp.full((HEADS, GRID_W, GRID_W), NEG, F32)
                      for i in range(K_ROWS)]
            per_row.append(jnp.concatenate(blocks, axis=-1))
        tabs.append(jnp.stack(per_row, axis=1).reshape(HEADS, TQ, TK))
    tabs.append(jnp.full((HEADS, TQ, TK), NEG, F32))
    return jnp.stack(tabs)


def _variant(g, ngx):
    return jnp.where(g == 0, 0, jnp.where(g >= ngx, 3, jnp.where(g == ngx - 1, 2, 1)))


def _key_start(g, r_grid):
    return pl.multiple_of(jnp.clip(g * Q_ROWS - WIN_H // 2, 0, r_grid - K_ROWS) * GRID_W, TQ)


def _nt(a, b):
    return lax.dot_general(a, b, (((1,), (1,)), ((), ())), preferred_element_type=F32)


def _tn(a, b):
    return lax.dot_general(a, b, (((0,), (0,)), ((), ())), preferred_element_type=F32)


def _nn(a, b):
    return jnp.dot(a, b, preferred_element_type=F32)


def _head_mask(h):
    lane = lax.broadcasted_iota(jnp.int32, (1, LANES), 1)
    return ((lane >= HEAD_DIM * h) & (lane < HEAD_DIM * (h + 1))).astype(F32)


def _softmax_parts(qm, knb, kcx, bias):
    s_nb = _nt(qm, knb) + bias
    s_cx = _nt(qm, kcx)
    m = jnp.maximum(jnp.max(s_nb, axis=-1, keepdims=True), jnp.max(s_cx, axis=-1, keepdims=True))
    e_nb = jnp.exp(s_nb - m)
    e_cx = jnp.exp(s_cx - m)
    inv = 1.0 / (jnp.sum(e_nb, axis=-1, keepdims=True) + jnp.sum(e_cx, axis=-1, keepdims=True))
    return e_nb, e_cx, inv


def _na_specs(t, ngx):
    q_spec = pl.BlockSpec((TQ, LANES), lambda hp, g: (g, hp))
    kv_spec = pl.BlockSpec((t, LANES), lambda hp, g: (0, hp))
    b_spec = pl.BlockSpec((1, 2, TQ, TK), lambda hp, g: (_variant(g, ngx), hp, 0, 0))
    return q_spec, kv_spec, b_spec


def _na_fwd(q, k, v, bias, s, name):
    t = q.shape[0]
    ctx_len = t - s
    r_grid = s // GRID_W
    q_spec, kv_spec, b_spec = _na_specs(t, s // TQ)

    def body(q_ref, k_ref, v_ref, b_ref, o_ref):
        start = _key_start(pl.program_id(1), r_grid)
        qf = q_ref[...].astype(F32) * (HEAD_DIM ** -0.5)
        knb, vnb = k_ref[pl.ds(start, TK), :], v_ref[pl.ds(start, TK), :]
        kcx, vcx = k_ref[pl.ds(s, ctx_len), :], v_ref[pl.ds(s, ctx_len), :]
        acc = jnp.zeros((TQ, LANES), F32)
        for h in range(2):
            mask = _head_mask(h)
            e_nb, e_cx, inv = _softmax_parts((qf * mask).astype(MXU), knb, kcx, b_ref[0, h])
            acc += (_nn(e_nb.astype(MXU), vnb) + _nn(e_cx.astype(MXU), vcx)) * (inv * mask)
        o_ref[...] = acc.astype(o_ref.dtype)

    return _call(body, name=name, grid=(NA_WIDTH // LANES, t // TQ), in_specs=[q_spec, kv_spec, kv_spec, b_spec],
                 out_specs=q_spec, out_shape=_sds((t, NA_WIDTH), MXU))(q, k, v, bias)


def _na_bwd(q, k, v, do, bias, s, name):
    t = q.shape[0]
    ctx_len = t - s
    r_grid = s // GRID_W
    ng, ngx = t // TQ, s // TQ
    q_spec, kv_spec, b_spec = _na_specs(t, ngx)

    def body(q_ref, k_ref, v_ref, do_ref, b_ref, dq_ref, dk_hbm, dv_hbm, db_ref, dk_acc, dv_acc):
        hp, g = pl.program_id(0), pl.program_id(1)
        start = _key_start(g, r_grid)

        @pl.when(g == 0)
        def _():
            dk_acc[...] = jnp.zeros_like(dk_acc)
            dv_acc[...] = jnp.zeros_like(dv_acc)

        @pl.when((g == 0) | (g == 1) | (g == ngx - 1) | (g == ngx))
        def _():
            db_ref[...] = jnp.zeros_like(db_ref)

        qf = q_ref[...].astype(F32) * (HEAD_DIM ** -0.5)
        do = do_ref[...].astype(F32)
        knb, vnb = k_ref[pl.ds(start, TK), :], v_ref[pl.ds(start, TK), :]
        kcx, vcx = k_ref[pl.ds(s, ctx_len), :], v_ref[pl.ds(s, ctx_len), :]
        dq = jnp.zeros((TQ, LANES), F32)
        dk_nb = jnp.zeros((TK, LANES), F32)
        dv_nb = jnp.zeros((TK, LANES), F32)
        dk_cx = jnp.zeros((ctx_len, LANES), F32)
        dv_cx = jnp.zeros((ctx_len, LANES), F32)
        for h in range(2):
            mask = _head_mask(h)
            qm = (qf * mask).astype(MXU)
            dom = (do * mask).astype(MXU)
            e_nb, e_cx, inv = _softmax_parts(qm, knb, kcx, b_ref[0, h])
            dp_nb = _nt(dom, vnb)
            dp_cx = _nt(dom, vcx)
            delta = inv * (jnp.sum(e_nb * dp_nb, axis=-1, keepdims=True) + jnp.sum(e_cx * dp_cx, axis=-1, keepdims=True))
            ds_nb = e_nb * (inv * (dp_nb - delta))
            ds_cx = e_cx * (inv * (dp_cx - delta))
            db_ref[0, h] += ds_nb
            ds_nb, ds_cx = ds_nb.astype(MXU), ds_cx.astype(MXU)
            dq += (_nn(ds_nb, knb) + _nn(ds_cx, kcx)) * (mask * (HEAD_DIM ** -0.5))
            dk_nb += _tn(ds_nb, qm)
            dk_cx += _tn(ds_cx, qm)
            dom = (do * (inv * mask)).astype(MXU)
            dv_nb += _tn(e_nb.astype(MXU), dom)
            dv_cx += _tn(e_cx.astype(MXU), dom)
        dq_ref[...] = dq
        dk_acc[pl.ds(start, TK), :] += dk_nb
        dv_acc[pl.ds(start, TK), :] += dv_nb
        dk_acc[pl.ds(s, ctx_len), :] += dk_cx
        dv_acc[pl.ds(s, ctx_len), :] += dv_cx

        @pl.when(g == ng - 1)
        def _():
            pltpu.sync_copy(dk_acc, dk_hbm.at[hp])
            pltpu.sync_copy(dv_acc, dv_hbm.at[hp])

    n_pairs = NA_WIDTH // LANES
    hbm = pl.BlockSpec(memory_space=pl.ANY)
    return _call(body, name=name, grid=(n_pairs, ng), in_specs=[q_spec, kv_spec, kv_spec, q_spec, b_spec],
                 out_specs=[q_spec, hbm, hbm, b_spec],
                 out_shape=[_sds((t, NA_WIDTH), F32), _sds((n_pairs, t, LANES), F32), _sds((n_pairs, t, LANES), F32),
                            _sds((4, HEADS, TQ, TK), F32)],
                 scratch=[pltpu.VMEM((t, LANES), F32), pltpu.VMEM((t, LANES), F32)])(q, k, v, do, bias)


def _rpb_grad(dbias, r_grid, name):
    rows, _, _ = _na_geometry(r_grid)
    n_dr, half, skew, lanes = 2 * WIN_H - 1, WIN_W - 1, TK + 2, 896
    z = dbias[:3].reshape(3, HEADS, Q_ROWS, GRID_W, TK)
    z = jnp.pad(z, ((0, 0),) * 4 + ((0, 1),)).reshape(3, HEADS, Q_ROWS, GRID_W * (TK + 1))
    z = jnp.pad(z, ((0, 0),) * 3 + ((0, GRID_W),)).reshape(3, HEADS, Q_ROWS, GRID_W, skew)
    z = jnp.pad(z, ((0, 0),) * 4 + ((0, lanes - skew),))

    def body(z_ref, o_ref):
        sums = [jnp.sum(z_ref[v, 0, a], axis=0, keepdims=True) for v in range(3) for a in range(Q_ROWS)]
        zs = jnp.concatenate(sums + [jnp.zeros((16 - 3 * Q_ROWS, lanes), F32)], axis=0)
        acc = [jnp.zeros((1, lanes), F32) for _ in range(n_dr)]
        for i in range(K_ROWS):
            if i == 0:
                at0 = pltpu.roll(zs, lanes - (skew - half), 1) + pltpu.roll(zs, half, 1)
            else:
                at0 = pltpu.roll(zs, lanes - (i * GRID_W - half), 1)
            for v, (dr, vr) in enumerate(rows):
                for a in range(Q_ROWS):
                    if vr[a, i]:
                        acc[dr[a, i]] = acc[dr[a, i]] + at0[v * Q_ROWS + a:v * Q_ROWS + a + 1, :]
        o_ref[0] = jnp.concatenate(acc + [jnp.zeros((1, lanes), F32)], axis=0)

    o = _call(body, name=name, grid=(HEADS,),
              in_specs=[pl.BlockSpec((3, 1, Q_ROWS, GRID_W, lanes), lambda h: (0, h, 0, 0, 0))],
              out_specs=pl.BlockSpec((1, 16, lanes), lambda h: (h, 0, 0)), out_shape=_sds((HEADS, 16, lanes), F32))(z)
    return o[:, :n_dr, :2 * WIN_W - 1]


_GELU_K, _GELU_C = 0.7978845608028654, 0.044715


def _gelu(x):
    return 0.5 * x * (1.0 + jnp.tanh(_GELU_K * (x + _GELU_C * x * x * x)))


def _gelu_grad(x):
    th = jnp.tanh(_GELU_K * (x + _GELU_C * x * x * x))
    return 0.5 * (1.0 + th) + 0.5 * x * (1.0 - th * th) * (_GELU_K * (1.0 + 3.0 * _GELU_C * x * x))


def _ln_stats(v):
    mu = jnp.mean(v, axis=-1, keepdims=True)
    vc = v - mu
    rstd = lax.rsqrt(jnp.mean(vc * vc, axis=-1, keepdims=True) + EPS)
    return vc * rstd, rstd


def _gmlp(p, ln_g, ln_b, w_s, b_s, name):
    t = p.shape[0]
    te = _row_tile(t)
    w = SG_WIDTH
    cw = w // SG_GROUPS

    def body(u_ref, v_ref, g_ref, b_ref, ws_ref, bs_ref, o_ref):
        xhat, _ = _ln_stats(_gelu(v_ref[...].astype(F32)))
        vn = (xhat * g_ref[...] + b_ref[...]).astype(MXU)
        ug = _gelu(u_ref[...].astype(F32))
        for ci in range(te // SG_CHUNK):
            rs = slice(ci * SG_CHUNK, (ci + 1) * SG_CHUNK)
            for gi in range(SG_GROUPS):
                cs = slice(gi * cw, (gi + 1) * cw)
                sg = _nn(ws_ref[gi].astype(MXU), vn[rs, cs]) + bs_ref[gi]
                o_ref[rs, cs] = (ug[rs, cs] * sg).astype(o_ref.dtype)

    return _call(body, name=name, grid=(t // te,),
                 in_specs=[_rows(te, w, 3), _rows(te, w, 4), _fixed((1, w)), _fixed((1, w)),
                           _fixed((SG_GROUPS, SG_CHUNK, SG_CHUNK)), _fixed((SG_GROUPS, SG_CHUNK, 1))],
                 out_specs=_rows(te, w), out_shape=_sds((t, w), MXU))(p, p, ln_g, ln_b, w_s, b_s)


def _gmlp_bwd(p, dob, ln_g, ln_b, w_s, b_s, name):
    t = p.shape[0]
    te = _row_tile(t)
    w = SG_WIDTH
    cw = w // SG_GROUPS

    def body(u_ref, v_ref, do_ref, g_ref, b_ref, ws_ref, bs_ref, du_ref, dv_ref, dws_ref, dbs_ref, dg_ref, db_ref, dvn_ref):
        @pl.when(pl.program_id(0) == 0)
        def _():
            dws_ref[...] = jnp.zeros_like(dws_ref)
            dbs_ref[...] = jnp.zeros_like(dbs_ref)
            dg_ref[...] = jnp.zeros_like(dg_ref)
            db_ref[...] = jnp.zeros_like(db_ref)

        u, v = u_ref[...].astype(F32), v_ref[...].astype(F32)
        xhat, rstd = _ln_stats(_gelu(v))
        vn = (xhat * g_ref[...] + b_ref[...]).astype(MXU)
        ug = _gelu(u)
        dob = do_ref[...].astype(F32)
        for ci in range(te // SG_CHUNK):
            rs = slice(ci * SG_CHUNK, (ci + 1) * SG_CHUNK)
            for gi in range(SG_GROUPS):
                cs = slice(gi * cw, (gi + 1) * cw)
                wsg = ws_ref[gi].astype(MXU)
                sg = _nn(wsg, vn[rs, cs]) + bs_ref[gi]
                du_ref[rs, cs] = (dob[rs, cs] * sg * _gelu_grad(u[rs, cs])).astype(du_ref.dtype)
                ds = dob[rs, cs] * ug[rs, cs]
                dbs_ref[gi] += jnp.sum(ds, axis=-1, keepdims=True)
                ds = ds.astype(MXU)
                dws_ref[gi] += _nt(ds, vn[rs, cs])
                dvn_ref[rs, cs] = _tn(wsg, ds)
        dvn = dvn_ref[...]
        dg_ref[...] += jnp.sum(dvn * xhat, axis=0, keepdims=True)
        db_ref[...] += jnp.sum(dvn, axis=0, keepdims=True)
        dxh = dvn * g_ref[...]
        dvg = rstd * (dxh - jnp.mean(dxh, axis=-1, keepdims=True) - xhat * jnp.mean(dxh * xhat, axis=-1, keepdims=True))
        dv_ref[...] = (dvg * _gelu_grad(v)).astype(dv_ref.dtype)

    return _call(body, name=name, grid=(t // te,),
                 in_specs=[_rows(te, w, 3), _rows(te, w, 4), _rows(te, w), _fixed((1, w)), _fixed((1, w)),
                           _fixed((SG_GROUPS, SG_CHUNK, SG_CHUNK)), _fixed((SG_GROUPS, SG_CHUNK, 1))],
                 out_specs=[_rows(te, w), _rows(te, w), _fixed((SG_GROUPS, SG_CHUNK, SG_CHUNK)),
                            _fixed((SG_GROUPS, SG_CHUNK, 1)), _fixed((1, w)), _fixed((1, w))],
                 out_shape=[_sds((t, w), MXU), _sds((t, w), MXU), _sds((SG_GROUPS, SG_CHUNK, SG_CHUNK), F32),
                            _sds((SG_GROUPS, SG_CHUNK, 1), F32), _sds((1, w), F32), _sds((1, w), F32)],
                 scratch=[pltpu.VMEM((te, w), F32)])(p, p, dob, ln_g, ln_b, w_s, b_s)


def _merge(pa, pb, p, b_gate, name):
    t, d = pa.shape
    te = _row_tile(t)
    hw = NA_WIDTH
    nh = d // hw
    c0 = (NA_WIDTH * 3 + SG_WIDTH * 2) // hw

    def body(pa_ref, pb_ref, la_ref, lb_ref, ba_ref, bb_ref, o_ref):
        ga = jax.nn.sigmoid(la_ref[...].astype(F32) + ba_ref[...])
        gb = jax.nn.sigmoid(lb_ref[...].astype(F32) + bb_ref[...])
        o_ref[...] = (ga * pa_ref[...].astype(F32) + gb * pb_ref[...].astype(F32)).astype(o_ref.dtype)

    tile = pl.BlockSpec((te, hw), lambda i, j: (i, j))
    return _call(body, name=name, grid=(t // te, nh),
                 in_specs=[tile, tile, pl.BlockSpec((te, hw), lambda i, j: (i, c0 + j)),
                           pl.BlockSpec((te, hw), lambda i, j: (i, c0 + nh + j)),
                           pl.BlockSpec((1, hw), lambda i, j: (0, j)), pl.BlockSpec((1, hw), lambda i, j: (0, nh + j))],
                 out_specs=tile, out_shape=_sds((t, d), MXU))(pa, pb, p, p, b_gate, b_gate)


def _merge_bwd(dmg, pa, pb, p, b_gate, name):
    t, d = pa.shape
    te = _row_tile(t)
    hw = NA_WIDTH
    nh = d // hw
    c0 = (NA_WIDTH * 3 + SG_WIDTH * 2) // hw

    def body(dm_ref, pa_ref, pb_ref, la_ref, lb_ref, ba_ref, bb_ref, dpa_ref, dpb_ref, dla_ref, dlb_ref, dba_ref, dbb_ref):
        @pl.when(pl.program_id(1) == 0)
        def _():
            dba_ref[...] = jnp.zeros_like(dba_ref)
            dbb_ref[...] = jnp.zeros_like(dbb_ref)

        dm = dm_ref[...].astype(F32)
        ga = jax.nn.sigmoid(la_ref[...].astype(F32) + ba_ref[...])
        gb = jax.nn.sigmoid(lb_ref[...].astype(F32) + bb_ref[...])
        dpa_ref[...] = (dm * ga).astype(dpa_ref.dtype)
        dpb_ref[...] = (dm * gb).astype(dpb_ref.dtype)
        dla = dm * pa_ref[...].astype(F32) * ga * (1.0 - ga)
        dlb = dm * pb_ref[...].astype(F32) * gb * (1.0 - gb)
        dla_ref[...] = dla.astype(dla_ref.dtype)
        dlb_ref[...] = dlb.astype(dlb_ref.dtype)
        dba_ref[...] += jnp.sum(dla, axis=0, keepdims=True)
        dbb_ref[...] += jnp.sum(dlb, axis=0, keepdims=True)

    tile = pl.BlockSpec((te, hw), lambda j, i: (i, j))
    bias_a = pl.BlockSpec((1, hw), lambda j, i: (0, j))
    bias_b = pl.BlockSpec((1, hw), lambda j, i: (0, nh + j))
    return _call(body, name=name, grid=(nh, t // te),
                 in_specs=[tile, tile, tile, pl.BlockSpec((te, hw), lambda j, i: (i, c0 + j)),
                           pl.BlockSpec((te, hw), lambda j, i: (i, c0 + nh + j)), bias_a, bias_b],
                 out_specs=[tile, tile, tile, tile, bias_a, bias_a],
                 out_shape=[_sds((t, d), MXU)] * 4 + [_sds((1, d), F32)] * 2)(dmg, pa, pb, p, p, b_gate, b_gate)


def _final(xs, tgt, g, name):
    t, d = xs.shape
    nx = tgt.shape[0] // TM

    def body(x_ref, t_ref, g_ref, l_ref, dx_ref, dg_ref):
        i = pl.program_id(0)

        @pl.when(i == 0)
        def _():
            l_ref[...] = jnp.zeros_like(l_ref)
            dg_ref[...] = jnp.zeros_like(dg_ref)

        @pl.when(i < nx)
        def _():
            x = x_ref[...]
            rstd = lax.rsqrt(jnp.mean(x * x, axis=-1, keepdims=True) + EPS)
            xhat = x * rstd
            err = xhat * g_ref[...] - t_ref[...]
            l_ref[...] += 0.5 * jnp.sum(jnp.mean(err * err, axis=-1, keepdims=True))
            dy = err * (1.0 / d)
            dg_ref[...] += jnp.sum(dy * xhat, axis=0, keepdims=True)
            dxh = dy * g_ref[...]
            dx_ref[...] = rstd * (dxh - xhat * jnp.mean(dxh * xhat, axis=-1, keepdims=True))

        @pl.when(i >= nx)
        def _():
            dx_ref[...] = jnp.zeros_like(dx_ref)

    return _call(body, name=name, grid=(t // TM,),
                 in_specs=[_rows(TM, d), pl.BlockSpec((TM, d), lambda i: (jnp.minimum(i, nx - 1), 0)), _fixed((1, d))],
                 out_specs=[_fixed((1, LANES)), _rows(TM, d), _fixed((1, d))],
                 out_shape=[_sds((1, LANES), F32), _sds((t, d), F32), _sds((1, d), F32)])(xs, tgt, g)


def _view2d(a):
    return a.reshape(1, -1) if a.ndim == 1 else a.reshape(-1, a.shape[-1])


def _tile_rows(r, c):
    for cand in (1024, 512, 256, 128, 64, 32, 16):
        if r % cand == 0 and cand * c * 4 <= 2 ** 20:
            return cand
    return r


def _pair_sum(g, recv, layer, name):
    _, a, b = g.shape
    tr = _tile_rows(a, b)

    def body(l_ref, g_ref, r_ref, o32_ref, o16_ref):
        acc = g_ref[...] + r_ref[...]
        o32_ref[...] = acc
        o16_ref[...] = acc.astype(o16_ref.dtype)

    first = pl.BlockSpec((None, tr, b), lambda i, l: (0, i, 0))
    return pl.pallas_call(
        body, name=name, out_shape=[_sds((1, a, b), F32), _sds((1, a, b), MXU)],
        grid_spec=pltpu.PrefetchScalarGridSpec(
            num_scalar_prefetch=1, grid=(a // tr,),
            in_specs=[pl.BlockSpec((None, tr, b), lambda i, l: (l[0], i, 0)), first], out_specs=[first, first]),
        compiler_params=pltpu.CompilerParams(dimension_semantics=("arbitrary",), vmem_limit_bytes=VMEM_LIMIT))(layer, g, recv)


def _ew(fn, arrays, out_dtypes, name):
    shape = arrays[0].shape
    views = [_view2d(a) for a in arrays]
    r, c = views[0].shape
    tr = _tile_rows(r, c)

    def body(*refs):
        outs = fn(*[ref[...] for ref in refs[:len(views)]])
        for ref, o in zip(refs[len(views):], outs):
            ref[...] = o.astype(ref.dtype)

    res = _call(body, name=name, grid=(r // tr,), in_specs=[_rows(tr, c)] * len(views), out_specs=[_rows(tr, c)] * len(out_dtypes),
                out_shape=[_sds((r, c), dt) for dt in out_dtypes])(*views)
    return [o.reshape(shape) for o in res]


def _sum_pieces(pieces, name, out_dtypes=(F32,)):
    def fn(*vals):
        acc = vals[0].astype(F32)
        for v in vals[1:]:
            acc = acc + v.astype(F32)
        return (acc,) * len(out_dtypes)

    return _ew(fn, pieces, list(out_dtypes), name)


def _adam_update(w, g, m, v):
    m2 = ADAM_B1 * m + (1.0 - ADAM_B1) * g
    v2 = ADAM_B2 * v + (1.0 - ADAM_B2) * (g * g)
    m_hat = m2 / (1.0 - ADAM_B1 ** ADAM_STEP)
    v_hat = v2 / (1.0 - ADAM_B2 ** ADAM_STEP)
    delta = -ADAM_LR * (m_hat / (jnp.sqrt(v_hat) + ADAM_EPS) + ADAM_WD * w)
    return g, delta, m2, v2


def _adamw(w, g_pieces, m, v, name):
    n_g = len(g_pieces)

    def fn(w_, *rest):
        g = rest[0]
        for piece in rest[1:n_g]:
            g = g + piece
        return _adam_update(w_, g, rest[n_g], rest[n_g + 1])

    return _ew(fn, [w, *g_pieces, m, v], [F32] * 4, name)


def _adamw_layers(w, mine, other, m, v, layer, name):
    _, a, b = w.shape
    tr = _tile_rows(a, b)
    nb = a // tr

    def body(l_ref, w_ref, mine_ref, other_ref, m_ref, v_ref, *o_refs):
        g = jnp.where(pl.program_id(0) // nb == l_ref[0], mine_ref[...], other_ref[...])
        for ref, val in zip(o_refs, _adam_update(w_ref[...], g, m_ref[...], v_ref[...])):
            ref[...] = val

    both = pl.BlockSpec((None, tr, b), lambda i, l: (i // nb, i % nb, 0))
    one = pl.BlockSpec((None, tr, b), lambda i, l: (0, i % nb, 0))
    return pl.pallas_call(
        body, name=name, out_shape=[_sds(w.shape, F32)] * 4,
        grid_spec=pltpu.PrefetchScalarGridSpec(num_scalar_prefetch=1, grid=(2 * nb,), in_specs=[both, one, one, both, both],
                                               out_specs=[both] * 4),
        compiler_params=pltpu.CompilerParams(dimension_semantics=("arbitrary",), vmem_limit_bytes=VMEM_LIMIT))(
            layer, w, mine, other, m, v)


def _place_shard(full, shard, chip, axis, name):
    _, a, b = shard.shape
    tr = _tile_rows(a, b)
    nb = a // tr

    def body(c_ref, s_ref, f_ref, o_ref):
        o_ref[...] = s_ref[...]

    if axis == 1:
        out_spec = pl.BlockSpec((None, tr, b), lambda l, i, c: (l, c[0] * nb + i, 0))
    else:
        out_spec = pl.BlockSpec((None, tr, b), lambda l, i, c: (l, i, c[0]))
    return pl.pallas_call(
        body, name=name, out_shape=_sds(full.shape, full.dtype), input_output_aliases={2: 0},
        grid_spec=pltpu.PrefetchScalarGridSpec(
            num_scalar_prefetch=1, grid=(2, nb),
            in_specs=[pl.BlockSpec((None, tr, b), lambda l, i, c: (l, i, 0)), pl.BlockSpec(memory_space=pl.ANY)],
            out_specs=out_spec),
        compiler_params=pltpu.CompilerParams(dimension_semantics=("arbitrary", "arbitrary"), vmem_limit_bytes=VMEM_LIMIT))(
            chip, shard, full)


def _ada_fwd(cond, w, b, name):
    r, d = cond.shape
    n = w.shape[1]
    tn = _pick(n, (1152, 768, 512, 384, 256, 128))

    def body(c_ref, w_ref, b_ref, o_ref, s_ref):
        c = c_ref[...]
        sc = c * jax.nn.sigmoid(c)
        s_ref[...] = sc
        o_ref[...] = _nn(sc.astype(MXU), w_ref[...].astype(MXU)) + b_ref[...]

    return _call(body, name=name, grid=(n // tn,),
                 in_specs=[_fixed((r, d)), pl.BlockSpec((d, tn), lambda j: (0, j)), pl.BlockSpec((1, tn), lambda j: (0, j))],
                 out_specs=[pl.BlockSpec((r, tn), lambda j: (0, j)), _fixed((r, d))],
                 out_shape=[_sds((r, n), F32), _sds((r, d), F32)])(cond, w, b)


def _cctx_grad(parts, c_ctx, name):
    n, d = parts.shape

    def body(p_ref, c_ref, o_ref):
        c = c_ref[...]
        sg = jax.nn.sigmoid(c)
        acc = p_ref[0:1, :]
        for j in range(1, n):
            acc = acc + p_ref[j:j + 1, :]
        o_ref[...] = acc * (sg * (1.0 + c * (1.0 - sg)))

    return _call(body, name=name, grid=(1,), in_specs=[_fixed((n, d)), _fixed((1, d))], out_specs=_fixed((1, d)),
                 out_shape=_sds((1, d), F32))(parts, c_ctx)


def _here():
    return lax.axis_index("x"), lax.axis_index("y"), lax.axis_index("c")


def _flip(v, bit):
    return 1 - v if bit else v


def _allgather8(xb, name):
    r, n = xb.shape

    def body(x_ref, out_ref, send_sems, recv_sems, local_sem):
        x, y, c = _here()
        me = 4 * x + 2 * y + c
        local = pltpu.make_async_copy(x_ref, out_ref.at[me], local_sem)
        local.start()
        sends = []
        for k in range(1, 8):
            peer = (_flip(x, k & 4), _flip(y, k & 2), _flip(c, k & 1))
            cp = pltpu.make_async_remote_copy(src_ref=x_ref, dst_ref=out_ref.at[me], send_sem=send_sems.at[k - 1],
                                              recv_sem=recv_sems.at[k - 1], device_id=peer, device_id_type=MESH)
            cp.start()
            sends.append(cp)
        for k in range(1, 8):
            peer = (_flip(x, k & 4), _flip(y, k & 2), _flip(c, k & 1))
            src = 4 * peer[0] + 2 * peer[1] + peer[2]
            pltpu.make_async_remote_copy(src_ref=x_ref, dst_ref=out_ref.at[src], send_sem=send_sems.at[k - 1],
                                         recv_sem=recv_sems.at[k - 1], device_id=peer, device_id_type=MESH).wait_recv()
        for cp in sends:
            cp.wait_send()
        local.wait()

    vmem = pl.BlockSpec(memory_space=pltpu.VMEM)
    return pl.pallas_call(
        body, name=name, out_shape=_sds((8, r, n), xb.dtype), in_specs=[vmem], out_specs=vmem,
        scratch_shapes=[pltpu.SemaphoreType.DMA((7,)), pltpu.SemaphoreType.DMA((7,)), pltpu.SemaphoreType.DMA(())],
        compiler_params=pltpu.CompilerParams(vmem_limit_bytes=VMEM_LIMIT))(xb)


def _shard_of(ref, axis, j, size):
    sl = pl.ds(j * size, size)
    return ref.at[:, sl, :] if axis == 1 else ref.at[:, :, sl]


def _piece(ref, axis, j, size, layer):
    lay, sl = pl.ds(layer, 1), pl.ds(j * size, size)
    return ref.at[lay, sl, :] if axis == 1 else ref.at[lay, :, sl]


def _gather_chips(shards, axes, name):
    n = len(shards)
    fulls = []
    for a, ax in zip(shards, axes):
        assert a.shape[0] == 2
        shp = list(a.shape)
        shp[ax] *= 4
        fulls.append(_sds(tuple(shp), a.dtype))

    def body(*refs):
        ins, outs = refs[:n], refs[n:2 * n]
        ici_send, ici_recv, d2d_send, d2d_recv = refs[2 * n:]
        x, y, c = _here()
        chips = [(_flip(x, k & 2), _flip(y, k & 1)) for k in range(1, 4)]
        sends = []
        for a in range(n):
            size = ins[a].shape[axes[a]]
            for j, (px, py) in enumerate(chips):
                cp = pltpu.make_async_remote_copy(src_ref=ins[a].at[pl.ds(c, 1)], dst_ref=_piece(outs[a], axes[a], 2 * x + y, size, c),
                                                  send_sem=ici_send.at[3 * a + j], recv_sem=ici_recv.at[3 * a + j],
                                                  device_id=(px, py, c), device_id_type=MESH)
                cp.start()
                sends.append(cp)
        for a in range(n):
            size = ins[a].shape[axes[a]]
            for j, (px, py) in enumerate(chips):
                landed = _piece(outs[a], axes[a], 2 * px + py, size, c)
                pltpu.make_async_remote_copy(src_ref=ins[a].at[pl.ds(c, 1)], dst_ref=landed, send_sem=ici_send.at[3 * a + j],
                                             recv_sem=ici_recv.at[3 * a + j], device_id=(px, py, c), device_id_type=MESH).wait_recv()
                cp = pltpu.make_async_remote_copy(src_ref=landed, dst_ref=landed, send_sem=d2d_send.at[3 * a + j],
                                                  recv_sem=d2d_recv.at[3 * a + j], device_id=(x, y, 1 - c), device_id_type=MESH)
                cp.start()
                sends.append(cp)
        for a in range(n):
            size = ins[a].shape[axes[a]]
            for j, (px, py) in enumerate(chips):
                passed = _piece(outs[a], axes[a], 2 * px + py, size, 1 - c)
                pltpu.make_async_remote_copy(src_ref=passed, dst_ref=passed, send_sem=d2d_send.at[3 * a + j],
                                             recv_sem=d2d_recv.at[3 * a + j], device_id=(x, y, 1 - c), device_id_type=MESH).wait_recv()
        for cp in sends:
            cp.wait_send()

    hbm = pl.BlockSpec(memory_space=pl.ANY)
    return pl.pallas_call(
        body, name=name, out_shape=fulls, in_specs=[hbm] * n, out_specs=[hbm] * n,
        scratch_shapes=[pltpu.SemaphoreType.DMA((3 * n,))] * 4)(*shards)


def _swap_layers(arrays, name):
    n = len(arrays)

    def body(*refs):
        ins, outs = refs[:n], refs[n:2 * n]
        send_sems, recv_sems = refs[2 * n:]
        x, y, c = _here()
        copies = []
        for a in range(n):
            cp = pltpu.make_async_remote_copy(src_ref=ins[a].at[pl.ds(1 - c, 1)], dst_ref=outs[a], send_sem=send_sems.at[a],
                                              recv_sem=recv_sems.at[a], device_id=(x, y, 1 - c), device_id_type=MESH)
            cp.start()
            copies.append(cp)
        for cp in copies:
            cp.wait()

    hbm = pl.BlockSpec(memory_space=pl.ANY)
    return pl.pallas_call(
        body, name=name, out_shape=[_sds((1, *a.shape[1:]), a.dtype) for a in arrays], in_specs=[hbm] * n, out_specs=[hbm] * n,
        scratch_shapes=[pltpu.SemaphoreType.DMA((n,)), pltpu.SemaphoreType.DMA((n,))])(*arrays)


def _scatter_chips(fulls, axes, name):
    n = len(fulls)
    recvs = []
    for a, ax in zip(fulls, axes):
        shp = list(a.shape)
        shp[ax] //= 4
        recvs.append(_sds((3, *shp), a.dtype))

    def body(*refs):
        ins, outs = refs[:n], refs[n:2 * n]
        send_sems, recv_sems = refs[2 * n:]
        x, y, c = _here()
        sends = []
        for a in range(n):
            size = ins[a].shape[axes[a]] // 4
            for k in range(1, 4):
                peer = (_flip(x, k & 2), _flip(y, k & 1), c)
                cp = pltpu.make_async_remote_copy(src_ref=_shard_of(ins[a], axes[a], 2 * peer[0] + peer[1], size),
                                                  dst_ref=outs[a].at[k - 1],
                                                  send_sem=send_sems.at[3 * a + k - 1], recv_sem=recv_sems.at[3 * a + k - 1],
                                                  device_id=peer, device_id_type=MESH)
                cp.start()
                sends.append(cp)
        for cp in sends:
            cp.wait_recv()
        for cp in sends:
            cp.wait_send()

    hbm = pl.BlockSpec(memory_space=pl.ANY)
    return pl.pallas_call(
        body, name=name, out_shape=recvs, in_specs=[hbm] * n, out_specs=[hbm] * n,
        scratch_shapes=[pltpu.SemaphoreType.DMA((3 * n,)), pltpu.SemaphoreType.DMA((3 * n,))])(*fulls)


def _sibling_swap(arrays, name):
    n = len(arrays)

    def body(*refs):
        ins, outs = refs[:n], refs[n:2 * n]
        send_sems, recv_sems = refs[2 * n:]
        x, y, c = _here()
        copies = []
        for a in range(n):
            cp = pltpu.make_async_remote_copy(src_ref=ins[a], dst_ref=outs[a], send_sem=send_sems.at[a], recv_sem=recv_sems.at[a],
                                              device_id=(x, y, 1 - c), device_id_type=MESH)
            cp.start()
            copies.append(cp)
        for cp in copies:
            cp.wait()

    hbm = pl.BlockSpec(memory_space=pl.ANY)
    return pl.pallas_call(
        body, name=name, out_shape=[_sds(a.shape, a.dtype) for a in arrays], in_specs=[hbm] * n, out_specs=[hbm] * n,
        scratch_shapes=[pltpu.SemaphoreType.DMA((n,)), pltpu.SemaphoreType.DMA((n,))])(*arrays)


def _ffn_fwd(xs, g, mods, k0, w_up, w_down, s, tag):
    h, ua, ub, act = _norm_mm(xs, g, mods, k0, k0 + 1, w_up, s, tag + "_up", True)
    y, xn = _mm_res(act, w_down, xs, mods, k0 + 2, 0.5, s, tag + "_down")
    return xn, (xs, h, ua, ub, act, y)


def _dw(gw, key, a, b, name, col0=0, n_total=None):
    shape = (gw["depth"], a.shape[1], n_total or b.shape[1])
    gw[key] = _mm(a, b, "tn", F32, name, into=(gw.get(key), shape, gw["layer"], col0))


def _ffn_bwd(dxn, saved, g, mods, k0, w_up, w_down, s, tag, gw, up_key, down_key):
    xs, h, ua, ub, act, y = saved
    dy, dgate, dua, dub = _resb_mm(dxn, y, mods, k0 + 2, 0.5, w_down, s, tag + "_down_dx", (ua, ub))
    _dw(gw, down_key, act, dy, tag + "_down_dw")
    f = dua.shape[1]
    _dw(gw, up_key, h, dua, tag + "_upa_dw", 0, 2 * f)
    _dw(gw, up_key, h, dub, tag + "_upb_dw", f, 2 * f)
    dx, dsh, dsc, dg = _mm_normb([dua, dub], w_up, xs, dxn, g, mods, k0 + 1, s, tag + "_up_dx")
    return dx, dg, [dsh, dsc, dgate]


def _mix_fwd(xs, g, mods, wl, pl_, tabs, s, tag):
    cos, sin = tabs
    h, p = _norm_mm(xs, g, mods, 3, 4, wl["w_in"], s, tag + "_in", False)
    q, k, v = _rope(p, cos, sin, tag + "_rope")
    bias = _bias_table(pl_["rpb"], s // GRID_W)
    oa = _na_fwd(q, k, v, bias, s, tag + "_na")
    ob = _gmlp(p, pl_["ln_v_g"], pl_["ln_v_b"], pl_["w_s"], pl_["b_s"], tag + "_sg")
    pa = _mm(oa, wl["w_pa"], "nn", MXU, tag + "_pa")
    pb = _mm(ob, wl["w_pb"], "nn", MXU, tag + "_pb")
    mg = _merge(pa, pb, p, pl_["b_gate"], tag + "_merge")
    y, xn = _mm_res(mg, wl["w_o"], xs, mods, 5, 1.0, s, tag + "_o")
    return xn, (xs, h, p, q, k, v, bias, oa, ob, pa, pb, mg, y)


def _mix_bwd(dxn, saved, g, mods, wl, pl_, tabs, s, tag, gw):
    xs, h, p, q, k, v, bias, oa, ob, pa, pb, mg, y = saved
    cos, sin = tabs
    gp = {}
    dy, dgate, dmg = _resb_mm(dxn, y, mods, 5, 1.0, wl["w_o"], s, tag + "_o_dx")
    _dw(gw, "w_o", mg, dy, tag + "_o_dw")
    dpa, dpb, dla, dlb, dba, dbb = _merge_bwd(dmg, pa, pb, p, pl_["b_gate"], tag + "_merge_b")
    gp["b_gate"] = jnp.concatenate([dba, dbb], axis=1)
    _dw(gw, "w_pa", oa, dpa, tag + "_pa_dw")
    doa = _mm(dpa, wl["w_pa"], "nt", MXU, tag + "_pa_dx")
    _dw(gw, "w_pb", ob, dpb, tag + "_pb_dw")
    dob = _mm(dpb, wl["w_pb"], "nt", MXU, tag + "_pb_dx")
    du, dvs, gp["w_s"], gp["b_s"], gp["ln_v_g"], gp["ln_v_b"] = _gmlp_bwd(
        p, dob, pl_["ln_v_g"], pl_["ln_v_b"], pl_["w_s"], pl_["b_s"], tag + "_sg_b")
    dqr, dkr, dv, dbias = _na_bwd(q, k, v, doa, bias, s, tag + "_na_b")
    gp["rpb"] = _rpb_grad(dbias, s // GRID_W, tag + "_rpb")
    dq, dk, dvv = _rope_bwd(dqr, dkr, dv, cos, sin, tag + "_rope_b")
    dp = [dq, dk, dvv, du, dvs, dla, dlb]
    n_in = sum(piece.shape[1] for piece in dp)
    for j, piece in enumerate(dp):
        _dw(gw, "w_in", h, piece, tag + f"_in_dw{j}", sum(q.shape[1] for q in dp[:j]), n_in)
    dx, dsh, dsc, dg = _mm_normb(dp, wl["w_in"], xs, dxn, g, mods, 4, s, tag + "_in_dx")
    return dx, gp, dg, [dsh, dsc, dgate]


def _local_step(x, ctx, tgt, mods, wts, prm):
    s, d = x.shape
    depth = mods.shape[0]
    tabs = _rope_tables(s, ctx.shape[0])
    xs = jnp.concatenate([x, ctx], axis=0)
    saved = []
    for l in range(depth):
        wl = {k: (v, l) for k, v in wts.items()}
        pl_ = _layer_params(prm, l)
        xs, s1 = _ffn_fwd(xs, pl_["g"][0], mods[l], 0, wl["w_ff1_up"], wl["w_ff1_down"], s, f"l{l}_ff1")
        xs, s2 = _mix_fwd(xs, pl_["g"][1], mods[l], wl, pl_, tabs, s, f"l{l}_mix")
        xs, s3 = _ffn_fwd(xs, pl_["g"][2], mods[l], 6, wl["w_ff2_up"], wl["w_ff2_down"], s, f"l{l}_ff2")
        saved.append((s1, s2, s3))
    loss, dxs, d_final_g = _final(xs, tgt, prm["final_g"].reshape(1, d), "final")
    gw = {"depth": depth}
    gp = {k: [None] * depth for k in ("norm_g", "b_gate", "rpb", "ln_v_g", "ln_v_b", "w_s", "b_s")}
    dmods = [None] * depth
    for l in reversed(range(depth)):
        wl = {k: (v, l) for k, v in wts.items()}
        pl_ = _layer_params(prm, l)
        s1, s2, s3 = saved[l]
        gw["layer"] = l
        dxs, dg2, dm2 = _ffn_bwd(dxs, s3, pl_["g"][2], mods[l], 6, wl["w_ff2_up"], wl["w_ff2_down"], s, f"l{l}_ff2",
                                 gw, "w_ff2_up", "w_ff2_down")
        dxs, gpm, dg1, dm1 = _mix_bwd(dxs, s2, pl_["g"][1], mods[l], wl, pl_, tabs, s, f"l{l}_mix", gw)
        dxs, dg0, dm0 = _ffn_bwd(dxs, s1, pl_["g"][0], mods[l], 0, wl["w_ff1_up"], wl["w_ff1_down"], s, f"l{l}_ff1",
                                 gw, "w_ff1_up", "w_ff1_down")
        gp["b_gate"][l] = gpm["b_gate"][0]
        gp["rpb"][l] = gpm["rpb"]
        gp["ln_v_g"][l] = gpm["ln_v_g"][0]
        gp["ln_v_b"][l] = gpm["ln_v_b"][0]
        gp["w_s"][l] = gpm["w_s"]
        gp["b_s"][l] = gpm["b_s"][..., 0]
        gp["norm_g"][l] = jnp.concatenate([dg0, dg1, dg2], axis=0)
        dmods[l] = jnp.concatenate(dm0 + dm1 + dm2, axis=1)
    gw = {k: gw[k] for k in wts}
    gp = {k: jnp.stack(v) for k, v in gp.items()}
    gp["final_g"] = d_final_g[0]
    return loss[0, 0], dxs[:s], jnp.stack(dmods), gw, gp


def _layer_params(prm, l):
    d = prm["norm_g"].shape[-1]
    return {
        "g": [prm["norm_g"][l, i].reshape(1, d) for i in range(3)],
        "b_gate": prm["b_gate"][l].reshape(1, -1),
        "rpb": prm["rpb"][l],
        "ln_v_g": prm["ln_v_g"][l].reshape(1, -1),
        "ln_v_b": prm["ln_v_b"][l].reshape(1, -1),
        "w_s": prm["w_s"][l],
        "b_s": prm["b_s"][l][..., None],
    }


SMALL = ("norm_g", "b_gate", "rpb", "ln_v_g", "ln_v_b", "w_s", "b_s", "final_g")
PACK_LANES = 1024


def _pack(parts):
    flat = jnp.concatenate([p.reshape(-1) for p in parts])
    rows = -(-flat.shape[0] // PACK_LANES)
    rows = -(-rows // 8) * 8
    return jnp.pad(flat, (0, rows * PACK_LANES - flat.shape[0])).reshape(rows, PACK_LANES)


def _unpack(flat, shapes):
    out, off = [], 0
    for shp in shapes:
        n = int(np.prod(shp))
        out.append(flat[..., off:off + n].reshape(*flat.shape[:-1], *shp))
        off += n
    return out


def kernel(x, c, ctx, c_ctx, w_ada, b_ada, norm_g, w_ff1_up, w_ff1_down, w_in, b_gate, rpb, ln_v_g, ln_v_b, w_s, b_s, w_pa, w_pb, w_o, w_ff2_up, w_ff2_down, final_g, loss_target, m_c_ctx, m_w_ada, m_b_ada, m_norm_g, m_w_ff1_up, m_w_ff1_down, m_w_in, m_b_gate, m_rpb, m_ln_v_g, m_ln_v_b, m_w_s, m_b_s, m_w_pa, m_w_pb, m_w_o, m_w_ff2_up, m_w_ff2_down, m_final_g, v_c_ctx, v_w_ada, v_b_ada, v_norm_g, v_w_ff1_up, v_w_ff1_down, v_w_in, v_b_gate, v_rpb, v_ln_v_g, v_ln_v_b, v_w_s, v_b_s, v_w_pa, v_w_pb, v_w_o, v_w_ff2_up, v_w_ff2_down, v_final_g):
    weights = dict(c_ctx=c_ctx, w_ada=w_ada, b_ada=b_ada, norm_g=norm_g, w_ff1_up=w_ff1_up, w_ff1_down=w_ff1_down, w_in=w_in,
                   b_gate=b_gate, rpb=rpb, ln_v_g=ln_v_g, ln_v_b=ln_v_b, w_s=w_s, b_s=b_s, w_pa=w_pa, w_pb=w_pb, w_o=w_o,
                   w_ff2_up=w_ff2_up, w_ff2_down=w_ff2_down, final_g=final_g)
    mom_m = dict(c_ctx=m_c_ctx, w_ada=m_w_ada, b_ada=m_b_ada, norm_g=m_norm_g, w_ff1_up=m_w_ff1_up, w_ff1_down=m_w_ff1_down,
                 w_in=m_w_in, b_gate=m_b_gate, rpb=m_rpb, ln_v_g=m_ln_v_g, ln_v_b=m_ln_v_b, w_s=m_w_s, b_s=m_b_s, w_pa=m_w_pa,
                 w_pb=m_w_pb, w_o=m_w_o, w_ff2_up=m_w_ff2_up, w_ff2_down=m_w_ff2_down, final_g=m_final_g)
    mom_v = dict(c_ctx=v_c_ctx, w_ada=v_w_ada, b_ada=v_b_ada, norm_g=v_norm_g, w_ff1_up=v_w_ff1_up, w_ff1_down=v_w_ff1_down,
                 w_in=v_w_in, b_gate=v_b_gate, rpb=v_rpb, ln_v_g=v_ln_v_g, ln_v_b=v_ln_v_b, w_s=v_w_s, b_s=v_b_s, w_pa=v_w_pa,
                 w_pb=v_w_pb, w_o=v_w_o, w_ff2_up=v_w_ff2_up, w_ff2_down=v_w_ff2_down, final_g=v_final_g)
    order = list(weights)
    mx, my, mc = _here()
    dev = 4 * mx + 2 * my + mc
    chip = 2 * mx + my
    depth, d, n_ada = w_ada.shape
    dq = d // 4

    c_all = _allgather8(jnp.pad(c, ((0, 7), (0, 0))), "gather_c")[:, 0, :]
    cond = jnp.concatenate([c_all, c_ctx[None, :], jnp.zeros((7, d), F32)], axis=0)
    b_shard = lax.dynamic_slice(b_ada, (0, chip * n_ada), (depth, n_ada))
    proj = [_ada_fwd(cond, w_ada[l], b_shard[l:l + 1], f"ada{l}") for l in range(depth)]
    silu_c = proj[0][1]
    mods_sh = _allgather8(jnp.concatenate([p[0] for p in proj], axis=0), "gather_mods")
    mods_all = jnp.transpose(mods_sh[0::2].reshape(4, depth, 16, n_ada), (1, 2, 0, 3)).reshape(depth, 16, N_MOD, d)
    mods = jnp.stack([lax.dynamic_index_in_dim(mods_all, dev, axis=1, keepdims=False), mods_all[:, 8]], axis=1)

    shards = [weights[k].astype(MXU) for k in BIG]
    full = _gather_chips(shards, [SHARD_AXIS[k] for k in BIG], "gather_w")
    my_chip = jnp.reshape(chip, (1,)).astype(jnp.int32)
    full = [_place_shard(f, sh, my_chip, SHARD_AXIS[k], "place_" + k) for k, f, sh in zip(BIG, full, shards)]
    wts = dict(zip(BIG, full))
    prm = {k: weights[k] for k in SMALL if k != "norm_g"}
    norm_full = _allgather8(jnp.pad(norm_g.reshape(depth * 3, dq), ((0, 8 - depth * 3), (0, 0))), "gather_norm_g")
    prm["norm_g"] = jnp.transpose(norm_full[0::2, :depth * 3].reshape(4, depth, 3, dq), (1, 2, 0, 3)).reshape(depth, 3, d)

    loss, grad_x, dmods, gw, gp = _local_step(x[0], ctx[0], loss_target[0], mods, wts, prm)
    loss = lax.psum(loss, ("x", "y", "c"))

    small_shapes = [(depth, 2, N_MOD * d)] + [weights[k].shape if k != "norm_g" else (depth, 3, d) for k in SMALL]
    packed = _allgather8(_pack([dmods.reshape(depth, 2, N_MOD * d)] + [gp[k] for k in SMALL]), "gather_small")
    rows = packed.shape[1]
    total = _sum_pieces([packed[i] for i in range(8)], "sum_small")[0].reshape(-1)
    sums = dict(zip(("dmods",) + SMALL, _unpack(total, small_shapes)))
    dmods_dev = _unpack(packed.reshape(8, rows * PACK_LANES), small_shapes[:1])[0]

    g_ada, cc_parts = [], []
    for l in range(depth):
        dm = jnp.concatenate([dmods_dev[:, l, 0], sums["dmods"][l, 1][None], jnp.zeros((7, N_MOD * d), F32)], axis=0)
        dm_sh = lax.dynamic_slice(dm, (0, chip * n_ada), (16, n_ada))
        g_ada.append(_mm(silu_c, dm_sh, "tn", F32, f"ada{l}_dw"))
        cc_parts.append(_mm(dm_sh, w_ada[l], "nt", F32, f"ada{l}_dc")[8:9])
    cc_all = _allgather8(jnp.pad(jnp.concatenate(cc_parts, axis=0), ((0, 8 - depth), (0, 0))), "gather_cctx")
    g_cctx = _cctx_grad(cc_all[0::2, :depth].reshape(4 * depth, d), c_ctx.reshape(1, d), "cctx_grad")

    axes = [SHARD_AXIS[k] for k in BIG]
    from_sibling = _swap_layers([gw[k] for k in BIG], "swap_layer_gw")
    my_layer = jnp.reshape(mc, (1,)).astype(jnp.int32)
    pair = [_pair_sum(gw[k], r, my_layer, "pair_" + k) for k, r in zip(BIG, from_sibling)]
    recv = _scatter_chips([p[1] for p in pair], axes, "scatter_gw")
    mine = []
    for k, ax, p, r in zip(BIG, axes, pair, recv):
        size = p[0].shape[ax] // 4
        own = lax.dynamic_slice_in_dim(p[0], chip * size, size, axis=ax)
        mine.append(_sum_pieces([own, r[0], r[1], r[2]], "sum_" + k)[0])
    other = _sibling_swap(mine, "swap_gw")

    res = {k: _adamw_layers(weights[k], a, b, mom_m[k], mom_v[k], my_layer, "adamw_" + k) for k, a, b in zip(BIG, mine, other)}
    pieces = {"w_ada": [jnp.stack(g_ada)]}
    pieces["b_ada"] = [sums["dmods"][:, 0], sums["dmods"][:, 1]]
    pieces["c_ctx"] = [g_cctx[0]]
    for k in SMALL:
        pieces[k] = [sums[k]]
    pieces["norm_g"] = [lax.dynamic_slice_in_dim(sums["norm_g"], chip * dq, dq, axis=2)]
    res.update({k: _adamw(weights[k], pieces[k], mom_m[k], mom_v[k], "adamw_" + k) for k in pieces})
    return (loss, grad_x[None], *[res[k][0] for k in order], *[res[k][1] for k in order],
            *[res[k][2] for k in order], *[res[k][3] for k in order])
```

```python
import numpy as np
import jax
import jax.numpy as jnp
from jax import lax
from jax.experimental import pallas as pl
from jax.experimental.pallas import tpu as pltpu

F32 = jnp.float32
MXU = jnp.bfloat16
EPS = 1e-6
GRID_W, HEADS, HEAD_DIM = 64, 8, 64
NA_WIDTH = SG_WIDTH = 512
WIN_H, WIN_W = 8, 16
SG_CHUNK, SG_GROUPS = 128, 4
N_MOD = 9
ROPE_THETA = 10000.0
Q_ROWS, K_ROWS = 4, 12
TQ, TK = Q_ROWS * GRID_W, K_ROWS * GRID_W
TM = 256
LANES = 128
NEG = -1e30
VMEM_LIMIT = 56 * 2 ** 20
ADAM_LR, ADAM_B1, ADAM_B2, ADAM_EPS, ADAM_WD, ADAM_STEP = 0.001, 0.9, 0.999, 1e-08, 0.01, 10
MESH = pl.DeviceIdType.MESH
BIG = ("w_ff1_up", "w_ff1_down", "w_in", "w_pa", "w_pb", "w_o", "w_ff2_up", "w_ff2_down")
SHARD_AXIS = {"w_ff1_up": 2, "w_ff1_down": 1, "w_in": 2, "w_pa": 2, "w_pb": 2, "w_o": 1, "w_ff2_up": 2, "w_ff2_down": 1}


def _call(body, *, name, grid, in_specs, out_specs, out_shape, scratch=(), aliases=None):
    return pl.pallas_call(
        body, name=name, grid=grid, in_specs=in_specs, out_specs=out_specs, out_shape=out_shape,
        scratch_shapes=list(scratch), input_output_aliases=aliases or {},
        compiler_params=pltpu.CompilerParams(dimension_semantics=("arbitrary",) * len(grid), vmem_limit_bytes=VMEM_LIMIT))


def _w_dims(w):
    return w[0].shape[1:] if isinstance(w, tuple) else w.shape


def _w_arr(w):
    return w[0] if isinstance(w, tuple) else w


def _w_spec(w, block, index):
    if isinstance(w, tuple):
        layer = w[1]
        return pl.BlockSpec((None, *block), lambda *ids: (layer, *index(*ids)))
    return pl.BlockSpec(block, index)


def _pick(n, prefs):
    for p in prefs:
        if n % p == 0:
            return p
    return n


def _row_tile(t):
    return _pick(t, (640, 256))


def _rows(tm, n, col=0):
    return pl.BlockSpec((tm, n), lambda i: (i, col))


def _fixed(shape):
    return pl.BlockSpec(shape, lambda *_: (0,) * len(shape))


def _sds(shape, dtype):
    return jax.ShapeDtypeStruct(shape, dtype)


def _mm(a, b, mode, out_dtype, name, into=None):
    if mode == "tn":
        r, m = a.shape
        n = b.shape[1]
        tm = _pick(m, (1024, 1408, 704, 512, 256, 128))
        tn = _pick(n, (512, 1408, 256, 128))
        tr = _pick(r, (1280, 640, 512, 256, 128))

        def body(a_ref, b_ref, *rest):
            o_ref = rest[-1]

            @pl.when(pl.program_id(2) == 0)
            def _():
                o_ref[...] = jnp.zeros_like(o_ref)

            o_ref[...] += lax.dot_general(a_ref[...].astype(MXU), b_ref[...].astype(MXU), (((0,), (0,)), ((), ())),
                                          preferred_element_type=F32)

        in_specs = [pl.BlockSpec((tr, tm), lambda i, j, k: (k, i)), pl.BlockSpec((tr, tn), lambda i, j, k: (k, j))]
        if into is None:
            return _call(body, name=name, grid=(m // tm, n // tn, r // tr), in_specs=in_specs,
                         out_specs=pl.BlockSpec((tm, tn), lambda i, j, k: (i, j)), out_shape=_sds((m, n), F32))(a, b)
        buf, shape, layer, col0 = into
        out_spec = pl.BlockSpec((None, tm, tn), lambda i, j, k: (layer, i, j + col0 // tn))
        if buf is None:
            return _call(body, name=name, grid=(m // tm, n // tn, r // tr), in_specs=in_specs, out_specs=out_spec,
                         out_shape=_sds(shape, F32))(a, b)
        return _call(body, name=name, grid=(m // tm, n // tn, r // tr), in_specs=in_specs + [pl.BlockSpec(memory_space=pl.ANY)],
                     out_specs=out_spec, out_shape=_sds(shape, F32), aliases={2: 0})(a, b, buf)
    m, k = a.shape
    n = _w_dims(b)[1] if mode == "nn" else _w_dims(b)[0]
    tm = _pick(m, (1280, 640, 512, 256, 128) if k <= 2816 else (640, 512, 256, 128))
    tn = _pick(n, (512, 1408, 256, 128))
    dims = (((1,), (0,)), ((), ())) if mode == "nn" else (((1,), (1,)), ((), ()))

    def body(a_ref, b_ref, o_ref):
        o_ref[...] = lax.dot_general(a_ref[...].astype(MXU), b_ref[...].astype(MXU), dims,
                                     preferred_element_type=F32).astype(o_ref.dtype)

    b_spec = _w_spec(b, (k, tn), lambda i, j: (0, j)) if mode == "nn" else _w_spec(b, (tn, k), lambda i, j: (j, 0))
    return _call(body, name=name, grid=(m // tm, n // tn), in_specs=[pl.BlockSpec((tm, k), lambda i, j: (i, 0)), b_spec],
                 out_specs=pl.BlockSpec((tm, tn), lambda i, j: (i, j)), out_shape=_sds((m, n), out_dtype))(a, _w_arr(b))


def _row_chunks(tm):
    rc = _pick(tm, (256, 128))
    return [slice(r, r + rc) for r in range(0, tm, rc)]


def _ctx_rows(i, tm, s):
    return (i * tm + lax.broadcasted_iota(jnp.int32, (tm, 1), 0)) >= s


def _mod_row(m_ref, k, ctx):
    return jnp.where(ctx, m_ref[1, k:k + 1, :], m_ref[0, k:k + 1, :])


def _stream_sums(i, tm, s, refs_and_vals):
    @pl.when((i + 1) * tm <= s)
    def _():
        for ref, val in refs_and_vals:
            ref[0] += jnp.sum(val, axis=0, keepdims=True)

    @pl.when((i + 1) * tm > s)
    def _():
        ctx = _ctx_rows(i, tm, s)
        for ref, val in refs_and_vals:
            ref[0] += jnp.sum(jnp.where(ctx, 0.0, val), axis=0, keepdims=True)
            ref[1] += jnp.sum(jnp.where(ctx, val, 0.0), axis=0, keepdims=True)


def _norm_mm(xs, g, mods, k_shift, k_scale, w, s, name, glu):
    t, d = xs.shape
    n = _w_dims(w)[1] // 2 if glu else _w_dims(w)[1]
    tm = _pick(t, (1280, 640, 256))
    tn = _pick(n, (256, 128)) if glu else _pick(n, (512, 256, 128))
    nj = n // tn

    def body(x_ref, g_ref, m_ref, *refs):
        i, j = pl.program_id(0), pl.program_id(1)
        w_refs, h_ref, o_refs = refs[:2 if glu else 1], refs[2 if glu else 1], refs[3 if glu else 2:]

        @pl.when(j == 0)
        def _():
            x = x_ref[...]
            rstd = lax.rsqrt(jnp.mean(x * x, axis=-1, keepdims=True) + EPS)
            ctx = _ctx_rows(i, tm, s)
            h = x * rstd * g_ref[...] * (1.0 + _mod_row(m_ref, k_scale, ctx)) + _mod_row(m_ref, k_shift, ctx)
            h_ref[...] = h.astype(h_ref.dtype)

        for rows in _row_chunks(tm):
            h = h_ref[rows, :]
            a = _nn(h, w_refs[0][...])
            if glu:
                b = _nn(h, w_refs[1][...])
                sg = jax.nn.sigmoid(a)
                silu = a * sg
                o_refs[0][rows, :] = (b * sg * (1.0 + a * (1.0 - sg))).astype(o_refs[0].dtype)
                o_refs[1][rows, :] = silu.astype(o_refs[1].dtype)
                o_refs[2][rows, :] = (silu * b).astype(o_refs[2].dtype)
            else:
                o_refs[0][rows, :] = a.astype(o_refs[0].dtype)

    tile = pl.BlockSpec((tm, tn), lambda i, j: (i, j))
    row = pl.BlockSpec((tm, d), lambda i, j: (i, 0))
    w_specs = [_w_spec(w, (d, tn), lambda i, j: (0, j))] + ([_w_spec(w, (d, tn), lambda i, j: (0, j + nj))] if glu else [])
    n_out = 3 if glu else 1
    return _call(body, name=name, grid=(t // tm, nj),
                 in_specs=[row, _fixed((1, d)), _fixed((2, N_MOD, d))] + w_specs,
                 out_specs=[row] + [tile] * n_out,
                 out_shape=[_sds((t, d), MXU)] + [_sds((t, n), MXU)] * n_out)(xs, g, mods, *([_w_arr(w)] * (2 if glu else 1)))


def _mm_res(a, w, xs, mods, k_gate, coef, s, name):
    t, k = a.shape
    d = _w_dims(w)[1]
    tm = _pick(t, (1280, 640, 256))
    tn = _pick(d, (512, 256, 128))

    def body(a_ref, w_ref, x_ref, m_ref, y_ref, o_ref):
        y = _nn(a_ref[...], w_ref[...])
        y_ref[...] = y.astype(y_ref.dtype)
        gate = _mod_row(m_ref, k_gate, _ctx_rows(pl.program_id(0), tm, s))
        o_ref[...] = x_ref[...] + (coef * gate) * y

    tile = pl.BlockSpec((tm, tn), lambda i, j: (i, j))
    return _call(body, name=name, grid=(t // tm, d // tn),
                 in_specs=[pl.BlockSpec((tm, k), lambda i, j: (i, 0)), _w_spec(w, (k, tn), lambda i, j: (0, j)), tile,
                           pl.BlockSpec((2, N_MOD, tn), lambda i, j: (0, 0, j))],
                 out_specs=[tile, tile], out_shape=[_sds((t, d), MXU), _sds((t, d), F32)])(a, _w_arr(w), xs, mods)


def _resb_mm(dxn, y, mods, k_gate, coef, w, s, name, ups=None):
    t, d = dxn.shape
    n = _w_dims(w)[0]
    tm = _pick(t, (1280, 640, 256))
    tn = _pick(n, (256, 128)) if ups else _pick(n, (512, 256, 128))

    def body(dx_ref, y_ref, m_ref, w_ref, *refs):
        i, j = pl.program_id(0), pl.program_id(1)
        u_refs, (dy_ref, dgt_ref), o_refs = (refs[:2], refs[2:4], refs[4:]) if ups else ((), refs[:2], refs[2:])

        @pl.when((i == 0) & (j == 0))
        def _():
            dgt_ref[...] = jnp.zeros_like(dgt_ref)

        @pl.when(j == 0)
        def _():
            dx = dx_ref[...]
            gate = _mod_row(m_ref, k_gate, _ctx_rows(i, tm, s))
            dy_ref[...] = ((coef * gate) * dx).astype(dy_ref.dtype)
            _stream_sums(i, tm, s, [(dgt_ref, coef * y_ref[...].astype(F32) * dx)])

        dact = _nt(dy_ref[...], w_ref[...])
        if ups:
            o_refs[0][...] = (dact * u_refs[0][...].astype(F32)).astype(o_refs[0].dtype)
            o_refs[1][...] = (dact * u_refs[1][...].astype(F32)).astype(o_refs[1].dtype)
        else:
            o_refs[0][...] = dact.astype(o_refs[0].dtype)

    tile = pl.BlockSpec((tm, tn), lambda i, j: (i, j))
    row = pl.BlockSpec((tm, d), lambda i, j: (i, 0))
    n_out = 2 if ups else 1
    return _call(body, name=name, grid=(t // tm, n // tn),
                 in_specs=[row, row, _fixed((2, N_MOD, d)), _w_spec(w, (tn, d), lambda i, j: (j, 0))] + ([tile, tile] if ups else []),
                 out_specs=[row, _fixed((2, 1, d))] + [tile] * n_out,
                 out_shape=[_sds((t, d), MXU), _sds((2, 1, d), F32)] + [_sds((t, n), MXU)] * n_out)(
                     dxn, y, mods, _w_arr(w), *(ups or ()))


def _mm_normb(a_list, w, xs, dres, g, mods, k_scale, s, name):
    t, d = xs.shape
    tm = _pick(t, (640, 256))
    n_a = len(a_list)
    widths = [a.shape[1] for a in a_list]
    tk = next(c for c in (2816, 1408, 1536, 1024, 512, 256, 128) if all(wd % c == 0 for wd in widths))
    counts = [wd // tk for wd in widths]
    starts = [sum(counts[:q]) for q in range(n_a)]
    nk = sum(counts)
    k_all = sum(widths)
    resident = 4 * k_all * (d + tm) + 24 * tm * d <= 48 * 2 ** 20

    def finish(i, dh, x_ref, dr_ref, g_ref, m_ref, dx_ref, dsh_ref, dsc_ref, dg_ref):
        x = x_ref[...]
        rstd = lax.rsqrt(jnp.mean(x * x, axis=-1, keepdims=True) + EPS)
        xhat = x * rstd
        gg = g_ref[...]
        _stream_sums(i, tm, s, [(dsh_ref, dh), (dsc_ref, dh * (xhat * gg))])
        dy = dh * (1.0 + _mod_row(m_ref, k_scale, _ctx_rows(i, tm, s)))
        dg_ref[...] += jnp.sum(dy * xhat, axis=0, keepdims=True)
        dxh = dy * gg
        dx_ref[...] = dr_ref[...] + rstd * (dxh - xhat * jnp.mean(dxh * xhat, axis=-1, keepdims=True))

    out_shape = [_sds((t, d), F32), _sds((2, 1, d), F32), _sds((2, 1, d), F32), _sds((1, d), F32)]
    if resident:
        def body_resident(*refs):
            a_refs, (w_ref, x_ref, dr_ref, g_ref, m_ref, dx_ref, dsh_ref, dsc_ref, dg_ref) = refs[:n_a], refs[n_a:]
            i = pl.program_id(0)

            @pl.when(i == 0)
            def _():
                dsh_ref[...] = jnp.zeros_like(dsh_ref)
                dsc_ref[...] = jnp.zeros_like(dsc_ref)
                dg_ref[...] = jnp.zeros_like(dg_ref)

            dh, off = None, 0
            for q in range(n_a):
                part = _nt(a_refs[q][...], w_ref[:, off:off + widths[q]])
                dh = part if dh is None else dh + part
                off += widths[q]
            finish(i, dh, x_ref, dr_ref, g_ref, m_ref, dx_ref, dsh_ref, dsc_ref, dg_ref)

        rows = pl.BlockSpec((tm, d), lambda i: (i, 0))
        return _call(body_resident, name=name, grid=(t // tm,),
                     in_specs=[_rows(tm, wd) for wd in widths] + [_w_spec(w, (d, k_all), lambda i: (0, 0)), rows, rows,
                                                                  _fixed((1, d)), _fixed((2, N_MOD, d))],
                     out_specs=[rows, _fixed((2, 1, d)), _fixed((2, 1, d)), _fixed((1, d))],
                     out_shape=out_shape)(*a_list, _w_arr(w), xs, dres, g, mods)

    def body(*refs):
        a_refs, (w_ref, x_ref, dr_ref, g_ref, m_ref, dx_ref, dsh_ref, dsc_ref, dg_ref, acc) = refs[:n_a], refs[n_a:]
        i, k = pl.program_id(0), pl.program_id(1)

        @pl.when((i == 0) & (k == 0))
        def _():
            dsh_ref[...] = jnp.zeros_like(dsh_ref)
            dsc_ref[...] = jnp.zeros_like(dsc_ref)
            dg_ref[...] = jnp.zeros_like(dg_ref)

        @pl.when(k == 0)
        def _():
            acc[...] = jnp.zeros_like(acc)

        for q in range(n_a):
            @pl.when((k >= starts[q]) & (k < starts[q] + counts[q]))
            def _():
                acc[...] += _nt(a_refs[q][...], w_ref[...])

        @pl.when(k == nk - 1)
        def _():
            finish(i, acc[...], x_ref, dr_ref, g_ref, m_ref, dx_ref, dsh_ref, dsc_ref, dg_ref)

    row = pl.BlockSpec((tm, d), lambda i, k: (i, 0))
    a_specs = [pl.BlockSpec((tm, tk), lambda i, k, q=q: (i, jnp.clip(k - starts[q], 0, counts[q] - 1))) for q in range(n_a)]
    return _call(body, name=name, grid=(t // tm, nk),
                 in_specs=a_specs + [_w_spec(w, (d, tk), lambda i, k: (0, k)), row, row, _fixed((1, d)), _fixed((2, N_MOD, d))],
                 out_specs=[row, _fixed((2, 1, d)), _fixed((2, 1, d)), _fixed((1, d))], out_shape=out_shape,
                 scratch=[pltpu.VMEM((tm, d), F32)])(*a_list, _w_arr(w), xs, dres, g, mods)


def _rope_tables(s, ctx_len):
    n_freq = HEAD_DIM // 4
    tok = jnp.arange(s)
    freqs = ROPE_THETA ** (-jnp.arange(n_freq, dtype=F32) / n_freq)
    ang = jnp.concatenate([(tok // GRID_W).astype(F32)[:, None] * freqs, (tok % GRID_W).astype(F32)[:, None] * freqs], axis=-1)
    cos = jnp.repeat(jnp.cos(ang), 2, axis=-1)
    sin = jnp.repeat(jnp.sin(ang), 2, axis=-1) * jnp.tile(jnp.array([-1.0, 1.0], F32), HEAD_DIM // 2)
    cos = jnp.concatenate([jnp.tile(cos, (1, LANES // HEAD_DIM)), jnp.ones((ctx_len, LANES), F32)], axis=0)
    sin = jnp.concatenate([jnp.tile(sin, (1, LANES // HEAD_DIM)), jnp.zeros((ctx_len, LANES), F32)], axis=0)
    return cos, sin


def _swap_pairs(x):
    n = x.shape[-1]
    lane = lax.broadcasted_iota(jnp.int32, x.shape, 1)
    return jnp.where(lane % 2 == 0, pltpu.roll(x, n - 1, 1), pltpu.roll(x, 1, 1))


def _rope(p, cos, sin, name):
    t = p.shape[0]
    w = NA_WIDTH
    te = _row_tile(t)

    def body(q_ref, k_ref, v_ref, c_ref, s_ref, qo_ref, ko_ref, vo_ref):
        c, s = c_ref[...], s_ref[...]
        for hp in range(w // LANES):
            cols = slice(hp * LANES, (hp + 1) * LANES)
            q, k = q_ref[:, cols].astype(F32), k_ref[:, cols].astype(F32)
            qo_ref[:, cols] = (q * c + _swap_pairs(q) * s).astype(qo_ref.dtype)
            ko_ref[:, cols] = (k * c + _swap_pairs(k) * s).astype(ko_ref.dtype)
        vo_ref[...] = v_ref[...].astype(vo_ref.dtype)

    return _call(body, name=name, grid=(t // te,),
                 in_specs=[_rows(te, w, 0), _rows(te, w, 1), _rows(te, w, 2), _rows(te, LANES), _rows(te, LANES)],
                 out_specs=[_rows(te, w)] * 3, out_shape=[_sds((t, w), MXU)] * 3)(p, p, p, cos, sin)


def _rope_bwd(dq, dk, dv, cos, sin, name):
    t = dq.shape[0]
    te = _row_tile(t)
    n_pairs = NA_WIDTH // LANES

    def body(dq_ref, dk_ref, dv_ref, c_ref, s_ref, qo_ref, ko_ref, vo_ref):
        c, s = c_ref[...], s_ref[...]
        for hp in range(n_pairs):
            cols = slice(hp * LANES, (hp + 1) * LANES)
            a, b = dq_ref[:, cols], dk_ref[hp]
            qo_ref[:, cols] = (a * c + _swap_pairs(a * s)).astype(qo_ref.dtype)
            ko_ref[:, cols] = (b * c + _swap_pairs(b * s)).astype(ko_ref.dtype)
            vo_ref[:, cols] = dv_ref[hp].astype(vo_ref.dtype)

    pairs = pl.BlockSpec((n_pairs, te, LANES), lambda i: (0, i, 0))
    return _call(body, name=name, grid=(t // te,),
                 in_specs=[_rows(te, NA_WIDTH), pairs, pairs, _rows(te, LANES), _rows(te, LANES)],
                 out_specs=[_rows(te, NA_WIDTH)] * 3, out_shape=[_sds((t, NA_WIDTH), MXU)] * 3)(dq, dk, dv, cos, sin)


def _na_geometry(r_grid):
    rows = []
    for r0, ks in ((0, 0), (Q_ROWS, 0), (r_grid - Q_ROWS, r_grid - K_ROWS)):
        dr = np.zeros((Q_ROWS, K_ROWS), np.int32)
        vr = np.zeros((Q_ROWS, K_ROWS), bool)
        for a in range(Q_ROWS):
            r = r0 + a
            rs = min(max(r - WIN_H // 2, 0), r_grid - WIN_H)
            for i in range(K_ROWS):
                kr = ks + i
                vr[a, i] = rs <= kr <= rs + WIN_H - 1
                dr[a, i] = kr - r + WIN_H - 1
        rows.append((dr, vr))
    c = np.arange(GRID_W)
    cs = np.clip(c - WIN_W // 2, 0, GRID_W - WIN_W)
    kc = np.arange(GRID_W)
    vc = (kc[None, :] >= cs[:, None]) & (kc[None, :] <= cs[:, None] + WIN_W - 1)
    dc = kc[None, :] - c[:, None] + WIN_W - 1
    return rows, dc, vc


def _bias_table(rpb, r_grid):
    rows, _, vc = _na_geometry(r_grid)
    n_dc, off = 2 * WIN_W - 1, GRID_W - WIN_W
    u = jnp.pad(rpb, ((0, 0), (0, 0), (off, 2 * GRID_W - 1 - off - n_dc)))
    toep = jnp.stack([u[:, :, GRID_W - 1 - c:2 * GRID_W - 1 - c] for c in range(GRID_W)], axis=1)
    toep = jnp.pad(toep, ((0, 0), (0, 0), (Q_ROWS, Q_ROWS), (0, 0)))
    tabs = []
    for dr, vr in rows:
        per_row = []
        for a in range(Q_ROWS):
            lo = int(dr[a, 0]) + Q_ROWS
            blocks = [jnp.where(vc[None], toep[:, :, lo + i, :], NEG) if vr[a, i] else jnp.full((HEADS, GRID_W, GRID_W), NEG, F32)
                      for i in range(K_ROWS)]
            per_row.append(jnp.concatenate(blocks, axis=-1))
        tabs.append(jnp.stack(per_row, axis=1).reshape(HEADS, TQ, TK))
    tabs.append(jnp.full((HEADS, TQ, TK), NEG, F32))
    return jnp.stack(tabs)


def _variant(g, ngx):
    return jnp.where(g == 0, 0, jnp.where(g >= ngx, 3, jnp.where(g == ngx - 1, 2, 1)))


def _key_start(g, r_grid):
    return pl.multiple_of(jnp.clip(g * Q_ROWS - WIN_H // 2, 0, r_grid - K_ROWS) * GRID_W, TQ)


def _nt(a, b):
    return lax.dot_general(a, b, (((1,), (1,)), ((), ())), preferred_element_type=F32)


def _tn(a, b):
    return lax.dot_general(a, b, (((0,), (0,)), ((), ())), preferred_element_type=F32)


def _nn(a, b):
    return jnp.dot(a, b, preferred_element_type=F32)


def _head_mask(h):
    lane = lax.broadcasted_iota(jnp.int32, (1, LANES), 1)
    return ((lane >= HEAD_DIM * h) & (lane < HEAD_DIM * (h + 1))).astype(F32)


def _softmax_parts(qm, knb, kcx, bias):
    s_nb = _nt(qm, knb) + bias
    s_cx = _nt(qm, kcx)
    m = jnp.maximum(jnp.max(s_nb, axis=-1, keepdims=True), jnp.max(s_cx, axis=-1, keepdims=True))
    e_nb = jnp.exp(s_nb - m)
    e_cx = jnp.exp(s_cx - m)
    inv = 1.0 / (jnp.sum(e_nb, axis=-1, keepdims=True) + jnp.sum(e_cx, axis=-1, keepdims=True))
    return e_nb, e_cx, inv


def _na_specs(t, ngx):
    q_spec = pl.BlockSpec((TQ, LANES), lambda hp, g: (g, hp))
    kv_spec = pl.BlockSpec((t, LANES), lambda hp, g: (0, hp))
    b_spec = pl.BlockSpec((1, 2, TQ, TK), lambda hp, g: (_variant(g, ngx), hp, 0, 0))
    return q_spec, kv_spec, b_spec


def _na_fwd(q, k, v, bias, s, name):
    t = q.shape[0]
    ctx_len = t - s
    r_grid = s // GRID_W
    q_spec, kv_spec, b_spec = _na_specs(t, s // TQ)

    def body(q_ref, k_ref, v_ref, b_ref, o_ref):
        start = _key_start(pl.program_id(1), r_grid)
        qf = q_ref[...].astype(F32) * (HEAD_DIM ** -0.5)
        knb, vnb = k_ref[pl.ds(start, TK), :], v_ref[pl.ds(start, TK), :]
        kcx, vcx = k_ref[pl.ds(s, ctx_len), :], v_ref[pl.ds(s, ctx_len), :]
        acc = jnp.zeros((TQ, LANES), F32)
        for h in range(2):
            mask = _head_mask(h)
            e_nb, e_cx, inv = _softmax_parts((qf * mask).astype(MXU), knb, kcx, b_ref[0, h])
            acc += (_nn(e_nb.astype(MXU), vnb) + _nn(e_cx.astype(MXU), vcx)) * (inv * mask)
        o_ref[...] = acc.astype(o_ref.dtype)

    return _call(body, name=name, grid=(NA_WIDTH // LANES, t // TQ), in_specs=[q_spec, kv_spec, kv_spec, b_spec],
                 out_specs=q_spec, out_shape=_sds((t, NA_WIDTH), MXU))(q, k, v, bias)


def _na_bwd(q, k, v, do, bias, s, name):
    t = q.shape[0]
    ctx_len = t - s
    r_grid = s // GRID_W
    ng, ngx = t // TQ, s // TQ
    q_spec, kv_spec, b_spec = _na_specs(t, ngx)

    def body(q_ref, k_ref, v_ref, do_ref, b_ref, dq_ref, dk_hbm, dv_hbm, db_ref, dk_acc, dv_acc):
        hp, g = pl.program_id(0), pl.program_id(1)
        start = _key_start(g, r_grid)

        @pl.when(g == 0)
        def _():
            dk_acc[...] = jnp.zeros_like(dk_acc)
            dv_acc[...] = jnp.zeros_like(dv_acc)

        @pl.when((g == 0) | (g == 1) | (g == ngx - 1) | (g == ngx))
        def _():
            db_ref[...] = jnp.zeros_like(db_ref)

        qf = q_ref[...].astype(F32) * (HEAD_DIM ** -0.5)
        do = do_ref[...].astype(F32)
        knb, vnb = k_ref[pl.ds(start, TK), :], v_ref[pl.ds(start, TK), :]
        kcx, vcx = k_ref[pl.ds(s, ctx_len), :], v_ref[pl.ds(s, ctx_len), :]
        dq = jnp.zeros((TQ, LANES), F32)
        dk_nb = jnp.zeros((TK, LANES), F32)
        dv_nb = jnp.zeros((TK, LANES), F32)
        dk_cx = jnp.zeros((ctx_len, LANES), F32)
        dv_cx = jnp.zeros((ctx_len, LANES), F32)
        for h in range(2):
            mask = _head_mask(h)
            qm = (qf * mask).astype(MXU)
            dom = (do * mask).astype(MXU)
            e_nb, e_cx, inv = _softmax_parts(qm, knb, kcx, b_ref[0, h])
            dp_nb = _nt(dom, vnb)
            dp_cx = _nt(dom, vcx)
            delta = inv * (jnp.sum(e_nb * dp_nb, axis=-1, keepdims=True) + jnp.sum(e_cx * dp_cx, axis=-1, keepdims=True))
            ds_nb = e_nb * (inv * (dp_nb - delta))
            ds_cx = e_cx * (inv * (dp_cx - delta))
            db_ref[0, h] += ds_nb
            ds_nb, ds_cx = ds_nb.astype(MXU), ds_cx.astype(MXU)
            dq += (_nn(ds_nb, knb) + _nn(ds_cx, kcx)) * (mask * (HEAD_DIM ** -0.5))
            dk_nb += _tn(ds_nb, qm)
            dk_cx += _tn(ds_cx, qm)
            dom = (do * (inv * mask)).astype(MXU)
            dv_nb += _tn(e_nb.astype(MXU), dom)
            dv_cx += _tn(e_cx.astype(MXU), dom)
        dq_ref[...] = dq
        dk_acc[pl.ds(start, TK), :] += dk_nb
        dv_acc[pl.ds(start, TK), :] += dv_nb
        dk_acc[pl.ds(s, ctx_len), :] += dk_cx
        dv_acc[pl.ds(s, ctx_len), :] += dv_cx

        @pl.when(g == ng - 1)
        def _():
            pltpu.sync_copy(dk_acc, dk_hbm.at[hp])
            pltpu.sync_copy(dv_acc, dv_hbm.at[hp])

    n_pairs = NA_WIDTH // LANES
    hbm = pl.BlockSpec(memory_space=pl.ANY)
    return _call(body, name=name, grid=(n_pairs, ng), in_specs=[q_spec, kv_spec, kv_spec, q_spec, b_spec],
                 out_specs=[q_spec, hbm, hbm, b_spec],
                 out_shape=[_sds((t, NA_WIDTH), F32), _sds((n_pairs, t, LANES), F32), _sds((n_pairs, t, LANES), F32),
                            _sds((4, HEADS, TQ, TK), F32)],
                 scratch=[pltpu.VMEM((t, LANES), F32), pltpu.VMEM((t, LANES), F32)])(q, k, v, do, bias)


def _rpb_grad(dbias, r_grid, name):
    rows, _, _ = _na_geometry(r_grid)
    n_dr, half, skew, lanes = 2 * WIN_H - 1, WIN_W - 1, TK + 2, 896
    z = dbias[:3].reshape(3, HEADS, Q_ROWS, GRID_W, TK)
    z = jnp.pad(z, ((0, 0),) * 4 + ((0, 1),)).reshape(3, HEADS, Q_ROWS, GRID_W * (TK + 1))
    z = jnp.pad(z, ((0, 0),) * 3 + ((0, GRID_W),)).reshape(3, HEADS, Q_ROWS, GRID_W, skew)
    z = jnp.pad(z, ((0, 0),) * 4 + ((0, lanes - skew),))

    def body(z_ref, o_ref):
        sums = [jnp.sum(z_ref[v, 0, a], axis=0, keepdims=True) for v in range(3) for a in range(Q_ROWS)]
        zs = jnp.concatenate(sums + [jnp.zeros((16 - 3 * Q_ROWS, lanes), F32)], axis=0)
        acc = [jnp.zeros((1, lanes), F32) for _ in range(n_dr)]
        for i in range(K_ROWS):
            if i == 0:
                at0 = pltpu.roll(zs, lanes - (skew - half), 1) + pltpu.roll(zs, half, 1)
            else:
                at0 = pltpu.roll(zs, lanes - (i * GRID_W - half), 1)
            for v, (dr, vr) in enumerate(rows):
                for a in range(Q_ROWS):
                    if vr[a, i]:
                        acc[dr[a, i]] = acc[dr[a, i]] + at0[v * Q_ROWS + a:v * Q_ROWS + a + 1, :]
        o_ref[0] = jnp.concatenate(acc + [jnp.zeros((1, lanes), F32)], axis=0)

    o = _call(body, name=name, grid=(HEADS,),
              in_specs=[pl.BlockSpec((3, 1, Q_ROWS, GRID_W, lanes), lambda h: (0, h, 0, 0, 0))],
              out_specs=pl.BlockSpec((1, 16, lanes), lambda h: (h, 0, 0)), out_shape=_sds((HEADS, 16, lanes), F32))(z)
    return o[:, :n_dr, :2 * WIN_W - 1]


_GELU_K, _GELU_C = 0.7978845608028654, 0.044715


def _gelu(x):
    return 0.5 * x * (1.0 + jnp.tanh(_GELU_K * (x + _GELU_C * x * x * x)))


def _gelu_grad(x):
    th = jnp.tanh(_GELU_K * (x + _GELU_C * x * x * x))
    return 0.5 * (1.0 + th) + 0.5 * x * (1.0 - th * th) * (_GELU_K * (1.0 + 3.0 * _GELU_C * x * x))


def _ln_stats(v):
    mu = jnp.mean(v, axis=-1, keepdims=True)
    vc = v - mu
    rstd = lax.rsqrt(jnp.mean(vc * vc, axis=-1, keepdims=True) + EPS)
    return vc * rstd, rstd


def _gmlp(p, ln_g, ln_b, w_s, b_s, name):
    t = p.shape[0]
    te = _row_tile(t)
    w = SG_WIDTH
    cw = w // SG_GROUPS

    def body(u_ref, v_ref, g_ref, b_ref, ws_ref, bs_ref, o_ref):
        xhat, _ = _ln_stats(_gelu(v_ref[...].astype(F32)))
        vn = (xhat * g_ref[...] + b_ref[...]).astype(MXU)
        ug = _gelu(u_ref[...].astype(F32))
        for ci in range(te // SG_CHUNK):
            rs = slice(ci * SG_CHUNK, (ci + 1) * SG_CHUNK)
            for gi in range(SG_GROUPS):
                cs = slice(gi * cw, (gi + 1) * cw)
                sg = _nn(ws_ref[gi].astype(MXU), vn[rs, cs]) + bs_ref[gi]
                o_ref[rs, cs] = (ug[rs, cs] * sg).astype(o_ref.dtype)

    return _call(body, name=name, grid=(t // te,),
                 in_specs=[_rows(te, w, 3), _rows(te, w, 4), _fixed((1, w)), _fixed((1, w)),
                           _fixed((SG_GROUPS, SG_CHUNK, SG_CHUNK)), _fixed((SG_GROUPS, SG_CHUNK, 1))],
                 out_specs=_rows(te, w), out_shape=_sds((t, w), MXU))(p, p, ln_g, ln_b, w_s, b_s)


def _gmlp_bwd(p, dob, ln_g, ln_b, w_s, b_s, name):
    t = p.shape[0]
    te = _row_tile(t)
    w = SG_WIDTH
    cw = w // SG_GROUPS

    def body(u_ref, v_ref, do_ref, g_ref, b_ref, ws_ref, bs_ref, du_ref, dv_ref, dws_ref, dbs_ref, dg_ref, db_ref, dvn_ref):
        @pl.when(pl.program_id(0) == 0)
        def _():
            dws_ref[...] = jnp.zeros_like(dws_ref)
            dbs_ref[...] = jnp.zeros_like(dbs_ref)
            dg_ref[...] = jnp.zeros_like(dg_ref)
            db_ref[...] = jnp.zeros_like(db_ref)

        u, v = u_ref[...].astype(F32), v_ref[...].astype(F32)
        xhat, rstd = _ln_stats(_gelu(v))
        vn = (xhat * g_ref[...] + b_ref[...]).astype(MXU)
        ug = _gelu(u)
        dob = do_ref[...].astype(F32)
        for ci in range(te // SG_CHUNK):
            rs = slice(ci * SG_CHUNK, (ci + 1) * SG_CHUNK)
            for gi in range(SG_GROUPS):
                cs = slice(gi * cw, (gi + 1) * cw)
                wsg = ws_ref[gi].astype(MXU)
                sg = _nn(wsg, vn[rs, cs]) + bs_ref[gi]
                du_ref[rs, cs] = (dob[rs, cs] * sg * _gelu_grad(u[rs, cs])).astype(du_ref.dtype)
                ds = dob[rs, cs] * ug[rs, cs]
                dbs_ref[gi] += jnp.sum(ds, axis=-1, keepdims=True)
                ds = ds.astype(MXU)
                dws_ref[gi] += _nt(ds, vn[rs, cs])
                dvn_ref[rs, cs] = _tn(wsg, ds)
        dvn = dvn_ref[...]
        dg_ref[...] += jnp.sum(dvn * xhat, axis=0, keepdims=True)
        db_ref[...] += jnp.sum(dvn, axis=0, keepdims=True)
        dxh = dvn * g_ref[...]
        dvg = rstd * (dxh - jnp.mean(dxh, axis=-1, keepdims=True) - xhat * jnp.mean(dxh * xhat, axis=-1, keepdims=True))
        dv_ref[...] = (dvg * _gelu_grad(v)).astype(dv_ref.dtype)

    return _call(body, name=name, grid=(t // te,),
                 in_specs=[_rows(te, w, 3), _rows(te, w, 4), _rows(te, w), _fixed((1, w)), _fixed((1, w)),
                           _fixed((SG_GROUPS, SG_CHUNK, SG_CHUNK)), _fixed((SG_GROUPS, SG_CHUNK, 1))],
                 out_specs=[_rows(te, w), _rows(te, w), _fixed((SG_GROUPS, SG_CHUNK, SG_CHUNK)),
                            _fixed((SG_GROUPS, SG_CHUNK, 1)), _fixed((1, w)), _fixed((1, w))],
                 out_shape=[_sds((t, w), MXU), _sds((t, w), MXU), _sds((SG_GROUPS, SG_CHUNK, SG_CHUNK), F32),
                            _sds((SG_GROUPS, SG_CHUNK, 1), F32), _sds((1, w), F32), _sds((1, w), F32)],
                 scratch=[pltpu.VMEM((te, w), F32)])(p, p, dob, ln_g, ln_b, w_s, b_s)


def _merge(pa, pb, p, b_gate, name):
    t, d = pa.shape
    te = _row_tile(t)
    hw = NA_WIDTH
    nh = d // hw
    c0 = (NA_WIDTH * 3 + SG_WIDTH * 2) // hw

    def body(pa_ref, pb_ref, la_ref, lb_ref, ba_ref, bb_ref, o_ref):
        ga = jax.nn.sigmoid(la_ref[...].astype(F32) + ba_ref[...])
        gb = jax.nn.sigmoid(lb_ref[...].astype(F32) + bb_ref[...])
        o_ref[...] = (ga * pa_ref[...].astype(F32) + gb * pb_ref[...].astype(F32)).astype(o_ref.dtype)

    tile = pl.BlockSpec((te, hw), lambda i, j: (i, j))
    return _call(body, name=name, grid=(t // te, nh),
                 in_specs=[tile, tile, pl.BlockSpec((te, hw), lambda i, j: (i, c0 + j)),
                           pl.BlockSpec((te, hw), lambda i, j: (i, c0 + nh + j)),
                           pl.BlockSpec((1, hw), lambda i, j: (0, j)), pl.BlockSpec((1, hw), lambda i, j: (0, nh + j))],
                 out_specs=tile, out_shape=_sds((t, d), MXU))(pa, pb, p, p, b_gate, b_gate)


def _merge_bwd(dmg, pa, pb, p, b_gate, name):
    t, d = pa.shape
    te = _row_tile(t)
    hw = NA_WIDTH
    nh = d // hw
    c0 = (NA_WIDTH * 3 + SG_WIDTH * 2) // hw

    def body(dm_ref, pa_ref, pb_ref, la_ref, lb_ref, ba_ref, bb_ref, dpa_ref, dpb_ref, dla_ref, dlb_ref, dba_ref, dbb_ref):
        @pl.when(pl.program_id(1) == 0)
        def _():
            dba_ref[...] = jnp.zeros_like(dba_ref)
            dbb_ref[...] = jnp.zeros_like(dbb_ref)

        dm = dm_ref[...].astype(F32)
        ga = jax.nn.sigmoid(la_ref[...].astype(F32) + ba_ref[...])
        gb = jax.nn.sigmoid(lb_ref[...].astype(F32) + bb_ref[...])
        dpa_ref[...] = (dm * ga).astype(dpa_ref.dtype)
        dpb_ref[...] = (dm * gb).astype(dpb_ref.dtype)
        dla = dm * pa_ref[...].astype(F32) * ga * (1.0 - ga)
        dlb = dm * pb_ref[...].astype(F32) * gb * (1.0 - gb)
        dla_ref[...] = dla.astype(dla_ref.dtype)
        dlb_ref[...] = dlb.astype(dlb_ref.dtype)
        dba_ref[...] += jnp.sum(dla, axis=0, keepdims=True)
        dbb_ref[...] += jnp.sum(dlb, axis=0, keepdims=True)

    tile = pl.BlockSpec((te, hw), lambda j, i: (i, j))
    bias_a = pl.BlockSpec((1, hw), lambda j, i: (0, j))
    bias_b = pl.BlockSpec((1, hw), lambda j, i: (0, nh + j))
    return _call(body, name=name, grid=(nh, t // te),
                 in_specs=[tile, tile, tile, pl.BlockSpec((te, hw), lambda j, i: (i, c0 + j)),
                           pl.BlockSpec((te, hw), lambda j, i: (i, c0 + nh + j)), bias_a, bias_b],
                 out_specs=[tile, tile, tile, tile, bias_a, bias_a],
                 out_shape=[_sds((t, d), MXU)] * 4 + [_sds((1, d), F32)] * 2)(dmg, pa, pb, p, p, b_gate, b_gate)


def _final(xs, tgt, g, name):
    t, d = xs.shape
    nx = tgt.shape[0] // TM

    def body(x_ref, t_ref, g_ref, l_ref, dx_ref, dg_ref):
        i = pl.program_id(0)

        @pl.when(i == 0)
        def _():
            l_ref[...] = jnp.zeros_like(l_ref)
            dg_ref[...] = jnp.zeros_like(dg_ref)

        @pl.when(i < nx)
        def _():
            x = x_ref[...]
            rstd = lax.rsqrt(jnp.mean(x * x, axis=-1, keepdims=True) + EPS)
            xhat = x * rstd
            err = xhat * g_ref[...] - t_ref[...]
            l_ref[...] += 0.5 * jnp.sum(jnp.mean(err * err, axis=-1, keepdims=True))
            dy = err * (1.0 / d)
            dg_ref[...] += jnp.sum(dy * xhat, axis=0, keepdims=True)
            dxh = dy * g_ref[...]
            dx_ref[...] = rstd * (dxh - xhat * jnp.mean(dxh * xhat, axis=-1, keepdims=True))

        @pl.when(i >= nx)
        def _():
            dx_ref[...] = jnp.zeros_like(dx_ref)

    return _call(body, name=name, grid=(t // TM,),
                 in_specs=[_rows(TM, d), pl.BlockSpec((TM, d), lambda i: (jnp.minimum(i, nx - 1), 0)), _fixed((1, d))],
                 out_specs=[_fixed((1, LANES)), _rows(TM, d), _fixed((1, d))],
                 out_shape=[_sds((1, LANES), F32), _sds((t, d), F32), _sds((1, d), F32)])(xs, tgt, g)


def _view2d(a):
    return a.reshape(1, -1) if a.ndim == 1 else a.reshape(-1, a.shape[-1])


def _tile_rows(r, c):
    for cand in (1024, 512, 256, 128, 64, 32, 16):
        if r % cand == 0 and cand * c * 4 <= 2 ** 20:
            return cand
    return r


def _pair_sum(g, recv, layer, name):
    _, a, b = g.shape
    tr = _tile_rows(a, b)

    def body(l_ref, g_ref, r_ref, o32_ref, o16_ref):
        acc = g_ref[...] + r_ref[...]
        o32_ref[...] = acc
        o16_ref[...] = acc.astype(o16_ref.dtype)

    first = pl.BlockSpec((None, tr, b), lambda i, l: (0, i, 0))
    return pl.pallas_call(
        body, name=name, out_shape=[_sds((1, a, b), F32), _sds((1, a, b), MXU)],
        grid_spec=pltpu.PrefetchScalarGridSpec(
            num_scalar_prefetch=1, grid=(a // tr,),
            in_specs=[pl.BlockSpec((None, tr, b), lambda i, l: (l[0], i, 0)), first], out_specs=[first, first]),
        compiler_params=pltpu.CompilerParams(dimension_semantics=("arbitrary",), vmem_limit_bytes=VMEM_LIMIT))(layer, g, recv)


def _ew(fn, arrays, out_dtypes, name):
    shape = arrays[0].shape
    views = [_view2d(a) for a in arrays]
    r, c = views[0].shape
    tr = _tile_rows(r, c)

    def body(*refs):
        outs = fn(*[ref[...] for ref in refs[:len(views)]])
        for ref, o in zip(refs[len(views):], outs):
            ref[...] = o.astype(ref.dtype)

    res = _call(body, name=name, grid=(r // tr,), in_specs=[_rows(tr, c)] * len(views), out_specs=[_rows(tr, c)] * len(out_dtypes),
                out_shape=[_sds((r, c), dt) for dt in out_dtypes])(*views)
    return [o.reshape(shape) for o in res]


def _sum_pieces(pieces, name, out_dtypes=(F32,)):
    def fn(*vals):
        acc = vals[0].astype(F32)
        for v in vals[1:]:
            acc = acc + v.astype(F32)
        return (acc,) * len(out_dtypes)

    return _ew(fn, pieces, list(out_dtypes), name)


def _adam_update(w, g, m, v):
    m2 = ADAM_B1 * m + (1.0 - ADAM_B1) * g
    v2 = ADAM_B2 * v + (1.0 - ADAM_B2) * (g * g)
    m_hat = m2 / (1.0 - ADAM_B1 ** ADAM_STEP)
    v_hat = v2 / (1.0 - ADAM_B2 ** ADAM_STEP)
    delta = -ADAM_LR * (m_hat / (jnp.sqrt(v_hat) + ADAM_EPS) + ADAM_WD * w)
    return g, delta, m2, v2


def _adamw(w, g_pieces, m, v, name):
    n_g = len(g_pieces)

    def fn(w_, *rest):
        g = rest[0]
        for piece in rest[1:n_g]:
            g = g + piece
        return _adam_update(w_, g, rest[n_g], rest[n_g + 1])

    return _ew(fn, [w, *g_pieces, m, v], [F32] * 4, name)


def _adamw_layers(w, mine, other, m, v, layer, name):
    _, a, b = w.shape
    tr = _tile_rows(a, b)
    nb = a // tr

    def body(l_ref, w_ref, mine_ref, other_ref, m_ref, v_ref, *o_refs):
        g = jnp.where(pl.program_id(0) // nb == l_ref[0], mine_ref[...], other_ref[...])
        for ref, val in zip(o_refs, _adam_update(w_ref[...], g, m_ref[...], v_ref[...])):
            ref[...] = val

    both = pl.BlockSpec((None, tr, b), lambda i, l: (i // nb, i % nb, 0))
    one = pl.BlockSpec((None, tr, b), lambda i, l: (0, i % nb, 0))
    return pl.pallas_call(
        body, name=name, out_shape=[_sds(w.shape, F32)] * 4,
        grid_spec=pltpu.PrefetchScalarGridSpec(num_scalar_prefetch=1, grid=(2 * nb,), in_specs=[both, one, one, both, both],
                                               out_specs=[both] * 4),
        compiler_params=pltpu.CompilerParams(dimension_semantics=("arbitrary",), vmem_limit_bytes=VMEM_LIMIT))(
            layer, w, mine, other, m, v)


def _place_shard(full, shard, chip, axis, name):
    _, a, b = shard.shape
    tr = _tile_rows(a, b)
    nb = a // tr

    def body(c_ref, s_ref, f_ref, o_ref):
        o_ref[...] = s_ref[...]

    if axis == 1:
        out_spec = pl.BlockSpec((None, tr, b), lambda l, i, c: (l, c[0] * nb + i, 0))
    else:
        out_spec = pl.BlockSpec((None, tr, b), lambda l, i, c: (l, i, c[0]))
    return pl.pallas_call(
        body, name=name, out_shape=_sds(full.shape, full.dtype), input_output_aliases={2: 0},
        grid_spec=pltpu.PrefetchScalarGridSpec(
            num_scalar_prefetch=1, grid=(2, nb),
            in_specs=[pl.BlockSpec((None, tr, b), lambda l, i, c: (l, i, 0)), pl.BlockSpec(memory_space=pl.ANY)],
            out_specs=out_spec),
        compiler_params=pltpu.CompilerParams(dimension_semantics=("arbitrary", "arbitrary"), vmem_limit_bytes=VMEM_LIMIT))(
            chip, shard, full)


def _ada_fwd(cond, w, b, name):
    r, d = cond.shape
    n = w.shape[1]
    tn = _pick(n, (1152, 768, 512, 384, 256, 128))

    def body(c_ref, w_ref, b_ref, o_ref, s_ref):
        c = c_ref[...]
        sc = c * jax.nn.sigmoid(c)
        s_ref[...] = sc
        o_ref[...] = _nn(sc.astype(MXU), w_ref[...].astype(MXU)) + b_ref[...]

    return _call(body, name=name, grid=(n // tn,),
                 in_specs=[_fixed((r, d)), pl.BlockSpec((d, tn), lambda j: (0, j)), pl.BlockSpec((1, tn), lambda j: (0, j))],
                 out_specs=[pl.BlockSpec((r, tn), lambda j: (0, j)), _fixed((r, d))],
                 out_shape=[_sds((r, n), F32), _sds((r, d), F32)])(cond, w, b)


def _cctx_grad(parts, c_ctx, name):
    n, d = parts.shape

    def body(p_ref, c_ref, o_ref):
        c = c_ref[...]
        sg = jax.nn.sigmoid(c)
        acc = p_ref[0:1, :]
        for j in range(1, n):
            acc = acc + p_ref[j:j + 1, :]
        o_ref[...] = acc * (sg * (1.0 + c * (1.0 - sg)))

    return _call(body, name=name, grid=(1,), in_specs=[_fixed((n, d)), _fixed((1, d))], out_specs=_fixed((1, d)),
                 out_shape=_sds((1, d), F32))(parts, c_ctx)


def _here():
    return lax.axis_index("x"), lax.axis_index("y"), lax.axis_index("c")


def _flip(v, bit):
    return 1 - v if bit else v


def _allgather8(xb, name):
    r, n = xb.shape

    def body(x_ref, out_ref, send_sems, recv_sems, local_sem):
        x, y, c = _here()
        me = 4 * x + 2 * y + c
        local = pltpu.make_async_copy(x_ref, out_ref.at[me], local_sem)
        local.start()
        sends = []
        for k in range(1, 8):
            peer = (_flip(x, k & 4), _flip(y, k & 2), _flip(c, k & 1))
            cp = pltpu.make_async_remote_copy(src_ref=x_ref, dst_ref=out_ref.at[me], send_sem=send_sems.at[k - 1],
                                              recv_sem=recv_sems.at[k - 1], device_id=peer, device_id_type=MESH)
            cp.start()
            sends.append(cp)
        for k in range(1, 8):
            peer = (_flip(x, k & 4), _flip(y, k & 2), _flip(c, k & 1))
            src = 4 * peer[0] + 2 * peer[1] + peer[2]
            pltpu.make_async_remote_copy(src_ref=x_ref, dst_ref=out_ref.at[src], send_sem=send_sems.at[k - 1],
                                         recv_sem=recv_sems.at[k - 1], device_id=peer, device_id_type=MESH).wait_recv()
        for cp in sends:
            cp.wait_send()
        local.wait()

    vmem = pl.BlockSpec(memory_space=pltpu.VMEM)
    return pl.pallas_call(
        body, name=name, out_shape=_sds((8, r, n), xb.dtype), in_specs=[vmem], out_specs=vmem,
        scratch_shapes=[pltpu.SemaphoreType.DMA((7,)), pltpu.SemaphoreType.DMA((7,)), pltpu.SemaphoreType.DMA(())],
        compiler_params=pltpu.CompilerParams(vmem_limit_bytes=VMEM_LIMIT))(xb)


def _shard_of(ref, axis, j, size):
    sl = pl.ds(j * size, size)
    return ref.at[:, sl, :] if axis == 1 else ref.at[:, :, sl]


def _piece(ref, axis, j, size, layer):
    lay, sl = pl.ds(layer, 1), pl.ds(j * size, size)
    return ref.at[lay, sl, :] if axis == 1 else ref.at[lay, :, sl]


def _gather_chips(shards, axes, name):
    n = len(shards)
    fulls = []
    for a, ax in zip(shards, axes):
        assert a.shape[0] == 2
        shp = list(a.shape)
        shp[ax] *= 4
        fulls.append(_sds(tuple(shp), a.dtype))

    def body(*refs):
        ins, outs = refs[:n], refs[n:2 * n]
        ici_send, ici_recv, d2d_send, d2d_recv = refs[2 * n:]
        x, y, c = _here()
        chips = [(_flip(x, k & 2), _flip(y, k & 1)) for k in range(1, 4)]
        sends = []
        for a in range(n):
            size = ins[a].shape[axes[a]]
            for j, (px, py) in enumerate(chips):
                cp = pltpu.make_async_remote_copy(src_ref=ins[a].at[pl.ds(c, 1)], dst_ref=_piece(outs[a], axes[a], 2 * x + y, size, c),
                                                  send_sem=ici_send.at[3 * a + j], recv_sem=ici_recv.at[3 * a + j],
                                                  device_id=(px, py, c), device_id_type=MESH)
                cp.start()
                sends.append(cp)
        for a in range(n):
            size = ins[a].shape[axes[a]]
            for j, (px, py) in enumerate(chips):
                landed = _piece(outs[a], axes[a], 2 * px + py, size, c)
                pltpu.make_async_remote_copy(src_ref=ins[a].at[pl.ds(c, 1)], dst_ref=landed, send_sem=ici_send.at[3 * a + j],
                                             recv_sem=ici_recv.at[3 * a + j], device_id=(px, py, c), device_id_type=MESH).wait_recv()
                cp = pltpu.make_async_remote_copy(src_ref=landed, dst_ref=landed, send_sem=d2d_send.at[3 * a + j],
                                                  recv_sem=d2d_recv.at[3 * a + j], device_id=(x, y, 1 - c), device_id_type=MESH)
                cp.start()
                sends.append(cp)
        for a in range(n):
            size = ins[a].shape[axes[a]]
            for j, (px, py) in enumerate(chips):
                passed = _piece(outs[a], axes[a], 2 * px + py, size, 1 - c)
                pltpu.make_async_remote_copy(src_ref=passed, dst_ref=passed, send_sem=d2d_send.at[3 * a + j],
                                             recv_sem=d2d_recv.at[3 * a + j], device_id=(x, y, 1 - c), device_id_type=MESH).wait_recv()
        for cp in sends:
            cp.wait_send()

    hbm = pl.BlockSpec(memory_space=pl.ANY)
    return pl.pallas_call(
        body, name=name, out_shape=fulls, in_specs=[hbm] * n, out_specs=[hbm] * n,
        scratch_shapes=[pltpu.SemaphoreType.DMA((3 * n,))] * 4)(*shards)


def _swap_layers(arrays, name):
    n = len(arrays)

    def body(*refs):
        ins, outs = refs[:n], refs[n:2 * n]
        send_sems, recv_sems = refs[2 * n:]
        x, y, c = _here()
        copies = []
        for a in range(n):
            cp = pltpu.make_async_remote_copy(src_ref=ins[a].at[pl.ds(1 - c, 1)], dst_ref=outs[a], send_sem=send_sems.at[a],
                                              recv_sem=recv_sems.at[a], device_id=(x, y, 1 - c), device_id_type=MESH)
            cp.start()
            copies.append(cp)
        for cp in copies:
            cp.wait()

    hbm = pl.BlockSpec(memory_space=pl.ANY)
    return pl.pallas_call(
        body, name=name, out_shape=[_sds((1, *a.shape[1:]), a.dtype) for a in arrays], in_specs=[hbm] * n, out_specs=[hbm] * n,
        scratch_shapes=[pltpu.SemaphoreType.DMA((n,)), pltpu.SemaphoreType.DMA((n,))])(*arrays)


def _scatter_chips(fulls, axes, name):
    n = len(fulls)
    recvs = []
    for a, ax in zip(fulls, axes):
        shp = list(a.shape)
        shp[ax] //= 4
        recvs.append(_sds((3, *shp), a.dtype))

    def body(*refs):
        ins, outs = refs[:n], refs[n:2 * n]
        send_sems, recv_sems = refs[2 * n:]
        x, y, c = _here()
        sends = []
        for a in range(n):
            size = ins[a].shape[axes[a]] // 4
            for k in range(1, 4):
                peer = (_flip(x, k & 2), _flip(y, k & 1), c)
                cp = pltpu.make_async_remote_copy(src_ref=_shard_of(ins[a], axes[a], 2 * peer[0] + peer[1], size),
                                                  dst_ref=outs[a].at[k - 1],
                                                  send_sem=send_sems.at[3 * a + k - 1], recv_sem=recv_sems.at[3 * a + k - 1],
                                                  device_id=peer, device_id_type=MESH)
                cp.start()
                sends.append(cp)
        for cp in sends:
            cp.wait_recv()
        for cp in sends:
            cp.wait_send()

    hbm = pl.BlockSpec(memory_space=pl.ANY)
    return pl.pallas_call(
        body, name=name, out_shape=recvs, in_specs=[hbm] * n, out_specs=[hbm] * n,
        scratch_shapes=[pltpu.SemaphoreType.DMA((3 * n,)), pltpu.SemaphoreType.DMA((3 * n,))])(*fulls)


def _sibling_swap(arrays, name):
    n = len(arrays)

    def body(*refs):
        ins, outs = refs[:n], refs[n:2 * n]
        send_sems, recv_sems = refs[2 * n:]
        x, y, c = _here()
        copies = []
        for a in range(n):
            cp = pltpu.make_async_remote_copy(src_ref=ins[a], dst_ref=outs[a], send_sem=send_sems.at[a], recv_sem=recv_sems.at[a],
                                              device_id=(x, y, 1 - c), device_id_type=MESH)
            cp.start()
            copies.append(cp)
        for cp in copies:
            cp.wait()

    hbm = pl.BlockSpec(memory_space=pl.ANY)
    return pl.pallas_call(
        body, name=name, out_shape=[_sds(a.shape, a.dtype) for a in arrays], in_specs=[hbm] * n, out_specs=[hbm] * n,
        scratch_shapes=[pltpu.SemaphoreType.DMA((n,)), pltpu.SemaphoreType.DMA((n,))])(*arrays)


def _ffn_fwd(xs, g, mods, k0, w_up, w_down, s, tag):
    h, ua, ub, act = _norm_mm(xs, g, mods, k0, k0 + 1, w_up, s, tag + "_up", True)
    y, xn = _mm_res(act, w_down, xs, mods, k0 + 2, 0.5, s, tag + "_down")
    return xn, (xs, h, ua, ub, act, y)


def _dw(gw, key, a, b, name, col0=0, n_total=None):
    shape = (gw["depth"], a.shape[1], n_total or b.shape[1])
    gw[key] = _mm(a, b, "tn", F32, name, into=(gw.get(key), shape, gw["layer"], col0))


def _ffn_bwd(dxn, saved, g, mods, k0, w_up, w_down, s, tag, gw, up_key, down_key):
    xs, h, ua, ub, act, y = saved
    dy, dgate, dua, dub = _resb_mm(dxn, y, mods, k0 + 2, 0.5, w_down, s, tag + "_down_dx", (ua, ub))
    _dw(gw, down_key, act, dy, tag + "_down_dw")
    f = dua.shape[1]
    _dw(gw, up_key, h, dua, tag + "_upa_dw", 0, 2 * f)
    _dw(gw, up_key, h, dub, tag + "_upb_dw", f, 2 * f)
    dx, dsh, dsc, dg = _mm_normb([dua, dub], w_up, xs, dxn, g, mods, k0 + 1, s, tag + "_up_dx")
    return dx, dg, [dsh, dsc, dgate]


def _mix_fwd(xs, g, mods, wl, pl_, tabs, s, tag):
    cos, sin = tabs
    h, p = _norm_mm(xs, g, mods, 3, 4, wl["w_in"], s, tag + "_in", False)
    q, k, v = _rope(p, cos, sin, tag + "_rope")
    bias = _bias_table(pl_["rpb"], s // GRID_W)
    oa = _na_fwd(q, k, v, bias, s, tag + "_na")
    ob = _gmlp(p, pl_["ln_v_g"], pl_["ln_v_b"], pl_["w_s"], pl_["b_s"], tag + "_sg")
    pa = _mm(oa, wl["w_pa"], "nn", MXU, tag + "_pa")
    pb = _mm(ob, wl["w_pb"], "nn", MXU, tag + "_pb")
    mg = _merge(pa, pb, p, pl_["b_gate"], tag + "_merge")
    y, xn = _mm_res(mg, wl["w_o"], xs, mods, 5, 1.0, s, tag + "_o")
    return xn, (xs, h, p, q, k, v, bias, oa, ob, pa, pb, mg, y)


def _mix_bwd(dxn, saved, g, mods, wl, pl_, tabs, s, tag, gw):
    xs, h, p, q, k, v, bias, oa, ob, pa, pb, mg, y = saved
    cos, sin = tabs
    gp = {}
    dy, dgate, dmg = _resb_mm(dxn, y, mods, 5, 1.0, wl["w_o"], s, tag + "_o_dx")
    _dw(gw, "w_o", mg, dy, tag + "_o_dw")
    dpa, dpb, dla, dlb, dba, dbb = _merge_bwd(dmg, pa, pb, p, pl_["b_gate"], tag + "_merge_b")
    gp["b_gate"] = jnp.concatenate([dba, dbb], axis=1)
    _dw(gw, "w_pa", oa, dpa, tag + "_pa_dw")
    doa = _mm(dpa, wl["w_pa"], "nt", MXU, tag + "_pa_dx")
    _dw(gw, "w_pb", ob, dpb, tag + "_pb_dw")
    dob = _mm(dpb, wl["w_pb"], "nt", MXU, tag + "_pb_dx")
    du, dvs, gp["w_s"], gp["b_s"], gp["ln_v_g"], gp["ln_v_b"] = _gmlp_bwd(
        p, dob, pl_["ln_v_g"], pl_["ln_v_b"], pl_["w_s"], pl_["b_s"], tag + "_sg_b")
    dqr, dkr, dv, dbias = _na_bwd(q, k, v, doa, bias, s, tag + "_na_b")
    gp["rpb"] = _rpb_grad(dbias, s // GRID_W, tag + "_rpb")
    dq, dk, dvv = _rope_bwd(dqr, dkr, dv, cos, sin, tag + "_rope_b")
    dp = [dq, dk, dvv, du, dvs, dla, dlb]
    n_in = sum(piece.shape[1] for piece in dp)
    for j, piece in enumerate(dp):
        _dw(gw, "w_in", h, piece, tag + f"_in_dw{j}", sum(q.shape[1] for q in dp[:j]), n_in)
    dx, dsh, dsc, dg = _mm_normb(dp, wl["w_in"], xs, dxn, g, mods, 4, s, tag + "_in_dx")
    return dx, gp, dg, [dsh, dsc, dgate]


def _local_step(x, ctx, tgt, mods, wts, prm):
    s, d = x.shape
    depth = mods.shape[0]
    tabs = _rope_tables(s, ctx.shape[0])
    xs = jnp.concatenate([x, ctx], axis=0)
    saved = []
    for l in range(depth):
        wl = {k: (v, l) for k, v in wts.items()}
        pl_ = _layer_params(prm, l)
        xs, s1 = _ffn_fwd(xs, pl_["g"][0], mods[l], 0, wl["w_ff1_up"], wl["w_ff1_down"], s, f"l{l}_ff1")
        xs, s2 = _mix_fwd(xs, pl_["g"][1], mods[l], wl, pl_, tabs, s, f"l{l}_mix")
        xs, s3 = _ffn_fwd(xs, pl_["g"][2], mods[l], 6, wl["w_ff2_up"], wl["w_ff2_down"], s, f"l{l}_ff2")
        saved.append((s1, s2, s3))
    loss, dxs, d_final_g = _final(xs, tgt, prm["final_g"].reshape(1, d), "final")
    gw = {"depth": depth}
    gp = {k: [None] * depth for k in ("norm_g", "b_gate", "rpb", "ln_v_g", "ln_v_b", "w_s", "b_s")}
    dmods = [None] * depth
    for l in reversed(range(depth)):
        wl = {k: (v, l) for k, v in wts.items()}
        pl_ = _layer_params(prm, l)
        s1, s2, s3 = saved[l]
        gw["layer"] = l
        dxs, dg2, dm2 = _ffn_bwd(dxs, s3, pl_["g"][2], mods[l], 6, wl["w_ff2_up"], wl["w_ff2_down"], s, f"l{l}_ff2",
                                 gw, "w_ff2_up", "w_ff2_down")
        dxs, gpm, dg1, dm1 = _mix_bwd(dxs, s2, pl_["g"][1], mods[l], wl, pl_, tabs, s, f"l{l}_mix", gw)
        dxs, dg0, dm0 = _ffn_bwd(dxs, s1, pl_["g"][0], mods[l], 0, wl["w_ff1_up"], wl["w_ff1_down"], s, f"l{l}_ff1",
                                 gw, "w_ff1_up", "w_ff1_down")
        gp["b_gate"][l] = gpm["b_gate"][0]
        gp["rpb"][l] = gpm["rpb"]
        gp["ln_v_g"][l] = gpm["ln_v_g"][0]
        gp["ln_v_b"][l] = gpm["ln_v_b"][0]
        gp["w_s"][l] = gpm["w_s"]
        gp["b_s"][l] = gpm["b_s"][..., 0]
        gp["norm_g"][l] = jnp.concatenate([dg0, dg1, dg2], axis=0)
        dmods[l] = jnp.concatenate(dm0 + dm1 + dm2, axis=1)
    gw = {k: gw[k] for k in wts}
    gp = {k: jnp.stack(v) for k, v in gp.items()}
    gp["final_g"] = d_final_g[0]
    return loss[0, 0], dxs[:s], jnp.stack(dmods), gw, gp


def _layer_params(prm, l):
    d = prm["norm_g"].shape[-1]
    return {
        "g": [prm["norm_g"][l, i].reshape(1, d) for i in range(3)],
        "b_gate": prm["b_gate"][l].reshape(1, -1),
        "rpb": prm["rpb"][l],
        "ln_v_g": prm["ln_v_g"][l].reshape(1, -1),
        "ln_v_b": prm["ln_v_b"][l].reshape(1, -1),
        "w_s": prm["w_s"][l],
        "b_s": prm["b_s"][l][..., None],
    }


SMALL = ("norm_g", "b_gate", "rpb", "ln_v_g", "ln_v_b", "w_s", "b_s", "final_g")
PACK_LANES = 1024


def _pack(parts):
    flat = jnp.concatenate([p.reshape(-1) for p in parts])
    rows = -(-flat.shape[0] // PACK_LANES)
    rows = -(-rows // 8) * 8
    return jnp.pad(flat, (0, rows * PACK_LANES - flat.shape[0])).reshape(rows, PACK_LANES)


def _unpack(flat, shapes):
    out, off = [], 0
    for shp in shapes:
        n = int(np.prod(shp))
        out.append(flat[..., off:off + n].reshape(*flat.shape[:-1], *shp))
        off += n
    return out


def kernel(x, c, ctx, c_ctx, w_ada, b_ada, norm_g, w_ff1_up, w_ff1_down, w_in, b_gate, rpb, ln_v_g, ln_v_b, w_s, b_s, w_pa, w_pb, w_o, w_ff2_up, w_ff2_down, final_g, loss_target, m_c_ctx, m_w_ada, m_b_ada, m_norm_g, m_w_ff1_up, m_w_ff1_down, m_w_in, m_b_gate, m_rpb, m_ln_v_g, m_ln_v_b, m_w_s, m_b_s, m_w_pa, m_w_pb, m_w_o, m_w_ff2_up, m_w_ff2_down, m_final_g, v_c_ctx, v_w_ada, v_b_ada, v_norm_g, v_w_ff1_up, v_w_ff1_down, v_w_in, v_b_gate, v_rpb, v_ln_v_g, v_ln_v_b, v_w_s, v_b_s, v_w_pa, v_w_pb, v_w_o, v_w_ff2_up, v_w_ff2_down, v_final_g):
    weights = dict(c_ctx=c_ctx, w_ada=w_ada, b_ada=b_ada, norm_g=norm_g, w_ff1_up=w_ff1_up, w_ff1_down=w_ff1_down, w_in=w_in,
                   b_gate=b_gate, rpb=rpb, ln_v_g=ln_v_g, ln_v_b=ln_v_b, w_s=w_s, b_s=b_s, w_pa=w_pa, w_pb=w_pb, w_o=w_o,
                   w_ff2_up=w_ff2_up, w_ff2_down=w_ff2_down, final_g=final_g)
    mom_m = dict(c_ctx=m_c_ctx, w_ada=m_w_ada, b_ada=m_b_ada, norm_g=m_norm_g, w_ff1_up=m_w_ff1_up, w_ff1_down=m_w_ff1_down,
                 w_in=m_w_in, b_gate=m_b_gate, rpb=m_rpb, ln_v_g=m_ln_v_g, ln_v_b=m_ln_v_b, w_s=m_w_s, b_s=m_b_s, w_pa=m_w_pa,
                 w_pb=m_w_pb, w_o=m_w_o, w_ff2_up=m_w_ff2_up, w_ff2_down=m_w_ff2_down, final_g=m_final_g)
    mom_v = dict(c_ctx=v_c_ctx, w_ada=v_w_ada, b_ada=v_b_ada, norm_g=v_norm_g, w_ff1_up=v_w_ff1_up, w_ff1_down=v_w_ff1_down,
                 w_in=v_w_in, b_gate=v_b_gate, rpb=v_rpb, ln_v_g=v_ln_v_g, ln_v_b=v_ln_v_b, w_s=v_w_s, b_s=v_b_s, w_pa=v_w_pa,
                 w_pb=v_w_pb, w_o=v_w_o, w_ff2_up=v_w_ff2_up, w_ff2_down=v_w_ff2_down, final_g=v_final_g)
    order = list(weights)
    mx, my, mc = _here()
    dev = 4 * mx + 2 * my + mc
    chip = 2 * mx + my
    depth, d, n_ada = w_ada.shape
    dq = d // 4

    c_all = _allgather8(jnp.pad(c, ((0, 7), (0, 0))), "gather_c")[:, 0, :]
    cond = jnp.concatenate([c_all, c_ctx[None, :], jnp.zeros((7, d), F32)], axis=0)
    b_shard = lax.dynamic_slice(b_ada, (0, chip * n_ada), (depth, n_ada))
    proj = [_ada_fwd(cond, w_ada[l], b_shard[l:l + 1], f"ada{l}") for l in range(depth)]
    silu_c = proj[0][1]
    mods_sh = _allgather8(jnp.concatenate([p[0] for p in proj], axis=0), "gather_mods")
    mods_all = jnp.transpose(mods_sh[0::2].reshape(4, depth, 16, n_ada), (1, 2, 0, 3)).reshape(depth, 16, N_MOD, d)
    mods = jnp.stack([lax.dynamic_index_in_dim(mods_all, dev, axis=1, keepdims=False), mods_all[:, 8]], axis=1)

    shards = [weights[k].astype(MXU) for k in BIG]
    full = _gather_chips(shards, [SHARD_AXIS[k] for k in BIG], "gather_w")
    my_chip = jnp.reshape(chip, (1,)).astype(jnp.int32)
    full = [_place_shard(f, sh, my_chip, SHARD_AXIS[k], "place_" + k) for k, f, sh in zip(BIG, full, shards)]
    wts = dict(zip(BIG, full))
    prm = {k: weights[k] for k in SMALL if k != "norm_g"}
    norm_full = _allgather8(jnp.pad(norm_g.reshape(depth * 3, dq), ((0, 8 - depth * 3), (0, 0))), "gather_norm_g")
    prm["norm_g"] = jnp.transpose(norm_full[0::2, :depth * 3].reshape(4, depth, 3, dq), (1, 2, 0, 3)).reshape(depth, 3, d)

    loss, grad_x, dmods, gw, gp = _local_step(x[0], ctx[0], loss_target[0], mods, wts, prm)
    loss = lax.psum(loss, ("x", "y", "c"))

    small_shapes = [(depth, 2, N_MOD * d)] + [weights[k].shape if k != "norm_g" else (depth, 3, d) for k in SMALL]
    packed = _allgather8(_pack([dmods.reshape(depth, 2, N_MOD * d)] + [gp[k] for k in SMALL]), "gather_small")
    rows = packed.shape[1]
    total = _sum_pieces([packed[i] for i in range(8)], "sum_small")[0].reshape(-1)
    sums = dict(zip(("dmods",) + SMALL, _unpack(total, small_shapes)))
    dmods_dev = _unpack(packed.reshape(8, rows * PACK_LANES), small_shapes[:1])[0]

    g_ada, cc_parts = [], []
    for l in range(depth):
        dm = jnp.concatenate([dmods_dev[:, l, 0], sums["dmods"][l, 1][None], jnp.zeros((7, N_MOD * d), F32)], axis=0)
        dm_sh = lax.dynamic_slice(dm, (0, chip * n_ada), (16, n_ada))
        g_ada.append(_mm(silu_c, dm_sh, "tn", F32, f"ada{l}_dw"))
        cc_parts.append(_mm(dm_sh, w_ada[l], "nt", F32, f"ada{l}_dc")[8:9])
    cc_all = _allgather8(jnp.pad(jnp.concatenate(cc_parts, axis=0), ((0, 8 - depth), (0, 0))), "gather_cctx")
    g_cctx = _cctx_grad(cc_all[0::2, :depth].reshape(4 * depth, d), c_ctx.reshape(1, d), "cctx_grad")

    axes = [SHARD_AXIS[k] for k in BIG]
    from_sibling = _swap_layers([gw[k] for k in BIG], "swap_layer_gw")
    my_layer = jnp.reshape(mc, (1,)).astype(jnp.int32)
    pair = [_pair_sum(gw[k], r, my_layer, "pair_" + k) for k, r in zip(BIG, from_sibling)]
    recv = _scatter_chips([p[1] for p in pair], axes, "scatter_gw")
    mine = []
    for k, ax, p, r in zip(BIG, axes, pair, recv):
        size = p[0].shape[ax] // 4
        own = lax.dynamic_slice_in_dim(p[0], chip * size, size, axis=ax)
        mine.append(_sum_pieces([own, r[0], r[1], r[2]], "sum_" + k)[0])
    other = _sibling_swap(mine, "swap_gw")

    res = {k: _adamw_layers(weights[k], a, b, mom_m[k], mom_v[k], my_layer, "adamw_" + k) for k, a, b in zip(BIG, mine, other)}
    pieces = {"w_ada": [jnp.stack(g_ada)]}
    pieces["b_ada"] = [sums["dmods"][:, 0], sums["dmods"][:, 1]]
    pieces["c_ctx"] = [g_cctx[0]]
    for k in SMALL:
        pieces[k] = [sums[k]]
    pieces["norm_g"] = [lax.dynamic_slice_in_dim(sums["norm_g"], chip * dq, dq, axis=2)]
    res.update({k: _adamw(weights[k], pieces[k], mom_m[k], mom_v[k], "adamw_" + k) for k in pieces})
    return (loss, grad_x[None], *[res[k][0] for k in order], *[res[k][1] for k in order],
            *[res[k][2] for k in order], *[res[k][3] for k in order])
```

```python
import numpy as np
import jax
import jax.numpy as jnp
from jax import lax
from jax.experimental import pallas as pl
from jax.experimental.pallas import tpu as pltpu

F32 = jnp.float32
MXU = jnp.bfloat16
EPS = 1e-6
GRID_W, HEADS, HEAD_DIM = 64, 8, 64
NA_WIDTH = SG_WIDTH = 512
WIN_H, WIN_W = 8, 16
SG_CHUNK, SG_GROUPS = 128, 4
N_MOD = 9
ROPE_THETA = 10000.0
Q_ROWS, K_ROWS = 4, 12
TQ, TK = Q_ROWS * GRID_W, K_ROWS * GRID_W
TM = 256
LANES = 128
NEG = -1e30
VMEM_LIMIT = 56 * 2 ** 20
ADAM_LR, ADAM_B1, ADAM_B2, ADAM_EPS, ADAM_WD, ADAM_STEP = 0.001, 0.9, 0.999, 1e-08, 0.01, 10
MESH = pl.DeviceIdType.MESH
BIG = ("w_ff1_up", "w_ff1_down", "w_in", "w_pa", "w_pb", "w_o", "w_ff2_up", "w_ff2_down")
SHARD_AXIS = {"w_ff1_up": 2, "w_ff1_down": 1, "w_in": 2, "w_pa": 2, "w_pb": 2, "w_o": 1, "w_ff2_up": 2, "w_ff2_down": 1}


def _call(body, *, name, grid, in_specs, out_specs, out_shape, scratch=(), aliases=None):
    return pl.pallas_call(
        body, name=name, grid=grid, in_specs=in_specs, out_specs=out_specs, out_shape=out_shape,
        scratch_shapes=list(scratch), input_output_aliases=aliases or {},
        compiler_params=pltpu.CompilerParams(dimension_semantics=("arbitrary",) * len(grid), vmem_limit_bytes=VMEM_LIMIT))


def _w_dims(w):
    return w[0].shape[1:] if isinstance(w, tuple) else w.shape


def _w_arr(w):
    return w[0] if isinstance(w, tuple) else w


def _w_spec(w, block, index):
    if isinstance(w, tuple):
        layer = w[1]
        return pl.BlockSpec((None, *block), lambda *ids: (layer, *index(*ids)))
    return pl.BlockSpec(block, index)


def _pick(n, prefs):
    for p in prefs:
        if n % p == 0:
            return p
    return n


def _row_tile(t):
    return _pick(t, (640, 256))


def _rows(tm, n, col=0):
    return pl.BlockSpec((tm, n), lambda i: (i, col))


def _fixed(shape):
    return pl.BlockSpec(shape, lambda *_: (0,) * len(shape))


def _sds(shape, dtype):
    return jax.ShapeDtypeStruct(shape, dtype)


def _mm(a, b, mode, out_dtype, name, into=None):
    if mode == "tn":
        r, m = a.shape
        n = b.shape[1]
        tm = _pick(m, (1024, 1408, 704, 512, 256, 128))
        tn = _pick(n, (512, 1408, 256, 128))
        tr = _pick(r, (1280, 640, 512, 256, 128))

        def body(a_ref, b_ref, *rest):
            o_ref = rest[-1]

            @pl.when(pl.program_id(2) == 0)
            def _():
                o_ref[...] = jnp.zeros_like(o_ref)

            o_ref[...] += lax.dot_general(a_ref[...].astype(MXU), b_ref[...].astype(MXU), (((0,), (0,)), ((), ())),
                                          preferred_element_type=F32)

        in_specs = [pl.BlockSpec((tr, tm), lambda i, j, k: (k, i)), pl.BlockSpec((tr, tn), lambda i, j, k: (k, j))]
        if into is None:
            return _call(body, name=name, grid=(m // tm, n // tn, r // tr), in_specs=in_specs,
                         out_specs=pl.BlockSpec((tm, tn), lambda i, j, k: (i, j)), out_shape=_sds((m, n), F32))(a, b)
        buf, shape, layer, col0 = into
        out_spec = pl.BlockSpec((None, tm, tn), lambda i, j, k: (layer, i, j + col0 // tn))
        if buf is None:
            return _call(body, name=name, grid=(m // tm, n // tn, r // tr), in_specs=in_specs, out_specs=out_spec,
                         out_shape=_sds(shape, F32))(a, b)
        return _call(body, name=name, grid=(m // tm, n // tn, r // tr), in_specs=in_specs + [pl.BlockSpec(memory_space=pl.ANY)],
                     out_specs=out_spec, out_shape=_sds(shape, F32), aliases={2: 0})(a, b, buf)
    m, k = a.shape
    n = _w_dims(b)[1] if mode == "nn" else _w_dims(b)[0]
    tm = _pick(m, (1280, 640, 512, 256, 128) if k <= 2816 else (640, 512, 256, 128))
    tn = _pick(n, (512, 1408, 256, 128))
    dims = (((1,), (0,)), ((), ())) if mode == "nn" else (((1,), (1,)), ((), ()))

    def body(a_ref, b_ref, o_ref):
        o_ref[...] = lax.dot_general(a_ref[...].astype(MXU), b_ref[...].astype(MXU), dims,
                                     preferred_element_type=F32).astype(o_ref.dtype)

    b_spec = _w_spec(b, (k, tn), lambda i, j: (0, j)) if mode == "nn" else _w_spec(b, (tn, k), lambda i, j: (j, 0))
    return _call(body, name=name, grid=(m // tm, n // tn), in_specs=[pl.BlockSpec((tm, k), lambda i, j: (i, 0)), b_spec],
                 out_specs=pl.BlockSpec((tm, tn), lambda i, j: (i, j)), out_shape=_sds((m, n), out_dtype))(a, _w_arr(b))


def _row_chunks(tm):
    rc = _pick(tm, (256, 128))
    return [slice(r, r + rc) for r in range(0, tm, rc)]


def _ctx_rows(i, tm, s):
    return (i * tm + lax.broadcasted_iota(jnp.int32, (tm, 1), 0)) >= s


def _mod_row(m_ref, k, ctx):
    return jnp.where(ctx, m_ref[1, k:k + 1, :], m_ref[0, k:k + 1, :])


def _stream_sums(i, tm, s, refs_and_vals):
    @pl.when((i + 1) * tm <= s)
    def _():
        for ref, val in refs_and_vals:
            ref[0] += jnp.sum(val, axis=0, keepdims=True)

    @pl.when((i + 1) * tm > s)
    def _():
        ctx = _ctx_rows(i, tm, s)
        for ref, val in refs_and_vals:
            ref[0] += jnp.sum(jnp.where(ctx, 0.0, val), axis=0, keepdims=True)
            ref[1] += jnp.sum(jnp.where(ctx, val, 0.0), axis=0, keepdims=True)


def _norm_mm(xs, g, mods, k_shift, k_scale, w, s, name, glu):
    t, d = xs.shape
    n = _w_dims(w)[1] // 2 if glu else _w_dims(w)[1]
    tm = _pick(t, (1280, 640, 256))
    tn = _pick(n, (256, 128)) if glu else _pick(n, (512, 256, 128))
    nj = n // tn

    def body(x_ref, g_ref, m_ref, *refs):
        i, j = pl.program_id(0), pl.program_id(1)
        w_refs, h_ref, o_refs = refs[:2 if glu else 1], refs[2 if glu else 1], refs[3 if glu else 2:]

        @pl.when(j == 0)
        def _():
            x = x_ref[...]
            rstd = lax.rsqrt(jnp.mean(x * x, axis=-1, keepdims=True) + EPS)
            ctx = _ctx_rows(i, tm, s)
            h = x * rstd * g_ref[...] * (1.0 + _mod_row(m_ref, k_scale, ctx)) + _mod_row(m_ref, k_shift, ctx)
            h_ref[...] = h.astype(h_ref.dtype)

        for rows in _row_chunks(tm):
            h = h_ref[rows, :]
            a = _nn(h, w_refs[0][...])
            if glu:
                b = _nn(h, w_refs[1][...])
                sg = jax.nn.sigmoid(a)
                silu = a * sg
                o_refs[0][rows, :] = (b * sg * (1.0 + a * (1.0 - sg))).astype(o_refs[0].dtype)
                o_refs[1][rows, :] = silu.astype(o_refs[1].dtype)
                o_refs[2][rows, :] = (silu * b).astype(o_refs[2].dtype)
            else:
                o_refs[0][rows, :] = a.astype(o_refs[0].dtype)

    tile = pl.BlockSpec((tm, tn), lambda i, j: (i, j))
    row = pl.BlockSpec((tm, d), lambda i, j: (i, 0))
    w_specs = [_w_spec(w, (d, tn), lambda i, j: (0, j))] + ([_w_spec(w, (d, tn), lambda i, j: (0, j + nj))] if glu else [])
    n_out = 3 if glu else 1
    return _call(body, name=name, grid=(t // tm, nj),
                 in_specs=[row, _fixed((1, d)), _fixed((2, N_MOD, d))] + w_specs,
                 out_specs=[row] + [tile] * n_out,
                 out_shape=[_sds((t, d), MXU)] + [_sds((t, n), MXU)] * n_out)(xs, g, mods, *([_w_arr(w)] * (2 if glu else 1)))


def _mm_res(a, w, xs, mods, k_gate, coef, s, name):
    t, k = a.shape
    d = _w_dims(w)[1]
    tm = _pick(t, (1280, 640, 256))
    tn = _pick(d, (512, 256, 128))

    def body(a_ref, w_ref, x_ref, m_ref, y_ref, o_ref):
        y = _nn(a_ref[...], w_ref[...])
        y_ref[...] = y.astype(y_ref.dtype)
        gate = _mod_row(m_ref, k_gate, _ctx_rows(pl.program_id(0), tm, s))
        o_ref[...] = x_ref[...] + (coef * gate) * y

    tile = pl.BlockSpec((tm, tn), lambda i, j: (i, j))
    return _call(body, name=name, grid=(t // tm, d // tn),
                 in_specs=[pl.BlockSpec((tm, k), lambda i, j: (i, 0)), _w_spec(w, (k, tn), lambda i, j: (0, j)), tile,
                           pl.BlockSpec((2, N_MOD, tn), lambda i, j: (0, 0, j))],
                 out_specs=[tile, tile], out_shape=[_sds((t, d), MXU), _sds((t, d), F32)])(a, _w_arr(w), xs, mods)


def _resb_mm(dxn, y, mods, k_gate, coef, w, s, name, ups=None):
    t, d = dxn.shape
    n = _w_dims(w)[0]
    tm = _pick(t, (1280, 640, 256))
    tn = _pick(n, (256, 128)) if ups else _pick(n, (512, 256, 128))

    def body(dx_ref, y_ref, m_ref, w_ref, *refs):
        i, j = pl.program_id(0), pl.program_id(1)
        u_refs, (dy_ref, dgt_ref), o_refs = (refs[:2], refs[2:4], refs[4:]) if ups else ((), refs[:2], refs[2:])

        @pl.when((i == 0) & (j == 0))
        def _():
            dgt_ref[...] = jnp.zeros_like(dgt_ref)

        @pl.when(j == 0)
        def _():
            dx = dx_ref[...]
            gate = _mod_row(m_ref, k_gate, _ctx_rows(i, tm, s))
            dy_ref[...] = ((coef * gate) * dx).astype(dy_ref.dtype)
            _stream_sums(i, tm, s, [(dgt_ref, coef * y_ref[...].astype(F32) * dx)])

        dact = _nt(dy_ref[...], w_ref[...])
        if ups:
            o_refs[0][...] = (dact * u_refs[0][...].astype(F32)).astype(o_refs[0].dtype)
            o_refs[1][...] = (dact * u_refs[1][...].astype(F32)).astype(o_refs[1].dtype)
        else:
            o_refs[0][...] = dact.astype(o_refs[0].dtype)

    tile = pl.BlockSpec((tm, tn), lambda i, j: (i, j))
    row = pl.BlockSpec((tm, d), lambda i, j: (i, 0))
    n_out = 2 if ups else 1
    return _call(body, name=name, grid=(t // tm, n // tn),
                 in_specs=[row, row, _fixed((2, N_MOD, d)), _w_spec(w, (tn, d), lambda i, j: (j, 0))] + ([tile, tile] if ups else []),
                 out_specs=[row, _fixed((2, 1, d))] + [tile] * n_out,
                 out_shape=[_sds((t, d), MXU), _sds((2, 1, d), F32)] + [_sds((t, n), MXU)] * n_out)(
                     dxn, y, mods, _w_arr(w), *(ups or ()))


def _mm_normb(a_list, w, xs, dres, g, mods, k_scale, s, name):
    t, d = xs.shape
    tm = _pick(t, (640, 256))
    n_a = len(a_list)
    widths = [a.shape[1] for a in a_list]
    tk = next(c for c in (2816, 1408, 1536, 1024, 512, 256, 128) if all(wd % c == 0 for wd in widths))
    counts = [wd // tk for wd in widths]
    starts = [sum(counts[:q]) for q in range(n_a)]
    nk = sum(counts)
    k_all = sum(widths)
    resident = 4 * k_all * (d + tm) + 24 * tm * d <= 48 * 2 ** 20

    def finish(i, dh, x_ref, dr_ref, g_ref, m_ref, dx_ref, dsh_ref, dsc_ref, dg_ref):
        x = x_ref[...]
        rstd = lax.rsqrt(jnp.mean(x * x, axis=-1, keepdims=True) + EPS)
        xhat = x * rstd
        gg = g_ref[...]
        _stream_sums(i, tm, s, [(dsh_ref, dh), (dsc_ref, dh * (xhat * gg))])
        dy = dh * (1.0 + _mod_row(m_ref, k_scale, _ctx_rows(i, tm, s)))
        dg_ref[...] += jnp.sum(dy * xhat, axis=0, keepdims=True)
        dxh = dy * gg
        dx_ref[...] = dr_ref[...] + rstd * (dxh - xhat * jnp.mean(dxh * xhat, axis=-1, keepdims=True))

    out_shape = [_sds((t, d), F32), _sds((2, 1, d), F32), _sds((2, 1, d), F32), _sds((1, d), F32)]
    if resident:
        def body_resident(*refs):
            a_refs, (w_ref, x_ref, dr_ref, g_ref, m_ref, dx_ref, dsh_ref, dsc_ref, dg_ref) = refs[:n_a], refs[n_a:]
            i = pl.program_id(0)

            @pl.when(i == 0)
            def _():
                dsh_ref[...] = jnp.zeros_like(dsh_ref)
                dsc_ref[...] = jnp.zeros_like(dsc_ref)
                dg_ref[...] = jnp.zeros_like(dg_ref)

            dh, off = None, 0
            for q in range(n_a):
                part = _nt(a_refs[q][...], w_ref[:, off:off + widths[q]])
                dh = part if dh is None else dh + part
                off += widths[q]
            finish(i, dh, x_ref, dr_ref, g_ref, m_ref, dx_ref, dsh_ref, dsc_ref, dg_ref)

        rows = pl.BlockSpec((tm, d), lambda i: (i, 0))
        return _call(body_resident, name=name, grid=(t // tm,),
                     in_specs=[_rows(tm, wd) for wd in widths] + [_w_spec(w, (d, k_all), lambda i: (0, 0)), rows, rows,
                                                                  _fixed((1, d)), _fixed((2, N_MOD, d))],
                     out_specs=[rows, _fixed((2, 1, d)), _fixed((2, 1, d)), _fixed((1, d))],
                     out_shape=out_shape)(*a_list, _w_arr(w), xs, dres, g, mods)

    def body(*refs):
        a_refs, (w_ref, x_ref, dr_ref, g_ref, m_ref, dx_ref, dsh_ref, dsc_ref, dg_ref, acc) = refs[:n_a], refs[n_a:]
        i, k = pl.program_id(0), pl.program_id(1)

        @pl.when((i == 0) & (k == 0))
        def _():
            dsh_ref[...] = jnp.zeros_like(dsh_ref)
            dsc_ref[...] = jnp.zeros_like(dsc_ref)
            dg_ref[...] = jnp.zeros_like(dg_ref)

        @pl.when(k == 0)
        def _():
            acc[...] = jnp.zeros_like(acc)

        for q in range(n_a):
            @pl.when((k >= starts[q]) & (k < starts[q] + counts[q]))
            def _():
                acc[...] += _nt(a_refs[q][...], w_ref[...])

        @pl.when(k == nk - 1)
        def _():
            finish(i, acc[...], x_ref, dr_ref, g_ref, m_ref, dx_ref, dsh_ref, dsc_ref, dg_ref)

    row = pl.BlockSpec((tm, d), lambda i, k: (i, 0))
    a_specs = [pl.BlockSpec((tm, tk), lambda i, k, q=q: (i, jnp.clip(k - starts[q], 0, counts[q] - 1))) for q in range(n_a)]
    return _call(body, name=name, grid=(t // tm, nk),
                 in_specs=a_specs + [_w_spec(w, (d, tk), lambda i, k: (0, k)), row, row, _fixed((1, d)), _fixed((2, N_MOD, d))],
                 out_specs=[row, _fixed((2, 1, d)), _fixed((2, 1, d)), _fixed((1, d))], out_shape=out_shape,
                 scratch=[pltpu.VMEM((tm, d), F32)])(*a_list, _w_arr(w), xs, dres, g, mods)


def _rope_tables(s, ctx_len):
    n_freq = HEAD_DIM // 4
    tok = jnp.arange(s)
    freqs = ROPE_THETA ** (-jnp.arange(n_freq, dtype=F32) / n_freq)
    ang = jnp.concatenate([(tok // GRID_W).astype(F32)[:, None] * freqs, (tok % GRID_W).astype(F32)[:, None] * freqs], axis=-1)
    cos = jnp.repeat(jnp.cos(ang), 2, axis=-1)
    sin = jnp.repeat(jnp.sin(ang), 2, axis=-1) * jnp.tile(jnp.array([-1.0, 1.0], F32), HEAD_DIM // 2)
    cos = jnp.concatenate([jnp.tile(cos, (1, LANES // HEAD_DIM)), jnp.ones((ctx_len, LANES), F32)], axis=0)
    sin = jnp.concatenate([jnp.tile(sin, (1, LANES // HEAD_DIM)), jnp.zeros((ctx_len, LANES), F32)], axis=0)
    return cos, sin


def _swap_pairs(x):
    n = x.shape[-1]
    lane = lax.broadcasted_iota(jnp.int32, x.shape, 1)
    return jnp.where(lane % 2 == 0, pltpu.roll(x, n - 1, 1), pltpu.roll(x, 1, 1))


def _rope(p, cos, sin, name):
    t = p.shape[0]
    w = NA_WIDTH
    te = _row_tile(t)

    def body(q_ref, k_ref, v_ref, c_ref, s_ref, qo_ref, ko_ref, vo_ref):
        c, s = c_ref[...], s_ref[...]
        for hp in range(w // LANES):
            cols = slice(hp * LANES, (hp + 1) * LANES)
            q, k = q_ref[:, cols].astype(F32), k_ref[:, cols].astype(F32)
            qo_ref[:, cols] = (q * c + _swap_pairs(q) * s).astype(qo_ref.dtype)
            ko_ref[:, cols] = (k * c + _swap_pairs(k) * s).astype(ko_ref.dtype)
        vo_ref[...] = v_ref[...].astype(vo_ref.dtype)

    return _call(body, name=name, grid=(t // te,),
                 in_specs=[_rows(te, w, 0), _rows(te, w, 1), _rows(te, w, 2), _rows(te, LANES), _rows(te, LANES)],
                 out_specs=[_rows(te, w)] * 3, out_shape=[_sds((t, w), MXU)] * 3)(p, p, p, cos, sin)


def _rope_bwd(dq, dk, dv, cos, sin, name):
    t = dq.shape[0]
    te = _row_tile(t)
    n_pairs = NA_WIDTH // LANES

    def body(dq_ref, dk_ref, dv_ref, c_ref, s_ref, qo_ref, ko_ref, vo_ref):
        c, s = c_ref[...], s_ref[...]
        for hp in range(n_pairs):
            cols = slice(hp * LANES, (hp + 1) * LANES)
            a, b = dq_ref[:, cols], dk_ref[hp]
            qo_ref[:, cols] = (a * c + _swap_pairs(a * s)).astype(qo_ref.dtype)
            ko_ref[:, cols] = (b * c + _swap_pairs(b * s)).astype(ko_ref.dtype)
            vo_ref[:, cols] = dv_ref[hp].astype(vo_ref.dtype)

    pairs = pl.BlockSpec((n_pairs, te, LANES), lambda i: (0, i, 0))
    return _call(body, name=name, grid=(t // te,),
                 in_specs=[_rows(te, NA_WIDTH), pairs, pairs, _rows(te, LANES), _rows(te, LANES)],
                 out_specs=[_rows(te, NA_WIDTH)] * 3, out_shape=[_sds((t, NA_WIDTH), MXU)] * 3)(dq, dk, dv, cos, sin)


def _na_geometry(r_grid):
    rows = []
    for r0, ks in ((0, 0), (Q_ROWS, 0), (r_grid - Q_ROWS, r_grid - K_ROWS)):
        dr = np.zeros((Q_ROWS, K_ROWS), np.int32)
        vr = np.zeros((Q_ROWS, K_ROWS), bool)
        for a in range(Q_ROWS):
            r = r0 + a
            rs = min(max(r - WIN_H // 2, 0), r_grid - WIN_H)
            for i in range(K_ROWS):
                kr = ks + i
                vr[a, i] = rs <= kr <= rs + WIN_H - 1
                dr[a, i] = kr - r + WIN_H - 1
        rows.append((dr, vr))
    c = np.arange(GRID_W)
    cs = np.clip(c - WIN_W // 2, 0, GRID_W - WIN_W)
    kc = np.arange(GRID_W)
    vc = (kc[None, :] >= cs[:, None]) & (kc[None, :] <= cs[:, None] + WIN_W - 1)
    dc = kc[None, :] - c[:, None] + WIN_W - 1
    return rows, dc, vc


def _bias_table(rpb, r_grid):
    rows, _, vc = _na_geometry(r_grid)
    n_dc, off = 2 * WIN_W - 1, GRID_W - WIN_W
    u = jnp.pad(rpb, ((0, 0), (0, 0), (off, 2 * GRID_W - 1 - off - n_dc)))
    toep = jnp.stack([u[:, :, GRID_W - 1 - c:2 * GRID_W - 1 - c] for c in range(GRID_W)], axis=1)
    toep = jnp.pad(toep, ((0, 0), (0, 0), (Q_ROWS, Q_ROWS), (0, 0)))
    tabs = []
    for dr, vr in rows:
        per_row = []
        for a in range(Q_ROWS):
            lo = int(dr[a, 0]) + Q_ROWS
            blocks = [jnp.where(vc[None], toep[:, :, lo + i, :], NEG) if vr[a, i] else jnp.full((HEADS, GRID_W, GRID_W), NEG, F32)
                      for i in range(K_ROWS)]
            per_row.append(jnp.concatenate(blocks, axis=-1))
        tabs.append(jnp.stack(per_row, axis=1).reshape(HEADS, TQ, TK))
    tabs.append(jnp.full((HEADS, TQ, TK), NEG, F32))
    return jnp.stack(tabs)


def _variant(g, ngx):
    return jnp.where(g == 0, 0, jnp.where(g >= ngx, 3, jnp.where(g == ngx - 1, 2, 1)))


def _key_start(g, r_grid):
    return pl.multiple_of(jnp.clip(g * Q_ROWS - WIN_H // 2, 0, r_grid - K_ROWS) * GRID_W, TQ)


def _nt(a, b):
    return lax.dot_general(a, b, (((1,), (1,)), ((), ())), preferred_element_type=F32)


def _tn(a, b):
    return lax.dot_general(a, b, (((0,), (0,)), ((), ())), preferred_element_type=F32)


def _nn(a, b):
    return jnp.dot(a, b, preferred_element_type=F32)


def _head_mask(h):
    lane = lax.broadcasted_iota(jnp.int32, (1, LANES), 1)
    return ((lane >= HEAD_DIM * h) & (lane < HEAD_DIM * (h + 1))).astype(F32)


def _softmax_parts(qm, knb, kcx, bias):
    s_nb = _nt(qm, knb) + bias
    s_cx = _nt(qm, kcx)
    m = jnp.maximum(jnp.max(s_nb, axis=-1, keepdims=True), jnp.max(s_cx, axis=-1, keepdims=True))
    e_nb = jnp.exp(s_nb - m)
    e_cx = jnp.exp(s_cx - m)
    inv = 1.0 / (jnp.sum(e_nb, axis=-1, keepdims=True) + jnp.sum(e_cx, axis=-1, keepdims=True))
    return e_nb, e_cx, inv


def _na_specs(t, ngx):
    q_spec = pl.BlockSpec((TQ, LANES), lambda hp, g: (g, hp))
    kv_spec = pl.BlockSpec((t, LANES), lambda hp, g: (0, hp))
    b_spec = pl.BlockSpec((1, 2, TQ, TK), lambda hp, g: (_variant(g, ngx), hp, 0, 0))
    return q_spec, kv_spec, b_spec


def _na_fwd(q, k, v, bias, s, name):
    t = q.shape[0]
    ctx_len = t - s
    r_grid = s // GRID_W
    q_spec, kv_spec, b_spec = _na_specs(t, s // TQ)

    def body(q_ref, k_ref, v_ref, b_ref, o_ref):
        start = _key_start(pl.program_id(1), r_grid)
        qf = q_ref[...].astype(F32) * (HEAD_DIM ** -0.5)
        knb, vnb = k_ref[pl.ds(start, TK), :], v_ref[pl.ds(start, TK), :]
        kcx, vcx = k_ref[pl.ds(s, ctx_len), :], v_ref[pl.ds(s, ctx_len), :]
        acc = jnp.zeros((TQ, LANES), F32)
        for h in range(2):
            mask = _head_mask(h)
            e_nb, e_cx, inv = _softmax_parts((qf * mask).astype(MXU), knb, kcx, b_ref[0, h])
            acc += (_nn(e_nb.astype(MXU), vnb) + _nn(e_cx.astype(MXU), vcx)) * (inv * mask)
        o_ref[...] = acc.astype(o_ref.dtype)

    return _call(body, name=name, grid=(NA_WIDTH // LANES, t // TQ), in_specs=[q_spec, kv_spec, kv_spec, b_spec],
                 out_specs=q_spec, out_shape=_sds((t, NA_WIDTH), MXU))(q, k, v, bias)


def _na_bwd(q, k, v, do, bias, s, name):
    t = q.shape[0]
    ctx_len = t - s
    r_grid = s // GRID_W
    ng, ngx = t // TQ, s // TQ
    q_spec, kv_spec, b_spec = _na_specs(t, ngx)

    def body(q_ref, k_ref, v_ref, do_ref, b_ref, dq_ref, dk_hbm, dv_hbm, db_ref, dk_acc, dv_acc):
        hp, g = pl.program_id(0), pl.program_id(1)
        start = _key_start(g, r_grid)

        @pl.when(g == 0)
        def _():
            dk_acc[...] = jnp.zeros_like(dk_acc)
            dv_acc[...] = jnp.zeros_like(dv_acc)

        @pl.when((g == 0) | (g == 1) | (g == ngx - 1) | (g == ngx))
        def _():
            db_ref[...] = jnp.zeros_like(db_ref)

        qf = q_ref[...].astype(F32) * (HEAD_DIM ** -0.5)
        do = do_ref[...].astype(F32)
        knb, vnb = k_ref[pl.ds(start, TK), :], v_ref[pl.ds(start, TK), :]
        kcx, vcx = k_ref[pl.ds(s, ctx_len), :], v_ref[pl.ds(s, ctx_len), :]
        dq = jnp.zeros((TQ, LANES), F32)
        dk_nb = jnp.zeros((TK, LANES), F32)
        dv_nb = jnp.zeros((TK, LANES), F32)
        dk_cx = jnp.zeros((ctx_len, LANES), F32)
        dv_cx = jnp.zeros((ctx_len, LANES), F32)
        for h in range(2):
            mask = _head_mask(h)
            qm = (qf * mask).astype(MXU)
            dom = (do * mask).astype(MXU)
            e_nb, e_cx, inv = _softmax_parts(qm, knb, kcx, b_ref[0, h])
            dp_nb = _nt(dom, vnb)
            dp_cx = _nt(dom, vcx)
            delta = inv * (jnp.sum(e_nb * dp_nb, axis=-1, keepdims=True) + jnp.sum(e_cx * dp_cx, axis=-1, keepdims=True))
            ds_nb = e_nb * (inv * (dp_nb - delta))
            ds_cx = e_cx * (inv * (dp_cx - delta))
            db_ref[0, h] += ds_nb
            ds_nb, ds_cx = ds_nb.astype(MXU), ds_cx.astype(MXU)
            dq += (_nn(ds_nb, knb) + _nn(ds_cx, kcx)) * (mask * (HEAD_DIM ** -0.5))
            dk_nb += _tn(ds_nb, qm)
            dk_cx += _tn(ds_cx, qm)
            dom = (do * (inv * mask)).astype(MXU)
            dv_nb += _tn(e_nb.astype(MXU), dom)
            dv_cx += _tn(e_cx.astype(MXU), dom)
        dq_ref[...] = dq
        dk_acc[pl.ds(start, TK), :] += dk_nb
        dv_acc[pl.ds(start, TK), :] += dv_nb
        dk_acc[pl.ds(s, ctx_len), :] += dk_cx
        dv_acc[pl.ds(s, ctx_len), :] += dv_cx

        @pl.when(g == ng - 1)
        def _():
            pltpu.sync_copy(dk_acc, dk_hbm.at[hp])
            pltpu.sync_copy(dv_acc, dv_hbm.at[hp])

    n_pairs = NA_WIDTH // LANES
    hbm = pl.BlockSpec(memory_space=pl.ANY)
    return _call(body, name=name, grid=(n_pairs, ng), in_specs=[q_spec, kv_spec, kv_spec, q_spec, b_spec],
                 out_specs=[q_spec, hbm, hbm, b_spec],
                 out_shape=[_sds((t, NA_WIDTH), F32), _sds((n_pairs, t, LANES), F32), _sds((n_pairs, t, LANES), F32),
                            _sds((4, HEADS, TQ, TK), F32)],
                 scratch=[pltpu.VMEM((t, LANES), F32), pltpu.VMEM((t, LANES), F32)])(q, k, v, do, bias)


def _rpb_grad(dbias, r_grid, name):
    rows, _, _ = _na_geometry(r_grid)
    n_dr, half, skew, lanes = 2 * WIN_H - 1, WIN_W - 1, TK + 2, 896
    z = dbias[:3].reshape(3, HEADS, Q_ROWS, GRID_W, TK)
    z = jnp.pad(z, ((0, 0),) * 4 + ((0, 1),)).reshape(3, HEADS, Q_ROWS, GRID_W * (TK + 1))
    z = jnp.pad(z, ((0, 0),) * 3 + ((0, GRID_W),)).reshape(3, HEADS, Q_ROWS, GRID_W, skew)
    z = jnp.pad(z, ((0, 0),) * 4 + ((0, lanes - skew),))

    def body(z_ref, o_ref):
        sums = [jnp.sum(z_ref[v, 0, a], axis=0, keepdims=True) for v in range(3) for a in range(Q_ROWS)]
        zs = jnp.concatenate(sums + [jnp.zeros((16 - 3 * Q_ROWS, lanes), F32)], axis=0)
        acc = [jnp.zeros((1, lanes), F32) for _ in range(n_dr)]
        for i in range(K_ROWS):
            if i == 0:
                at0 = pltpu.roll(zs, lanes - (skew - half), 1) + pltpu.roll(zs, half, 1)
            else:
                at0 = pltpu.roll(zs, lanes - (i * GRID_W - half), 1)
            for v, (dr, vr) in enumerate(rows):
                for a in range(Q_ROWS):
                    if vr[a, i]:
                        acc[dr[a, i]] = acc[dr[a, i]] + at0[v * Q_ROWS + a:v * Q_ROWS + a + 1, :]
        o_ref[0] = jnp.concatenate(acc + [jnp.zeros((1, lanes), F32)], axis=0)

    o = _call(body, name=name, grid=(HEADS,),
              in_specs=[pl.BlockSpec((3, 1, Q_ROWS, GRID_W, lanes), lambda h: (0, h, 0, 0, 0))],
              out_specs=pl.BlockSpec((1, 16, lanes), lambda h: (h, 0, 0)), out_shape=_sds((HEADS, 16, lanes), F32))(z)
    return o[:, :n_dr, :2 * WIN_W - 1]


_GELU_K, _GELU_C = 0.7978845608028654, 0.044715


def _gelu(x):
    return 0.5 * x * (1.0 + jnp.tanh(_GELU_K * (x + _GELU_C * x * x * x)))


def _gelu_grad(x):
    th = jnp.tanh(_GELU_K * (x + _GELU_C * x * x * x))
    return 0.5 * (1.0 + th) + 0.5 * x * (1.0 - th * th) * (_GELU_K * (1.0 + 3.0 * _GELU_C * x * x))


def _ln_stats(v):
    mu = jnp.mean(v, axis=-1, keepdims=True)
    vc = v - mu
    rstd = lax.rsqrt(jnp.mean(vc * vc, axis=-1, keepdims=True) + EPS)
    return vc * rstd, rstd


def _gmlp(p, ln_g, ln_b, w_s, b_s, name):
    t = p.shape[0]
    te = _row_tile(t)
    w = SG_WIDTH
    cw = w // SG_GROUPS

    def body(u_ref, v_ref, g_ref, b_ref, ws_ref, bs_ref, o_ref):
        xhat, _ = _ln_stats(_gelu(v_ref[...].astype(F32)))
        vn = (xhat * g_ref[...] + b_ref[...]).astype(MXU)
        ug = _gelu(u_ref[...].astype(F32))
        for ci in range(te // SG_CHUNK):
            rs = slice(ci * SG_CHUNK, (ci + 1) * SG_CHUNK)
            for gi in range(SG_GROUPS):
                cs = slice(gi * cw, (gi + 1) * cw)
                sg = _nn(ws_ref[gi].astype(MXU), vn[rs, cs]) + bs_ref[gi]
                o_ref[rs, cs] = (ug[rs, cs] * sg).astype(o_ref.dtype)

    return _call(body, name=name, grid=(t // te,),
                 in_specs=[_rows(te, w, 3), _rows(te, w, 4), _fixed((1, w)), _fixed((1, w)),
                           _fixed((SG_GROUPS, SG_CHUNK, SG_CHUNK)), _fixed((SG_GROUPS, SG_CHUNK, 1))],
                 out_specs=_rows(te, w), out_shape=_sds((t, w), MXU))(p, p, ln_g, ln_b, w_s, b_s)


def _gmlp_bwd(p, dob, ln_g, ln_b, w_s, b_s, name):
    t = p.shape[0]
    te = _row_tile(t)
    w = SG_WIDTH
    cw = w // SG_GROUPS

    def body(u_ref, v_ref, do_ref, g_ref, b_ref, ws_ref, bs_ref, du_ref, dv_ref, dws_ref, dbs_ref, dg_ref, db_ref, dvn_ref):
        @pl.when(pl.program_id(0) == 0)
        def _():
            dws_ref[...] = jnp.zeros_like(dws_ref)
            dbs_ref[...] = jnp.zeros_like(dbs_ref)
            dg_ref[...] = jnp.zeros_like(dg_ref)
            db_ref[...] = jnp.zeros_like(db_ref)

        u, v = u_ref[...].astype(F32), v_ref[...].astype(F32)
        xhat, rstd = _ln_stats(_gelu(v))
        vn = (xhat * g_ref[...] + b_ref[...]).astype(MXU)
        ug = _gelu(u)
        dob = do_ref[...].astype(F32)
        for ci in range(te // SG_CHUNK):
            rs = slice(ci * SG_CHUNK, (ci + 1) * SG_CHUNK)
            for gi in range(SG_GROUPS):
                cs = slice(gi * cw, (gi + 1) * cw)
                wsg = ws_ref[gi].astype(MXU)
                sg = _nn(wsg, vn[rs, cs]) + bs_ref[gi]
                du_ref[rs, cs] = (dob[rs, cs] * sg * _gelu_grad(u[rs, cs])).astype(du_ref.dtype)
                ds = dob[rs, cs] * ug[rs, cs]
                dbs_ref[gi] += jnp.sum(ds, axis=-1, keepdims=True)
                ds = ds.astype(MXU)
                dws_ref[gi] += _nt(ds, vn[rs, cs])
                dvn_ref[rs, cs] = _tn(wsg, ds)
        dvn = dvn_ref[...]
        dg_ref[...] += jnp.sum(dvn * xhat, axis=0, keepdims=True)
        db_ref[...] += jnp.sum(dvn, axis=0, keepdims=True)
        dxh = dvn * g_ref[...]
        dvg = rstd * (dxh - jnp.mean(dxh, axis=-1, keepdims=True) - xhat * jnp.mean(dxh * xhat, axis=-1, keepdims=True))
        dv_ref[...] = (dvg * _gelu_grad(v)).astype(dv_ref.dtype)

    return _call(body, name=name, grid=(t // te,),
                 in_specs=[_rows(te, w, 3), _rows(te, w, 4), _rows(te, w), _fixed((1, w)), _fixed((1, w)),
                           _fixed((SG_GROUPS, SG_CHUNK, SG_CHUNK)), _fixed((SG_GROUPS, SG_CHUNK, 1))],
                 out_specs=[_rows(te, w), _rows(te, w), _fixed((SG_GROUPS, SG_CHUNK, SG_CHUNK)),
                            _fixed((SG_GROUPS, SG_CHUNK, 1)), _fixed((1, w)), _fixed((1, w))],
                 out_shape=[_sds((t, w), MXU), _sds((t, w), MXU), _sds((SG_GROUPS, SG_CHUNK, SG_CHUNK), F32),
                            _sds((SG_GROUPS, SG_CHUNK, 1), F32), _sds((1, w), F32), _sds((1, w), F32)],
                 scratch=[pltpu.VMEM((te, w), F32)])(p, p, dob, ln_g, ln_b, w_s, b_s)


def _merge(pa, pb, p, b_gate, name):
    t, d = pa.shape
    te = _row_tile(t)
    hw = NA_WIDTH
    nh = d // hw
    c0 = (NA_WIDTH * 3 + SG_WIDTH * 2) // hw

    def body(pa_ref, pb_ref, la_ref, lb_ref, ba_ref, bb_ref, o_ref):
        ga = jax.nn.sigmoid(la_ref[...].astype(F32) + ba_ref[...])
        gb = jax.nn.sigmoid(lb_ref[...].astype(F32) + bb_ref[...])
        o_ref[...] = (ga * pa_ref[...].astype(F32) + gb * pb_ref[...].astype(F32)).astype(o_ref.dtype)

    tile = pl.BlockSpec((te, hw), lambda i, j: (i, j))
    return _call(body, name=name, grid=(t // te, nh),
                 in_specs=[tile, tile, pl.BlockSpec((te, hw), lambda i, j: (i, c0 + j)),
                           pl.BlockSpec((te, hw), lambda i, j: (i, c0 + nh + j)),
                           pl.BlockSpec((1, hw), lambda i, j: (0, j)), pl.BlockSpec((1, hw), lambda i, j: (0, nh + j))],
                 out_specs=tile, out_shape=_sds((t, d), MXU))(pa, pb, p, p, b_gate, b_gate)


def _merge_bwd(dmg, pa, pb, p, b_gate, name):
    t, d = pa.shape
    te = _row_tile(t)
    hw = NA_WIDTH
    nh = d // hw
    c0 = (NA_WIDTH * 3 + SG_WIDTH * 2) // hw

    def body(dm_ref, pa_ref, pb_ref, la_ref, lb_ref, ba_ref, bb_ref, dpa_ref, dpb_ref, dla_ref, dlb_ref, dba_ref, dbb_ref):
        @pl.when(pl.program_id(1) == 0)
        def _():
            dba_ref[...] = jnp.zeros_like(dba_ref)
            dbb_ref[...] = jnp.zeros_like(dbb_ref)

        dm = dm_ref[...].astype(F32)
        ga = jax.nn.sigmoid(la_ref[...].astype(F32) + ba_ref[...])
        gb = jax.nn.sigmoid(lb_ref[...].astype(F32) + bb_ref[...])
        dpa_ref[...] = (dm * ga).astype(dpa_ref.dtype)
        dpb_ref[...] = (dm * gb).astype(dpb_ref.dtype)
        dla = dm * pa_ref[...].astype(F32) * ga * (1.0 - ga)
        dlb = dm * pb_ref[...].astype(F32) * gb * (1.0 - gb)
        dla_ref[...] = dla.astype(dla_ref.dtype)
        dlb_ref[...] = dlb.astype(dlb_ref.dtype)
        dba_ref[...] += jnp.sum(dla, axis=0, keepdims=True)
        dbb_ref[...] += jnp.sum(dlb, axis=0, keepdims=True)

    tile = pl.BlockSpec((te, hw), lambda j, i: (i, j))
    bias_a = pl.BlockSpec((1, hw), lambda j, i: (0, j))
    bias_b = pl.BlockSpec((1, hw), lambda j, i: (0, nh + j))
    return _call(body, name=name, grid=(nh, t // te),
                 in_specs=[tile, tile, tile, pl.BlockSpec((te, hw), lambda j, i: (i, c0 + j)),
                           pl.BlockSpec((te, hw), lambda j, i: (i, c0 + nh + j)), bias_a, bias_b],
                 out_specs=[tile, tile, tile, tile, bias_a, bias_a],
                 out_shape=[_sds((t, d), MXU)] * 4 + [_sds((1, d), F32)] * 2)(dmg, pa, pb, p, p, b_gate, b_gate)


def _final(xs, tgt, g, name):
    t, d = xs.shape
    nx = tgt.shape[0] // TM

    def body(x_ref, t_ref, g_ref, l_ref, dx_ref, dg_ref):
        i = pl.program_id(0)

        @pl.when(i == 0)
        def _():
            l_ref[...] = jnp.zeros_like(l_ref)
            dg_ref[...] = jnp.zeros_like(dg_ref)

        @pl.when(i < nx)
        def _():
            x = x_ref[...]
            rstd = lax.rsqrt(jnp.mean(x * x, axis=-1, keepdims=True) + EPS)
            xhat = x * rstd
            err = xhat * g_ref[...] - t_ref[...]
            l_ref[...] += 0.5 * jnp.sum(jnp.mean(err * err, axis=-1, keepdims=True))
            dy = err * (1.0 / d)
            dg_ref[...] += jnp.sum(dy * xhat, axis=0, keepdims=True)
            dxh = dy * g_ref[...]
            dx_ref[...] = rstd * (dxh - xhat * jnp.mean(dxh * xhat, axis=-1, keepdims=True))

        @pl.when(i >= nx)
        def _():
            dx_ref[...] = jnp.zeros_like(dx_ref)

    return _call(body, name=name, grid=(t // TM,),
                 in_specs=[_rows(TM, d), pl.BlockSpec((TM, d), lambda i: (jnp.minimum(i, nx - 1), 0)), _fixed((1, d))],
                 out_specs=[_fixed((1, LANES)), _rows(TM, d), _fixed((1, d))],
                 out_shape=[_sds((1, LANES), F32), _sds((t, d), F32), _sds((1, d), F32)])(xs, tgt, g)


def _view2d(a):
    return a.reshape(1, -1) if a.ndim == 1 else a.reshape(-1, a.shape[-1])


def _tile_rows(r, c):
    for cand in (1024, 512, 256, 128, 64, 32, 16):
        if r % cand == 0 and cand * c * 4 <= 2 ** 20:
            return cand
    return r


def _pair_sum(g, recv, layer, name):
    _, a, b = g.shape
    tr = _tile_rows(a, b)

    def body(l_ref, g_ref, r_ref, o32_ref, o16_ref):
        acc = g_ref[...] + r_ref[...]
        o32_ref[...] = acc
        o16_ref[...] = acc.astype(o16_ref.dtype)

    first = pl.BlockSpec((None, tr, b), lambda i, l: (0, i, 0))
    return pl.pallas_call(
        body, name=name, out_shape=[_sds((1, a, b), F32), _sds((1, a, b), MXU)],
        grid_spec=pltpu.PrefetchScalarGridSpec(
            num_scalar_prefetch=1, grid=(a // tr,),
            in_specs=[pl.BlockSpec((None, tr, b), lambda i, l: (l[0], i, 0)), first], out_specs=[first, first]),
        compiler_params=pltpu.CompilerParams(dimension_semantics=("arbitrary",), vmem_limit_bytes=VMEM_LIMIT))(layer, g, recv)


def _ew(fn, arrays, out_dtypes, name):
    shape = arrays[0].shape
    views = [_view2d(a) for a in arrays]
    r, c = views[0].shape
    tr = _tile_rows(r, c)

    def body(*refs):
        outs = fn(*[ref[...] for ref in refs[:len(views)]])
        for ref, o in zip(refs[len(views):], outs):
            ref[...] = o.astype(ref.dtype)

    res = _call(body, name=name, grid=(r // tr,), in_specs=[_rows(tr, c)] * len(views), out_specs=[_rows(tr, c)] * len(out_dtypes),
                out_shape=[_sds((r, c), dt) for dt in out_dtypes])(*views)
    return [o.reshape(shape) for o in res]


def _sum_pieces(pieces, name, out_dtypes=(F32,)):
    def fn(*vals):
        acc = vals[0].astype(F32)
        for v in vals[1:]:
            acc = acc + v.astype(F32)
        return (acc,) * len(out_dtypes)

    return _ew(fn, pieces, list(out_dtypes), name)


def _adam_update(w, g, m, v):
    m2 = ADAM_B1 * m + (1.0 - ADAM_B1) * g
    v2 = ADAM_B2 * v + (1.0 - ADAM_B2) * (g * g)
    m_hat = m2 / (1.0 - ADAM_B1 ** ADAM_STEP)
    v_hat = v2 / (1.0 - ADAM_B2 ** ADAM_STEP)
    delta = -ADAM_LR * (m_hat / (jnp.sqrt(v_hat) + ADAM_EPS) + ADAM_WD * w)
    return g, delta, m2, v2


def _adamw(w, g_pieces, m, v, name):
    n_g = len(g_pieces)

    def fn(w_, *rest):
        g = rest[0]
        for piece in rest[1:n_g]:
            g = g + piece
        return _adam_update(w_, g, rest[n_g], rest[n_g + 1])

    return _ew(fn, [w, *g_pieces, m, v], [F32] * 4, name)


def _adamw_layers(w, mine, other, m, v, layer, name):
    _, a, b = w.shape
    tr = _tile_rows(a, b)
    nb = a // tr

    def body(l_ref, w_ref, mine_ref, other_ref, m_ref, v_ref, *o_refs):
        g = jnp.where(pl.program_id(0) // nb == l_ref[0], mine_ref[...], other_ref[...])
        for ref, val in zip(o_refs, _adam_update(w_ref[...], g, m_ref[...], v_ref[...])):
            ref[...] = val

    both = pl.BlockSpec((None, tr, b), lambda i, l: (i // nb, i % nb, 0))
    one = pl.BlockSpec((None, tr, b), lambda i, l: (0, i % nb, 0))
    return pl.pallas_call(
        body, name=name, out_shape=[_sds(w.shape, F32)] * 4,
        grid_spec=pltpu.PrefetchScalarGridSpec(num_scalar_prefetch=1, grid=(2 * nb,), in_specs=[both, one, one, both, both],
                                               out_specs=[both] * 4),
        compiler_params=pltpu.CompilerParams(dimension_semantics=("arbitrary",), vmem_limit_bytes=VMEM_LIMIT))(
            layer, w, mine, other, m, v)


def _place_shard(full, shard, chip, axis, name):
    _, a, b = shard.shape
    tr = _tile_rows(a, b)
    nb = a // tr

    def body(c_ref, s_ref, f_ref, o_ref):
        o_ref[...] = s_ref[...]

    if axis == 1:
        out_spec = pl.BlockSpec((None, tr, b), lambda l, i, c: (l, c[0] * nb + i, 0))
    else:
        out_spec = pl.BlockSpec((None, tr, b), lambda l, i, c: (l, i, c[0]))
    return pl.pallas_call(
        body, name=name, out_shape=_sds(full.shape, full.dtype), input_output_aliases={2: 0},
        grid_spec=pltpu.PrefetchScalarGridSpec(
            num_scalar_prefetch=1, grid=(2, nb),
            in_specs=[pl.BlockSpec((None, tr, b), lambda l, i, c: (l, i, 0)), pl.BlockSpec(memory_space=pl.ANY)],
            out_specs=out_spec),
        compiler_params=pltpu.CompilerParams(dimension_semantics=("arbitrary", "arbitrary"), vmem_limit_bytes=VMEM_LIMIT))(
            chip, shard, full)


def _ada_fwd(cond, w, b, name):
    r, d = cond.shape
    n = w.shape[1]
    tn = _pick(n, (1152, 768, 512, 384, 256, 128))

    def body(c_ref, w_ref, b_ref, o_ref, s_ref):
        c = c_ref[...]
        sc = c * jax.nn.sigmoid(c)
        s_ref[...] = sc
        o_ref[...] = _nn(sc.astype(MXU), w_ref[...].astype(MXU)) + b_ref[...]

    return _call(body, name=name, grid=(n // tn,),
                 in_specs=[_fixed((r, d)), pl.BlockSpec((d, tn), lambda j: (0, j)), pl.BlockSpec((1, tn), lambda j: (0, j))],
                 out_specs=[pl.BlockSpec((r, tn), lambda j: (0, j)), _fixed((r, d))],
                 out_shape=[_sds((r, n), F32), _sds((r, d), F32)])(cond, w, b)


def _cctx_grad(parts, c_ctx, name):
    n, d = parts.shape

    def body(p_ref, c_ref, o_ref):
        c = c_ref[...]
        sg = jax.nn.sigmoid(c)
        acc = p_ref[0:1, :]
        for j in range(1, n):
            acc = acc + p_ref[j:j + 1, :]
        o_ref[...] = acc * (sg * (1.0 + c * (1.0 - sg)))

    return _call(body, name=name, grid=(1,), in_specs=[_fixed((n, d)), _fixed((1, d))], out_specs=_fixed((1, d)),
                 out_shape=_sds((1, d), F32))(parts, c_ctx)


def _here():
    return lax.axis_index("x"), lax.axis_index("y"), lax.axis_index("c")


def _flip(v, bit):
    return 1 - v if bit else v


def _allgather8(xb, name):
    r, n = xb.shape

    def body(x_ref, out_ref, send_sems, recv_sems, local_sem):
        x, y, c = _here()
        me = 4 * x + 2 * y + c
        local = pltpu.make_async_copy(x_ref, out_ref.at[me], local_sem)
        local.start()
        sends = []
        for k in range(1, 8):
            peer = (_flip(x, k & 4), _flip(y, k & 2), _flip(c, k & 1))
            cp = pltpu.make_async_remote_copy(src_ref=x_ref, dst_ref=out_ref.at[me], send_sem=send_sems.at[k - 1],
                                              recv_sem=recv_sems.at[k - 1], device_id=peer, device_id_type=MESH)
            cp.start()
            sends.append(cp)
        for k in range(1, 8):
            peer = (_flip(x, k & 4), _flip(y, k & 2), _flip(c, k & 1))
            src = 4 * peer[0] + 2 * peer[1] + peer[2]
            pltpu.make_async_remote_copy(src_ref=x_ref, dst_ref=out_ref.at[src], send_sem=send_sems.at[k - 1],
                                         recv_sem=recv_sems.at[k - 1], device_id=peer, device_id_type=MESH).wait_recv()
        for cp in sends:
            cp.wait_send()
        local.wait()

    vmem = pl.BlockSpec(memory_space=pltpu.VMEM)
    return pl.pallas_call(
        body, name=name, out_shape=_sds((8, r, n), xb.dtype), in_specs=[vmem], out_specs=vmem,
        scratch_shapes=[pltpu.SemaphoreType.DMA((7,)), pltpu.SemaphoreType.DMA((7,)), pltpu.SemaphoreType.DMA(())],
        compiler_params=pltpu.CompilerParams(vmem_limit_bytes=VMEM_LIMIT))(xb)


def _shard_of(ref, axis, j, size):
    sl = pl.ds(j * size, size)
    return ref.at[:, sl, :] if axis == 1 else ref.at[:, :, sl]


def _piece(ref, axis, j, size, layer):
    lay, sl = pl.ds(layer, 1), pl.ds(j * size, size)
    return ref.at[lay, sl, :] if axis == 1 else ref.at[lay, :, sl]


def _gather_chips(shards, axes, name):
    n = len(shards)
    fulls = []
    for a, ax in zip(shards, axes):
        assert a.shape[0] == 2
        shp = list(a.shape)
        shp[ax] *= 4
        fulls.append(_sds(tuple(shp), a.dtype))

    def body(*refs):
        ins, outs = refs[:n], refs[n:2 * n]
        ici_send, ici_recv, d2d_send, d2d_recv = refs[2 * n:]
        x, y, c = _here()
        chips = [(_flip(x, k & 2), _flip(y, k & 1)) for k in range(1, 4)]
        sends = []
        for a in range(n):
            size = ins[a].shape[axes[a]]
            for j, (px, py) in enumerate(chips):
                cp = pltpu.make_async_remote_copy(src_ref=ins[a].at[pl.ds(c, 1)], dst_ref=_piece(outs[a], axes[a], 2 * x + y, size, c),
                                                  send_sem=ici_send.at[3 * a + j], recv_sem=ici_recv.at[3 * a + j],
                                                  device_id=(px, py, c), device_id_type=MESH)
                cp.start()
                sends.append(cp)
        for a in range(n):
            size = ins[a].shape[axes[a]]
            for j, (px, py) in enumerate(chips):
                landed = _piece(outs[a], axes[a], 2 * px + py, size, c)
                pltpu.make_async_remote_copy(src_ref=ins[a].at[pl.ds(c, 1)], dst_ref=landed, send_sem=ici_send.at[3 * a + j],
                                             recv_sem=ici_recv.at[3 * a + j], device_id=(px, py, c), device_id_type=MESH).wait_recv()
                cp = pltpu.make_async_remote_copy(src_ref=landed, dst_ref=landed, send_sem=d2d_send.at[3 * a + j],
                                                  recv_sem=d2d_recv.at[3 * a + j], device_id=(x, y, 1 - c), device_id_type=MESH)
                cp.start()
                sends.append(cp)
        for a in range(n):
            size = ins[a].shape[axes[a]]
            for j, (px, py) in enumerate(chips):
                passed = _piece(outs[a], axes[a], 2 * px + py, size, 1 - c)
                pltpu.make_async_remote_copy(src_ref=passed, dst_ref=passed, send_sem=d2d_send.at[3 * a + j],
                                             recv_sem=d2d_recv.at[3 * a + j], device_id=(x, y, 1 - c), device_id_type=MESH).wait_recv()
        for cp in sends:
            cp.wait_send()

    hbm = pl.BlockSpec(memory_space=pl.ANY)
    return pl.pallas_call(
        body, name=name, out_shape=fulls, in_specs=[hbm] * n, out_specs=[hbm] * n,
        scratch_shapes=[pltpu.SemaphoreType.DMA((3 * n,))] * 4)(*shards)


def _swap_layers(arrays, name):
    n = len(arrays)

    def body(*refs):
        ins, outs = refs[:n], refs[n:2 * n]
        send_sems, recv_sems = refs[2 * n:]
        x, y, c = _here()
        copies = []
        for a in range(n):
            cp = pltpu.make_async_remote_copy(src_ref=ins[a].at[pl.ds(1 - c, 1)], dst_ref=outs[a], send_sem=send_sems.at[a],
                                              recv_sem=recv_sems.at[a], device_id=(x, y, 1 - c), device_id_type=MESH)
            cp.start()
            copies.append(cp)
        for cp in copies:
            cp.wait()

    hbm = pl.BlockSpec(memory_space=pl.ANY)
    return pl.pallas_call(
        body, name=name, out_shape=[_sds((1, *a.shape[1:]), a.dtype) for a in arrays], in_specs=[hbm] * n, out_specs=[hbm] * n,
        scratch_shapes=[pltpu.SemaphoreType.DMA((n,)), pltpu.SemaphoreType.DMA((n,))])(*arrays)


def _scatter_chips(fulls, axes, name):
    n = len(fulls)
    recvs = []
    for a, ax in zip(fulls, axes):
        shp = list(a.shape)
        shp[ax] //= 4
        recvs.append(_sds((3, *shp), a.dtype))

    def body(*refs):
        ins, outs = refs[:n], refs[n:2 * n]
        send_sems, recv_sems = refs[2 * n:]
        x, y, c = _here()
        sends = []
        for a in range(n):
            size = ins[a].shape[axes[a]] // 4
            for k in range(1, 4):
                peer = (_flip(x, k & 2), _flip(y, k & 1), c)
                cp = pltpu.make_async_remote_copy(src_ref=_shard_of(ins[a], axes[a], 2 * peer[0] + peer[1], size),
                                                  dst_ref=outs[a].at[k - 1],
                                                  send_sem=send_sems.at[3 * a + k - 1], recv_sem=recv_sems.at[3 * a + k - 1],
                                                  device_id=peer, device_id_type=MESH)
                cp.start()
                sends.append(cp)
        for cp in sends:
            cp.wait_recv()
        for cp in sends:
            cp.wait_send()

    hbm = pl.BlockSpec(memory_space=pl.ANY)
    return pl.pallas_call(
        body, name=name, out_shape=recvs, in_specs=[hbm] * n, out_specs=[hbm] * n,
        scratch_shapes=[pltpu.SemaphoreType.DMA((3 * n,)), pltpu.SemaphoreType.DMA((3 * n,))])(*fulls)


def _sibling_swap(arrays, name):
    n = len(arrays)

    def body(*refs):
        ins, outs = refs[:n], refs[n:2 * n]
        send_sems, recv_sems = refs[2 * n:]
        x, y, c = _here()
        copies = []
        for a in range(n):
            cp = pltpu.make_async_remote_copy(src_ref=ins[a], dst_ref=outs[a], send_sem=send_sems.at[a], recv_sem=recv_sems.at[a],
                                              device_id=(x, y, 1 - c), device_id_type=MESH)
            cp.start()
            copies.append(cp)
        for cp in copies:
            cp.wait()

    hbm = pl.BlockSpec(memory_space=pl.ANY)
    return pl.pallas_call(
        body, name=name, out_shape=[_sds(a.shape, a.dtype) for a in arrays], in_specs=[hbm] * n, out_specs=[hbm] * n,
        scratch_shapes=[pltpu.SemaphoreType.DMA((n,)), pltpu.SemaphoreType.DMA((n,))])(*arrays)


def _ffn_fwd(xs, g, mods, k0, w_up, w_down, s, tag):
    h, ua, ub, act = _norm_mm(xs, g, mods, k0, k0 + 1, w_up, s, tag + "_up", True)
    y, xn = _mm_res(act, w_down, xs, mods, k0 + 2, 0.5, s, tag + "_down")
    return xn, (xs, h, ua, ub, act, y)


def _dw(gw, key, a, b, name, col0=0, n_total=None):
    shape = (gw["depth"], a.shape[1], n_total or b.shape[1])
    gw[key] = _mm(a, b, "tn", F32, name, into=(gw.get(key), shape, gw["layer"], col0))


def _ffn_bwd(dxn, saved, g, mods, k0, w_up, w_down, s, tag, gw, up_key, down_key):
    xs, h, ua, ub, act, y = saved
    dy, dgate, dua, dub = _resb_mm(dxn, y, mods, k0 + 2, 0.5, w_down, s, tag + "_down_dx", (ua, ub))
    _dw(gw, down_key, act, dy, tag + "_down_dw")
    f = dua.shape[1]
    _dw(gw, up_key, h, dua, tag + "_upa_dw", 0, 2 * f)
    _dw(gw, up_key, h, dub, tag + "_upb_dw", f, 2 * f)
    dx, dsh, dsc, dg = _mm_normb([dua, dub], w_up, xs, dxn, g, mods, k0 + 1, s, tag + "_up_dx")
    return dx, dg, [dsh, dsc, dgate]


def _mix_fwd(xs, g, mods, wl, pl_, tabs, s, tag):
    cos, sin = tabs
    h, p = _norm_mm(xs, g, mods, 3, 4, wl["w_in"], s, tag + "_in", False)
    q, k, v = _rope(p, cos, sin, tag + "_rope")
    bias = _bias_table(pl_["rpb"], s // GRID_W)
    oa = _na_fwd(q, k, v, bias, s, tag + "_na")
    ob = _gmlp(p, pl_["ln_v_g"], pl_["ln_v_b"], pl_["w_s"], pl_["b_s"], tag + "_sg")
    pa = _mm(oa, wl["w_pa"], "nn", MXU, tag + "_pa")
    pb = _mm(ob, wl["w_pb"], "nn", MXU, tag + "_pb")
    mg = _merge(pa, pb, p, pl_["b_gate"], tag + "_merge")
    y, xn = _mm_res(mg, wl["w_o"], xs, mods, 5, 1.0, s, tag + "_o")
    return xn, (xs, h, p, q, k, v, bias, oa, ob, pa, pb, mg, y)


def _mix_bwd(dxn, saved, g, mods, wl, pl_, tabs, s, tag, gw):
    xs, h, p, q, k, v, bias, oa, ob, pa, pb, mg, y = saved
    cos, sin = tabs
    gp = {}
    dy, dgate, dmg = _resb_mm(dxn, y, mods, 5, 1.0, wl["w_o"], s, tag + "_o_dx")
    _dw(gw, "w_o", mg, dy, tag + "_o_dw")
    dpa, dpb, dla, dlb, dba, dbb = _merge_bwd(dmg, pa, pb, p, pl_["b_gate"], tag + "_merge_b")
    gp["b_gate"] = jnp.concatenate([dba, dbb], axis=1)
    _dw(gw, "w_pa", oa, dpa, tag + "_pa_dw")
    doa = _mm(dpa, wl["w_pa"], "nt", MXU, tag + "_pa_dx")
    _dw(gw, "w_pb", ob, dpb, tag + "_pb_dw")
    dob = _mm(dpb, wl["w_pb"], "nt", MXU, tag + "_pb_dx")
    du, dvs, gp["w_s"], gp["b_s"], gp["ln_v_g"], gp["ln_v_b"] = _gmlp_bwd(
        p, dob, pl_["ln_v_g"], pl_["ln_v_b"], pl_["w_s"], pl_["b_s"], tag + "_sg_b")
    dqr, dkr, dv, dbias = _na_bwd(q, k, v, doa, bias, s, tag + "_na_b")
    gp["rpb"] = _rpb_grad(dbias, s // GRID_W, tag + "_rpb")
    dq, dk, dvv = _rope_bwd(dqr, dkr, dv, cos, sin, tag + "_rope_b")
    dp = [dq, dk, dvv, du, dvs, dla, dlb]
    n_in = sum(piece.shape[1] for piece in dp)
    for j, piece in enumerate(dp):
        _dw(gw, "w_in", h, piece, tag + f"_in_dw{j}", sum(q.shape[1] for q in dp[:j]), n_in)
    dx, dsh, dsc, dg = _mm_normb(dp, wl["w_in"], xs, dxn, g, mods, 4, s, tag + "_in_dx")
    return dx, gp, dg, [dsh, dsc, dgate]


def _local_step(x, ctx, tgt, mods, wts, prm):
    s, d = x.shape
    depth = mods.shape[0]
    tabs = _rope_tables(s, ctx.shape[0])
    xs = jnp.concatenate([x, ctx], axis=0)
    saved = []
    for l in range(depth):
        wl = {k: (v, l) for k, v in wts.items()}
        pl_ = _layer_params(prm, l)
        xs, s1 = _ffn_fwd(xs, pl_["g"][0], mods[l], 0, wl["w_ff1_up"], wl["w_ff1_down"], s, f"l{l}_ff1")
        xs, s2 = _mix_fwd(xs, pl_["g"][1], mods[l], wl, pl_, tabs, s, f"l{l}_mix")
        xs, s3 = _ffn_fwd(xs, pl_["g"][2], mods[l], 6, wl["w_ff2_up"], wl["w_ff2_down"], s, f"l{l}_ff2")
        saved.append((s1, s2, s3))
    loss, dxs, d_final_g = _final(xs, tgt, prm["final_g"].reshape(1, d), "final")
    gw = {"depth": depth}
    gp = {k: [None] * depth for k in ("norm_g", "b_gate", "rpb", "ln_v_g", "ln_v_b", "w_s", "b_s")}
    dmods = [None] * depth
    for l in reversed(range(depth)):
        wl = {k: (v, l) for k, v in wts.items()}
        pl_ = _layer_params(prm, l)
        s1, s2, s3 = saved[l]
        gw["layer"] = l
        dxs, dg2, dm2 = _ffn_bwd(dxs, s3, pl_["g"][2], mods[l], 6, wl["w_ff2_up"], wl["w_ff2_down"], s, f"l{l}_ff2",
                                 gw, "w_ff2_up", "w_ff2_down")
        dxs, gpm, dg1, dm1 = _mix_bwd(dxs, s2, pl_["g"][1], mods[l], wl, pl_, tabs, s, f"l{l}_mix", gw)
        dxs, dg0, dm0 = _ffn_bwd(dxs, s1, pl_["g"][0], mods[l], 0, wl["w_ff1_up"], wl["w_ff1_down"], s, f"l{l}_ff1",
                                 gw, "w_ff1_up", "w_ff1_down")
        gp["b_gate"][l] = gpm["b_gate"][0]
        gp["rpb"][l] = gpm["rpb"]
        gp["ln_v_g"][l] = gpm["ln_v_g"][0]
        gp["ln_v_b"][l] = gpm["ln_v_b"][0]
        gp["w_s"][l] = gpm["w_s"]
        gp["b_s"][l] = gpm["b_s"][..., 0]
        gp["norm_g"][l] = jnp.concatenate([dg0, dg1, dg2], axis=0)
        dmods[l] = jnp.concatenate(dm0 + dm1 + dm2, axis=1)
    gw = {k: gw[k] for k in wts}
    gp = {k: jnp.stack(v) for k, v in gp.items()}
    gp["final_g"] = d_final_g[0]
    return loss[0, 0], dxs[:s], jnp.stack(dmods), gw, gp


def _layer_params(prm, l):
    d = prm["norm_g"].shape[-1]
    return {
        "g": [prm["norm_g"][l, i].reshape(1, d) for i in range(3)],
        "b_gate": prm["b_gate"][l].reshape(1, -1),
        "rpb": prm["rpb"][l],
        "ln_v_g": prm["ln_v_g"][l].reshape(1, -1),
        "ln_v_b": prm["ln_v_b"][l].reshape(1, -1),
        "w_s": prm["w_s"][l],
        "b_s": prm["b_s"][l][..., None],
    }


SMALL = ("norm_g", "b_gate", "rpb", "ln_v_g", "ln_v_b", "w_s", "b_s", "final_g")
PACK_LANES = 1024


def _pack(parts):
    flat = jnp.concatenate([p.reshape(-1) for p in parts])
    rows = -(-flat.shape[0] // PACK_LANES)
    rows = -(-rows // 8) * 8
    return jnp.pad(flat, (0, rows * PACK_LANES - flat.shape[0])).reshape(rows, PACK_LANES)


def _unpack(flat, shapes):
    out, off = [], 0
    for shp in shapes:
        n = int(np.prod(shp))
        out.append(flat[..., off:off + n].reshape(*flat.shape[:-1], *shp))
        off += n
    return out


def kernel(x, c, ctx, c_ctx, w_ada, b_ada, norm_g, w_ff1_up, w_ff1_down, w_in, b_gate, rpb, ln_v_g, ln_v_b, w_s, b_s, w_pa, w_pb, w_o, w_ff2_up, w_ff2_down, final_g, loss_target, m_c_ctx, m_w_ada, m_b_ada, m_norm_g, m_w_ff1_up, m_w_ff1_down, m_w_in, m_b_gate, m_rpb, m_ln_v_g, m_ln_v_b, m_w_s, m_b_s, m_w_pa, m_w_pb, m_w_o, m_w_ff2_up, m_w_ff2_down, m_final_g, v_c_ctx, v_w_ada, v_b_ada, v_norm_g, v_w_ff1_up, v_w_ff1_down, v_w_in, v_b_gate, v_rpb, v_ln_v_g, v_ln_v_b, v_w_s, v_b_s, v_w_pa, v_w_pb, v_w_o, v_w_ff2_up, v_w_ff2_down, v_final_g):
    weights = dict(c_ctx=c_ctx, w_ada=w_ada, b_ada=b_ada, norm_g=norm_g, w_ff1_up=w_ff1_up, w_ff1_down=w_ff1_down, w_in=w_in,
                   b_gate=b_gate, rpb=rpb, ln_v_g=ln_v_g, ln_v_b=ln_v_b, w_s=w_s, b_s=b_s, w_pa=w_pa, w_pb=w_pb, w_o=w_o,
                   w_ff2_up=w_ff2_up, w_ff2_down=w_ff2_down, final_g=final_g)
    mom_m = dict(c_ctx=m_c_ctx, w_ada=m_w_ada, b_ada=m_b_ada, norm_g=m_norm_g, w_ff1_up=m_w_ff1_up, w_ff1_down=m_w_ff1_down,
                 w_in=m_w_in, b_gate=m_b_gate, rpb=m_rpb, ln_v_g=m_ln_v_g, ln_v_b=m_ln_v_b, w_s=m_w_s, b_s=m_b_s, w_pa=m_w_pa,
                 w_pb=m_w_pb, w_o=m_w_o, w_ff2_up=m_w_ff2_up, w_ff2_down=m_w_ff2_down, final_g=m_final_g)
    mom_v = dict(c_ctx=v_c_ctx, w_ada=v_w_ada, b_ada=v_b_ada, norm_g=v_norm_g, w_ff1_up=v_w_ff1_up, w_ff1_down=v_w_ff1_down,
                 w_in=v_w_in, b_gate=v_b_gate, rpb=v_rpb, ln_v_g=v_ln_v_g, ln_v_b=v_ln_v_b, w_s=v_w_s, b_s=v_b_s, w_pa=v_w_pa,
                 w_pb=v_w_pb, w_o=v_w_o, w_ff2_up=v_w_ff2_up, w_ff2_down=v_w_ff2_down, final_g=v_final_g)
    order = list(weights)
    mx, my, mc = _here()
    dev = 4 * mx + 2 * my + mc
    chip = 2 * mx + my
    depth, d, n_ada = w_ada.shape
    dq = d // 4

    n_norm = depth * 3 * dq
    norm_rows = -(-n_norm // d)
    first = _allgather8(jnp.concatenate([c, jnp.pad(norm_g.reshape(-1), (0, norm_rows * d - n_norm)).reshape(norm_rows, d),
                                         jnp.zeros((7 - norm_rows, d), F32)], axis=0), "gather_c_norm_g")
    c_all = first[:, 0, :]
    cond = jnp.concatenate([c_all, c_ctx[None, :], jnp.zeros((7, d), F32)], axis=0)
    b_shard = lax.dynamic_slice(b_ada, (0, chip * n_ada), (depth, n_ada))
    proj = [_ada_fwd(cond, w_ada[l], b_shard[l:l + 1], f"ada{l}") for l in range(depth)]
    silu_c = proj[0][1]
    mods_sh = _allgather8(jnp.concatenate([p[0] for p in proj], axis=0), "gather_mods")
    mods_all = jnp.transpose(mods_sh[0::2].reshape(4, depth, 16, n_ada), (1, 2, 0, 3)).reshape(depth, 16, N_MOD, d)
    mods = jnp.stack([lax.dynamic_index_in_dim(mods_all, dev, axis=1, keepdims=False), mods_all[:, 8]], axis=1)

    shards = [weights[k].astype(MXU) for k in BIG]
    full = _gather_chips(shards, [SHARD_AXIS[k] for k in BIG], "gather_w")
    my_chip = jnp.reshape(chip, (1,)).astype(jnp.int32)
    full = [_place_shard(f, sh, my_chip, SHARD_AXIS[k], "place_" + k) for k, f, sh in zip(BIG, full, shards)]
    wts = dict(zip(BIG, full))
    prm = {k: weights[k] for k in SMALL if k != "norm_g"}
    norm_full = first[0::2, 1:1 + norm_rows, :].reshape(4, norm_rows * d)[:, :n_norm]
    prm["norm_g"] = jnp.transpose(norm_full.reshape(4, depth, 3, dq), (1, 2, 0, 3)).reshape(depth, 3, d)

    loss, grad_x, dmods, gw, gp = _local_step(x[0], ctx[0], loss_target[0], mods, wts, prm)
    loss = lax.psum(loss, ("x", "y", "c"))

    small_shapes = [(depth, 2, N_MOD * d)] + [weights[k].shape if k != "norm_g" else (depth, 3, d) for k in SMALL]
    packed = _allgather8(_pack([dmods.reshape(depth, 2, N_MOD * d)] + [gp[k] for k in SMALL]), "gather_small")
    rows = packed.shape[1]
    total = _sum_pieces([packed[i] for i in range(8)], "sum_small")[0].reshape(-1)
    sums = dict(zip(("dmods",) + SMALL, _unpack(total, small_shapes)))
    dmods_dev = _unpack(packed.reshape(8, rows * PACK_LANES), small_shapes[:1])[0]

    g_ada, cc_parts = [], []
    for l in range(depth):
        dm = jnp.concatenate([dmods_dev[:, l, 0], sums["dmods"][l, 1][None], jnp.zeros((7, N_MOD * d), F32)], axis=0)
        dm_sh = lax.dynamic_slice(dm, (0, chip * n_ada), (16, n_ada))
        g_ada.append(_mm(silu_c, dm_sh, "tn", F32, f"ada{l}_dw"))
        cc_parts.append(_mm(dm_sh, w_ada[l], "nt", F32, f"ada{l}_dc")[8:9])
    cc_all = _allgather8(jnp.pad(jnp.concatenate(cc_parts, axis=0), ((0, 8 - depth), (0, 0))), "gather_cctx")
    g_cctx = _cctx_grad(cc_all[0::2, :depth].reshape(4 * depth, d), c_ctx.reshape(1, d), "cctx_grad")

    axes = [SHARD_AXIS[k] for k in BIG]
    from_sibling = _swap_layers([gw[k] for k in BIG], "swap_layer_gw")
    my_layer = jnp.reshape(mc, (1,)).astype(jnp.int32)
    pair = [_pair_sum(gw[k], r, my_layer, "pair_" + k) for k, r in zip(BIG, from_sibling)]
    recv = _scatter_chips([p[1] for p in pair], axes, "scatter_gw")
    mine = []
    for k, ax, p, r in zip(BIG, axes, pair, recv):
        size = p[0].shape[ax] // 4
        own = lax.dynamic_slice_in_dim(p[0], chip * size, size, axis=ax)
        mine.append(_sum_pieces([own, r[0], r[1], r[2]], "sum_" + k)[0])
    other = _sibling_swap(mine, "swap_gw")

    res = {k: _adamw_layers(weights[k], a, b, mom_m[k], mom_v[k], my_layer, "adamw_" + k) for k, a, b in zip(BIG, mine, other)}
    pieces = {"w_ada": [jnp.stack(g_ada)]}
    pieces["b_ada"] = [sums["dmods"][:, 0], sums["dmods"][:, 1]]
    pieces["c_ctx"] = [g_cctx[0]]
    for k in SMALL:
        pieces[k] = [sums[k]]
    pieces["norm_g"] = [lax.dynamic_slice_in_dim(sums["norm_g"], chip * dq, dq, axis=2)]
    res.update({k: _adamw(weights[k], pieces[k], mom_m[k], mom_v[k], "adamw_" + k) for k in pieces})
    return (loss, grad_x[None], *[res[k][0] for k in order], *[res[k][1] for k in order],
            *[res[k][2] for k in order], *[res[k][3] for k in order])
```
